```python
import math
import jax, jax.numpy as jnp
from jax import lax
import numpy as np

D_MODEL = 1024
BATCH = 8
SEQ = 8192
DEPTH = 2

CHUNK = 64
N_META = 16
N_A_LAYERS = DEPTH // 2
N_B_LAYERS = DEPTH - N_A_LAYERS
GDN_HEADS = 8
GDN_HEAD_DIM = 128
GDN_WIDTH = GDN_HEADS * GDN_HEAD_DIM
CONV_WIDTH = 4
SB_HEADS = 16
SB_HEAD_DIM = 64
SB_WIDTH = SB_HEADS * SB_HEAD_DIM
SB_BLOCK = 128
D_FF = -(-(8 * D_MODEL) // (3 * 256)) * 256
GDN_IN_COLS = 4 * GDN_WIDTH + 2 * GDN_HEADS
EPS = 1e-6

kernel_name = "yoco_gdn_stickbreaking_hybrid"


def rms_norm(x, g):
    xf = x.astype(jnp.float32)
    y = xf * lax.rsqrt(jnp.mean(xf * xf, axis=-1, keepdims=True) + EPS)
    return (y * g.astype(jnp.float32)).astype(x.dtype)


def l2_normalize(x):
    return x * lax.rsqrt(jnp.sum(x * x, axis=-1, keepdims=True) + EPS)


def causal_depthwise_conv(x, w):
    L = x.shape[1]
    xp = jnp.pad(x, ((0, 0), (CONV_WIDTH - 1, 0), (0, 0)))
    return sum(xp[:, i:i + L] * w[i] for i in range(CONV_WIDTH))


def gated_delta_chunked(q, k, v, g, beta):
    b, nh, L, dk = q.shape
    dv = v.shape[-1]
    n = L // CHUNK
    q = q.reshape(b, nh, n, CHUNK, dk) * (dk ** -0.5)
    k = k.reshape(b, nh, n, CHUNK, dk)
    v = v.reshape(b, nh, n, CHUNK, dv)
    g = g.reshape(b, nh, n, CHUNK)
    beta = beta.reshape(b, nh, n, CHUNK)
    G = jnp.cumsum(g, axis=-1)
    incl = jnp.tril(jnp.ones((CHUNK, CHUNK), dtype=bool))
    strict = jnp.tril(jnp.ones((CHUNK, CHUNK), dtype=bool), -1)
    diff = G[..., :, None] - G[..., None, :]
    decay = jnp.where(incl, jnp.exp(jnp.where(incl, diff, 0.0)), 0.0)
    kk = jnp.einsum('bhnid,bhnjd->bhnij', k, k)
    lower = jnp.eye(CHUNK, dtype=q.dtype) + jnp.where(strict, beta[..., :, None] * decay * kk, 0.0)
    rhs = jnp.concatenate([beta[..., None] * v, (beta * jnp.exp(G))[..., None] * k], axis=-1)
    sol = lax.linalg.triangular_solve(lower, rhs, left_side=True, lower=True, unit_diagonal=True)
    u_base = sol[..., :dv]
    w_corr = sol[..., dv:]
    attn = jnp.einsum('bhnid,bhnjd->bhnij', q, k) * decay
    q_dec = q * jnp.exp(G)[..., None]
    g_last = G[..., -1]
    k_dec = k * jnp.exp(g_last[..., None] - G)[..., None]

    def step(S, inp):
        u_b, w_c, a_c, qd, kd, gl = inp
        u = u_b - jnp.einsum('bhcd,bhde->bhce', w_c, S)
        o = jnp.einsum('bhcd,bhde->bhce', qd, S) + jnp.einsum('bhij,bhje->bhie', a_c, u)
        S = S * jnp.exp(gl)[..., None, None] + jnp.einsum('bhcd,bhce->bhde', kd, u)
        return S, o

    xs = tuple(jnp.moveaxis(t, 2, 0) for t in (u_base, w_corr, attn, q_dec, k_dec, g_last))
    S0 = jnp.zeros((b, nh, dk, dv), dtype=q.dtype)
    _, o = lax.scan(step, S0, xs)
    return jnp.moveaxis(o, 0, 2).reshape(b, nh, L, dv)


def gdn_mixer(h, w_in, conv_w, a_log, dt_bias, onorm_g, w_out):
    b, L, _ = h.shape
    proj = h @ w_in
    qkv = jax.nn.silu(causal_depthwise_conv(proj[..., :3 * GDN_WIDTH], conv_w)).astype(jnp.float32)
    gate = proj[..., 3 * GDN_WIDTH:4 * GDN_WIDTH].astype(jnp.float32)
    a_in = proj[..., 4 * GDN_WIDTH:4 * GDN_WIDTH + GDN_HEADS].astype(jnp.float32)
    b_in = proj[..., 4 * GDN_WIDTH + GDN_HEADS:].astype(jnp.float32)
    heads = lambda t: t.reshape(b, L, GDN_HEADS, GDN_HEAD_DIM).transpose(0, 2, 1, 3)
    q = l2_normalize(heads(qkv[..., :GDN_WIDTH]))
    k = l2_normalize(heads(qkv[..., GDN_WIDTH:2 * GDN_WIDTH]))
    v = heads(qkv[..., 2 * GDN_WIDTH:])
    g = (-jnp.exp(a_log.astype(jnp.float32)) * jax.nn.softplus(a_in + dt_bias.astype(jnp.float32))).transpose(0, 2, 1)
    beta = jax.nn.sigmoid(b_in).transpose(0, 2, 1)
    front = (-N_META) % CHUNK
    back = (-(front + L)) % CHUNK
    p4 = ((0, 0), (0, 0), (front, back), (0, 0))
    p3 = ((0, 0), (0, 0), (front, back))
    o = gated_delta_chunked(jnp.pad(q, p4), jnp.pad(k, p4), jnp.pad(v, p4), jnp.pad(g, p3), jnp.pad(beta, p3))
    o = o[:, :, front:front + L].transpose(0, 2, 1, 3)
    o = o * lax.rsqrt(jnp.mean(o * o, axis=-1, keepdims=True) + EPS) * onorm_g.astype(jnp.float32)
    o = o * jax.nn.silu(gate.reshape(b, L, GDN_HEADS, GDN_HEAD_DIM))
    return o.reshape(b, L, GDN_WIDTH).astype(h.dtype) @ w_out


def shared_kv(h, norm_g, w_kv):
    b, L, _ = h.shape
    kv = (rms_norm(h, norm_g) @ w_kv).astype(jnp.float32).reshape(b, L, 2, SB_HEADS, SB_HEAD_DIM)
    return kv[:, :, 0].transpose(0, 2, 1, 3), kv[:, :, 1].transpose(0, 2, 1, 3)


def sb_mixer(h, k_sh, v_sh, w_q, w_o):
    b, L, _ = h.shape
    q = (h @ w_q).astype(jnp.float32).reshape(b, L, SB_HEADS, SB_HEAD_DIM).transpose(0, 2, 1, 3)
    pad = (-L) % SB_BLOCK
    Lp = L + pad
    p4 = ((0, 0), (0, 0), (0, pad), (0, 0))
    q, k, v = jnp.pad(q, p4), jnp.pad(k_sh, p4), jnp.pad(v_sh, p4)
    nb = Lp // SB_BLOCK
    q_blocks = q.reshape(b, SB_HEADS, nb, SB_BLOCK, SB_HEAD_DIM).transpose(2, 0, 1, 3, 4)
    key_pos = jnp.arange(Lp)
    scale = SB_HEAD_DIM ** -0.5

    def one_block(args):
        q_blk, blk = args
        q_pos = blk * SB_BLOCK + jnp.arange(SB_BLOCK)
        visible = key_pos[None, :] < q_pos[:, None]
        z = jnp.einsum('bhqd,bhkd->bhqk', q_blk, k) * scale
        log_beta = jax.nn.log_sigmoid(z)
        log_keep = jnp.where(visible, log_beta - z, 0.0)
        log_w = log_beta + lax.cumsum(log_keep, axis=3, reverse=True) - log_keep
        w = jnp.where(visible, jnp.exp(log_w), 0.0)
        return jnp.einsum('bhqk,bhkd->bhqd', w, v)

    o = lax.map(one_block, (q_blocks, jnp.arange(nb)))
    o = o.transpose(1, 2, 0, 3, 4).reshape(b, SB_HEADS, Lp, SB_HEAD_DIM)[:, :, :L]
    o = o.transpose(0, 2, 1, 3).reshape(b, L, SB_WIDTH).astype(h.dtype)
    return o @ w_o


def swiglu(h, w_gate_up, w_down):
    gu = h @ w_gate_up
    return (jax.nn.silu(gu[..., :D_FF]) * gu[..., D_FF:]) @ w_down


def _fwd_setup_inputs(seed: int = 0) -> dict:
    key = jax.random.key(seed)
    ks = jax.random.split(key, 20)
    f32 = jnp.float32
    dense = lambda k, shape, fan_in: jax.random.normal(k, shape, f32) * (fan_in ** -0.5)
    gain = lambda k, shape: 1.0 + 0.02 * jax.random.normal(k, shape, f32)
    dt = jnp.exp(jax.random.uniform(ks[5], (N_A_LAYERS, GDN_HEADS), f32, math.log(1e-3), math.log(1e-1)))
    return {
        "x": jax.random.normal(ks[0], (BATCH, SEQ, D_MODEL), f32),
        "meta_tokens": jax.random.normal(ks[1], (N_META, D_MODEL), f32),
        "gdn_norm_g": gain(ks[2], (N_A_LAYERS, D_MODEL)),
        "gdn_w_in": dense(ks[3], (N_A_LAYERS, D_MODEL, GDN_IN_COLS), D_MODEL),
        "gdn_conv_w": dense(ks[4], (N_A_LAYERS, CONV_WIDTH, 3 * GDN_WIDTH), CONV_WIDTH),
        "gdn_a_log": jnp.log(jax.random.uniform(ks[6], (N_A_LAYERS, GDN_HEADS), f32, 1.0, 16.0)),
        "gdn_dt_bias": dt + jnp.log(-jnp.expm1(-dt)),
        "gdn_onorm_g": gain(ks[7], (N_A_LAYERS, GDN_HEAD_DIM)),
        "gdn_w_out": dense(ks[8], (N_A_LAYERS, GDN_WIDTH, D_MODEL), GDN_WIDTH),
        "kv_norm_g": gain(ks[9], (D_MODEL,)),
        "w_kv": dense(ks[10], (D_MODEL, 2 * SB_WIDTH), D_MODEL),
        "sb_norm_g": gain(ks[11], (N_B_LAYERS, D_MODEL)),
        "sb_w_q": dense(ks[12], (N_B_LAYERS, D_MODEL, SB_WIDTH), D_MODEL),
        "sb_w_o": dense(ks[13], (N_B_LAYERS, SB_WIDTH, D_MODEL), SB_WIDTH),
        "ffn_norm_g": gain(ks[14], (DEPTH, D_MODEL)),
        "ffn_w_gate_up": dense(ks[15], (DEPTH, D_MODEL, 2 * D_FF), D_MODEL),
        "ffn_w_down": dense(ks[16], (DEPTH, D_FF, D_MODEL), D_FF),
        "final_norm_g": gain(ks[17], (D_MODEL,)),
    }


def _fwd_reference(x, meta_tokens, gdn_norm_g, gdn_w_in, gdn_conv_w, gdn_a_log, gdn_dt_bias,
              gdn_onorm_g, gdn_w_out, kv_norm_g, w_kv, sb_norm_g, sb_w_q, sb_w_o,
              ffn_norm_g, ffn_w_gate_up, ffn_w_down, final_norm_g):
    b = x.shape[0]
    meta = jnp.broadcast_to(meta_tokens.astype(x.dtype)[None], (b, N_META, D_MODEL))
    h = jnp.concatenate([meta, x], axis=1)
    k_sh = None
    v_sh = None
    for layer in range(DEPTH):
        if layer < N_A_LAYERS:
            h = h + gdn_mixer(rms_norm(h, gdn_norm_g[layer]), gdn_w_in[layer], gdn_conv_w[layer],
                              gdn_a_log[layer], gdn_dt_bias[layer], gdn_onorm_g[layer], gdn_w_out[layer])
        else:
            j = layer - N_A_LAYERS
            h = h + sb_mixer(rms_norm(h, sb_norm_g[j]), k_sh, v_sh, sb_w_q[j], sb_w_o[j])
        h = h + swiglu(rms_norm(h, ffn_norm_g[layer]), ffn_w_gate_up[layer], ffn_w_down[layer])
        if layer == N_A_LAYERS - 1:
            k_sh, v_sh = shared_kv(h, kv_norm_g, w_kv)
    return rms_norm(h, final_norm_g)[:, N_META:]


import jax as _jax
import jax.numpy as _jnp

TWIN_FORMAT = 'train_step'
FWD_PARAMS = ['x', 'meta_tokens', 'gdn_norm_g', 'gdn_w_in', 'gdn_conv_w', 'gdn_a_log', 'gdn_dt_bias', 'gdn_onorm_g', 'gdn_w_out', 'kv_norm_g', 'w_kv', 'sb_norm_g', 'sb_w_q', 'sb_w_o', 'ffn_norm_g', 'ffn_w_gate_up', 'ffn_w_down', 'final_norm_g']
TWIN_WEIGHTS = ['meta_tokens', 'gdn_norm_g', 'gdn_w_in', 'gdn_conv_w', 'gdn_a_log', 'gdn_dt_bias', 'gdn_onorm_g', 'gdn_w_out', 'kv_norm_g', 'w_kv', 'sb_norm_g', 'sb_w_q', 'sb_w_o', 'ffn_norm_g', 'ffn_w_gate_up', 'ffn_w_down', 'final_norm_g']
TWIN_DIFF_INPUT = 'x'
TWIN_INPUTS = ['x', 'meta_tokens', 'gdn_norm_g', 'gdn_w_in', 'gdn_conv_w', 'gdn_a_log', 'gdn_dt_bias', 'gdn_onorm_g', 'gdn_w_out', 'kv_norm_g', 'w_kv', 'sb_norm_g', 'sb_w_q', 'sb_w_o', 'ffn_norm_g', 'ffn_w_gate_up', 'ffn_w_down', 'final_norm_g', 'loss_target', 'm_meta_tokens', 'm_gdn_norm_g', 'm_gdn_w_in', 'm_gdn_conv_w', 'm_gdn_a_log', 'm_gdn_dt_bias', 'm_gdn_onorm_g', 'm_gdn_w_out', 'm_kv_norm_g', 'm_w_kv', 'm_sb_norm_g', 'm_sb_w_q', 'm_sb_w_o', 'm_ffn_norm_g', 'm_ffn_w_gate_up', 'm_ffn_w_down', 'm_final_norm_g', 'v_meta_tokens', 'v_gdn_norm_g', 'v_gdn_w_in', 'v_gdn_conv_w', 'v_gdn_a_log', 'v_gdn_dt_bias', 'v_gdn_onorm_g', 'v_gdn_w_out', 'v_kv_norm_g', 'v_w_kv', 'v_sb_norm_g', 'v_sb_w_q', 'v_sb_w_o', 'v_ffn_norm_g', 'v_ffn_w_gate_up', 'v_ffn_w_down', 'v_final_norm_g']
TWIN_OUTPUTS = ['loss', 'grad_x', 'grad_meta_tokens', 'grad_gdn_norm_g', 'grad_gdn_w_in', 'grad_gdn_conv_w', 'grad_gdn_a_log', 'grad_gdn_dt_bias', 'grad_gdn_onorm_g', 'grad_gdn_w_out', 'grad_kv_norm_g', 'grad_w_kv', 'grad_sb_norm_g', 'grad_sb_w_q', 'grad_sb_w_o', 'grad_ffn_norm_g', 'grad_ffn_w_gate_up', 'grad_ffn_w_down', 'grad_final_norm_g', 'delta_meta_tokens', 'delta_gdn_norm_g', 'delta_gdn_w_in', 'delta_gdn_conv_w', 'delta_gdn_a_log', 'delta_gdn_dt_bias', 'delta_gdn_onorm_g', 'delta_gdn_w_out', 'delta_kv_norm_g', 'delta_w_kv', 'delta_sb_norm_g', 'delta_sb_w_q', 'delta_sb_w_o', 'delta_ffn_norm_g', 'delta_ffn_w_gate_up', 'delta_ffn_w_down', 'delta_final_norm_g', 'new_m_meta_tokens', 'new_m_gdn_norm_g', 'new_m_gdn_w_in', 'new_m_gdn_conv_w', 'new_m_gdn_a_log', 'new_m_gdn_dt_bias', 'new_m_gdn_onorm_g', 'new_m_gdn_w_out', 'new_m_kv_norm_g', 'new_m_w_kv', 'new_m_sb_norm_g', 'new_m_sb_w_q', 'new_m_sb_w_o', 'new_m_ffn_norm_g', 'new_m_ffn_w_gate_up', 'new_m_ffn_w_down', 'new_m_final_norm_g', 'new_v_meta_tokens', 'new_v_gdn_norm_g', 'new_v_gdn_w_in', 'new_v_gdn_conv_w', 'new_v_gdn_a_log', 'new_v_gdn_dt_bias', 'new_v_gdn_onorm_g', 'new_v_gdn_w_out', 'new_v_kv_norm_g', 'new_v_w_kv', 'new_v_sb_norm_g', 'new_v_sb_w_q', 'new_v_sb_w_o', 'new_v_ffn_norm_g', 'new_v_ffn_w_gate_up', 'new_v_ffn_w_down', 'new_v_final_norm_g']
TWIN_LEAF_KINDS = {'loss': 'loss', 'grad_x': 'grad_x', 'grad_meta_tokens': 'grad_w', 'grad_gdn_norm_g': 'grad_w', 'grad_gdn_w_in': 'grad_w', 'grad_gdn_conv_w': 'grad_w', 'grad_gdn_a_log': 'grad_w', 'grad_gdn_dt_bias': 'grad_w', 'grad_gdn_onorm_g': 'grad_w', 'grad_gdn_w_out': 'grad_w', 'grad_kv_norm_g': 'grad_w', 'grad_w_kv': 'grad_w', 'grad_sb_norm_g': 'grad_w', 'grad_sb_w_q': 'grad_w', 'grad_sb_w_o': 'grad_w', 'grad_ffn_norm_g': 'grad_w', 'grad_ffn_w_gate_up': 'grad_w', 'grad_ffn_w_down': 'grad_w', 'grad_final_norm_g': 'grad_w', 'delta_meta_tokens': 'delta_w', 'delta_gdn_norm_g': 'delta_w', 'delta_gdn_w_in': 'delta_w', 'delta_gdn_conv_w': 'delta_w', 'delta_gdn_a_log': 'delta_w', 'delta_gdn_dt_bias': 'delta_w', 'delta_gdn_onorm_g': 'delta_w', 'delta_gdn_w_out': 'delta_w', 'delta_kv_norm_g': 'delta_w', 'delta_w_kv': 'delta_w', 'delta_sb_norm_g': 'delta_w', 'delta_sb_w_q': 'delta_w', 'delta_sb_w_o': 'delta_w', 'delta_ffn_norm_g': 'delta_w', 'delta_ffn_w_gate_up': 'delta_w', 'delta_ffn_w_down': 'delta_w', 'delta_final_norm_g': 'delta_w', 'new_m_meta_tokens': 'new_m', 'new_m_gdn_norm_g': 'new_m', 'new_m_gdn_w_in': 'new_m', 'new_m_gdn_conv_w': 'new_m', 'new_m_gdn_a_log': 'new_m', 'new_m_gdn_dt_bias': 'new_m', 'new_m_gdn_onorm_g': 'new_m', 'new_m_gdn_w_out': 'new_m', 'new_m_kv_norm_g': 'new_m', 'new_m_w_kv': 'new_m', 'new_m_sb_norm_g': 'new_m', 'new_m_sb_w_q': 'new_m', 'new_m_sb_w_o': 'new_m', 'new_m_ffn_norm_g': 'new_m', 'new_m_ffn_w_gate_up': 'new_m', 'new_m_ffn_w_down': 'new_m', 'new_m_final_norm_g': 'new_m', 'new_v_meta_tokens': 'new_v', 'new_v_gdn_norm_g': 'new_v', 'new_v_gdn_w_in': 'new_v', 'new_v_gdn_conv_w': 'new_v', 'new_v_gdn_a_log': 'new_v', 'new_v_gdn_dt_bias': 'new_v', 'new_v_gdn_onorm_g': 'new_v', 'new_v_gdn_w_out': 'new_v', 'new_v_kv_norm_g': 'new_v', 'new_v_w_kv': 'new_v', 'new_v_sb_norm_g': 'new_v', 'new_v_sb_w_q': 'new_v', 'new_v_sb_w_o': 'new_v', 'new_v_ffn_norm_g': 'new_v', 'new_v_ffn_w_gate_up': 'new_v', 'new_v_ffn_w_down': 'new_v', 'new_v_final_norm_g': 'new_v'}


def _forward(args):
    return _fwd_reference(*[args[k] for k in FWD_PARAMS])


def _output_shape():
    def fwd():
        inp = _fwd_setup_inputs(0)
        return _fwd_reference(*[inp[k] for k in FWD_PARAMS])
    out = _jax.eval_shape(fwd)
    return out.shape, out.dtype

N_MICROBATCH = 1
ADAM_LR = 0.001
ADAM_B1 = 0.9
ADAM_B2 = 0.999
ADAM_EPS = 1e-08
ADAM_WD = 0.01
ADAM_STEP = 10
PER_EXAMPLE_BATCH_AXIS = {'x': 0, 'loss_target': 0}
SHARED_INPUTS = []
_WEIGHT_DTYPES = {'meta_tokens': _jnp.float32, 'gdn_norm_g': _jnp.float32, 'gdn_w_in': _jnp.float32, 'gdn_conv_w': _jnp.float32, 'gdn_a_log': _jnp.float32, 'gdn_dt_bias': _jnp.float32, 'gdn_onorm_g': _jnp.float32, 'gdn_w_out': _jnp.float32, 'kv_norm_g': _jnp.float32, 'w_kv': _jnp.float32, 'sb_norm_g': _jnp.float32, 'sb_w_q': _jnp.float32, 'sb_w_o': _jnp.float32, 'ffn_norm_g': _jnp.float32, 'ffn_w_gate_up': _jnp.float32, 'ffn_w_down': _jnp.float32, 'final_norm_g': _jnp.float32}
MOMENT_SCALE = {'meta_tokens': 6.706793e-03, 'gdn_norm_g': 2.675909e-01, 'gdn_w_in': 1.327901e-01, 'gdn_conv_w': 1.214737e-01, 'gdn_a_log': 1.078308e+00, 'gdn_dt_bias': 1.035421e+00, 'gdn_onorm_g': 4.346402e-01, 'gdn_w_out': 1.548254e-01, 'kv_norm_g': 1.264207e-01, 'w_kv': 9.407426e-02, 'sb_norm_g': 5.548810e-02, 'sb_w_q': 5.533117e-02, 'sb_w_o': 1.215598e-01, 'ffn_norm_g': 1.708369e-01, 'ffn_w_gate_up': 6.952516e-02, 'ffn_w_down': 1.135204e-01, 'final_norm_g': 6.406980e+01}


def _to_microbatches(a, axis):
    t = _jnp.moveaxis(a, axis, 0)
    t = t.reshape((N_MICROBATCH, t.shape[0] // N_MICROBATCH) + t.shape[1:])
    return _jnp.moveaxis(t, 1, axis + 1)


def setup_inputs(seed: int = 0) -> dict:
    inp = _fwd_setup_inputs(seed)
    key = _jax.random.fold_in(_jax.random.key(seed), 7919)
    shape, _ = _output_shape()
    out = dict(inp)
    out["loss_target"] = _jax.random.normal(_jax.random.fold_in(key, 0), shape, _jnp.float32)
    for i, name in enumerate(TWIN_WEIGHTS):
        w = inp[name].astype(_jnp.float32)
        if MOMENT_SCALE is None:
            s = _jnp.sqrt(_jnp.mean(_jnp.square(w)) + 1e-30)
        else:
            s = MOMENT_SCALE[name]
        km, kv = _jax.random.split(_jax.random.fold_in(key, i + 1))
        out[name] = w
        out["m_" + name] = s * _jax.random.normal(km, w.shape, _jnp.float32)
        out["v_" + name] = (s * s) * _jax.random.uniform(kv, w.shape, _jnp.float32, 0.5, 1.5)
    if N_MICROBATCH > 1:
        for name, axis in PER_EXAMPLE_BATCH_AXIS.items():
            out[name] = _to_microbatches(out[name], axis)
    return {'x': out['x'], 'meta_tokens': out['meta_tokens'], 'gdn_norm_g': out['gdn_norm_g'], 'gdn_w_in': out['gdn_w_in'], 'gdn_conv_w': out['gdn_conv_w'], 'gdn_a_log': out['gdn_a_log'], 'gdn_dt_bias': out['gdn_dt_bias'], 'gdn_onorm_g': out['gdn_onorm_g'], 'gdn_w_out': out['gdn_w_out'], 'kv_norm_g': out['kv_norm_g'], 'w_kv': out['w_kv'], 'sb_norm_g': out['sb_norm_g'], 'sb_w_q': out['sb_w_q'], 'sb_w_o': out['sb_w_o'], 'ffn_norm_g': out['ffn_norm_g'], 'ffn_w_gate_up': out['ffn_w_gate_up'], 'ffn_w_down': out['ffn_w_down'], 'final_norm_g': out['final_norm_g'], 'loss_target': out['loss_target'], 'm_meta_tokens': out['m_meta_tokens'], 'm_gdn_norm_g': out['m_gdn_norm_g'], 'm_gdn_w_in': out['m_gdn_w_in'], 'm_gdn_conv_w': out['m_gdn_conv_w'], 'm_gdn_a_log': out['m_gdn_a_log'], 'm_gdn_dt_bias': out['m_gdn_dt_bias'], 'm_gdn_onorm_g': out['m_gdn_onorm_g'], 'm_gdn_w_out': out['m_gdn_w_out'], 'm_kv_norm_g': out['m_kv_norm_g'], 'm_w_kv': out['m_w_kv'], 'm_sb_norm_g': out['m_sb_norm_g'], 'm_sb_w_q': out['m_sb_w_q'], 'm_sb_w_o': out['m_sb_w_o'], 'm_ffn_norm_g': out['m_ffn_norm_g'], 'm_ffn_w_gate_up': out['m_ffn_w_gate_up'], 'm_ffn_w_down': out['m_ffn_w_down'], 'm_final_norm_g': out['m_final_norm_g'], 'v_meta_tokens': out['v_meta_tokens'], 'v_gdn_norm_g': out['v_gdn_norm_g'], 'v_gdn_w_in': out['v_gdn_w_in'], 'v_gdn_conv_w': out['v_gdn_conv_w'], 'v_gdn_a_log': out['v_gdn_a_log'], 'v_gdn_dt_bias': out['v_gdn_dt_bias'], 'v_gdn_onorm_g': out['v_gdn_onorm_g'], 'v_gdn_w_out': out['v_gdn_w_out'], 'v_kv_norm_g': out['v_kv_norm_g'], 'v_w_kv': out['v_w_kv'], 'v_sb_norm_g': out['v_sb_norm_g'], 'v_sb_w_q': out['v_sb_w_q'], 'v_sb_w_o': out['v_sb_w_o'], 'v_ffn_norm_g': out['v_ffn_norm_g'], 'v_ffn_w_gate_up': out['v_ffn_w_gate_up'], 'v_ffn_w_down': out['v_ffn_w_down'], 'v_final_norm_g': out['v_final_norm_g']}


def _loss(weights, diff, rest, loss_target):
    with _jax.named_scope("forward"):
        args = {**rest, TWIN_DIFF_INPUT: diff, **{k: w.astype(_WEIGHT_DTYPES[k]) for k, w in weights.items()}}
        y = _forward(args)
    with _jax.named_scope("loss_head"):
        err = _jnp.square(y.astype(_jnp.float32) - loss_target)
        return 0.5 * _jnp.sum(_jnp.mean(err, axis=-1)) if err.ndim else 0.5 * err


def _adamw(w, g, m, v):
    m = ADAM_B1 * m + (1.0 - ADAM_B1) * g
    v = ADAM_B2 * v + (1.0 - ADAM_B2) * _jnp.square(g)
    m_hat = m / (1.0 - ADAM_B1 ** ADAM_STEP)
    v_hat = v / (1.0 - ADAM_B2 ** ADAM_STEP)
    delta = -ADAM_LR * (m_hat / (_jnp.sqrt(v_hat) + ADAM_EPS) + ADAM_WD * w)
    return delta, m, v


def reference(x, meta_tokens, gdn_norm_g, gdn_w_in, gdn_conv_w, gdn_a_log, gdn_dt_bias, gdn_onorm_g, gdn_w_out, kv_norm_g, w_kv, sb_norm_g, sb_w_q, sb_w_o, ffn_norm_g, ffn_w_gate_up, ffn_w_down, final_norm_g, loss_target, m_meta_tokens, m_gdn_norm_g, m_gdn_w_in, m_gdn_conv_w, m_gdn_a_log, m_gdn_dt_bias, m_gdn_onorm_g, m_gdn_w_out, m_kv_norm_g, m_w_kv, m_sb_norm_g, m_sb_w_q, m_sb_w_o, m_ffn_norm_g, m_ffn_w_gate_up, m_ffn_w_down, m_final_norm_g, v_meta_tokens, v_gdn_norm_g, v_gdn_w_in, v_gdn_conv_w, v_gdn_a_log, v_gdn_dt_bias, v_gdn_onorm_g, v_gdn_w_out, v_kv_norm_g, v_w_kv, v_sb_norm_g, v_sb_w_q, v_sb_w_o, v_ffn_norm_g, v_ffn_w_gate_up, v_ffn_w_down, v_final_norm_g):
    given = dict(x=x, meta_tokens=meta_tokens, gdn_norm_g=gdn_norm_g, gdn_w_in=gdn_w_in, gdn_conv_w=gdn_conv_w, gdn_a_log=gdn_a_log, gdn_dt_bias=gdn_dt_bias, gdn_onorm_g=gdn_onorm_g, gdn_w_out=gdn_w_out, kv_norm_g=kv_norm_g, w_kv=w_kv, sb_norm_g=sb_norm_g, sb_w_q=sb_w_q, sb_w_o=sb_w_o, ffn_norm_g=ffn_norm_g, ffn_w_gate_up=ffn_w_gate_up, ffn_w_down=ffn_w_down, final_norm_g=final_norm_g, loss_target=loss_target, m_meta_tokens=m_meta_tokens, m_gdn_norm_g=m_gdn_norm_g, m_gdn_w_in=m_gdn_w_in, m_gdn_conv_w=m_gdn_conv_w, m_gdn_a_log=m_gdn_a_log, m_gdn_dt_bias=m_gdn_dt_bias, m_gdn_onorm_g=m_gdn_onorm_g, m_gdn_w_out=m_gdn_w_out, m_kv_norm_g=m_kv_norm_g, m_w_kv=m_w_kv, m_sb_norm_g=m_sb_norm_g, m_sb_w_q=m_sb_w_q, m_sb_w_o=m_sb_w_o, m_ffn_norm_g=m_ffn_norm_g, m_ffn_w_gate_up=m_ffn_w_gate_up, m_ffn_w_down=m_ffn_w_down, m_final_norm_g=m_final_norm_g, v_meta_tokens=v_meta_tokens, v_gdn_norm_g=v_gdn_norm_g, v_gdn_w_in=v_gdn_w_in, v_gdn_conv_w=v_gdn_conv_w, v_gdn_a_log=v_gdn_a_log, v_gdn_dt_bias=v_gdn_dt_bias, v_gdn_onorm_g=v_gdn_onorm_g, v_gdn_w_out=v_gdn_w_out, v_kv_norm_g=v_kv_norm_g, v_w_kv=v_w_kv, v_sb_norm_g=v_sb_norm_g, v_sb_w_q=v_sb_w_q, v_sb_w_o=v_sb_w_o, v_ffn_norm_g=v_ffn_norm_g, v_ffn_w_gate_up=v_ffn_w_gate_up, v_ffn_w_down=v_ffn_w_down, v_final_norm_g=v_final_norm_g)
    weights = {n: given[n] for n in TWIN_WEIGHTS}
    shared = {n: given[n] for n in SHARED_INPUTS}
    per_example = {n: given[n] for n in ['x']}
    grad_fn = _jax.value_and_grad(_loss, argnums=(0, 1))

    def one_microbatch(ex, loss_target):
        ex = dict(ex)
        diff = ex.pop(TWIN_DIFF_INPUT)
        return grad_fn(weights, diff, {**shared, **ex}, loss_target)

    if N_MICROBATCH == 1:
        loss, (grad_w, grad_x) = one_microbatch(per_example, given["loss_target"])
    else:
        def body(carry, xs):
            loss_sum, grad_sum = carry
            l_k, (gw_k, gx_k) = one_microbatch(xs[0], xs[1])
            with _jax.named_scope("update"):
                return (loss_sum + l_k, _jax.tree.map(_jnp.add, grad_sum, gw_k)), gx_k

        init = (_jnp.zeros((), _jnp.float32), _jax.tree.map(_jnp.zeros_like, weights))
        (loss, grad_w), grad_x = _jax.lax.scan(body, init, (per_example, given["loss_target"]))
    with _jax.named_scope("update"):
        delta_w, new_m, new_v = {}, {}, {}
        for n in TWIN_WEIGHTS:
            delta_w[n], new_m[n], new_v[n] = _adamw(weights[n], grad_w[n], given["m_" + n], given["v_" + n])
    return (loss, grad_x, *[grad_w[n] for n in TWIN_WEIGHTS], *[delta_w[n] for n in TWIN_WEIGHTS],
            *[new_m[n] for n in TWIN_WEIGHTS], *[new_v[n] for n in TWIN_WEIGHTS])
```

```python
import functools

import jax
import jax.numpy as jnp
from jax import lax
from jax.experimental import pallas as pl
from jax.experimental.pallas import tpu as pltpu

f32 = jnp.float32
bf16 = jnp.bfloat16
i32 = jnp.int32

EPS = 1e-6
N_META = 16
CHUNK = 64
FRONT = (-N_META) % CHUNK
GDN_HEADS = 8
GDN_DIM = 128
GDN_WIDTH = GDN_HEADS * GDN_DIM
CONV_WIDTH = 4
SB_DIM = 64
SB_BLOCK = 128
SB_UNDERFLOW = 104.0
LANES = 128
PACK_COLS = 1024
N_CHIPS = 4
N_DEV = 8
ADAM_LR, ADAM_B1, ADAM_B2, ADAM_EPS, ADAM_WD, ADAM_STEP = 0.001, 0.9, 0.999, 1e-08, 0.01, 10
VMEM_LIMIT = 56 * 1024 * 1024
MESH = pl.DeviceIdType.MESH


def _pick(n, prefs):
    for p in prefs:
        if n % p == 0:
            return p
    return n


def _params(sem):
    return pltpu.CompilerParams(dimension_semantics=sem, vmem_limit_bytes=VMEM_LIMIT)


_DIMS = {"nn": ((1,), (0,)), "nt": ((1,), (1,)), "tn": ((0,), (0,))}


def _bdot(a, b, mode):
    return lax.dot_general(a.astype(bf16), b.astype(bf16), (_DIMS[mode], ((), ())), preferred_element_type=f32)


def _matmul(a, b, mode, name, res=None, out_dtype=f32):
    if mode == "nn":
        (m, k), n = a.shape, b.shape[1]
    elif mode == "nt":
        (m, k), n = a.shape, b.shape[0]
    else:
        (k, m), n = a.shape, b.shape[1]
    tm = _pick(m, (640, 1408, 1024, 512, 384, 256, 128))
    tn = _pick(n, (1408, 2176, 1024, 512, 384, 256, 128))
    tk = _pick(k, (1408, 2176, 1024, 640, 512, 384, 256, 128))
    nk = k // tk
    a_spec = pl.BlockSpec((tk, tm), lambda i, j, kk: (kk, i)) if mode == "tn" else pl.BlockSpec((tm, tk), lambda i, j, kk: (i, kk))
    b_spec = pl.BlockSpec((tn, tk), lambda i, j, kk: (j, kk)) if mode == "nt" else pl.BlockSpec((tk, tn), lambda i, j, kk: (kk, j))
    o_spec = pl.BlockSpec((tm, tn), lambda i, j, kk: (i, j))
    has_res = res is not None

    def body(*refs):
        if has_res:
            a_ref, b_ref, r_ref, o_ref, acc = refs
        else:
            a_ref, b_ref, o_ref, acc = refs
        kk = pl.program_id(2)

        @pl.when(kk == 0)
        def _():
            acc[...] = jnp.zeros_like(acc)

        acc[...] += _bdot(a_ref[...], b_ref[...], mode)

        @pl.when(kk == nk - 1)
        def _():
            y = acc[...]
            if has_res:
                y = y + r_ref[...]
            o_ref[...] = y.astype(o_ref.dtype)

    ins = [a, b] + ([res] if has_res else [])
    specs = [a_spec, b_spec] + ([o_spec] if has_res else [])
    return pl.pallas_call(
        body, name=name, out_shape=jax.ShapeDtypeStruct((m, n), out_dtype), grid=(m // tm, n // tn, nk),
        in_specs=specs, out_specs=o_spec, scratch_shapes=[pltpu.VMEM((tm, tn), f32)],
        compiler_params=_params(("parallel", "parallel", "arbitrary")),
    )(*ins)


def _rowwise(fn, name, rows, tm, ins, outs, reds=()):
    n_in, n_out, n_red = len(ins), len(outs), len(reds)
    in_specs = []
    for arr, spec in ins:
        if spec is None:
            in_specs.append(pl.BlockSpec(arr.shape, lambda i, nd=arr.ndim: (0,) * nd))
        else:
            w, cb = spec
            in_specs.append(pl.BlockSpec((tm, w), lambda i, cb=cb: (i, cb)))
    out_specs = [pl.BlockSpec((tm, w), lambda i: (i, 0)) for w, _ in outs]
    out_specs += [pl.BlockSpec(s, lambda i, nd=len(s): (0,) * nd) for s in reds]
    out_shape = [jax.ShapeDtypeStruct((rows, w), dt) for w, dt in outs]
    out_shape += [jax.ShapeDtypeStruct(s, f32) for s in reds]

    def body(*refs):
        i = pl.program_id(0)
        vals = fn(i, *[r[...] for r in refs[:n_in]])
        for r, v in zip(refs[n_in:n_in + n_out], vals[:n_out]):
            r[...] = v.astype(r.dtype)
        red_refs = refs[n_in + n_out:]

        @pl.when(i == 0)
        def _():
            for r in red_refs:
                r[...] = jnp.zeros_like(r)

        for r, v in zip(red_refs, vals[n_out:]):
            r[...] += v

    res = pl.pallas_call(
        body, name=name, out_shape=out_shape, grid=(rows // tm,), in_specs=in_specs, out_specs=out_specs,
        compiler_params=_params(("arbitrary",)),
    )(*[a for a, _ in ins])
    return res


def _rms(x, g):
    return x * lax.rsqrt(jnp.mean(x * x, axis=-1, keepdims=True) + EPS) * g


def _row_mask(i, tm, lo, hi, shape):
    r = i * tm + lax.broadcasted_iota(i32, shape, 0)
    return (r >= lo) & (r < hi)


def _rms_fwd(x, g, name):
    rows, d = x.shape
    tm = _pick(rows, (640, 512, 384, 256, 128))
    return _rowwise(lambda i, xv, gv: (_rms(xv, gv),), name, rows, tm, [(x, (d, 0)), (g, None)], [(d, bf16)])[0]


def _rms_bwd(x, g, dn, res, name):
    rows, d = x.shape
    tm = _pick(rows, (640, 512, 384, 256, 128))

    def fn(i, xv, gv, dnv, rv):
        _, vjp = jax.vjp(_rms, xv, gv)
        dx, dg = vjp(dnv)
        return rv + dx, dg

    return _rowwise(fn, name, rows, tm, [(x, (d, 0)), (g, None), (dn, (d, 0)), (res, (d, 0))], [(d, f32)], [(1, d)])


def _swiglu(gate, up):
    return jax.nn.silu(gate) * up


def _swiglu_fwd(gu, name):
    rows, f2 = gu.shape
    f = f2 // 2
    tm = _pick(rows, (128,))
    return _rowwise(lambda i, a, b: (_swiglu(a, b),), name, rows, tm, [(gu, (f, 0)), (gu, (f, 1))], [(f, bf16)])[0]


def _swiglu_bwd(gu, dact, name):
    rows, f2 = gu.shape
    f = f2 // 2
    tm = _pick(rows, (128,))

    def fn(i, a, b, dv):
        _, vjp = jax.vjp(_swiglu, a, b)
        da, db = vjp(dv)
        return (jnp.concatenate([da, db], axis=1),)

    return _rowwise(fn, name, rows, tm, [(gu, (f, 0)), (gu, (f, 1)), (dact, (f, 0))], [(f2, bf16)])[0]


def _loss_head(h, g, tgt, lo, hi, name):
    rows, d = h.shape
    tm = _pick(rows, (640, 512, 384, 256, 128))

    def fn(i, hv, gv, tv):
        mask = _row_mask(i, tm, lo, hi, (tm, 1))

        def f(hh, gg):
            err = _rms(hh, gg) - tv
            per_row = jnp.where(mask, jnp.mean(err * err, axis=-1, keepdims=True), 0.0)
            return 0.5 * jnp.sum(per_row, axis=0, keepdims=True)

        loss, vjp = jax.vjp(f, hv, gv)
        dh, dg = vjp(jnp.ones_like(loss))
        return dh, dg, jnp.broadcast_to(loss, (1, LANES))

    return _rowwise(fn, name, rows, tm, [(h, (d, 0)), (g, None), (tgt, (d, 0))], [(d, f32)], [(1, d), (1, LANES)])


def _heads_l2(x):
    t = x.shape[0]
    x3 = x.reshape(t, GDN_HEADS, GDN_DIM)
    return (x3 * lax.rsqrt(jnp.sum(x3 * x3, axis=-1, keepdims=True) + EPS)).reshape(t, GDN_WIDTH)


def _gdn_act(conv, a_in, b_in, a_log, dt_bias, mask):
    s = jax.nn.silu(conv)
    q = _heads_l2(s[:, :GDN_WIDTH])
    k = _heads_l2(s[:, GDN_WIDTH:2 * GDN_WIDTH])
    v = s[:, 2 * GDN_WIDTH:]
    g = jnp.where(mask, -jnp.exp(a_log) * jax.nn.softplus(a_in + dt_bias), 0.0)
    beta = jnp.where(mask, jax.nn.sigmoid(b_in), 0.0)
    return q, k, v, g, beta


def _widen(x8):
    return jnp.concatenate([jnp.broadcast_to(x8[:, h:h + 1], (x8.shape[0], GDN_DIM)) for h in range(GDN_HEADS)], axis=1)


def _narrow(xw):
    t = xw.shape[0]
    lane = lax.broadcasted_iota(i32, (t, LANES), 1)
    out = jnp.zeros((t, LANES), f32)
    for h in range(GDN_HEADS):
        s = jnp.sum(xw[:, h * GDN_DIM:(h + 1) * GDN_DIM], axis=1, keepdims=True)
        out = out + jnp.where(lane == h, s, 0.0)
    return out


def _conv_taps(cur, prev8, w):
    tm = cur.shape[0]
    cat = jnp.concatenate([prev8, cur], axis=0)
    y = cur * w[CONV_WIDTH - 1:CONV_WIDTH, :]
    for j in range(1, CONV_WIDTH):
        y = y + pltpu.roll(cat, j, axis=0)[8:8 + tm, :] * w[CONV_WIDTH - 1 - j:CONV_WIDTH - j, :]
    return y


def _gdn_prep_specs(proj, tm):
    c3 = 3 * GDN_WIDTH
    ab = 4 * GDN_WIDTH // LANES
    t8 = tm // 8
    return [
        pl.BlockSpec((tm, c3), lambda i: (i, 0)),
        pl.BlockSpec((8, c3), lambda i: (jnp.maximum(i * t8 - 1, 0), 0)),
        pl.BlockSpec((tm, LANES), lambda i: (i, ab)),
        pl.BlockSpec((tm, LANES), lambda i: (i, ab + 1)),
    ]


def _full(arr):
    return pl.BlockSpec(arr.shape, lambda i, nd=arr.ndim: (0,) * nd)


def _gdn_prep_fwd(proj, conv_w, a_log, dt_bias, ltri, lo, hi, tm):
    rows = proj.shape[0]

    def body(cur, prev8, a_in, b_in, w, al, dtb, lt, q_o, k_o, v_o, g_o, b_o):
        i = pl.program_id(0)
        mask = _row_mask(i, tm, lo, hi, (tm, LANES)) & (lax.broadcasted_iota(i32, (tm, LANES), 1) < GDN_HEADS)
        conv = _conv_taps(cur[...], prev8[...], w[...])
        q, k, v, g, beta = _gdn_act(conv, a_in[...], b_in[...], al[...], dtb[...], mask)
        q_o[...] = q
        k_o[...] = k
        v_o[...] = v
        gcum = jnp.dot(lt[...], g, preferred_element_type=f32, precision=lax.Precision.HIGHEST)
        g_o[...] = _widen(gcum)
        b_o[...] = _widen(beta)

    wide = jax.ShapeDtypeStruct((rows, GDN_WIDTH), f32)
    o_spec = pl.BlockSpec((tm, GDN_WIDTH), lambda i: (i, 0))
    return pl.pallas_call(
        body, name="gdn_prep_fwd", out_shape=[wide] * 5, grid=(rows // tm,),
        in_specs=_gdn_prep_specs(proj, tm) + [_full(conv_w), _full(a_log), _full(dt_bias), _full(ltri)],
        out_specs=[o_spec] * 5, compiler_params=_params(("parallel",)),
    )(proj, proj, proj, proj, conv_w, a_log, dt_bias, ltri)


def _gdn_prep_bwd_act(proj, conv_w, a_log, dt_bias, ltri, dq, dk, dv, dgw, dbw, lo, hi, tm):
    rows = proj.shape[0]
    c3 = 3 * GDN_WIDTH

    def body(cur, prev8, a_in, b_in, w, al, dtb, lt, dq_r, dk_r, dv_r, dg_r, db_r, dconv_o, da_o, dbin_o, dal_o, ddt_o):
        i = pl.program_id(0)
        mask = _row_mask(i, tm, lo, hi, (tm, LANES)) & (lax.broadcasted_iota(i32, (tm, LANES), 1) < GDN_HEADS)
        conv = _conv_taps(cur[...], prev8[...], w[...])
        dgcum = _narrow(dg_r[...])
        dg = lax.dot_general(lt[...], dgcum, (((0,), (0,)), ((), ())), preferred_element_type=f32,
                             precision=lax.Precision.HIGHEST)
        dbeta = _narrow(db_r[...])
        _, vjp = jax.vjp(lambda c, a, b, x, y: _gdn_act(c, a, b, x, y, mask), conv, a_in[...], b_in[...], al[...], dtb[...])
        dconv, da, dbin, dal, ddt = vjp((dq_r[...], dk_r[...], dv_r[...], dg, dbeta))
        dconv_o[...] = dconv
        da_o[...] = da
        dbin_o[...] = dbin

        @pl.when(i == 0)
        def _():
            dal_o[...] = jnp.zeros_like(dal_o)
            ddt_o[...] = jnp.zeros_like(ddt_o)

        dal_o[...] += dal
        ddt_o[...] += ddt

    w_spec = pl.BlockSpec((tm, GDN_WIDTH), lambda i: (i, 0))
    n_spec = pl.BlockSpec((tm, LANES), lambda i: (i, 0))
    s_spec = pl.BlockSpec((1, LANES), lambda i: (0, 0))
    return pl.pallas_call(
        body, name="gdn_prep_bwd_act",
        out_shape=[jax.ShapeDtypeStruct((rows, c3), f32), jax.ShapeDtypeStruct((rows, LANES), f32),
                   jax.ShapeDtypeStruct((rows, LANES), f32), jax.ShapeDtypeStruct((1, LANES), f32),
                   jax.ShapeDtypeStruct((1, LANES), f32)],
        grid=(rows // tm,),
        in_specs=_gdn_prep_specs(proj, tm) + [_full(conv_w), _full(a_log), _full(dt_bias), _full(ltri)] + [w_spec] * 5,
        out_specs=[pl.BlockSpec((tm, c3), lambda i: (i, 0)), n_spec, n_spec, s_spec, s_spec],
        compiler_params=_params(("arbitrary",)),
    )(proj, proj, proj, proj, conv_w, a_log, dt_bias, ltri, dq, dk, dv, dgw, dbw)


def _gdn_prep_bwd_conv(proj, conv_w, dconv, dgate, da, dbin, tm):
    rows, width = proj.shape
    c3 = 3 * GDN_WIDTH
    t8 = tm // 8
    nt = rows // tm

    def body(cur, prev8, w, dc, dnext8, dgt, da_r, db_r, dp_o, dw_o):
        i = pl.program_id(0)
        d = dc[...]
        nxt = jnp.where(i == nt - 1, 0.0, dnext8[...])
        cat = jnp.concatenate([d, nxt], axis=0)
        wv = w[...]
        dx = d * wv[CONV_WIDTH - 1:CONV_WIDTH, :]
        for j in range(1, CONV_WIDTH):
            dx = dx + pltpu.roll(cat, tm + 8 - j, axis=0)[:tm, :] * wv[CONV_WIDTH - 1 - j:CONV_WIDTH - j, :]
        dp_o[:, :c3] = dx.astype(bf16)
        dp_o[:, c3:4 * GDN_WIDTH] = dgt[...].astype(bf16)
        dp_o[:, 4 * GDN_WIDTH:4 * GDN_WIDTH + LANES] = da_r[...].astype(bf16)
        dp_o[:, 4 * GDN_WIDTH + LANES:] = db_r[...].astype(bf16)

        xcat = jnp.concatenate([prev8[...], cur[...]], axis=0)
        parts = [jnp.sum(d * cur[...], axis=0, keepdims=True)]
        for j in range(1, CONV_WIDTH):
            parts.append(jnp.sum(d * pltpu.roll(xcat, j, axis=0)[8:8 + tm, :], axis=0, keepdims=True))
        dwt = jnp.concatenate(parts[::-1], axis=0)

        @pl.when(i == 0)
        def _():
            dw_o[...] = jnp.zeros_like(dw_o)

        dw_o[...] += dwt

    n_spec = pl.BlockSpec((tm, LANES), lambda i: (i, 0))
    return pl.pallas_call(
        body, name="gdn_prep_bwd_conv",
        out_shape=[jax.ShapeDtypeStruct((rows, width), bf16), jax.ShapeDtypeStruct((CONV_WIDTH, c3), f32)],
        grid=(nt,),
        in_specs=[pl.BlockSpec((tm, c3), lambda i: (i, 0)),
                  pl.BlockSpec((8, c3), lambda i: (jnp.maximum(i * t8 - 1, 0), 0)),
                  _full(conv_w),
                  pl.BlockSpec((tm, c3), lambda i: (i, 0)),
                  pl.BlockSpec((8, c3), lambda i: (jnp.minimum((i + 1) * t8, rows // 8 - 1), 0)),
                  pl.BlockSpec((tm, GDN_WIDTH), lambda i: (i, 0)), n_spec, n_spec],
        out_specs=[pl.BlockSpec((tm, width), lambda i: (i, 0)), pl.BlockSpec((CONV_WIDTH, c3), lambda i: (0, 0))],
        compiler_params=_params(("arbitrary",)),
    )(proj, proj, conv_w, dconv, dconv, dgate, da, dbin)


def _split(a):
    hi = a.astype(bf16)
    return hi, (a - hi.astype(f32)).astype(bf16)


def _dot3(a, b, mode):
    ah, al = _split(a)
    bh, bl = _split(b)
    d = lambda x, y: lax.dot_general(x, y, (_DIMS[mode], ((), ())), preferred_element_type=f32)
    return d(ah, bh) + (d(ah, bl) + d(al, bh))


def _make_mm(dot):
    @jax.custom_vjp
    def nn(a, b):
        return dot(a, b, "nn")

    nn.defvjp(lambda a, b: (dot(a, b, "nn"), (a, b)),
              lambda r, ct: (dot(ct, r[1], "nt"), dot(r[0], ct, "tn")))

    @jax.custom_vjp
    def nt(a, b):
        return dot(a, b, "nt")

    nt.defvjp(lambda a, b: (dot(a, b, "nt"), (a, b)),
              lambda r, ct: (dot(ct, r[1], "nn"), dot(ct, r[0], "tn")))

    @jax.custom_vjp
    def tn(a, b):
        return dot(a, b, "tn")

    tn.defvjp(lambda a, b: (dot(a, b, "tn"), (a, b)),
              lambda r, ct: (dot(r[1], ct, "nt"), dot(r[0], ct, "nn")))
    return nn, nt, tn


_mm, _mm_nt, _mm_tn = _make_mm(_bdot)
_mm3, _, _ = _make_mm(_dot3)


def _gdn_chunk(q, k, v, gcb, bcb, s_in):
    c = q.shape[0]
    ri = lax.broadcasted_iota(i32, (c, c), 0)
    ci = lax.broadcasted_iota(i32, (c, c), 1)
    incl, strict = ri >= ci, ri > ci
    qs = q * (GDN_DIM ** -0.5)
    gc = gcb[:, :c]
    decay = jnp.where(incl, jnp.exp(jnp.where(incl, gc - gc.T, 0.0)), 0.0)
    a1 = jnp.where(strict, bcb[:, :c] * decay * _mm_nt(k, k), 0.0)
    eg = jnp.exp(gcb)
    x = jnp.concatenate([bcb * v, (bcb * eg) * k], axis=1)
    pows = [a1]
    for _ in range(5):
        pows.append(_mm3(pows[-1], pows[-1]))
    for p in pows[:0:-1]:
        x = x + _mm3(p, x)
    x = x - _mm3(a1, x)
    u_base, w_corr = x[:, :GDN_DIM], x[:, GDN_DIM:]
    attn = _mm_nt(qs, k) * decay
    rowi = lax.broadcasted_iota(i32, gcb.shape, 0)
    glast = jnp.sum(jnp.where(rowi == c - 1, gcb, 0.0), axis=0, keepdims=True)
    u = u_base - _mm(w_corr, s_in)
    o = _mm(qs * eg, s_in) + _mm(attn, u)
    s_out = s_in * jnp.exp(glast) + _mm_tn(k * jnp.exp(glast - gcb), u)
    return o, s_out


def _gdn_fwd(q, k, v, gw, bw):
    rows = q.shape[0]
    nc = rows // CHUNK
    blk = pl.BlockSpec((CHUNK, GDN_DIM), lambda h, c: (c, h))

    def body(q_r, k_r, v_r, g_r, b_r, o_r, st_r, s_sc):
        @pl.when(pl.program_id(1) == 0)
        def _():
            s_sc[...] = jnp.zeros_like(s_sc)

        s_in = s_sc[...]
        st_r[0, 0] = s_in
        o, s_out = _gdn_chunk(q_r[...], k_r[...], v_r[...], g_r[...], b_r[...], s_in)
        o_r[...] = o
        s_sc[...] = s_out

    return pl.pallas_call(
        body, name="gdn_fwd",
        out_shape=[jax.ShapeDtypeStruct((rows, GDN_WIDTH), f32), jax.ShapeDtypeStruct((nc, GDN_HEADS, GDN_DIM, GDN_DIM), f32)],
        grid=(GDN_HEADS, nc), in_specs=[blk] * 5,
        out_specs=[blk, pl.BlockSpec((1, 1, GDN_DIM, GDN_DIM), lambda h, c: (c, h, 0, 0))],
        scratch_shapes=[pltpu.VMEM((GDN_DIM, GDN_DIM), f32)],
        compiler_params=_params(("parallel", "arbitrary")),
    )(q, k, v, gw, bw)


def _gdn_bwd(q, k, v, gw, bw, states, do):
    rows = q.shape[0]
    nc = rows // CHUNK
    blk = pl.BlockSpec((CHUNK, GDN_DIM), lambda h, c: (nc - 1 - c, h))

    def body(q_r, k_r, v_r, g_r, b_r, st_r, do_r, dq_r, dk_r, dv_r, dg_r, db_r, ds_sc):
        @pl.when(pl.program_id(1) == 0)
        def _():
            ds_sc[...] = jnp.zeros_like(ds_sc)

        _, vjp = jax.vjp(_gdn_chunk, q_r[...], k_r[...], v_r[...], g_r[...], b_r[...], st_r[0, 0])
        dq, dk, dv, dg, db, ds_in = vjp((do_r[...], ds_sc[...]))
        dq_r[...] = dq
        dk_r[...] = dk
        dv_r[...] = dv
        dg_r[...] = dg
        db_r[...] = db
        ds_sc[...] = ds_in

    wide = jax.ShapeDtypeStruct((rows, GDN_WIDTH), f32)
    return pl.pallas_call(
        body, name="gdn_bwd", out_shape=[wide] * 5, grid=(GDN_HEADS, nc),
        in_specs=[blk] * 5 + [pl.BlockSpec((1, 1, GDN_DIM, GDN_DIM), lambda h, c: (nc - 1 - c, h, 0, 0)), blk],
        out_specs=[blk] * 5, scratch_shapes=[pltpu.VMEM((GDN_DIM, GDN_DIM), f32)],
        compiler_params=_params(("parallel", "arbitrary")),
    )(q, k, v, gw, bw, states, do)


def _gdn_gate(o, gate, og):
    t = o.shape[0]
    o3 = o.reshape(t, GDN_HEADS, GDN_DIM)
    n = o3 * lax.rsqrt(jnp.mean(o3 * o3, axis=-1, keepdims=True) + EPS) * og.reshape(1, 1, GDN_DIM)
    return n.reshape(t, GDN_WIDTH) * jax.nn.silu(gate)


def _gdn_gate_fwd(o, proj, og, tm):
    rows = o.shape[0]
    return _rowwise(lambda i, ov, gv, w: (_gdn_gate(ov, gv, w),), "gdn_gate_fwd", rows, tm,
                    [(o, (GDN_WIDTH, 0)), (proj, (GDN_WIDTH, 3)), (og, None)], [(GDN_WIDTH, bf16)])[0]


def _gdn_gate_bwd(o, proj, og, dy, tm):
    rows = o.shape[0]

    def fn(i, ov, gv, w, d):
        _, vjp = jax.vjp(_gdn_gate, ov, gv, w)
        return vjp(d)

    return _rowwise(fn, "gdn_gate_bwd", rows, tm,
                    [(o, (GDN_WIDTH, 0)), (proj, (GDN_WIDTH, 3)), (og, None), (dy, (GDN_WIDTH, 0))],
                    [(GDN_WIDTH, f32), (GDN_WIDTH, f32)], [(1, GDN_DIM)])


def _sb_scores(qh, kj, i, j, scale):
    z = _bdot(qh, kj, "nt") * scale
    qpos = i * SB_BLOCK + lax.broadcasted_iota(i32, z.shape, 0)
    kpos = j * SB_BLOCK + lax.broadcasted_iota(i32, z.shape, 1)
    vis = (kpos < qpos) & (kpos >= FRONT)
    l1p = jnp.log1p(jnp.exp(-jnp.abs(z)))
    log_beta = -(jnp.maximum(-z, 0.0) + l1p)
    log_keep = jnp.where(vis, -(jnp.maximum(z, 0.0) + l1p), 0.0)
    return z, vis, log_beta, log_keep


def _tri_sum(x, tri):
    hi, lo = _split(x)
    return jnp.dot(hi, tri, preferred_element_type=f32) + jnp.dot(lo, tri, preferred_element_type=f32)


def _sb_live(t, i, run):
    return (t <= i) & (jnp.max(run) > -SB_UNDERFLOW)


def _sb_fwd(q, kv, width):
    rows = q.shape[0]
    nq = rows // SB_BLOCK
    npair = width // LANES
    scale = SB_DIM ** -0.5

    def body(q_r, k_r, v_r, o_r):
        i = pl.program_id(1)
        rj = lax.broadcasted_iota(i32, (SB_BLOCK, SB_BLOCK), 0)
        cs = lax.broadcasted_iota(i32, (SB_BLOCK, SB_BLOCK), 1)
        after = (rj > cs).astype(bf16)
        for a in range(LANES // SB_DIM):
            sl = slice(a * SB_DIM, (a + 1) * SB_DIM)
            qh = q_r[:, sl]

            def step(carry):
                t, acc, run = carry
                j = i - t
                ks = pl.ds(pl.multiple_of(j * SB_BLOCK, SB_BLOCK), SB_BLOCK)
                _, vis, log_beta, log_keep = _sb_scores(qh, k_r[ks, sl], i, j, scale)
                log_w = log_beta + _tri_sum(log_keep, after) + run
                w = jnp.where(vis, jnp.exp(log_w), 0.0)
                acc = acc + _bdot(w, v_r[ks, sl], "nn")
                return t + 1, acc, run + jnp.sum(log_keep, axis=1, keepdims=True)

            init = (jnp.int32(0), jnp.zeros((SB_BLOCK, SB_DIM), f32), jnp.zeros((SB_BLOCK, 1), f32))
            _, acc, _ = lax.while_loop(lambda c: _sb_live(c[0], i, c[2]), step, init)
            o_r[:, sl] = acc

    return pl.pallas_call(
        body, name="sb_fwd", out_shape=jax.ShapeDtypeStruct((rows, width), f32), grid=(npair, nq),
        in_specs=[pl.BlockSpec((SB_BLOCK, LANES), lambda p, i: (i, p)),
                  pl.BlockSpec((rows, LANES), lambda p, i: (0, p)),
                  pl.BlockSpec((rows, LANES), lambda p, i: (0, npair + p))],
        out_specs=pl.BlockSpec((SB_BLOCK, LANES), lambda p, i: (i, p)),
        compiler_params=_params(("parallel", "arbitrary")),
    )(q, kv, kv)


def _sb_bwd(q, kv, do, width):
    rows = q.shape[0]
    nq = rows // SB_BLOCK
    npair = width // LANES
    scale = SB_DIM ** -0.5

    def body(q_r, k_r, v_r, do_r, dq_r, dk_r, dv_r, e_sc, sig_sc, w_sc):
        i = pl.program_id(1)

        @pl.when(i == 0)
        def _():
            dk_r[...] = jnp.zeros_like(dk_r)
            dv_r[...] = jnp.zeros_like(dv_r)

        rj = lax.broadcasted_iota(i32, (SB_BLOCK, SB_BLOCK), 0)
        cs = lax.broadcasted_iota(i32, (SB_BLOCK, SB_BLOCK), 1)
        after = (rj > cs).astype(bf16)
        from_s = (rj >= cs).astype(bf16)
        zero1 = jnp.zeros((SB_BLOCK, 1), f32)
        for a in range(LANES // SB_DIM):
            sl = slice(a * SB_DIM, (a + 1) * SB_DIM)
            qh = q_r[:, sl]
            doh = do_r[:, sl]

            def weigh(carry):
                t, run, erun = carry
                j = i - t
                ks = pl.ds(pl.multiple_of(j * SB_BLOCK, SB_BLOCK), SB_BLOCK)
                z, vis, log_beta, log_keep = _sb_scores(qh, k_r[ks, sl], i, j, scale)
                log_w = log_beta + _tri_sum(log_keep, after) + run
                w = jnp.where(vis, jnp.exp(log_w), 0.0)
                e = w * _bdot(doh, v_r[ks, sl], "nt")
                e_sc[t] = e
                sig_sc[t] = jax.nn.sigmoid(z)
                w_sc[t] = w.astype(w_sc.dtype)
                return t + 1, run + jnp.sum(log_keep, axis=1, keepdims=True), erun + jnp.sum(e, axis=1, keepdims=True)

            n_blk, _, etot = lax.while_loop(lambda c: _sb_live(c[0], i, c[1]), weigh, (jnp.int32(0), zero1, zero1))

            def push(t, carry):
                dq, erun = carry
                j = i - t
                ks = pl.ds(pl.multiple_of(j * SB_BLOCK, SB_BLOCK), SB_BLOCK)
                e, sig = e_sc[t], sig_sc[t]
                vis = (j * SB_BLOCK + cs < i * SB_BLOCK + rj) & (j * SB_BLOCK + cs >= FRONT)
                before = etot - erun - _tri_sum(e, from_s)
                dz = jnp.where(vis, e * (1.0 - sig) - before * sig, 0.0) * scale
                dk_r[ks, sl] += _bdot(dz, qh, "tn")
                dv_r[ks, sl] += _bdot(w_sc[t], doh, "tn")
                return dq + _bdot(dz, k_r[ks, sl], "nn"), erun + jnp.sum(e, axis=1, keepdims=True)

            dq, _ = lax.fori_loop(0, n_blk, push, (jnp.zeros((SB_BLOCK, SB_DIM), f32), zero1))
            dq_r[:, sl] = dq

    blk = pl.BlockSpec((SB_BLOCK, LANES), lambda p, i: (i, p))
    col = pl.BlockSpec((rows, LANES), lambda p, i: (0, p))
    wide = jax.ShapeDtypeStruct((rows, width), f32)
    return pl.pallas_call(
        body, name="sb_bwd", out_shape=[wide] * 3, grid=(npair, nq),
        in_specs=[blk, col, pl.BlockSpec((rows, LANES), lambda p, i: (0, npair + p)), blk],
        out_specs=[blk, col, col],
        scratch_shapes=[pltpu.VMEM((nq, SB_BLOCK, SB_BLOCK), f32), pltpu.VMEM((nq, SB_BLOCK, SB_BLOCK), f32),
                        pltpu.VMEM((nq, SB_BLOCK, SB_BLOCK), bf16)],
        compiler_params=_params(("parallel", "arbitrary")),
    )(q, kv, kv, do)


_FLIPS = ((1, 0), (0, 1), (1, 1))
_ANY = pl.BlockSpec(memory_space=pl.ANY)


def _flip(v, a):
    return v + a - 2 * a * v


def _gather_chips(p):
    rows, cols = p.shape
    half = rows // 2

    def body(p_ref, o_ref, send_sems, recv_sems, local_sem):
        x, y, c = lax.axis_index("x"), lax.axis_index("y"), lax.axis_index("c")
        me, sibling = (x, y, c), (x, y, 1 - c)
        chip = 2 * x + y

        def piece(s, hc):
            return o_ref.at[s, pl.ds(pl.multiple_of(hc * half, 16), half), :]

        def copy(n, s, hc, to, src=None):
            return pltpu.make_async_remote_copy(
                src_ref=piece(s, hc) if src is None else src, dst_ref=piece(s, hc),
                send_sem=send_sems.at[n], recv_sem=recv_sems.at[n], device_id=to, device_id_type=MESH)

        mine = pltpu.make_async_copy(p_ref, o_ref.at[chip], local_sem)
        mine.start()
        others = [(_flip(x, a), _flip(y, b)) for a, b in _FLIPS]
        my_half = p_ref.at[pl.ds(pl.multiple_of(c * half, 16), half), :]
        first = [copy(n, chip, c, (ox, oy, c), src=my_half) for n, (ox, oy) in enumerate(others)]
        for cp in first:
            cp.start()
        passed = [copy(3 + n, 2 * ox + oy, c, sibling) for n, (ox, oy) in enumerate(others)]
        for n, (ox, oy) in enumerate(others):
            copy(n, 2 * ox + oy, c, me).wait_recv()
            passed[n].start()
        for n, (ox, oy) in enumerate(others):
            copy(3 + n, 2 * ox + oy, 1 - c, me).wait_recv()
        for cp in first + passed:
            cp.wait_send()
        mine.wait()

    return pl.pallas_call(
        body, name="gather_chips", out_shape=jax.ShapeDtypeStruct((N_CHIPS, rows, cols), p.dtype),
        in_specs=[_ANY], out_specs=_ANY,
        scratch_shapes=[pltpu.SemaphoreType.DMA((6,)), pltpu.SemaphoreType.DMA((6,)), pltpu.SemaphoreType.DMA],
    )(p)


def _scatter_chips(g):
    _, rows, cols = g.shape

    def body(g_ref, o_ref, send_sems, recv_sems):
        x, y, c = lax.axis_index("x"), lax.axis_index("y"), lax.axis_index("c")
        cps = []
        for n, (a, b) in enumerate(_FLIPS):
            ox, oy = _flip(x, a), _flip(y, b)
            cps.append(pltpu.make_async_remote_copy(
                src_ref=g_ref.at[2 * ox + oy], dst_ref=o_ref.at[n], send_sem=send_sems.at[n], recv_sem=recv_sems.at[n],
                device_id=(ox, oy, c), device_id_type=MESH))
        for cp in cps:
            cp.start()
        for cp in cps:
            cp.wait()

    return pl.pallas_call(
        body, name="scatter_chips", out_shape=jax.ShapeDtypeStruct((3, rows, cols), g.dtype),
        in_specs=[_ANY], out_specs=_ANY,
        scratch_shapes=[pltpu.SemaphoreType.DMA((3,)), pltpu.SemaphoreType.DMA((3,))],
    )(g)


def _swap_sibling(a):
    def body(a_ref, o_ref, send_sem, recv_sem):
        x, y, c = lax.axis_index("x"), lax.axis_index("y"), lax.axis_index("c")
        cp = pltpu.make_async_remote_copy(src_ref=a_ref, dst_ref=o_ref, send_sem=send_sem, recv_sem=recv_sem,
                                          device_id=(x, y, 1 - c), device_id_type=MESH)
        cp.start()
        cp.wait()

    return pl.pallas_call(
        body, name="swap_sibling", out_shape=jax.ShapeDtypeStruct(a.shape, a.dtype), in_specs=[_ANY], out_specs=_ANY,
        scratch_shapes=[pltpu.SemaphoreType.DMA, pltpu.SemaphoreType.DMA],
    )(a)


def _gather_all(v):
    m_per, n = v.shape

    def body(x_ref, out_ref, send_sems, recv_sems, local_sem):
        x, y, c = lax.axis_index("x"), lax.axis_index("y"), lax.axis_index("c")
        me, sibling = (x, y, c), (x, y, 1 - c)
        chips = [(_flip(x, a), _flip(y, b)) for a, b in _FLIPS]

        def rows(px, py, pc):
            return out_ref.at[pl.ds(pl.multiple_of((4 * px + 2 * py + pc) * m_per, 8), m_per), :]

        def copy(k, block, to, src=None):
            return pltpu.make_async_remote_copy(
                src_ref=rows(*block) if src is None else src, dst_ref=rows(*block),
                send_sem=send_sems.at[k], recv_sem=recv_sems.at[k], device_id=to, device_id_type=MESH)

        mine = pltpu.make_async_copy(x_ref, rows(*me), local_sem)
        mine.start()
        first = [copy(0, me, sibling, src=x_ref)]
        first += [copy(1 + j, me, (*chip, c), src=x_ref) for j, chip in enumerate(chips)]
        for cp in first:
            cp.start()
        passed = [copy(4 + j, (*chip, c), sibling) for j, chip in enumerate(chips)]
        for j, chip in enumerate(chips):
            copy(1 + j, (*chip, c), me).wait_recv()
            passed[j].start()
        copy(0, sibling, me).wait_recv()
        for j, chip in enumerate(chips):
            copy(4 + j, (*chip, 1 - c), me).wait_recv()
        for cp in first + passed:
            cp.wait_send()
        mine.wait()

    return pl.pallas_call(
        body, name="gather_all", out_shape=jax.ShapeDtypeStruct((N_DEV * m_per, n), v.dtype),
        in_specs=[pl.BlockSpec(memory_space=pltpu.VMEM)], out_specs=pl.BlockSpec(memory_space=pltpu.VMEM),
        scratch_shapes=[pltpu.SemaphoreType.DMA((7,)), pltpu.SemaphoreType.DMA((7,)), pltpu.SemaphoreType.DMA],
    )(v)


def _sum_chips(own, got):
    rows, cols = own.shape
    tm = _pick(rows, (512, 256, 128, 64, 32, 16))

    def body(own_r, got_r, o_r):
        acc = own_r[...]
        for n in range(3):
            acc = acc + got_r[n].astype(f32)
        o_r[...] = acc

    return pl.pallas_call(
        body, name="sum_chips", out_shape=jax.ShapeDtypeStruct((rows, cols), f32), grid=(rows // tm,),
        in_specs=[pl.BlockSpec((tm, cols), lambda i: (i, 0)), pl.BlockSpec((3, tm, cols), lambda i: (0, i, 0))],
        out_specs=pl.BlockSpec((tm, cols), lambda i: (i, 0)), compiler_params=_params(("parallel",)),
    )(own, got)


def _add(a, b, name):
    rows, cols = a.shape
    tm = _pick(rows, (512, 256, 128, 64, 32, 16, 8))
    return _rowwise(lambda i, u, v: (u + v,), name, rows, tm, [(a, (cols, 0)), (b, (cols, 0))], [(cols, f32)])[0]


def _sum_devices(g, m_per):
    n = g.shape[1]

    def body(g_r, o_r):
        acc = g_r[0:m_per, :]
        for d in range(1, N_DEV):
            acc = acc + g_r[d * m_per:(d + 1) * m_per, :]
        o_r[...] = acc

    return pl.pallas_call(body, name="sum_devices", out_shape=jax.ShapeDtypeStruct((m_per, n), f32))(g)


def _adamw(w, g, m, v, name):
    shape = w.shape
    cols = shape[-1]
    rows = w.size // cols
    w2, g2, m2, v2 = (t.reshape(rows, cols) for t in (w, g, m, v))
    tm = _pick(rows, (256, 128, 64, 32, 16, 8)) if rows * cols * 4 > (1 << 20) else rows

    def fn(i, wv, gv, mv, vv):
        mn = ADAM_B1 * mv + (1.0 - ADAM_B1) * gv
        vn = ADAM_B2 * vv + (1.0 - ADAM_B2) * jnp.square(gv)
        m_hat = mn / (1.0 - ADAM_B1 ** ADAM_STEP)
        v_hat = vn / (1.0 - ADAM_B2 ** ADAM_STEP)
        delta = -ADAM_LR * (m_hat / (jnp.sqrt(v_hat) + ADAM_EPS) + ADAM_WD * wv)
        return delta, mn, vn

    outs = _rowwise(fn, name, rows, tm, [(t, (cols, 0)) for t in (w2, g2, m2, v2)], [(cols, f32)] * 3)
    return tuple(o.reshape(shape) for o in outs)


def _pack(pieces, rows, dtype):
    flat = jnp.concatenate([p.reshape(-1).astype(dtype) for p in pieces])
    return jnp.pad(flat, (0, rows * PACK_COLS - flat.size)).reshape(rows, PACK_COLS)


def _unpack(buf, shapes):
    lead = buf.shape[:-2]
    flat = buf.reshape(lead + (-1,))
    out, off = [], 0
    for s in shapes:
        n = 1
        for d in s:
            n *= d
        out.append(flat[..., off:off + n].reshape(lead + tuple(s)))
        off += n
    return out


def _join_cols(t):
    return jnp.moveaxis(t, 0, -2).reshape(t.shape[1:-1] + (N_CHIPS * t.shape[-1],))


def _join_rows(t):
    return t.reshape((N_CHIPS * t.shape[1],) + t.shape[2:])


def _split_cols(t):
    r, c4 = t.shape
    return jnp.moveaxis(t.reshape(r, N_CHIPS, c4 // N_CHIPS), 1, 0)


def _split_rows(t):
    return t.reshape((N_CHIPS, t.shape[0] // N_CHIPS) + t.shape[1:])


def _as_bf16_pairs(t):
    return lax.bitcast_convert_type(t, jnp.bfloat16).reshape(-1)


def _from_bf16_pairs(t, shape):
    pairs = t.astype(jnp.bfloat16).reshape(t.shape[:-1] + (t.shape[-1] // 2, 2))
    return lax.bitcast_convert_type(pairs, f32).reshape(t.shape[:-1] + tuple(shape))


def kernel(x, meta_tokens, gdn_norm_g, gdn_w_in, gdn_conv_w, gdn_a_log, gdn_dt_bias, gdn_onorm_g, gdn_w_out, kv_norm_g, w_kv, sb_norm_g, sb_w_q, sb_w_o, ffn_norm_g, ffn_w_gate_up, ffn_w_down, final_norm_g, loss_target, m_meta_tokens, m_gdn_norm_g, m_gdn_w_in, m_gdn_conv_w, m_gdn_a_log, m_gdn_dt_bias, m_gdn_onorm_g, m_gdn_w_out, m_kv_norm_g, m_w_kv, m_sb_norm_g, m_sb_w_q, m_sb_w_o, m_ffn_norm_g, m_ffn_w_gate_up, m_ffn_w_down, m_final_norm_g, v_meta_tokens, v_gdn_norm_g, v_gdn_w_in, v_gdn_conv_w, v_gdn_a_log, v_gdn_dt_bias, v_gdn_onorm_g, v_gdn_w_out, v_kv_norm_g, v_w_kv, v_sb_norm_g, v_sb_w_q, v_sb_w_o, v_ffn_norm_g, v_ffn_w_gate_up, v_ffn_w_down, v_final_norm_g):
    weights = dict(meta_tokens=meta_tokens, gdn_norm_g=gdn_norm_g, gdn_w_in=gdn_w_in, gdn_conv_w=gdn_conv_w,
                   gdn_a_log=gdn_a_log, gdn_dt_bias=gdn_dt_bias, gdn_onorm_g=gdn_onorm_g, gdn_w_out=gdn_w_out,
                   kv_norm_g=kv_norm_g, w_kv=w_kv, sb_norm_g=sb_norm_g, sb_w_q=sb_w_q, sb_w_o=sb_w_o,
                   ffn_norm_g=ffn_norm_g, ffn_w_gate_up=ffn_w_gate_up, ffn_w_down=ffn_w_down, final_norm_g=final_norm_g)
    m_in = dict(meta_tokens=m_meta_tokens, gdn_norm_g=m_gdn_norm_g, gdn_w_in=m_gdn_w_in, gdn_conv_w=m_gdn_conv_w,
                gdn_a_log=m_gdn_a_log, gdn_dt_bias=m_gdn_dt_bias, gdn_onorm_g=m_gdn_onorm_g, gdn_w_out=m_gdn_w_out,
                kv_norm_g=m_kv_norm_g, w_kv=m_w_kv, sb_norm_g=m_sb_norm_g, sb_w_q=m_sb_w_q, sb_w_o=m_sb_w_o,
                ffn_norm_g=m_ffn_norm_g, ffn_w_gate_up=m_ffn_w_gate_up, ffn_w_down=m_ffn_w_down, final_norm_g=m_final_norm_g)
    v_in = dict(meta_tokens=v_meta_tokens, gdn_norm_g=v_gdn_norm_g, gdn_w_in=v_gdn_w_in, gdn_conv_w=v_gdn_conv_w,
                gdn_a_log=v_gdn_a_log, gdn_dt_bias=v_gdn_dt_bias, gdn_onorm_g=v_gdn_onorm_g, gdn_w_out=v_gdn_w_out,
                kv_norm_g=v_kv_norm_g, w_kv=v_w_kv, sb_norm_g=v_sb_norm_g, sb_w_q=v_sb_w_q, sb_w_o=v_sb_w_o,
                ffn_norm_g=v_ffn_norm_g, ffn_w_gate_up=v_ffn_w_gate_up, ffn_w_down=v_ffn_w_down, final_norm_g=v_final_norm_g)
    names = list(weights)

    seq, d = x.shape[1], x.shape[2]
    lo_frames = FRONT + N_META
    used = lo_frames + seq
    rows = -(-used // SB_BLOCK) * SB_BLOCK
    tm = _pick(rows, (640, 512, 384, 256, 128))
    tp = _pick(rows, (320, 256, 128))
    n_ffn = ffn_w_gate_up.shape[0]
    d_ff = ffn_w_down.shape[1] * N_CHIPS
    sb_width = sb_w_q.shape[2]
    in_cols = gdn_w_in.shape[2] * N_CHIPS
    chip = 2 * lax.axis_index("x") + lax.axis_index("y")

    big = [gdn_w_in[0], gdn_w_out[0], w_kv, sb_w_q[0], sb_w_o[0], ffn_w_gate_up, ffn_w_down]
    small = [meta_tokens, gdn_norm_g, gdn_conv_w[0]]
    n_big = sum(t.size for t in big)
    n_all = n_big + 2 * sum(t.size for t in small)
    w_rows = -(-n_all // (32 * PACK_COLS)) * 32
    packed = _pack([t for t in big] + [_as_bf16_pairs(t) for t in small], w_rows, bf16)
    gathered = _gather_chips(packed)
    parts = _unpack(gathered, [t.shape for t in big] + [(2 * t.size,) for t in small])
    w_in_s, w_out_s, w_kv_s, w_q_s, w_o_s, w_gu_s, w_dn_s = parts[:7]
    w_in = _join_cols(w_in_s)
    pad_ab = jnp.zeros((d, LANES - GDN_HEADS), bf16)
    w_in_ext = jnp.concatenate([w_in[:, :4 * GDN_WIDTH], w_in[:, 4 * GDN_WIDTH:4 * GDN_WIDTH + GDN_HEADS], pad_ab,
                                w_in[:, 4 * GDN_WIDTH + GDN_HEADS:], pad_ab], axis=1)
    w_out = _join_rows(w_out_s)
    w_kvf = _join_cols(w_kv_s)
    w_k, w_v = w_kvf[:, :sb_width], w_kvf[:, sb_width:]
    w_q = _join_rows(w_q_s)
    w_o = _join_rows(w_o_s)
    w_gu = [_join_cols(w_gu_s[:, l]) for l in range(n_ffn)]
    w_dn = [_join_rows(w_dn_s[:, l]) for l in range(n_ffn)]
    meta_full = _join_cols(_from_bf16_pairs(parts[7], meta_tokens.shape))
    gdn_g_full = _join_cols(_from_bf16_pairs(parts[8], gdn_norm_g.shape))
    conv_full = _join_cols(_from_bf16_pairs(parts[9], gdn_conv_w.shape[1:]))

    zeros = lambda n: jnp.zeros((n, d), f32)
    h0 = jnp.concatenate([zeros(FRONT), meta_full, x[0], zeros(rows - used)], axis=0)
    tgt = jnp.concatenate([zeros(lo_frames), loss_target[0], zeros(rows - used)], axis=0)
    pad8 = lambda t: jnp.pad(t, ((0, 0), (0, LANES - t.shape[1])))
    a_log8, dt_bias8 = pad8(gdn_a_log), pad8(gdn_dt_bias)
    r_i = jnp.arange(tp)
    ltri = ((r_i[:, None] >= r_i[None, :]) & (r_i[:, None] // CHUNK == r_i[None, :] // CHUNK)).astype(f32)
    ffn_g = [ffn_norm_g[l:l + 1] for l in range(n_ffn)]
    kv_g, fin_g = kv_norm_g.reshape(1, d), final_norm_g.reshape(1, d)

    n0 = _rms_fwd(h0, gdn_g_full, "gdn_norm")
    proj = _matmul(n0, w_in_ext, "nn", "gdn_proj")
    gq, gk, gv, gw, bw = _gdn_prep_fwd(proj, conv_full, a_log8, dt_bias8, ltri, FRONT, used, tp)
    g_o, g_states = _gdn_fwd(gq, gk, gv, gw, bw)
    og = _gdn_gate_fwd(g_o, proj, gdn_onorm_g, tm)
    h1 = _matmul(og, w_out, "nn", "gdn_out", res=h0)

    def ffn_fwd(h, l):
        n = _rms_fwd(h, ffn_g[l], f"ffn{l}_norm")
        gu = _matmul(n, w_gu[l], "nn", f"ffn{l}_gate_up")
        act = _swiglu_fwd(gu, f"ffn{l}_act")
        return _matmul(act, w_dn[l], "nn", f"ffn{l}_down", res=h), (n, gu, act)

    h2, ffn0_saved = ffn_fwd(h1, 0)
    n_kv = _rms_fwd(h2, kv_g, "kv_norm")
    kv = jnp.concatenate([_matmul(n_kv, w_k, "nn", "k_proj", out_dtype=bf16),
                          _matmul(n_kv, w_v, "nn", "v_proj", out_dtype=bf16)], axis=1)
    n_sb = _rms_fwd(h2, sb_norm_g, "sb_norm")
    sq = _matmul(n_sb, w_q, "nn", "q_proj", out_dtype=bf16)
    s_o = _sb_fwd(sq, kv, sb_width)
    h3 = _matmul(s_o, w_o, "nn", "sb_out", res=h2)
    h4, ffn1_saved = ffn_fwd(h3, 1)
    dh4, d_fin_g, loss_part = _loss_head(h4, fin_g, tgt, lo_frames, used, "loss_head")

    def ffn_bwd(dh, h, l, saved):
        n, gu, act = saved
        d_wdn = _matmul(act, dh, "tn", f"ffn{l}_d_w_down")
        dact = _matmul(dh, w_dn[l], "nt", f"ffn{l}_d_act")
        dgu = _swiglu_bwd(gu, dact, f"ffn{l}_d_gate_up")
        d_wgu = _matmul(n, dgu, "tn", f"ffn{l}_d_w_gate_up")
        dn = _matmul(dgu, w_gu[l], "nt", f"ffn{l}_d_norm")
        dh_in, dg = _rms_bwd(h, ffn_g[l], dn, dh, f"ffn{l}_norm_bwd")
        return dh_in, d_wgu, d_wdn, dg

    dh3, d_wgu1, d_wdn1, d_ffn_g1 = ffn_bwd(dh4, h3, 1, ffn1_saved)
    d_wo = _matmul(s_o, dh3, "tn", "d_w_o")
    d_so = _matmul(dh3, w_o, "nt", "d_sb_o")
    d_sq, d_sk, d_sv = _sb_bwd(sq, kv, d_so, sb_width)
    d_wq = _matmul(n_sb, d_sq, "tn", "d_w_q")
    dh2, d_sb_g = _rms_bwd(h2, sb_norm_g, _matmul(d_sq, w_q, "nt", "d_sb_norm"), dh3, "sb_norm_bwd")
    d_wkv = jnp.concatenate([_matmul(n_kv, d_sk, "tn", "d_w_k"), _matmul(n_kv, d_sv, "tn", "d_w_v")], axis=1)
    dn_kv = _matmul(d_sv, w_v, "nt", "d_kv_norm_v", res=_matmul(d_sk, w_k, "nt", "d_kv_norm_k"))
    dh2, d_kv_g = _rms_bwd(h2, kv_g, dn_kv, dh2, "kv_norm_bwd")
    dh1, d_wgu0, d_wdn0, d_ffn_g0 = ffn_bwd(dh2, h1, 0, ffn0_saved)
    d_wout = _matmul(og, dh1, "tn", "d_w_out")
    d_og = _matmul(dh1, w_out, "nt", "d_gdn_gated")
    d_go, d_gate, d_onorm = _gdn_gate_bwd(g_o, proj, gdn_onorm_g, d_og, tm)
    d_gq, d_gk, d_gv, d_gw, d_bw = _gdn_bwd(gq, gk, gv, gw, bw, g_states, d_go)
    dconv, d_a_in, d_b_in, d_a_log8, d_dt_bias8 = _gdn_prep_bwd_act(
        proj, conv_full, a_log8, dt_bias8, ltri, d_gq, d_gk, d_gv, d_gw, d_bw, FRONT, used, tp)
    dproj, d_conv = _gdn_prep_bwd_conv(proj, conv_full, dconv, d_gate, d_a_in, d_b_in, tp)
    d_win_ext = _matmul(n0, dproj, "tn", "d_w_in")
    dh0, d_gdn_g = _rms_bwd(h0, gdn_g_full, _matmul(dproj, w_in_ext, "nt", "d_gdn_norm"), dh1, "gdn_norm_bwd")
    grad_x = dh0[lo_frames:used][None]
    d_win = jnp.concatenate([d_win_ext[:, :4 * GDN_WIDTH], d_win_ext[:, 4 * GDN_WIDTH:4 * GDN_WIDTH + GDN_HEADS],
                             d_win_ext[:, 4 * GDN_WIDTH + LANES:4 * GDN_WIDTH + LANES + GDN_HEADS]], axis=1)

    by_chip = [_split_cols(d_win), _split_rows(d_wout), _split_cols(d_wkv), _split_rows(d_wq), _split_rows(d_wo),
               jnp.stack([_split_cols(d_wgu0), _split_cols(d_wgu1)], axis=1),
               jnp.stack([_split_rows(d_wdn0), _split_rows(d_wdn1)], axis=1)]
    g_rows = -(-n_big // (512 * PACK_COLS)) * 512
    flat = jnp.concatenate([t.reshape(N_CHIPS, -1) for t in by_chip], axis=1)
    g_all = jnp.pad(flat, ((0, 0), (0, g_rows * PACK_COLS - n_big))).reshape(N_CHIPS, g_rows, PACK_COLS)
    own = lax.dynamic_index_in_dim(g_all, chip, 0, keepdims=False)
    got = _scatter_chips(g_all.astype(bf16))
    over_chips = _sum_chips(own, got)
    g_sum = _add(over_chips, _swap_sibling(over_chips), "sum_cores")
    g_big = _unpack(g_sum, [t.shape for t in big])

    small_parts = [dh0[FRONT:lo_frames], d_gdn_g, d_conv, d_a_log8, d_dt_bias8, d_onorm, d_kv_g, d_sb_g,
                   d_ffn_g0, d_ffn_g1, d_fin_g, loss_part]
    s_rows = -(-sum(t.size for t in small_parts) // (8 * PACK_COLS)) * 8
    s_sum = _sum_devices(_gather_all(_pack(small_parts, s_rows, f32)), s_rows)
    (g_meta, g_gdn_g, g_conv, g_a_log8, g_dt8, g_onorm, g_kv_g, g_sb_g, g_ffn_g0, g_ffn_g1, g_fin_g,
     loss_v) = _unpack(s_sum, [t.shape for t in small_parts])
    col_shard = lambda t, w: lax.dynamic_slice_in_dim(t, chip * w, w, axis=t.ndim - 1)

    grads = dict(
        meta_tokens=col_shard(g_meta, meta_tokens.shape[1]), gdn_norm_g=col_shard(g_gdn_g, gdn_norm_g.shape[1]),
        gdn_w_in=g_big[0][None], gdn_conv_w=col_shard(g_conv, gdn_conv_w.shape[2])[None],
        gdn_a_log=g_a_log8[:, :GDN_HEADS], gdn_dt_bias=g_dt8[:, :GDN_HEADS], gdn_onorm_g=g_onorm,
        gdn_w_out=g_big[1][None], kv_norm_g=g_kv_g.reshape(-1), w_kv=g_big[2], sb_norm_g=g_sb_g,
        sb_w_q=g_big[3][None], sb_w_o=g_big[4][None], ffn_norm_g=jnp.concatenate([g_ffn_g0, g_ffn_g1], axis=0),
        ffn_w_gate_up=g_big[5], ffn_w_down=g_big[6], final_norm_g=g_fin_g.reshape(-1))

    delta, new_m, new_v = {}, {}, {}
    for n in names:
        delta[n], new_m[n], new_v[n] = _adamw(weights[n], grads[n], m_in[n], v_in[n], f"adamw_{n}")
    loss = loss_v[0, 0]
    return (loss, grad_x, *[grads[n] for n in names], *[delta[n] for n in names],
            *[new_m[n] for n in names], *[new_v[n] for n in names])
```

```python
import functools

import jax
import jax.numpy as jnp
from jax import lax
from jax.experimental import pallas as pl
from jax.experimental.pallas import tpu as pltpu

f32 = jnp.float32
bf16 = jnp.bfloat16
i32 = jnp.int32

EPS = 1e-6
N_META = 16
CHUNK = 64
FRONT = (-N_META) % CHUNK
GDN_HEADS = 8
GDN_DIM = 128
GDN_WIDTH = GDN_HEADS * GDN_DIM
GDN_GROUP = 8
CONV_WIDTH = 4
SB_DIM = 64
SB_BLOCK = 128
SB_FWD_HEADS = 4
SB_UNDERFLOW = 104.0
LANES = 128
PACK_COLS = 1024
N_CHIPS = 4
N_DEV = 8
ADAM_LR, ADAM_B1, ADAM_B2, ADAM_EPS, ADAM_WD, ADAM_STEP = 0.001, 0.9, 0.999, 1e-08, 0.01, 10
VMEM_LIMIT = 56 * 1024 * 1024
MESH = pl.DeviceIdType.MESH


def _pick(n, prefs):
    for p in prefs:
        if n % p == 0:
            return p
    return n


def _params(sem):
    return pltpu.CompilerParams(dimension_semantics=sem, vmem_limit_bytes=VMEM_LIMIT)


_DIMS = {"nn": ((1,), (0,)), "nt": ((1,), (1,)), "tn": ((0,), (0,))}


def _bdot(a, b, mode):
    return lax.dot_general(a.astype(bf16), b.astype(bf16), (_DIMS[mode], ((), ())), preferred_element_type=f32)


def _matmul(a, b, mode, name, res=None, out_dtype=f32):
    if mode == "nn":
        (m, k), n = a.shape, b.shape[1]
    elif mode == "nt":
        (m, k), n = a.shape, b.shape[0]
    else:
        (k, m), n = a.shape, b.shape[1]
    tm = _pick(m, (640, 1408, 1024, 512, 384, 256, 128))
    tn = _pick(n, (1408, 2176, 1024, 512, 384, 256, 128))
    tk = _pick(k, (1408, 2176, 1024, 640, 512, 384, 256, 128))
    nk = k // tk
    a_spec = pl.BlockSpec((tk, tm), lambda i, j, kk: (kk, i)) if mode == "tn" else pl.BlockSpec((tm, tk), lambda i, j, kk: (i, kk))
    b_spec = pl.BlockSpec((tn, tk), lambda i, j, kk: (j, kk)) if mode == "nt" else pl.BlockSpec((tk, tn), lambda i, j, kk: (kk, j))
    o_spec = pl.BlockSpec((tm, tn), lambda i, j, kk: (i, j))
    has_res = res is not None

    def body(*refs):
        if has_res:
            a_ref, b_ref, r_ref, o_ref, acc = refs
        else:
            a_ref, b_ref, o_ref, acc = refs
        kk = pl.program_id(2)

        @pl.when(kk == 0)
        def _():
            acc[...] = jnp.zeros_like(acc)

        acc[...] += _bdot(a_ref[...], b_ref[...], mode)

        @pl.when(kk == nk - 1)
        def _():
            y = acc[...]
            if has_res:
                y = y + r_ref[...]
            o_ref[...] = y.astype(o_ref.dtype)

    ins = [a, b] + ([res] if has_res else [])
    specs = [a_spec, b_spec] + ([o_spec] if has_res else [])
    return pl.pallas_call(
        body, name=name, out_shape=jax.ShapeDtypeStruct((m, n), out_dtype), grid=(m // tm, n // tn, nk),
        in_specs=specs, out_specs=o_spec, scratch_shapes=[pltpu.VMEM((tm, tn), f32)],
        compiler_params=_params(("parallel", "parallel", "arbitrary")),
    )(*ins)


def _rowwise(fn, name, rows, tm, ins, outs, reds=()):
    n_in, n_out, n_red = len(ins), len(outs), len(reds)
    in_specs = []
    for arr, spec in ins:
        if spec is None:
            in_specs.append(pl.BlockSpec(arr.shape, lambda i, nd=arr.ndim: (0,) * nd))
        else:
            w, cb = spec
            in_specs.append(pl.BlockSpec((tm, w), lambda i, cb=cb: (i, cb)))
    out_specs = [pl.BlockSpec((tm, w), lambda i: (i, 0)) for w, _ in outs]
    out_specs += [pl.BlockSpec(s, lambda i, nd=len(s): (0,) * nd) for s in reds]
    out_shape = [jax.ShapeDtypeStruct((rows, w), dt) for w, dt in outs]
    out_shape += [jax.ShapeDtypeStruct(s, f32) for s in reds]

    def body(*refs):
        i = pl.program_id(0)
        vals = fn(i, *[r[...] for r in refs[:n_in]])
        for r, v in zip(refs[n_in:n_in + n_out], vals[:n_out]):
            r[...] = v.astype(r.dtype)
        red_refs = refs[n_in + n_out:]

        @pl.when(i == 0)
        def _():
            for r in red_refs:
                r[...] = jnp.zeros_like(r)

        for r, v in zip(red_refs, vals[n_out:]):
            r[...] += v

    res = pl.pallas_call(
        body, name=name, out_shape=out_shape, grid=(rows // tm,), in_specs=in_specs, out_specs=out_specs,
        compiler_params=_params(("arbitrary",)),
    )(*[a for a, _ in ins])
    return res


def _rms(x, g):
    return x * lax.rsqrt(jnp.mean(x * x, axis=-1, keepdims=True) + EPS) * g


def _row_mask(i, tm, lo, hi, shape):
    r = i * tm + lax.broadcasted_iota(i32, shape, 0)
    return (r >= lo) & (r < hi)


def _rms_fwd(x, g, name):
    rows, d = x.shape
    tm = _pick(rows, (640, 512, 384, 256, 128))
    return _rowwise(lambda i, xv, gv: (_rms(xv, gv),), name, rows, tm, [(x, (d, 0)), (g, None)], [(d, bf16)])[0]


def _rms_bwd(x, g, dn, res, name):
    rows, d = x.shape
    tm = _pick(rows, (640, 512, 384, 256, 128))

    def fn(i, xv, gv, dnv, rv):
        _, vjp = jax.vjp(_rms, xv, gv)
        dx, dg = vjp(dnv)
        return rv + dx, dg

    return _rowwise(fn, name, rows, tm, [(x, (d, 0)), (g, None), (dn, (d, 0)), (res, (d, 0))], [(d, f32)], [(1, d)])


def _swiglu(gate, up):
    return jax.nn.silu(gate) * up


def _swiglu_fwd(gu, name):
    rows, f2 = gu.shape
    f = f2 // 2
    tm = _pick(rows, (128,))
    return _rowwise(lambda i, a, b: (_swiglu(a, b),), name, rows, tm, [(gu, (f, 0)), (gu, (f, 1))], [(f, bf16)])[0]


def _swiglu_bwd(gu, dact, name):
    rows, f2 = gu.shape
    f = f2 // 2
    tm = _pick(rows, (128,))

    def fn(i, a, b, dv):
        _, vjp = jax.vjp(_swiglu, a, b)
        da, db = vjp(dv)
        return (jnp.concatenate([da, db], axis=1),)

    return _rowwise(fn, name, rows, tm, [(gu, (f, 0)), (gu, (f, 1)), (dact, (f, 0))], [(f2, bf16)])[0]


def _loss_head(h, g, tgt, lo, hi, name):
    rows, d = h.shape
    tm = _pick(rows, (640, 512, 384, 256, 128))

    def fn(i, hv, gv, tv):
        mask = _row_mask(i, tm, lo, hi, (tm, 1))

        def f(hh, gg):
            err = _rms(hh, gg) - tv
            per_row = jnp.where(mask, jnp.mean(err * err, axis=-1, keepdims=True), 0.0)
            return 0.5 * jnp.sum(per_row, axis=0, keepdims=True)

        loss, vjp = jax.vjp(f, hv, gv)
        dh, dg = vjp(jnp.ones_like(loss))
        return dh, dg, jnp.broadcast_to(loss, (1, LANES))

    return _rowwise(fn, name, rows, tm, [(h, (d, 0)), (g, None), (tgt, (d, 0))], [(d, f32)], [(1, d), (1, LANES)])


def _heads_l2(x):
    t = x.shape[0]
    x3 = x.reshape(t, GDN_HEADS, GDN_DIM)
    return (x3 * lax.rsqrt(jnp.sum(x3 * x3, axis=-1, keepdims=True) + EPS)).reshape(t, GDN_WIDTH)


def _gdn_act(conv, a_in, b_in, a_log, dt_bias, mask):
    s = jax.nn.silu(conv)
    q = _heads_l2(s[:, :GDN_WIDTH])
    k = _heads_l2(s[:, GDN_WIDTH:2 * GDN_WIDTH])
    v = s[:, 2 * GDN_WIDTH:]
    g = jnp.where(mask, -jnp.exp(a_log) * jax.nn.softplus(a_in + dt_bias), 0.0)
    beta = jnp.where(mask, jax.nn.sigmoid(b_in), 0.0)
    return q, k, v, g, beta


def _widen(x8):
    return jnp.concatenate([jnp.broadcast_to(x8[:, h:h + 1], (x8.shape[0], GDN_DIM)) for h in range(GDN_HEADS)], axis=1)


def _narrow(xw):
    t = xw.shape[0]
    lane = lax.broadcasted_iota(i32, (t, LANES), 1)
    out = jnp.zeros((t, LANES), f32)
    for h in range(GDN_HEADS):
        s = jnp.sum(xw[:, h * GDN_DIM:(h + 1) * GDN_DIM], axis=1, keepdims=True)
        out = out + jnp.where(lane == h, s, 0.0)
    return out


def _conv_taps(cur, prev8, w):
    tm = cur.shape[0]
    cat = jnp.concatenate([prev8, cur], axis=0)
    y = cur * w[CONV_WIDTH - 1:CONV_WIDTH, :]
    for j in range(1, CONV_WIDTH):
        y = y + pltpu.roll(cat, j, axis=0)[8:8 + tm, :] * w[CONV_WIDTH - 1 - j:CONV_WIDTH - j, :]
    return y


def _gdn_prep_specs(proj, tm):
    c3 = 3 * GDN_WIDTH
    ab = 4 * GDN_WIDTH // LANES
    t8 = tm // 8
    return [
        pl.BlockSpec((tm, c3), lambda i: (i, 0)),
        pl.BlockSpec((8, c3), lambda i: (jnp.maximum(i * t8 - 1, 0), 0)),
        pl.BlockSpec((tm, LANES), lambda i: (i, ab)),
        pl.BlockSpec((tm, LANES), lambda i: (i, ab + 1)),
    ]


def _full(arr):
    return pl.BlockSpec(arr.shape, lambda i, nd=arr.ndim: (0,) * nd)


def _gdn_prep_fwd(proj, conv_w, a_log, dt_bias, ltri, lo, hi, tm):
    rows = proj.shape[0]

    def body(cur, prev8, a_in, b_in, w, al, dtb, lt, q_o, k_o, v_o, g_o, b_o):
        i = pl.program_id(0)
        mask = _row_mask(i, tm, lo, hi, (tm, LANES)) & (lax.broadcasted_iota(i32, (tm, LANES), 1) < GDN_HEADS)
        conv = _conv_taps(cur[...], prev8[...], w[...])
        q, k, v, g, beta = _gdn_act(conv, a_in[...], b_in[...], al[...], dtb[...], mask)
        q_o[...] = q
        k_o[...] = k
        v_o[...] = v
        gcum = jnp.dot(lt[...], g, preferred_element_type=f32, precision=lax.Precision.HIGHEST)
        g_o[...] = _widen(gcum)
        b_o[...] = _widen(beta)

    wide = jax.ShapeDtypeStruct((rows, GDN_WIDTH), f32)
    o_spec = pl.BlockSpec((tm, GDN_WIDTH), lambda i: (i, 0))
    return pl.pallas_call(
        body, name="gdn_prep_fwd", out_shape=[wide] * 5, grid=(rows // tm,),
        in_specs=_gdn_prep_specs(proj, tm) + [_full(conv_w), _full(a_log), _full(dt_bias), _full(ltri)],
        out_specs=[o_spec] * 5, compiler_params=_params(("parallel",)),
    )(proj, proj, proj, proj, conv_w, a_log, dt_bias, ltri)


def _gdn_prep_bwd_act(proj, conv_w, a_log, dt_bias, ltri, dq, dk, dv, dgw, dbw, lo, hi, tm):
    rows = proj.shape[0]
    c3 = 3 * GDN_WIDTH

    def body(cur, prev8, a_in, b_in, w, al, dtb, lt, dq_r, dk_r, dv_r, dg_r, db_r, dconv_o, da_o, dbin_o, dal_o, ddt_o):
        i = pl.program_id(0)
        mask = _row_mask(i, tm, lo, hi, (tm, LANES)) & (lax.broadcasted_iota(i32, (tm, LANES), 1) < GDN_HEADS)
        conv = _conv_taps(cur[...], prev8[...], w[...])
        dgcum = _narrow(dg_r[...])
        dg = lax.dot_general(lt[...], dgcum, (((0,), (0,)), ((), ())), preferred_element_type=f32,
                             precision=lax.Precision.HIGHEST)
        dbeta = _narrow(db_r[...])
        _, vjp = jax.vjp(lambda c, a, b, x, y: _gdn_act(c, a, b, x, y, mask), conv, a_in[...], b_in[...], al[...], dtb[...])
        dconv, da, dbin, dal, ddt = vjp((dq_r[...], dk_r[...], dv_r[...], dg, dbeta))
        dconv_o[...] = dconv
        da_o[...] = da
        dbin_o[...] = dbin

        @pl.when(i == 0)
        def _():
            dal_o[...] = jnp.zeros_like(dal_o)
            ddt_o[...] = jnp.zeros_like(ddt_o)

        dal_o[...] += dal
        ddt_o[...] += ddt

    w_spec = pl.BlockSpec((tm, GDN_WIDTH), lambda i: (i, 0))
    n_spec = pl.BlockSpec((tm, LANES), lambda i: (i, 0))
    s_spec = pl.BlockSpec((1, LANES), lambda i: (0, 0))
    return pl.pallas_call(
        body, name="gdn_prep_bwd_act",
        out_shape=[jax.ShapeDtypeStruct((rows, c3), f32), jax.ShapeDtypeStruct((rows, LANES), f32),
                   jax.ShapeDtypeStruct((rows, LANES), f32), jax.ShapeDtypeStruct((1, LANES), f32),
                   jax.ShapeDtypeStruct((1, LANES), f32)],
        grid=(rows // tm,),
        in_specs=_gdn_prep_specs(proj, tm) + [_full(conv_w), _full(a_log), _full(dt_bias), _full(ltri)] + [w_spec] * 5,
        out_specs=[pl.BlockSpec((tm, c3), lambda i: (i, 0)), n_spec, n_spec, s_spec, s_spec],
        compiler_params=_params(("arbitrary",)),
    )(proj, proj, proj, proj, conv_w, a_log, dt_bias, ltri, dq, dk, dv, dgw, dbw)


def _gdn_prep_bwd_conv(proj, conv_w, dconv, dgate, da, dbin, tm):
    rows, width = proj.shape
    c3 = 3 * GDN_WIDTH
    t8 = tm // 8
    nt = rows // tm

    def body(cur, prev8, w, dc, dnext8, dgt, da_r, db_r, dp_o, dw_o):
        i = pl.program_id(0)
        d = dc[...]
        nxt = jnp.where(i == nt - 1, 0.0, dnext8[...])
        cat = jnp.concatenate([d, nxt], axis=0)
        wv = w[...]
        dx = d * wv[CONV_WIDTH - 1:CONV_WIDTH, :]
        for j in range(1, CONV_WIDTH):
            dx = dx + pltpu.roll(cat, tm + 8 - j, axis=0)[:tm, :] * wv[CONV_WIDTH - 1 - j:CONV_WIDTH - j, :]
        dp_o[:, :c3] = dx.astype(bf16)
        dp_o[:, c3:4 * GDN_WIDTH] = dgt[...].astype(bf16)
        dp_o[:, 4 * GDN_WIDTH:4 * GDN_WIDTH + LANES] = da_r[...].astype(bf16)
        dp_o[:, 4 * GDN_WIDTH + LANES:] = db_r[...].astype(bf16)

        xcat = jnp.concatenate([prev8[...], cur[...]], axis=0)
        parts = [jnp.sum(d * cur[...], axis=0, keepdims=True)]
        for j in range(1, CONV_WIDTH):
            parts.append(jnp.sum(d * pltpu.roll(xcat, j, axis=0)[8:8 + tm, :], axis=0, keepdims=True))
        dwt = jnp.concatenate(parts[::-1], axis=0)

        @pl.when(i == 0)
        def _():
            dw_o[...] = jnp.zeros_like(dw_o)

        dw_o[...] += dwt

    n_spec = pl.BlockSpec((tm, LANES), lambda i: (i, 0))
    return pl.pallas_call(
        body, name="gdn_prep_bwd_conv",
        out_shape=[jax.ShapeDtypeStruct((rows, width), bf16), jax.ShapeDtypeStruct((CONV_WIDTH, c3), f32)],
        grid=(nt,),
        in_specs=[pl.BlockSpec((tm, c3), lambda i: (i, 0)),
                  pl.BlockSpec((8, c3), lambda i: (jnp.maximum(i * t8 - 1, 0), 0)),
                  _full(conv_w),
                  pl.BlockSpec((tm, c3), lambda i: (i, 0)),
                  pl.BlockSpec((8, c3), lambda i: (jnp.minimum((i + 1) * t8, rows // 8 - 1), 0)),
                  pl.BlockSpec((tm, GDN_WIDTH), lambda i: (i, 0)), n_spec, n_spec],
        out_specs=[pl.BlockSpec((tm, width), lambda i: (i, 0)), pl.BlockSpec((CONV_WIDTH, c3), lambda i: (0, 0))],
        compiler_params=_params(("arbitrary",)),
    )(proj, proj, conv_w, dconv, dconv, dgate, da, dbin)


def _split(a):
    hi = a.astype(bf16)
    return hi, (a - hi.astype(f32)).astype(bf16)


def _dot3(a, b, mode):
    ah, al = _split(a)
    bh, bl = _split(b)
    d = lambda x, y: lax.dot_general(x, y, (_DIMS[mode], ((), ())), preferred_element_type=f32)
    return d(ah, bh) + (d(ah, bl) + d(al, bh))


def _make_mm(dot):
    @jax.custom_vjp
    def nn(a, b):
        return dot(a, b, "nn")

    nn.defvjp(lambda a, b: (dot(a, b, "nn"), (a, b)),
              lambda r, ct: (dot(ct, r[1], "nt"), dot(r[0], ct, "tn")))

    @jax.custom_vjp
    def nt(a, b):
        return dot(a, b, "nt")

    nt.defvjp(lambda a, b: (dot(a, b, "nt"), (a, b)),
              lambda r, ct: (dot(ct, r[1], "nn"), dot(ct, r[0], "tn")))

    @jax.custom_vjp
    def tn(a, b):
        return dot(a, b, "tn")

    tn.defvjp(lambda a, b: (dot(a, b, "tn"), (a, b)),
              lambda r, ct: (dot(r[1], ct, "nt"), dot(r[0], ct, "nn")))
    return nn, nt, tn


_mm, _mm_nt, _mm_tn = _make_mm(_bdot)
_mm3, _, _ = _make_mm(_dot3)


def _each(f, *lists):
    return [f(*xs) for xs in zip(*lists)]


def _gdn_chunk(q, k, v, gcb, bcb, s_in):
    c = q[0].shape[0]
    ri = lax.broadcasted_iota(i32, (c, c), 0)
    ci = lax.broadcasted_iota(i32, (c, c), 1)
    incl, strict = ri >= ci, ri > ci
    rowi = lax.broadcasted_iota(i32, gcb[0].shape, 0)
    qs = _each(lambda t: t * (GDN_DIM ** -0.5), q)
    decay = _each(lambda g: jnp.where(incl, jnp.exp(jnp.where(incl, g[:, :c] - g[:, :c].T, 0.0)), 0.0), gcb)
    kk = _each(lambda t: _mm_nt(t, t), k)
    a1 = _each(lambda b, d, t: jnp.where(strict, b[:, :c] * d * t, 0.0), bcb, decay, kk)
    eg = _each(jnp.exp, gcb)
    x = _each(lambda b, vv, e, t: jnp.concatenate([b * vv, (b * e) * t], axis=1), bcb, v, eg, k)
    pows = [a1]
    for _ in range(5):
        pows.append(_each(lambda p: _mm3(p, p), pows[-1]))
    for ps in pows[:0:-1]:
        x = _each(lambda p, t: t + _mm3(p, t), ps, x)
    x = _each(lambda p, t: t - _mm3(p, t), a1, x)
    attn = _each(lambda a, b, d: _mm_nt(a, b) * d, qs, k, decay)
    glast = _each(lambda g: jnp.sum(jnp.where(rowi == c - 1, g, 0.0), axis=0, keepdims=True), gcb)
    u = _each(lambda t, s: t[:, :GDN_DIM] - _mm(t[:, GDN_DIM:], s), x, s_in)
    o = _each(lambda a, e, s, w, uu: _mm(a * e, s) + _mm(w, uu), qs, eg, s_in, attn, u)
    s_out = _each(lambda s, gl, t, g, uu: s * jnp.exp(gl) + _mm_tn(t * jnp.exp(gl - g), uu), s_in, glast, k, gcb, u)
    return o, s_out


def _gdn_fwd(q, k, v, gw, bw):
    rows = q.shape[0]
    nc = rows // CHUNK
    blk = pl.BlockSpec((CHUNK, GDN_GROUP * GDN_DIM), lambda g, c: (c, g))

    def body(q_r, k_r, v_r, g_r, b_r, o_r, st_r, s_sc):
        @pl.when(pl.program_id(1) == 0)
        def _():
            s_sc[...] = jnp.zeros_like(s_sc)

        heads = lambda r: [r[:, h * GDN_DIM:(h + 1) * GDN_DIM] for h in range(GDN_GROUP)]
        s_in = [s_sc[h] for h in range(GDN_GROUP)]
        st_r[0] = s_sc[...]
        o, s_out = _gdn_chunk(heads(q_r), heads(k_r), heads(v_r), heads(g_r), heads(b_r), s_in)
        o_r[...] = jnp.concatenate(o, axis=1)
        for h in range(GDN_GROUP):
            s_sc[h] = s_out[h]

    return pl.pallas_call(
        body, name="gdn_fwd",
        out_shape=[jax.ShapeDtypeStruct((rows, GDN_WIDTH), f32), jax.ShapeDtypeStruct((nc, GDN_HEADS, GDN_DIM, GDN_DIM), f32)],
        grid=(GDN_HEADS // GDN_GROUP, nc), in_specs=[blk] * 5,
        out_specs=[blk, pl.BlockSpec((1, GDN_GROUP, GDN_DIM, GDN_DIM), lambda g, c: (c, g, 0, 0))],
        scratch_shapes=[pltpu.VMEM((GDN_GROUP, GDN_DIM, GDN_DIM), f32)],
        compiler_params=_params(("parallel", "arbitrary")),
    )(q, k, v, gw, bw)


def _gdn_bwd(q, k, v, gw, bw, states, do):
    rows = q.shape[0]
    nc = rows // CHUNK
    blk = pl.BlockSpec((CHUNK, GDN_GROUP * GDN_DIM), lambda g, c: (nc - 1 - c, g))

    def body(q_r, k_r, v_r, g_r, b_r, st_r, do_r, dq_r, dk_r, dv_r, dg_r, db_r, ds_sc):
        @pl.when(pl.program_id(1) == 0)
        def _():
            ds_sc[...] = jnp.zeros_like(ds_sc)

        heads = lambda r: [r[:, h * GDN_DIM:(h + 1) * GDN_DIM] for h in range(GDN_GROUP)]
        s_in = [st_r[0, h] for h in range(GDN_GROUP)]
        _, vjp = jax.vjp(_gdn_chunk, heads(q_r), heads(k_r), heads(v_r), heads(g_r), heads(b_r), s_in)
        dq, dk, dv, dg, db, ds_in = vjp((heads(do_r), [ds_sc[h] for h in range(GDN_GROUP)]))
        dq_r[...] = jnp.concatenate(dq, axis=1)
        dk_r[...] = jnp.concatenate(dk, axis=1)
        dv_r[...] = jnp.concatenate(dv, axis=1)
        dg_r[...] = jnp.concatenate(dg, axis=1)
        db_r[...] = jnp.concatenate(db, axis=1)
        for h in range(GDN_GROUP):
            ds_sc[h] = ds_in[h]

    wide = jax.ShapeDtypeStruct((rows, GDN_WIDTH), f32)
    return pl.pallas_call(
        body, name="gdn_bwd", out_shape=[wide] * 5, grid=(GDN_HEADS // GDN_GROUP, nc),
        in_specs=[blk] * 5 + [pl.BlockSpec((1, GDN_GROUP, GDN_DIM, GDN_DIM), lambda g, c: (nc - 1 - c, g, 0, 0)), blk],
        out_specs=[blk] * 5, scratch_shapes=[pltpu.VMEM((GDN_GROUP, GDN_DIM, GDN_DIM), f32)],
        compiler_params=_params(("parallel", "arbitrary")),
    )(q, k, v, gw, bw, states, do)


def _gdn_gate(o, gate, og):
    t = o.shape[0]
    o3 = o.reshape(t, GDN_HEADS, GDN_DIM)
    n = o3 * lax.rsqrt(jnp.mean(o3 * o3, axis=-1, keepdims=True) + EPS) * og.reshape(1, 1, GDN_DIM)
    return n.reshape(t, GDN_WIDTH) * jax.nn.silu(gate)


def _gdn_gate_fwd(o, proj, og, tm):
    rows = o.shape[0]
    return _rowwise(lambda i, ov, gv, w: (_gdn_gate(ov, gv, w),), "gdn_gate_fwd", rows, tm,
                    [(o, (GDN_WIDTH, 0)), (proj, (GDN_WIDTH, 3)), (og, None)], [(GDN_WIDTH, bf16)])[0]


def _gdn_gate_bwd(o, proj, og, dy, tm):
    rows = o.shape[0]

    def fn(i, ov, gv, w, d):
        _, vjp = jax.vjp(_gdn_gate, ov, gv, w)
        return vjp(d)

    return _rowwise(fn, "gdn_gate_bwd", rows, tm,
                    [(o, (GDN_WIDTH, 0)), (proj, (GDN_WIDTH, 3)), (og, None), (dy, (GDN_WIDTH, 0))],
                    [(GDN_WIDTH, f32), (GDN_WIDTH, f32)], [(1, GDN_DIM)])


def _sb_visible(i, j):
    qpos = i * SB_BLOCK + lax.broadcasted_iota(i32, (SB_BLOCK, SB_BLOCK), 0)
    kpos = j * SB_BLOCK + lax.broadcasted_iota(i32, (SB_BLOCK, SB_BLOCK), 1)
    return (kpos < qpos) & (kpos >= FRONT)


def _sb_logs(z, vis):
    l1p = jnp.log1p(jnp.exp(-jnp.abs(z)))
    return -(jnp.maximum(-z, 0.0) + l1p), jnp.where(vis, -(jnp.maximum(z, 0.0) + l1p), 0.0)


def _tri_sum(x, tri):
    hi, lo = _split(x)
    return jnp.dot(hi, tri, preferred_element_type=f32) + jnp.dot(lo, tri, preferred_element_type=f32)


def _sb_live(t, i, run):
    return (t <= i) & (jnp.max(run) > -SB_UNDERFLOW)


def _sb_fwd(q, kv, width):
    rows = q.shape[0]
    nq = rows // SB_BLOCK
    lanes = SB_FWD_HEADS * SB_DIM
    npair = width // lanes
    scale = SB_DIM ** -0.5

    def body(q_r, k_r, v_r, o_r):
        i = pl.program_id(1)
        rj = lax.broadcasted_iota(i32, (SB_BLOCK, SB_BLOCK), 0)
        cs = lax.broadcasted_iota(i32, (SB_BLOCK, SB_BLOCK), 1)
        after = (rj > cs).astype(bf16)
        sls = [slice(a * SB_DIM, (a + 1) * SB_DIM) for a in range(SB_FWD_HEADS)]
        qs = [q_r[:, sl] for sl in sls]

        def step(carry):
            t, accs, runs = carry
            j = i - t
            ks = pl.ds(pl.multiple_of(j * SB_BLOCK, SB_BLOCK), SB_BLOCK)
            vis = _sb_visible(i, j)
            z = _each(lambda qh, sl: _bdot(qh, k_r[ks, sl], "nt") * scale, qs, sls)
            logs = _each(lambda zz: _sb_logs(zz, vis), z)
            later = _each(lambda l: _tri_sum(l[1], after), logs)
            w = _each(lambda l, s, run: jnp.where(vis, jnp.exp(l[0] + s + run), 0.0), logs, later, runs)
            accs = _each(lambda acc, ww, sl: acc + _bdot(ww, v_r[ks, sl], "nn"), accs, w, sls)
            runs = _each(lambda run, l: run + jnp.sum(l[1], axis=1, keepdims=True), runs, logs)
            return t + 1, tuple(accs), tuple(runs)

        init = (jnp.int32(0), tuple(jnp.zeros((SB_BLOCK, SB_DIM), f32) for _ in sls),
                tuple(jnp.zeros((SB_BLOCK, 1), f32) for _ in sls))
        _, accs, _ = lax.while_loop(lambda c: _sb_live(c[0], i, functools.reduce(jnp.maximum, c[2])), step, init)
        o_r[...] = jnp.concatenate(accs, axis=1)

    return pl.pallas_call(
        body, name="sb_fwd", out_shape=jax.ShapeDtypeStruct((rows, width), f32), grid=(npair, nq),
        in_specs=[pl.BlockSpec((SB_BLOCK, lanes), lambda p, i: (i, p)),
                  pl.BlockSpec((rows, lanes), lambda p, i: (0, p)),
                  pl.BlockSpec((rows, lanes), lambda p, i: (0, npair + p))],
        out_specs=pl.BlockSpec((SB_BLOCK, lanes), lambda p, i: (i, p)),
        compiler_params=_params(("parallel", "arbitrary")),
    )(q, kv, kv)


def _sb_bwd(q, kv, do, width):
    rows = q.shape[0]
    nq = rows // SB_BLOCK
    npair = width // LANES
    nh = LANES // SB_DIM
    scale = SB_DIM ** -0.5

    def body(q_r, k_r, v_r, do_r, dq_r, dk_r, dv_r, e_sc, sig_sc, w_sc):
        i = pl.program_id(1)

        @pl.when(i == 0)
        def _():
            dk_r[...] = jnp.zeros_like(dk_r)
            dv_r[...] = jnp.zeros_like(dv_r)

        rj = lax.broadcasted_iota(i32, (SB_BLOCK, SB_BLOCK), 0)
        cs = lax.broadcasted_iota(i32, (SB_BLOCK, SB_BLOCK), 1)
        after = (rj > cs).astype(bf16)
        from_s = (rj >= cs).astype(bf16)
        zero1 = jnp.zeros((SB_BLOCK, 1), f32)
        sls = [slice(a * SB_DIM, (a + 1) * SB_DIM) for a in range(nh)]
        qs = [q_r[:, sl] for sl in sls]
        dos = [do_r[:, sl] for sl in sls]

        def weigh(carry):
            t, runs, eruns = carry
            j = i - t
            ks = pl.ds(pl.multiple_of(j * SB_BLOCK, SB_BLOCK), SB_BLOCK)
            vis = _sb_visible(i, j)
            z = _each(lambda qh, sl: _bdot(qh, k_r[ks, sl], "nt") * scale, qs, sls)
            dw = _each(lambda doh, sl: _bdot(doh, v_r[ks, sl], "nt"), dos, sls)
            logs = _each(lambda zz: _sb_logs(zz, vis), z)
            later = _each(lambda l: _tri_sum(l[1], after), logs)
            w = _each(lambda l, s, run: jnp.where(vis, jnp.exp(l[0] + s + run), 0.0), logs, later, runs)
            e = _each(lambda ww, d: ww * d, w, dw)
            for a in range(nh):
                e_sc[a, t] = e[a]
                sig_sc[a, t] = jax.nn.sigmoid(z[a])
                w_sc[a, t] = w[a].astype(w_sc.dtype)
            runs = _each(lambda run, l: run + jnp.sum(l[1], axis=1, keepdims=True), runs, logs)
            eruns = _each(lambda erun, ee: erun + jnp.sum(ee, axis=1, keepdims=True), eruns, e)
            return t + 1, tuple(runs), tuple(eruns)

        n_blk, _, etots = lax.while_loop(lambda c: _sb_live(c[0], i, functools.reduce(jnp.maximum, c[1])), weigh,
                                         (jnp.int32(0), (zero1,) * nh, (zero1,) * nh))

        def push(t, carry):
            dqs, eruns = carry
            j = i - t
            ks = pl.ds(pl.multiple_of(j * SB_BLOCK, SB_BLOCK), SB_BLOCK)
            vis = (j * SB_BLOCK + cs < i * SB_BLOCK + rj) & (j * SB_BLOCK + cs >= FRONT)
            heads = list(range(nh))
            e = _each(lambda a: e_sc[a, t], heads)
            dvs = _each(lambda a, doh: _bdot(w_sc[a, t], doh, "tn"), heads, dos)
            upto = _each(lambda ee: _tri_sum(ee, from_s), e)
            dz = _each(lambda a, ee, u, erun, etot: jnp.where(
                vis, ee * (1.0 - sig_sc[a, t]) - (etot - erun - u) * sig_sc[a, t], 0.0) * scale, heads, e, upto, eruns, etots)
            dks = _each(lambda d, qh: _bdot(d, qh, "tn"), dz, qs)
            dqs = _each(lambda dq, d, sl: dq + _bdot(d, k_r[ks, sl], "nn"), dqs, dz, sls)
            eruns = _each(lambda erun, ee: erun + jnp.sum(ee, axis=1, keepdims=True), eruns, e)
            dk_r[ks, :] += jnp.concatenate(dks, axis=1)
            dv_r[ks, :] += jnp.concatenate(dvs, axis=1)
            return tuple(dqs), tuple(eruns)

        dqs, _ = lax.fori_loop(0, n_blk, push, (tuple(jnp.zeros((SB_BLOCK, SB_DIM), f32) for _ in sls), (zero1,) * nh))
        dq_r[...] = jnp.concatenate(dqs, axis=1)

    blk = pl.BlockSpec((SB_BLOCK, LANES), lambda p, i: (i, p))
    col = pl.BlockSpec((rows, LANES), lambda p, i: (0, p))
    wide = jax.ShapeDtypeStruct((rows, width), f32)
    return pl.pallas_call(
        body, name="sb_bwd", out_shape=[wide] * 3, grid=(npair, nq),
        in_specs=[blk, col, pl.BlockSpec((rows, LANES), lambda p, i: (0, npair + p)), blk],
        out_specs=[blk, col, col],
        scratch_shapes=[pltpu.VMEM((nh, nq, SB_BLOCK, SB_BLOCK), f32), pltpu.VMEM((nh, nq, SB_BLOCK, SB_BLOCK), f32),
                        pltpu.VMEM((nh, nq, SB_BLOCK, SB_BLOCK), bf16)],
        compiler_params=_params(("parallel", "arbitrary")),
    )(q, kv, kv, do)


_FLIPS = ((1, 0), (0, 1), (1, 1))
_ANY = pl.BlockSpec(memory_space=pl.ANY)


def _flip(v, a):
    return v + a - 2 * a * v


def _gather_chips(p):
    rows, cols = p.shape
    half = rows // 2

    def body(p_ref, o_ref, send_sems, recv_sems, local_sem):
        x, y, c = lax.axis_index("x"), lax.axis_index("y"), lax.axis_index("c")
        me, sibling = (x, y, c), (x, y, 1 - c)
        chip = 2 * x + y

        def piece(s, hc):
            return o_ref.at[s, pl.ds(pl.multiple_of(hc * half, 16), half), :]

        def copy(n, s, hc, to, src=None):
            return pltpu.make_async_remote_copy(
                src_ref=piece(s, hc) if src is None else src, dst_ref=piece(s, hc),
                send_sem=send_sems.at[n], recv_sem=recv_sems.at[n], device_id=to, device_id_type=MESH)

        mine = pltpu.make_async_copy(p_ref, o_ref.at[chip], local_sem)
        mine.start()
        others = [(_flip(x, a), _flip(y, b)) for a, b in _FLIPS]
        my_half = p_ref.at[pl.ds(pl.multiple_of(c * half, 16), half), :]
        first = [copy(n, chip, c, (ox, oy, c), src=my_half) for n, (ox, oy) in enumerate(others)]
        for cp in first:
            cp.start()
        passed = [copy(3 + n, 2 * ox + oy, c, sibling) for n, (ox, oy) in enumerate(others)]
        for n, (ox, oy) in enumerate(others):
            copy(n, 2 * ox + oy, c, me).wait_recv()
            passed[n].start()
        for n, (ox, oy) in enumerate(others):
            copy(3 + n, 2 * ox + oy, 1 - c, me).wait_recv()
        for cp in first + passed:
            cp.wait_send()
        mine.wait()

    return pl.pallas_call(
        body, name="gather_chips", out_shape=jax.ShapeDtypeStruct((N_CHIPS, rows, cols), p.dtype),
        in_specs=[_ANY], out_specs=_ANY,
        scratch_shapes=[pltpu.SemaphoreType.DMA((6,)), pltpu.SemaphoreType.DMA((6,)), pltpu.SemaphoreType.DMA],
    )(p)


def _scatter_chips(g):
    _, rows, cols = g.shape

    def body(g_ref, o_ref, send_sems, recv_sems):
        x, y, c = lax.axis_index("x"), lax.axis_index("y"), lax.axis_index("c")
        cps = []
        for n, (a, b) in enumerate(_FLIPS):
            ox, oy = _flip(x, a), _flip(y, b)
            cps.append(pltpu.make_async_remote_copy(
                src_ref=g_ref.at[2 * ox + oy], dst_ref=o_ref.at[n], send_sem=send_sems.at[n], recv_sem=recv_sems.at[n],
                device_id=(ox, oy, c), device_id_type=MESH))
        for cp in cps:
            cp.start()
        for cp in cps:
            cp.wait()

    return pl.pallas_call(
        body, name="scatter_chips", out_shape=jax.ShapeDtypeStruct((3, rows, cols), g.dtype),
        in_specs=[_ANY], out_specs=_ANY,
        scratch_shapes=[pltpu.SemaphoreType.DMA((3,)), pltpu.SemaphoreType.DMA((3,))],
    )(g)


def _swap_sibling(a):
    def body(a_ref, o_ref, send_sem, recv_sem):
        x, y, c = lax.axis_index("x"), lax.axis_index("y"), lax.axis_index("c")
        cp = pltpu.make_async_remote_copy(src_ref=a_ref, dst_ref=o_ref, send_sem=send_sem, recv_sem=recv_sem,
                                          device_id=(x, y, 1 - c), device_id_type=MESH)
        cp.start()
        cp.wait()

    return pl.pallas_call(
        body, name="swap_sibling", out_shape=jax.ShapeDtypeStruct(a.shape, a.dtype), in_specs=[_ANY], out_specs=_ANY,
        scratch_shapes=[pltpu.SemaphoreType.DMA, pltpu.SemaphoreType.DMA],
    )(a)


def _gather_all(v):
    m_per, n = v.shape

    def body(x_ref, out_ref, send_sems, recv_sems, local_sem):
        x, y, c = lax.axis_index("x"), lax.axis_index("y"), lax.axis_index("c")
        me, sibling = (x, y, c), (x, y, 1 - c)
        chips = [(_flip(x, a), _flip(y, b)) for a, b in _FLIPS]

        def rows(px, py, pc):
            return out_ref.at[pl.ds(pl.multiple_of((4 * px + 2 * py + pc) * m_per, 8), m_per), :]

        def copy(k, block, to, src=None):
            return pltpu.make_async_remote_copy(
                src_ref=rows(*block) if src is None else src, dst_ref=rows(*block),
                send_sem=send_sems.at[k], recv_sem=recv_sems.at[k], device_id=to, device_id_type=MESH)

        mine = pltpu.make_async_copy(x_ref, rows(*me), local_sem)
        mine.start()
        first = [copy(0, me, sibling, src=x_ref)]
        first += [copy(1 + j, me, (*chip, c), src=x_ref) for j, chip in enumerate(chips)]
        for cp in first:
            cp.start()
        passed = [copy(4 + j, (*chip, c), sibling) for j, chip in enumerate(chips)]
        for j, chip in enumerate(chips):
            copy(1 + j, (*chip, c), me).wait_recv()
            passed[j].start()
        copy(0, sibling, me).wait_recv()
        for j, chip in enumerate(chips):
            copy(4 + j, (*chip, 1 - c), me).wait_recv()
        for cp in first + passed:
            cp.wait_send()
        mine.wait()

    return pl.pallas_call(
        body, name="gather_all", out_shape=jax.ShapeDtypeStruct((N_DEV * m_per, n), v.dtype),
        in_specs=[pl.BlockSpec(memory_space=pltpu.VMEM)], out_specs=pl.BlockSpec(memory_space=pltpu.VMEM),
        scratch_shapes=[pltpu.SemaphoreType.DMA((7,)), pltpu.SemaphoreType.DMA((7,)), pltpu.SemaphoreType.DMA],
    )(v)


def _sum_chips(own, got):
    rows, cols = own.shape
    tm = _pick(rows, (512, 256, 128, 64, 32, 16))

    def body(own_r, got_r, o_r):
        acc = own_r[...]
        for n in range(3):
            acc = acc + got_r[n].astype(f32)
        o_r[...] = acc

    return pl.pallas_call(
        body, name="sum_chips", out_shape=jax.ShapeDtypeStruct((rows, cols), f32), grid=(rows // tm,),
        in_specs=[pl.BlockSpec((tm, cols), lambda i: (i, 0)), pl.BlockSpec((3, tm, cols), lambda i: (0, i, 0))],
        out_specs=pl.BlockSpec((tm, cols), lambda i: (i, 0)), compiler_params=_params(("parallel",)),
    )(own, got)


def _add(a, b, name):
    rows, cols = a.shape
    tm = _pick(rows, (512, 256, 128, 64, 32, 16, 8))
    return _rowwise(lambda i, u, v: (u + v,), name, rows, tm, [(a, (cols, 0)), (b, (cols, 0))], [(cols, f32)])[0]


def _sum_devices(g, m_per):
    n = g.shape[1]

    def body(g_r, o_r):
        acc = g_r[0:m_per, :]
        for d in range(1, N_DEV):
            acc = acc + g_r[d * m_per:(d + 1) * m_per, :]
        o_r[...] = acc

    return pl.pallas_call(body, name="sum_devices", out_shape=jax.ShapeDtypeStruct((m_per, n), f32))(g)


def _adamw(w, g, m, v, name):
    shape = w.shape
    cols = shape[-1]
    rows = w.size // cols
    w2, g2, m2, v2 = (t.reshape(rows, cols) for t in (w, g, m, v))
    tm = _pick(rows, (256, 128, 64, 32, 16, 8)) if rows * cols * 4 > (1 << 20) else rows

    def fn(i, wv, gv, mv, vv):
        mn = ADAM_B1 * mv + (1.0 - ADAM_B1) * gv
        vn = ADAM_B2 * vv + (1.0 - ADAM_B2) * jnp.square(gv)
        m_hat = mn / (1.0 - ADAM_B1 ** ADAM_STEP)
        v_hat = vn / (1.0 - ADAM_B2 ** ADAM_STEP)
        delta = -ADAM_LR * (m_hat / (jnp.sqrt(v_hat) + ADAM_EPS) + ADAM_WD * wv)
        return delta, mn, vn

    outs = _rowwise(fn, name, rows, tm, [(t, (cols, 0)) for t in (w2, g2, m2, v2)], [(cols, f32)] * 3)
    return tuple(o.reshape(shape) for o in outs)


def _pack(pieces, rows, dtype):
    flat = jnp.concatenate([p.reshape(-1).astype(dtype) for p in pieces])
    return jnp.pad(flat, (0, rows * PACK_COLS - flat.size)).reshape(rows, PACK_COLS)


def _unpack(buf, shapes):
    lead = buf.shape[:-2]
    flat = buf.reshape(lead + (-1,))
    out, off = [], 0
    for s in shapes:
        n = 1
        for d in s:
            n *= d
        out.append(flat[..., off:off + n].reshape(lead + tuple(s)))
        off += n
    return out


def _join_cols(t):
    return jnp.moveaxis(t, 0, -2).reshape(t.shape[1:-1] + (N_CHIPS * t.shape[-1],))


def _join_rows(t):
    return t.reshape((N_CHIPS * t.shape[1],) + t.shape[2:])


def _split_cols(t):
    r, c4 = t.shape
    return jnp.moveaxis(t.reshape(r, N_CHIPS, c4 // N_CHIPS), 1, 0)


def _split_rows(t):
    return t.reshape((N_CHIPS, t.shape[0] // N_CHIPS) + t.shape[1:])


def _as_bf16_pairs(t):
    return lax.bitcast_convert_type(t, jnp.bfloat16).reshape(-1)


def _from_bf16_pairs(t, shape):
    pairs = t.astype(jnp.bfloat16).reshape(t.shape[:-1] + (t.shape[-1] // 2, 2))
    return lax.bitcast_convert_type(pairs, f32).reshape(t.shape[:-1] + tuple(shape))


def kernel(x, meta_tokens, gdn_norm_g, gdn_w_in, gdn_conv_w, gdn_a_log, gdn_dt_bias, gdn_onorm_g, gdn_w_out, kv_norm_g, w_kv, sb_norm_g, sb_w_q, sb_w_o, ffn_norm_g, ffn_w_gate_up, ffn_w_down, final_norm_g, loss_target, m_meta_tokens, m_gdn_norm_g, m_gdn_w_in, m_gdn_conv_w, m_gdn_a_log, m_gdn_dt_bias, m_gdn_onorm_g, m_gdn_w_out, m_kv_norm_g, m_w_kv, m_sb_norm_g, m_sb_w_q, m_sb_w_o, m_ffn_norm_g, m_ffn_w_gate_up, m_ffn_w_down, m_final_norm_g, v_meta_tokens, v_gdn_norm_g, v_gdn_w_in, v_gdn_conv_w, v_gdn_a_log, v_gdn_dt_bias, v_gdn_onorm_g, v_gdn_w_out, v_kv_norm_g, v_w_kv, v_sb_norm_g, v_sb_w_q, v_sb_w_o, v_ffn_norm_g, v_ffn_w_gate_up, v_ffn_w_down, v_final_norm_g):
    weights = dict(meta_tokens=meta_tokens, gdn_norm_g=gdn_norm_g, gdn_w_in=gdn_w_in, gdn_conv_w=gdn_conv_w,
                   gdn_a_log=gdn_a_log, gdn_dt_bias=gdn_dt_bias, gdn_onorm_g=gdn_onorm_g, gdn_w_out=gdn_w_out,
                   kv_norm_g=kv_norm_g, w_kv=w_kv, sb_norm_g=sb_norm_g, sb_w_q=sb_w_q, sb_w_o=sb_w_o,
                   ffn_norm_g=ffn_norm_g, ffn_w_gate_up=ffn_w_gate_up, ffn_w_down=ffn_w_down, final_norm_g=final_norm_g)
    m_in = dict(meta_tokens=m_meta_tokens, gdn_norm_g=m_gdn_norm_g, gdn_w_in=m_gdn_w_in, gdn_conv_w=m_gdn_conv_w,
                gdn_a_log=m_gdn_a_log, gdn_dt_bias=m_gdn_dt_bias, gdn_onorm_g=m_gdn_onorm_g, gdn_w_out=m_gdn_w_out,
                kv_norm_g=m_kv_norm_g, w_kv=m_w_kv, sb_norm_g=m_sb_norm_g, sb_w_q=m_sb_w_q, sb_w_o=m_sb_w_o,
                ffn_norm_g=m_ffn_norm_g, ffn_w_gate_up=m_ffn_w_gate_up, ffn_w_down=m_ffn_w_down, final_norm_g=m_final_norm_g)
    v_in = dict(meta_tokens=v_meta_tokens, gdn_norm_g=v_gdn_norm_g, gdn_w_in=v_gdn_w_in, gdn_conv_w=v_gdn_conv_w,
                gdn_a_log=v_gdn_a_log, gdn_dt_bias=v_gdn_dt_bias, gdn_onorm_g=v_gdn_onorm_g, gdn_w_out=v_gdn_w_out,
                kv_norm_g=v_kv_norm_g, w_kv=v_w_kv, sb_norm_g=v_sb_norm_g, sb_w_q=v_sb_w_q, sb_w_o=v_sb_w_o,
                ffn_norm_g=v_ffn_norm_g, ffn_w_gate_up=v_ffn_w_gate_up, ffn_w_down=v_ffn_w_down, final_norm_g=v_final_norm_g)
    names = list(weights)

    seq, d = x.shape[1], x.shape[2]
    lo_frames = FRONT + N_META
    used = lo_frames + seq
    rows = -(-used // SB_BLOCK) * SB_BLOCK
    tm = _pick(rows, (640, 512, 384, 256, 128))
    tp = _pick(rows, (320, 256, 128))
    n_ffn = ffn_w_gate_up.shape[0]
    d_ff = ffn_w_down.shape[1] * N_CHIPS
    sb_width = sb_w_q.shape[2]
    in_cols = gdn_w_in.shape[2] * N_CHIPS
    chip = 2 * lax.axis_index("x") + lax.axis_index("y")

    big = [gdn_w_in[0], gdn_w_out[0], w_kv, sb_w_q[0], sb_w_o[0], ffn_w_gate_up, ffn_w_down]
    small = [meta_tokens, gdn_norm_g, gdn_conv_w[0]]
    n_big = sum(t.size for t in big)
    n_all = n_big + 2 * sum(t.size for t in small)
    w_rows = -(-n_all // (32 * PACK_COLS)) * 32
    packed = _pack([t for t in big] + [_as_bf16_pairs(t) for t in small], w_rows, bf16)
    gathered = _gather_chips(packed)
    parts = _unpack(gathered, [t.shape for t in big] + [(2 * t.size,) for t in small])
    w_in_s, w_out_s, w_kv_s, w_q_s, w_o_s, w_gu_s, w_dn_s = parts[:7]
    w_in = _join_cols(w_in_s)
    pad_ab = jnp.zeros((d, LANES - GDN_HEADS), bf16)
    w_in_ext = jnp.concatenate([w_in[:, :4 * GDN_WIDTH], w_in[:, 4 * GDN_WIDTH:4 * GDN_WIDTH + GDN_HEADS], pad_ab,
                                w_in[:, 4 * GDN_WIDTH + GDN_HEADS:], pad_ab], axis=1)
    w_out = _join_rows(w_out_s)
    w_kvf = _join_cols(w_kv_s)
    w_k, w_v = w_kvf[:, :sb_width], w_kvf[:, sb_width:]
    w_q = _join_rows(w_q_s)
    w_o = _join_rows(w_o_s)
    w_gu = [_join_cols(w_gu_s[:, l]) for l in range(n_ffn)]
    w_dn = [_join_rows(w_dn_s[:, l]) for l in range(n_ffn)]
    meta_full = _join_cols(_from_bf16_pairs(parts[7], meta_tokens.shape))
    gdn_g_full = _join_cols(_from_bf16_pairs(parts[8], gdn_norm_g.shape))
    conv_full = _join_cols(_from_bf16_pairs(parts[9], gdn_conv_w.shape[1:]))

    zeros = lambda n: jnp.zeros((n, d), f32)
    h0 = jnp.concatenate([zeros(FRONT), meta_full, x[0], zeros(rows - used)], axis=0)
    tgt = jnp.concatenate([zeros(lo_frames), loss_target[0], zeros(rows - used)], axis=0)
    pad8 = lambda t: jnp.pad(t, ((0, 0), (0, LANES - t.shape[1])))
    a_log8, dt_bias8 = pad8(gdn_a_log), pad8(gdn_dt_bias)
    r_i = jnp.arange(tp)
    ltri = ((r_i[:, None] >= r_i[None, :]) & (r_i[:, None] // CHUNK == r_i[None, :] // CHUNK)).astype(f32)
    ffn_g = [ffn_norm_g[l:l + 1] for l in range(n_ffn)]
    kv_g, fin_g = kv_norm_g.reshape(1, d), final_norm_g.reshape(1, d)

    n0 = _rms_fwd(h0, gdn_g_full, "gdn_norm")
    proj = _matmul(n0, w_in_ext, "nn", "gdn_proj")
    gq, gk, gv, gw, bw = _gdn_prep_fwd(proj, conv_full, a_log8, dt_bias8, ltri, FRONT, used, tp)
    g_o, g_states = _gdn_fwd(gq, gk, gv, gw, bw)
    og = _gdn_gate_fwd(g_o, proj, gdn_onorm_g, tm)
    h1 = _matmul(og, w_out, "nn", "gdn_out", res=h0)

    def ffn_fwd(h, l):
        n = _rms_fwd(h, ffn_g[l], f"ffn{l}_norm")
        gu = _matmul(n, w_gu[l], "nn", f"ffn{l}_gate_up")
        act = _swiglu_fwd(gu, f"ffn{l}_act")
        return _matmul(act, w_dn[l], "nn", f"ffn{l}_down", res=h), (n, gu, act)

    h2, ffn0_saved = ffn_fwd(h1, 0)
    n_kv = _rms_fwd(h2, kv_g, "kv_norm")
    kv = jnp.concatenate([_matmul(n_kv, w_k, "nn", "k_proj", out_dtype=bf16),
                          _matmul(n_kv, w_v, "nn", "v_proj", out_dtype=bf16)], axis=1)
    n_sb = _rms_fwd(h2, sb_norm_g, "sb_norm")
    sq = _matmul(n_sb, w_q, "nn", "q_proj", out_dtype=bf16)
    s_o = _sb_fwd(sq, kv, sb_width)
    h3 = _matmul(s_o, w_o, "nn", "sb_out", res=h2)
    h4, ffn1_saved = ffn_fwd(h3, 1)
    dh4, d_fin_g, loss_part = _loss_head(h4, fin_g, tgt, lo_frames, used, "loss_head")

    def ffn_bwd(dh, h, l, saved):
        n, gu, act = saved
        d_wdn = _matmul(act, dh, "tn", f"ffn{l}_d_w_down")
        dact = _matmul(dh, w_dn[l], "nt", f"ffn{l}_d_act")
        dgu = _swiglu_bwd(gu, dact, f"ffn{l}_d_gate_up")
        d_wgu = _matmul(n, dgu, "tn", f"ffn{l}_d_w_gate_up")
        dn = _matmul(dgu, w_gu[l], "nt", f"ffn{l}_d_norm")
        dh_in, dg = _rms_bwd(h, ffn_g[l], dn, dh, f"ffn{l}_norm_bwd")
        return dh_in, d_wgu, d_wdn, dg

    dh3, d_wgu1, d_wdn1, d_ffn_g1 = ffn_bwd(dh4, h3, 1, ffn1_saved)
    d_wo = _matmul(s_o, dh3, "tn", "d_w_o")
    d_so = _matmul(dh3, w_o, "nt", "d_sb_o")
    d_sq, d_sk, d_sv = _sb_bwd(sq, kv, d_so, sb_width)
    d_wq = _matmul(n_sb, d_sq, "tn", "d_w_q")
    dh2, d_sb_g = _rms_bwd(h2, sb_norm_g, _matmul(d_sq, w_q, "nt", "d_sb_norm"), dh3, "sb_norm_bwd")
    d_wkv = jnp.concatenate([_matmul(n_kv, d_sk, "tn", "d_w_k"), _matmul(n_kv, d_sv, "tn", "d_w_v")], axis=1)
    dn_kv = _matmul(d_sv, w_v, "nt", "d_kv_norm_v", res=_matmul(d_sk, w_k, "nt", "d_kv_norm_k"))
    dh2, d_kv_g = _rms_bwd(h2, kv_g, dn_kv, dh2, "kv_norm_bwd")
    dh1, d_wgu0, d_wdn0, d_ffn_g0 = ffn_bwd(dh2, h1, 0, ffn0_saved)
    d_wout = _matmul(og, dh1, "tn", "d_w_out")
    d_og = _matmul(dh1, w_out, "nt", "d_gdn_gated")
    d_go, d_gate, d_onorm = _gdn_gate_bwd(g_o, proj, gdn_onorm_g, d_og, tm)
    d_gq, d_gk, d_gv, d_gw, d_bw = _gdn_bwd(gq, gk, gv, gw, bw, g_states, d_go)
    dconv, d_a_in, d_b_in, d_a_log8, d_dt_bias8 = _gdn_prep_bwd_act(
        proj, conv_full, a_log8, dt_bias8, ltri, d_gq, d_gk, d_gv, d_gw, d_bw, FRONT, used, tp)
    dproj, d_conv = _gdn_prep_bwd_conv(proj, conv_full, dconv, d_gate, d_a_in, d_b_in, tp)
    d_win_ext = _matmul(n0, dproj, "tn", "d_w_in")
    dh0, d_gdn_g = _rms_bwd(h0, gdn_g_full, _matmul(dproj, w_in_ext, "nt", "d_gdn_norm"), dh1, "gdn_norm_bwd")
    grad_x = dh0[lo_frames:used][None]
    d_win = jnp.concatenate([d_win_ext[:, :4 * GDN_WIDTH], d_win_ext[:, 4 * GDN_WIDTH:4 * GDN_WIDTH + GDN_HEADS],
                             d_win_ext[:, 4 * GDN_WIDTH + LANES:4 * GDN_WIDTH + LANES + GDN_HEADS]], axis=1)

    by_chip = [_split_cols(d_win), _split_rows(d_wout), _split_cols(d_wkv), _split_rows(d_wq), _split_rows(d_wo),
               jnp.stack([_split_cols(d_wgu0), _split_cols(d_wgu1)], axis=1),
               jnp.stack([_split_rows(d_wdn0), _split_rows(d_wdn1)], axis=1)]
    g_rows = -(-n_big // (512 * PACK_COLS)) * 512
    flat = jnp.concatenate([t.reshape(N_CHIPS, -1) for t in by_chip], axis=1)
    g_all = jnp.pad(flat, ((0, 0), (0, g_rows * PACK_COLS - n_big))).reshape(N_CHIPS, g_rows, PACK_COLS)
    own = lax.dynamic_index_in_dim(g_all, chip, 0, keepdims=False)
    got = _scatter_chips(g_all.astype(bf16))
    over_chips = _sum_chips(own, got)
    g_sum = _add(over_chips, _swap_sibling(over_chips), "sum_cores")
    g_big = _unpack(g_sum, [t.shape for t in big])

    small_parts = [dh0[FRONT:lo_frames], d_gdn_g, d_conv, d_a_log8, d_dt_bias8, d_onorm, d_kv_g, d_sb_g,
                   d_ffn_g0, d_ffn_g1, d_fin_g, loss_part]
    s_rows = -(-sum(t.size for t in small_parts) // (8 * PACK_COLS)) * 8
    s_sum = _sum_devices(_gather_all(_pack(small_parts, s_rows, f32)), s_rows)
    (g_meta, g_gdn_g, g_conv, g_a_log8, g_dt8, g_onorm, g_kv_g, g_sb_g, g_ffn_g0, g_ffn_g1, g_fin_g,
     loss_v) = _unpack(s_sum, [t.shape for t in small_parts])
    col_shard = lambda t, w: lax.dynamic_slice_in_dim(t, chip * w, w, axis=t.ndim - 1)

    grads = dict(
        meta_tokens=col_shard(g_meta, meta_tokens.shape[1]), gdn_norm_g=col_shard(g_gdn_g, gdn_norm_g.shape[1]),
        gdn_w_in=g_big[0][None], gdn_conv_w=col_shard(g_conv, gdn_conv_w.shape[2])[None],
        gdn_a_log=g_a_log8[:, :GDN_HEADS], gdn_dt_bias=g_dt8[:, :GDN_HEADS], gdn_onorm_g=g_onorm,
        gdn_w_out=g_big[1][None], kv_norm_g=g_kv_g.reshape(-1), w_kv=g_big[2], sb_norm_g=g_sb_g,
        sb_w_q=g_big[3][None], sb_w_o=g_big[4][None], ffn_norm_g=jnp.concatenate([g_ffn_g0, g_ffn_g1], axis=0),
        ffn_w_gate_up=g_big[5], ffn_w_down=g_big[6], final_norm_g=g_fin_g.reshape(-1))

    delta, new_m, new_v = {}, {}, {}
    for n in names:
        delta[n], new_m[n], new_v[n] = _adamw(weights[n], grads[n], m_in[n], v_in[n], f"adamw_{n}")
    loss = loss_v[0, 0]
    return (loss, grad_x, *[grads[n] for n in names], *[delta[n] for n in names],
            *[new_m[n] for n in names], *[new_v[n] for n in names])
```

```python
import functools

import jax
import jax.numpy as jnp
from jax import lax
from jax.experimental import pallas as pl
from jax.experimental.pallas import tpu as pltpu

f32 = jnp.float32
bf16 = jnp.bfloat16
i32 = jnp.int32

EPS = 1e-6
N_META = 16
CHUNK = 64
FRONT = (-N_META) % CHUNK
GDN_HEADS = 8
GDN_DIM = 128
GDN_WIDTH = GDN_HEADS * GDN_DIM
GDN_GROUP = 8
CONV_WIDTH = 4
SB_DIM = 64
SB_BLOCK = 128
SB_FWD_HEADS = 4
SB_UNDERFLOW = 104.0
LANES = 128
PACK_COLS = 1024
N_CHIPS = 4
N_DEV = 8
ADAM_LR, ADAM_B1, ADAM_B2, ADAM_EPS, ADAM_WD, ADAM_STEP = 0.001, 0.9, 0.999, 1e-08, 0.01, 10
VMEM_LIMIT = 56 * 1024 * 1024
MESH = pl.DeviceIdType.MESH


def _pick(n, prefs):
    for p in prefs:
        if n % p == 0:
            return p
    return n


def _params(sem):
    return pltpu.CompilerParams(dimension_semantics=sem, vmem_limit_bytes=VMEM_LIMIT)


_DIMS = {"nn": ((1,), (0,)), "nt": ((1,), (1,)), "tn": ((0,), (0,))}


def _bdot(a, b, mode):
    return lax.dot_general(a.astype(bf16), b.astype(bf16), (_DIMS[mode], ((), ())), preferred_element_type=f32)


def _matmul(a, b, mode, name, res=None, out_dtype=f32):
    if mode == "nn":
        (m, k), n = a.shape, b.shape[1]
    elif mode == "nt":
        (m, k), n = a.shape, b.shape[0]
    else:
        (k, m), n = a.shape, b.shape[1]
    tm = _pick(m, (640, 1408, 1024, 512, 384, 256, 128))
    tn = _pick(n, (1408, 2176, 1024, 512, 384, 256, 128))
    tk = _pick(k, (1408, 2176, 1024, 640, 512, 384, 256, 128))
    nk = k // tk
    a_spec = pl.BlockSpec((tk, tm), lambda i, j, kk: (kk, i)) if mode == "tn" else pl.BlockSpec((tm, tk), lambda i, j, kk: (i, kk))
    b_spec = pl.BlockSpec((tn, tk), lambda i, j, kk: (j, kk)) if mode == "nt" else pl.BlockSpec((tk, tn), lambda i, j, kk: (kk, j))
    o_spec = pl.BlockSpec((tm, tn), lambda i, j, kk: (i, j))
    has_res = res is not None

    def body(*refs):
        if has_res:
            a_ref, b_ref, r_ref, o_ref, acc = refs
        else:
            a_ref, b_ref, o_ref, acc = refs
        kk = pl.program_id(2)

        @pl.when(kk == 0)
        def _():
            acc[...] = jnp.zeros_like(acc)

        acc[...] += _bdot(a_ref[...], b_ref[...], mode)

        @pl.when(kk == nk - 1)
        def _():
            y = acc[...]
            if has_res:
                y = y + r_ref[...]
            o_ref[...] = y.astype(o_ref.dtype)

    ins = [a, b] + ([res] if has_res else [])
    specs = [a_spec, b_spec] + ([o_spec] if has_res else [])
    return pl.pallas_call(
        body, name=name, out_shape=jax.ShapeDtypeStruct((m, n), out_dtype), grid=(m // tm, n // tn, nk),
        in_specs=specs, out_specs=o_spec, scratch_shapes=[pltpu.VMEM((tm, tn), f32)],
        compiler_params=_params(("parallel", "parallel", "arbitrary")),
    )(*ins)


def _rowwise(fn, name, rows, tm, ins, outs, reds=()):
    n_in, n_out, n_red = len(ins), len(outs), len(reds)
    in_specs = []
    for arr, spec in ins:
        if spec is None:
            in_specs.append(pl.BlockSpec(arr.shape, lambda i, nd=arr.ndim: (0,) * nd))
        else:
            w, cb = spec
            in_specs.append(pl.BlockSpec((tm, w), lambda i, cb=cb: (i, cb)))
    out_specs = [pl.BlockSpec((tm, w), lambda i: (i, 0)) for w, _ in outs]
    out_specs += [pl.BlockSpec(s, lambda i, nd=len(s): (0,) * nd) for s in reds]
    out_shape = [jax.ShapeDtypeStruct((rows, w), dt) for w, dt in outs]
    out_shape += [jax.ShapeDtypeStruct(s, f32) for s in reds]

    def body(*refs):
        i = pl.program_id(0)
        vals = fn(i, *[r[...] for r in refs[:n_in]])
        for r, v in zip(refs[n_in:n_in + n_out], vals[:n_out]):
            r[...] = v.astype(r.dtype)
        red_refs = refs[n_in + n_out:]

        @pl.when(i == 0)
        def _():
            for r in red_refs:
                r[...] = jnp.zeros_like(r)

        for r, v in zip(red_refs, vals[n_out:]):
            r[...] += v

    res = pl.pallas_call(
        body, name=name, out_shape=out_shape, grid=(rows // tm,), in_specs=in_specs, out_specs=out_specs,
        compiler_params=_params(("arbitrary",)),
    )(*[a for a, _ in ins])
    return res


def _rms(x, g):
    return x * lax.rsqrt(jnp.mean(x * x, axis=-1, keepdims=True) + EPS) * g


def _row_mask(i, tm, lo, hi, shape):
    r = i * tm + lax.broadcasted_iota(i32, shape, 0)
    return (r >= lo) & (r < hi)


def _rms_fwd(x, g, name):
    rows, d = x.shape
    tm = _pick(rows, (640, 512, 384, 256, 128))
    return _rowwise(lambda i, xv, gv: (_rms(xv, gv),), name, rows, tm, [(x, (d, 0)), (g, None)], [(d, bf16)])[0]


def _rms_bwd(x, g, dn, res, name):
    rows, d = x.shape
    tm = _pick(rows, (640, 512, 384, 256, 128))

    def fn(i, xv, gv, dnv, rv):
        _, vjp = jax.vjp(_rms, xv, gv)
        dx, dg = vjp(dnv)
        return rv + dx, dg

    return _rowwise(fn, name, rows, tm, [(x, (d, 0)), (g, None), (dn, (d, 0)), (res, (d, 0))], [(d, f32)], [(1, d)])


def _swiglu(gate, up):
    return jax.nn.silu(gate) * up


def _swiglu_fwd(gu, name):
    rows, f2 = gu.shape
    f = f2 // 2
    tm = _pick(rows, (128,))
    return _rowwise(lambda i, a, b: (_swiglu(a, b),), name, rows, tm, [(gu, (f, 0)), (gu, (f, 1))], [(f, bf16)])[0]


def _swiglu_bwd(gu, dact, name):
    rows, f2 = gu.shape
    f = f2 // 2
    tm = _pick(rows, (128,))

    def fn(i, a, b, dv):
        _, vjp = jax.vjp(_swiglu, a, b)
        da, db = vjp(dv)
        return (jnp.concatenate([da, db], axis=1),)

    return _rowwise(fn, name, rows, tm, [(gu, (f, 0)), (gu, (f, 1)), (dact, (f, 0))], [(f2, bf16)])[0]


def _loss_head(h, g, tgt, lo, hi, name):
    rows, d = h.shape
    tm = _pick(rows, (640, 512, 384, 256, 128))

    def fn(i, hv, gv, tv):
        mask = _row_mask(i, tm, lo, hi, (tm, 1))

        def f(hh, gg):
            err = _rms(hh, gg) - tv
            per_row = jnp.where(mask, jnp.mean(err * err, axis=-1, keepdims=True), 0.0)
            return 0.5 * jnp.sum(per_row, axis=0, keepdims=True)

        loss, vjp = jax.vjp(f, hv, gv)
        dh, dg = vjp(jnp.ones_like(loss))
        return dh, dg, jnp.broadcast_to(loss, (1, LANES))

    return _rowwise(fn, name, rows, tm, [(h, (d, 0)), (g, None), (tgt, (d, 0))], [(d, f32)], [(1, d), (1, LANES)])


def _heads_l2(x):
    t = x.shape[0]
    x3 = x.reshape(t, GDN_HEADS, GDN_DIM)
    return (x3 * lax.rsqrt(jnp.sum(x3 * x3, axis=-1, keepdims=True) + EPS)).reshape(t, GDN_WIDTH)


def _gdn_act(conv, a_in, b_in, a_log, dt_bias, mask):
    s = jax.nn.silu(conv)
    q = _heads_l2(s[:, :GDN_WIDTH])
    k = _heads_l2(s[:, GDN_WIDTH:2 * GDN_WIDTH])
    v = s[:, 2 * GDN_WIDTH:]
    g = jnp.where(mask, -jnp.exp(a_log) * jax.nn.softplus(a_in + dt_bias), 0.0)
    beta = jnp.where(mask, jax.nn.sigmoid(b_in), 0.0)
    return q, k, v, g, beta


def _widen(x8):
    return jnp.concatenate([jnp.broadcast_to(x8[:, h:h + 1], (x8.shape[0], GDN_DIM)) for h in range(GDN_HEADS)], axis=1)


def _narrow(xw):
    t = xw.shape[0]
    lane = lax.broadcasted_iota(i32, (t, LANES), 1)
    out = jnp.zeros((t, LANES), f32)
    for h in range(GDN_HEADS):
        s = jnp.sum(xw[:, h * GDN_DIM:(h + 1) * GDN_DIM], axis=1, keepdims=True)
        out = out + jnp.where(lane == h, s, 0.0)
    return out


def _conv_taps(cur, prev8, w):
    tm = cur.shape[0]
    cat = jnp.concatenate([prev8, cur], axis=0)
    y = cur * w[CONV_WIDTH - 1:CONV_WIDTH, :]
    for j in range(1, CONV_WIDTH):
        y = y + pltpu.roll(cat, j, axis=0)[8:8 + tm, :] * w[CONV_WIDTH - 1 - j:CONV_WIDTH - j, :]
    return y


def _gdn_prep_specs(proj, tm):
    c3 = 3 * GDN_WIDTH
    ab = 4 * GDN_WIDTH // LANES
    t8 = tm // 8
    return [
        pl.BlockSpec((tm, c3), lambda i: (i, 0)),
        pl.BlockSpec((8, c3), lambda i: (jnp.maximum(i * t8 - 1, 0), 0)),
        pl.BlockSpec((tm, LANES), lambda i: (i, ab)),
        pl.BlockSpec((tm, LANES), lambda i: (i, ab + 1)),
    ]


def _full(arr):
    return pl.BlockSpec(arr.shape, lambda i, nd=arr.ndim: (0,) * nd)


def _gdn_prep_fwd(proj, conv_w, a_log, dt_bias, ltri, lo, hi, tm):
    rows = proj.shape[0]

    def body(cur, prev8, a_in, b_in, w, al, dtb, lt, q_o, k_o, v_o, g_o, b_o):
        i = pl.program_id(0)
        mask = _row_mask(i, tm, lo, hi, (tm, LANES)) & (lax.broadcasted_iota(i32, (tm, LANES), 1) < GDN_HEADS)
        conv = _conv_taps(cur[...], prev8[...], w[...])
        q, k, v, g, beta = _gdn_act(conv, a_in[...], b_in[...], al[...], dtb[...], mask)
        q_o[...] = q
        k_o[...] = k
        v_o[...] = v
        gcum = jnp.dot(lt[...], g, preferred_element_type=f32, precision=lax.Precision.HIGHEST)
        g_o[...] = _widen(gcum)
        b_o[...] = _widen(beta)

    wide = jax.ShapeDtypeStruct((rows, GDN_WIDTH), f32)
    o_spec = pl.BlockSpec((tm, GDN_WIDTH), lambda i: (i, 0))
    return pl.pallas_call(
        body, name="gdn_prep_fwd", out_shape=[wide] * 5, grid=(rows // tm,),
        in_specs=_gdn_prep_specs(proj, tm) + [_full(conv_w), _full(a_log), _full(dt_bias), _full(ltri)],
        out_specs=[o_spec] * 5, compiler_params=_params(("parallel",)),
    )(proj, proj, proj, proj, conv_w, a_log, dt_bias, ltri)


def _gdn_prep_bwd_act(proj, conv_w, a_log, dt_bias, ltri, dq, dk, dv, dgw, dbw, lo, hi, tm):
    rows = proj.shape[0]
    c3 = 3 * GDN_WIDTH

    def body(cur, prev8, a_in, b_in, w, al, dtb, lt, dq_r, dk_r, dv_r, dg_r, db_r, dconv_o, da_o, dbin_o, dal_o, ddt_o):
        i = pl.program_id(0)
        mask = _row_mask(i, tm, lo, hi, (tm, LANES)) & (lax.broadcasted_iota(i32, (tm, LANES), 1) < GDN_HEADS)
        conv = _conv_taps(cur[...], prev8[...], w[...])
        dgcum = _narrow(dg_r[...])
        dg = lax.dot_general(lt[...], dgcum, (((0,), (0,)), ((), ())), preferred_element_type=f32,
                             precision=lax.Precision.HIGHEST)
        dbeta = _narrow(db_r[...])
        _, vjp = jax.vjp(lambda c, a, b, x, y: _gdn_act(c, a, b, x, y, mask), conv, a_in[...], b_in[...], al[...], dtb[...])
        dconv, da, dbin, dal, ddt = vjp((dq_r[...], dk_r[...], dv_r[...], dg, dbeta))
        dconv_o[...] = dconv
        da_o[...] = da
        dbin_o[...] = dbin

        @pl.when(i == 0)
        def _():
            dal_o[...] = jnp.zeros_like(dal_o)
            ddt_o[...] = jnp.zeros_like(ddt_o)

        dal_o[...] += dal
        ddt_o[...] += ddt

    w_spec = pl.BlockSpec((tm, GDN_WIDTH), lambda i: (i, 0))
    n_spec = pl.BlockSpec((tm, LANES), lambda i: (i, 0))
    s_spec = pl.BlockSpec((1, LANES), lambda i: (0, 0))
    return pl.pallas_call(
        body, name="gdn_prep_bwd_act",
        out_shape=[jax.ShapeDtypeStruct((rows, c3), f32), jax.ShapeDtypeStruct((rows, LANES), f32),
                   jax.ShapeDtypeStruct((rows, LANES), f32), jax.ShapeDtypeStruct((1, LANES), f32),
                   jax.ShapeDtypeStruct((1, LANES), f32)],
        grid=(rows // tm,),
        in_specs=_gdn_prep_specs(proj, tm) + [_full(conv_w), _full(a_log), _full(dt_bias), _full(ltri)] + [w_spec] * 5,
        out_specs=[pl.BlockSpec((tm, c3), lambda i: (i, 0)), n_spec, n_spec, s_spec, s_spec],
        compiler_params=_params(("arbitrary",)),
    )(proj, proj, proj, proj, conv_w, a_log, dt_bias, ltri, dq, dk, dv, dgw, dbw)


def _gdn_prep_bwd_conv(proj, conv_w, dconv, dgate, da, dbin, tm):
    rows, width = proj.shape
    c3 = 3 * GDN_WIDTH
    t8 = tm // 8
    nt = rows // tm

    def body(cur, prev8, w, dc, dnext8, dgt, da_r, db_r, dp_o, dw_o):
        i = pl.program_id(0)
        d = dc[...]
        nxt = jnp.where(i == nt - 1, 0.0, dnext8[...])
        cat = jnp.concatenate([d, nxt], axis=0)
        wv = w[...]
        dx = d * wv[CONV_WIDTH - 1:CONV_WIDTH, :]
        for j in range(1, CONV_WIDTH):
            dx = dx + pltpu.roll(cat, tm + 8 - j, axis=0)[:tm, :] * wv[CONV_WIDTH - 1 - j:CONV_WIDTH - j, :]
        dp_o[:, :c3] = dx.astype(bf16)
        dp_o[:, c3:4 * GDN_WIDTH] = dgt[...].astype(bf16)
        dp_o[:, 4 * GDN_WIDTH:4 * GDN_WIDTH + LANES] = da_r[...].astype(bf16)
        dp_o[:, 4 * GDN_WIDTH + LANES:] = db_r[...].astype(bf16)

        xcat = jnp.concatenate([prev8[...], cur[...]], axis=0)
        parts = [jnp.sum(d * cur[...], axis=0, keepdims=True)]
        for j in range(1, CONV_WIDTH):
            parts.append(jnp.sum(d * pltpu.roll(xcat, j, axis=0)[8:8 + tm, :], axis=0, keepdims=True))
        dwt = jnp.concatenate(parts[::-1], axis=0)

        @pl.when(i == 0)
        def _():
            dw_o[...] = jnp.zeros_like(dw_o)

        dw_o[...] += dwt

    n_spec = pl.BlockSpec((tm, LANES), lambda i: (i, 0))
    return pl.pallas_call(
        body, name="gdn_prep_bwd_conv",
        out_shape=[jax.ShapeDtypeStruct((rows, width), bf16), jax.ShapeDtypeStruct((CONV_WIDTH, c3), f32)],
        grid=(nt,),
        in_specs=[pl.BlockSpec((tm, c3), lambda i: (i, 0)),
                  pl.BlockSpec((8, c3), lambda i: (jnp.maximum(i * t8 - 1, 0), 0)),
                  _full(conv_w),
                  pl.BlockSpec((tm, c3), lambda i: (i, 0)),
                  pl.BlockSpec((8, c3), lambda i: (jnp.minimum((i + 1) * t8, rows // 8 - 1), 0)),
                  pl.BlockSpec((tm, GDN_WIDTH), lambda i: (i, 0)), n_spec, n_spec],
        out_specs=[pl.BlockSpec((tm, width), lambda i: (i, 0)), pl.BlockSpec((CONV_WIDTH, c3), lambda i: (0, 0))],
        compiler_params=_params(("arbitrary",)),
    )(proj, proj, conv_w, dconv, dconv, dgate, da, dbin)


def _split(a):
    hi = a.astype(bf16)
    return hi, (a - hi.astype(f32)).astype(bf16)


def _dot3(a, b, mode):
    ah, al = _split(a)
    bh, bl = _split(b)
    d = lambda x, y: lax.dot_general(x, y, (_DIMS[mode], ((), ())), preferred_element_type=f32)
    return d(ah, bh) + (d(ah, bl) + d(al, bh))


def _make_mm(dot):
    @jax.custom_vjp
    def nn(a, b):
        return dot(a, b, "nn")

    nn.defvjp(lambda a, b: (dot(a, b, "nn"), (a, b)),
              lambda r, ct: (dot(ct, r[1], "nt"), dot(r[0], ct, "tn")))

    @jax.custom_vjp
    def nt(a, b):
        return dot(a, b, "nt")

    nt.defvjp(lambda a, b: (dot(a, b, "nt"), (a, b)),
              lambda r, ct: (dot(ct, r[1], "nn"), dot(ct, r[0], "tn")))

    @jax.custom_vjp
    def tn(a, b):
        return dot(a, b, "tn")

    tn.defvjp(lambda a, b: (dot(a, b, "tn"), (a, b)),
              lambda r, ct: (dot(r[1], ct, "nt"), dot(r[0], ct, "nn")))
    return nn, nt, tn


_mm, _mm_nt, _mm_tn = _make_mm(_bdot)
_mm3, _, _ = _make_mm(_dot3)


def _each(f, *lists):
    return [f(*xs) for xs in zip(*lists)]


def _gdn_chunk(q, k, v, gcb, bcb, s_in):
    c = q[0].shape[0]
    ri = lax.broadcasted_iota(i32, (c, c), 0)
    ci = lax.broadcasted_iota(i32, (c, c), 1)
    incl, strict = ri >= ci, ri > ci
    rowi = lax.broadcasted_iota(i32, gcb[0].shape, 0)
    qs = _each(lambda t: t * (GDN_DIM ** -0.5), q)
    decay = _each(lambda g: jnp.where(incl, jnp.exp(jnp.where(incl, g[:, :c] - g[:, :c].T, 0.0)), 0.0), gcb)
    kk = _each(lambda t: _mm_nt(t, t), k)
    a1 = _each(lambda b, d, t: jnp.where(strict, b[:, :c] * d * t, 0.0), bcb, decay, kk)
    eg = _each(jnp.exp, gcb)
    x = _each(lambda b, vv, e, t: jnp.concatenate([b * vv, (b * e) * t], axis=1), bcb, v, eg, k)
    pows = [a1]
    for _ in range(5):
        pows.append(_each(lambda p: _mm3(p, p), pows[-1]))
    for ps in pows[:0:-1]:
        x = _each(lambda p, t: t + _mm3(p, t), ps, x)
    x = _each(lambda p, t: t - _mm3(p, t), a1, x)
    attn = _each(lambda a, b, d: _mm_nt(a, b) * d, qs, k, decay)
    glast = _each(lambda g: jnp.sum(jnp.where(rowi == c - 1, g, 0.0), axis=0, keepdims=True), gcb)
    u = _each(lambda t, s: t[:, :GDN_DIM] - _mm(t[:, GDN_DIM:], s), x, s_in)
    o = _each(lambda a, e, s, w, uu: _mm(a * e, s) + _mm(w, uu), qs, eg, s_in, attn, u)
    s_out = _each(lambda s, gl, t, g, uu: s * jnp.exp(gl) + _mm_tn(t * jnp.exp(gl - g), uu), s_in, glast, k, gcb, u)
    return o, s_out


def _gdn_fwd(q, k, v, gw, bw):
    rows = q.shape[0]
    nc = rows // CHUNK
    blk = pl.BlockSpec((CHUNK, GDN_GROUP * GDN_DIM), lambda g, c: (c, g))

    def body(q_r, k_r, v_r, g_r, b_r, o_r, st_r, s_sc):
        @pl.when(pl.program_id(1) == 0)
        def _():
            s_sc[...] = jnp.zeros_like(s_sc)

        heads = lambda r: [r[:, h * GDN_DIM:(h + 1) * GDN_DIM] for h in range(GDN_GROUP)]
        s_in = [s_sc[h] for h in range(GDN_GROUP)]
        st_r[0] = s_sc[...]
        o, s_out = _gdn_chunk(heads(q_r), heads(k_r), heads(v_r), heads(g_r), heads(b_r), s_in)
        o_r[...] = jnp.concatenate(o, axis=1)
        for h in range(GDN_GROUP):
            s_sc[h] = s_out[h]

    return pl.pallas_call(
        body, name="gdn_fwd",
        out_shape=[jax.ShapeDtypeStruct((rows, GDN_WIDTH), f32), jax.ShapeDtypeStruct((nc, GDN_HEADS, GDN_DIM, GDN_DIM), f32)],
        grid=(GDN_HEADS // GDN_GROUP, nc), in_specs=[blk] * 5,
        out_specs=[blk, pl.BlockSpec((1, GDN_GROUP, GDN_DIM, GDN_DIM), lambda g, c: (c, g, 0, 0))],
        scratch_shapes=[pltpu.VMEM((GDN_GROUP, GDN_DIM, GDN_DIM), f32)],
        compiler_params=_params(("parallel", "arbitrary")),
    )(q, k, v, gw, bw)


def _gdn_bwd(q, k, v, gw, bw, states, do):
    rows = q.shape[0]
    nc = rows // CHUNK
    blk = pl.BlockSpec((CHUNK, GDN_GROUP * GDN_DIM), lambda g, c: (nc - 1 - c, g))

    def body(q_r, k_r, v_r, g_r, b_r, st_r, do_r, dq_r, dk_r, dv_r, dg_r, db_r, ds_sc):
        @pl.when(pl.program_id(1) == 0)
        def _():
            ds_sc[...] = jnp.zeros_like(ds_sc)

        heads = lambda r: [r[:, h * GDN_DIM:(h + 1) * GDN_DIM] for h in range(GDN_GROUP)]
        s_in = [st_r[0, h] for h in range(GDN_GROUP)]
        _, vjp = jax.vjp(_gdn_chunk, heads(q_r), heads(k_r), heads(v_r), heads(g_r), heads(b_r), s_in)
        dq, dk, dv, dg, db, ds_in = vjp((heads(do_r), [ds_sc[h] for h in range(GDN_GROUP)]))
        dq_r[...] = jnp.concatenate(dq, axis=1)
        dk_r[...] = jnp.concatenate(dk, axis=1)
        dv_r[...] = jnp.concatenate(dv, axis=1)
        dg_r[...] = jnp.concatenate(dg, axis=1)
        db_r[...] = jnp.concatenate(db, axis=1)
        for h in range(GDN_GROUP):
            ds_sc[h] = ds_in[h]

    wide = jax.ShapeDtypeStruct((rows, GDN_WIDTH), f32)
    return pl.pallas_call(
        body, name="gdn_bwd", out_shape=[wide] * 5, grid=(GDN_HEADS // GDN_GROUP, nc),
        in_specs=[blk] * 5 + [pl.BlockSpec((1, GDN_GROUP, GDN_DIM, GDN_DIM), lambda g, c: (nc - 1 - c, g, 0, 0)), blk],
        out_specs=[blk] * 5, scratch_shapes=[pltpu.VMEM((GDN_GROUP, GDN_DIM, GDN_DIM), f32)],
        compiler_params=_params(("parallel", "arbitrary")),
    )(q, k, v, gw, bw, states, do)


def _gdn_gate(o, gate, og):
    t = o.shape[0]
    o3 = o.reshape(t, GDN_HEADS, GDN_DIM)
    n = o3 * lax.rsqrt(jnp.mean(o3 * o3, axis=-1, keepdims=True) + EPS) * og.reshape(1, 1, GDN_DIM)
    return n.reshape(t, GDN_WIDTH) * jax.nn.silu(gate)


def _gdn_gate_fwd(o, proj, og, tm):
    rows = o.shape[0]
    return _rowwise(lambda i, ov, gv, w: (_gdn_gate(ov, gv, w),), "gdn_gate_fwd", rows, tm,
                    [(o, (GDN_WIDTH, 0)), (proj, (GDN_WIDTH, 3)), (og, None)], [(GDN_WIDTH, bf16)])[0]


def _gdn_gate_bwd(o, proj, og, dy, tm):
    rows = o.shape[0]

    def fn(i, ov, gv, w, d):
        _, vjp = jax.vjp(_gdn_gate, ov, gv, w)
        return vjp(d)

    return _rowwise(fn, "gdn_gate_bwd", rows, tm,
                    [(o, (GDN_WIDTH, 0)), (proj, (GDN_WIDTH, 3)), (og, None), (dy, (GDN_WIDTH, 0))],
                    [(GDN_WIDTH, f32), (GDN_WIDTH, f32)], [(1, GDN_DIM)])


def _sb_visible(i, j):
    qpos = i * SB_BLOCK + lax.broadcasted_iota(i32, (SB_BLOCK, SB_BLOCK), 0)
    kpos = j * SB_BLOCK + lax.broadcasted_iota(i32, (SB_BLOCK, SB_BLOCK), 1)
    return (kpos < qpos) & (kpos >= FRONT)


def _sb_logs(z, vis):
    l1p = jnp.log1p(jnp.exp(-jnp.abs(z)))
    return -(jnp.maximum(-z, 0.0) + l1p), jnp.where(vis, -(jnp.maximum(z, 0.0) + l1p), 0.0)


def _tri_sum(x, tri):
    hi, lo = _split(x)
    return jnp.dot(hi, tri, preferred_element_type=f32) + jnp.dot(lo, tri, preferred_element_type=f32)


def _sb_live(t, i, run):
    return (t <= i) & (jnp.max(run) > -SB_UNDERFLOW)


def _sb_fwd(q, kv, width):
    rows = q.shape[0]
    nq = rows // SB_BLOCK
    lanes = SB_FWD_HEADS * SB_DIM
    npair = width // lanes
    scale = SB_DIM ** -0.5

    def body(q_r, k_r, v_r, o_r):
        i = pl.program_id(1)
        rj = lax.broadcasted_iota(i32, (SB_BLOCK, SB_BLOCK), 0)
        cs = lax.broadcasted_iota(i32, (SB_BLOCK, SB_BLOCK), 1)
        after = (rj > cs).astype(bf16)
        sls = [slice(a * SB_DIM, (a + 1) * SB_DIM) for a in range(SB_FWD_HEADS)]
        qs = [q_r[:, sl] for sl in sls]

        def step(carry):
            t, accs, runs = carry
            j = i - t
            ks = pl.ds(pl.multiple_of(j * SB_BLOCK, SB_BLOCK), SB_BLOCK)
            vis = _sb_visible(i, j)
            z = _each(lambda qh, sl: _bdot(qh, k_r[ks, sl], "nt") * scale, qs, sls)
            logs = _each(lambda zz: _sb_logs(zz, vis), z)
            later = _each(lambda l: _tri_sum(l[1], after), logs)
            w = _each(lambda l, s, run: jnp.where(vis, jnp.exp(l[0] + s + run), 0.0), logs, later, runs)
            accs = _each(lambda acc, ww, sl: acc + _bdot(ww, v_r[ks, sl], "nn"), accs, w, sls)
            runs = _each(lambda run, l: run + jnp.sum(l[1], axis=1, keepdims=True), runs, logs)
            return t + 1, tuple(accs), tuple(runs)

        init = (jnp.int32(0), tuple(jnp.zeros((SB_BLOCK, SB_DIM), f32) for _ in sls),
                tuple(jnp.zeros((SB_BLOCK, 1), f32) for _ in sls))
        _, accs, _ = lax.while_loop(lambda c: _sb_live(c[0], i, functools.reduce(jnp.maximum, c[2])), step, init)
        o_r[...] = jnp.concatenate(accs, axis=1)

    return pl.pallas_call(
        body, name="sb_fwd", out_shape=jax.ShapeDtypeStruct((rows, width), f32), grid=(npair, nq),
        in_specs=[pl.BlockSpec((SB_BLOCK, lanes), lambda p, i: (i, p)),
                  pl.BlockSpec((rows, lanes), lambda p, i: (0, p)),
                  pl.BlockSpec((rows, lanes), lambda p, i: (0, npair + p))],
        out_specs=pl.BlockSpec((SB_BLOCK, lanes), lambda p, i: (i, p)),
        compiler_params=_params(("parallel", "arbitrary")),
    )(q, kv, kv)


def _sb_bwd(q, kv, do, width):
    rows = q.shape[0]
    nq = rows // SB_BLOCK
    npair = width // LANES
    nh = LANES // SB_DIM
    scale = SB_DIM ** -0.5

    def body(q_r, k_r, v_r, do_r, dq_r, dk_r, dv_r, e_sc, sig_sc, w_sc):
        i = pl.program_id(1)

        @pl.when(i == 0)
        def _():
            dk_r[...] = jnp.zeros_like(dk_r)
            dv_r[...] = jnp.zeros_like(dv_r)

        rj = lax.broadcasted_iota(i32, (SB_BLOCK, SB_BLOCK), 0)
        cs = lax.broadcasted_iota(i32, (SB_BLOCK, SB_BLOCK), 1)
        after = (rj > cs).astype(bf16)
        from_s = (rj >= cs).astype(bf16)
        zero1 = jnp.zeros((SB_BLOCK, 1), f32)
        sls = [slice(a * SB_DIM, (a + 1) * SB_DIM) for a in range(nh)]
        qs = [q_r[:, sl] for sl in sls]
        dos = [do_r[:, sl] for sl in sls]

        def weigh(carry):
            t, runs, eruns = carry
            j = i - t
            ks = pl.ds(pl.multiple_of(j * SB_BLOCK, SB_BLOCK), SB_BLOCK)
            vis = _sb_visible(i, j)
            z = _each(lambda qh, sl: _bdot(qh, k_r[ks, sl], "nt") * scale, qs, sls)
            dw = _each(lambda doh, sl: _bdot(doh, v_r[ks, sl], "nt"), dos, sls)
            logs = _each(lambda zz: _sb_logs(zz, vis), z)
            later = _each(lambda l: _tri_sum(l[1], after), logs)
            w = _each(lambda l, s, run: jnp.where(vis, jnp.exp(l[0] + s + run), 0.0), logs, later, runs)
            e = _each(lambda ww, d: ww * d, w, dw)
            for a in range(nh):
                e_sc[a, t] = e[a]
                sig_sc[a, t] = jax.nn.sigmoid(z[a])
                w_sc[a, t] = w[a].astype(w_sc.dtype)
            runs = _each(lambda run, l: run + jnp.sum(l[1], axis=1, keepdims=True), runs, logs)
            eruns = _each(lambda erun, ee: erun + jnp.sum(ee, axis=1, keepdims=True), eruns, e)
            return t + 1, tuple(runs), tuple(eruns)

        n_blk, _, etots = lax.while_loop(lambda c: _sb_live(c[0], i, functools.reduce(jnp.maximum, c[1])), weigh,
                                         (jnp.int32(0), (zero1,) * nh, (zero1,) * nh))

        def push(t, carry):
            dqs, eruns = carry
            j = i - t
            ks = pl.ds(pl.multiple_of(j * SB_BLOCK, SB_BLOCK), SB_BLOCK)
            vis = (j * SB_BLOCK + cs < i * SB_BLOCK + rj) & (j * SB_BLOCK + cs >= FRONT)
            heads = list(range(nh))
            e = _each(lambda a: e_sc[a, t], heads)
            dvs = _each(lambda a, doh: _bdot(w_sc[a, t], doh, "tn"), heads, dos)
            upto = _each(lambda ee: _tri_sum(ee, from_s), e)
            dz = _each(lambda a, ee, u, erun, etot: jnp.where(
                vis, ee * (1.0 - sig_sc[a, t]) - (etot - erun - u) * sig_sc[a, t], 0.0) * scale, heads, e, upto, eruns, etots)
            dks = _each(lambda d, qh: _bdot(d, qh, "tn"), dz, qs)
            dqs = _each(lambda dq, d, sl: dq + _bdot(d, k_r[ks, sl], "nn"), dqs, dz, sls)
            eruns = _each(lambda erun, ee: erun + jnp.sum(ee, axis=1, keepdims=True), eruns, e)
            dk_r[ks, :] += jnp.concatenate(dks, axis=1)
            dv_r[ks, :] += jnp.concatenate(dvs, axis=1)
            return tuple(dqs), tuple(eruns)

        dqs, _ = lax.fori_loop(0, n_blk, push, (tuple(jnp.zeros((SB_BLOCK, SB_DIM), f32) for _ in sls), (zero1,) * nh))
        dq_r[...] = jnp.concatenate(dqs, axis=1)

    blk = pl.BlockSpec((SB_BLOCK, LANES), lambda p, i: (i, p))
    col = pl.BlockSpec((rows, LANES), lambda p, i: (0, p))
    wide = jax.ShapeDtypeStruct((rows, width), f32)
    return pl.pallas_call(
        body, name="sb_bwd", out_shape=[wide] * 3, grid=(npair, nq),
        in_specs=[blk, col, pl.BlockSpec((rows, LANES), lambda p, i: (0, npair + p)), blk],
        out_specs=[blk, col, col],
        scratch_shapes=[pltpu.VMEM((nh, nq, SB_BLOCK, SB_BLOCK), f32), pltpu.VMEM((nh, nq, SB_BLOCK, SB_BLOCK), f32),
                        pltpu.VMEM((nh, nq, SB_BLOCK, SB_BLOCK), bf16)],
        compiler_params=_params(("parallel", "arbitrary")),
    )(q, kv, kv, do)


_FLIPS = ((1, 0), (0, 1), (1, 1))
_ANY = pl.BlockSpec(memory_space=pl.ANY)


def _flip(v, a):
    return v + a - 2 * a * v


def _gather_chips(shards):
    num = len(shards)

    def body(*refs):
        ins, outs = refs[:num], refs[num:2 * num]
        send_sems, recv_sems, local_sems = refs[2 * num:]
        x, y, c = lax.axis_index("x"), lax.axis_index("y"), lax.axis_index("c")
        me, sibling = (x, y, c), (x, y, 1 - c)
        chip = 2 * x + y
        others = [(_flip(x, a), _flip(y, b)) for a, b in _FLIPS]

        def half_of(ref, hc):
            half = ref.shape[0] // 2
            start = hc * half
            for align in (16, 8):
                if half % align == 0:
                    start = pl.multiple_of(start, align)
                    break
            return ref.at[pl.ds(start, half)]

        def copy(k, n, s, hc, to, src=None):
            dst = half_of(outs[k].at[s], hc)
            return pltpu.make_async_remote_copy(
                src_ref=dst if src is None else src, dst_ref=dst,
                send_sem=send_sems.at[6 * k + n], recv_sem=recv_sems.at[6 * k + n], device_id=to, device_id_type=MESH)

        mine = [pltpu.make_async_copy(ins[k], outs[k].at[chip], local_sems.at[k]) for k in range(num)]
        first = [copy(k, n, chip, c, (ox, oy, c), src=half_of(ins[k], c))
                 for k in range(num) for n, (ox, oy) in enumerate(others)]
        for cp in mine + first:
            cp.start()
        passed = []
        for k in range(num):
            for n, (ox, oy) in enumerate(others):
                copy(k, n, 2 * ox + oy, c, me).wait_recv()
                passed.append(copy(k, 3 + n, 2 * ox + oy, c, sibling))
                passed[-1].start()
        for k in range(num):
            for n, (ox, oy) in enumerate(others):
                copy(k, 3 + n, 2 * ox + oy, 1 - c, me).wait_recv()
        for cp in first + passed:
            cp.wait_send()
        for cp in mine:
            cp.wait()

    return pl.pallas_call(
        body, name="gather_chips", out_shape=[jax.ShapeDtypeStruct((N_CHIPS,) + t.shape, t.dtype) for t in shards],
        in_specs=[_ANY] * num, out_specs=[_ANY] * num,
        scratch_shapes=[pltpu.SemaphoreType.DMA((6 * num,)), pltpu.SemaphoreType.DMA((6 * num,)),
                        pltpu.SemaphoreType.DMA((num,))],
    )(*shards)


def _scatter_chips(parts):
    num = len(parts)

    def body(*refs):
        ins, outs = refs[:num], refs[num:2 * num]
        send_sems, recv_sems = refs[2 * num:]
        x, y, c = lax.axis_index("x"), lax.axis_index("y"), lax.axis_index("c")
        cps = []
        for k in range(num):
            for n, (a, b) in enumerate(_FLIPS):
                ox, oy = _flip(x, a), _flip(y, b)
                cps.append(pltpu.make_async_remote_copy(
                    src_ref=ins[k].at[2 * ox + oy], dst_ref=outs[k].at[n], send_sem=send_sems.at[3 * k + n],
                    recv_sem=recv_sems.at[3 * k + n], device_id=(ox, oy, c), device_id_type=MESH))
        for cp in cps:
            cp.start()
        for cp in cps:
            cp.wait()

    return pl.pallas_call(
        body, name="scatter_chips", out_shape=[jax.ShapeDtypeStruct((3,) + t.shape[1:], t.dtype) for t in parts],
        in_specs=[_ANY] * num, out_specs=[_ANY] * num,
        scratch_shapes=[pltpu.SemaphoreType.DMA((3 * num,)), pltpu.SemaphoreType.DMA((3 * num,))],
    )(*parts)


def _swap_sibling(arrs):
    num = len(arrs)

    def body(*refs):
        ins, outs = refs[:num], refs[num:2 * num]
        send_sems, recv_sems = refs[2 * num:]
        x, y, c = lax.axis_index("x"), lax.axis_index("y"), lax.axis_index("c")
        cps = [pltpu.make_async_remote_copy(src_ref=ins[k], dst_ref=outs[k], send_sem=send_sems.at[k],
                                            recv_sem=recv_sems.at[k], device_id=(x, y, 1 - c), device_id_type=MESH)
               for k in range(num)]
        for cp in cps:
            cp.start()
        for cp in cps:
            cp.wait()

    return pl.pallas_call(
        body, name="swap_sibling", out_shape=[jax.ShapeDtypeStruct(t.shape, t.dtype) for t in arrs],
        in_specs=[_ANY] * num, out_specs=[_ANY] * num,
        scratch_shapes=[pltpu.SemaphoreType.DMA((num,)), pltpu.SemaphoreType.DMA((num,))],
    )(*arrs)


def _gather_all(v):
    m_per, n = v.shape

    def body(x_ref, out_ref, send_sems, recv_sems, local_sem):
        x, y, c = lax.axis_index("x"), lax.axis_index("y"), lax.axis_index("c")
        me, sibling = (x, y, c), (x, y, 1 - c)
        chips = [(_flip(x, a), _flip(y, b)) for a, b in _FLIPS]

        def rows(px, py, pc):
            return out_ref.at[pl.ds(pl.multiple_of((4 * px + 2 * py + pc) * m_per, 8), m_per), :]

        def copy(k, block, to, src=None):
            return pltpu.make_async_remote_copy(
                src_ref=rows(*block) if src is None else src, dst_ref=rows(*block),
                send_sem=send_sems.at[k], recv_sem=recv_sems.at[k], device_id=to, device_id_type=MESH)

        mine = pltpu.make_async_copy(x_ref, rows(*me), local_sem)
        mine.start()
        first = [copy(0, me, sibling, src=x_ref)]
        first += [copy(1 + j, me, (*chip, c), src=x_ref) for j, chip in enumerate(chips)]
        for cp in first:
            cp.start()
        passed = [copy(4 + j, (*chip, c), sibling) for j, chip in enumerate(chips)]
        for j, chip in enumerate(chips):
            copy(1 + j, (*chip, c), me).wait_recv()
            passed[j].start()
        copy(0, sibling, me).wait_recv()
        for j, chip in enumerate(chips):
            copy(4 + j, (*chip, 1 - c), me).wait_recv()
        for cp in first + passed:
            cp.wait_send()
        mine.wait()

    return pl.pallas_call(
        body, name="gather_all", out_shape=jax.ShapeDtypeStruct((N_DEV * m_per, n), v.dtype),
        in_specs=[pl.BlockSpec(memory_space=pltpu.VMEM)], out_specs=pl.BlockSpec(memory_space=pltpu.VMEM),
        scratch_shapes=[pltpu.SemaphoreType.DMA((7,)), pltpu.SemaphoreType.DMA((7,)), pltpu.SemaphoreType.DMA],
    )(v)


def _sum_chips(parts, got, chip, name):
    cols = parts.shape[-1]
    rows = parts.size // (N_CHIPS * cols)
    tm = _pick(rows, (256, 128, 64, 32, 16))

    def body(chip_r, own_r, got_r, o_r):
        acc = own_r[0]
        for n in range(3):
            acc = acc + got_r[n].astype(f32)
        o_r[...] = acc

    return pl.pallas_call(
        body, name=name, out_shape=jax.ShapeDtypeStruct((rows, cols), f32),
        grid_spec=pltpu.PrefetchScalarGridSpec(
            num_scalar_prefetch=1, grid=(rows // tm,),
            in_specs=[pl.BlockSpec((1, tm, cols), lambda i, s: (s[0], i, 0)),
                      pl.BlockSpec((3, tm, cols), lambda i, s: (0, i, 0))],
            out_specs=pl.BlockSpec((tm, cols), lambda i, s: (i, 0))),
        compiler_params=_params(("parallel",)),
    )(chip, parts.reshape(N_CHIPS, rows, cols), got.reshape(3, rows, cols))


def _sum_devices(g, m_per):
    n = g.shape[1]

    def body(g_r, o_r):
        acc = g_r[0:m_per, :]
        for d in range(1, N_DEV):
            acc = acc + g_r[d * m_per:(d + 1) * m_per, :]
        o_r[...] = acc

    return pl.pallas_call(body, name="sum_devices", out_shape=jax.ShapeDtypeStruct((m_per, n), f32))(g)


def _adamw(w, gs, m, v, name):
    shape = w.shape
    cols = shape[-1]
    rows = w.size // cols
    tm = _pick(rows, (256, 128, 64, 32, 16, 8)) if rows * cols * 4 > (1 << 20) else rows

    def fn(i, wv, mv, vv, *gv):
        g = functools.reduce(jnp.add, gv)
        mn = ADAM_B1 * mv + (1.0 - ADAM_B1) * g
        vn = ADAM_B2 * vv + (1.0 - ADAM_B2) * jnp.square(g)
        m_hat = mn / (1.0 - ADAM_B1 ** ADAM_STEP)
        v_hat = vn / (1.0 - ADAM_B2 ** ADAM_STEP)
        delta = -ADAM_LR * (m_hat / (jnp.sqrt(v_hat) + ADAM_EPS) + ADAM_WD * wv)
        return g, delta, mn, vn

    outs = _rowwise(fn, name, rows, tm, [(t.reshape(rows, cols), (cols, 0)) for t in (w, m, v) + tuple(gs)], [(cols, f32)] * 4)
    return tuple(o.reshape(shape) for o in outs)


def _pack(pieces, rows, dtype):
    flat = jnp.concatenate([p.reshape(-1).astype(dtype) for p in pieces])
    return jnp.pad(flat, (0, rows * PACK_COLS - flat.size)).reshape(rows, PACK_COLS)


def _unpack(buf, shapes):
    lead = buf.shape[:-2]
    flat = buf.reshape(lead + (-1,))
    out, off = [], 0
    for s in shapes:
        n = 1
        for d in s:
            n *= d
        out.append(flat[..., off:off + n].reshape(lead + tuple(s)))
        off += n
    return out


def _join_cols(t):
    return jnp.moveaxis(t, 0, -2).reshape(t.shape[1:-1] + (N_CHIPS * t.shape[-1],))


def _join_rows(t):
    return t.reshape((N_CHIPS * t.shape[1],) + t.shape[2:])


def _split_cols(t):
    r, c4 = t.shape
    return jnp.moveaxis(t.reshape(r, N_CHIPS, c4 // N_CHIPS), 1, 0)


def _split_rows(t):
    return t.reshape((N_CHIPS, t.shape[0] // N_CHIPS) + t.shape[1:])


def kernel(x, meta_tokens, gdn_norm_g, gdn_w_in, gdn_conv_w, gdn_a_log, gdn_dt_bias, gdn_onorm_g, gdn_w_out, kv_norm_g, w_kv, sb_norm_g, sb_w_q, sb_w_o, ffn_norm_g, ffn_w_gate_up, ffn_w_down, final_norm_g, loss_target, m_meta_tokens, m_gdn_norm_g, m_gdn_w_in, m_gdn_conv_w, m_gdn_a_log, m_gdn_dt_bias, m_gdn_onorm_g, m_gdn_w_out, m_kv_norm_g, m_w_kv, m_sb_norm_g, m_sb_w_q, m_sb_w_o, m_ffn_norm_g, m_ffn_w_gate_up, m_ffn_w_down, m_final_norm_g, v_meta_tokens, v_gdn_norm_g, v_gdn_w_in, v_gdn_conv_w, v_gdn_a_log, v_gdn_dt_bias, v_gdn_onorm_g, v_gdn_w_out, v_kv_norm_g, v_w_kv, v_sb_norm_g, v_sb_w_q, v_sb_w_o, v_ffn_norm_g, v_ffn_w_gate_up, v_ffn_w_down, v_final_norm_g):
    weights = dict(meta_tokens=meta_tokens, gdn_norm_g=gdn_norm_g, gdn_w_in=gdn_w_in, gdn_conv_w=gdn_conv_w,
                   gdn_a_log=gdn_a_log, gdn_dt_bias=gdn_dt_bias, gdn_onorm_g=gdn_onorm_g, gdn_w_out=gdn_w_out,
                   kv_norm_g=kv_norm_g, w_kv=w_kv, sb_norm_g=sb_norm_g, sb_w_q=sb_w_q, sb_w_o=sb_w_o,
                   ffn_norm_g=ffn_norm_g, ffn_w_gate_up=ffn_w_gate_up, ffn_w_down=ffn_w_down, final_norm_g=final_norm_g)
    m_in = dict(meta_tokens=m_meta_tokens, gdn_norm_g=m_gdn_norm_g, gdn_w_in=m_gdn_w_in, gdn_conv_w=m_gdn_conv_w,
                gdn_a_log=m_gdn_a_log, gdn_dt_bias=m_gdn_dt_bias, gdn_onorm_g=m_gdn_onorm_g, gdn_w_out=m_gdn_w_out,
                kv_norm_g=m_kv_norm_g, w_kv=m_w_kv, sb_norm_g=m_sb_norm_g, sb_w_q=m_sb_w_q, sb_w_o=m_sb_w_o,
                ffn_norm_g=m_ffn_norm_g, ffn_w_gate_up=m_ffn_w_gate_up, ffn_w_down=m_ffn_w_down, final_norm_g=m_final_norm_g)
    v_in = dict(meta_tokens=v_meta_tokens, gdn_norm_g=v_gdn_norm_g, gdn_w_in=v_gdn_w_in, gdn_conv_w=v_gdn_conv_w,
                gdn_a_log=v_gdn_a_log, gdn_dt_bias=v_gdn_dt_bias, gdn_onorm_g=v_gdn_onorm_g, gdn_w_out=v_gdn_w_out,
                kv_norm_g=v_kv_norm_g, w_kv=v_w_kv, sb_norm_g=v_sb_norm_g, sb_w_q=v_sb_w_q, sb_w_o=v_sb_w_o,
                ffn_norm_g=v_ffn_norm_g, ffn_w_gate_up=v_ffn_w_gate_up, ffn_w_down=v_ffn_w_down, final_norm_g=v_final_norm_g)
    names = list(weights)

    seq, d = x.shape[1], x.shape[2]
    lo_frames = FRONT + N_META
    used = lo_frames + seq
    rows = -(-used // SB_BLOCK) * SB_BLOCK
    tm = _pick(rows, (640, 512, 384, 256, 128))
    tp = _pick(rows, (320, 256, 128))
    n_ffn = ffn_w_gate_up.shape[0]
    sb_width = sb_w_q.shape[2]
    chip =2 * lax.axis_index("x") + lax.axis_index("y")

    big = [gdn_w_in[0], gdn_w_out[0], w_kv, sb_w_q[0], sb_w_o[0], ffn_w_gate_up, ffn_w_down]
    small = [meta_tokens, gdn_norm_g, gdn_conv_w[0]]
    gathered = _gather_chips([t.astype(bf16) for t in big] + [_pack(small, 16, f32)])
    w_in_s, w_out_s, w_kv_s, w_q_s, w_o_s, w_gu_s, w_dn_s = gathered[:7]
    small_s = _unpack(gathered[7], [t.shape for t in small])
    w_in = _join_cols(w_in_s)
    pad_ab = jnp.zeros((d, LANES - GDN_HEADS), bf16)
    w_in_ext = jnp.concatenate([w_in[:, :4 * GDN_WIDTH], w_in[:, 4 * GDN_WIDTH:4 * GDN_WIDTH + GDN_HEADS], pad_ab,
                                w_in[:, 4 * GDN_WIDTH + GDN_HEADS:], pad_ab], axis=1)
    w_out = _join_rows(w_out_s)
    w_kvf = _join_cols(w_kv_s)
    w_k, w_v = w_kvf[:, :sb_width], w_kvf[:, sb_width:]
    w_q = _join_rows(w_q_s)
    w_o = _join_rows(w_o_s)
    w_gu = [_join_cols(w_gu_s[:, l]) for l in range(n_ffn)]
    w_dn = [_join_rows(w_dn_s[:, l]) for l in range(n_ffn)]
    meta_full, gdn_g_full, conv_full = (_join_cols(t) for t in small_s)

    zeros = lambda n: jnp.zeros((n, d), f32)
    h0 = jnp.concatenate([zeros(FRONT), meta_full, x[0], zeros(rows - used)], axis=0)
    tgt = jnp.concatenate([zeros(lo_frames), loss_target[0], zeros(rows - used)], axis=0)
    pad8 = lambda t: jnp.pad(t, ((0, 0), (0, LANES - t.shape[1])))
    a_log8, dt_bias8 = pad8(gdn_a_log), pad8(gdn_dt_bias)
    r_i = jnp.arange(tp)
    ltri = ((r_i[:, None] >= r_i[None, :]) & (r_i[:, None] // CHUNK == r_i[None, :] // CHUNK)).astype(f32)
    ffn_g = [ffn_norm_g[l:l + 1] for l in range(n_ffn)]
    kv_g, fin_g = kv_norm_g.reshape(1, d), final_norm_g.reshape(1, d)

    n0 = _rms_fwd(h0, gdn_g_full, "gdn_norm")
    proj = _matmul(n0, w_in_ext, "nn", "gdn_proj")
    gq, gk, gv, gw, bw = _gdn_prep_fwd(proj, conv_full, a_log8, dt_bias8, ltri, FRONT, used, tp)
    g_o, g_states = _gdn_fwd(gq, gk, gv, gw, bw)
    og = _gdn_gate_fwd(g_o, proj, gdn_onorm_g, tm)
    h1 = _matmul(og, w_out, "nn", "gdn_out", res=h0)

    def ffn_fwd(h, l):
        n = _rms_fwd(h, ffn_g[l], f"ffn{l}_norm")
        gu = _matmul(n, w_gu[l], "nn", f"ffn{l}_gate_up")
        act = _swiglu_fwd(gu, f"ffn{l}_act")
        return _matmul(act, w_dn[l], "nn", f"ffn{l}_down", res=h), (n, gu, act)

    h2, ffn0_saved = ffn_fwd(h1, 0)
    n_kv = _rms_fwd(h2, kv_g, "kv_norm")
    kv = _matmul(n_kv, w_kvf, "nn", "kv_proj", out_dtype=bf16)
    n_sb = _rms_fwd(h2, sb_norm_g, "sb_norm")
    sq = _matmul(n_sb, w_q, "nn", "q_proj", out_dtype=bf16)
    s_o = _sb_fwd(sq, kv, sb_width)
    h3 = _matmul(s_o, w_o, "nn", "sb_out", res=h2)
    h4, ffn1_saved = ffn_fwd(h3, 1)
    dh4, d_fin_g, loss_part = _loss_head(h4, fin_g, tgt, lo_frames, used, "loss_head")

    def ffn_bwd(dh, h, l, saved):
        n, gu, act = saved
        d_wdn = _matmul(act, dh, "tn", f"ffn{l}_d_w_down")
        dact = _matmul(dh, w_dn[l], "nt", f"ffn{l}_d_act")
        dgu = _swiglu_bwd(gu, dact, f"ffn{l}_d_gate_up")
        d_wgu = _matmul(n, dgu, "tn", f"ffn{l}_d_w_gate_up")
        dn = _matmul(dgu, w_gu[l], "nt", f"ffn{l}_d_norm")
        dh_in, dg = _rms_bwd(h, ffn_g[l], dn, dh, f"ffn{l}_norm_bwd")
        return dh_in, d_wgu, d_wdn, dg

    dh3, d_wgu1, d_wdn1, d_ffn_g1 = ffn_bwd(dh4, h3, 1, ffn1_saved)
    d_wo = _matmul(s_o, dh3, "tn", "d_w_o")
    d_so = _matmul(dh3, w_o, "nt", "d_sb_o")
    d_sq, d_sk, d_sv = _sb_bwd(sq, kv, d_so, sb_width)
    d_wq = _matmul(n_sb, d_sq, "tn", "d_w_q")
    dh2, d_sb_g = _rms_bwd(h2, sb_norm_g, _matmul(d_sq, w_q, "nt", "d_sb_norm"), dh3, "sb_norm_bwd")
    d_wkv = jnp.concatenate([_matmul(n_kv, d_sk, "tn", "d_w_k"), _matmul(n_kv, d_sv, "tn", "d_w_v")], axis=1)
    dn_kv = _matmul(d_sv, w_v, "nt", "d_kv_norm_v", res=_matmul(d_sk, w_k, "nt", "d_kv_norm_k"))
    dh2, d_kv_g = _rms_bwd(h2, kv_g, dn_kv, dh2, "kv_norm_bwd")
    dh1, d_wgu0, d_wdn0, d_ffn_g0 = ffn_bwd(dh2, h1, 0, ffn0_saved)
    d_wout = _matmul(og, dh1, "tn", "d_w_out")
    d_og = _matmul(dh1, w_out, "nt", "d_gdn_gated")
    d_go, d_gate, d_onorm = _gdn_gate_bwd(g_o, proj, gdn_onorm_g, d_og, tm)
    d_gq, d_gk, d_gv, d_gw, d_bw = _gdn_bwd(gq, gk, gv, gw, bw, g_states, d_go)
    dconv, d_a_in, d_b_in, d_a_log8, d_dt_bias8 = _gdn_prep_bwd_act(
        proj, conv_full, a_log8, dt_bias8, ltri, d_gq, d_gk, d_gv, d_gw, d_bw, FRONT, used, tp)
    dproj, d_conv = _gdn_prep_bwd_conv(proj, conv_full, dconv, d_gate, d_a_in, d_b_in, tp)
    d_win_ext = _matmul(n0, dproj, "tn", "d_w_in")
    dh0, d_gdn_g = _rms_bwd(h0, gdn_g_full, _matmul(dproj, w_in_ext, "nt", "d_gdn_norm"), dh1, "gdn_norm_bwd")
    grad_x = dh0[lo_frames:used][None]
    d_win = jnp.concatenate([d_win_ext[:, :4 * GDN_WIDTH], d_win_ext[:, 4 * GDN_WIDTH:4 * GDN_WIDTH + GDN_HEADS],
                             d_win_ext[:, 4 * GDN_WIDTH + LANES:4 * GDN_WIDTH + LANES + GDN_HEADS]], axis=1)

    by_chip = [_split_cols(d_win), _split_rows(d_wout), _split_cols(d_wkv), _split_rows(d_wq), _split_rows(d_wo),
               jnp.stack([_split_cols(d_wgu0), _split_cols(d_wgu1)], axis=1),
               jnp.stack([_split_rows(d_wdn0), _split_rows(d_wdn1)], axis=1)]
    got = _scatter_chips([t.astype(bf16) for t in by_chip])
    chip_arr = jnp.reshape(chip, (1,)).astype(i32)
    over_chips = [_sum_chips(t, g, chip_arr, f"sum_chips_{k}") for k, (t, g) in enumerate(zip(by_chip, got))]
    over_sibling = _swap_sibling(over_chips)
    big_names = ["gdn_w_in", "gdn_w_out", "w_kv", "sb_w_q", "sb_w_o", "ffn_w_gate_up", "ffn_w_down"]
    g_big = dict(zip(big_names, zip(over_chips, over_sibling)))

    small_parts = [dh0[FRONT:lo_frames], d_gdn_g, d_conv, d_a_log8, d_dt_bias8, d_onorm, d_kv_g, d_sb_g,
                   d_ffn_g0, d_ffn_g1, d_fin_g, loss_part]
    s_rows = -(-sum(t.size for t in small_parts) // (8 * PACK_COLS)) * 8
    s_sum = _sum_devices(_gather_all(_pack(small_parts, s_rows, f32)), s_rows)
    (g_meta, g_gdn_g, g_conv, g_a_log8, g_dt8, g_onorm, g_kv_g, g_sb_g, g_ffn_g0, g_ffn_g1, g_fin_g,
     loss_v) = _unpack(s_sum, [t.shape for t in small_parts])
    col_shard = lambda t, w: lax.dynamic_slice_in_dim(t, chip * w, w, axis=t.ndim - 1)

    g_small = dict(
        meta_tokens=col_shard(g_meta, meta_tokens.shape[1]), gdn_norm_g=col_shard(g_gdn_g, gdn_norm_g.shape[1]),
        gdn_conv_w=col_shard(g_conv, gdn_conv_w.shape[2])[None],
        gdn_a_log=g_a_log8[:, :GDN_HEADS], gdn_dt_bias=g_dt8[:, :GDN_HEADS], gdn_onorm_g=g_onorm,
        kv_norm_g=g_kv_g.reshape(-1), sb_norm_g=g_sb_g, ffn_norm_g=jnp.concatenate([g_ffn_g0, g_ffn_g1], axis=0),
        final_norm_g=g_fin_g.reshape(-1))

    grads, delta, new_m, new_v = {}, {}, {}, {}
    for n in names:
        gs = g_big[n] if n in g_big else (g_small[n],)
        grads[n], delta[n], new_m[n], new_v[n] = _adamw(weights[n], gs, m_in[n], v_in[n], f"adamw_{n}")
    loss = loss_v[0, 0]
    return (loss, grad_x, *[grads[n] for n in names], *[delta[n] for n in names],
            *[new_m[n] for n in names], *[new_v[n] for n in names])
```

```python
import functools

import jax
import jax.numpy as jnp
from jax import lax
from jax.experimental import pallas as pl
from jax.experimental.pallas import tpu as pltpu

f32 = jnp.float32
bf16 = jnp.bfloat16
i32 = jnp.int32

EPS = 1e-6
N_META = 16
CHUNK = 64
FRONT = (-N_META) % CHUNK
GDN_HEADS = 8
GDN_DIM = 128
GDN_WIDTH = GDN_HEADS * GDN_DIM
CONV_WIDTH = 4
SB_DIM = 64
SB_BLOCK = 128
SB_FWD_HEADS = 4
SB_UNDERFLOW = 104.0
LANES = 128
PACK_COLS = 1024
N_CHIPS = 4
N_DEV = 8
ADAM_LR, ADAM_B1, ADAM_B2, ADAM_EPS, ADAM_WD, ADAM_STEP = 0.001, 0.9, 0.999, 1e-08, 0.01, 10
VMEM_LIMIT = 56 * 1024 * 1024
MESH = pl.DeviceIdType.MESH


def _pick(n, prefs):
    for p in prefs:
        if n % p == 0:
            return p
    return n


def _params(sem):
    return pltpu.CompilerParams(dimension_semantics=sem, vmem_limit_bytes=VMEM_LIMIT)


_DIMS = {"nn": ((1,), (0,)), "nt": ((1,), (1,)), "tn": ((0,), (0,))}


def _bdot(a, b, mode):
    return lax.dot_general(a.astype(bf16), b.astype(bf16), (_DIMS[mode], ((), ())), preferred_element_type=f32)


def _matmul(a, b, mode, name, res=None, out_dtype=f32):
    if mode == "nn":
        (m, k), n = a.shape, b.shape[1]
    elif mode == "nt":
        (m, k), n = a.shape, b.shape[0]
    else:
        (k, m), n = a.shape, b.shape[1]
    tm = _pick(m, (640, 1408, 1024, 512, 384, 256, 128))
    tn = _pick(n, (1408, 2176, 1024, 512, 384, 256, 128))
    tk = _pick(k, (1408, 2176, 1024, 640, 512, 384, 256, 128))
    nk = k // tk
    a_spec = pl.BlockSpec((tk, tm), lambda i, j, kk: (kk, i)) if mode == "tn" else pl.BlockSpec((tm, tk), lambda i, j, kk: (i, kk))
    b_spec = pl.BlockSpec((tn, tk), lambda i, j, kk: (j, kk)) if mode == "nt" else pl.BlockSpec((tk, tn), lambda i, j, kk: (kk, j))
    o_spec = pl.BlockSpec((tm, tn), lambda i, j, kk: (i, j))
    has_res = res is not None

    def body(*refs):
        if has_res:
            a_ref, b_ref, r_ref, o_ref, acc = refs
        else:
            a_ref, b_ref, o_ref, acc = refs
        kk = pl.program_id(2)

        @pl.when(kk == 0)
        def _():
            acc[...] = jnp.zeros_like(acc)

        acc[...] += _bdot(a_ref[...], b_ref[...], mode)

        @pl.when(kk == nk - 1)
        def _():
            y = acc[...]
            if has_res:
                y = y + r_ref[...]
            o_ref[...] = y.astype(o_ref.dtype)

    ins = [a, b] + ([res] if has_res else [])
    specs = [a_spec, b_spec] + ([o_spec] if has_res else [])
    return pl.pallas_call(
        body, name=name, out_shape=jax.ShapeDtypeStruct((m, n), out_dtype), grid=(m // tm, n // tn, nk),
        in_specs=specs, out_specs=o_spec, scratch_shapes=[pltpu.VMEM((tm, tn), f32)],
        compiler_params=_params(("parallel", "parallel", "arbitrary")),
    )(*ins)


def _rowwise(fn, name, rows, tm, ins, outs, reds=()):
    n_in, n_out, n_red = len(ins), len(outs), len(reds)
    in_specs = []
    for arr, spec in ins:
        if spec is None:
            in_specs.append(pl.BlockSpec(arr.shape, lambda i, nd=arr.ndim: (0,) * nd))
        else:
            w, cb = spec
            in_specs.append(pl.BlockSpec((tm, w), lambda i, cb=cb: (i, cb)))
    out_specs = [pl.BlockSpec((tm, w), lambda i: (i, 0)) for w, _ in outs]
    out_specs += [pl.BlockSpec(s, lambda i, nd=len(s): (0,) * nd) for s in reds]
    out_shape = [jax.ShapeDtypeStruct((rows, w), dt) for w, dt in outs]
    out_shape += [jax.ShapeDtypeStruct(s, f32) for s in reds]

    def body(*refs):
        i = pl.program_id(0)
        vals = fn(i, *[r[...] for r in refs[:n_in]])
        for r, v in zip(refs[n_in:n_in + n_out], vals[:n_out]):
            r[...] = v.astype(r.dtype)
        red_refs = refs[n_in + n_out:]

        @pl.when(i == 0)
        def _():
            for r in red_refs:
                r[...] = jnp.zeros_like(r)

        for r, v in zip(red_refs, vals[n_out:]):
            r[...] += v

    res = pl.pallas_call(
        body, name=name, out_shape=out_shape, grid=(rows // tm,), in_specs=in_specs, out_specs=out_specs,
        compiler_params=_params(("arbitrary",)),
    )(*[a for a, _ in ins])
    return res


def _rms(x, g):
    return x * lax.rsqrt(jnp.mean(x * x, axis=-1, keepdims=True) + EPS) * g


def _row_mask(i, tm, lo, hi, shape):
    r = i * tm + lax.broadcasted_iota(i32, shape, 0)
    return (r >= lo) & (r < hi)


def _rms_fwd(x, g, name):
    rows, d = x.shape
    tm = _pick(rows, (640, 512, 384, 256, 128))
    return _rowwise(lambda i, xv, gv: (_rms(xv, gv),), name, rows, tm, [(x, (d, 0)), (g, None)], [(d, bf16)])[0]


def _rms_bwd(x, g, dn, res, name):
    rows, d = x.shape
    tm = _pick(rows, (640, 512, 384, 256, 128))

    def fn(i, xv, gv, dnv, rv):
        _, vjp = jax.vjp(_rms, xv, gv)
        dx, dg = vjp(dnv)
        return rv + dx, dg

    return _rowwise(fn, name, rows, tm, [(x, (d, 0)), (g, None), (dn, (d, 0)), (res, (d, 0))], [(d, f32)], [(1, d)])


def _swiglu(gate, up):
    return jax.nn.silu(gate) * up


def _swiglu_fwd(gu, name):
    rows, f2 = gu.shape
    f = f2 // 2
    tm = _pick(rows, (128,))
    return _rowwise(lambda i, a, b: (_swiglu(a, b),), name, rows, tm, [(gu, (f, 0)), (gu, (f, 1))], [(f, bf16)])[0]


def _swiglu_bwd(gu, dact, name):
    rows, f2 = gu.shape
    f = f2 // 2
    tm = _pick(rows, (128,))

    def fn(i, a, b, dv):
        _, vjp = jax.vjp(_swiglu, a, b)
        da, db = vjp(dv)
        return (jnp.concatenate([da, db], axis=1),)

    return _rowwise(fn, name, rows, tm, [(gu, (f, 0)), (gu, (f, 1)), (dact, (f, 0))], [(f2, bf16)])[0]


def _loss_head(h, g, tgt, lo, hi, name):
    rows, d = h.shape
    tm = _pick(rows, (640, 512, 384, 256, 128))

    def fn(i, hv, gv, tv):
        mask = _row_mask(i, tm, lo, hi, (tm, 1))

        def f(hh, gg):
            err = _rms(hh, gg) - tv
            per_row = jnp.where(mask, jnp.mean(err * err, axis=-1, keepdims=True), 0.0)
            return 0.5 * jnp.sum(per_row, axis=0, keepdims=True)

        loss, vjp = jax.vjp(f, hv, gv)
        dh, dg = vjp(jnp.ones_like(loss))
        return dh, dg, jnp.broadcast_to(loss, (1, LANES))

    return _rowwise(fn, name, rows, tm, [(h, (d, 0)), (g, None), (tgt, (d, 0))], [(d, f32)], [(1, d), (1, LANES)])


def _heads_l2(x):
    t = x.shape[0]
    x3 = x.reshape(t, GDN_HEADS, GDN_DIM)
    return (x3 * lax.rsqrt(jnp.sum(x3 * x3, axis=-1, keepdims=True) + EPS)).reshape(t, GDN_WIDTH)


def _gdn_act(conv, a_in, b_in, a_log, dt_bias, mask):
    s = jax.nn.silu(conv)
    q = _heads_l2(s[:, :GDN_WIDTH])
    k = _heads_l2(s[:, GDN_WIDTH:2 * GDN_WIDTH])
    v = s[:, 2 * GDN_WIDTH:]
    g = jnp.where(mask, -jnp.exp(a_log) * jax.nn.softplus(a_in + dt_bias), 0.0)
    beta = jnp.where(mask, jax.nn.sigmoid(b_in), 0.0)
    return q, k, v, g, beta


def _widen(x8):
    return jnp.concatenate([jnp.broadcast_to(x8[:, h:h + 1], (x8.shape[0], GDN_DIM)) for h in range(GDN_HEADS)], axis=1)


def _narrow(xw):
    t = xw.shape[0]
    lane = lax.broadcasted_iota(i32, (t, LANES), 1)
    out = jnp.zeros((t, LANES), f32)
    for h in range(GDN_HEADS):
        s = jnp.sum(xw[:, h * GDN_DIM:(h + 1) * GDN_DIM], axis=1, keepdims=True)
        out = out + jnp.where(lane == h, s, 0.0)
    return out


def _conv_taps(cur, prev8, w):
    tm = cur.shape[0]
    cat = jnp.concatenate([prev8, cur], axis=0)
    y = cur * w[CONV_WIDTH - 1:CONV_WIDTH, :]
    for j in range(1, CONV_WIDTH):
        y = y + pltpu.roll(cat, j, axis=0)[8:8 + tm, :] * w[CONV_WIDTH - 1 - j:CONV_WIDTH - j, :]
    return y


def _gdn_prep_specs(proj, tm):
    c3 = 3 * GDN_WIDTH
    ab = 4 * GDN_WIDTH // LANES
    t8 = tm // 8
    return [
        pl.BlockSpec((tm, c3), lambda i: (i, 0)),
        pl.BlockSpec((8, c3), lambda i: (jnp.maximum(i * t8 - 1, 0), 0)),
        pl.BlockSpec((tm, LANES), lambda i: (i, ab)),
        pl.BlockSpec((tm, LANES), lambda i: (i, ab + 1)),
    ]


def _full(arr):
    return pl.BlockSpec(arr.shape, lambda i, nd=arr.ndim: (0,) * nd)


def _gdn_prep_fwd(proj, conv_w, a_log, dt_bias, ltri, lo, hi, tm):
    rows = proj.shape[0]

    def body(cur, prev8, a_in, b_in, w, al, dtb, lt, q_o, k_o, v_o, g_o, b_o):
        i = pl.program_id(0)
        mask = _row_mask(i, tm, lo, hi, (tm, LANES)) & (lax.broadcasted_iota(i32, (tm, LANES), 1) < GDN_HEADS)
        conv = _conv_taps(cur[...], prev8[...], w[...])
        q, k, v, g, beta = _gdn_act(conv, a_in[...], b_in[...], al[...], dtb[...], mask)
        q_o[...] = q
        k_o[...] = k
        v_o[...] = v
        gcum = jnp.dot(lt[...], g, preferred_element_type=f32, precision=lax.Precision.HIGHEST)
        g_o[...] = _widen(gcum)
        b_o[...] = _widen(beta)

    wide = jax.ShapeDtypeStruct((rows, GDN_WIDTH), f32)
    o_spec = pl.BlockSpec((tm, GDN_WIDTH), lambda i: (i, 0))
    return pl.pallas_call(
        body, name="gdn_prep_fwd", out_shape=[wide] * 5, grid=(rows // tm,),
        in_specs=_gdn_prep_specs(proj, tm) + [_full(conv_w), _full(a_log), _full(dt_bias), _full(ltri)],
        out_specs=[o_spec] * 5, compiler_params=_params(("parallel",)),
    )(proj, proj, proj, proj, conv_w, a_log, dt_bias, ltri)


def _gdn_prep_bwd_act(proj, conv_w, a_log, dt_bias, ltri, dq, dk, dv, dgw, dbw, lo, hi, tm):
    rows = proj.shape[0]
    c3 = 3 * GDN_WIDTH

    def body(cur, prev8, a_in, b_in, w, al, dtb, lt, dq_r, dk_r, dv_r, dg_r, db_r, dconv_o, da_o, dbin_o, dal_o, ddt_o):
        i = pl.program_id(0)
        mask = _row_mask(i, tm, lo, hi, (tm, LANES)) & (lax.broadcasted_iota(i32, (tm, LANES), 1) < GDN_HEADS)
        conv = _conv_taps(cur[...], prev8[...], w[...])
        dgcum = _narrow(dg_r[...])
        dg = lax.dot_general(lt[...], dgcum, (((0,), (0,)), ((), ())), preferred_element_type=f32,
                             precision=lax.Precision.HIGHEST)
        dbeta = _narrow(db_r[...])
        _, vjp = jax.vjp(lambda c, a, b, x, y: _gdn_act(c, a, b, x, y, mask), conv, a_in[...], b_in[...], al[...], dtb[...])
        dconv, da, dbin, dal, ddt = vjp((dq_r[...], dk_r[...], dv_r[...], dg, dbeta))
        dconv_o[...] = dconv
        da_o[...] = da
        dbin_o[...] = dbin

        @pl.when(i == 0)
        def _():
            dal_o[...] = jnp.zeros_like(dal_o)
            ddt_o[...] = jnp.zeros_like(ddt_o)

        dal_o[...] += dal
        ddt_o[...] += ddt

    w_spec = pl.BlockSpec((tm, GDN_WIDTH), lambda i: (i, 0))
    n_spec = pl.BlockSpec((tm, LANES), lambda i: (i, 0))
    s_spec = pl.BlockSpec((1, LANES), lambda i: (0, 0))
    return pl.pallas_call(
        body, name="gdn_prep_bwd_act",
        out_shape=[jax.ShapeDtypeStruct((rows, c3), f32), jax.ShapeDtypeStruct((rows, LANES), f32),
                   jax.ShapeDtypeStruct((rows, LANES), f32), jax.ShapeDtypeStruct((1, LANES), f32),
                   jax.ShapeDtypeStruct((1, LANES), f32)],
        grid=(rows // tm,),
        in_specs=_gdn_prep_specs(proj, tm) + [_full(conv_w), _full(a_log), _full(dt_bias), _full(ltri)] + [w_spec] * 5,
        out_specs=[pl.BlockSpec((tm, c3), lambda i: (i, 0)), n_spec, n_spec, s_spec, s_spec],
        compiler_params=_params(("arbitrary",)),
    )(proj, proj, proj, proj, conv_w, a_log, dt_bias, ltri, dq, dk, dv, dgw, dbw)


def _gdn_prep_bwd_conv(proj, conv_w, dconv, dgate, da, dbin, tm):
    rows, width = proj.shape
    c3 = 3 * GDN_WIDTH
    t8 = tm // 8
    nt = rows // tm

    def body(cur, prev8, w, dc, dnext8, dgt, da_r, db_r, dp_o, dw_o):
        i = pl.program_id(0)
        d = dc[...]
        nxt = jnp.where(i == nt - 1, 0.0, dnext8[...])
        cat = jnp.concatenate([d, nxt], axis=0)
        wv = w[...]
        dx = d * wv[CONV_WIDTH - 1:CONV_WIDTH, :]
        for j in range(1, CONV_WIDTH):
            dx = dx + pltpu.roll(cat, tm + 8 - j, axis=0)[:tm, :] * wv[CONV_WIDTH - 1 - j:CONV_WIDTH - j, :]
        dp_o[:, :c3] = dx.astype(bf16)
        dp_o[:, c3:4 * GDN_WIDTH] = dgt[...].astype(bf16)
        dp_o[:, 4 * GDN_WIDTH:4 * GDN_WIDTH + LANES] = da_r[...].astype(bf16)
        dp_o[:, 4 * GDN_WIDTH + LANES:] = db_r[...].astype(bf16)

        xcat = jnp.concatenate([prev8[...], cur[...]], axis=0)
        parts = [jnp.sum(d * cur[...], axis=0, keepdims=True)]
        for j in range(1, CONV_WIDTH):
            parts.append(jnp.sum(d * pltpu.roll(xcat, j, axis=0)[8:8 + tm, :], axis=0, keepdims=True))
        dwt = jnp.concatenate(parts[::-1], axis=0)

        @pl.when(i == 0)
        def _():
            dw_o[...] = jnp.zeros_like(dw_o)

        dw_o[...] += dwt

    n_spec = pl.BlockSpec((tm, LANES), lambda i: (i, 0))
    return pl.pallas_call(
        body, name="gdn_prep_bwd_conv",
        out_shape=[jax.ShapeDtypeStruct((rows, width), bf16), jax.ShapeDtypeStruct((CONV_WIDTH, c3), f32)],
        grid=(nt,),
        in_specs=[pl.BlockSpec((tm, c3), lambda i: (i, 0)),
                  pl.BlockSpec((8, c3), lambda i: (jnp.maximum(i * t8 - 1, 0), 0)),
                  _full(conv_w),
                  pl.BlockSpec((tm, c3), lambda i: (i, 0)),
                  pl.BlockSpec((8, c3), lambda i: (jnp.minimum((i + 1) * t8, rows // 8 - 1), 0)),
                  pl.BlockSpec((tm, GDN_WIDTH), lambda i: (i, 0)), n_spec, n_spec],
        out_specs=[pl.BlockSpec((tm, width), lambda i: (i, 0)), pl.BlockSpec((CONV_WIDTH, c3), lambda i: (0, 0))],
        compiler_params=_params(("arbitrary",)),
    )(proj, proj, conv_w, dconv, dconv, dgate, da, dbin)


def _split(a):
    hi = a.astype(bf16)
    return hi, (a - hi.astype(f32)).astype(bf16)


def _make_mm(dot):
    @jax.custom_vjp
    def nn(a, b):
        return dot(a, b, "nn")

    nn.defvjp(lambda a, b: (dot(a, b, "nn"), (a, b)),
              lambda r, ct: (dot(ct, r[1], "nt"), dot(r[0], ct, "tn")))

    @jax.custom_vjp
    def nt(a, b):
        return dot(a, b, "nt")

    nt.defvjp(lambda a, b: (dot(a, b, "nt"), (a, b)),
              lambda r, ct: (dot(ct, r[1], "nn"), dot(ct, r[0], "tn")))

    @jax.custom_vjp
    def tn(a, b):
        return dot(a, b, "tn")

    tn.defvjp(lambda a, b: (dot(a, b, "tn"), (a, b)),
              lambda r, ct: (dot(r[1], ct, "nt"), dot(r[0], ct, "nn")))
    return nn, nt, tn


_mm, _mm_nt, _mm_tn = _make_mm(_bdot)


def _each(f, *lists):
    return [f(*xs) for xs in zip(*lists)]


def _gdn_chunk(q, k, v, gcb, bcb, s_in):
    c = q[0].shape[0]
    ri = lax.broadcasted_iota(i32, (c, c), 0)
    ci = lax.broadcasted_iota(i32, (c, c), 1)
    incl, strict = ri >= ci, ri > ci
    rowi = lax.broadcasted_iota(i32, gcb[0].shape, 0)
    qs = _each(lambda t: t * (GDN_DIM ** -0.5), q)
    decay = _each(lambda g: jnp.where(incl, jnp.exp(jnp.where(incl, g[:, :c] - g[:, :c].T, 0.0)), 0.0), gcb)
    kk = _each(lambda t: _mm_nt(t, t), k)
    a1 = _each(lambda b, d, t: jnp.where(strict, b[:, :c] * d * t, 0.0), bcb, decay, kk)
    eg = _each(jnp.exp, gcb)
    x = _each(lambda b, vv, e, t: jnp.concatenate([b * vv, (b * e) * t], axis=1), bcb, v, eg, k)
    pows = [a1]
    for _ in range(5):
        pows.append(_each(lambda p: _mm(p, p), pows[-1]))
    for ps in pows[:0:-1]:
        x = _each(lambda p, t: t + _mm(p, t), ps, x)
    x = _each(lambda p, t: t - _mm(p, t), a1, x)
    attn = _each(lambda a, b, d: _mm_nt(a, b) * d, qs, k, decay)
    glast = _each(lambda g: jnp.sum(jnp.where(rowi == c - 1, g, 0.0), axis=0, keepdims=True), gcb)
    u = _each(lambda t, s: t[:, :GDN_DIM] - _mm(t[:, GDN_DIM:], s), x, s_in)
    o = _each(lambda a, e, s, w, uu: _mm(a * e, s) + _mm(w, uu), qs, eg, s_in, attn, u)
    s_out = _each(lambda s, gl, t, g, uu: s * jnp.exp(gl) + _mm_tn(t * jnp.exp(gl - g), uu), s_in, glast, k, gcb, u)
    return o, s_out


def _gdn_heads(ref):
    return [ref[:, h * GDN_DIM:(h + 1) * GDN_DIM] for h in range(GDN_HEADS)]


def _gdn_fwd(q, k, v, gw, bw, shards):
    rows = q.shape[0]
    nc = rows // CHUNK
    num = len(shards)
    blk = pl.BlockSpec((CHUNK, GDN_WIDTH), lambda c: (c, 0))

    def body(*refs):
        q_r, k_r, v_r, g_r, b_r = refs[:5]
        ins = refs[5:5 + num]
        o_r, st_r = refs[5 + num:7 + num]
        outs = refs[7 + num:7 + 2 * num]
        s_sc, send_sems, recv_sems, local_sems = refs[7 + 2 * num:]
        c = pl.program_id(0)
        start, forward, finish = _gather_plan(ins, outs, send_sems, recv_sems, local_sems)

        @pl.when(c == 0)
        def _():
            s_sc[...] = jnp.zeros_like(s_sc)
            start()

        s_in = [s_sc[h] for h in range(GDN_HEADS)]
        st_r[0] = s_sc[...]
        o, s_out = _gdn_chunk(_gdn_heads(q_r), _gdn_heads(k_r), _gdn_heads(v_r), _gdn_heads(g_r), _gdn_heads(b_r), s_in)
        o_r[...] = jnp.concatenate(o, axis=1)
        for h in range(GDN_HEADS):
            s_sc[h] = s_out[h]
        pl.when(c == nc // 2)(forward)
        pl.when(c == nc - 1)(finish)

    res = pl.pallas_call(
        body, name="gdn_fwd",
        out_shape=[jax.ShapeDtypeStruct((rows, GDN_WIDTH), f32), jax.ShapeDtypeStruct((nc, GDN_HEADS, GDN_DIM, GDN_DIM), f32)]
        + [jax.ShapeDtypeStruct((N_CHIPS,) + t.shape, t.dtype) for t in shards],
        grid=(nc,), in_specs=[blk] * 5 + [_ANY] * num,
        out_specs=[blk, pl.BlockSpec((1, GDN_HEADS, GDN_DIM, GDN_DIM), lambda c: (c, 0, 0, 0))] + [_ANY] * num,
        scratch_shapes=[pltpu.VMEM((GDN_HEADS, GDN_DIM, GDN_DIM), f32), pltpu.SemaphoreType.DMA((6 * num,)),
                        pltpu.SemaphoreType.DMA((6 * num,)), pltpu.SemaphoreType.DMA((num,))],
        compiler_params=_params(("arbitrary",)),
    )(q, k, v, gw, bw, *shards)
    return res[0], res[1], res[2:]


def _gdn_bwd(q, k, v, gw, bw, states, do, parts):
    rows = q.shape[0]
    nc = rows // CHUNK
    num = len(parts)
    blk = pl.BlockSpec((CHUNK, GDN_WIDTH), lambda c: (nc - 1 - c, 0))

    def body(*refs):
        q_r, k_r, v_r, g_r, b_r, st_r, do_r = refs[:7]
        ins = refs[7:7 + num]
        dq_r, dk_r, dv_r, dg_r, db_r = refs[7 + num:12 + num]
        outs = refs[12 + num:12 + 2 * num]
        ds_sc, send_sems, recv_sems = refs[12 + 2 * num:]
        c = pl.program_id(0)
        start, finish = _scatter_plan(ins, outs, send_sems, recv_sems)

        @pl.when(c == 0)
        def _():
            ds_sc[...] = jnp.zeros_like(ds_sc)
            start()

        s_in = [st_r[0, h] for h in range(GDN_HEADS)]
        _, vjp = jax.vjp(_gdn_chunk, _gdn_heads(q_r), _gdn_heads(k_r), _gdn_heads(v_r), _gdn_heads(g_r), _gdn_heads(b_r), s_in)
        dq, dk, dv, dg, db, ds_in = vjp((_gdn_heads(do_r), [ds_sc[h] for h in range(GDN_HEADS)]))
        dq_r[...] = jnp.concatenate(dq, axis=1)
        dk_r[...] = jnp.concatenate(dk, axis=1)
        dv_r[...] = jnp.concatenate(dv, axis=1)
        dg_r[...] = jnp.concatenate(dg, axis=1)
        db_r[...] = jnp.concatenate(db, axis=1)
        for h in range(GDN_HEADS):
            ds_sc[h] = ds_in[h]
        pl.when(c == nc - 1)(finish)

    wide = jax.ShapeDtypeStruct((rows, GDN_WIDTH), f32)
    res = pl.pallas_call(
        body, name="gdn_bwd", out_shape=[wide] * 5 + [jax.ShapeDtypeStruct((3,) + t.shape[1:], t.dtype) for t in parts],
        grid=(nc,),
        in_specs=[blk] * 5 + [pl.BlockSpec((1, GDN_HEADS, GDN_DIM, GDN_DIM), lambda c: (nc - 1 - c, 0, 0, 0)), blk] + [_ANY] * num,
        out_specs=[blk] * 5 + [_ANY] * num,
        scratch_shapes=[pltpu.VMEM((GDN_HEADS, GDN_DIM, GDN_DIM), f32), pltpu.SemaphoreType.DMA((3 * num,)),
                        pltpu.SemaphoreType.DMA((3 * num,))],
        compiler_params=_params(("arbitrary",)),
    )(q, k, v, gw, bw, states, do, *parts)
    return res[:5], res[5:]


def _gdn_gate(o, gate, og):
    t = o.shape[0]
    o3 = o.reshape(t, GDN_HEADS, GDN_DIM)
    n = o3 * lax.rsqrt(jnp.mean(o3 * o3, axis=-1, keepdims=True) + EPS) * og.reshape(1, 1, GDN_DIM)
    return n.reshape(t, GDN_WIDTH) * jax.nn.silu(gate)


def _gdn_gate_fwd(o, proj, og, tm):
    rows = o.shape[0]
    return _rowwise(lambda i, ov, gv, w: (_gdn_gate(ov, gv, w),), "gdn_gate_fwd", rows, tm,
                    [(o, (GDN_WIDTH, 0)), (proj, (GDN_WIDTH, 3)), (og, None)], [(GDN_WIDTH, bf16)])[0]


def _gdn_gate_bwd(o, proj, og, dy, tm):
    rows = o.shape[0]

    def fn(i, ov, gv, w, d):
        _, vjp = jax.vjp(_gdn_gate, ov, gv, w)
        return vjp(d)

    return _rowwise(fn, "gdn_gate_bwd", rows, tm,
                    [(o, (GDN_WIDTH, 0)), (proj, (GDN_WIDTH, 3)), (og, None), (dy, (GDN_WIDTH, 0))],
                    [(GDN_WIDTH, f32), (GDN_WIDTH, f32)], [(1, GDN_DIM)])


def _sb_visible(i, j):
    qpos = i * SB_BLOCK + lax.broadcasted_iota(i32, (SB_BLOCK, SB_BLOCK), 0)
    kpos = j * SB_BLOCK + lax.broadcasted_iota(i32, (SB_BLOCK, SB_BLOCK), 1)
    return (kpos < qpos) & (kpos >= FRONT)


def _sb_logs(z, vis):
    l1p = jnp.log1p(jnp.exp(-jnp.abs(z)))
    return -(jnp.maximum(-z, 0.0) + l1p), jnp.where(vis, -(jnp.maximum(z, 0.0) + l1p), 0.0)


def _tri_sum(x, tri):
    hi, lo = _split(x)
    return jnp.dot(hi, tri, preferred_element_type=f32) + jnp.dot(lo, tri, preferred_element_type=f32)


def _sb_live(t, i, run):
    return (t <= i) & (jnp.max(run) > -SB_UNDERFLOW)


def _sb_fwd(q, kv, width):
    rows = q.shape[0]
    nq = rows // SB_BLOCK
    lanes = SB_FWD_HEADS * SB_DIM
    npair = width // lanes
    scale = SB_DIM ** -0.5

    def body(q_r, k_r, v_r, o_r):
        i = pl.program_id(1)
        rj = lax.broadcasted_iota(i32, (SB_BLOCK, SB_BLOCK), 0)
        cs = lax.broadcasted_iota(i32, (SB_BLOCK, SB_BLOCK), 1)
        after = (rj > cs).astype(bf16)
        sls = [slice(a * SB_DIM, (a + 1) * SB_DIM) for a in range(SB_FWD_HEADS)]
        qs = [q_r[:, sl] for sl in sls]

        def step(carry):
            t, accs, runs = carry
            j = i - t
            ks = pl.ds(pl.multiple_of(j * SB_BLOCK, SB_BLOCK), SB_BLOCK)
            vis = _sb_visible(i, j)
            z = _each(lambda qh, sl: _bdot(qh, k_r[ks, sl], "nt") * scale, qs, sls)
            logs = _each(lambda zz: _sb_logs(zz, vis), z)
            later = _each(lambda l: _tri_sum(l[1], after), logs)
            w = _each(lambda l, s, run: jnp.where(vis, jnp.exp(l[0] + s + run), 0.0), logs, later, runs)
            accs = _each(lambda acc, ww, sl: acc + _bdot(ww, v_r[ks, sl], "nn"), accs, w, sls)
            runs = _each(lambda run, l: run + jnp.sum(l[1], axis=1, keepdims=True), runs, logs)
            return t + 1, tuple(accs), tuple(runs)

        init = (jnp.int32(0), tuple(jnp.zeros((SB_BLOCK, SB_DIM), f32) for _ in sls),
                tuple(jnp.zeros((SB_BLOCK, 1), f32) for _ in sls))
        _, accs, _ = lax.while_loop(lambda c: _sb_live(c[0], i, functools.reduce(jnp.maximum, c[2])), step, init)
        o_r[...] = jnp.concatenate(accs, axis=1)

    return pl.pallas_call(
        body, name="sb_fwd", out_shape=jax.ShapeDtypeStruct((rows, width), f32), grid=(npair, nq),
        in_specs=[pl.BlockSpec((SB_BLOCK, lanes), lambda p, i: (i, p)),
                  pl.BlockSpec((rows, lanes), lambda p, i: (0, p)),
                  pl.BlockSpec((rows, lanes), lambda p, i: (0, npair + p))],
        out_specs=pl.BlockSpec((SB_BLOCK, lanes), lambda p, i: (i, p)),
        compiler_params=_params(("parallel", "arbitrary")),
    )(q, kv, kv)


def _sb_bwd(q, kv, do, width):
    rows = q.shape[0]
    nq = rows // SB_BLOCK
    npair = width // LANES
    nh = LANES // SB_DIM
    scale = SB_DIM ** -0.5

    def body(q_r, k_r, v_r, do_r, dq_r, dk_r, dv_r, e_sc, sig_sc, w_sc):
        i = pl.program_id(1)

        @pl.when(i == 0)
        def _():
            dk_r[...] = jnp.zeros_like(dk_r)
            dv_r[...] = jnp.zeros_like(dv_r)

        rj = lax.broadcasted_iota(i32, (SB_BLOCK, SB_BLOCK), 0)
        cs = lax.broadcasted_iota(i32, (SB_BLOCK, SB_BLOCK), 1)
        after = (rj > cs).astype(bf16)
        from_s = (rj >= cs).astype(bf16)
        zero1 = jnp.zeros((SB_BLOCK, 1), f32)
        sls = [slice(a * SB_DIM, (a + 1) * SB_DIM) for a in range(nh)]
        qs = [q_r[:, sl] for sl in sls]
        dos = [do_r[:, sl] for sl in sls]

        def weigh(carry):
            t, runs, eruns = carry
            j = i - t
            ks = pl.ds(pl.multiple_of(j * SB_BLOCK, SB_BLOCK), SB_BLOCK)
            vis = _sb_visible(i, j)
            z = _each(lambda qh, sl: _bdot(qh, k_r[ks, sl], "nt") * scale, qs, sls)
            dw = _each(lambda doh, sl: _bdot(doh, v_r[ks, sl], "nt"), dos, sls)
            logs = _each(lambda zz: _sb_logs(zz, vis), z)
            later = _each(lambda l: _tri_sum(l[1], after), logs)
            w = _each(lambda l, s, run: jnp.where(vis, jnp.exp(l[0] + s + run), 0.0), logs, later, runs)
            e = _each(lambda ww, d: ww * d, w, dw)
            for a in range(nh):
                e_sc[a, t] = e[a]
                sig_sc[a, t] = jax.nn.sigmoid(z[a])
                w_sc[a, t] = w[a].astype(w_sc.dtype)
            runs = _each(lambda run, l: run + jnp.sum(l[1], axis=1, keepdims=True), runs, logs)
            eruns = _each(lambda erun, ee: erun + jnp.sum(ee, axis=1, keepdims=True), eruns, e)
            return t + 1, tuple(runs), tuple(eruns)

        n_blk, _, etots = lax.while_loop(lambda c: _sb_live(c[0], i, functools.reduce(jnp.maximum, c[1])), weigh,
                                         (jnp.int32(0), (zero1,) * nh, (zero1,) * nh))

        def push(t, carry):
            dqs, eruns = carry
            j = i - t
            ks = pl.ds(pl.multiple_of(j * SB_BLOCK, SB_BLOCK), SB_BLOCK)
            vis = (j * SB_BLOCK + cs < i * SB_BLOCK + rj) & (j * SB_BLOCK + cs >= FRONT)
            heads = list(range(nh))
            e = _each(lambda a: e_sc[a, t], heads)
            dvs = _each(lambda a, doh: _bdot(w_sc[a, t], doh, "tn"), heads, dos)
            upto = _each(lambda ee: _tri_sum(ee, from_s), e)
            dz = _each(lambda a, ee, u, erun, etot: jnp.where(
                vis, ee * (1.0 - sig_sc[a, t]) - (etot - erun - u) * sig_sc[a, t], 0.0) * scale, heads, e, upto, eruns, etots)
            dks = _each(lambda d, qh: _bdot(d, qh, "tn"), dz, qs)
            dqs = _each(lambda dq, d, sl: dq + _bdot(d, k_r[ks, sl], "nn"), dqs, dz, sls)
            eruns = _each(lambda erun, ee: erun + jnp.sum(ee, axis=1, keepdims=True), eruns, e)
            dk_r[ks, :] += jnp.concatenate(dks, axis=1)
            dv_r[ks, :] += jnp.concatenate(dvs, axis=1)
            return tuple(dqs), tuple(eruns)

        dqs, _ = lax.fori_loop(0, n_blk, push, (tuple(jnp.zeros((SB_BLOCK, SB_DIM), f32) for _ in sls), (zero1,) * nh))
        dq_r[...] = jnp.concatenate(dqs, axis=1)

    blk = pl.BlockSpec((SB_BLOCK, LANES), lambda p, i: (i, p))
    col = pl.BlockSpec((rows, LANES), lambda p, i: (0, p))
    wide = jax.ShapeDtypeStruct((rows, width), f32)
    return pl.pallas_call(
        body, name="sb_bwd", out_shape=[wide] * 3, grid=(npair, nq),
        in_specs=[blk, col, pl.BlockSpec((rows, LANES), lambda p, i: (0, npair + p)), blk],
        out_specs=[blk, col, col],
        scratch_shapes=[pltpu.VMEM((nh, nq, SB_BLOCK, SB_BLOCK), f32), pltpu.VMEM((nh, nq, SB_BLOCK, SB_BLOCK), f32),
                        pltpu.VMEM((nh, nq, SB_BLOCK, SB_BLOCK), bf16)],
        compiler_params=_params(("parallel", "arbitrary")),
    )(q, kv, kv, do)


_FLIPS = ((1, 0), (0, 1), (1, 1))
_ANY = pl.BlockSpec(memory_space=pl.ANY)


def _flip(v, a):
    return v + a - 2 * a * v


def _gather_plan(ins, outs, send_sems, recv_sems, local_sems):
    num = len(ins)
    x, y, c = lax.axis_index("x"), lax.axis_index("y"), lax.axis_index("c")
    me, sibling = (x, y, c), (x, y, 1 - c)
    chip = 2 * x + y
    others = [(_flip(x, a), _flip(y, b)) for a, b in _FLIPS]
    pairs = [(k, n, 2 * ox + oy) for k in range(num) for n, (ox, oy) in enumerate(others)]

    def half_of(ref, hc):
        half = ref.shape[0] // 2
        start = hc * half
        for align in (16, 8):
            if half % align == 0:
                start = pl.multiple_of(start, align)
                break
        return ref.at[pl.ds(start, half)]

    def copy(k, n, s, hc, to, src=None):
        dst = half_of(outs[k].at[s], hc)
        return pltpu.make_async_remote_copy(
            src_ref=dst if src is None else src, dst_ref=dst,
            send_sem=send_sems.at[6 * k + n], recv_sem=recv_sems.at[6 * k + n], device_id=to, device_id_type=MESH)

    mine = [pltpu.make_async_copy(ins[k], outs[k].at[chip], local_sems.at[k]) for k in range(num)]
    first = [copy(k, n, chip, c, (others[n][0], others[n][1], c), src=half_of(ins[k], c)) for k, n, _ in pairs]
    passed = [copy(k, 3 + n, s, c, sibling) for k, n, s in pairs]

    def start():
        for cp in mine + first:
            cp.start()

    def forward():
        for (k, n, s), fw in zip(pairs, passed):
            copy(k, n, s, c, me).wait_recv()
            fw.start()

    def finish():
        for k, n, s in pairs:
            copy(k, 3 + n, s, 1 - c, me).wait_recv()
        for cp in first + passed:
            cp.wait_send()
        for cp in mine:
            cp.wait()

    return start, forward, finish


def _gather_chips(shards):
    num = len(shards)

    def body(*refs):
        for phase in _gather_plan(refs[:num], refs[num:2 * num], *refs[2 * num:]):
            phase()

    return pl.pallas_call(
        body, name="gather_chips", out_shape=[jax.ShapeDtypeStruct((N_CHIPS,) + t.shape, t.dtype) for t in shards],
        in_specs=[_ANY] * num, out_specs=[_ANY] * num,
        scratch_shapes=[pltpu.SemaphoreType.DMA((6 * num,)), pltpu.SemaphoreType.DMA((6 * num,)),
                        pltpu.SemaphoreType.DMA((num,))],
    )(*shards)


def _scatter_plan(ins, outs, send_sems, recv_sems):
    x, y, c = lax.axis_index("x"), lax.axis_index("y"), lax.axis_index("c")
    cps = []
    for k in range(len(ins)):
        for n, (a, b) in enumerate(_FLIPS):
            ox, oy = _flip(x, a), _flip(y, b)
            cps.append(pltpu.make_async_remote_copy(
                src_ref=ins[k].at[2 * ox + oy], dst_ref=outs[k].at[n], send_sem=send_sems.at[3 * k + n],
                recv_sem=recv_sems.at[3 * k + n], device_id=(ox, oy, c), device_id_type=MESH))

    def start():
        for cp in cps:
            cp.start()

    def finish():
        for cp in cps:
            cp.wait()

    return start, finish


def _scatter_chips(parts):
    num = len(parts)

    def body(*refs):
        for phase in _scatter_plan(refs[:num], refs[num:2 * num], *refs[2 * num:]):
            phase()

    return pl.pallas_call(
        body, name="scatter_chips", out_shape=[jax.ShapeDtypeStruct((3,) + t.shape[1:], t.dtype) for t in parts],
        in_specs=[_ANY] * num, out_specs=[_ANY] * num,
        scratch_shapes=[pltpu.SemaphoreType.DMA((3 * num,)), pltpu.SemaphoreType.DMA((3 * num,))],
    )(*parts)


def _swap_sibling(arrs):
    num = len(arrs)

    def body(*refs):
        ins, outs = refs[:num], refs[num:2 * num]
        send_sems, recv_sems = refs[2 * num:]
        x, y, c = lax.axis_index("x"), lax.axis_index("y"), lax.axis_index("c")
        cps = [pltpu.make_async_remote_copy(src_ref=ins[k], dst_ref=outs[k], send_sem=send_sems.at[k],
                                            recv_sem=recv_sems.at[k], device_id=(x, y, 1 - c), device_id_type=MESH)
               for k in range(num)]
        for cp in cps:
            cp.start()
        for cp in cps:
            cp.wait()

    return pl.pallas_call(
        body, name="swap_sibling", out_shape=[jax.ShapeDtypeStruct(t.shape, t.dtype) for t in arrs],
        in_specs=[_ANY] * num, out_specs=[_ANY] * num,
        scratch_shapes=[pltpu.SemaphoreType.DMA((num,)), pltpu.SemaphoreType.DMA((num,))],
    )(*arrs)


def _gather_all(v):
    m_per, n = v.shape

    def body(x_ref, out_ref, send_sems, recv_sems, local_sem):
        x, y, c = lax.axis_index("x"), lax.axis_index("y"), lax.axis_index("c")
        me, sibling = (x, y, c), (x, y, 1 - c)
        chips = [(_flip(x, a), _flip(y, b)) for a, b in _FLIPS]

        def rows(px, py, pc):
            return out_ref.at[pl.ds(pl.multiple_of((4 * px + 2 * py + pc) * m_per, 8), m_per), :]

        def copy(k, block, to, src=None):
            return pltpu.make_async_remote_copy(
                src_ref=rows(*block) if src is None else src, dst_ref=rows(*block),
                send_sem=send_sems.at[k], recv_sem=recv_sems.at[k], device_id=to, device_id_type=MESH)

        mine = pltpu.make_async_copy(x_ref, rows(*me), local_sem)
        mine.start()
        first = [copy(0, me, sibling, src=x_ref)]
        first += [copy(1 + j, me, (*chip, c), src=x_ref) for j, chip in enumerate(chips)]
        for cp in first:
            cp.start()
        passed = [copy(4 + j, (*chip, c), sibling) for j, chip in enumerate(chips)]
        for j, chip in enumerate(chips):
            copy(1 + j, (*chip, c), me).wait_recv()
            passed[j].start()
        copy(0, sibling, me).wait_recv()
        for j, chip in enumerate(chips):
            copy(4 + j, (*chip, 1 - c), me).wait_recv()
        for cp in first + passed:
            cp.wait_send()
        mine.wait()

    return pl.pallas_call(
        body, name="gather_all", out_shape=jax.ShapeDtypeStruct((N_DEV * m_per, n), v.dtype),
        in_specs=[pl.BlockSpec(memory_space=pltpu.VMEM)], out_specs=pl.BlockSpec(memory_space=pltpu.VMEM),
        scratch_shapes=[pltpu.SemaphoreType.DMA((7,)), pltpu.SemaphoreType.DMA((7,)), pltpu.SemaphoreType.DMA],
    )(v)


def _sum_chips(parts, got, chip, name):
    cols = parts.shape[-1]
    rows = parts.size // (N_CHIPS * cols)
    tm = _pick(rows, (256, 128, 64, 32, 16))

    def body(chip_r, own_r, got_r, o_r):
        acc = own_r[0]
        for n in range(3):
            acc = acc + got_r[n].astype(f32)
        o_r[...] = acc

    return pl.pallas_call(
        body, name=name, out_shape=jax.ShapeDtypeStruct((rows, cols), f32),
        grid_spec=pltpu.PrefetchScalarGridSpec(
            num_scalar_prefetch=1, grid=(rows // tm,),
            in_specs=[pl.BlockSpec((1, tm, cols), lambda i, s: (s[0], i, 0)),
                      pl.BlockSpec((3, tm, cols), lambda i, s: (0, i, 0))],
            out_specs=pl.BlockSpec((tm, cols), lambda i, s: (i, 0))),
        compiler_params=_params(("parallel",)),
    )(chip, parts.reshape(N_CHIPS, rows, cols), got.reshape(3, rows, cols))


def _sum_devices(g, m_per):
    n = g.shape[1]

    def body(g_r, o_r):
        acc = g_r[0:m_per, :]
        for d in range(1, N_DEV):
            acc = acc + g_r[d * m_per:(d + 1) * m_per, :]
        o_r[...] = acc

    return pl.pallas_call(body, name="sum_devices", out_shape=jax.ShapeDtypeStruct((m_per, n), f32))(g)


def _adamw(w, gs, m, v, name):
    shape = w.shape
    cols = shape[-1]
    rows = w.size // cols
    tm = _pick(rows, (256, 128, 64, 32, 16, 8)) if rows * cols * 4 > (1 << 20) else rows

    def fn(i, wv, mv, vv, *gv):
        g = functools.reduce(jnp.add, gv)
        mn = ADAM_B1 * mv + (1.0 - ADAM_B1) * g
        vn = ADAM_B2 * vv + (1.0 - ADAM_B2) * jnp.square(g)
        m_hat = mn / (1.0 - ADAM_B1 ** ADAM_STEP)
        v_hat = vn / (1.0 - ADAM_B2 ** ADAM_STEP)
        delta = -ADAM_LR * (m_hat / (jnp.sqrt(v_hat) + ADAM_EPS) + ADAM_WD * wv)
        return g, delta, mn, vn

    outs = _rowwise(fn, name, rows, tm, [(t.reshape(rows, cols), (cols, 0)) for t in (w, m, v) + tuple(gs)], [(cols, f32)] * 4)
    return tuple(o.reshape(shape) for o in outs)


def _pack(pieces, rows, dtype):
    flat = jnp.concatenate([p.reshape(-1).astype(dtype) for p in pieces])
    return jnp.pad(flat, (0, rows * PACK_COLS - flat.size)).reshape(rows, PACK_COLS)


def _unpack(buf, shapes):
    lead = buf.shape[:-2]
    flat = buf.reshape(lead + (-1,))
    out, off = [], 0
    for s in shapes:
        n = 1
        for d in s:
            n *= d
        out.append(flat[..., off:off + n].reshape(lead + tuple(s)))
        off += n
    return out


def _join_cols(t):
    return jnp.moveaxis(t, 0, -2).reshape(t.shape[1:-1] + (N_CHIPS * t.shape[-1],))


def _join_rows(t):
    return t.reshape((N_CHIPS * t.shape[1],) + t.shape[2:])


def _split_cols(t):
    r, c4 = t.shape
    return jnp.moveaxis(t.reshape(r, N_CHIPS, c4 // N_CHIPS), 1, 0)


def _split_rows(t):
    return t.reshape((N_CHIPS, t.shape[0] // N_CHIPS) + t.shape[1:])


def kernel(x, meta_tokens, gdn_norm_g, gdn_w_in, gdn_conv_w, gdn_a_log, gdn_dt_bias, gdn_onorm_g, gdn_w_out, kv_norm_g, w_kv, sb_norm_g, sb_w_q, sb_w_o, ffn_norm_g, ffn_w_gate_up, ffn_w_down, final_norm_g, loss_target, m_meta_tokens, m_gdn_norm_g, m_gdn_w_in, m_gdn_conv_w, m_gdn_a_log, m_gdn_dt_bias, m_gdn_onorm_g, m_gdn_w_out, m_kv_norm_g, m_w_kv, m_sb_norm_g, m_sb_w_q, m_sb_w_o, m_ffn_norm_g, m_ffn_w_gate_up, m_ffn_w_down, m_final_norm_g, v_meta_tokens, v_gdn_norm_g, v_gdn_w_in, v_gdn_conv_w, v_gdn_a_log, v_gdn_dt_bias, v_gdn_onorm_g, v_gdn_w_out, v_kv_norm_g, v_w_kv, v_sb_norm_g, v_sb_w_q, v_sb_w_o, v_ffn_norm_g, v_ffn_w_gate_up, v_ffn_w_down, v_final_norm_g):
    weights = dict(meta_tokens=meta_tokens, gdn_norm_g=gdn_norm_g, gdn_w_in=gdn_w_in, gdn_conv_w=gdn_conv_w,
                   gdn_a_log=gdn_a_log, gdn_dt_bias=gdn_dt_bias, gdn_onorm_g=gdn_onorm_g, gdn_w_out=gdn_w_out,
                   kv_norm_g=kv_norm_g, w_kv=w_kv, sb_norm_g=sb_norm_g, sb_w_q=sb_w_q, sb_w_o=sb_w_o,
                   ffn_norm_g=ffn_norm_g, ffn_w_gate_up=ffn_w_gate_up, ffn_w_down=ffn_w_down, final_norm_g=final_norm_g)
    m_in = dict(meta_tokens=m_meta_tokens, gdn_norm_g=m_gdn_norm_g, gdn_w_in=m_gdn_w_in, gdn_conv_w=m_gdn_conv_w,
                gdn_a_log=m_gdn_a_log, gdn_dt_bias=m_gdn_dt_bias, gdn_onorm_g=m_gdn_onorm_g, gdn_w_out=m_gdn_w_out,
                kv_norm_g=m_kv_norm_g, w_kv=m_w_kv, sb_norm_g=m_sb_norm_g, sb_w_q=m_sb_w_q, sb_w_o=m_sb_w_o,
                ffn_norm_g=m_ffn_norm_g, ffn_w_gate_up=m_ffn_w_gate_up, ffn_w_down=m_ffn_w_down, final_norm_g=m_final_norm_g)
    v_in = dict(meta_tokens=v_meta_tokens, gdn_norm_g=v_gdn_norm_g, gdn_w_in=v_gdn_w_in, gdn_conv_w=v_gdn_conv_w,
                gdn_a_log=v_gdn_a_log, gdn_dt_bias=v_gdn_dt_bias, gdn_onorm_g=v_gdn_onorm_g, gdn_w_out=v_gdn_w_out,
                kv_norm_g=v_kv_norm_g, w_kv=v_w_kv, sb_norm_g=v_sb_norm_g, sb_w_q=v_sb_w_q, sb_w_o=v_sb_w_o,
                ffn_norm_g=v_ffn_norm_g, ffn_w_gate_up=v_ffn_w_gate_up, ffn_w_down=v_ffn_w_down, final_norm_g=v_final_norm_g)
    names = list(weights)

    seq, d = x.shape[1], x.shape[2]
    lo_frames = FRONT + N_META
    used = lo_frames + seq
    rows = -(-used // SB_BLOCK) * SB_BLOCK
    tm = _pick(rows, (640, 512, 384, 256, 128))
    tp = _pick(rows, (320, 256, 128))
    n_ffn = ffn_w_gate_up.shape[0]
    sb_width = sb_w_q.shape[2]
    chip =2 * lax.axis_index("x") + lax.axis_index("y")

    big = [gdn_w_in[0], gdn_w_out[0], w_kv, sb_w_q[0], sb_w_o[0], ffn_w_gate_up, ffn_w_down]
    small = [meta_tokens, gdn_norm_g, gdn_conv_w[0]]
    n_early = 2
    big_bf16 = [t.astype(bf16) for t in big]
    w_in_s, w_out_s, small_g = _gather_chips(big_bf16[:n_early] + [_pack(small, 16, f32)])
    small_s = _unpack(small_g, [t.shape for t in small])
    w_in = _join_cols(w_in_s)
    pad_ab = jnp.zeros((d, LANES - GDN_HEADS), bf16)
    w_in_ext = jnp.concatenate([w_in[:, :4 * GDN_WIDTH], w_in[:, 4 * GDN_WIDTH:4 * GDN_WIDTH + GDN_HEADS], pad_ab,
                                w_in[:, 4 * GDN_WIDTH + GDN_HEADS:], pad_ab], axis=1)
    w_out = _join_rows(w_out_s)
    meta_full, gdn_g_full, conv_full = (_join_cols(t) for t in small_s)

    zeros = lambda n: jnp.zeros((n, d), f32)
    h0 = jnp.concatenate([zeros(FRONT), meta_full, x[0], zeros(rows - used)], axis=0)
    tgt = jnp.concatenate([zeros(lo_frames), loss_target[0], zeros(rows - used)], axis=0)
    pad8 = lambda t: jnp.pad(t, ((0, 0), (0, LANES - t.shape[1])))
    a_log8, dt_bias8 = pad8(gdn_a_log), pad8(gdn_dt_bias)
    r_i = jnp.arange(tp)
    ltri = ((r_i[:, None] >= r_i[None, :]) & (r_i[:, None] // CHUNK == r_i[None, :] // CHUNK)).astype(f32)
    ffn_g = [ffn_norm_g[l:l + 1] for l in range(n_ffn)]
    kv_g, fin_g = kv_norm_g.reshape(1, d), final_norm_g.reshape(1, d)

    n0 = _rms_fwd(h0, gdn_g_full, "gdn_norm")
    proj = _matmul(n0, w_in_ext, "nn", "gdn_proj")
    gq, gk, gv, gw, bw = _gdn_prep_fwd(proj, conv_full, a_log8, dt_bias8, ltri, FRONT, used, tp)
    g_o, g_states, (w_kv_s, w_q_s, w_o_s, w_gu_s, w_dn_s) = _gdn_fwd(gq, gk, gv, gw, bw, big_bf16[n_early:])
    w_kvf = _join_cols(w_kv_s)
    w_k, w_v = w_kvf[:, :sb_width], w_kvf[:, sb_width:]
    w_q = _join_rows(w_q_s)
    w_o = _join_rows(w_o_s)
    w_gu = [_join_cols(w_gu_s[:, l]) for l in range(n_ffn)]
    w_dn = [_join_rows(w_dn_s[:, l]) for l in range(n_ffn)]
    og = _gdn_gate_fwd(g_o, proj, gdn_onorm_g, tm)
    h1 = _matmul(og, w_out, "nn", "gdn_out", res=h0)

    def ffn_fwd(h, l):
        n = _rms_fwd(h, ffn_g[l], f"ffn{l}_norm")
        gu = _matmul(n, w_gu[l], "nn", f"ffn{l}_gate_up")
        act = _swiglu_fwd(gu, f"ffn{l}_act")
        return _matmul(act, w_dn[l], "nn", f"ffn{l}_down", res=h), (n, gu, act)

    h2, ffn0_saved = ffn_fwd(h1, 0)
    n_kv = _rms_fwd(h2, kv_g, "kv_norm")
    kv = _matmul(n_kv, w_kvf, "nn", "kv_proj", out_dtype=bf16)
    n_sb = _rms_fwd(h2, sb_norm_g, "sb_norm")
    sq = _matmul(n_sb, w_q, "nn", "q_proj", out_dtype=bf16)
    s_o = _sb_fwd(sq, kv, sb_width)
    h3 = _matmul(s_o, w_o, "nn", "sb_out", res=h2)
    h4, ffn1_saved = ffn_fwd(h3, 1)
    dh4, d_fin_g, loss_part = _loss_head(h4, fin_g, tgt, lo_frames, used, "loss_head")

    def ffn_bwd(dh, h, l, saved):
        n, gu, act = saved
        d_wdn = _matmul(act, dh, "tn", f"ffn{l}_d_w_down")
        dact = _matmul(dh, w_dn[l], "nt", f"ffn{l}_d_act")
        dgu = _swiglu_bwd(gu, dact, f"ffn{l}_d_gate_up")
        d_wgu = _matmul(n, dgu, "tn", f"ffn{l}_d_w_gate_up")
        dn = _matmul(dgu, w_gu[l], "nt", f"ffn{l}_d_norm")
        dh_in, dg = _rms_bwd(h, ffn_g[l], dn, dh, f"ffn{l}_norm_bwd")
        return dh_in, d_wgu, d_wdn, dg

    dh3, d_wgu1, d_wdn1, d_ffn_g1 = ffn_bwd(dh4, h3, 1, ffn1_saved)
    d_wo = _matmul(s_o, dh3, "tn", "d_w_o")
    d_so = _matmul(dh3, w_o, "nt", "d_sb_o")
    d_sq, d_sk, d_sv = _sb_bwd(sq, kv, d_so, sb_width)
    d_wq = _matmul(n_sb, d_sq, "tn", "d_w_q")
    dh2, d_sb_g = _rms_bwd(h2, sb_norm_g, _matmul(d_sq, w_q, "nt", "d_sb_norm"), dh3, "sb_norm_bwd")
    d_wkv = jnp.concatenate([_matmul(n_kv, d_sk, "tn", "d_w_k"), _matmul(n_kv, d_sv, "tn", "d_w_v")], axis=1)
    dn_kv = _matmul(d_sv, w_v, "nt", "d_kv_norm_v", res=_matmul(d_sk, w_k, "nt", "d_kv_norm_k"))
    dh2, d_kv_g = _rms_bwd(h2, kv_g, dn_kv, dh2, "kv_norm_bwd")
    dh1, d_wgu0, d_wdn0, d_ffn_g0 = ffn_bwd(dh2, h1, 0, ffn0_saved)
    d_wout = _matmul(og, dh1, "tn", "d_w_out")
    d_og = _matmul(dh1, w_out, "nt", "d_gdn_gated")
    d_go, d_gate, d_onorm = _gdn_gate_bwd(g_o, proj, gdn_onorm_g, d_og, tm)
    by_chip = [None, _split_rows(d_wout), _split_cols(d_wkv), _split_rows(d_wq), _split_rows(d_wo),
               jnp.stack([_split_cols(d_wgu0), _split_cols(d_wgu1)], axis=1),
               jnp.stack([_split_rows(d_wdn0), _split_rows(d_wdn1)], axis=1)]
    (d_gq, d_gk, d_gv, d_gw, d_bw), got_early = _gdn_bwd(gq, gk, gv, gw, bw, g_states, d_go,
                                                         [t.astype(bf16) for t in by_chip[1:]])
    dconv, d_a_in, d_b_in, d_a_log8, d_dt_bias8 = _gdn_prep_bwd_act(
        proj, conv_full, a_log8, dt_bias8, ltri, d_gq, d_gk, d_gv, d_gw, d_bw, FRONT, used, tp)
    dproj, d_conv = _gdn_prep_bwd_conv(proj, conv_full, dconv, d_gate, d_a_in, d_b_in, tp)
    d_win_ext = _matmul(n0, dproj, "tn", "d_w_in")
    dh0, d_gdn_g = _rms_bwd(h0, gdn_g_full, _matmul(dproj, w_in_ext, "nt", "d_gdn_norm"), dh1, "gdn_norm_bwd")
    grad_x = dh0[lo_frames:used][None]
    d_win = jnp.concatenate([d_win_ext[:, :4 * GDN_WIDTH], d_win_ext[:, 4 * GDN_WIDTH:4 * GDN_WIDTH + GDN_HEADS],
                             d_win_ext[:, 4 * GDN_WIDTH + LANES:4 * GDN_WIDTH + LANES + GDN_HEADS]], axis=1)

    by_chip[0] = _split_cols(d_win)
    got = list(_scatter_chips([by_chip[0].astype(bf16)])) + list(got_early)
    chip_arr = jnp.reshape(chip, (1,)).astype(i32)
    over_chips = [_sum_chips(t, g, chip_arr, f"sum_chips_{k}") for k, (t, g) in enumerate(zip(by_chip, got))]
    over_sibling = _swap_sibling(over_chips)
    big_names = ["gdn_w_in", "gdn_w_out", "w_kv", "sb_w_q", "sb_w_o", "ffn_w_gate_up", "ffn_w_down"]
    g_big = dict(zip(big_names, zip(over_chips, over_sibling)))

    small_parts = [dh0[FRONT:lo_frames], d_gdn_g, d_conv, d_a_log8, d_dt_bias8, d_onorm, d_kv_g, d_sb_g,
                   d_ffn_g0, d_ffn_g1, d_fin_g, loss_part]
    s_rows = -(-sum(t.size for t in small_parts) // (8 * PACK_COLS)) * 8
    s_sum = _sum_devices(_gather_all(_pack(small_parts, s_rows, f32)), s_rows)
    (g_meta, g_gdn_g, g_conv, g_a_log8, g_dt8, g_onorm, g_kv_g, g_sb_g, g_ffn_g0, g_ffn_g1, g_fin_g,
     loss_v) = _unpack(s_sum, [t.shape for t in small_parts])
    col_shard = lambda t, w: lax.dynamic_slice_in_dim(t, chip * w, w, axis=t.ndim - 1)

    g_small = dict(
        meta_tokens=col_shard(g_meta, meta_tokens.shape[1]), gdn_norm_g=col_shard(g_gdn_g, gdn_norm_g.shape[1]),
        gdn_conv_w=col_shard(g_conv, gdn_conv_w.shape[2])[None],
        gdn_a_log=g_a_log8[:, :GDN_HEADS], gdn_dt_bias=g_dt8[:, :GDN_HEADS], gdn_onorm_g=g_onorm,
        kv_norm_g=g_kv_g.reshape(-1), sb_norm_g=g_sb_g, ffn_norm_g=jnp.concatenate([g_ffn_g0, g_ffn_g1], axis=0),
        final_norm_g=g_fin_g.reshape(-1))

    grads, delta, new_m, new_v = {}, {}, {}, {}
    for n in names:
        gs = g_big[n] if n in g_big else (g_small[n],)
        grads[n], delta[n], new_m[n], new_v[n] = _adamw(weights[n], gs, m_in[n], v_in[n], f"adamw_{n}")
    loss = loss_v[0, 0]
    return (loss, grad_x, *[grads[n] for n in names], *[delta[n] for n in names],
            *[new_m[n] for n in names], *[new_v[n] for n in names])
```

```python
import functools

import jax
import jax.numpy as jnp
from jax import lax
from jax.experimental import pallas as pl
from jax.experimental.pallas import tpu as pltpu

f32 = jnp.float32
bf16 = jnp.bfloat16
i32 = jnp.int32

EPS = 1e-6
N_META = 16
CHUNK = 64
FRONT = (-N_META) % CHUNK
GDN_HEADS = 8
GDN_DIM = 128
GDN_WIDTH = GDN_HEADS * GDN_DIM
CONV_WIDTH = 4
SB_DIM = 64
SB_BLOCK = 128
SB_FWD_HEADS = 4
SB_UNDERFLOW = 104.0
LANES = 128
PACK_COLS = 1024
N_CHIPS = 4
N_DEV = 8
ADAM_LR, ADAM_B1, ADAM_B2, ADAM_EPS, ADAM_WD, ADAM_STEP = 0.001, 0.9, 0.999, 1e-08, 0.01, 10
VMEM_LIMIT = 56 * 1024 * 1024
MESH = pl.DeviceIdType.MESH


def _pick(n, prefs):
    for p in prefs:
        if n % p == 0:
            return p
    return n


def _params(sem):
    return pltpu.CompilerParams(dimension_semantics=sem, vmem_limit_bytes=VMEM_LIMIT)


_DIMS = {"nn": ((1,), (0,)), "nt": ((1,), (1,)), "tn": ((0,), (0,))}


def _bdot(a, b, mode):
    return lax.dot_general(a.astype(bf16), b.astype(bf16), (_DIMS[mode], ((), ())), preferred_element_type=f32)


def _matmul(a, b, mode, name, res=None, out_dtype=f32):
    if mode == "nn":
        (m, k), n = a.shape, b.shape[1]
    elif mode == "nt":
        (m, k), n = a.shape, b.shape[0]
    else:
        (k, m), n = a.shape, b.shape[1]
    tm = _pick(m, (640, 1408, 1024, 512, 384, 256, 128))
    tn = _pick(n, (1408, 2176, 1024, 512, 384, 256, 128))
    tk = _pick(k, (1408, 2176, 1024, 640, 512, 384, 256, 128))
    nk = k // tk
    a_spec = pl.BlockSpec((tk, tm), lambda i, j, kk: (kk, i)) if mode == "tn" else pl.BlockSpec((tm, tk), lambda i, j, kk: (i, kk))
    b_spec = pl.BlockSpec((tn, tk), lambda i, j, kk: (j, kk)) if mode == "nt" else pl.BlockSpec((tk, tn), lambda i, j, kk: (kk, j))
    o_spec = pl.BlockSpec((tm, tn), lambda i, j, kk: (i, j))
    has_res = res is not None

    def body(*refs):
        if has_res:
            a_ref, b_ref, r_ref, o_ref, acc = refs
        else:
            a_ref, b_ref, o_ref, acc = refs
        kk = pl.program_id(2)

        @pl.when(kk == 0)
        def _():
            acc[...] = jnp.zeros_like(acc)

        acc[...] += _bdot(a_ref[...], b_ref[...], mode)

        @pl.when(kk == nk - 1)
        def _():
            y = acc[...]
            if has_res:
                y = y + r_ref[...]
            o_ref[...] = y.astype(o_ref.dtype)

    ins = [a, b] + ([res] if has_res else [])
    specs = [a_spec, b_spec] + ([o_spec] if has_res else [])
    return pl.pallas_call(
        body, name=name, out_shape=jax.ShapeDtypeStruct((m, n), out_dtype), grid=(m // tm, n // tn, nk),
        in_specs=specs, out_specs=o_spec, scratch_shapes=[pltpu.VMEM((tm, tn), f32)],
        compiler_params=_params(("parallel", "parallel", "arbitrary")),
    )(*ins)


def _rowwise(fn, name, rows, tm, ins, outs, reds=()):
    n_in, n_out, n_red = len(ins), len(outs), len(reds)
    in_specs = []
    for arr, spec in ins:
        if spec is None:
            in_specs.append(pl.BlockSpec(arr.shape, lambda i, nd=arr.ndim: (0,) * nd))
        else:
            w, cb = spec
            in_specs.append(pl.BlockSpec((tm, w), lambda i, cb=cb: (i, cb)))
    out_specs = [pl.BlockSpec((tm, w), lambda i: (i, 0)) for w, _ in outs]
    out_specs += [pl.BlockSpec(s, lambda i, nd=len(s): (0,) * nd) for s in reds]
    out_shape = [jax.ShapeDtypeStruct((rows, w), dt) for w, dt in outs]
    out_shape += [jax.ShapeDtypeStruct(s, f32) for s in reds]

    def body(*refs):
        i = pl.program_id(0)
        vals = fn(i, *[r[...] for r in refs[:n_in]])
        for r, v in zip(refs[n_in:n_in + n_out], vals[:n_out]):
            r[...] = v.astype(r.dtype)
        red_refs = refs[n_in + n_out:]

        @pl.when(i == 0)
        def _():
            for r in red_refs:
                r[...] = jnp.zeros_like(r)

        for r, v in zip(red_refs, vals[n_out:]):
            r[...] += v

    res = pl.pallas_call(
        body, name=name, out_shape=out_shape, grid=(rows // tm,), in_specs=in_specs, out_specs=out_specs,
        compiler_params=_params(("arbitrary",)),
    )(*[a for a, _ in ins])
    return res


def _rms(x, g):
    return x * lax.rsqrt(jnp.mean(x * x, axis=-1, keepdims=True) + EPS) * g


def _row_mask(i, tm, lo, hi, shape):
    r = i * tm + lax.broadcasted_iota(i32, shape, 0)
    return (r >= lo) & (r < hi)


def _rms_fwd(x, g, name):
    rows, d = x.shape
    tm = _pick(rows, (640, 512, 384, 256, 128))
    return _rowwise(lambda i, xv, gv: (_rms(xv, gv),), name, rows, tm, [(x, (d, 0)), (g, None)], [(d, bf16)])[0]


def _rms_bwd(x, g, dn, res, name):
    rows, d = x.shape
    tm = _pick(rows, (640, 512, 384, 256, 128))

    def fn(i, xv, gv, dnv, rv):
        _, vjp = jax.vjp(_rms, xv, gv)
        dx, dg = vjp(dnv)
        return rv + dx, dg

    return _rowwise(fn, name, rows, tm, [(x, (d, 0)), (g, None), (dn, (d, 0)), (res, (d, 0))], [(d, f32)], [(1, d)])


def _swiglu(gate, up):
    return jax.nn.silu(gate) * up


def _swiglu_fwd(gu, name):
    rows, f2 = gu.shape
    f = f2 // 2
    tm = _pick(rows, (128,))
    return _rowwise(lambda i, a, b: (_swiglu(a, b),), name, rows, tm, [(gu, (f, 0)), (gu, (f, 1))], [(f, bf16)])[0]


def _swiglu_bwd(gu, dact, name):
    rows, f2 = gu.shape
    f = f2 // 2
    tm = _pick(rows, (128,))

    def fn(i, a, b, dv):
        _, vjp = jax.vjp(_swiglu, a, b)
        da, db = vjp(dv)
        return (jnp.concatenate([da, db], axis=1),)

    return _rowwise(fn, name, rows, tm, [(gu, (f, 0)), (gu, (f, 1)), (dact, (f, 0))], [(f2, bf16)])[0]


def _loss_head(h, g, tgt, lo, hi, name):
    rows, d = h.shape
    tm = _pick(rows, (640, 512, 384, 256, 128))

    def fn(i, hv, gv, tv):
        mask = _row_mask(i, tm, lo, hi, (tm, 1))

        def f(hh, gg):
            err = _rms(hh, gg) - tv
            per_row = jnp.where(mask, jnp.mean(err * err, axis=-1, keepdims=True), 0.0)
            return 0.5 * jnp.sum(per_row, axis=0, keepdims=True)

        loss, vjp = jax.vjp(f, hv, gv)
        dh, dg = vjp(jnp.ones_like(loss))
        return dh, dg, jnp.broadcast_to(loss, (1, LANES))

    return _rowwise(fn, name, rows, tm, [(h, (d, 0)), (g, None), (tgt, (d, 0))], [(d, f32)], [(1, d), (1, LANES)])


def _heads_l2(x):
    t = x.shape[0]
    x3 = x.reshape(t, GDN_HEADS, GDN_DIM)
    return (x3 * lax.rsqrt(jnp.sum(x3 * x3, axis=-1, keepdims=True) + EPS)).reshape(t, GDN_WIDTH)


def _gdn_act(conv, a_in, b_in, a_log, dt_bias, mask):
    s = jax.nn.silu(conv)
    q = _heads_l2(s[:, :GDN_WIDTH])
    k = _heads_l2(s[:, GDN_WIDTH:2 * GDN_WIDTH])
    v = s[:, 2 * GDN_WIDTH:]
    g = jnp.where(mask, -jnp.exp(a_log) * jax.nn.softplus(a_in + dt_bias), 0.0)
    beta = jnp.where(mask, jax.nn.sigmoid(b_in), 0.0)
    return q, k, v, g, beta


def _widen(x8):
    return jnp.concatenate([jnp.broadcast_to(x8[:, h:h + 1], (x8.shape[0], GDN_DIM)) for h in range(GDN_HEADS)], axis=1)


def _narrow(xw):
    t = xw.shape[0]
    lane = lax.broadcasted_iota(i32, (t, LANES), 1)
    out = jnp.zeros((t, LANES), f32)
    for h in range(GDN_HEADS):
        s = jnp.sum(xw[:, h * GDN_DIM:(h + 1) * GDN_DIM], axis=1, keepdims=True)
        out = out + jnp.where(lane == h, s, 0.0)
    return out


def _conv_taps(cur, prev8, w):
    tm = cur.shape[0]
    cat = jnp.concatenate([prev8, cur], axis=0)
    y = cur * w[CONV_WIDTH - 1:CONV_WIDTH, :]
    for j in range(1, CONV_WIDTH):
        y = y + pltpu.roll(cat, j, axis=0)[8:8 + tm, :] * w[CONV_WIDTH - 1 - j:CONV_WIDTH - j, :]
    return y


def _gdn_prep_specs(proj, tm):
    c3 = 3 * GDN_WIDTH
    ab = 4 * GDN_WIDTH // LANES
    t8 = tm // 8
    return [
        pl.BlockSpec((tm, c3), lambda i: (i, 0)),
        pl.BlockSpec((8, c3), lambda i: (jnp.maximum(i * t8 - 1, 0), 0)),
        pl.BlockSpec((tm, LANES), lambda i: (i, ab)),
        pl.BlockSpec((tm, LANES), lambda i: (i, ab + 1)),
    ]


def _full(arr):
    return pl.BlockSpec(arr.shape, lambda i, nd=arr.ndim: (0,) * nd)


def _gdn_prep_fwd(proj, conv_w, a_log, dt_bias, ltri, lo, hi, tm):
    rows = proj.shape[0]

    def body(cur, prev8, a_in, b_in, w, al, dtb, lt, q_o, k_o, v_o, g_o, b_o):
        i = pl.program_id(0)
        mask = _row_mask(i, tm, lo, hi, (tm, LANES)) & (lax.broadcasted_iota(i32, (tm, LANES), 1) < GDN_HEADS)
        conv = _conv_taps(cur[...], prev8[...], w[...])
        q, k, v, g, beta = _gdn_act(conv, a_in[...], b_in[...], al[...], dtb[...], mask)
        q_o[...] = q
        k_o[...] = k
        v_o[...] = v
        gcum = jnp.dot(lt[...], g, preferred_element_type=f32, precision=lax.Precision.HIGHEST)
        g_o[...] = _widen(gcum)
        b_o[...] = _widen(beta)

    wide = jax.ShapeDtypeStruct((rows, GDN_WIDTH), f32)
    o_spec = pl.BlockSpec((tm, GDN_WIDTH), lambda i: (i, 0))
    return pl.pallas_call(
        body, name="gdn_prep_fwd", out_shape=[wide] * 5, grid=(rows // tm,),
        in_specs=_gdn_prep_specs(proj, tm) + [_full(conv_w), _full(a_log), _full(dt_bias), _full(ltri)],
        out_specs=[o_spec] * 5, compiler_params=_params(("parallel",)),
    )(proj, proj, proj, proj, conv_w, a_log, dt_bias, ltri)


def _gdn_prep_bwd_act(proj, conv_w, a_log, dt_bias, ltri, dq, dk, dv, dgw, dbw, lo, hi, tm):
    rows = proj.shape[0]
    c3 = 3 * GDN_WIDTH

    def body(cur, prev8, a_in, b_in, w, al, dtb, lt, dq_r, dk_r, dv_r, dg_r, db_r, dconv_o, da_o, dbin_o, dal_o, ddt_o):
        i = pl.program_id(0)
        mask = _row_mask(i, tm, lo, hi, (tm, LANES)) & (lax.broadcasted_iota(i32, (tm, LANES), 1) < GDN_HEADS)
        conv = _conv_taps(cur[...], prev8[...], w[...])
        dgcum = _narrow(dg_r[...])
        dg = lax.dot_general(lt[...], dgcum, (((0,), (0,)), ((), ())), preferred_element_type=f32,
                             precision=lax.Precision.HIGHEST)
        dbeta = _narrow(db_r[...])
        _, vjp = jax.vjp(lambda c, a, b, x, y: _gdn_act(c, a, b, x, y, mask), conv, a_in[...], b_in[...], al[...], dtb[...])
        dconv, da, dbin, dal, ddt = vjp((dq_r[...], dk_r[...], dv_r[...], dg, dbeta))
        dconv_o[...] = dconv
        da_o[...] = da
        dbin_o[...] = dbin

        @pl.when(i == 0)
        def _():
            dal_o[...] = jnp.zeros_like(dal_o)
            ddt_o[...] = jnp.zeros_like(ddt_o)

        dal_o[...] += dal
        ddt_o[...] += ddt

    w_spec = pl.BlockSpec((tm, GDN_WIDTH), lambda i: (i, 0))
    n_spec = pl.BlockSpec((tm, LANES), lambda i: (i, 0))
    s_spec = pl.BlockSpec((1, LANES), lambda i: (0, 0))
    return pl.pallas_call(
        body, name="gdn_prep_bwd_act",
        out_shape=[jax.ShapeDtypeStruct((rows, c3), f32), jax.ShapeDtypeStruct((rows, LANES), f32),
                   jax.ShapeDtypeStruct((rows, LANES), f32), jax.ShapeDtypeStruct((1, LANES), f32),
                   jax.ShapeDtypeStruct((1, LANES), f32)],
        grid=(rows // tm,),
        in_specs=_gdn_prep_specs(proj, tm) + [_full(conv_w), _full(a_log), _full(dt_bias), _full(ltri)] + [w_spec] * 5,
        out_specs=[pl.BlockSpec((tm, c3), lambda i: (i, 0)), n_spec, n_spec, s_spec, s_spec],
        compiler_params=_params(("arbitrary",)),
    )(proj, proj, proj, proj, conv_w, a_log, dt_bias, ltri, dq, dk, dv, dgw, dbw)


def _gdn_prep_bwd_conv(proj, conv_w, dconv, dgate, da, dbin, tm):
    rows, width = proj.shape
    c3 = 3 * GDN_WIDTH
    t8 = tm // 8
    nt = rows // tm

    def body(cur, prev8, w, dc, dnext8, dgt, da_r, db_r, dp_o, dw_o):
        i = pl.program_id(0)
        d = dc[...]
        nxt = jnp.where(i == nt - 1, 0.0, dnext8[...])
        cat = jnp.concatenate([d, nxt], axis=0)
        wv = w[...]
        dx = d * wv[CONV_WIDTH - 1:CONV_WIDTH, :]
        for j in range(1, CONV_WIDTH):
            dx = dx + pltpu.roll(cat, tm + 8 - j, axis=0)[:tm, :] * wv[CONV_WIDTH - 1 - j:CONV_WIDTH - j, :]
        dp_o[:, :c3] = dx.astype(bf16)
        dp_o[:, c3:4 * GDN_WIDTH] = dgt[...].astype(bf16)
        dp_o[:, 4 * GDN_WIDTH:4 * GDN_WIDTH + LANES] = da_r[...].astype(bf16)
        dp_o[:, 4 * GDN_WIDTH + LANES:] = db_r[...].astype(bf16)

        xcat = jnp.concatenate([prev8[...], cur[...]], axis=0)
        parts = [jnp.sum(d * cur[...], axis=0, keepdims=True)]
        for j in range(1, CONV_WIDTH):
            parts.append(jnp.sum(d * pltpu.roll(xcat, j, axis=0)[8:8 + tm, :], axis=0, keepdims=True))
        dwt = jnp.concatenate(parts[::-1], axis=0)

        @pl.when(i == 0)
        def _():
            dw_o[...] = jnp.zeros_like(dw_o)

        dw_o[...] += dwt

    n_spec = pl.BlockSpec((tm, LANES), lambda i: (i, 0))
    return pl.pallas_call(
        body, name="gdn_prep_bwd_conv",
        out_shape=[jax.ShapeDtypeStruct((rows, width), bf16), jax.ShapeDtypeStruct((CONV_WIDTH, c3), f32)],
        grid=(nt,),
        in_specs=[pl.BlockSpec((tm, c3), lambda i: (i, 0)),
                  pl.BlockSpec((8, c3), lambda i: (jnp.maximum(i * t8 - 1, 0), 0)),
                  _full(conv_w),
                  pl.BlockSpec((tm, c3), lambda i: (i, 0)),
                  pl.BlockSpec((8, c3), lambda i: (jnp.minimum((i + 1) * t8, rows // 8 - 1), 0)),
                  pl.BlockSpec((tm, GDN_WIDTH), lambda i: (i, 0)), n_spec, n_spec],
        out_specs=[pl.BlockSpec((tm, width), lambda i: (i, 0)), pl.BlockSpec((CONV_WIDTH, c3), lambda i: (0, 0))],
        compiler_params=_params(("arbitrary",)),
    )(proj, proj, conv_w, dconv, dconv, dgate, da, dbin)


def _split(a):
    hi = a.astype(bf16)
    return hi, (a - hi.astype(f32)).astype(bf16)


def _make_mm(dot):
    @jax.custom_vjp
    def nn(a, b):
        return dot(a, b, "nn")

    nn.defvjp(lambda a, b: (dot(a, b, "nn"), (a, b)),
              lambda r, ct: (dot(ct, r[1], "nt"), dot(r[0], ct, "tn")))

    @jax.custom_vjp
    def nt(a, b):
        return dot(a, b, "nt")

    nt.defvjp(lambda a, b: (dot(a, b, "nt"), (a, b)),
              lambda r, ct: (dot(ct, r[1], "nn"), dot(ct, r[0], "tn")))

    @jax.custom_vjp
    def tn(a, b):
        return dot(a, b, "tn")

    tn.defvjp(lambda a, b: (dot(a, b, "tn"), (a, b)),
              lambda r, ct: (dot(r[1], ct, "nt"), dot(r[0], ct, "nn")))
    return nn, nt, tn


_mm, _mm_nt, _mm_tn = _make_mm(_bdot)


def _each(f, *lists):
    return [f(*xs) for xs in zip(*lists)]


def _gdn_chunk(q, k, v, gcb, bcb, s_in):
    c = q[0].shape[0]
    ri = lax.broadcasted_iota(i32, (c, c), 0)
    ci = lax.broadcasted_iota(i32, (c, c), 1)
    incl, strict = ri >= ci, ri > ci
    rowi = lax.broadcasted_iota(i32, gcb[0].shape, 0)
    qs = _each(lambda t: t * (GDN_DIM ** -0.5), q)
    decay = _each(lambda g: jnp.where(incl, jnp.exp(jnp.where(incl, g[:, :c] - g[:, :c].T, 0.0)), 0.0), gcb)
    kk = _each(lambda t: _mm_nt(t, t), k)
    a1 = _each(lambda b, d, t: jnp.where(strict, b[:, :c] * d * t, 0.0), bcb, decay, kk)
    eg = _each(jnp.exp, gcb)
    x = _each(lambda b, vv, e, t: jnp.concatenate([b * vv, (b * e) * t], axis=1), bcb, v, eg, k)
    pows = [a1]
    for _ in range(5):
        pows.append(_each(lambda p: _mm(p, p), pows[-1]))
    for ps in pows[:0:-1]:
        x = _each(lambda p, t: t + _mm(p, t), ps, x)
    x = _each(lambda p, t: t - _mm(p, t), a1, x)
    attn = _each(lambda a, b, d: _mm_nt(a, b) * d, qs, k, decay)
    glast = _each(lambda g: jnp.sum(jnp.where(rowi == c - 1, g, 0.0), axis=0, keepdims=True), gcb)
    u = _each(lambda t, s: t[:, :GDN_DIM] - _mm(t[:, GDN_DIM:], s), x, s_in)
    o = _each(lambda a, e, s, w, uu: _mm(a * e, s) + _mm(w, uu), qs, eg, s_in, attn, u)
    s_out = _each(lambda s, gl, t, g, uu: s * jnp.exp(gl) + _mm_tn(t * jnp.exp(gl - g), uu), s_in, glast, k, gcb, u)
    return o, s_out


def _gdn_heads(ref):
    return [ref[:, h * GDN_DIM:(h + 1) * GDN_DIM] for h in range(GDN_HEADS)]


def _gdn_fwd(q, k, v, gw, bw, shards):
    rows = q.shape[0]
    nc = rows // CHUNK
    num = len(shards)
    blk = pl.BlockSpec((CHUNK, GDN_WIDTH), lambda c: (c, 0))

    def body(*refs):
        q_r, k_r, v_r, g_r, b_r = refs[:5]
        ins = refs[5:5 + num]
        o_r, st_r = refs[5 + num:7 + num]
        outs = refs[7 + num:7 + 2 * num]
        s_sc, send_sems, recv_sems, local_sems = refs[7 + 2 * num:]
        c = pl.program_id(0)
        start, forward, finish = _gather_plan(ins, outs, send_sems, recv_sems, local_sems)

        @pl.when(c == 0)
        def _():
            s_sc[...] = jnp.zeros_like(s_sc)
            start()

        s_in = [s_sc[h] for h in range(GDN_HEADS)]
        st_r[0] = s_sc[...]
        o, s_out = _gdn_chunk(_gdn_heads(q_r), _gdn_heads(k_r), _gdn_heads(v_r), _gdn_heads(g_r), _gdn_heads(b_r), s_in)
        o_r[...] = jnp.concatenate(o, axis=1)
        for h in range(GDN_HEADS):
            s_sc[h] = s_out[h]
        pl.when(c == nc // 2)(forward)
        pl.when(c == nc - 1)(finish)

    res = pl.pallas_call(
        body, name="gdn_fwd",
        out_shape=[jax.ShapeDtypeStruct((rows, GDN_WIDTH), f32), jax.ShapeDtypeStruct((nc, GDN_HEADS, GDN_DIM, GDN_DIM), f32)]
        + [jax.ShapeDtypeStruct((N_CHIPS,) + t.shape, t.dtype) for t in shards],
        grid=(nc,), in_specs=[blk] * 5 + [_ANY] * num,
        out_specs=[blk, pl.BlockSpec((1, GDN_HEADS, GDN_DIM, GDN_DIM), lambda c: (c, 0, 0, 0))] + [_ANY] * num,
        scratch_shapes=[pltpu.VMEM((GDN_HEADS, GDN_DIM, GDN_DIM), f32), pltpu.SemaphoreType.DMA((6 * num,)),
                        pltpu.SemaphoreType.DMA((6 * num,)), pltpu.SemaphoreType.DMA((num,))],
        compiler_params=_params(("arbitrary",)),
    )(q, k, v, gw, bw, *shards)
    return res[0], res[1], res[2:]


def _gdn_bwd(q, k, v, gw, bw, states, do, parts):
    rows = q.shape[0]
    nc = rows // CHUNK
    num = len(parts)
    blk = pl.BlockSpec((CHUNK, GDN_WIDTH), lambda c: (nc - 1 - c, 0))

    def body(*refs):
        q_r, k_r, v_r, g_r, b_r, st_r, do_r = refs[:7]
        ins = refs[7:7 + num]
        dq_r, dk_r, dv_r, dg_r, db_r = refs[7 + num:12 + num]
        outs = refs[12 + num:12 + 2 * num]
        ds_sc, send_sems, recv_sems = refs[12 + 2 * num:]
        c = pl.program_id(0)
        start, finish = _scatter_plan(ins, outs, send_sems, recv_sems)

        @pl.when(c == 0)
        def _():
            ds_sc[...] = jnp.zeros_like(ds_sc)
            start()

        s_in = [st_r[0, h] for h in range(GDN_HEADS)]
        _, vjp = jax.vjp(_gdn_chunk, _gdn_heads(q_r), _gdn_heads(k_r), _gdn_heads(v_r), _gdn_heads(g_r), _gdn_heads(b_r), s_in)
        dq, dk, dv, dg, db, ds_in = vjp((_gdn_heads(do_r), [ds_sc[h] for h in range(GDN_HEADS)]))
        dq_r[...] = jnp.concatenate(dq, axis=1)
        dk_r[...] = jnp.concatenate(dk, axis=1)
        dv_r[...] = jnp.concatenate(dv, axis=1)
        dg_r[...] = jnp.concatenate(dg, axis=1)
        db_r[...] = jnp.concatenate(db, axis=1)
        for h in range(GDN_HEADS):
            ds_sc[h] = ds_in[h]
        pl.when(c == nc - 1)(finish)

    wide = jax.ShapeDtypeStruct((rows, GDN_WIDTH), f32)
    res = pl.pallas_call(
        body, name="gdn_bwd", out_shape=[wide] * 5 + [jax.ShapeDtypeStruct((3,) + t.shape[1:], t.dtype) for t in parts],
        grid=(nc,),
        in_specs=[blk] * 5 + [pl.BlockSpec((1, GDN_HEADS, GDN_DIM, GDN_DIM), lambda c: (nc - 1 - c, 0, 0, 0)), blk] + [_ANY] * num,
        out_specs=[blk] * 5 + [_ANY] * num,
        scratch_shapes=[pltpu.VMEM((GDN_HEADS, GDN_DIM, GDN_DIM), f32), pltpu.SemaphoreType.DMA((3 * num,)),
                        pltpu.SemaphoreType.DMA((3 * num,))],
        compiler_params=_params(("arbitrary",)),
    )(q, k, v, gw, bw, states, do, *parts)
    return res[:5], res[5:]


def _gdn_gate(o, gate, og):
    t = o.shape[0]
    o3 = o.reshape(t, GDN_HEADS, GDN_DIM)
    n = o3 * lax.rsqrt(jnp.mean(o3 * o3, axis=-1, keepdims=True) + EPS) * og.reshape(1, 1, GDN_DIM)
    return n.reshape(t, GDN_WIDTH) * jax.nn.silu(gate)


def _gdn_gate_fwd(o, proj, og, tm):
    rows = o.shape[0]
    return _rowwise(lambda i, ov, gv, w: (_gdn_gate(ov, gv, w),), "gdn_gate_fwd", rows, tm,
                    [(o, (GDN_WIDTH, 0)), (proj, (GDN_WIDTH, 3)), (og, None)], [(GDN_WIDTH, bf16)])[0]


def _gdn_gate_bwd(o, proj, og, dy, tm):
    rows = o.shape[0]

    def fn(i, ov, gv, w, d):
        _, vjp = jax.vjp(_gdn_gate, ov, gv, w)
        return vjp(d)

    return _rowwise(fn, "gdn_gate_bwd", rows, tm,
                    [(o, (GDN_WIDTH, 0)), (proj, (GDN_WIDTH, 3)), (og, None), (dy, (GDN_WIDTH, 0))],
                    [(GDN_WIDTH, f32), (GDN_WIDTH, f32)], [(1, GDN_DIM)])


def _sb_visible(i, j, valid):
    qpos = i * SB_BLOCK + lax.broadcasted_iota(i32, (SB_BLOCK, SB_BLOCK), 0)
    kpos = j * SB_BLOCK + lax.broadcasted_iota(i32, (SB_BLOCK, SB_BLOCK), 1)
    return (kpos < qpos) & (kpos >= FRONT) & valid


def _sb_logs(z, vis):
    l1p = jnp.log1p(jnp.exp(-jnp.abs(z)))
    return -(jnp.maximum(-z, 0.0) + l1p), jnp.where(vis, -(jnp.maximum(z, 0.0) + l1p), 0.0)


def _tri_sum(x, tri):
    hi, lo = _split(x)
    return jnp.dot(hi, tri, preferred_element_type=f32) + jnp.dot(lo, tri, preferred_element_type=f32)


def _sb_live(t, i, runs):
    return (t <= i) & (jnp.max(functools.reduce(jnp.maximum, runs)) > -SB_UNDERFLOW)


def _sb_blocks(i, t, nb):
    js = [i - t - b for b in range(nb)]
    kss = [pl.ds(pl.multiple_of(jnp.maximum(j, 0) * SB_BLOCK, SB_BLOCK), SB_BLOCK) for j in js]
    return kss, [_sb_visible(i, j, j >= 0) for j in js]


def _sb_weights(i, t, nb, qs, sls, k_r, runs, after, scale):
    nh = len(qs)
    kss, vis = _sb_blocks(i, t, nb)
    units = [(a, b) for b in range(nb) for a in range(nh)]
    z = [_bdot(qs[a], k_r[kss[b], sls[a]], "nt") * scale for a, b in units]
    logs = [_sb_logs(zz, vis[b]) for zz, (a, b) in zip(z, units)]
    later = [_tri_sum(l[1], after) for l in logs]
    sums = [jnp.sum(l[1], axis=1, keepdims=True) for l in logs]
    w = []
    runs = list(runs)
    for b in range(nb):
        for a in range(nh):
            u = b * nh + a
            w.append(jnp.where(vis[b], jnp.exp(logs[u][0] + later[u] + runs[a]), 0.0))
        runs = [runs[a] + sums[b * nh + a] for a in range(nh)]
    return kss, vis, units, z, w, tuple(runs)


def _sb_fwd(q, kv, width):
    rows = q.shape[0]
    nq = rows // SB_BLOCK
    lanes = SB_FWD_HEADS * SB_DIM
    npair = width // lanes
    scale = SB_DIM ** -0.5

    def body(q_r, k_r, v_r, o_r):
        i = pl.program_id(1)
        rj = lax.broadcasted_iota(i32, (SB_BLOCK, SB_BLOCK), 0)
        cs = lax.broadcasted_iota(i32, (SB_BLOCK, SB_BLOCK), 1)
        after = (rj > cs).astype(bf16)
        sls = [slice(a * SB_DIM, (a + 1) * SB_DIM) for a in range(SB_FWD_HEADS)]
        qs = [q_r[:, sl] for sl in sls]

        def step(carry, nb):
            t, accs, runs = carry
            kss, _, units, _, w, runs = _sb_weights(i, t, nb, qs, sls, k_r, runs, after, scale)
            prods = [_bdot(ww, v_r[kss[b], sls[a]], "nn") for ww, (a, b) in zip(w, units)]
            accs = tuple(functools.reduce(jnp.add, [accs[a]] + prods[a::SB_FWD_HEADS]) for a in range(SB_FWD_HEADS))
            return t + nb, accs, runs

        init = (jnp.int32(0), tuple(jnp.zeros((SB_BLOCK, SB_DIM), f32) for _ in sls),
                tuple(jnp.zeros((SB_BLOCK, 1), f32) for _ in sls))
        _, accs, _ = lax.while_loop(lambda c: _sb_live(c[0], i, c[2]), lambda c: step(c, 2), step(init, 1))
        o_r[...] = jnp.concatenate(accs, axis=1)

    return pl.pallas_call(
        body, name="sb_fwd", out_shape=jax.ShapeDtypeStruct((rows, width), f32), grid=(npair, nq),
        in_specs=[pl.BlockSpec((SB_BLOCK, lanes), lambda p, i: (i, p)),
                  pl.BlockSpec((rows, lanes), lambda p, i: (0, p)),
                  pl.BlockSpec((rows, lanes), lambda p, i: (0, npair + p))],
        out_specs=pl.BlockSpec((SB_BLOCK, lanes), lambda p, i: (i, p)),
        compiler_params=_params(("parallel", "arbitrary")),
    )(q, kv, kv)


def _sb_bwd(q, kv, do, width):
    rows = q.shape[0]
    nq = rows // SB_BLOCK
    npair = width // LANES
    nh = LANES // SB_DIM
    scale = SB_DIM ** -0.5

    def body(q_r, k_r, v_r, do_r, dq_r, dk_r, dv_r, e_sc, sig_sc, w_sc):
        i = pl.program_id(1)

        @pl.when(i == 0)
        def _():
            dk_r[...] = jnp.zeros_like(dk_r)
            dv_r[...] = jnp.zeros_like(dv_r)

        rj = lax.broadcasted_iota(i32, (SB_BLOCK, SB_BLOCK), 0)
        cs = lax.broadcasted_iota(i32, (SB_BLOCK, SB_BLOCK), 1)
        after = (rj > cs).astype(bf16)
        from_s = (rj >= cs).astype(bf16)
        zero1 = jnp.zeros((SB_BLOCK, 1), f32)
        sls = [slice(a * SB_DIM, (a + 1) * SB_DIM) for a in range(nh)]
        qs = [q_r[:, sl] for sl in sls]
        dos = [do_r[:, sl] for sl in sls]

        def weigh(carry, nb):
            t, runs, eruns = carry
            kss, _, units, z, w, runs = _sb_weights(i, t, nb, qs, sls, k_r, runs, after, scale)
            dw = [_bdot(dos[a], v_r[kss[b], sls[a]], "nt") for a, b in units]
            e = [ww * d for ww, d in zip(w, dw)]
            for u, (a, b) in enumerate(units):
                e_sc[a, t + b] = e[u]
                sig_sc[a, t + b] = jax.nn.sigmoid(z[u])
                w_sc[a, t + b] = w[u].astype(w_sc.dtype)
            sums = [jnp.sum(ee, axis=1, keepdims=True) for ee in e]
            eruns = tuple(functools.reduce(jnp.add, [eruns[a]] + sums[a::nh]) for a in range(nh))
            return t + nb, runs, eruns

        n_blk, _, etots = lax.while_loop(lambda c: _sb_live(c[0], i, c[1]), lambda c: weigh(c, 2),
                                         weigh((jnp.int32(0), (zero1,) * nh, (zero1,) * nh), 1))

        def push(t, carry, nb):
            dqs, eruns = carry
            kss, vis = _sb_blocks(i, t, nb)
            units = [(a, b) for b in range(nb) for a in range(nh)]
            e = [e_sc[a, t + b] for a, b in units]
            dvs = [_bdot(w_sc[a, t + b], dos[a], "tn") for a, b in units]
            upto = [_tri_sum(ee, from_s) for ee in e]
            sums = [jnp.sum(ee, axis=1, keepdims=True) for ee in e]
            dz = []
            eruns = list(eruns)
            for b in range(nb):
                for a in range(nh):
                    u = b * nh + a
                    sig = sig_sc[a, t + b]
                    before = etots[a] - eruns[a] - upto[u]
                    dz.append(jnp.where(vis[b], e[u] * (1.0 - sig) - before * sig, 0.0) * scale)
                eruns = [eruns[a] + sums[b * nh + a] for a in range(nh)]
            dks = [_bdot(d, qs[a], "tn") for d, (a, b) in zip(dz, units)]
            dqp = [_bdot(d, k_r[kss[b], sls[a]], "nn") for d, (a, b) in zip(dz, units)]
            for b in range(nb):
                dk_r[kss[b], :] += jnp.concatenate(dks[b * nh:(b + 1) * nh], axis=1)
                dv_r[kss[b], :] += jnp.concatenate(dvs[b * nh:(b + 1) * nh], axis=1)
            dqs = tuple(functools.reduce(jnp.add, [dqs[a]] + dqp[a::nh]) for a in range(nh))
            return dqs, tuple(eruns)

        first = push(jnp.int32(0), (tuple(jnp.zeros((SB_BLOCK, SB_DIM), f32) for _ in sls), (zero1,) * nh), 1)
        dqs, _ = lax.fori_loop(0, (n_blk - 1) // 2, lambda p, c: push(1 + 2 * p, c, 2), first)
        dq_r[...] = jnp.concatenate(dqs, axis=1)

    blk = pl.BlockSpec((SB_BLOCK, LANES), lambda p, i: (i, p))
    col = pl.BlockSpec((rows, LANES), lambda p, i: (0, p))
    wide = jax.ShapeDtypeStruct((rows, width), f32)
    depth = nq + 1
    return pl.pallas_call(
        body, name="sb_bwd", out_shape=[wide] * 3, grid=(npair, nq),
        in_specs=[blk, col, pl.BlockSpec((rows, LANES), lambda p, i: (0, npair + p)), blk],
        out_specs=[blk, col, col],
        scratch_shapes=[pltpu.VMEM((nh, depth, SB_BLOCK, SB_BLOCK), f32), pltpu.VMEM((nh, depth, SB_BLOCK, SB_BLOCK), f32),
                        pltpu.VMEM((nh, depth, SB_BLOCK, SB_BLOCK), bf16)],
        compiler_params=_params(("parallel", "arbitrary")),
    )(q, kv, kv, do)


_FLIPS = ((1, 0), (0, 1), (1, 1))
_ANY = pl.BlockSpec(memory_space=pl.ANY)


def _flip(v, a):
    return v + a - 2 * a * v


def _gather_plan(ins, outs, send_sems, recv_sems, local_sems):
    num = len(ins)
    x, y, c = lax.axis_index("x"), lax.axis_index("y"), lax.axis_index("c")
    me, sibling = (x, y, c), (x, y, 1 - c)
    chip = 2 * x + y
    others = [(_flip(x, a), _flip(y, b)) for a, b in _FLIPS]
    pairs = [(k, n, 2 * ox + oy) for k in range(num) for n, (ox, oy) in enumerate(others)]

    def half_of(ref, hc):
        half = ref.shape[0] // 2
        start = hc * half
        for align in (16, 8):
            if half % align == 0:
                start = pl.multiple_of(start, align)
                break
        return ref.at[pl.ds(start, half)]

    def copy(k, n, s, hc, to, src=None):
        dst = half_of(outs[k].at[s], hc)
        return pltpu.make_async_remote_copy(
            src_ref=dst if src is None else src, dst_ref=dst,
            send_sem=send_sems.at[6 * k + n], recv_sem=recv_sems.at[6 * k + n], device_id=to, device_id_type=MESH)

    mine = [pltpu.make_async_copy(ins[k], outs[k].at[chip], local_sems.at[k]) for k in range(num)]
    first = [copy(k, n, chip, c, (others[n][0], others[n][1], c), src=half_of(ins[k], c)) for k, n, _ in pairs]
    passed = [copy(k, 3 + n, s, c, sibling) for k, n, s in pairs]

    def start():
        for cp in mine + first:
            cp.start()

    def forward():
        for (k, n, s), fw in zip(pairs, passed):
            copy(k, n, s, c, me).wait_recv()
            fw.start()

    def finish():
        for k, n, s in pairs:
            copy(k, 3 + n, s, 1 - c, me).wait_recv()
        for cp in first + passed:
            cp.wait_send()
        for cp in mine:
            cp.wait()

    return start, forward, finish


def _gather_chips(shards):
    num = len(shards)

    def body(*refs):
        for phase in _gather_plan(refs[:num], refs[num:2 * num], *refs[2 * num:]):
            phase()

    return pl.pallas_call(
        body, name="gather_chips", out_shape=[jax.ShapeDtypeStruct((N_CHIPS,) + t.shape, t.dtype) for t in shards],
        in_specs=[_ANY] * num, out_specs=[_ANY] * num,
        scratch_shapes=[pltpu.SemaphoreType.DMA((6 * num,)), pltpu.SemaphoreType.DMA((6 * num,)),
                        pltpu.SemaphoreType.DMA((num,))],
    )(*shards)


def _scatter_plan(ins, outs, send_sems, recv_sems):
    x, y, c = lax.axis_index("x"), lax.axis_index("y"), lax.axis_index("c")
    cps = []
    for k in range(len(ins)):
        for n, (a, b) in enumerate(_FLIPS):
            ox, oy = _flip(x, a), _flip(y, b)
            cps.append(pltpu.make_async_remote_copy(
                src_ref=ins[k].at[2 * ox + oy], dst_ref=outs[k].at[n], send_sem=send_sems.at[3 * k + n],
                recv_sem=recv_sems.at[3 * k + n], device_id=(ox, oy, c), device_id_type=MESH))

    def start():
        for cp in cps:
            cp.start()

    def finish():
        for cp in cps:
            cp.wait()

    return start, finish


def _scatter_chips(parts):
    num = len(parts)

    def body(*refs):
        for phase in _scatter_plan(refs[:num], refs[num:2 * num], *refs[2 * num:]):
            phase()

    return pl.pallas_call(
        body, name="scatter_chips", out_shape=[jax.ShapeDtypeStruct((3,) + t.shape[1:], t.dtype) for t in parts],
        in_specs=[_ANY] * num, out_specs=[_ANY] * num,
        scratch_shapes=[pltpu.SemaphoreType.DMA((3 * num,)), pltpu.SemaphoreType.DMA((3 * num,))],
    )(*parts)


def _swap_sibling(arrs):
    num = len(arrs)

    def body(*refs):
        ins, outs = refs[:num], refs[num:2 * num]
        send_sems, recv_sems = refs[2 * num:]
        x, y, c = lax.axis_index("x"), lax.axis_index("y"), lax.axis_index("c")
        cps = [pltpu.make_async_remote_copy(src_ref=ins[k], dst_ref=outs[k], send_sem=send_sems.at[k],
                                            recv_sem=recv_sems.at[k], device_id=(x, y, 1 - c), device_id_type=MESH)
               for k in range(num)]
        for cp in cps:
            cp.start()
        for cp in cps:
            cp.wait()

    return pl.pallas_call(
        body, name="swap_sibling", out_shape=[jax.ShapeDtypeStruct(t.shape, t.dtype) for t in arrs],
        in_specs=[_ANY] * num, out_specs=[_ANY] * num,
        scratch_shapes=[pltpu.SemaphoreType.DMA((num,)), pltpu.SemaphoreType.DMA((num,))],
    )(*arrs)


def _gather_all(v):
    m_per, n = v.shape

    def body(x_ref, out_ref, send_sems, recv_sems, local_sem):
        x, y, c = lax.axis_index("x"), lax.axis_index("y"), lax.axis_index("c")
        me, sibling = (x, y, c), (x, y, 1 - c)
        chips = [(_flip(x, a), _flip(y, b)) for a, b in _FLIPS]

        def rows(px, py, pc):
            return out_ref.at[pl.ds(pl.multiple_of((4 * px + 2 * py + pc) * m_per, 8), m_per), :]

        def copy(k, block, to, src=None):
            return pltpu.make_async_remote_copy(
                src_ref=rows(*block) if src is None else src, dst_ref=rows(*block),
                send_sem=send_sems.at[k], recv_sem=recv_sems.at[k], device_id=to, device_id_type=MESH)

        mine = pltpu.make_async_copy(x_ref, rows(*me), local_sem)
        mine.start()
        first = [copy(0, me, sibling, src=x_ref)]
        first += [copy(1 + j, me, (*chip, c), src=x_ref) for j, chip in enumerate(chips)]
        for cp in first:
            cp.start()
        passed = [copy(4 + j, (*chip, c), sibling) for j, chip in enumerate(chips)]
        for j, chip in enumerate(chips):
            copy(1 + j, (*chip, c), me).wait_recv()
            passed[j].start()
        copy(0, sibling, me).wait_recv()
        for j, chip in enumerate(chips):
            copy(4 + j, (*chip, 1 - c), me).wait_recv()
        for cp in first + passed:
            cp.wait_send()
        mine.wait()

    return pl.pallas_call(
        body, name="gather_all", out_shape=jax.ShapeDtypeStruct((N_DEV * m_per, n), v.dtype),
        in_specs=[pl.BlockSpec(memory_space=pltpu.VMEM)], out_specs=pl.BlockSpec(memory_space=pltpu.VMEM),
        scratch_shapes=[pltpu.SemaphoreType.DMA((7,)), pltpu.SemaphoreType.DMA((7,)), pltpu.SemaphoreType.DMA],
    )(v)


def _sum_chips(parts, got, chip, name):
    cols = parts.shape[-1]
    rows = parts.size // (N_CHIPS * cols)
    tm = _pick(rows, (256, 128, 64, 32, 16))

    def body(chip_r, own_r, got_r, o_r):
        acc = own_r[0]
        for n in range(3):
            acc = acc + got_r[n].astype(f32)
        o_r[...] = acc

    return pl.pallas_call(
        body, name=name, out_shape=jax.ShapeDtypeStruct((rows, cols), f32),
        grid_spec=pltpu.PrefetchScalarGridSpec(
            num_scalar_prefetch=1, grid=(rows // tm,),
            in_specs=[pl.BlockSpec((1, tm, cols), lambda i, s: (s[0], i, 0)),
                      pl.BlockSpec((3, tm, cols), lambda i, s: (0, i, 0))],
            out_specs=pl.BlockSpec((tm, cols), lambda i, s: (i, 0))),
        compiler_params=_params(("parallel",)),
    )(chip, parts.reshape(N_CHIPS, rows, cols), got.reshape(3, rows, cols))


def _sum_devices(g, m_per):
    n = g.shape[1]

    def body(g_r, o_r):
        acc = g_r[0:m_per, :]
        for d in range(1, N_DEV):
            acc = acc + g_r[d * m_per:(d + 1) * m_per, :]
        o_r[...] = acc

    return pl.pallas_call(body, name="sum_devices", out_shape=jax.ShapeDtypeStruct((m_per, n), f32))(g)


def _adamw(w, gs, m, v, name):
    shape = w.shape
    cols = shape[-1]
    rows = w.size // cols
    tm = _pick(rows, (256, 128, 64, 32, 16, 8)) if rows * cols * 4 > (1 << 20) else rows

    def fn(i, wv, mv, vv, *gv):
        g = functools.reduce(jnp.add, gv)
        mn = ADAM_B1 * mv + (1.0 - ADAM_B1) * g
        vn = ADAM_B2 * vv + (1.0 - ADAM_B2) * jnp.square(g)
        m_hat = mn / (1.0 - ADAM_B1 ** ADAM_STEP)
        v_hat = vn / (1.0 - ADAM_B2 ** ADAM_STEP)
        delta = -ADAM_LR * (m_hat / (jnp.sqrt(v_hat) + ADAM_EPS) + ADAM_WD * wv)
        return g, delta, mn, vn

    outs = _rowwise(fn, name, rows, tm, [(t.reshape(rows, cols), (cols, 0)) for t in (w, m, v) + tuple(gs)], [(cols, f32)] * 4)
    return tuple(o.reshape(shape) for o in outs)


def _pack(pieces, rows, dtype):
    flat = jnp.concatenate([p.reshape(-1).astype(dtype) for p in pieces])
    return jnp.pad(flat, (0, rows * PACK_COLS - flat.size)).reshape(rows, PACK_COLS)


def _unpack(buf, shapes):
    lead = buf.shape[:-2]
    flat = buf.reshape(lead + (-1,))
    out, off = [], 0
    for s in shapes:
        n = 1
        for d in s:
            n *= d
        out.append(flat[..., off:off + n].reshape(lead + tuple(s)))
        off += n
    return out


def _join_cols(t):
    return jnp.moveaxis(t, 0, -2).reshape(t.shape[1:-1] + (N_CHIPS * t.shape[-1],))


def _join_rows(t):
    return t.reshape((N_CHIPS * t.shape[1],) + t.shape[2:])


def _split_cols(t):
    r, c4 = t.shape
    return jnp.moveaxis(t.reshape(r, N_CHIPS, c4 // N_CHIPS), 1, 0)


def _split_rows(t):
    return t.reshape((N_CHIPS, t.shape[0] // N_CHIPS) + t.shape[1:])


def kernel(x, meta_tokens, gdn_norm_g, gdn_w_in, gdn_conv_w, gdn_a_log, gdn_dt_bias, gdn_onorm_g, gdn_w_out, kv_norm_g, w_kv, sb_norm_g, sb_w_q, sb_w_o, ffn_norm_g, ffn_w_gate_up, ffn_w_down, final_norm_g, loss_target, m_meta_tokens, m_gdn_norm_g, m_gdn_w_in, m_gdn_conv_w, m_gdn_a_log, m_gdn_dt_bias, m_gdn_onorm_g, m_gdn_w_out, m_kv_norm_g, m_w_kv, m_sb_norm_g, m_sb_w_q, m_sb_w_o, m_ffn_norm_g, m_ffn_w_gate_up, m_ffn_w_down, m_final_norm_g, v_meta_tokens, v_gdn_norm_g, v_gdn_w_in, v_gdn_conv_w, v_gdn_a_log, v_gdn_dt_bias, v_gdn_onorm_g, v_gdn_w_out, v_kv_norm_g, v_w_kv, v_sb_norm_g, v_sb_w_q, v_sb_w_o, v_ffn_norm_g, v_ffn_w_gate_up, v_ffn_w_down, v_final_norm_g):
    weights = dict(meta_tokens=meta_tokens, gdn_norm_g=gdn_norm_g, gdn_w_in=gdn_w_in, gdn_conv_w=gdn_conv_w,
                   gdn_a_log=gdn_a_log, gdn_dt_bias=gdn_dt_bias, gdn_onorm_g=gdn_onorm_g, gdn_w_out=gdn_w_out,
                   kv_norm_g=kv_norm_g, w_kv=w_kv, sb_norm_g=sb_norm_g, sb_w_q=sb_w_q, sb_w_o=sb_w_o,
                   ffn_norm_g=ffn_norm_g, ffn_w_gate_up=ffn_w_gate_up, ffn_w_down=ffn_w_down, final_norm_g=final_norm_g)
    m_in = dict(meta_tokens=m_meta_tokens, gdn_norm_g=m_gdn_norm_g, gdn_w_in=m_gdn_w_in, gdn_conv_w=m_gdn_conv_w,
                gdn_a_log=m_gdn_a_log, gdn_dt_bias=m_gdn_dt_bias, gdn_onorm_g=m_gdn_onorm_g, gdn_w_out=m_gdn_w_out,
                kv_norm_g=m_kv_norm_g, w_kv=m_w_kv, sb_norm_g=m_sb_norm_g, sb_w_q=m_sb_w_q, sb_w_o=m_sb_w_o,
                ffn_norm_g=m_ffn_norm_g, ffn_w_gate_up=m_ffn_w_gate_up, ffn_w_down=m_ffn_w_down, final_norm_g=m_final_norm_g)
    v_in = dict(meta_tokens=v_meta_tokens, gdn_norm_g=v_gdn_norm_g, gdn_w_in=v_gdn_w_in, gdn_conv_w=v_gdn_conv_w,
                gdn_a_log=v_gdn_a_log, gdn_dt_bias=v_gdn_dt_bias, gdn_onorm_g=v_gdn_onorm_g, gdn_w_out=v_gdn_w_out,
                kv_norm_g=v_kv_norm_g, w_kv=v_w_kv, sb_norm_g=v_sb_norm_g, sb_w_q=v_sb_w_q, sb_w_o=v_sb_w_o,
                ffn_norm_g=v_ffn_norm_g, ffn_w_gate_up=v_ffn_w_gate_up, ffn_w_down=v_ffn_w_down, final_norm_g=v_final_norm_g)
    names = list(weights)

    seq, d = x.shape[1], x.shape[2]
    lo_frames = FRONT + N_META
    used = lo_frames + seq
    rows = -(-used // SB_BLOCK) * SB_BLOCK
    tm = _pick(rows, (640, 512, 384, 256, 128))
    tp = _pick(rows, (320, 256, 128))
    n_ffn = ffn_w_gate_up.shape[0]
    sb_width = sb_w_q.shape[2]
    chip =2 * lax.axis_index("x") + lax.axis_index("y")

    big = [gdn_w_in[0], gdn_w_out[0], w_kv, sb_w_q[0], sb_w_o[0], ffn_w_gate_up, ffn_w_down]
    small = [meta_tokens, gdn_norm_g, gdn_conv_w[0]]
    n_early = 2
    big_bf16 = [t.astype(bf16) for t in big]
    w_in_s, w_out_s, small_g = _gather_chips(big_bf16[:n_early] + [_pack(small, 16, f32)])
    small_s = _unpack(small_g, [t.shape for t in small])
    w_in = _join_cols(w_in_s)
    pad_ab = jnp.zeros((d, LANES - GDN_HEADS), bf16)
    w_in_ext = jnp.concatenate([w_in[:, :4 * GDN_WIDTH], w_in[:, 4 * GDN_WIDTH:4 * GDN_WIDTH + GDN_HEADS], pad_ab,
                                w_in[:, 4 * GDN_WIDTH + GDN_HEADS:], pad_ab], axis=1)
    w_out = _join_rows(w_out_s)
    meta_full, gdn_g_full, conv_full = (_join_cols(t) for t in small_s)

    zeros = lambda n: jnp.zeros((n, d), f32)
    h0 = jnp.concatenate([zeros(FRONT), meta_full, x[0], zeros(rows - used)], axis=0)
    tgt = jnp.concatenate([zeros(lo_frames), loss_target[0], zeros(rows - used)], axis=0)
    pad8 = lambda t: jnp.pad(t, ((0, 0), (0, LANES - t.shape[1])))
    a_log8, dt_bias8 = pad8(gdn_a_log), pad8(gdn_dt_bias)
    r_i = jnp.arange(tp)
    ltri = ((r_i[:, None] >= r_i[None, :]) & (r_i[:, None] // CHUNK == r_i[None, :] // CHUNK)).astype(f32)
    ffn_g = [ffn_norm_g[l:l + 1] for l in range(n_ffn)]
    kv_g, fin_g = kv_norm_g.reshape(1, d), final_norm_g.reshape(1, d)

    n0 = _rms_fwd(h0, gdn_g_full, "gdn_norm")
    proj = _matmul(n0, w_in_ext, "nn", "gdn_proj")
    gq, gk, gv, gw, bw = _gdn_prep_fwd(proj, conv_full, a_log8, dt_bias8, ltri, FRONT, used, tp)
    g_o, g_states, (w_kv_s, w_q_s, w_o_s, w_gu_s, w_dn_s) = _gdn_fwd(gq, gk, gv, gw, bw, big_bf16[n_early:])
    w_kvf = _join_cols(w_kv_s)
    w_k, w_v = w_kvf[:, :sb_width], w_kvf[:, sb_width:]
    w_q = _join_rows(w_q_s)
    w_o = _join_rows(w_o_s)
    w_gu = [_join_cols(w_gu_s[:, l]) for l in range(n_ffn)]
    w_dn = [_join_rows(w_dn_s[:, l]) for l in range(n_ffn)]
    og = _gdn_gate_fwd(g_o, proj, gdn_onorm_g, tm)
    h1 = _matmul(og, w_out, "nn", "gdn_out", res=h0)

    def ffn_fwd(h, l):
        n = _rms_fwd(h, ffn_g[l], f"ffn{l}_norm")
        gu = _matmul(n, w_gu[l], "nn", f"ffn{l}_gate_up")
        act = _swiglu_fwd(gu, f"ffn{l}_act")
        return _matmul(act, w_dn[l], "nn", f"ffn{l}_down", res=h), (n, gu, act)

    h2, ffn0_saved = ffn_fwd(h1, 0)
    n_kv = _rms_fwd(h2, kv_g, "kv_norm")
    kv = _matmul(n_kv, w_kvf, "nn", "kv_proj", out_dtype=bf16)
    n_sb = _rms_fwd(h2, sb_norm_g, "sb_norm")
    sq = _matmul(n_sb, w_q, "nn", "q_proj", out_dtype=bf16)
    s_o = _sb_fwd(sq, kv, sb_width)
    h3 = _matmul(s_o, w_o, "nn", "sb_out", res=h2)
    h4, ffn1_saved = ffn_fwd(h3, 1)
    dh4, d_fin_g, loss_part = _loss_head(h4, fin_g, tgt, lo_frames, used, "loss_head")

    def ffn_bwd(dh, h, l, saved):
        n, gu, act = saved
        d_wdn = _matmul(act, dh, "tn", f"ffn{l}_d_w_down")
        dact = _matmul(dh, w_dn[l], "nt", f"ffn{l}_d_act")
        dgu = _swiglu_bwd(gu, dact, f"ffn{l}_d_gate_up")
        d_wgu = _matmul(n, dgu, "tn", f"ffn{l}_d_w_gate_up")
        dn = _matmul(dgu, w_gu[l], "nt", f"ffn{l}_d_norm")
        dh_in, dg = _rms_bwd(h, ffn_g[l], dn, dh, f"ffn{l}_norm_bwd")
        return dh_in, d_wgu, d_wdn, dg

    dh3, d_wgu1, d_wdn1, d_ffn_g1 = ffn_bwd(dh4, h3, 1, ffn1_saved)
    d_wo = _matmul(s_o, dh3, "tn", "d_w_o")
    d_so = _matmul(dh3, w_o, "nt", "d_sb_o")
    d_sq, d_sk, d_sv = _sb_bwd(sq, kv, d_so, sb_width)
    d_wq = _matmul(n_sb, d_sq, "tn", "d_w_q")
    dh2, d_sb_g = _rms_bwd(h2, sb_norm_g, _matmul(d_sq, w_q, "nt", "d_sb_norm"), dh3, "sb_norm_bwd")
    d_wkv = jnp.concatenate([_matmul(n_kv, d_sk, "tn", "d_w_k"), _matmul(n_kv, d_sv, "tn", "d_w_v")], axis=1)
    dn_kv = _matmul(d_sv, w_v, "nt", "d_kv_norm_v", res=_matmul(d_sk, w_k, "nt", "d_kv_norm_k"))
    dh2, d_kv_g = _rms_bwd(h2, kv_g, dn_kv, dh2, "kv_norm_bwd")
    dh1, d_wgu0, d_wdn0, d_ffn_g0 = ffn_bwd(dh2, h1, 0, ffn0_saved)
    d_wout = _matmul(og, dh1, "tn", "d_w_out")
    d_og = _matmul(dh1, w_out, "nt", "d_gdn_gated")
    d_go, d_gate, d_onorm = _gdn_gate_bwd(g_o, proj, gdn_onorm_g, d_og, tm)
    by_chip = [None, _split_rows(d_wout), _split_cols(d_wkv), _split_rows(d_wq), _split_rows(d_wo),
               jnp.stack([_split_cols(d_wgu0), _split_cols(d_wgu1)], axis=1),
               jnp.stack([_split_rows(d_wdn0), _split_rows(d_wdn1)], axis=1)]
    (d_gq, d_gk, d_gv, d_gw, d_bw), got_early = _gdn_bwd(gq, gk, gv, gw, bw, g_states, d_go,
                                                         [t.astype(bf16) for t in by_chip[1:]])
    dconv, d_a_in, d_b_in, d_a_log8, d_dt_bias8 = _gdn_prep_bwd_act(
        proj, conv_full, a_log8, dt_bias8, ltri, d_gq, d_gk, d_gv, d_gw, d_bw, FRONT, used, tp)
    dproj, d_conv = _gdn_prep_bwd_conv(proj, conv_full, dconv, d_gate, d_a_in, d_b_in, tp)
    d_win_ext = _matmul(n0, dproj, "tn", "d_w_in")
    dh0, d_gdn_g = _rms_bwd(h0, gdn_g_full, _matmul(dproj, w_in_ext, "nt", "d_gdn_norm"), dh1, "gdn_norm_bwd")
    grad_x = dh0[lo_frames:used][None]
    d_win = jnp.concatenate([d_win_ext[:, :4 * GDN_WIDTH], d_win_ext[:, 4 * GDN_WIDTH:4 * GDN_WIDTH + GDN_HEADS],
                             d_win_ext[:, 4 * GDN_WIDTH + LANES:4 * GDN_WIDTH + LANES + GDN_HEADS]], axis=1)

    by_chip[0] = _split_cols(d_win)
    got = list(_scatter_chips([by_chip[0].astype(bf16)])) + list(got_early)
    chip_arr = jnp.reshape(chip, (1,)).astype(i32)
    over_chips = [_sum_chips(t, g, chip_arr, f"sum_chips_{k}") for k, (t, g) in enumerate(zip(by_chip, got))]
    over_sibling = _swap_sibling(over_chips)
    big_names = ["gdn_w_in", "gdn_w_out", "w_kv", "sb_w_q", "sb_w_o", "ffn_w_gate_up", "ffn_w_down"]
    g_big = dict(zip(big_names, zip(over_chips, over_sibling)))

    small_parts = [dh0[FRONT:lo_frames], d_gdn_g, d_conv, d_a_log8, d_dt_bias8, d_onorm, d_kv_g, d_sb_g,
                   d_ffn_g0, d_ffn_g1, d_fin_g, loss_part]
    s_rows = -(-sum(t.size for t in small_parts) // (8 * PACK_COLS)) * 8
    s_sum = _sum_devices(_gather_all(_pack(small_parts, s_rows, f32)), s_rows)
    (g_meta, g_gdn_g, g_conv, g_a_log8, g_dt8, g_onorm, g_kv_g, g_sb_g, g_ffn_g0, g_ffn_g1, g_fin_g,
     loss_v) = _unpack(s_sum, [t.shape for t in small_parts])
    col_shard = lambda t, w: lax.dynamic_slice_in_dim(t, chip * w, w, axis=t.ndim - 1)

    g_small = dict(
        meta_tokens=col_shard(g_meta, meta_tokens.shape[1]), gdn_norm_g=col_shard(g_gdn_g, gdn_norm_g.shape[1]),
        gdn_conv_w=col_shard(g_conv, gdn_conv_w.shape[2])[None],
        gdn_a_log=g_a_log8[:, :GDN_HEADS], gdn_dt_bias=g_dt8[:, :GDN_HEADS], gdn_onorm_g=g_onorm,
        kv_norm_g=g_kv_g.reshape(-1), sb_norm_g=g_sb_g, ffn_norm_g=jnp.concatenate([g_ffn_g0, g_ffn_g1], axis=0),
        final_norm_g=g_fin_g.reshape(-1))

    grads, delta, new_m, new_v = {}, {}, {}, {}
    for n in names:
        gs = g_big[n] if n in g_big else (g_small[n],)
        grads[n], delta[n], new_m[n], new_v[n] = _adamw(weights[n], gs, m_in[n], v_in[n], f"adamw_{n}")
    loss = loss_v[0, 0]
    return (loss, grad_x, *[grads[n] for n in names], *[delta[n] for n in names],
            *[new_m[n] for n in names], *[new_v[n] for n in names])
```

```python
import functools

import jax
import jax.numpy as jnp
from jax import lax
from jax.experimental import pallas as pl
from jax.experimental.pallas import tpu as pltpu

f32 = jnp.float32
bf16 = jnp.bfloat16
i32 = jnp.int32

EPS = 1e-6
N_META = 16
CHUNK = 64
FRONT = (-N_META) % CHUNK
GDN_HEADS = 8
GDN_DIM = 128
GDN_WIDTH = GDN_HEADS * GDN_DIM
CONV_WIDTH = 4
SB_DIM = 64
SB_BLOCK = 128
SB_FWD_HEADS = 4
SB_FIRST = 3
SB_UNDERFLOW = 104.0
LANES = 128
PACK_COLS = 1024
N_CHIPS = 4
N_DEV = 8
ADAM_LR, ADAM_B1, ADAM_B2, ADAM_EPS, ADAM_WD, ADAM_STEP = 0.001, 0.9, 0.999, 1e-08, 0.01, 10
VMEM_LIMIT = 56 * 1024 * 1024
MESH = pl.DeviceIdType.MESH


def _pick(n, prefs):
    for p in prefs:
        if n % p == 0:
            return p
    return n


def _params(sem):
    return pltpu.CompilerParams(dimension_semantics=sem, vmem_limit_bytes=VMEM_LIMIT)


_DIMS = {"nn": ((1,), (0,)), "nt": ((1,), (1,)), "tn": ((0,), (0,))}


def _bdot(a, b, mode):
    return lax.dot_general(a.astype(bf16), b.astype(bf16), (_DIMS[mode], ((), ())), preferred_element_type=f32)


def _matmul(a, b, mode, name, res=None, out_dtype=f32):
    if mode == "nn":
        (m, k), n = a.shape, b.shape[1]
    elif mode == "nt":
        (m, k), n = a.shape, b.shape[0]
    else:
        (k, m), n = a.shape, b.shape[1]
    tm = _pick(m, (640, 1408, 1024, 512, 384, 256, 128))
    tn = _pick(n, (1408, 2176, 1024, 512, 384, 256, 128))
    tk = _pick(k, (1408, 2176, 1024, 640, 512, 384, 256, 128))
    nk = k // tk
    a_spec = pl.BlockSpec((tk, tm), lambda i, j, kk: (kk, i)) if mode == "tn" else pl.BlockSpec((tm, tk), lambda i, j, kk: (i, kk))
    b_spec = pl.BlockSpec((tn, tk), lambda i, j, kk: (j, kk)) if mode == "nt" else pl.BlockSpec((tk, tn), lambda i, j, kk: (kk, j))
    o_spec = pl.BlockSpec((tm, tn), lambda i, j, kk: (i, j))
    has_res = res is not None

    def body(*refs):
        if has_res:
            a_ref, b_ref, r_ref, o_ref, acc = refs
        else:
            a_ref, b_ref, o_ref, acc = refs
        kk = pl.program_id(2)

        @pl.when(kk == 0)
        def _():
            acc[...] = jnp.zeros_like(acc)

        acc[...] += _bdot(a_ref[...], b_ref[...], mode)

        @pl.when(kk == nk - 1)
        def _():
            y = acc[...]
            if has_res:
                y = y + r_ref[...]
            o_ref[...] = y.astype(o_ref.dtype)

    ins = [a, b] + ([res] if has_res else [])
    specs = [a_spec, b_spec] + ([o_spec] if has_res else [])
    return pl.pallas_call(
        body, name=name, out_shape=jax.ShapeDtypeStruct((m, n), out_dtype), grid=(m // tm, n // tn, nk),
        in_specs=specs, out_specs=o_spec, scratch_shapes=[pltpu.VMEM((tm, tn), f32)],
        compiler_params=_params(("parallel", "parallel", "arbitrary")),
    )(*ins)


def _rowwise(fn, name, rows, tm, ins, outs, reds=()):
    n_in, n_out, n_red = len(ins), len(outs), len(reds)
    in_specs = []
    for arr, spec in ins:
        if spec is None:
            in_specs.append(pl.BlockSpec(arr.shape, lambda i, nd=arr.ndim: (0,) * nd))
        else:
            w, cb = spec
            in_specs.append(pl.BlockSpec((tm, w), lambda i, cb=cb: (i, cb)))
    out_specs = [pl.BlockSpec((tm, w), lambda i: (i, 0)) for w, _ in outs]
    out_specs += [pl.BlockSpec(s, lambda i, nd=len(s): (0,) * nd) for s in reds]
    out_shape = [jax.ShapeDtypeStruct((rows, w), dt) for w, dt in outs]
    out_shape += [jax.ShapeDtypeStruct(s, f32) for s in reds]

    def body(*refs):
        i = pl.program_id(0)
        vals = fn(i, *[r[...] for r in refs[:n_in]])
        for r, v in zip(refs[n_in:n_in + n_out], vals[:n_out]):
            r[...] = v.astype(r.dtype)
        red_refs = refs[n_in + n_out:]

        @pl.when(i == 0)
        def _():
            for r in red_refs:
                r[...] = jnp.zeros_like(r)

        for r, v in zip(red_refs, vals[n_out:]):
            r[...] += v

    res = pl.pallas_call(
        body, name=name, out_shape=out_shape, grid=(rows // tm,), in_specs=in_specs, out_specs=out_specs,
        compiler_params=_params(("arbitrary",)),
    )(*[a for a, _ in ins])
    return res


def _rms(x, g):
    return x * lax.rsqrt(jnp.mean(x * x, axis=-1, keepdims=True) + EPS) * g


def _row_mask(i, tm, lo, hi, shape):
    r = i * tm + lax.broadcasted_iota(i32, shape, 0)
    return (r >= lo) & (r < hi)


def _rms_fwd(x, g, name):
    rows, d = x.shape
    tm = _pick(rows, (640, 512, 384, 256, 128))
    return _rowwise(lambda i, xv, gv: (_rms(xv, gv),), name, rows, tm, [(x, (d, 0)), (g, None)], [(d, bf16)])[0]


def _rms_bwd(x, g, dn, res, name):
    rows, d = x.shape
    tm = _pick(rows, (640, 512, 384, 256, 128))

    def fn(i, xv, gv, dnv, rv):
        _, vjp = jax.vjp(_rms, xv, gv)
        dx, dg = vjp(dnv)
        return rv + dx, dg

    return _rowwise(fn, name, rows, tm, [(x, (d, 0)), (g, None), (dn, (d, 0)), (res, (d, 0))], [(d, f32)], [(1, d)])


def _swiglu(gate, up):
    return jax.nn.silu(gate) * up


def _ffn_up(n, w_gu, name):
    rows, d = n.shape
    f = w_gu.shape[1] // 2
    tm = _pick(rows, (640, 512, 384, 256, 128))
    tn = _pick(f, (1408, 1024, 512, 384, 256, 128))
    nj = f // tn

    def body(n_r, wg_r, wu_r, g_r, u_r, a_r):
        g = jnp.dot(n_r[...], wg_r[...], preferred_element_type=f32)
        u = jnp.dot(n_r[...], wu_r[...], preferred_element_type=f32)
        g_r[...] = g
        u_r[...] = u
        a_r[...] = _swiglu(g, u).astype(a_r.dtype)

    o_spec = pl.BlockSpec((tm, tn), lambda j, i: (i, j))
    return pl.pallas_call(
        body, name=name, grid=(nj, rows // tm),
        out_shape=[jax.ShapeDtypeStruct((rows, f), f32)] * 2 + [jax.ShapeDtypeStruct((rows, f), bf16)],
        in_specs=[pl.BlockSpec((tm, d), lambda j, i: (i, 0)), pl.BlockSpec((d, tn), lambda j, i: (0, j)),
                  pl.BlockSpec((d, tn), lambda j, i: (0, nj + j))],
        out_specs=[o_spec] * 3, compiler_params=_params(("parallel", "parallel")),
    )(n, w_gu, w_gu)


def _ffn_dact(dh, w_dn, gate, up, name):
    rows, d = dh.shape
    f = w_dn.shape[0]
    tm = _pick(rows, (640, 512, 384, 256, 128))
    tn = _pick(f, (1408, 1024, 512, 384, 256, 128))

    def body(dh_r, w_r, g_r, u_r, dg_r, du_r):
        dact = _bdot(dh_r[...], w_r[...], "nt")
        _, vjp = jax.vjp(_swiglu, g_r[...], u_r[...])
        dg, du = vjp(dact)
        dg_r[...] = dg.astype(dg_r.dtype)
        du_r[...] = du.astype(du_r.dtype)

    t_spec = pl.BlockSpec((tm, tn), lambda j, i: (i, j))
    return pl.pallas_call(
        body, name=name, grid=(f // tn, rows // tm), out_shape=[jax.ShapeDtypeStruct((rows, f), bf16)] * 2,
        in_specs=[pl.BlockSpec((tm, d), lambda j, i: (i, 0)), pl.BlockSpec((tn, d), lambda j, i: (j, 0)), t_spec, t_spec],
        out_specs=[t_spec] * 2, compiler_params=_params(("parallel", "parallel")),
    )(dh, w_dn, gate, up)


def _loss_head(h, g, tgt, lo, hi, name):
    rows, d = h.shape
    tm = _pick(rows, (640, 512, 384, 256, 128))

    def fn(i, hv, gv, tv):
        mask = _row_mask(i, tm, lo, hi, (tm, 1))

        def f(hh, gg):
            err = _rms(hh, gg) - tv
            per_row = jnp.where(mask, jnp.mean(err * err, axis=-1, keepdims=True), 0.0)
            return 0.5 * jnp.sum(per_row, axis=0, keepdims=True)

        loss, vjp = jax.vjp(f, hv, gv)
        dh, dg = vjp(jnp.ones_like(loss))
        return dh, dg, jnp.broadcast_to(loss, (1, LANES))

    return _rowwise(fn, name, rows, tm, [(h, (d, 0)), (g, None), (tgt, (d, 0))], [(d, f32)], [(1, d), (1, LANES)])


def _heads_l2(x):
    t = x.shape[0]
    x3 = x.reshape(t, GDN_HEADS, GDN_DIM)
    return (x3 * lax.rsqrt(jnp.sum(x3 * x3, axis=-1, keepdims=True) + EPS)).reshape(t, GDN_WIDTH)


def _gdn_act(conv, a_in, b_in, a_log, dt_bias, mask):
    s = jax.nn.silu(conv)
    q = _heads_l2(s[:, :GDN_WIDTH])
    k = _heads_l2(s[:, GDN_WIDTH:2 * GDN_WIDTH])
    v = s[:, 2 * GDN_WIDTH:]
    g = jnp.where(mask, -jnp.exp(a_log) * jax.nn.softplus(a_in + dt_bias), 0.0)
    beta = jnp.where(mask, jax.nn.sigmoid(b_in), 0.0)
    return q, k, v, g, beta


def _widen(x8):
    return jnp.concatenate([jnp.broadcast_to(x8[:, h:h + 1], (x8.shape[0], GDN_DIM)) for h in range(GDN_HEADS)], axis=1)


def _narrow(xw):
    t = xw.shape[0]
    lane = lax.broadcasted_iota(i32, (t, LANES), 1)
    out = jnp.zeros((t, LANES), f32)
    for h in range(GDN_HEADS):
        s = jnp.sum(xw[:, h * GDN_DIM:(h + 1) * GDN_DIM], axis=1, keepdims=True)
        out = out + jnp.where(lane == h, s, 0.0)
    return out


def _conv_taps(cur, prev8, w):
    tm = cur.shape[0]
    cat = jnp.concatenate([prev8, cur], axis=0)
    y = cur * w[CONV_WIDTH - 1:CONV_WIDTH, :]
    for j in range(1, CONV_WIDTH):
        y = y + pltpu.roll(cat, j, axis=0)[8:8 + tm, :] * w[CONV_WIDTH - 1 - j:CONV_WIDTH - j, :]
    return y


def _gdn_prep_specs(proj, tm):
    c3 = 3 * GDN_WIDTH
    ab = 4 * GDN_WIDTH // LANES
    t8 = tm // 8
    return [
        pl.BlockSpec((tm, c3), lambda i: (i, 0)),
        pl.BlockSpec((8, c3), lambda i: (jnp.maximum(i * t8 - 1, 0), 0)),
        pl.BlockSpec((tm, LANES), lambda i: (i, ab)),
        pl.BlockSpec((tm, LANES), lambda i: (i, ab + 1)),
    ]


def _full(arr):
    return pl.BlockSpec(arr.shape, lambda i, nd=arr.ndim: (0,) * nd)


def _gdn_prep_fwd(proj, conv_w, a_log, dt_bias, ltri, lo, hi, tm):
    rows = proj.shape[0]

    def body(cur, prev8, a_in, b_in, w, al, dtb, lt, q_o, k_o, v_o, g_o, b_o):
        i = pl.program_id(0)
        mask = _row_mask(i, tm, lo, hi, (tm, LANES)) & (lax.broadcasted_iota(i32, (tm, LANES), 1) < GDN_HEADS)
        conv = _conv_taps(cur[...], prev8[...], w[...])
        q, k, v, g, beta = _gdn_act(conv, a_in[...], b_in[...], al[...], dtb[...], mask)
        q_o[...] = q
        k_o[...] = k
        v_o[...] = v
        gcum = jnp.dot(lt[...], g, preferred_element_type=f32, precision=lax.Precision.HIGHEST)
        g_o[...] = _widen(gcum)
        b_o[...] = _widen(beta)

    wide = jax.ShapeDtypeStruct((rows, GDN_WIDTH), f32)
    o_spec = pl.BlockSpec((tm, GDN_WIDTH), lambda i: (i, 0))
    return pl.pallas_call(
        body, name="gdn_prep_fwd", out_shape=[wide] * 5, grid=(rows // tm,),
        in_specs=_gdn_prep_specs(proj, tm) + [_full(conv_w), _full(a_log), _full(dt_bias), _full(ltri)],
        out_specs=[o_spec] * 5, compiler_params=_params(("parallel",)),
    )(proj, proj, proj, proj, conv_w, a_log, dt_bias, ltri)


def _gdn_prep_bwd_act(proj, conv_w, a_log, dt_bias, ltri, dq, dk, dv, dgw, dbw, lo, hi, tm):
    rows = proj.shape[0]
    c3 = 3 * GDN_WIDTH

    def body(cur, prev8, a_in, b_in, w, al, dtb, lt, dq_r, dk_r, dv_r, dg_r, db_r, dconv_o, da_o, dbin_o, dal_o, ddt_o):
        i = pl.program_id(0)
        mask = _row_mask(i, tm, lo, hi, (tm, LANES)) & (lax.broadcasted_iota(i32, (tm, LANES), 1) < GDN_HEADS)
        conv = _conv_taps(cur[...], prev8[...], w[...])
        dgcum = _narrow(dg_r[...])
        dg = lax.dot_general(lt[...], dgcum, (((0,), (0,)), ((), ())), preferred_element_type=f32,
                             precision=lax.Precision.HIGHEST)
        dbeta = _narrow(db_r[...])
        _, vjp = jax.vjp(lambda c, a, b, x, y: _gdn_act(c, a, b, x, y, mask), conv, a_in[...], b_in[...], al[...], dtb[...])
        dconv, da, dbin, dal, ddt = vjp((dq_r[...], dk_r[...], dv_r[...], dg, dbeta))
        dconv_o[...] = dconv
        da_o[...] = da
        dbin_o[...] = dbin

        @pl.when(i == 0)
        def _():
            dal_o[...] = jnp.zeros_like(dal_o)
            ddt_o[...] = jnp.zeros_like(ddt_o)

        dal_o[...] += dal
        ddt_o[...] += ddt

    w_spec = pl.BlockSpec((tm, GDN_WIDTH), lambda i: (i, 0))
    n_spec = pl.BlockSpec((tm, LANES), lambda i: (i, 0))
    s_spec = pl.BlockSpec((1, LANES), lambda i: (0, 0))
    return pl.pallas_call(
        body, name="gdn_prep_bwd_act",
        out_shape=[jax.ShapeDtypeStruct((rows, c3), f32), jax.ShapeDtypeStruct((rows, LANES), f32),
                   jax.ShapeDtypeStruct((rows, LANES), f32), jax.ShapeDtypeStruct((1, LANES), f32),
                   jax.ShapeDtypeStruct((1, LANES), f32)],
        grid=(rows // tm,),
        in_specs=_gdn_prep_specs(proj, tm) + [_full(conv_w), _full(a_log), _full(dt_bias), _full(ltri)] + [w_spec] * 5,
        out_specs=[pl.BlockSpec((tm, c3), lambda i: (i, 0)), n_spec, n_spec, s_spec, s_spec],
        compiler_params=_params(("arbitrary",)),
    )(proj, proj, proj, proj, conv_w, a_log, dt_bias, ltri, dq, dk, dv, dgw, dbw)


def _gdn_prep_bwd_conv(proj, conv_w, dconv, dgate, da, dbin, tm):
    rows, width = proj.shape
    c3 = 3 * GDN_WIDTH
    t8 = tm // 8
    nt = rows // tm

    def body(cur, prev8, w, dc, dnext8, dgt, da_r, db_r, dp_o, dw_o):
        i = pl.program_id(0)
        d = dc[...]
        nxt = jnp.where(i == nt - 1, 0.0, dnext8[...])
        cat = jnp.concatenate([d, nxt], axis=0)
        wv = w[...]
        dx = d * wv[CONV_WIDTH - 1:CONV_WIDTH, :]
        for j in range(1, CONV_WIDTH):
            dx = dx + pltpu.roll(cat, tm + 8 - j, axis=0)[:tm, :] * wv[CONV_WIDTH - 1 - j:CONV_WIDTH - j, :]
        dp_o[:, :c3] = dx.astype(bf16)
        dp_o[:, c3:4 * GDN_WIDTH] = dgt[...].astype(bf16)
        dp_o[:, 4 * GDN_WIDTH:4 * GDN_WIDTH + LANES] = da_r[...].astype(bf16)
        dp_o[:, 4 * GDN_WIDTH + LANES:] = db_r[...].astype(bf16)

        xcat = jnp.concatenate([prev8[...], cur[...]], axis=0)
        parts = [jnp.sum(d * cur[...], axis=0, keepdims=True)]
        for j in range(1, CONV_WIDTH):
            parts.append(jnp.sum(d * pltpu.roll(xcat, j, axis=0)[8:8 + tm, :], axis=0, keepdims=True))
        dwt = jnp.concatenate(parts[::-1], axis=0)

        @pl.when(i == 0)
        def _():
            dw_o[...] = jnp.zeros_like(dw_o)

        dw_o[...] += dwt

    n_spec = pl.BlockSpec((tm, LANES), lambda i: (i, 0))
    return pl.pallas_call(
        body, name="gdn_prep_bwd_conv",
        out_shape=[jax.ShapeDtypeStruct((rows, width), bf16), jax.ShapeDtypeStruct((CONV_WIDTH, c3), f32)],
        grid=(nt,),
        in_specs=[pl.BlockSpec((tm, c3), lambda i: (i, 0)),
                  pl.BlockSpec((8, c3), lambda i: (jnp.maximum(i * t8 - 1, 0), 0)),
                  _full(conv_w),
                  pl.BlockSpec((tm, c3), lambda i: (i, 0)),
                  pl.BlockSpec((8, c3), lambda i: (jnp.minimum((i + 1) * t8, rows // 8 - 1), 0)),
                  pl.BlockSpec((tm, GDN_WIDTH), lambda i: (i, 0)), n_spec, n_spec],
        out_specs=[pl.BlockSpec((tm, width), lambda i: (i, 0)), pl.BlockSpec((CONV_WIDTH, c3), lambda i: (0, 0))],
        compiler_params=_params(("arbitrary",)),
    )(proj, proj, conv_w, dconv, dconv, dgate, da, dbin)


def _split(a):
    hi = a.astype(bf16)
    return hi, (a - hi.astype(f32)).astype(bf16)


def _make_mm(dot):
    @jax.custom_vjp
    def nn(a, b):
        return dot(a, b, "nn")

    nn.defvjp(lambda a, b: (dot(a, b, "nn"), (a, b)),
              lambda r, ct: (dot(ct, r[1], "nt"), dot(r[0], ct, "tn")))

    @jax.custom_vjp
    def nt(a, b):
        return dot(a, b, "nt")

    nt.defvjp(lambda a, b: (dot(a, b, "nt"), (a, b)),
              lambda r, ct: (dot(ct, r[1], "nn"), dot(ct, r[0], "tn")))

    @jax.custom_vjp
    def tn(a, b):
        return dot(a, b, "tn")

    tn.defvjp(lambda a, b: (dot(a, b, "tn"), (a, b)),
              lambda r, ct: (dot(r[1], ct, "nt"), dot(r[0], ct, "nn")))
    return nn, nt, tn


_mm, _mm_nt, _mm_tn = _make_mm(_bdot)


def _each(f, *lists):
    return [f(*xs) for xs in zip(*lists)]


def _gdn_chunk(q, k, v, gcb, bcb, s_in):
    c = q[0].shape[0]
    ri = lax.broadcasted_iota(i32, (c, c), 0)
    ci = lax.broadcasted_iota(i32, (c, c), 1)
    incl, strict = ri >= ci, ri > ci
    rowi = lax.broadcasted_iota(i32, gcb[0].shape, 0)
    qs = _each(lambda t: t * (GDN_DIM ** -0.5), q)
    decay = _each(lambda g: jnp.where(incl, jnp.exp(jnp.where(incl, g[:, :c] - g[:, :c].T, 0.0)), 0.0), gcb)
    kk = _each(lambda t: _mm_nt(t, t), k)
    a1 = _each(lambda b, d, t: jnp.where(strict, b[:, :c] * d * t, 0.0), bcb, decay, kk)
    eg = _each(jnp.exp, gcb)
    x = _each(lambda b, vv, e, t: jnp.concatenate([b * vv, (b * e) * t], axis=1), bcb, v, eg, k)
    pows = [a1]
    for _ in range(5):
        pows.append(_each(lambda p: _mm(p, p), pows[-1]))
    for ps in pows[:0:-1]:
        x = _each(lambda p, t: t + _mm(p, t), ps, x)
    x = _each(lambda p, t: t - _mm(p, t), a1, x)
    attn = _each(lambda a, b, d: _mm_nt(a, b) * d, qs, k, decay)
    glast = _each(lambda g: jnp.sum(jnp.where(rowi == c - 1, g, 0.0), axis=0, keepdims=True), gcb)
    u = _each(lambda t, s: t[:, :GDN_DIM] - _mm(t[:, GDN_DIM:], s), x, s_in)
    o = _each(lambda a, e, s, w, uu: _mm(a * e, s) + _mm(w, uu), qs, eg, s_in, attn, u)
    s_out = _each(lambda s, gl, t, g, uu: s * jnp.exp(gl) + _mm_tn(t * jnp.exp(gl - g), uu), s_in, glast, k, gcb, u)
    return o, s_out


def _gdn_heads(ref):
    return [ref[:, h * GDN_DIM:(h + 1) * GDN_DIM] for h in range(GDN_HEADS)]


def _gdn_fwd(q, k, v, gw, bw, shards):
    rows = q.shape[0]
    nc = rows // CHUNK
    num = len(shards)
    blk = pl.BlockSpec((CHUNK, GDN_WIDTH), lambda c: (c, 0))

    def body(*refs):
        q_r, k_r, v_r, g_r, b_r = refs[:5]
        ins = refs[5:5 + num]
        o_r, st_r = refs[5 + num:7 + num]
        outs = refs[7 + num:7 + 2 * num]
        s_sc, send_sems, recv_sems, local_sems = refs[7 + 2 * num:]
        c = pl.program_id(0)
        start, forward, finish = _gather_plan(ins, outs, send_sems, recv_sems, local_sems)

        @pl.when(c == 0)
        def _():
            s_sc[...] = jnp.zeros_like(s_sc)
            start()

        s_in = [s_sc[h] for h in range(GDN_HEADS)]
        st_r[0] = s_sc[...]
        o, s_out = _gdn_chunk(_gdn_heads(q_r), _gdn_heads(k_r), _gdn_heads(v_r), _gdn_heads(g_r), _gdn_heads(b_r), s_in)
        o_r[...] = jnp.concatenate(o, axis=1)
        for h in range(GDN_HEADS):
            s_sc[h] = s_out[h]
        pl.when(c == nc // 2)(forward)
        pl.when(c == nc - 1)(finish)

    res = pl.pallas_call(
        body, name="gdn_fwd",
        out_shape=[jax.ShapeDtypeStruct((rows, GDN_WIDTH), f32), jax.ShapeDtypeStruct((nc, GDN_HEADS, GDN_DIM, GDN_DIM), f32)]
        + [jax.ShapeDtypeStruct((N_CHIPS,) + t.shape, t.dtype) for t in shards],
        grid=(nc,), in_specs=[blk] * 5 + [_ANY] * num,
        out_specs=[blk, pl.BlockSpec((1, GDN_HEADS, GDN_DIM, GDN_DIM), lambda c: (c, 0, 0, 0))] + [_ANY] * num,
        scratch_shapes=[pltpu.VMEM((GDN_HEADS, GDN_DIM, GDN_DIM), f32), pltpu.SemaphoreType.DMA((6 * num,)),
                        pltpu.SemaphoreType.DMA((6 * num,)), pltpu.SemaphoreType.DMA((num,))],
        compiler_params=_params(("arbitrary",)),
    )(q, k, v, gw, bw, *shards)
    return res[0], res[1], res[2:]


def _gdn_bwd(q, k, v, gw, bw, states, do, parts):
    rows = q.shape[0]
    nc = rows // CHUNK
    num = len(parts)
    blk = pl.BlockSpec((CHUNK, GDN_WIDTH), lambda c: (nc - 1 - c, 0))

    def body(*refs):
        q_r, k_r, v_r, g_r, b_r, st_r, do_r = refs[:7]
        ins = refs[7:7 + num]
        dq_r, dk_r, dv_r, dg_r, db_r = refs[7 + num:12 + num]
        outs = refs[12 + num:12 + 2 * num]
        ds_sc, send_sems, recv_sems = refs[12 + 2 * num:]
        c = pl.program_id(0)
        start, finish = _scatter_plan(ins, outs, send_sems, recv_sems)

        @pl.when(c == 0)
        def _():
            ds_sc[...] = jnp.zeros_like(ds_sc)
            start()

        s_in = [st_r[0, h] for h in range(GDN_HEADS)]
        _, vjp = jax.vjp(_gdn_chunk, _gdn_heads(q_r), _gdn_heads(k_r), _gdn_heads(v_r), _gdn_heads(g_r), _gdn_heads(b_r), s_in)
        dq, dk, dv, dg, db, ds_in = vjp((_gdn_heads(do_r), [ds_sc[h] for h in range(GDN_HEADS)]))
        dq_r[...] = jnp.concatenate(dq, axis=1)
        dk_r[...] = jnp.concatenate(dk, axis=1)
        dv_r[...] = jnp.concatenate(dv, axis=1)
        dg_r[...] = jnp.concatenate(dg, axis=1)
        db_r[...] = jnp.concatenate(db, axis=1)
        for h in range(GDN_HEADS):
            ds_sc[h] = ds_in[h]
        pl.when(c == nc - 1)(finish)

    wide = jax.ShapeDtypeStruct((rows, GDN_WIDTH), f32)
    res = pl.pallas_call(
        body, name="gdn_bwd", out_shape=[wide] * 5 + [jax.ShapeDtypeStruct((3,) + t.shape[1:], t.dtype) for t in parts],
        grid=(nc,),
        in_specs=[blk] * 5 + [pl.BlockSpec((1, GDN_HEADS, GDN_DIM, GDN_DIM), lambda c: (nc - 1 - c, 0, 0, 0)), blk] + [_ANY] * num,
        out_specs=[blk] * 5 + [_ANY] * num,
        scratch_shapes=[pltpu.VMEM((GDN_HEADS, GDN_DIM, GDN_DIM), f32), pltpu.SemaphoreType.DMA((3 * num,)),
                        pltpu.SemaphoreType.DMA((3 * num,))],
        compiler_params=_params(("arbitrary",)),
    )(q, k, v, gw, bw, states, do, *parts)
    return res[:5], res[5:]


def _gdn_gate(o, gate, og):
    t = o.shape[0]
    o3 = o.reshape(t, GDN_HEADS, GDN_DIM)
    n = o3 * lax.rsqrt(jnp.mean(o3 * o3, axis=-1, keepdims=True) + EPS) * og.reshape(1, 1, GDN_DIM)
    return n.reshape(t, GDN_WIDTH) * jax.nn.silu(gate)


def _gdn_gate_fwd(o, proj, og, tm):
    rows = o.shape[0]
    return _rowwise(lambda i, ov, gv, w: (_gdn_gate(ov, gv, w),), "gdn_gate_fwd", rows, tm,
                    [(o, (GDN_WIDTH, 0)), (proj, (GDN_WIDTH, 3)), (og, None)], [(GDN_WIDTH, bf16)])[0]


def _gdn_gate_bwd(o, proj, og, dy, tm):
    rows = o.shape[0]

    def fn(i, ov, gv, w, d):
        _, vjp = jax.vjp(_gdn_gate, ov, gv, w)
        return vjp(d)

    return _rowwise(fn, "gdn_gate_bwd", rows, tm,
                    [(o, (GDN_WIDTH, 0)), (proj, (GDN_WIDTH, 3)), (og, None), (dy, (GDN_WIDTH, 0))],
                    [(GDN_WIDTH, f32), (GDN_WIDTH, f32)], [(1, GDN_DIM)])


def _sb_visible(i, j, valid):
    qpos = i * SB_BLOCK + lax.broadcasted_iota(i32, (SB_BLOCK, SB_BLOCK), 0)
    kpos = j * SB_BLOCK + lax.broadcasted_iota(i32, (SB_BLOCK, SB_BLOCK), 1)
    return (kpos < qpos) & (kpos >= FRONT) & valid


def _sb_logs(z, vis):
    l1p = jnp.log1p(jnp.exp(-jnp.abs(z)))
    return -(jnp.maximum(-z, 0.0) + l1p), jnp.where(vis, -(jnp.maximum(z, 0.0) + l1p), 0.0)


def _tri_sum(x, tri):
    hi, lo = _split(x)
    return jnp.dot(hi, tri, preferred_element_type=f32) + jnp.dot(lo, tri, preferred_element_type=f32)


def _sb_live(t, i, runs):
    return (t <= i) & (jnp.max(functools.reduce(jnp.maximum, runs)) > -SB_UNDERFLOW)


def _sb_blocks(i, t, nb):
    js = [i - t - b for b in range(nb)]
    kss = [pl.ds(pl.multiple_of(jnp.maximum(j, 0) * SB_BLOCK, SB_BLOCK), SB_BLOCK) for j in js]
    return kss, [_sb_visible(i, j, j >= 0) for j in js]


def _sb_weights(i, t, nb, qs, sls, k_r, runs, after, scale):
    nh = len(qs)
    kss, vis = _sb_blocks(i, t, nb)
    units = [(a, b) for b in range(nb) for a in range(nh)]
    z = [_bdot(qs[a], k_r[kss[b], sls[a]], "nt") * scale for a, b in units]
    logs = [_sb_logs(zz, vis[b]) for zz, (a, b) in zip(z, units)]
    later = [_tri_sum(l[1], after) for l in logs]
    sums = [jnp.sum(l[1], axis=1, keepdims=True) for l in logs]
    w = []
    runs = list(runs)
    for b in range(nb):
        for a in range(nh):
            u = b * nh + a
            w.append(jnp.where(vis[b], jnp.exp(logs[u][0] + later[u] + runs[a]), 0.0))
        runs = [runs[a] + sums[b * nh + a] for a in range(nh)]
    return kss, vis, units, z, w, tuple(runs)


def _sb_fwd(q, kv, width):
    rows = q.shape[0]
    nq = rows // SB_BLOCK
    lanes = SB_FWD_HEADS * SB_DIM
    npair = width // lanes
    scale = SB_DIM ** -0.5

    def body(q_r, k_r, v_r, o_r):
        i = pl.program_id(1)
        rj = lax.broadcasted_iota(i32, (SB_BLOCK, SB_BLOCK), 0)
        cs = lax.broadcasted_iota(i32, (SB_BLOCK, SB_BLOCK), 1)
        after = (rj > cs).astype(bf16)
        sls = [slice(a * SB_DIM, (a + 1) * SB_DIM) for a in range(SB_FWD_HEADS)]
        qs = [q_r[:, sl] for sl in sls]

        def step(carry, nb):
            t, accs, runs = carry
            kss, _, units, _, w, runs = _sb_weights(i, t, nb, qs, sls, k_r, runs, after, scale)
            prods = [_bdot(ww, v_r[kss[b], sls[a]], "nn") for ww, (a, b) in zip(w, units)]
            accs = tuple(functools.reduce(jnp.add, [accs[a]] + prods[a::SB_FWD_HEADS]) for a in range(SB_FWD_HEADS))
            return t + nb, accs, runs

        init = (jnp.int32(0), tuple(jnp.zeros((SB_BLOCK, SB_DIM), f32) for _ in sls),
                tuple(jnp.zeros((SB_BLOCK, 1), f32) for _ in sls))
        _, accs, _ = lax.while_loop(lambda c: _sb_live(c[0], i, c[2]), lambda c: step(c, 2), step(init, SB_FIRST))
        o_r[...] = jnp.concatenate(accs, axis=1)

    return pl.pallas_call(
        body, name="sb_fwd", out_shape=jax.ShapeDtypeStruct((rows, width), f32), grid=(npair, nq),
        in_specs=[pl.BlockSpec((SB_BLOCK, lanes), lambda p, i: (i, p)),
                  pl.BlockSpec((rows, lanes), lambda p, i: (0, p)),
                  pl.BlockSpec((rows, lanes), lambda p, i: (0, npair + p))],
        out_specs=pl.BlockSpec((SB_BLOCK, lanes), lambda p, i: (i, p)),
        compiler_params=_params(("parallel", "arbitrary")),
    )(q, kv, kv)


def _sb_bwd(q, kv, do, width):
    rows = q.shape[0]
    nq = rows // SB_BLOCK
    npair = width // LANES
    nh = LANES // SB_DIM
    scale = SB_DIM ** -0.5

    def body(q_r, k_r, v_r, do_r, dq_r, dk_r, dv_r, e_sc, sig_sc, w_sc):
        i = pl.program_id(1)

        @pl.when(i == 0)
        def _():
            dk_r[...] = jnp.zeros_like(dk_r)
            dv_r[...] = jnp.zeros_like(dv_r)

        rj = lax.broadcasted_iota(i32, (SB_BLOCK, SB_BLOCK), 0)
        cs = lax.broadcasted_iota(i32, (SB_BLOCK, SB_BLOCK), 1)
        after = (rj > cs).astype(bf16)
        from_s = (rj >= cs).astype(bf16)
        zero1 = jnp.zeros((SB_BLOCK, 1), f32)
        sls = [slice(a * SB_DIM, (a + 1) * SB_DIM) for a in range(nh)]
        qs = [q_r[:, sl] for sl in sls]
        dos = [do_r[:, sl] for sl in sls]

        def weigh(carry, nb):
            t, runs, eruns = carry
            kss, _, units, z, w, runs = _sb_weights(i, t, nb, qs, sls, k_r, runs, after, scale)
            dw = [_bdot(dos[a], v_r[kss[b], sls[a]], "nt") for a, b in units]
            e = [ww * d for ww, d in zip(w, dw)]
            for u, (a, b) in enumerate(units):
                e_sc[a, t + b] = e[u]
                sig_sc[a, t + b] = jax.nn.sigmoid(z[u])
                w_sc[a, t + b] = w[u].astype(w_sc.dtype)
            sums = [jnp.sum(ee, axis=1, keepdims=True) for ee in e]
            eruns = tuple(functools.reduce(jnp.add, [eruns[a]] + sums[a::nh]) for a in range(nh))
            return t + nb, runs, eruns

        n_blk, _, etots = lax.while_loop(lambda c: _sb_live(c[0], i, c[1]), lambda c: weigh(c, 2),
                                         weigh((jnp.int32(0), (zero1,) * nh, (zero1,) * nh), SB_FIRST))

        def push(t, carry, nb):
            dqs, eruns = carry
            kss, vis = _sb_blocks(i, t, nb)
            units = [(a, b) for b in range(nb) for a in range(nh)]
            e = [e_sc[a, t + b] for a, b in units]
            dvs = [_bdot(w_sc[a, t + b], dos[a], "tn") for a, b in units]
            upto = [_tri_sum(ee, from_s) for ee in e]
            sums = [jnp.sum(ee, axis=1, keepdims=True) for ee in e]
            dz = []
            eruns = list(eruns)
            for b in range(nb):
                for a in range(nh):
                    u = b * nh + a
                    sig = sig_sc[a, t + b]
                    before = etots[a] - eruns[a] - upto[u]
                    dz.append(jnp.where(vis[b], e[u] * (1.0 - sig) - before * sig, 0.0) * scale)
                eruns = [eruns[a] + sums[b * nh + a] for a in range(nh)]
            dks = [_bdot(d, qs[a], "tn") for d, (a, b) in zip(dz, units)]
            dqp = [_bdot(d, k_r[kss[b], sls[a]], "nn") for d, (a, b) in zip(dz, units)]
            for b in range(nb):
                dk_r[kss[b], :] += jnp.concatenate(dks[b * nh:(b + 1) * nh], axis=1)
                dv_r[kss[b], :] += jnp.concatenate(dvs[b * nh:(b + 1) * nh], axis=1)
            dqs = tuple(functools.reduce(jnp.add, [dqs[a]] + dqp[a::nh]) for a in range(nh))
            return dqs, tuple(eruns)

        first = push(jnp.int32(0), (tuple(jnp.zeros((SB_BLOCK, SB_DIM), f32) for _ in sls), (zero1,) * nh), SB_FIRST)
        dqs, _ = lax.fori_loop(0, (n_blk - SB_FIRST) // 2, lambda p, c: push(SB_FIRST + 2 * p, c, 2), first)
        dq_r[...] = jnp.concatenate(dqs, axis=1)

    blk = pl.BlockSpec((SB_BLOCK, LANES), lambda p, i: (i, p))
    col = pl.BlockSpec((rows, LANES), lambda p, i: (0, p))
    wide = jax.ShapeDtypeStruct((rows, width), f32)
    depth = nq + SB_FIRST
    return pl.pallas_call(
        body, name="sb_bwd", out_shape=[wide] * 3, grid=(npair, nq),
        in_specs=[blk, col, pl.BlockSpec((rows, LANES), lambda p, i: (0, npair + p)), blk],
        out_specs=[blk, col, col],
        scratch_shapes=[pltpu.VMEM((nh, depth, SB_BLOCK, SB_BLOCK), f32), pltpu.VMEM((nh, depth, SB_BLOCK, SB_BLOCK), f32),
                        pltpu.VMEM((nh, depth, SB_BLOCK, SB_BLOCK), bf16)],
        compiler_params=_params(("parallel", "arbitrary")),
    )(q, kv, kv, do)


_FLIPS = ((1, 0), (0, 1), (1, 1))
_ANY = pl.BlockSpec(memory_space=pl.ANY)


def _flip(v, a):
    return v + a - 2 * a * v


def _gather_plan(ins, outs, send_sems, recv_sems, local_sems):
    num = len(ins)
    x, y, c = lax.axis_index("x"), lax.axis_index("y"), lax.axis_index("c")
    me, sibling = (x, y, c), (x, y, 1 - c)
    chip = 2 * x + y
    others = [(_flip(x, a), _flip(y, b)) for a, b in _FLIPS]
    pairs = [(k, n, 2 * ox + oy) for k in range(num) for n, (ox, oy) in enumerate(others)]

    def half_of(ref, hc):
        half = ref.shape[0] // 2
        start = hc * half
        for align in (16, 8):
            if half % align == 0:
                start = pl.multiple_of(start, align)
                break
        return ref.at[pl.ds(start, half)]

    def copy(k, n, s, hc, to, src=None):
        dst = half_of(outs[k].at[s], hc)
        return pltpu.make_async_remote_copy(
            src_ref=dst if src is None else src, dst_ref=dst,
            send_sem=send_sems.at[6 * k + n], recv_sem=recv_sems.at[6 * k + n], device_id=to, device_id_type=MESH)

    mine = [pltpu.make_async_copy(ins[k], outs[k].at[chip], local_sems.at[k]) for k in range(num)]
    first = [copy(k, n, chip, c, (others[n][0], others[n][1], c), src=half_of(ins[k], c)) for k, n, _ in pairs]
    passed = [copy(k, 3 + n, s, c, sibling) for k, n, s in pairs]

    def start():
        for cp in mine + first:
            cp.start()

    def forward():
        for (k, n, s), fw in zip(pairs, passed):
            copy(k, n, s, c, me).wait_recv()
            fw.start()

    def finish():
        for k, n, s in pairs:
            copy(k, 3 + n, s, 1 - c, me).wait_recv()
        for cp in first + passed:
            cp.wait_send()
        for cp in mine:
            cp.wait()

    return start, forward, finish


def _gather_chips(shards):
    num = len(shards)

    def body(*refs):
        for phase in _gather_plan(refs[:num], refs[num:2 * num], *refs[2 * num:]):
            phase()

    return pl.pallas_call(
        body, name="gather_chips", out_shape=[jax.ShapeDtypeStruct((N_CHIPS,) + t.shape, t.dtype) for t in shards],
        in_specs=[_ANY] * num, out_specs=[_ANY] * num,
        scratch_shapes=[pltpu.SemaphoreType.DMA((6 * num,)), pltpu.SemaphoreType.DMA((6 * num,)),
                        pltpu.SemaphoreType.DMA((num,))],
    )(*shards)


def _scatter_plan(ins, outs, send_sems, recv_sems):
    x, y, c = lax.axis_index("x"), lax.axis_index("y"), lax.axis_index("c")
    cps = []
    for k in range(len(ins)):
        for n, (a, b) in enumerate(_FLIPS):
            ox, oy = _flip(x, a), _flip(y, b)
            cps.append(pltpu.make_async_remote_copy(
                src_ref=ins[k].at[2 * ox + oy], dst_ref=outs[k].at[n], send_sem=send_sems.at[3 * k + n],
                recv_sem=recv_sems.at[3 * k + n], device_id=(ox, oy, c), device_id_type=MESH))

    def start():
        for cp in cps:
            cp.start()

    def finish():
        for cp in cps:
            cp.wait()

    return start, finish


def _scatter_chips(parts):
    num = len(parts)

    def body(*refs):
        for phase in _scatter_plan(refs[:num], refs[num:2 * num], *refs[2 * num:]):
            phase()

    return pl.pallas_call(
        body, name="scatter_chips", out_shape=[jax.ShapeDtypeStruct((3,) + t.shape[1:], t.dtype) for t in parts],
        in_specs=[_ANY] * num, out_specs=[_ANY] * num,
        scratch_shapes=[pltpu.SemaphoreType.DMA((3 * num,)), pltpu.SemaphoreType.DMA((3 * num,))],
    )(*parts)


def _swap_sibling(arrs):
    num = len(arrs)

    def body(*refs):
        ins, outs = refs[:num], refs[num:2 * num]
        send_sems, recv_sems = refs[2 * num:]
        x, y, c = lax.axis_index("x"), lax.axis_index("y"), lax.axis_index("c")
        cps = [pltpu.make_async_remote_copy(src_ref=ins[k], dst_ref=outs[k], send_sem=send_sems.at[k],
                                            recv_sem=recv_sems.at[k], device_id=(x, y, 1 - c), device_id_type=MESH)
               for k in range(num)]
        for cp in cps:
            cp.start()
        for cp in cps:
            cp.wait()

    return pl.pallas_call(
        body, name="swap_sibling", out_shape=[jax.ShapeDtypeStruct(t.shape, t.dtype) for t in arrs],
        in_specs=[_ANY] * num, out_specs=[_ANY] * num,
        scratch_shapes=[pltpu.SemaphoreType.DMA((num,)), pltpu.SemaphoreType.DMA((num,))],
    )(*arrs)


def _gather_all(v):
    m_per, n = v.shape

    def body(x_ref, out_ref, send_sems, recv_sems, local_sem):
        x, y, c = lax.axis_index("x"), lax.axis_index("y"), lax.axis_index("c")
        me, sibling = (x, y, c), (x, y, 1 - c)
        chips = [(_flip(x, a), _flip(y, b)) for a, b in _FLIPS]

        def rows(px, py, pc):
            return out_ref.at[pl.ds(pl.multiple_of((4 * px + 2 * py + pc) * m_per, 8), m_per), :]

        def copy(k, block, to, src=None):
            return pltpu.make_async_remote_copy(
                src_ref=rows(*block) if src is None else src, dst_ref=rows(*block),
                send_sem=send_sems.at[k], recv_sem=recv_sems.at[k], device_id=to, device_id_type=MESH)

        mine = pltpu.make_async_copy(x_ref, rows(*me), local_sem)
        mine.start()
        first = [copy(0, me, sibling, src=x_ref)]
        first += [copy(1 + j, me, (*chip, c), src=x_ref) for j, chip in enumerate(chips)]
        for cp in first:
            cp.start()
        passed = [copy(4 + j, (*chip, c), sibling) for j, chip in enumerate(chips)]
        for j, chip in enumerate(chips):
            copy(1 + j, (*chip, c), me).wait_recv()
            passed[j].start()
        copy(0, sibling, me).wait_recv()
        for j, chip in enumerate(chips):
            copy(4 + j, (*chip, 1 - c), me).wait_recv()
        for cp in first + passed:
            cp.wait_send()
        mine.wait()

    return pl.pallas_call(
        body, name="gather_all", out_shape=jax.ShapeDtypeStruct((N_DEV * m_per, n), v.dtype),
        in_specs=[pl.BlockSpec(memory_space=pltpu.VMEM)], out_specs=pl.BlockSpec(memory_space=pltpu.VMEM),
        scratch_shapes=[pltpu.SemaphoreType.DMA((7,)), pltpu.SemaphoreType.DMA((7,)), pltpu.SemaphoreType.DMA],
    )(v)


def _sum_chips(parts, got, chip, name):
    cols = parts.shape[-1]
    rows = parts.size // (N_CHIPS * cols)
    tm = _pick(rows, (256, 128, 64, 32, 16))

    def body(chip_r, own_r, got_r, o_r):
        acc = own_r[0]
        for n in range(3):
            acc = acc + got_r[n].astype(f32)
        o_r[...] = acc

    return pl.pallas_call(
        body, name=name, out_shape=jax.ShapeDtypeStruct((rows, cols), f32),
        grid_spec=pltpu.PrefetchScalarGridSpec(
            num_scalar_prefetch=1, grid=(rows // tm,),
            in_specs=[pl.BlockSpec((1, tm, cols), lambda i, s: (s[0], i, 0)),
                      pl.BlockSpec((3, tm, cols), lambda i, s: (0, i, 0))],
            out_specs=pl.BlockSpec((tm, cols), lambda i, s: (i, 0))),
        compiler_params=_params(("parallel",)),
    )(chip, parts.reshape(N_CHIPS, rows, cols), got.reshape(3, rows, cols))


def _sum_devices(g, m_per):
    n = g.shape[1]

    def body(g_r, o_r):
        acc = g_r[0:m_per, :]
        for d in range(1, N_DEV):
            acc = acc + g_r[d * m_per:(d + 1) * m_per, :]
        o_r[...] = acc

    return pl.pallas_call(body, name="sum_devices", out_shape=jax.ShapeDtypeStruct((m_per, n), f32))(g)


def _adamw(w, gs, m, v, name):
    shape = w.shape
    cols = shape[-1]
    rows = w.size // cols
    tm = _pick(rows, (256, 128, 64, 32, 16, 8)) if rows * cols * 4 > (1 << 20) else rows

    def fn(i, wv, mv, vv, *gv):
        g = functools.reduce(jnp.add, gv)
        mn = ADAM_B1 * mv + (1.0 - ADAM_B1) * g
        vn = ADAM_B2 * vv + (1.0 - ADAM_B2) * jnp.square(g)
        m_hat = mn / (1.0 - ADAM_B1 ** ADAM_STEP)
        v_hat = vn / (1.0 - ADAM_B2 ** ADAM_STEP)
        delta = -ADAM_LR * (m_hat / (jnp.sqrt(v_hat) + ADAM_EPS) + ADAM_WD * wv)
        return g, delta, mn, vn

    outs = _rowwise(fn, name, rows, tm, [(t.reshape(rows, cols), (cols, 0)) for t in (w, m, v) + tuple(gs)], [(cols, f32)] * 4)
    return tuple(o.reshape(shape) for o in outs)


def _pack(pieces, rows, dtype):
    flat = jnp.concatenate([p.reshape(-1).astype(dtype) for p in pieces])
    return jnp.pad(flat, (0, rows * PACK_COLS - flat.size)).reshape(rows, PACK_COLS)


def _unpack(buf, shapes):
    lead = buf.shape[:-2]
    flat = buf.reshape(lead + (-1,))
    out, off = [], 0
    for s in shapes:
        n = 1
        for d in s:
            n *= d
        out.append(flat[..., off:off + n].reshape(lead + tuple(s)))
        off += n
    return out


def _join_cols(t):
    return jnp.moveaxis(t, 0, -2).reshape(t.shape[1:-1] + (N_CHIPS * t.shape[-1],))


def _join_rows(t):
    return t.reshape((N_CHIPS * t.shape[1],) + t.shape[2:])


def _split_cols(t, parts=N_CHIPS):
    r, cols = t.shape
    return jnp.moveaxis(t.reshape(r, parts, cols // parts), 1, 0)


def _split_rows(t):
    return t.reshape((N_CHIPS, t.shape[0] // N_CHIPS) + t.shape[1:])


def kernel(x, meta_tokens, gdn_norm_g, gdn_w_in, gdn_conv_w, gdn_a_log, gdn_dt_bias, gdn_onorm_g, gdn_w_out, kv_norm_g, w_kv, sb_norm_g, sb_w_q, sb_w_o, ffn_norm_g, ffn_w_gate_up, ffn_w_down, final_norm_g, loss_target, m_meta_tokens, m_gdn_norm_g, m_gdn_w_in, m_gdn_conv_w, m_gdn_a_log, m_gdn_dt_bias, m_gdn_onorm_g, m_gdn_w_out, m_kv_norm_g, m_w_kv, m_sb_norm_g, m_sb_w_q, m_sb_w_o, m_ffn_norm_g, m_ffn_w_gate_up, m_ffn_w_down, m_final_norm_g, v_meta_tokens, v_gdn_norm_g, v_gdn_w_in, v_gdn_conv_w, v_gdn_a_log, v_gdn_dt_bias, v_gdn_onorm_g, v_gdn_w_out, v_kv_norm_g, v_w_kv, v_sb_norm_g, v_sb_w_q, v_sb_w_o, v_ffn_norm_g, v_ffn_w_gate_up, v_ffn_w_down, v_final_norm_g):
    weights = dict(meta_tokens=meta_tokens, gdn_norm_g=gdn_norm_g, gdn_w_in=gdn_w_in, gdn_conv_w=gdn_conv_w,
                   gdn_a_log=gdn_a_log, gdn_dt_bias=gdn_dt_bias, gdn_onorm_g=gdn_onorm_g, gdn_w_out=gdn_w_out,
                   kv_norm_g=kv_norm_g, w_kv=w_kv, sb_norm_g=sb_norm_g, sb_w_q=sb_w_q, sb_w_o=sb_w_o,
                   ffn_norm_g=ffn_norm_g, ffn_w_gate_up=ffn_w_gate_up, ffn_w_down=ffn_w_down, final_norm_g=final_norm_g)
    m_in = dict(meta_tokens=m_meta_tokens, gdn_norm_g=m_gdn_norm_g, gdn_w_in=m_gdn_w_in, gdn_conv_w=m_gdn_conv_w,
                gdn_a_log=m_gdn_a_log, gdn_dt_bias=m_gdn_dt_bias, gdn_onorm_g=m_gdn_onorm_g, gdn_w_out=m_gdn_w_out,
                kv_norm_g=m_kv_norm_g, w_kv=m_w_kv, sb_norm_g=m_sb_norm_g, sb_w_q=m_sb_w_q, sb_w_o=m_sb_w_o,
                ffn_norm_g=m_ffn_norm_g, ffn_w_gate_up=m_ffn_w_gate_up, ffn_w_down=m_ffn_w_down, final_norm_g=m_final_norm_g)
    v_in = dict(meta_tokens=v_meta_tokens, gdn_norm_g=v_gdn_norm_g, gdn_w_in=v_gdn_w_in, gdn_conv_w=v_gdn_conv_w,
                gdn_a_log=v_gdn_a_log, gdn_dt_bias=v_gdn_dt_bias, gdn_onorm_g=v_gdn_onorm_g, gdn_w_out=v_gdn_w_out,
                kv_norm_g=v_kv_norm_g, w_kv=v_w_kv, sb_norm_g=v_sb_norm_g, sb_w_q=v_sb_w_q, sb_w_o=v_sb_w_o,
                ffn_norm_g=v_ffn_norm_g, ffn_w_gate_up=v_ffn_w_gate_up, ffn_w_down=v_ffn_w_down, final_norm_g=v_final_norm_g)
    names = list(weights)

    seq, d = x.shape[1], x.shape[2]
    lo_frames = FRONT + N_META
    used = lo_frames + seq
    rows = -(-used // SB_BLOCK) * SB_BLOCK
    tm = _pick(rows, (640, 512, 384, 256, 128))
    tp = _pick(rows, (320, 256, 128))
    n_ffn = ffn_w_gate_up.shape[0]
    sb_width = sb_w_q.shape[2]
    chip =2 * lax.axis_index("x") + lax.axis_index("y")

    big = [gdn_w_in[0], gdn_w_out[0], w_kv, sb_w_q[0], sb_w_o[0], ffn_w_gate_up, ffn_w_down]
    small = [meta_tokens, gdn_norm_g, gdn_conv_w[0]]
    n_early = 2
    big_bf16 = [t.astype(bf16) for t in big]
    w_in_s, w_out_s, small_g = _gather_chips(big_bf16[:n_early] + [_pack(small, 16, f32)])
    small_s = _unpack(small_g, [t.shape for t in small])
    w_in = _join_cols(w_in_s)
    pad_ab = jnp.zeros((d, LANES - GDN_HEADS), bf16)
    w_in_ext = jnp.concatenate([w_in[:, :4 * GDN_WIDTH], w_in[:, 4 * GDN_WIDTH:4 * GDN_WIDTH + GDN_HEADS], pad_ab,
                                w_in[:, 4 * GDN_WIDTH + GDN_HEADS:], pad_ab], axis=1)
    w_out = _join_rows(w_out_s)
    meta_full, gdn_g_full, conv_full = (_join_cols(t) for t in small_s)

    zeros = lambda n: jnp.zeros((n, d), f32)
    h0 = jnp.concatenate([zeros(FRONT), meta_full, x[0], zeros(rows - used)], axis=0)
    tgt = jnp.concatenate([zeros(lo_frames), loss_target[0], zeros(rows - used)], axis=0)
    pad8 = lambda t: jnp.pad(t, ((0, 0), (0, LANES - t.shape[1])))
    a_log8, dt_bias8 = pad8(gdn_a_log), pad8(gdn_dt_bias)
    r_i = jnp.arange(tp)
    ltri = ((r_i[:, None] >= r_i[None, :]) & (r_i[:, None] // CHUNK == r_i[None, :] // CHUNK)).astype(f32)
    ffn_g = [ffn_norm_g[l:l + 1] for l in range(n_ffn)]
    kv_g, fin_g = kv_norm_g.reshape(1, d), final_norm_g.reshape(1, d)

    n0 = _rms_fwd(h0, gdn_g_full, "gdn_norm")
    proj = _matmul(n0, w_in_ext, "nn", "gdn_proj")
    gq, gk, gv, gw, bw = _gdn_prep_fwd(proj, conv_full, a_log8, dt_bias8, ltri, FRONT, used, tp)
    g_o, g_states, (w_kv_s, w_q_s, w_o_s, w_gu_s, w_dn_s) = _gdn_fwd(gq, gk, gv, gw, bw, big_bf16[n_early:])
    w_kvf = _join_cols(w_kv_s)
    w_k, w_v = w_kvf[:, :sb_width], w_kvf[:, sb_width:]
    w_q = _join_rows(w_q_s)
    w_o = _join_rows(w_o_s)
    w_gu = [_join_cols(w_gu_s[:, l]) for l in range(n_ffn)]
    w_dn = [_join_rows(w_dn_s[:, l]) for l in range(n_ffn)]
    og = _gdn_gate_fwd(g_o, proj, gdn_onorm_g, tm)
    h1 = _matmul(og, w_out, "nn", "gdn_out", res=h0)

    def ffn_fwd(h, l):
        n = _rms_fwd(h, ffn_g[l], f"ffn{l}_norm")
        gate, up, act = _ffn_up(n, w_gu[l], f"ffn{l}_gate_up")
        return _matmul(act, w_dn[l], "nn", f"ffn{l}_down", res=h), (n, gate, up, act)

    h2, ffn0_saved = ffn_fwd(h1, 0)
    n_kv = _rms_fwd(h2, kv_g, "kv_norm")
    kv = _matmul(n_kv, w_kvf, "nn", "kv_proj", out_dtype=bf16)
    n_sb = _rms_fwd(h2, sb_norm_g, "sb_norm")
    sq = _matmul(n_sb, w_q, "nn", "q_proj", out_dtype=bf16)
    s_o = _sb_fwd(sq, kv, sb_width)
    h3 = _matmul(s_o, w_o, "nn", "sb_out", res=h2)
    h4, ffn1_saved = ffn_fwd(h3, 1)
    dh4, d_fin_g, loss_part = _loss_head(h4, fin_g, tgt, lo_frames, used, "loss_head")

    def ffn_bwd(dh, h, l, saved):
        n, gate, up, act = saved
        f = gate.shape[1]
        d_wdn = _matmul(act, dh, "tn", f"ffn{l}_d_w_down")
        d_gate, d_up = _ffn_dact(dh, w_dn[l], gate, up, f"ffn{l}_d_gate_up")
        d_wgu = jnp.concatenate([_split_cols(_matmul(n, d_gate, "tn", f"ffn{l}_d_w_gate"), N_CHIPS // 2),
                                 _split_cols(_matmul(n, d_up, "tn", f"ffn{l}_d_w_up"), N_CHIPS // 2)], axis=0)
        dn = _matmul(d_up, w_gu[l][:, f:], "nt", f"ffn{l}_d_norm_up",
                     res=_matmul(d_gate, w_gu[l][:, :f], "nt", f"ffn{l}_d_norm_gate"))
        dh_in, dg = _rms_bwd(h, ffn_g[l], dn, dh, f"ffn{l}_norm_bwd")
        return dh_in, d_wgu, d_wdn, dg

    dh3, d_wgu1, d_wdn1, d_ffn_g1 = ffn_bwd(dh4, h3, 1, ffn1_saved)
    d_wo = _matmul(s_o, dh3, "tn", "d_w_o")
    d_so = _matmul(dh3, w_o, "nt", "d_sb_o")
    d_sq, d_sk, d_sv = _sb_bwd(sq, kv, d_so, sb_width)
    d_wq = _matmul(n_sb, d_sq, "tn", "d_w_q")
    dh2, d_sb_g = _rms_bwd(h2, sb_norm_g, _matmul(d_sq, w_q, "nt", "d_sb_norm"), dh3, "sb_norm_bwd")
    d_wkv = jnp.concatenate([_matmul(n_kv, d_sk, "tn", "d_w_k"), _matmul(n_kv, d_sv, "tn", "d_w_v")], axis=1)
    dn_kv = _matmul(d_sv, w_v, "nt", "d_kv_norm_v", res=_matmul(d_sk, w_k, "nt", "d_kv_norm_k"))
    dh2, d_kv_g = _rms_bwd(h2, kv_g, dn_kv, dh2, "kv_norm_bwd")
    dh1, d_wgu0, d_wdn0, d_ffn_g0 = ffn_bwd(dh2, h1, 0, ffn0_saved)
    d_wout = _matmul(og, dh1, "tn", "d_w_out")
    d_og = _matmul(dh1, w_out, "nt", "d_gdn_gated")
    d_go, d_gate, d_onorm = _gdn_gate_bwd(g_o, proj, gdn_onorm_g, d_og, tm)
    by_chip = [None, _split_rows(d_wout), _split_cols(d_wkv), _split_rows(d_wq), _split_rows(d_wo),
               jnp.stack([d_wgu0, d_wgu1], axis=1),
               jnp.stack([_split_rows(d_wdn0), _split_rows(d_wdn1)], axis=1)]
    (d_gq, d_gk, d_gv, d_gw, d_bw), got_early = _gdn_bwd(gq, gk, gv, gw, bw, g_states, d_go,
                                                         [t.astype(bf16) for t in by_chip[1:]])
    dconv, d_a_in, d_b_in, d_a_log8, d_dt_bias8 = _gdn_prep_bwd_act(
        proj, conv_full, a_log8, dt_bias8, ltri, d_gq, d_gk, d_gv, d_gw, d_bw, FRONT, used, tp)
    dproj, d_conv = _gdn_prep_bwd_conv(proj, conv_full, dconv, d_gate, d_a_in, d_b_in, tp)
    d_win_ext = _matmul(n0, dproj, "tn", "d_w_in")
    dh0, d_gdn_g = _rms_bwd(h0, gdn_g_full, _matmul(dproj, w_in_ext, "nt", "d_gdn_norm"), dh1, "gdn_norm_bwd")
    grad_x = dh0[lo_frames:used][None]
    d_win = jnp.concatenate([d_win_ext[:, :4 * GDN_WIDTH], d_win_ext[:, 4 * GDN_WIDTH:4 * GDN_WIDTH + GDN_HEADS],
                             d_win_ext[:, 4 * GDN_WIDTH + LANES:4 * GDN_WIDTH + LANES + GDN_HEADS]], axis=1)

    by_chip[0] = _split_cols(d_win)
    got = list(_scatter_chips([by_chip[0].astype(bf16)])) + list(got_early)
    chip_arr = jnp.reshape(chip, (1,)).astype(i32)
    over_chips = [_sum_chips(t, g, chip_arr, f"sum_chips_{k}") for k, (t, g) in enumerate(zip(by_chip, got))]
    over_sibling = _swap_sibling(over_chips)
    big_names = ["gdn_w_in", "gdn_w_out", "w_kv", "sb_w_q", "sb_w_o", "ffn_w_gate_up", "ffn_w_down"]
    g_big = dict(zip(big_names, zip(over_chips, over_sibling)))

    small_parts = [dh0[FRONT:lo_frames], d_gdn_g, d_conv, d_a_log8, d_dt_bias8, d_onorm, d_kv_g, d_sb_g,
                   d_ffn_g0, d_ffn_g1, d_fin_g, loss_part]
    s_rows = -(-sum(t.size for t in small_parts) // (8 * PACK_COLS)) * 8
    s_sum = _sum_devices(_gather_all(_pack(small_parts, s_rows, f32)), s_rows)
    (g_meta, g_gdn_g, g_conv, g_a_log8, g_dt8, g_onorm, g_kv_g, g_sb_g, g_ffn_g0, g_ffn_g1, g_fin_g,
     loss_v) = _unpack(s_sum, [t.shape for t in small_parts])
    col_shard = lambda t, w: lax.dynamic_slice_in_dim(t, chip * w, w, axis=t.ndim - 1)

    g_small = dict(
        meta_tokens=col_shard(g_meta, meta_tokens.shape[1]), gdn_norm_g=col_shard(g_gdn_g, gdn_norm_g.shape[1]),
        gdn_conv_w=col_shard(g_conv, gdn_conv_w.shape[2])[None],
        gdn_a_log=g_a_log8[:, :GDN_HEADS], gdn_dt_bias=g_dt8[:, :GDN_HEADS], gdn_onorm_g=g_onorm,
        kv_norm_g=g_kv_g.reshape(-1), sb_norm_g=g_sb_g, ffn_norm_g=jnp.concatenate([g_ffn_g0, g_ffn_g1], axis=0),
        final_norm_g=g_fin_g.reshape(-1))

    grads, delta, new_m, new_v = {}, {}, {}, {}
    for n in names:
        gs = g_big[n] if n in g_big else (g_small[n],)
        grads[n], delta[n], new_m[n], new_v[n] = _adamw(weights[n], gs, m_in[n], v_in[n], f"adamw_{n}")
    loss = loss_v[0, 0]
    return (loss, grad_x, *[grads[n] for n in names], *[delta[n] for n in names],
            *[new_m[n] for n in names], *[new_v[n] for n in names])
```

```python
import functools

import jax
import jax.numpy as jnp
from jax import lax
from jax.experimental import pallas as pl
from jax.experimental.pallas import tpu as pltpu

f32 = jnp.float32
bf16 = jnp.bfloat16
i32 = jnp.int32

EPS = 1e-6
N_META = 16
CHUNK = 64
FRONT = (-N_META) % CHUNK
GDN_HEADS = 8
GDN_DIM = 128
GDN_WIDTH = GDN_HEADS * GDN_DIM
CONV_WIDTH = 4
SB_DIM = 64
SB_BLOCK = 128
SB_FWD_HEADS = 4
SB_FIRST = 3
SB_UNDERFLOW = 104.0
LANES = 128
PACK_COLS = 1024
N_CHIPS = 4
N_DEV = 8
ADAM_LR, ADAM_B1, ADAM_B2, ADAM_EPS, ADAM_WD, ADAM_STEP = 0.001, 0.9, 0.999, 1e-08, 0.01, 10
VMEM_LIMIT = 56 * 1024 * 1024
MESH = pl.DeviceIdType.MESH


def _pick(n, prefs):
    for p in prefs:
        if n % p == 0:
            return p
    return n


def _params(sem):
    return pltpu.CompilerParams(dimension_semantics=sem, vmem_limit_bytes=VMEM_LIMIT)


_DIMS = {"nn": ((1,), (0,)), "nt": ((1,), (1,)), "tn": ((0,), (0,))}


def _bdot(a, b, mode):
    return lax.dot_general(a.astype(bf16), b.astype(bf16), (_DIMS[mode], ((), ())), preferred_element_type=f32)


def _matmul(a, b, mode, name, res=None, out_dtype=f32):
    if mode == "nn":
        (m, k), n = a.shape, b.shape[1]
    elif mode == "nt":
        (m, k), n = a.shape, b.shape[0]
    else:
        (k, m), n = a.shape, b.shape[1]
    tm = _pick(m, (640, 1408, 1024, 512, 384, 256, 128))
    tn = _pick(n, (1408, 2176, 1024, 512, 384, 256, 128))
    tk = _pick(k, (1408, 2176, 1024, 640, 512, 384, 256, 128))
    nk = k // tk
    a_spec = pl.BlockSpec((tk, tm), lambda i, j, kk: (kk, i)) if mode == "tn" else pl.BlockSpec((tm, tk), lambda i, j, kk: (i, kk))
    b_spec = pl.BlockSpec((tn, tk), lambda i, j, kk: (j, kk)) if mode == "nt" else pl.BlockSpec((tk, tn), lambda i, j, kk: (kk, j))
    o_spec = pl.BlockSpec((tm, tn), lambda i, j, kk: (i, j))
    has_res = res is not None

    def body(*refs):
        if has_res:
            a_ref, b_ref, r_ref, o_ref, acc = refs
        else:
            a_ref, b_ref, o_ref, acc = refs
        kk = pl.program_id(2)

        @pl.when(kk == 0)
        def _():
            acc[...] = jnp.zeros_like(acc)

        acc[...] += _bdot(a_ref[...], b_ref[...], mode)

        @pl.when(kk == nk - 1)
        def _():
            y = acc[...]
            if has_res:
                y = y + r_ref[...]
            o_ref[...] = y.astype(o_ref.dtype)

    ins = [a, b] + ([res] if has_res else [])
    specs = [a_spec, b_spec] + ([o_spec] if has_res else [])
    return pl.pallas_call(
        body, name=name, out_shape=jax.ShapeDtypeStruct((m, n), out_dtype), grid=(m // tm, n // tn, nk),
        in_specs=specs, out_specs=o_spec, scratch_shapes=[pltpu.VMEM((tm, tn), f32)],
        compiler_params=_params(("parallel", "parallel", "arbitrary")),
    )(*ins)


def _rowwise(fn, name, rows, tm, ins, outs, reds=()):
    n_in, n_out, n_red = len(ins), len(outs), len(reds)
    in_specs = []
    for arr, spec in ins:
        if spec is None:
            in_specs.append(pl.BlockSpec(arr.shape, lambda i, nd=arr.ndim: (0,) * nd))
        else:
            w, cb = spec
            in_specs.append(pl.BlockSpec((tm, w), lambda i, cb=cb: (i, cb)))
    out_specs = [pl.BlockSpec((tm, w), lambda i: (i, 0)) for w, _ in outs]
    out_specs += [pl.BlockSpec(s, lambda i, nd=len(s): (0,) * nd) for s in reds]
    out_shape = [jax.ShapeDtypeStruct((rows, w), dt) for w, dt in outs]
    out_shape += [jax.ShapeDtypeStruct(s, f32) for s in reds]

    def body(*refs):
        i = pl.program_id(0)
        vals = fn(i, *[r[...] for r in refs[:n_in]])
        for r, v in zip(refs[n_in:n_in + n_out], vals[:n_out]):
            r[...] = v.astype(r.dtype)
        red_refs = refs[n_in + n_out:]

        @pl.when(i == 0)
        def _():
            for r in red_refs:
                r[...] = jnp.zeros_like(r)

        for r, v in zip(red_refs, vals[n_out:]):
            r[...] += v

    res = pl.pallas_call(
        body, name=name, out_shape=out_shape, grid=(rows // tm,), in_specs=in_specs, out_specs=out_specs,
        compiler_params=_params(("arbitrary",)),
    )(*[a for a, _ in ins])
    return res


def _rms(x, g):
    return x * lax.rsqrt(jnp.mean(x * x, axis=-1, keepdims=True) + EPS) * g


def _row_mask(i, tm, lo, hi, shape):
    r = i * tm + lax.broadcasted_iota(i32, shape, 0)
    return (r >= lo) & (r < hi)


def _rms_fwd(x, g, name):
    rows, d = x.shape
    tm = _pick(rows, (640, 512, 384, 256, 128))
    return _rowwise(lambda i, xv, gv: (_rms(xv, gv),), name, rows, tm, [(x, (d, 0)), (g, None)], [(d, bf16)])[0]


def _rms_bwd(x, g, dn, res, name):
    rows, d = x.shape
    tm = _pick(rows, (640, 512, 384, 256, 128))

    def fn(i, xv, gv, dnv, rv):
        _, vjp = jax.vjp(_rms, xv, gv)
        dx, dg = vjp(dnv)
        return rv + dx, dg

    return _rowwise(fn, name, rows, tm, [(x, (d, 0)), (g, None), (dn, (d, 0)), (res, (d, 0))], [(d, f32)], [(1, d)])


def _swiglu(gate, up):
    return jax.nn.silu(gate) * up


def _ffn_up(n, w_gu, name):
    rows, d = n.shape
    f = w_gu.shape[1] // 2
    tm = _pick(rows, (640, 512, 384, 256, 128))
    tn = _pick(f, (1408, 1024, 512, 384, 256, 128))
    nj = f // tn

    def body(n_r, wg_r, wu_r, g_r, u_r, a_r):
        g = jnp.dot(n_r[...], wg_r[...], preferred_element_type=f32)
        u = jnp.dot(n_r[...], wu_r[...], preferred_element_type=f32)
        g_r[...] = g
        u_r[...] = u
        a_r[...] = _swiglu(g, u).astype(a_r.dtype)

    o_spec = pl.BlockSpec((tm, tn), lambda j, i: (i, j))
    return pl.pallas_call(
        body, name=name, grid=(nj, rows // tm),
        out_shape=[jax.ShapeDtypeStruct((rows, f), f32)] * 2 + [jax.ShapeDtypeStruct((rows, f), bf16)],
        in_specs=[pl.BlockSpec((tm, d), lambda j, i: (i, 0)), pl.BlockSpec((d, tn), lambda j, i: (0, j)),
                  pl.BlockSpec((d, tn), lambda j, i: (0, nj + j))],
        out_specs=[o_spec] * 3, compiler_params=_params(("parallel", "parallel")),
    )(n, w_gu, w_gu)


def _ffn_dnorm(d_gate, d_up, w_gu, name):
    rows, f = d_gate.shape
    d = w_gu.shape[0]
    tm = _pick(rows, (640, 512, 384, 256, 128))
    tk = _pick(f, (1408, 1024, 512, 384, 256, 128))
    nk = f // tk

    def body(dg_r, du_r, wg_r, wu_r, o_r, acc):
        kk = pl.program_id(1)

        @pl.when(kk == 0)
        def _():
            acc[...] = jnp.zeros_like(acc)

        acc[...] += _bdot(dg_r[...], wg_r[...], "nt") + _bdot(du_r[...], wu_r[...], "nt")

        @pl.when(kk == nk - 1)
        def _():
            o_r[...] = acc[...]

    a_spec = pl.BlockSpec((tm, tk), lambda i, kk: (i, kk))
    return pl.pallas_call(
        body, name=name, out_shape=jax.ShapeDtypeStruct((rows, d), f32), grid=(rows // tm, nk),
        in_specs=[a_spec, a_spec, pl.BlockSpec((d, tk), lambda i, kk: (0, kk)), pl.BlockSpec((d, tk), lambda i, kk: (0, nk + kk))],
        out_specs=pl.BlockSpec((tm, d), lambda i, kk: (i, 0)), scratch_shapes=[pltpu.VMEM((tm, d), f32)],
        compiler_params=_params(("parallel", "arbitrary")),
    )(d_gate, d_up, w_gu, w_gu)


def _ffn_dact(dh, w_dn, gate, up, name):
    rows, d = dh.shape
    f = w_dn.shape[0]
    tm = _pick(rows, (640, 512, 384, 256, 128))
    tn = _pick(f, (1408, 1024, 512, 384, 256, 128))

    def body(dh_r, w_r, g_r, u_r, dg_r, du_r):
        dact = _bdot(dh_r[...], w_r[...], "nt")
        _, vjp = jax.vjp(_swiglu, g_r[...], u_r[...])
        dg, du = vjp(dact)
        dg_r[...] = dg.astype(dg_r.dtype)
        du_r[...] = du.astype(du_r.dtype)

    t_spec = pl.BlockSpec((tm, tn), lambda j, i: (i, j))
    return pl.pallas_call(
        body, name=name, grid=(f // tn, rows // tm), out_shape=[jax.ShapeDtypeStruct((rows, f), bf16)] * 2,
        in_specs=[pl.BlockSpec((tm, d), lambda j, i: (i, 0)), pl.BlockSpec((tn, d), lambda j, i: (j, 0)), t_spec, t_spec],
        out_specs=[t_spec] * 2, compiler_params=_params(("parallel", "parallel")),
    )(dh, w_dn, gate, up)


def _loss_head(h, g, tgt, lo, hi, name):
    rows, d = h.shape
    tm = _pick(rows, (640, 512, 384, 256, 128))

    def fn(i, hv, gv, tv):
        mask = _row_mask(i, tm, lo, hi, (tm, 1))

        def f(hh, gg):
            err = _rms(hh, gg) - tv
            per_row = jnp.where(mask, jnp.mean(err * err, axis=-1, keepdims=True), 0.0)
            return 0.5 * jnp.sum(per_row, axis=0, keepdims=True)

        loss, vjp = jax.vjp(f, hv, gv)
        dh, dg = vjp(jnp.ones_like(loss))
        return dh, dg, jnp.broadcast_to(loss, (1, LANES))

    return _rowwise(fn, name, rows, tm, [(h, (d, 0)), (g, None), (tgt, (d, 0))], [(d, f32)], [(1, d), (1, LANES)])


def _heads_l2(x):
    t = x.shape[0]
    x3 = x.reshape(t, GDN_HEADS, GDN_DIM)
    return (x3 * lax.rsqrt(jnp.sum(x3 * x3, axis=-1, keepdims=True) + EPS)).reshape(t, GDN_WIDTH)


def _gdn_act(conv, a_in, b_in, a_log, dt_bias, mask):
    s = jax.nn.silu(conv)
    q = _heads_l2(s[:, :GDN_WIDTH])
    k = _heads_l2(s[:, GDN_WIDTH:2 * GDN_WIDTH])
    v = s[:, 2 * GDN_WIDTH:]
    g = jnp.where(mask, -jnp.exp(a_log) * jax.nn.softplus(a_in + dt_bias), 0.0)
    beta = jnp.where(mask, jax.nn.sigmoid(b_in), 0.0)
    return q, k, v, g, beta


def _widen(x8):
    return [jnp.broadcast_to(x8[:, h:h + 1], (x8.shape[0], GDN_DIM)) for h in range(GDN_HEADS)]


def _narrow(per_head):
    t = per_head[0].shape[0]
    lane = lax.broadcasted_iota(i32, (t, LANES), 1)
    out = jnp.zeros((t, LANES), f32)
    for h, x in enumerate(per_head):
        out = out + jnp.where(lane == h, jnp.sum(x, axis=1, keepdims=True), 0.0)
    return out


def _conv_taps(cur, prev8, w):
    tm = cur.shape[0]
    cat = jnp.concatenate([prev8, cur], axis=0)
    y = cur * w[CONV_WIDTH - 1:CONV_WIDTH, :]
    for j in range(1, CONV_WIDTH):
        y = y + pltpu.roll(cat, j, axis=0)[8:8 + tm, :] * w[CONV_WIDTH - 1 - j:CONV_WIDTH - j, :]
    return y


def _gdn_prep_specs(proj, tm):
    c3 = 3 * GDN_WIDTH
    ab = 4 * GDN_WIDTH // LANES
    t8 = tm // 8
    return [
        pl.BlockSpec((tm, c3), lambda i: (i, 0)),
        pl.BlockSpec((8, c3), lambda i: (jnp.maximum(i * t8 - 1, 0), 0)),
        pl.BlockSpec((tm, LANES), lambda i: (i, ab)),
        pl.BlockSpec((tm, LANES), lambda i: (i, ab + 1)),
    ]


def _full(arr):
    return pl.BlockSpec(arr.shape, lambda i, nd=arr.ndim: (0,) * nd)


def _gdn_prep_fwd(proj, conv_w, a_log, dt_bias, ltri, lo, hi, tm):
    rows = proj.shape[0]

    def body(cur, prev8, a_in, b_in, w, al, dtb, lt, q_o, k_o, v_o, g_o, b_o):
        i = pl.program_id(0)
        mask = _row_mask(i, tm, lo, hi, (tm, LANES)) & (lax.broadcasted_iota(i32, (tm, LANES), 1) < GDN_HEADS)
        conv = _conv_taps(cur[...], prev8[...], w[...])
        q, k, v, g, beta = _gdn_act(conv, a_in[...], b_in[...], al[...], dtb[...], mask)
        q_o[...] = q
        k_o[...] = k
        v_o[...] = v
        gcum = jnp.dot(lt[...], g, preferred_element_type=f32, precision=lax.Precision.HIGHEST)
        g_o[...] = gcum
        b_o[...] = beta

    wide = jax.ShapeDtypeStruct((rows, GDN_WIDTH), f32)
    narrow = jax.ShapeDtypeStruct((rows, LANES), f32)
    o_spec = pl.BlockSpec((tm, GDN_WIDTH), lambda i: (i, 0))
    n_spec = pl.BlockSpec((tm, LANES), lambda i: (i, 0))
    return pl.pallas_call(
        body, name="gdn_prep_fwd", out_shape=[wide] * 3 + [narrow] * 2, grid=(rows // tm,),
        in_specs=_gdn_prep_specs(proj, tm) + [_full(conv_w), _full(a_log), _full(dt_bias), _full(ltri)],
        out_specs=[o_spec] * 3 + [n_spec] * 2, compiler_params=_params(("parallel",)),
    )(proj, proj, proj, proj, conv_w, a_log, dt_bias, ltri)


def _gdn_prep_bwd_act(proj, conv_w, a_log, dt_bias, ltri, dq, dk, dv, dgw, dbw, lo, hi, tm):
    rows = proj.shape[0]
    c3 = 3 * GDN_WIDTH

    def body(cur, prev8, a_in, b_in, w, al, dtb, lt, dq_r, dk_r, dv_r, dg_r, db_r, dconv_o, da_o, dbin_o, dal_o, ddt_o):
        i = pl.program_id(0)
        mask = _row_mask(i, tm, lo, hi, (tm, LANES)) & (lax.broadcasted_iota(i32, (tm, LANES), 1) < GDN_HEADS)
        conv = _conv_taps(cur[...], prev8[...], w[...])
        dg = lax.dot_general(lt[...], dg_r[...], (((0,), (0,)), ((), ())), preferred_element_type=f32,
                             precision=lax.Precision.HIGHEST)
        dbeta = db_r[...]
        _, vjp = jax.vjp(lambda c, a, b, x, y: _gdn_act(c, a, b, x, y, mask), conv, a_in[...], b_in[...], al[...], dtb[...])
        dconv, da, dbin, dal, ddt = vjp((dq_r[...], dk_r[...], dv_r[...], dg, dbeta))
        dconv_o[...] = dconv
        da_o[...] = da
        dbin_o[...] = dbin

        @pl.when(i == 0)
        def _():
            dal_o[...] = jnp.zeros_like(dal_o)
            ddt_o[...] = jnp.zeros_like(ddt_o)

        dal_o[...] += dal
        ddt_o[...] += ddt

    w_spec = pl.BlockSpec((tm, GDN_WIDTH), lambda i: (i, 0))
    n_spec = pl.BlockSpec((tm, LANES), lambda i: (i, 0))
    s_spec = pl.BlockSpec((1, LANES), lambda i: (0, 0))
    return pl.pallas_call(
        body, name="gdn_prep_bwd_act",
        out_shape=[jax.ShapeDtypeStruct((rows, c3), f32), jax.ShapeDtypeStruct((rows, LANES), f32),
                   jax.ShapeDtypeStruct((rows, LANES), f32), jax.ShapeDtypeStruct((1, LANES), f32),
                   jax.ShapeDtypeStruct((1, LANES), f32)],
        grid=(rows // tm,),
        in_specs=_gdn_prep_specs(proj, tm) + [_full(conv_w), _full(a_log), _full(dt_bias), _full(ltri)] + [w_spec] * 3 + [n_spec] * 2,
        out_specs=[pl.BlockSpec((tm, c3), lambda i: (i, 0)), n_spec, n_spec, s_spec, s_spec],
        compiler_params=_params(("arbitrary",)),
    )(proj, proj, proj, proj, conv_w, a_log, dt_bias, ltri, dq, dk, dv, dgw, dbw)


def _gdn_prep_bwd_conv(proj, conv_w, dconv, dgate, da, dbin, tm):
    rows, width = proj.shape
    c3 = 3 * GDN_WIDTH
    t8 = tm // 8
    nt = rows // tm

    def body(cur, prev8, w, dc, dnext8, dgt, da_r, db_r, dp_o, dw_o):
        i = pl.program_id(0)
        d = dc[...]
        nxt = jnp.where(i == nt - 1, 0.0, dnext8[...])
        cat = jnp.concatenate([d, nxt], axis=0)
        wv = w[...]
        dx = d * wv[CONV_WIDTH - 1:CONV_WIDTH, :]
        for j in range(1, CONV_WIDTH):
            dx = dx + pltpu.roll(cat, tm + 8 - j, axis=0)[:tm, :] * wv[CONV_WIDTH - 1 - j:CONV_WIDTH - j, :]
        dp_o[:, :c3] = dx.astype(bf16)
        dp_o[:, c3:4 * GDN_WIDTH] = dgt[...].astype(bf16)
        dp_o[:, 4 * GDN_WIDTH:4 * GDN_WIDTH + LANES] = da_r[...].astype(bf16)
        dp_o[:, 4 * GDN_WIDTH + LANES:] = db_r[...].astype(bf16)

        xcat = jnp.concatenate([prev8[...], cur[...]], axis=0)
        parts = [jnp.sum(d * cur[...], axis=0, keepdims=True)]
        for j in range(1, CONV_WIDTH):
            parts.append(jnp.sum(d * pltpu.roll(xcat, j, axis=0)[8:8 + tm, :], axis=0, keepdims=True))
        dwt = jnp.concatenate(parts[::-1], axis=0)

        @pl.when(i == 0)
        def _():
            dw_o[...] = jnp.zeros_like(dw_o)

        dw_o[...] += dwt

    n_spec = pl.BlockSpec((tm, LANES), lambda i: (i, 0))
    return pl.pallas_call(
        body, name="gdn_prep_bwd_conv",
        out_shape=[jax.ShapeDtypeStruct((rows, width), bf16), jax.ShapeDtypeStruct((CONV_WIDTH, c3), f32)],
        grid=(nt,),
        in_specs=[pl.BlockSpec((tm, c3), lambda i: (i, 0)),
                  pl.BlockSpec((8, c3), lambda i: (jnp.maximum(i * t8 - 1, 0), 0)),
                  _full(conv_w),
                  pl.BlockSpec((tm, c3), lambda i: (i, 0)),
                  pl.BlockSpec((8, c3), lambda i: (jnp.minimum((i + 1) * t8, rows // 8 - 1), 0)),
                  pl.BlockSpec((tm, GDN_WIDTH), lambda i: (i, 0)), n_spec, n_spec],
        out_specs=[pl.BlockSpec((tm, width), lambda i: (i, 0)), pl.BlockSpec((CONV_WIDTH, c3), lambda i: (0, 0))],
        compiler_params=_params(("arbitrary",)),
    )(proj, proj, conv_w, dconv, dconv, dgate, da, dbin)


def _split(a):
    hi = a.astype(bf16)
    return hi, (a - hi.astype(f32)).astype(bf16)


def _make_mm(dot):
    @jax.custom_vjp
    def nn(a, b):
        return dot(a, b, "nn")

    nn.defvjp(lambda a, b: (dot(a, b, "nn"), (a, b)),
              lambda r, ct: (dot(ct, r[1], "nt"), dot(r[0], ct, "tn")))

    @jax.custom_vjp
    def nt(a, b):
        return dot(a, b, "nt")

    nt.defvjp(lambda a, b: (dot(a, b, "nt"), (a, b)),
              lambda r, ct: (dot(ct, r[1], "nn"), dot(ct, r[0], "tn")))

    @jax.custom_vjp
    def tn(a, b):
        return dot(a, b, "tn")

    tn.defvjp(lambda a, b: (dot(a, b, "tn"), (a, b)),
              lambda r, ct: (dot(r[1], ct, "nt"), dot(r[0], ct, "nn")))
    return nn, nt, tn


_mm, _mm_nt, _mm_tn = _make_mm(_bdot)


def _each(f, *lists):
    return [f(*xs) for xs in zip(*lists)]


def _gdn_chunk(q, k, v, gcb, bcb, s_in):
    c = q[0].shape[0]
    ri = lax.broadcasted_iota(i32, (c, c), 0)
    ci = lax.broadcasted_iota(i32, (c, c), 1)
    incl, strict = ri >= ci, ri > ci
    rowi = lax.broadcasted_iota(i32, gcb[0].shape, 0)
    qs = _each(lambda t: t * (GDN_DIM ** -0.5), q)
    decay = _each(lambda g: jnp.where(incl, jnp.exp(jnp.where(incl, g[:, :c] - g[:, :c].T, 0.0)), 0.0), gcb)
    kk = _each(lambda t: _mm_nt(t, t), k)
    a1 = _each(lambda b, d, t: jnp.where(strict, b[:, :c] * d * t, 0.0), bcb, decay, kk)
    eg = _each(jnp.exp, gcb)
    x = _each(lambda b, vv, e, t: jnp.concatenate([b * vv, (b * e) * t], axis=1), bcb, v, eg, k)
    pows = [a1]
    for _ in range(5):
        pows.append(_each(lambda p: _mm(p, p), pows[-1]))
    for ps in pows[:0:-1]:
        x = _each(lambda p, t: t + _mm(p, t), ps, x)
    x = _each(lambda p, t: t - _mm(p, t), a1, x)
    attn = _each(lambda a, b, d: _mm_nt(a, b) * d, qs, k, decay)
    glast = _each(lambda g: jnp.sum(jnp.where(rowi == c - 1, g, 0.0), axis=0, keepdims=True), gcb)
    u = _each(lambda t, s: t[:, :GDN_DIM] - _mm(t[:, GDN_DIM:], s), x, s_in)
    o = _each(lambda a, e, s, w, uu: _mm(a * e, s) + _mm(w, uu), qs, eg, s_in, attn, u)
    s_out = _each(lambda s, gl, t, g, uu: s * jnp.exp(gl) + _mm_tn(t * jnp.exp(gl - g), uu), s_in, glast, k, gcb, u)
    return o, s_out


def _gdn_heads(ref):
    return [ref[:, h * GDN_DIM:(h + 1) * GDN_DIM] for h in range(GDN_HEADS)]


def _gdn_fwd(q, k, v, gw, bw, shards):
    rows = q.shape[0]
    nc = rows // CHUNK
    num = len(shards)
    blk = pl.BlockSpec((CHUNK, GDN_WIDTH), lambda c: (c, 0))

    def body(*refs):
        q_r, k_r, v_r, g_r, b_r = refs[:5]
        ins = refs[5:5 + num]
        o_r, st_r = refs[5 + num:7 + num]
        outs = refs[7 + num:7 + 2 * num]
        s_sc, send_sems, recv_sems, local_sems = refs[7 + 2 * num:]
        c = pl.program_id(0)
        start, forward, finish = _gather_plan(ins, outs, send_sems, recv_sems, local_sems)

        @pl.when(c == 0)
        def _():
            s_sc[...] = jnp.zeros_like(s_sc)
            start()

        s_in = [s_sc[h] for h in range(GDN_HEADS)]
        st_r[0] = s_sc[...]
        o, s_out = _gdn_chunk(_gdn_heads(q_r), _gdn_heads(k_r), _gdn_heads(v_r), _widen(g_r[...]), _widen(b_r[...]), s_in)
        o_r[...] = jnp.concatenate(o, axis=1)
        for h in range(GDN_HEADS):
            s_sc[h] = s_out[h]
        pl.when(c == nc // 2)(forward)
        pl.when(c == nc - 1)(finish)

    res = pl.pallas_call(
        body, name="gdn_fwd",
        out_shape=[jax.ShapeDtypeStruct((rows, GDN_WIDTH), f32), jax.ShapeDtypeStruct((nc, GDN_HEADS, GDN_DIM, GDN_DIM), f32)]
        + [jax.ShapeDtypeStruct((N_CHIPS,) + t.shape, t.dtype) for t in shards],
        grid=(nc,), in_specs=[blk] * 3 + [pl.BlockSpec((CHUNK, LANES), lambda c: (c, 0))] * 2 + [_ANY] * num,
        out_specs=[blk, pl.BlockSpec((1, GDN_HEADS, GDN_DIM, GDN_DIM), lambda c: (c, 0, 0, 0))] + [_ANY] * num,
        scratch_shapes=[pltpu.VMEM((GDN_HEADS, GDN_DIM, GDN_DIM), f32), pltpu.SemaphoreType.DMA((6 * num,)),
                        pltpu.SemaphoreType.DMA((6 * num,)), pltpu.SemaphoreType.DMA((num,))],
        compiler_params=_params(("arbitrary",)),
    )(q, k, v, gw, bw, *shards)
    return res[0], res[1], res[2:]


def _gdn_bwd(q, k, v, gw, bw, states, do, parts):
    rows = q.shape[0]
    nc = rows // CHUNK
    num = len(parts)
    blk = pl.BlockSpec((CHUNK, GDN_WIDTH), lambda c: (nc - 1 - c, 0))

    def body(*refs):
        q_r, k_r, v_r, g_r, b_r, st_r, do_r = refs[:7]
        ins = refs[7:7 + num]
        dq_r, dk_r, dv_r, dg_r, db_r = refs[7 + num:12 + num]
        outs = refs[12 + num:12 + 2 * num]
        ds_sc, send_sems, recv_sems = refs[12 + 2 * num:]
        c = pl.program_id(0)
        start, finish = _scatter_plan(ins, outs, send_sems, recv_sems)

        @pl.when(c == 0)
        def _():
            ds_sc[...] = jnp.zeros_like(ds_sc)
            start()

        s_in = [st_r[0, h] for h in range(GDN_HEADS)]
        _, vjp = jax.vjp(_gdn_chunk, _gdn_heads(q_r), _gdn_heads(k_r), _gdn_heads(v_r), _widen(g_r[...]), _widen(b_r[...]), s_in)
        dq, dk, dv, dg, db, ds_in = vjp((_gdn_heads(do_r), [ds_sc[h] for h in range(GDN_HEADS)]))
        dq_r[...] = jnp.concatenate(dq, axis=1)
        dk_r[...] = jnp.concatenate(dk, axis=1)
        dv_r[...] = jnp.concatenate(dv, axis=1)
        dg_r[...] = _narrow(dg)
        db_r[...] = _narrow(db)
        for h in range(GDN_HEADS):
            ds_sc[h] = ds_in[h]
        pl.when(c == nc - 1)(finish)

    wide = jax.ShapeDtypeStruct((rows, GDN_WIDTH), f32)
    narrow = jax.ShapeDtypeStruct((rows, LANES), f32)
    nblk = pl.BlockSpec((CHUNK, LANES), lambda c: (nc - 1 - c, 0))
    res = pl.pallas_call(
        body, name="gdn_bwd",
        out_shape=[wide] * 3 + [narrow] * 2 + [jax.ShapeDtypeStruct((3,) + t.shape[1:], t.dtype) for t in parts],
        grid=(nc,),
        in_specs=[blk] * 3 + [nblk] * 2
        + [pl.BlockSpec((1, GDN_HEADS, GDN_DIM, GDN_DIM), lambda c: (nc - 1 - c, 0, 0, 0)), blk] + [_ANY] * num,
        out_specs=[blk] * 3 + [nblk] * 2 + [_ANY] * num,
        scratch_shapes=[pltpu.VMEM((GDN_HEADS, GDN_DIM, GDN_DIM), f32), pltpu.SemaphoreType.DMA((3 * num,)),
                        pltpu.SemaphoreType.DMA((3 * num,))],
        compiler_params=_params(("arbitrary",)),
    )(q, k, v, gw, bw, states, do, *parts)
    return res[:5], res[5:]


def _gdn_gate(o, gate, og):
    t = o.shape[0]
    o3 = o.reshape(t, GDN_HEADS, GDN_DIM)
    n = o3 * lax.rsqrt(jnp.mean(o3 * o3, axis=-1, keepdims=True) + EPS) * og.reshape(1, 1, GDN_DIM)
    return n.reshape(t, GDN_WIDTH) * jax.nn.silu(gate)


def _gdn_gate_fwd(o, proj, og, tm):
    rows = o.shape[0]
    return _rowwise(lambda i, ov, gv, w: (_gdn_gate(ov, gv, w),), "gdn_gate_fwd", rows, tm,
                    [(o, (GDN_WIDTH, 0)), (proj, (GDN_WIDTH, 3)), (og, None)], [(GDN_WIDTH, bf16)])[0]


def _gdn_gate_bwd(o, proj, og, dy, tm):
    rows = o.shape[0]

    def fn(i, ov, gv, w, d):
        _, vjp = jax.vjp(_gdn_gate, ov, gv, w)
        return vjp(d)

    return _rowwise(fn, "gdn_gate_bwd", rows, tm,
                    [(o, (GDN_WIDTH, 0)), (proj, (GDN_WIDTH, 3)), (og, None), (dy, (GDN_WIDTH, 0))],
                    [(GDN_WIDTH, f32), (GDN_WIDTH, f32)], [(1, GDN_DIM)])


def _sb_visible(i, j, valid):
    qpos = i * SB_BLOCK + lax.broadcasted_iota(i32, (SB_BLOCK, SB_BLOCK), 0)
    kpos = j * SB_BLOCK + lax.broadcasted_iota(i32, (SB_BLOCK, SB_BLOCK), 1)
    return (kpos < qpos) & (kpos >= FRONT) & valid


def _sb_logs(z, vis):
    l1p = jnp.log(1.0 + jnp.exp(-jnp.abs(z)))
    return -(jnp.maximum(-z, 0.0) + l1p), jnp.where(vis, -(jnp.maximum(z, 0.0) + l1p), 0.0)


def _tri_sum(x, tri):
    hi, lo = _split(x)
    return jnp.dot(hi, tri, preferred_element_type=f32) + jnp.dot(lo, tri, preferred_element_type=f32)


def _sb_live(t, i, runs):
    return (t <= i) & (jnp.max(functools.reduce(jnp.maximum, runs)) > -SB_UNDERFLOW)


def _sb_blocks(i, t, nb):
    js = [i - t - b for b in range(nb)]
    kss = [pl.ds(pl.multiple_of(jnp.maximum(j, 0) * SB_BLOCK, SB_BLOCK), SB_BLOCK) for j in js]
    return kss, [_sb_visible(i, j, j >= 0) for j in js]


def _sb_weights(i, t, nb, qs, sls, k_r, runs, after, scale):
    nh = len(qs)
    kss, vis = _sb_blocks(i, t, nb)
    units = [(a, b) for b in range(nb) for a in range(nh)]
    z = [_bdot(qs[a], k_r[kss[b], sls[a]], "nt") * scale for a, b in units]
    logs = [_sb_logs(zz, vis[b]) for zz, (a, b) in zip(z, units)]
    later = [_tri_sum(l[1], after) for l in logs]
    sums = [jnp.sum(l[1], axis=1, keepdims=True) for l in logs]
    w = []
    runs = list(runs)
    for b in range(nb):
        for a in range(nh):
            u = b * nh + a
            w.append(jnp.where(vis[b], jnp.exp(logs[u][0] + later[u] + runs[a]), 0.0))
        runs = [runs[a] + sums[b * nh + a] for a in range(nh)]
    return kss, vis, units, logs, w, tuple(runs)


def _sb_fwd(q, kv, width):
    rows = q.shape[0]
    nq = rows // SB_BLOCK
    lanes = SB_FWD_HEADS * SB_DIM
    npair = width // lanes
    scale = SB_DIM ** -0.5

    def body(q_r, k_r, v_r, o_r):
        i = pl.program_id(1)
        rj = lax.broadcasted_iota(i32, (SB_BLOCK, SB_BLOCK), 0)
        cs = lax.broadcasted_iota(i32, (SB_BLOCK, SB_BLOCK), 1)
        after = (rj > cs).astype(bf16)
        sls = [slice(a * SB_DIM, (a + 1) * SB_DIM) for a in range(SB_FWD_HEADS)]
        qs = [q_r[:, sl] for sl in sls]

        def step(carry, nb):
            t, accs, runs = carry
            kss, _, units, _, w, runs = _sb_weights(i, t, nb, qs, sls, k_r, runs, after, scale)
            prods = [_bdot(ww, v_r[kss[b], sls[a]], "nn") for ww, (a, b) in zip(w, units)]
            accs = tuple(functools.reduce(jnp.add, [accs[a]] + prods[a::SB_FWD_HEADS]) for a in range(SB_FWD_HEADS))
            return t + nb, accs, runs

        init = (jnp.int32(0), tuple(jnp.zeros((SB_BLOCK, SB_DIM), f32) for _ in sls),
                tuple(jnp.zeros((SB_BLOCK, 1), f32) for _ in sls))
        _, accs, _ = lax.while_loop(lambda c: _sb_live(c[0], i, c[2]), lambda c: step(c, 2), step(init, SB_FIRST))
        o_r[...] = jnp.concatenate(accs, axis=1)

    return pl.pallas_call(
        body, name="sb_fwd", out_shape=jax.ShapeDtypeStruct((rows, width), f32), grid=(npair, nq),
        in_specs=[pl.BlockSpec((SB_BLOCK, lanes), lambda p, i: (i, p)),
                  pl.BlockSpec((rows, lanes), lambda p, i: (0, p)),
                  pl.BlockSpec((rows, lanes), lambda p, i: (0, npair + p))],
        out_specs=pl.BlockSpec((SB_BLOCK, lanes), lambda p, i: (i, p)),
        compiler_params=_params(("parallel", "arbitrary")),
    )(q, kv, kv)


def _sb_bwd(q, kv, do, width):
    rows = q.shape[0]
    nq = rows // SB_BLOCK
    npair = width // LANES
    nh = LANES // SB_DIM
    scale = SB_DIM ** -0.5

    def body(q_r, k_r, v_r, do_r, dq_r, dk_r, dv_r, e_sc, sig_sc, w_sc):
        i = pl.program_id(1)

        @pl.when(i == 0)
        def _():
            dk_r[...] = jnp.zeros_like(dk_r)
            dv_r[...] = jnp.zeros_like(dv_r)

        rj = lax.broadcasted_iota(i32, (SB_BLOCK, SB_BLOCK), 0)
        cs = lax.broadcasted_iota(i32, (SB_BLOCK, SB_BLOCK), 1)
        after = (rj > cs).astype(bf16)
        from_s = (rj >= cs).astype(bf16)
        zero1 = jnp.zeros((SB_BLOCK, 1), f32)
        sls = [slice(a * SB_DIM, (a + 1) * SB_DIM) for a in range(nh)]
        qs = [q_r[:, sl] for sl in sls]
        dos = [do_r[:, sl] for sl in sls]

        def weigh(carry, nb):
            t, runs, eruns = carry
            kss, _, units, logs, w, runs = _sb_weights(i, t, nb, qs, sls, k_r, runs, after, scale)
            dw = [_bdot(dos[a], v_r[kss[b], sls[a]], "nt") for a, b in units]
            e = [ww * d for ww, d in zip(w, dw)]
            for u, (a, b) in enumerate(units):
                e_sc[a, t + b] = e[u]
                sig_sc[a, t + b] = jnp.exp(logs[u][0])
                w_sc[a, t + b] = w[u].astype(w_sc.dtype)
            sums = [jnp.sum(ee, axis=1, keepdims=True) for ee in e]
            eruns = tuple(functools.reduce(jnp.add, [eruns[a]] + sums[a::nh]) for a in range(nh))
            return t + nb, runs, eruns

        n_blk, _, etots = lax.while_loop(lambda c: _sb_live(c[0], i, c[1]), lambda c: weigh(c, 2),
                                         weigh((jnp.int32(0), (zero1,) * nh, (zero1,) * nh), SB_FIRST))

        def push(t, carry, nb):
            dqs, eruns = carry
            kss, vis = _sb_blocks(i, t, nb)
            units = [(a, b) for b in range(nb) for a in range(nh)]
            e = [e_sc[a, t + b] for a, b in units]
            dvs = [_bdot(w_sc[a, t + b], dos[a], "tn") for a, b in units]
            upto = [_tri_sum(ee, from_s) for ee in e]
            sums = [jnp.sum(ee, axis=1, keepdims=True) for ee in e]
            dz = []
            eruns = list(eruns)
            for b in range(nb):
                for a in range(nh):
                    u = b * nh + a
                    sig = sig_sc[a, t + b]
                    before = etots[a] - eruns[a] - upto[u]
                    dz.append(jnp.where(vis[b], e[u] * (1.0 - sig) - before * sig, 0.0) * scale)
                eruns = [eruns[a] + sums[b * nh + a] for a in range(nh)]
            dks = [_bdot(d, qs[a], "tn") for d, (a, b) in zip(dz, units)]
            dqp = [_bdot(d, k_r[kss[b], sls[a]], "nn") for d, (a, b) in zip(dz, units)]
            for b in range(nb):
                dk_r[kss[b], :] += jnp.concatenate(dks[b * nh:(b + 1) * nh], axis=1)
                dv_r[kss[b], :] += jnp.concatenate(dvs[b * nh:(b + 1) * nh], axis=1)
            dqs = tuple(functools.reduce(jnp.add, [dqs[a]] + dqp[a::nh]) for a in range(nh))
            return dqs, tuple(eruns)

        first = push(jnp.int32(0), (tuple(jnp.zeros((SB_BLOCK, SB_DIM), f32) for _ in sls), (zero1,) * nh), SB_FIRST)
        dqs, _ = lax.fori_loop(0, (n_blk - SB_FIRST) // 2, lambda p, c: push(SB_FIRST + 2 * p, c, 2), first)
        dq_r[...] = jnp.concatenate(dqs, axis=1)

    blk = pl.BlockSpec((SB_BLOCK, LANES), lambda p, i: (i, p))
    col = pl.BlockSpec((rows, LANES), lambda p, i: (0, p))
    wide = jax.ShapeDtypeStruct((rows, width), f32)
    depth = nq + SB_FIRST
    return pl.pallas_call(
        body, name="sb_bwd", out_shape=[wide] * 3, grid=(npair, nq),
        in_specs=[blk, col, pl.BlockSpec((rows, LANES), lambda p, i: (0, npair + p)), blk],
        out_specs=[blk, col, col],
        scratch_shapes=[pltpu.VMEM((nh, depth, SB_BLOCK, SB_BLOCK), f32), pltpu.VMEM((nh, depth, SB_BLOCK, SB_BLOCK), f32),
                        pltpu.VMEM((nh, depth, SB_BLOCK, SB_BLOCK), bf16)],
        compiler_params=_params(("parallel", "arbitrary")),
    )(q, kv, kv, do)


_FLIPS = ((1, 0), (0, 1), (1, 1))
_ANY = pl.BlockSpec(memory_space=pl.ANY)


def _flip(v, a):
    return v + a - 2 * a * v


def _gather_plan(ins, outs, send_sems, recv_sems, local_sems):
    num = len(ins)
    x, y, c = lax.axis_index("x"), lax.axis_index("y"), lax.axis_index("c")
    me, sibling = (x, y, c), (x, y, 1 - c)
    chip = 2 * x + y
    others = [(_flip(x, a), _flip(y, b)) for a, b in _FLIPS]
    pairs = [(k, n, 2 * ox + oy) for k in range(num) for n, (ox, oy) in enumerate(others)]

    def half_of(ref, hc):
        half = ref.shape[0] // 2
        start = hc * half
        for align in (16, 8):
            if half % align == 0:
                start = pl.multiple_of(start, align)
                break
        return ref.at[pl.ds(start, half)]

    def copy(k, n, s, hc, to, src=None):
        dst = half_of(outs[k].at[s], hc)
        return pltpu.make_async_remote_copy(
            src_ref=dst if src is None else src, dst_ref=dst,
            send_sem=send_sems.at[6 * k + n], recv_sem=recv_sems.at[6 * k + n], device_id=to, device_id_type=MESH)

    mine = [pltpu.make_async_copy(ins[k], outs[k].at[chip], local_sems.at[k]) for k in range(num)]
    first = [copy(k, n, chip, c, (others[n][0], others[n][1], c), src=half_of(ins[k], c)) for k, n, _ in pairs]
    passed = [copy(k, 3 + n, s, c, sibling) for k, n, s in pairs]

    def start():
        for cp in mine + first:
            cp.start()

    def forward():
        for (k, n, s), fw in zip(pairs, passed):
            copy(k, n, s, c, me).wait_recv()
            fw.start()

    def finish():
        for k, n, s in pairs:
            copy(k, 3 + n, s, 1 - c, me).wait_recv()
        for cp in first + passed:
            cp.wait_send()
        for cp in mine:
            cp.wait()

    return start, forward, finish


def _gather_chips(shards):
    num = len(shards)

    def body(*refs):
        for phase in _gather_plan(refs[:num], refs[num:2 * num], *refs[2 * num:]):
            phase()

    return pl.pallas_call(
        body, name="gather_chips", out_shape=[jax.ShapeDtypeStruct((N_CHIPS,) + t.shape, t.dtype) for t in shards],
        in_specs=[_ANY] * num, out_specs=[_ANY] * num,
        scratch_shapes=[pltpu.SemaphoreType.DMA((6 * num,)), pltpu.SemaphoreType.DMA((6 * num,)),
                        pltpu.SemaphoreType.DMA((num,))],
    )(*shards)


def _scatter_plan(ins, outs, send_sems, recv_sems):
    x, y, c = lax.axis_index("x"), lax.axis_index("y"), lax.axis_index("c")
    cps = []
    for k in range(len(ins)):
        for n, (a, b) in enumerate(_FLIPS):
            ox, oy = _flip(x, a), _flip(y, b)
            cps.append(pltpu.make_async_remote_copy(
                src_ref=ins[k].at[2 * ox + oy], dst_ref=outs[k].at[n], send_sem=send_sems.at[3 * k + n],
                recv_sem=recv_sems.at[3 * k + n], device_id=(ox, oy, c), device_id_type=MESH))

    def start():
        for cp in cps:
            cp.start()

    def finish():
        for cp in cps:
            cp.wait()

    return start, finish


def _scatter_chips(parts):
    num = len(parts)

    def body(*refs):
        for phase in _scatter_plan(refs[:num], refs[num:2 * num], *refs[2 * num:]):
            phase()

    return pl.pallas_call(
        body, name="scatter_chips", out_shape=[jax.ShapeDtypeStruct((3,) + t.shape[1:], t.dtype) for t in parts],
        in_specs=[_ANY] * num, out_specs=[_ANY] * num,
        scratch_shapes=[pltpu.SemaphoreType.DMA((3 * num,)), pltpu.SemaphoreType.DMA((3 * num,))],
    )(*parts)


def _swap_sibling(arrs):
    num = len(arrs)

    def body(*refs):
        ins, outs = refs[:num], refs[num:2 * num]
        send_sems, recv_sems = refs[2 * num:]
        x, y, c = lax.axis_index("x"), lax.axis_index("y"), lax.axis_index("c")
        cps = [pltpu.make_async_remote_copy(src_ref=ins[k], dst_ref=outs[k], send_sem=send_sems.at[k],
                                            recv_sem=recv_sems.at[k], device_id=(x, y, 1 - c), device_id_type=MESH)
               for k in range(num)]
        for cp in cps:
            cp.start()
        for cp in cps:
            cp.wait()

    return pl.pallas_call(
        body, name="swap_sibling", out_shape=[jax.ShapeDtypeStruct(t.shape, t.dtype) for t in arrs],
        in_specs=[_ANY] * num, out_specs=[_ANY] * num,
        scratch_shapes=[pltpu.SemaphoreType.DMA((num,)), pltpu.SemaphoreType.DMA((num,))],
    )(*arrs)


def _gather_all(v):
    m_per, n = v.shape

    def body(x_ref, out_ref, send_sems, recv_sems, local_sem):
        x, y, c = lax.axis_index("x"), lax.axis_index("y"), lax.axis_index("c")
        me, sibling = (x, y, c), (x, y, 1 - c)
        chips = [(_flip(x, a), _flip(y, b)) for a, b in _FLIPS]

        def rows(px, py, pc):
            return out_ref.at[pl.ds(pl.multiple_of((4 * px + 2 * py + pc) * m_per, 8), m_per), :]

        def copy(k, block, to, src=None):
            return pltpu.make_async_remote_copy(
                src_ref=rows(*block) if src is None else src, dst_ref=rows(*block),
                send_sem=send_sems.at[k], recv_sem=recv_sems.at[k], device_id=to, device_id_type=MESH)

        mine = pltpu.make_async_copy(x_ref, rows(*me), local_sem)
        mine.start()
        first = [copy(0, me, sibling, src=x_ref)]
        first += [copy(1 + j, me, (*chip, c), src=x_ref) for j, chip in enumerate(chips)]
        for cp in first:
            cp.start()
        passed = [copy(4 + j, (*chip, c), sibling) for j, chip in enumerate(chips)]
        for j, chip in enumerate(chips):
            copy(1 + j, (*chip, c), me).wait_recv()
            passed[j].start()
        copy(0, sibling, me).wait_recv()
        for j, chip in enumerate(chips):
            copy(4 + j, (*chip, 1 - c), me).wait_recv()
        for cp in first + passed:
            cp.wait_send()
        mine.wait()

    return pl.pallas_call(
        body, name="gather_all", out_shape=jax.ShapeDtypeStruct((N_DEV * m_per, n), v.dtype),
        in_specs=[pl.BlockSpec(memory_space=pltpu.VMEM)], out_specs=pl.BlockSpec(memory_space=pltpu.VMEM),
        scratch_shapes=[pltpu.SemaphoreType.DMA((7,)), pltpu.SemaphoreType.DMA((7,)), pltpu.SemaphoreType.DMA],
    )(v)


def _sum_chips(parts, got, chip, name):
    cols = parts.shape[-1]
    rows = parts.size // (N_CHIPS * cols)
    tm = _pick(rows, (256, 128, 64, 32, 16))

    def body(chip_r, own_r, got_r, o_r):
        acc = own_r[0]
        for n in range(3):
            acc = acc + got_r[n].astype(f32)
        o_r[...] = acc

    return pl.pallas_call(
        body, name=name, out_shape=jax.ShapeDtypeStruct((rows, cols), f32),
        grid_spec=pltpu.PrefetchScalarGridSpec(
            num_scalar_prefetch=1, grid=(rows // tm,),
            in_specs=[pl.BlockSpec((1, tm, cols), lambda i, s: (s[0], i, 0)),
                      pl.BlockSpec((3, tm, cols), lambda i, s: (0, i, 0))],
            out_specs=pl.BlockSpec((tm, cols), lambda i, s: (i, 0))),
        compiler_params=_params(("parallel",)),
    )(chip, parts.reshape(N_CHIPS, rows, cols), got.reshape(3, rows, cols))


def _sum_devices(g, m_per):
    n = g.shape[1]

    def body(g_r, o_r):
        acc = g_r[0:m_per, :]
        for d in range(1, N_DEV):
            acc = acc + g_r[d * m_per:(d + 1) * m_per, :]
        o_r[...] = acc

    return pl.pallas_call(body, name="sum_devices", out_shape=jax.ShapeDtypeStruct((m_per, n), f32))(g)


def _adamw(w, gs, m, v, name):
    shape = w.shape
    cols = shape[-1]
    rows = w.size // cols
    tm = _pick(rows, (256, 128, 64, 32, 16, 8)) if rows * cols * 4 > (1 << 20) else rows

    def fn(i, wv, mv, vv, *gv):
        g = functools.reduce(jnp.add, gv)
        mn = ADAM_B1 * mv + (1.0 - ADAM_B1) * g
        vn = ADAM_B2 * vv + (1.0 - ADAM_B2) * jnp.square(g)
        m_hat = mn / (1.0 - ADAM_B1 ** ADAM_STEP)
        v_hat = vn / (1.0 - ADAM_B2 ** ADAM_STEP)
        delta = -ADAM_LR * (m_hat / (jnp.sqrt(v_hat) + ADAM_EPS) + ADAM_WD * wv)
        return g, delta, mn, vn

    outs = _rowwise(fn, name, rows, tm, [(t.reshape(rows, cols), (cols, 0)) for t in (w, m, v) + tuple(gs)], [(cols, f32)] * 4)
    return tuple(o.reshape(shape) for o in outs)


def _pack(pieces, rows, dtype):
    flat = jnp.concatenate([p.reshape(-1).astype(dtype) for p in pieces])
    return jnp.pad(flat, (0, rows * PACK_COLS - flat.size)).reshape(rows, PACK_COLS)


def _unpack(buf, shapes):
    lead = buf.shape[:-2]
    flat = buf.reshape(lead + (-1,))
    out, off = [], 0
    for s in shapes:
        n = 1
        for d in s:
            n *= d
        out.append(flat[..., off:off + n].reshape(lead + tuple(s)))
        off += n
    return out


def _join_cols(t):
    return jnp.moveaxis(t, 0, -2).reshape(t.shape[1:-1] + (N_CHIPS * t.shape[-1],))


def _join_rows(t):
    return t.reshape((N_CHIPS * t.shape[1],) + t.shape[2:])


def _split_cols(t, parts=N_CHIPS):
    r, cols = t.shape
    return jnp.moveaxis(t.reshape(r, parts, cols // parts), 1, 0)


def _split_rows(t):
    return t.reshape((N_CHIPS, t.shape[0] // N_CHIPS) + t.shape[1:])


def kernel(x, meta_tokens, gdn_norm_g, gdn_w_in, gdn_conv_w, gdn_a_log, gdn_dt_bias, gdn_onorm_g, gdn_w_out, kv_norm_g, w_kv, sb_norm_g, sb_w_q, sb_w_o, ffn_norm_g, ffn_w_gate_up, ffn_w_down, final_norm_g, loss_target, m_meta_tokens, m_gdn_norm_g, m_gdn_w_in, m_gdn_conv_w, m_gdn_a_log, m_gdn_dt_bias, m_gdn_onorm_g, m_gdn_w_out, m_kv_norm_g, m_w_kv, m_sb_norm_g, m_sb_w_q, m_sb_w_o, m_ffn_norm_g, m_ffn_w_gate_up, m_ffn_w_down, m_final_norm_g, v_meta_tokens, v_gdn_norm_g, v_gdn_w_in, v_gdn_conv_w, v_gdn_a_log, v_gdn_dt_bias, v_gdn_onorm_g, v_gdn_w_out, v_kv_norm_g, v_w_kv, v_sb_norm_g, v_sb_w_q, v_sb_w_o, v_ffn_norm_g, v_ffn_w_gate_up, v_ffn_w_down, v_final_norm_g):
    weights = dict(meta_tokens=meta_tokens, gdn_norm_g=gdn_norm_g, gdn_w_in=gdn_w_in, gdn_conv_w=gdn_conv_w,
                   gdn_a_log=gdn_a_log, gdn_dt_bias=gdn_dt_bias, gdn_onorm_g=gdn_onorm_g, gdn_w_out=gdn_w_out,
                   kv_norm_g=kv_norm_g, w_kv=w_kv, sb_norm_g=sb_norm_g, sb_w_q=sb_w_q, sb_w_o=sb_w_o,
                   ffn_norm_g=ffn_norm_g, ffn_w_gate_up=ffn_w_gate_up, ffn_w_down=ffn_w_down, final_norm_g=final_norm_g)
    m_in = dict(meta_tokens=m_meta_tokens, gdn_norm_g=m_gdn_norm_g, gdn_w_in=m_gdn_w_in, gdn_conv_w=m_gdn_conv_w,
                gdn_a_log=m_gdn_a_log, gdn_dt_bias=m_gdn_dt_bias, gdn_onorm_g=m_gdn_onorm_g, gdn_w_out=m_gdn_w_out,
                kv_norm_g=m_kv_norm_g, w_kv=m_w_kv, sb_norm_g=m_sb_norm_g, sb_w_q=m_sb_w_q, sb_w_o=m_sb_w_o,
                ffn_norm_g=m_ffn_norm_g, ffn_w_gate_up=m_ffn_w_gate_up, ffn_w_down=m_ffn_w_down, final_norm_g=m_final_norm_g)
    v_in = dict(meta_tokens=v_meta_tokens, gdn_norm_g=v_gdn_norm_g, gdn_w_in=v_gdn_w_in, gdn_conv_w=v_gdn_conv_w,
                gdn_a_log=v_gdn_a_log, gdn_dt_bias=v_gdn_dt_bias, gdn_onorm_g=v_gdn_onorm_g, gdn_w_out=v_gdn_w_out,
                kv_norm_g=v_kv_norm_g, w_kv=v_w_kv, sb_norm_g=v_sb_norm_g, sb_w_q=v_sb_w_q, sb_w_o=v_sb_w_o,
                ffn_norm_g=v_ffn_norm_g, ffn_w_gate_up=v_ffn_w_gate_up, ffn_w_down=v_ffn_w_down, final_norm_g=v_final_norm_g)
    names = list(weights)

    seq, d = x.shape[1], x.shape[2]
    lo_frames = FRONT + N_META
    used = lo_frames + seq
    rows = -(-used // SB_BLOCK) * SB_BLOCK
    tm = _pick(rows, (640, 512, 384, 256, 128))
    tp = _pick(rows, (320, 256, 128))
    n_ffn = ffn_w_gate_up.shape[0]
    sb_width = sb_w_q.shape[2]
    chip =2 * lax.axis_index("x") + lax.axis_index("y")

    big = [gdn_w_in[0], gdn_w_out[0], w_kv, sb_w_q[0], sb_w_o[0], ffn_w_gate_up, ffn_w_down]
    small = [meta_tokens, gdn_norm_g, gdn_conv_w[0]]
    n_early = 2
    big_bf16 = [t.astype(bf16) for t in big]
    w_in_s, w_out_s, small_g = _gather_chips(big_bf16[:n_early] + [_pack(small, 16, f32)])
    small_s = _unpack(small_g, [t.shape for t in small])
    w_in = _join_cols(w_in_s)
    pad_ab = jnp.zeros((d, LANES - GDN_HEADS), bf16)
    w_in_ext = jnp.concatenate([w_in[:, :4 * GDN_WIDTH], w_in[:, 4 * GDN_WIDTH:4 * GDN_WIDTH + GDN_HEADS], pad_ab,
                                w_in[:, 4 * GDN_WIDTH + GDN_HEADS:], pad_ab], axis=1)
    w_out = _join_rows(w_out_s)
    meta_full, gdn_g_full, conv_full = (_join_cols(t) for t in small_s)

    zeros = lambda n: jnp.zeros((n, d), f32)
    h0 = jnp.concatenate([zeros(FRONT), meta_full, x[0], zeros(rows - used)], axis=0)
    tgt = jnp.concatenate([zeros(lo_frames), loss_target[0], zeros(rows - used)], axis=0)
    pad8 = lambda t: jnp.pad(t, ((0, 0), (0, LANES - t.shape[1])))
    a_log8, dt_bias8 = pad8(gdn_a_log), pad8(gdn_dt_bias)
    r_i = jnp.arange(tp)
    ltri = ((r_i[:, None] >= r_i[None, :]) & (r_i[:, None] // CHUNK == r_i[None, :] // CHUNK)).astype(f32)
    ffn_g = [ffn_norm_g[l:l + 1] for l in range(n_ffn)]
    kv_g, fin_g = kv_norm_g.reshape(1, d), final_norm_g.reshape(1, d)

    n0 = _rms_fwd(h0, gdn_g_full, "gdn_norm")
    proj = _matmul(n0, w_in_ext, "nn", "gdn_proj")
    gq, gk, gv, gw, bw = _gdn_prep_fwd(proj, conv_full, a_log8, dt_bias8, ltri, FRONT, used, tp)
    g_o, g_states, (w_kv_s, w_q_s, w_o_s, w_gu_s, w_dn_s) = _gdn_fwd(gq, gk, gv, gw, bw, big_bf16[n_early:])
    w_kvf = _join_cols(w_kv_s)
    w_k, w_v = w_kvf[:, :sb_width], w_kvf[:, sb_width:]
    w_q = _join_rows(w_q_s)
    w_o = _join_rows(w_o_s)
    w_gu = [_join_cols(w_gu_s[:, l]) for l in range(n_ffn)]
    w_dn = [_join_rows(w_dn_s[:, l]) for l in range(n_ffn)]
    og = _gdn_gate_fwd(g_o, proj, gdn_onorm_g, tm)
    h1 = _matmul(og, w_out, "nn", "gdn_out", res=h0)

    def ffn_fwd(h, l):
        n = _rms_fwd(h, ffn_g[l], f"ffn{l}_norm")
        gate, up, act = _ffn_up(n, w_gu[l], f"ffn{l}_gate_up")
        return _matmul(act, w_dn[l], "nn", f"ffn{l}_down", res=h), (n, gate, up, act)

    h2, ffn0_saved = ffn_fwd(h1, 0)
    n_kv = _rms_fwd(h2, kv_g, "kv_norm")
    kv = _matmul(n_kv, w_kvf, "nn", "kv_proj", out_dtype=bf16)
    n_sb = _rms_fwd(h2, sb_norm_g, "sb_norm")
    sq = _matmul(n_sb, w_q, "nn", "q_proj", out_dtype=bf16)
    s_o = _sb_fwd(sq, kv, sb_width)
    h3 = _matmul(s_o, w_o, "nn", "sb_out", res=h2)
    h4, ffn1_saved = ffn_fwd(h3, 1)
    dh4, d_fin_g, loss_part = _loss_head(h4, fin_g, tgt, lo_frames, used, "loss_head")

    def ffn_bwd(dh, h, l, saved):
        n, gate, up, act = saved
        d_wdn = _matmul(act, dh, "tn", f"ffn{l}_d_w_down")
        d_gate, d_up = _ffn_dact(dh, w_dn[l], gate, up, f"ffn{l}_d_gate_up")
        d_wgu = jnp.concatenate([_split_cols(_matmul(n, d_gate, "tn", f"ffn{l}_d_w_gate"), N_CHIPS // 2),
                                 _split_cols(_matmul(n, d_up, "tn", f"ffn{l}_d_w_up"), N_CHIPS // 2)], axis=0)
        dn = _ffn_dnorm(d_gate, d_up, w_gu[l], f"ffn{l}_d_norm")
        dh_in, dg = _rms_bwd(h, ffn_g[l], dn, dh, f"ffn{l}_norm_bwd")
        return dh_in, d_wgu, d_wdn, dg

    dh3, d_wgu1, d_wdn1, d_ffn_g1 = ffn_bwd(dh4, h3, 1, ffn1_saved)
    d_wo = _matmul(s_o, dh3, "tn", "d_w_o")
    d_so = _matmul(dh3, w_o, "nt", "d_sb_o")
    d_sq, d_sk, d_sv = _sb_bwd(sq, kv, d_so, sb_width)
    d_wq = _matmul(n_sb, d_sq, "tn", "d_w_q")
    dh2, d_sb_g = _rms_bwd(h2, sb_norm_g, _matmul(d_sq, w_q, "nt", "d_sb_norm"), dh3, "sb_norm_bwd")
    d_wkv = jnp.concatenate([_matmul(n_kv, d_sk, "tn", "d_w_k"), _matmul(n_kv, d_sv, "tn", "d_w_v")], axis=1)
    dn_kv = _matmul(d_sv, w_v, "nt", "d_kv_norm_v", res=_matmul(d_sk, w_k, "nt", "d_kv_norm_k"))
    dh2, d_kv_g = _rms_bwd(h2, kv_g, dn_kv, dh2, "kv_norm_bwd")
    dh1, d_wgu0, d_wdn0, d_ffn_g0 = ffn_bwd(dh2, h1, 0, ffn0_saved)
    d_wout = _matmul(og, dh1, "tn", "d_w_out")
    d_og = _matmul(dh1, w_out, "nt", "d_gdn_gated")
    d_go, d_gate, d_onorm = _gdn_gate_bwd(g_o, proj, gdn_onorm_g, d_og, tm)
    by_chip = [None, _split_rows(d_wout), _split_cols(d_wkv), _split_rows(d_wq), _split_rows(d_wo),
               jnp.stack([d_wgu0, d_wgu1], axis=1),
               jnp.stack([_split_rows(d_wdn0), _split_rows(d_wdn1)], axis=1)]
    (d_gq, d_gk, d_gv, d_gw, d_bw), got_early = _gdn_bwd(gq, gk, gv, gw, bw, g_states, d_go,
                                                         [t.astype(bf16) for t in by_chip[1:]])
    dconv, d_a_in, d_b_in, d_a_log8, d_dt_bias8 = _gdn_prep_bwd_act(
        proj, conv_full, a_log8, dt_bias8, ltri, d_gq, d_gk, d_gv, d_gw, d_bw, FRONT, used, tp)
    dproj, d_conv = _gdn_prep_bwd_conv(proj, conv_full, dconv, d_gate, d_a_in, d_b_in, tp)
    d_win_ext = _matmul(n0, dproj, "tn", "d_w_in")
    dh0, d_gdn_g = _rms_bwd(h0, gdn_g_full, _matmul(dproj, w_in_ext, "nt", "d_gdn_norm"), dh1, "gdn_norm_bwd")
    grad_x = dh0[lo_frames:used][None]
    d_win = jnp.concatenate([d_win_ext[:, :4 * GDN_WIDTH], d_win_ext[:, 4 * GDN_WIDTH:4 * GDN_WIDTH + GDN_HEADS],
                             d_win_ext[:, 4 * GDN_WIDTH + LANES:4 * GDN_WIDTH + LANES + GDN_HEADS]], axis=1)

    by_chip[0] = _split_cols(d_win)
    got = list(_scatter_chips([by_chip[0].astype(bf16)])) + list(got_early)
    chip_arr = jnp.reshape(chip, (1,)).astype(i32)
    over_chips = [_sum_chips(t, g, chip_arr, f"sum_chips_{k}") for k, (t, g) in enumerate(zip(by_chip, got))]
    over_sibling = _swap_sibling(over_chips)
    big_names = ["gdn_w_in", "gdn_w_out", "w_kv", "sb_w_q", "sb_w_o", "ffn_w_gate_up", "ffn_w_down"]
    g_big = dict(zip(big_names, zip(over_chips, over_sibling)))

    small_parts = [dh0[FRONT:lo_frames], d_gdn_g, d_conv, d_a_log8, d_dt_bias8, d_onorm, d_kv_g, d_sb_g,
                   d_ffn_g0, d_ffn_g1, d_fin_g, loss_part]
    s_rows = -(-sum(t.size for t in small_parts) // (8 * PACK_COLS)) * 8
    s_sum = _sum_devices(_gather_all(_pack(small_parts, s_rows, f32)), s_rows)
    (g_meta, g_gdn_g, g_conv, g_a_log8, g_dt8, g_onorm, g_kv_g, g_sb_g, g_ffn_g0, g_ffn_g1, g_fin_g,
     loss_v) = _unpack(s_sum, [t.shape for t in small_parts])
    col_shard = lambda t, w: lax.dynamic_slice_in_dim(t, chip * w, w, axis=t.ndim - 1)

    g_small = dict(
        meta_tokens=col_shard(g_meta, meta_tokens.shape[1]), gdn_norm_g=col_shard(g_gdn_g, gdn_norm_g.shape[1]),
        gdn_conv_w=col_shard(g_conv, gdn_conv_w.shape[2])[None],
        gdn_a_log=g_a_log8[:, :GDN_HEADS], gdn_dt_bias=g_dt8[:, :GDN_HEADS], gdn_onorm_g=g_onorm,
        kv_norm_g=g_kv_g.reshape(-1), sb_norm_g=g_sb_g, ffn_norm_g=jnp.concatenate([g_ffn_g0, g_ffn_g1], axis=0),
        final_norm_g=g_fin_g.reshape(-1))

    grads, delta, new_m, new_v = {}, {}, {}, {}
    for n in names:
        gs = g_big[n] if n in g_big else (g_small[n],)
        grads[n], delta[n], new_m[n], new_v[n] = _adamw(weights[n], gs, m_in[n], v_in[n], f"adamw_{n}")
    loss = loss_v[0, 0]
    return (loss, grad_x, *[grads[n] for n in names], *[delta[n] for n in names],
            *[new_m[n] for n in names], *[new_v[n] for n in names])
```

```python
import functools

import jax
import jax.numpy as jnp
from jax import lax
from jax.experimental import pallas as pl
from jax.experimental.pallas import tpu as pltpu

f32 = jnp.float32
bf16 = jnp.bfloat16
i32 = jnp.int32

EPS = 1e-6
N_META = 16
CHUNK = 64
FRONT = (-N_META) % CHUNK
GDN_HEADS = 8
GDN_DIM = 128
GDN_WIDTH = GDN_HEADS * GDN_DIM
CONV_WIDTH = 4
SB_DIM = 64
SB_BLOCK = 128
SB_FWD_HEADS = 4
SB_FIRST = 3
SB_UNDERFLOW = 104.0
LANES = 128
PACK_COLS = 1024
N_CHIPS = 4
N_DEV = 8
ADAM_LR, ADAM_B1, ADAM_B2, ADAM_EPS, ADAM_WD, ADAM_STEP = 0.001, 0.9, 0.999, 1e-08, 0.01, 10
VMEM_LIMIT = 56 * 1024 * 1024
MESH = pl.DeviceIdType.MESH


def _pick(n, prefs):
    for p in prefs:
        if n % p == 0:
            return p
    return n


def _params(sem):
    return pltpu.CompilerParams(dimension_semantics=sem, vmem_limit_bytes=VMEM_LIMIT)


_DIMS = {"nn": ((1,), (0,)), "nt": ((1,), (1,)), "tn": ((0,), (0,))}


def _bdot(a, b, mode):
    return lax.dot_general(a.astype(bf16), b.astype(bf16), (_DIMS[mode], ((), ())), preferred_element_type=f32)


def _matmul(a, b, mode, name, res=None, out_dtype=f32):
    if mode == "nn":
        (m, k), n = a.shape, b.shape[1]
    elif mode == "nt":
        (m, k), n = a.shape, b.shape[0]
    else:
        (k, m), n = a.shape, b.shape[1]
    tm = _pick(m, (640, 1408, 1024, 512, 384, 256, 128))
    tn = _pick(n, (1408, 2176, 1024, 512, 384, 256, 128))
    tk = _pick(k, (1408, 2176, 1024, 640, 512, 384, 256, 128))
    nk = k // tk
    a_spec = pl.BlockSpec((tk, tm), lambda i, j, kk: (kk, i)) if mode == "tn" else pl.BlockSpec((tm, tk), lambda i, j, kk: (i, kk))
    b_spec = pl.BlockSpec((tn, tk), lambda i, j, kk: (j, kk)) if mode == "nt" else pl.BlockSpec((tk, tn), lambda i, j, kk: (kk, j))
    o_spec = pl.BlockSpec((tm, tn), lambda i, j, kk: (i, j))
    has_res = res is not None

    def body(*refs):
        if has_res:
            a_ref, b_ref, r_ref, o_ref, acc = refs
        else:
            a_ref, b_ref, o_ref, acc = refs
        kk = pl.program_id(2)

        @pl.when(kk == 0)
        def _():
            acc[...] = jnp.zeros_like(acc)

        acc[...] += _bdot(a_ref[...], b_ref[...], mode)

        @pl.when(kk == nk - 1)
        def _():
            y = acc[...]
            if has_res:
                y = y + r_ref[...]
            o_ref[...] = y.astype(o_ref.dtype)

    ins = [a, b] + ([res] if has_res else [])
    specs = [a_spec, b_spec] + ([o_spec] if has_res else [])
    return pl.pallas_call(
        body, name=name, out_shape=jax.ShapeDtypeStruct((m, n), out_dtype), grid=(m // tm, n // tn, nk),
        in_specs=specs, out_specs=o_spec, scratch_shapes=[pltpu.VMEM((tm, tn), f32)],
        compiler_params=_params(("parallel", "parallel", "arbitrary")),
    )(*ins)


def _rowwise(fn, name, rows, tm, ins, outs, reds=()):
    n_in, n_out, n_red = len(ins), len(outs), len(reds)
    in_specs = []
    for arr, spec in ins:
        if spec is None:
            in_specs.append(pl.BlockSpec(arr.shape, lambda i, nd=arr.ndim: (0,) * nd))
        else:
            w, cb = spec
            in_specs.append(pl.BlockSpec((tm, w), lambda i, cb=cb: (i, cb)))
    out_specs = [pl.BlockSpec((tm, w), lambda i: (i, 0)) for w, _ in outs]
    out_specs += [pl.BlockSpec(s, lambda i, nd=len(s): (0,) * nd) for s in reds]
    out_shape = [jax.ShapeDtypeStruct((rows, w), dt) for w, dt in outs]
    out_shape += [jax.ShapeDtypeStruct(s, f32) for s in reds]

    def body(*refs):
        i = pl.program_id(0)
        vals = fn(i, *[r[...] for r in refs[:n_in]])
        for r, v in zip(refs[n_in:n_in + n_out], vals[:n_out]):
            r[...] = v.astype(r.dtype)
        red_refs = refs[n_in + n_out:]

        @pl.when(i == 0)
        def _():
            for r in red_refs:
                r[...] = jnp.zeros_like(r)

        for r, v in zip(red_refs, vals[n_out:]):
            r[...] += v

    res = pl.pallas_call(
        body, name=name, out_shape=out_shape, grid=(rows // tm,), in_specs=in_specs, out_specs=out_specs,
        compiler_params=_params(("arbitrary",)),
    )(*[a for a, _ in ins])
    return res


def _rms(x, g):
    return x * lax.rsqrt(jnp.mean(x * x, axis=-1, keepdims=True) + EPS) * g


def _row_mask(i, tm, lo, hi, shape):
    r = i * tm + lax.broadcasted_iota(i32, shape, 0)
    return (r >= lo) & (r < hi)


def _rms_fwd(x, g, name):
    rows, d = x.shape
    tm = _pick(rows, (640, 512, 384, 256, 128))
    return _rowwise(lambda i, xv, gv: (_rms(xv, gv),), name, rows, tm, [(x, (d, 0)), (g, None)], [(d, bf16)])[0]


def _norm_bwd(parts, w, x, g, res, name):
    rows, k = parts[0].shape
    d = w.shape[0]
    num = len(parts)
    tm = _pick(rows, (640, 512, 384, 256, 128))
    tk = _pick(k, (1408, 2176, 1024, 512, 384, 256, 128))
    nk = k // tk

    def body(*refs):
        a_refs, w_refs = refs[:num], refs[num:2 * num]
        x_r, g_r, r_r, o_r, dg_r, acc = refs[2 * num:]
        i, kk = pl.program_id(0), pl.program_id(1)

        @pl.when(kk == 0)
        def _():
            acc[...] = jnp.zeros_like(acc)

        acc[...] += functools.reduce(jnp.add, [_bdot(a[...], b[...], "nt") for a, b in zip(a_refs, w_refs)])

        @pl.when((i == 0) & (kk == 0))
        def _():
            dg_r[...] = jnp.zeros_like(dg_r)

        @pl.when(kk == nk - 1)
        def _():
            _, vjp = jax.vjp(_rms, x_r[...], g_r[...])
            dx, dg = vjp(acc[...])
            o_r[...] = r_r[...] + dx
            dg_r[...] += dg

    row = pl.BlockSpec((tm, d), lambda i, kk: (i, 0))
    one = pl.BlockSpec((1, d), lambda i, kk: (0, 0))
    return pl.pallas_call(
        body, name=name, out_shape=[jax.ShapeDtypeStruct((rows, d), f32), jax.ShapeDtypeStruct((1, d), f32)],
        grid=(rows // tm, nk),
        in_specs=[pl.BlockSpec((tm, tk), lambda i, kk: (i, kk))] * num
        + [pl.BlockSpec((d, tk), lambda i, kk, p=p: (0, p * nk + kk)) for p in range(num)] + [row, one, row],
        out_specs=[row, one], scratch_shapes=[pltpu.VMEM((tm, d), f32)],
        compiler_params=_params(("arbitrary", "arbitrary")),
    )(*parts, *([w] * num), x, g, res)


def _swiglu(gate, up):
    return jax.nn.silu(gate) * up


def _ffn_up(n, w_gu, name):
    rows, d = n.shape
    f = w_gu.shape[1] // 2
    tm = _pick(rows, (640, 512, 384, 256, 128))
    tn = _pick(f, (1408, 1024, 512, 384, 256, 128))
    nj = f // tn

    def body(n_r, wg_r, wu_r, g_r, u_r, a_r):
        g = jnp.dot(n_r[...], wg_r[...], preferred_element_type=f32)
        u = jnp.dot(n_r[...], wu_r[...], preferred_element_type=f32)
        g_r[...] = g.astype(g_r.dtype)
        u_r[...] = u.astype(u_r.dtype)
        a_r[...] = _swiglu(g, u).astype(a_r.dtype)

    o_spec = pl.BlockSpec((tm, tn), lambda j, i: (i, j))
    return pl.pallas_call(
        body, name=name, grid=(nj, rows // tm), out_shape=[jax.ShapeDtypeStruct((rows, f), bf16)] * 3,
        in_specs=[pl.BlockSpec((tm, d), lambda j, i: (i, 0)), pl.BlockSpec((d, tn), lambda j, i: (0, j)),
                  pl.BlockSpec((d, tn), lambda j, i: (0, nj + j))],
        out_specs=[o_spec] * 3, compiler_params=_params(("parallel", "parallel")),
    )(n, w_gu, w_gu)


def _ffn_dact(dh, w_dn, gate, up, name):
    rows, d = dh.shape
    f = w_dn.shape[0]
    tm = _pick(rows, (640, 512, 384, 256, 128))
    tn = _pick(f, (1408, 1024, 512, 384, 256, 128))

    def body(dh_r, w_r, g_r, u_r, dg_r, du_r):
        dact = _bdot(dh_r[...], w_r[...], "nt")
        _, vjp = jax.vjp(_swiglu, g_r[...].astype(f32), u_r[...].astype(f32))
        dg, du = vjp(dact)
        dg_r[...] = dg.astype(dg_r.dtype)
        du_r[...] = du.astype(du_r.dtype)

    t_spec = pl.BlockSpec((tm, tn), lambda j, i: (i, j))
    return pl.pallas_call(
        body, name=name, grid=(f // tn, rows // tm), out_shape=[jax.ShapeDtypeStruct((rows, f), bf16)] * 2,
        in_specs=[pl.BlockSpec((tm, d), lambda j, i: (i, 0)), pl.BlockSpec((tn, d), lambda j, i: (j, 0)), t_spec, t_spec],
        out_specs=[t_spec] * 2, compiler_params=_params(("parallel", "parallel")),
    )(dh, w_dn, gate, up)


def _loss_head(h, g, tgt, lo, hi, name):
    rows, d = h.shape
    tm = _pick(rows, (640, 512, 384, 256, 128))

    def fn(i, hv, gv, tv):
        mask = _row_mask(i, tm, lo, hi, (tm, 1))

        def f(hh, gg):
            err = _rms(hh, gg) - tv
            per_row = jnp.where(mask, jnp.mean(err * err, axis=-1, keepdims=True), 0.0)
            return 0.5 * jnp.sum(per_row, axis=0, keepdims=True)

        loss, vjp = jax.vjp(f, hv, gv)
        dh, dg = vjp(jnp.ones_like(loss))
        return dh, dg, jnp.broadcast_to(loss, (1, LANES))

    return _rowwise(fn, name, rows, tm, [(h, (d, 0)), (g, None), (tgt, (d, 0))], [(d, f32)], [(1, d), (1, LANES)])


def _heads_l2(x):
    t = x.shape[0]
    x3 = x.reshape(t, GDN_HEADS, GDN_DIM)
    return (x3 * lax.rsqrt(jnp.sum(x3 * x3, axis=-1, keepdims=True) + EPS)).reshape(t, GDN_WIDTH)


def _gdn_act(conv, a_in, b_in, a_log, dt_bias, mask):
    s = jax.nn.silu(conv)
    q = _heads_l2(s[:, :GDN_WIDTH])
    k = _heads_l2(s[:, GDN_WIDTH:2 * GDN_WIDTH])
    v = s[:, 2 * GDN_WIDTH:]
    g = jnp.where(mask, -jnp.exp(a_log) * jax.nn.softplus(a_in + dt_bias), 0.0)
    beta = jnp.where(mask, jax.nn.sigmoid(b_in), 0.0)
    return q, k, v, g, beta


def _widen(x8):
    return [jnp.broadcast_to(x8[:, h:h + 1], (x8.shape[0], GDN_DIM)) for h in range(GDN_HEADS)]


def _narrow(per_head):
    t = per_head[0].shape[0]
    lane = lax.broadcasted_iota(i32, (t, LANES), 1)
    out = jnp.zeros((t, LANES), f32)
    for h, x in enumerate(per_head):
        out = out + jnp.where(lane == h, jnp.sum(x, axis=1, keepdims=True), 0.0)
    return out


def _conv_taps(cur, prev8, w):
    tm = cur.shape[0]
    cat = jnp.concatenate([prev8, cur], axis=0)
    y = cur * w[CONV_WIDTH - 1:CONV_WIDTH, :]
    for j in range(1, CONV_WIDTH):
        y = y + pltpu.roll(cat, j, axis=0)[8:8 + tm, :] * w[CONV_WIDTH - 1 - j:CONV_WIDTH - j, :]
    return y


def _gdn_prep_specs(proj, tm):
    c3 = 3 * GDN_WIDTH
    ab = 4 * GDN_WIDTH // LANES
    t8 = tm // 8
    return [
        pl.BlockSpec((tm, c3), lambda i: (i, 0)),
        pl.BlockSpec((8, c3), lambda i: (jnp.maximum(i * t8 - 1, 0), 0)),
        pl.BlockSpec((tm, LANES), lambda i: (i, ab)),
        pl.BlockSpec((tm, LANES), lambda i: (i, ab + 1)),
    ]


def _full(arr):
    return pl.BlockSpec(arr.shape, lambda i, nd=arr.ndim: (0,) * nd)


def _gdn_prep_fwd(proj, conv_w, a_log, dt_bias, ltri, lo, hi, tm):
    rows = proj.shape[0]

    def body(cur, prev8, a_in, b_in, w, al, dtb, lt, q_o, k_o, v_o, g_o, b_o):
        i = pl.program_id(0)
        mask = _row_mask(i, tm, lo, hi, (tm, LANES)) & (lax.broadcasted_iota(i32, (tm, LANES), 1) < GDN_HEADS)
        conv = _conv_taps(cur[...], prev8[...], w[...])
        q, k, v, g, beta = _gdn_act(conv, a_in[...], b_in[...], al[...], dtb[...], mask)
        q_o[...] = q
        k_o[...] = k
        v_o[...] = v
        gcum = jnp.dot(lt[...], g, preferred_element_type=f32, precision=lax.Precision.HIGHEST)
        g_o[...] = gcum
        b_o[...] = beta

    wide = jax.ShapeDtypeStruct((rows, GDN_WIDTH), f32)
    narrow = jax.ShapeDtypeStruct((rows, LANES), f32)
    o_spec = pl.BlockSpec((tm, GDN_WIDTH), lambda i: (i, 0))
    n_spec = pl.BlockSpec((tm, LANES), lambda i: (i, 0))
    return pl.pallas_call(
        body, name="gdn_prep_fwd", out_shape=[wide] * 3 + [narrow] * 2, grid=(rows // tm,),
        in_specs=_gdn_prep_specs(proj, tm) + [_full(conv_w), _full(a_log), _full(dt_bias), _full(ltri)],
        out_specs=[o_spec] * 3 + [n_spec] * 2, compiler_params=_params(("parallel",)),
    )(proj, proj, proj, proj, conv_w, a_log, dt_bias, ltri)


def _gdn_prep_bwd_act(proj, conv_w, a_log, dt_bias, ltri, dq, dk, dv, dgw, dbw, lo, hi, tm):
    rows = proj.shape[0]
    c3 = 3 * GDN_WIDTH

    def body(cur, prev8, a_in, b_in, w, al, dtb, lt, dq_r, dk_r, dv_r, dg_r, db_r, dconv_o, da_o, dbin_o, dal_o, ddt_o):
        i = pl.program_id(0)
        mask = _row_mask(i, tm, lo, hi, (tm, LANES)) & (lax.broadcasted_iota(i32, (tm, LANES), 1) < GDN_HEADS)
        conv = _conv_taps(cur[...], prev8[...], w[...])
        dg = lax.dot_general(lt[...], dg_r[...], (((0,), (0,)), ((), ())), preferred_element_type=f32,
                             precision=lax.Precision.HIGHEST)
        dbeta = db_r[...]
        _, vjp = jax.vjp(lambda c, a, b, x, y: _gdn_act(c, a, b, x, y, mask), conv, a_in[...], b_in[...], al[...], dtb[...])
        dconv, da, dbin, dal, ddt = vjp((dq_r[...], dk_r[...], dv_r[...], dg, dbeta))
        dconv_o[...] = dconv
        da_o[...] = da
        dbin_o[...] = dbin

        @pl.when(i == 0)
        def _():
            dal_o[...] = jnp.zeros_like(dal_o)
            ddt_o[...] = jnp.zeros_like(ddt_o)

        dal_o[...] += dal
        ddt_o[...] += ddt

    w_spec = pl.BlockSpec((tm, GDN_WIDTH), lambda i: (i, 0))
    n_spec = pl.BlockSpec((tm, LANES), lambda i: (i, 0))
    s_spec = pl.BlockSpec((1, LANES), lambda i: (0, 0))
    return pl.pallas_call(
        body, name="gdn_prep_bwd_act",
        out_shape=[jax.ShapeDtypeStruct((rows, c3), f32), jax.ShapeDtypeStruct((rows, LANES), f32),
                   jax.ShapeDtypeStruct((rows, LANES), f32), jax.ShapeDtypeStruct((1, LANES), f32),
                   jax.ShapeDtypeStruct((1, LANES), f32)],
        grid=(rows // tm,),
        in_specs=_gdn_prep_specs(proj, tm) + [_full(conv_w), _full(a_log), _full(dt_bias), _full(ltri)] + [w_spec] * 3 + [n_spec] * 2,
        out_specs=[pl.BlockSpec((tm, c3), lambda i: (i, 0)), n_spec, n_spec, s_spec, s_spec],
        compiler_params=_params(("arbitrary",)),
    )(proj, proj, proj, proj, conv_w, a_log, dt_bias, ltri, dq, dk, dv, dgw, dbw)


def _gdn_prep_bwd_conv(proj, conv_w, dconv, dgate, da, dbin, tm):
    rows, width = proj.shape
    c3 = 3 * GDN_WIDTH
    t8 = tm // 8
    nt = rows // tm

    def body(cur, prev8, w, dc, dnext8, dgt, da_r, db_r, dp_o, dw_o):
        i = pl.program_id(0)
        d = dc[...]
        nxt = jnp.where(i == nt - 1, 0.0, dnext8[...])
        cat = jnp.concatenate([d, nxt], axis=0)
        wv = w[...]
        dx = d * wv[CONV_WIDTH - 1:CONV_WIDTH, :]
        for j in range(1, CONV_WIDTH):
            dx = dx + pltpu.roll(cat, tm + 8 - j, axis=0)[:tm, :] * wv[CONV_WIDTH - 1 - j:CONV_WIDTH - j, :]
        dp_o[:, :c3] = dx.astype(bf16)
        dp_o[:, c3:4 * GDN_WIDTH] = dgt[...].astype(bf16)
        dp_o[:, 4 * GDN_WIDTH:4 * GDN_WIDTH + LANES] = da_r[...].astype(bf16)
        dp_o[:, 4 * GDN_WIDTH + LANES:] = db_r[...].astype(bf16)

        xcat = jnp.concatenate([prev8[...], cur[...]], axis=0)
        parts = [jnp.sum(d * cur[...], axis=0, keepdims=True)]
        for j in range(1, CONV_WIDTH):
            parts.append(jnp.sum(d * pltpu.roll(xcat, j, axis=0)[8:8 + tm, :], axis=0, keepdims=True))
        dwt = jnp.concatenate(parts[::-1], axis=0)

        @pl.when(i == 0)
        def _():
            dw_o[...] = jnp.zeros_like(dw_o)

        dw_o[...] += dwt

    n_spec = pl.BlockSpec((tm, LANES), lambda i: (i, 0))
    return pl.pallas_call(
        body, name="gdn_prep_bwd_conv",
        out_shape=[jax.ShapeDtypeStruct((rows, width), bf16), jax.ShapeDtypeStruct((CONV_WIDTH, c3), f32)],
        grid=(nt,),
        in_specs=[pl.BlockSpec((tm, c3), lambda i: (i, 0)),
                  pl.BlockSpec((8, c3), lambda i: (jnp.maximum(i * t8 - 1, 0), 0)),
                  _full(conv_w),
                  pl.BlockSpec((tm, c3), lambda i: (i, 0)),
                  pl.BlockSpec((8, c3), lambda i: (jnp.minimum((i + 1) * t8, rows // 8 - 1), 0)),
                  pl.BlockSpec((tm, GDN_WIDTH), lambda i: (i, 0)), n_spec, n_spec],
        out_specs=[pl.BlockSpec((tm, width), lambda i: (i, 0)), pl.BlockSpec((CONV_WIDTH, c3), lambda i: (0, 0))],
        compiler_params=_params(("arbitrary",)),
    )(proj, proj, conv_w, dconv, dconv, dgate, da, dbin)


def _split(a):
    hi = a.astype(bf16)
    return hi, (a - hi.astype(f32)).astype(bf16)


def _make_mm(dot):
    @jax.custom_vjp
    def nn(a, b):
        return dot(a, b, "nn")

    nn.defvjp(lambda a, b: (dot(a, b, "nn"), (a, b)),
              lambda r, ct: (dot(ct, r[1], "nt"), dot(r[0], ct, "tn")))

    @jax.custom_vjp
    def nt(a, b):
        return dot(a, b, "nt")

    nt.defvjp(lambda a, b: (dot(a, b, "nt"), (a, b)),
              lambda r, ct: (dot(ct, r[1], "nn"), dot(ct, r[0], "tn")))

    @jax.custom_vjp
    def tn(a, b):
        return dot(a, b, "tn")

    tn.defvjp(lambda a, b: (dot(a, b, "tn"), (a, b)),
              lambda r, ct: (dot(r[1], ct, "nt"), dot(r[0], ct, "nn")))
    return nn, nt, tn


_mm, _mm_nt, _mm_tn = _make_mm(_bdot)


def _each(f, *lists):
    return [f(*xs) for xs in zip(*lists)]


def _gdn_chunk(q, k, v, gcb, bcb, s_in):
    c = q[0].shape[0]
    ri = lax.broadcasted_iota(i32, (c, c), 0)
    ci = lax.broadcasted_iota(i32, (c, c), 1)
    incl, strict = ri >= ci, ri > ci
    rowi = lax.broadcasted_iota(i32, gcb[0].shape, 0)
    qs = _each(lambda t: t * (GDN_DIM ** -0.5), q)
    decay = _each(lambda g: jnp.where(incl, jnp.exp(jnp.where(incl, g[:, :c] - g[:, :c].T, 0.0)), 0.0), gcb)
    kk = _each(lambda t: _mm_nt(t, t), k)
    a1 = _each(lambda b, d, t: jnp.where(strict, b[:, :c] * d * t, 0.0), bcb, decay, kk)
    eg = _each(jnp.exp, gcb)
    x = _each(lambda b, vv, e, t: jnp.concatenate([b * vv, (b * e) * t], axis=1), bcb, v, eg, k)
    pows = [a1]
    for _ in range(5):
        pows.append(_each(lambda p: _mm(p, p), pows[-1]))
    for ps in pows[:0:-1]:
        x = _each(lambda p, t: t + _mm(p, t), ps, x)
    x = _each(lambda p, t: t - _mm(p, t), a1, x)
    attn = _each(lambda a, b, d: _mm_nt(a, b) * d, qs, k, decay)
    glast = _each(lambda g: jnp.sum(jnp.where(rowi == c - 1, g, 0.0), axis=0, keepdims=True), gcb)
    u = _each(lambda t, s: t[:, :GDN_DIM] - _mm(t[:, GDN_DIM:], s), x, s_in)
    o = _each(lambda a, e, s, w, uu: _mm(a * e, s) + _mm(w, uu), qs, eg, s_in, attn, u)
    s_out = _each(lambda s, gl, t, g, uu: s * jnp.exp(gl) + _mm_tn(t * jnp.exp(gl - g), uu), s_in, glast, k, gcb, u)
    return o, s_out


def _gdn_heads(ref):
    return [ref[:, h * GDN_DIM:(h + 1) * GDN_DIM] for h in range(GDN_HEADS)]


def _gdn_fwd(q, k, v, gw, bw, shards):
    rows = q.shape[0]
    nc = rows // CHUNK
    num = len(shards)
    blk = pl.BlockSpec((CHUNK, GDN_WIDTH), lambda c: (c, 0))

    def body(*refs):
        q_r, k_r, v_r, g_r, b_r = refs[:5]
        ins = refs[5:5 + num]
        o_r, st_r = refs[5 + num:7 + num]
        outs = refs[7 + num:7 + 2 * num]
        s_sc, send_sems, recv_sems, local_sems = refs[7 + 2 * num:]
        c = pl.program_id(0)
        start, forward, finish = _gather_plan(ins, outs, send_sems, recv_sems, local_sems)

        @pl.when(c == 0)
        def _():
            s_sc[...] = jnp.zeros_like(s_sc)
            start()

        s_in = [s_sc[h] for h in range(GDN_HEADS)]
        st_r[0] = s_sc[...]
        o, s_out = _gdn_chunk(_gdn_heads(q_r), _gdn_heads(k_r), _gdn_heads(v_r), _widen(g_r[...]), _widen(b_r[...]), s_in)
        o_r[...] = jnp.concatenate(o, axis=1)
        for h in range(GDN_HEADS):
            s_sc[h] = s_out[h]
        pl.when(c == nc // 2)(forward)
        pl.when(c == nc - 1)(finish)

    res = pl.pallas_call(
        body, name="gdn_fwd",
        out_shape=[jax.ShapeDtypeStruct((rows, GDN_WIDTH), f32), jax.ShapeDtypeStruct((nc, GDN_HEADS, GDN_DIM, GDN_DIM), f32)]
        + [jax.ShapeDtypeStruct((N_CHIPS,) + t.shape, t.dtype) for t in shards],
        grid=(nc,), in_specs=[blk] * 3 + [pl.BlockSpec((CHUNK, LANES), lambda c: (c, 0))] * 2 + [_ANY] * num,
        out_specs=[blk, pl.BlockSpec((1, GDN_HEADS, GDN_DIM, GDN_DIM), lambda c: (c, 0, 0, 0))] + [_ANY] * num,
        scratch_shapes=[pltpu.VMEM((GDN_HEADS, GDN_DIM, GDN_DIM), f32), pltpu.SemaphoreType.DMA((6 * num,)),
                        pltpu.SemaphoreType.DMA((6 * num,)), pltpu.SemaphoreType.DMA((num,))],
        compiler_params=_params(("arbitrary",)),
    )(q, k, v, gw, bw, *shards)
    return res[0], res[1], res[2:]


def _gdn_bwd(q, k, v, gw, bw, states, do, parts):
    rows = q.shape[0]
    nc = rows // CHUNK
    num = len(parts)
    blk = pl.BlockSpec((CHUNK, GDN_WIDTH), lambda c: (nc - 1 - c, 0))

    def body(*refs):
        q_r, k_r, v_r, g_r, b_r, st_r, do_r = refs[:7]
        ins = refs[7:7 + num]
        dq_r, dk_r, dv_r, dg_r, db_r = refs[7 + num:12 + num]
        outs = refs[12 + num:12 + 2 * num]
        ds_sc, send_sems, recv_sems = refs[12 + 2 * num:]
        c = pl.program_id(0)
        start, finish = _scatter_plan(ins, outs, send_sems, recv_sems)

        @pl.when(c == 0)
        def _():
            ds_sc[...] = jnp.zeros_like(ds_sc)
            start()

        s_in = [st_r[0, h] for h in range(GDN_HEADS)]
        _, vjp = jax.vjp(_gdn_chunk, _gdn_heads(q_r), _gdn_heads(k_r), _gdn_heads(v_r), _widen(g_r[...]), _widen(b_r[...]), s_in)
        dq, dk, dv, dg, db, ds_in = vjp((_gdn_heads(do_r), [ds_sc[h] for h in range(GDN_HEADS)]))
        dq_r[...] = jnp.concatenate(dq, axis=1)
        dk_r[...] = jnp.concatenate(dk, axis=1)
        dv_r[...] = jnp.concatenate(dv, axis=1)
        dg_r[...] = _narrow(dg)
        db_r[...] = _narrow(db)
        for h in range(GDN_HEADS):
            ds_sc[h] = ds_in[h]
        pl.when(c == nc - 1)(finish)

    wide = jax.ShapeDtypeStruct((rows, GDN_WIDTH), f32)
    narrow = jax.ShapeDtypeStruct((rows, LANES), f32)
    nblk = pl.BlockSpec((CHUNK, LANES), lambda c: (nc - 1 - c, 0))
    res = pl.pallas_call(
        body, name="gdn_bwd",
        out_shape=[wide] * 3 + [narrow] * 2 + [jax.ShapeDtypeStruct((3,) + t.shape[1:], t.dtype) for t in parts],
        grid=(nc,),
        in_specs=[blk] * 3 + [nblk] * 2
        + [pl.BlockSpec((1, GDN_HEADS, GDN_DIM, GDN_DIM), lambda c: (nc - 1 - c, 0, 0, 0)), blk] + [_ANY] * num,
        out_specs=[blk] * 3 + [nblk] * 2 + [_ANY] * num,
        scratch_shapes=[pltpu.VMEM((GDN_HEADS, GDN_DIM, GDN_DIM), f32), pltpu.SemaphoreType.DMA((3 * num,)),
                        pltpu.SemaphoreType.DMA((3 * num,))],
        compiler_params=_params(("arbitrary",)),
    )(q, k, v, gw, bw, states, do, *parts)
    return res[:5], res[5:]


def _gdn_gate(o, gate, og):
    t = o.shape[0]
    o3 = o.reshape(t, GDN_HEADS, GDN_DIM)
    n = o3 * lax.rsqrt(jnp.mean(o3 * o3, axis=-1, keepdims=True) + EPS) * og.reshape(1, 1, GDN_DIM)
    return n.reshape(t, GDN_WIDTH) * jax.nn.silu(gate)


def _gdn_gate_fwd(o, proj, og, tm):
    rows = o.shape[0]
    return _rowwise(lambda i, ov, gv, w: (_gdn_gate(ov, gv, w),), "gdn_gate_fwd", rows, tm,
                    [(o, (GDN_WIDTH, 0)), (proj, (GDN_WIDTH, 3)), (og, None)], [(GDN_WIDTH, bf16)])[0]


def _gdn_gate_bwd(o, proj, og, dy, tm):
    rows = o.shape[0]

    def fn(i, ov, gv, w, d):
        _, vjp = jax.vjp(_gdn_gate, ov, gv, w)
        return vjp(d)

    return _rowwise(fn, "gdn_gate_bwd", rows, tm,
                    [(o, (GDN_WIDTH, 0)), (proj, (GDN_WIDTH, 3)), (og, None), (dy, (GDN_WIDTH, 0))],
                    [(GDN_WIDTH, f32), (GDN_WIDTH, f32)], [(1, GDN_DIM)])


def _sb_visible(i, j, valid):
    qpos = i * SB_BLOCK + lax.broadcasted_iota(i32, (SB_BLOCK, SB_BLOCK), 0)
    kpos = j * SB_BLOCK + lax.broadcasted_iota(i32, (SB_BLOCK, SB_BLOCK), 1)
    return (kpos < qpos) & (kpos >= FRONT) & valid


def _sb_logs(z, vis):
    l1p = jnp.log(1.0 + jnp.exp(-jnp.abs(z)))
    return -(jnp.maximum(-z, 0.0) + l1p), jnp.where(vis, -(jnp.maximum(z, 0.0) + l1p), 0.0)


def _tri_sum(x, tri):
    hi, lo = _split(x)
    return jnp.dot(hi, tri, preferred_element_type=f32) + jnp.dot(lo, tri, preferred_element_type=f32)


def _sb_live(t, i, runs):
    return (t <= i) & (jnp.max(functools.reduce(jnp.maximum, runs)) > -SB_UNDERFLOW)


def _sb_blocks(i, t, nb):
    js = [i - t - b for b in range(nb)]
    kss = [pl.ds(pl.multiple_of(jnp.maximum(j, 0) * SB_BLOCK, SB_BLOCK), SB_BLOCK) for j in js]
    return kss, [_sb_visible(i, j, j >= 0) for j in js]


def _sb_weights(i, t, nb, qs, sls, k_r, runs, after, scale):
    nh = len(qs)
    kss, vis = _sb_blocks(i, t, nb)
    units = [(a, b) for b in range(nb) for a in range(nh)]
    z = [_bdot(qs[a], k_r[kss[b], sls[a]], "nt") * scale for a, b in units]
    logs = [_sb_logs(zz, vis[b]) for zz, (a, b) in zip(z, units)]
    later = [_tri_sum(l[1], after) for l in logs]
    sums = [jnp.sum(l[1], axis=1, keepdims=True) for l in logs]
    w = []
    runs = list(runs)
    for b in range(nb):
        for a in range(nh):
            u = b * nh + a
            w.append(jnp.where(vis[b], jnp.exp(logs[u][0] + later[u] + runs[a]), 0.0))
        runs = [runs[a] + sums[b * nh + a] for a in range(nh)]
    return kss, vis, units, logs, w, tuple(runs)


def _sb_fwd(q, kv, width):
    rows = q.shape[0]
    nq = rows // SB_BLOCK
    lanes = SB_FWD_HEADS * SB_DIM
    npair = width // lanes
    scale = SB_DIM ** -0.5

    def body(q_r, k_r, v_r, o_r):
        i = pl.program_id(1)
        rj = lax.broadcasted_iota(i32, (SB_BLOCK, SB_BLOCK), 0)
        cs = lax.broadcasted_iota(i32, (SB_BLOCK, SB_BLOCK), 1)
        after = (rj > cs).astype(bf16)
        sls = [slice(a * SB_DIM, (a + 1) * SB_DIM) for a in range(SB_FWD_HEADS)]
        qs = [q_r[:, sl] for sl in sls]

        def step(carry, nb):
            t, accs, runs = carry
            kss, _, units, _, w, runs = _sb_weights(i, t, nb, qs, sls, k_r, runs, after, scale)
            prods = [_bdot(ww, v_r[kss[b], sls[a]], "nn") for ww, (a, b) in zip(w, units)]
            accs = tuple(functools.reduce(jnp.add, [accs[a]] + prods[a::SB_FWD_HEADS]) for a in range(SB_FWD_HEADS))
            return t + nb, accs, runs

        init = (jnp.int32(0), tuple(jnp.zeros((SB_BLOCK, SB_DIM), f32) for _ in sls),
                tuple(jnp.zeros((SB_BLOCK, 1), f32) for _ in sls))
        _, accs, _ = lax.while_loop(lambda c: _sb_live(c[0], i, c[2]), lambda c: step(c, 2), step(init, SB_FIRST))
        o_r[...] = jnp.concatenate(accs, axis=1)

    return pl.pallas_call(
        body, name="sb_fwd", out_shape=jax.ShapeDtypeStruct((rows, width), f32), grid=(npair, nq),
        in_specs=[pl.BlockSpec((SB_BLOCK, lanes), lambda p, i: (i, p)),
                  pl.BlockSpec((rows, lanes), lambda p, i: (0, p)),
                  pl.BlockSpec((rows, lanes), lambda p, i: (0, npair + p))],
        out_specs=pl.BlockSpec((SB_BLOCK, lanes), lambda p, i: (i, p)),
        compiler_params=_params(("parallel", "arbitrary")),
    )(q, kv, kv)


def _sb_bwd(q, kv, do, width):
    rows = q.shape[0]
    nq = rows // SB_BLOCK
    npair = width // LANES
    nh = LANES // SB_DIM
    scale = SB_DIM ** -0.5

    def body(q_r, k_r, v_r, do_r, dq_r, dk_r, dv_r, e_sc, sig_sc, w_sc):
        i = pl.program_id(1)

        @pl.when(i == 0)
        def _():
            dk_r[...] = jnp.zeros_like(dk_r)
            dv_r[...] = jnp.zeros_like(dv_r)

        rj = lax.broadcasted_iota(i32, (SB_BLOCK, SB_BLOCK), 0)
        cs = lax.broadcasted_iota(i32, (SB_BLOCK, SB_BLOCK), 1)
        after = (rj > cs).astype(bf16)
        from_s = (rj >= cs).astype(bf16)
        zero1 = jnp.zeros((SB_BLOCK, 1), f32)
        sls = [slice(a * SB_DIM, (a + 1) * SB_DIM) for a in range(nh)]
        qs = [q_r[:, sl] for sl in sls]
        dos = [do_r[:, sl] for sl in sls]

        def weigh(carry, nb):
            t, runs, eruns = carry
            kss, _, units, logs, w, runs = _sb_weights(i, t, nb, qs, sls, k_r, runs, after, scale)
            dw = [_bdot(dos[a], v_r[kss[b], sls[a]], "nt") for a, b in units]
            e = [ww * d for ww, d in zip(w, dw)]
            for u, (a, b) in enumerate(units):
                e_sc[a, t + b] = e[u]
                sig_sc[a, t + b] = jnp.exp(logs[u][0])
                w_sc[a, t + b] = w[u].astype(w_sc.dtype)
            sums = [jnp.sum(ee, axis=1, keepdims=True) for ee in e]
            eruns = tuple(functools.reduce(jnp.add, [eruns[a]] + sums[a::nh]) for a in range(nh))
            return t + nb, runs, eruns

        n_blk, _, etots = lax.while_loop(lambda c: _sb_live(c[0], i, c[1]), lambda c: weigh(c, 2),
                                         weigh((jnp.int32(0), (zero1,) * nh, (zero1,) * nh), SB_FIRST))

        def push(t, carry, nb):
            dqs, eruns = carry
            kss, vis = _sb_blocks(i, t, nb)
            units = [(a, b) for b in range(nb) for a in range(nh)]
            e = [e_sc[a, t + b] for a, b in units]
            dvs = [_bdot(w_sc[a, t + b], dos[a], "tn") for a, b in units]
            upto = [_tri_sum(ee, from_s) for ee in e]
            sums = [jnp.sum(ee, axis=1, keepdims=True) for ee in e]
            dz = []
            eruns = list(eruns)
            for b in range(nb):
                for a in range(nh):
                    u = b * nh + a
                    sig = sig_sc[a, t + b]
                    before = etots[a] - eruns[a] - upto[u]
                    dz.append(jnp.where(vis[b], e[u] * (1.0 - sig) - before * sig, 0.0) * scale)
                eruns = [eruns[a] + sums[b * nh + a] for a in range(nh)]
            dks = [_bdot(d, qs[a], "tn") for d, (a, b) in zip(dz, units)]
            dqp = [_bdot(d, k_r[kss[b], sls[a]], "nn") for d, (a, b) in zip(dz, units)]
            for b in range(nb):
                dk_r[kss[b], :] += jnp.concatenate(dks[b * nh:(b + 1) * nh], axis=1)
                dv_r[kss[b], :] += jnp.concatenate(dvs[b * nh:(b + 1) * nh], axis=1)
            dqs = tuple(functools.reduce(jnp.add, [dqs[a]] + dqp[a::nh]) for a in range(nh))
            return dqs, tuple(eruns)

        first = push(jnp.int32(0), (tuple(jnp.zeros((SB_BLOCK, SB_DIM), f32) for _ in sls), (zero1,) * nh), SB_FIRST)
        dqs, _ = lax.fori_loop(0, (n_blk - SB_FIRST) // 2, lambda p, c: push(SB_FIRST + 2 * p, c, 2), first)
        dq_r[...] = jnp.concatenate(dqs, axis=1)

    blk = pl.BlockSpec((SB_BLOCK, LANES), lambda p, i: (i, p))
    col = pl.BlockSpec((rows, LANES), lambda p, i: (0, p))
    wide = jax.ShapeDtypeStruct((rows, width), f32)
    depth = nq + SB_FIRST
    return pl.pallas_call(
        body, name="sb_bwd", out_shape=[wide] * 3, grid=(npair, nq),
        in_specs=[blk, col, pl.BlockSpec((rows, LANES), lambda p, i: (0, npair + p)), blk],
        out_specs=[blk, col, col],
        scratch_shapes=[pltpu.VMEM((nh, depth, SB_BLOCK, SB_BLOCK), f32), pltpu.VMEM((nh, depth, SB_BLOCK, SB_BLOCK), f32),
                        pltpu.VMEM((nh, depth, SB_BLOCK, SB_BLOCK), bf16)],
        compiler_params=_params(("parallel", "arbitrary")),
    )(q, kv, kv, do)


_FLIPS = ((1, 0), (0, 1), (1, 1))
_ANY = pl.BlockSpec(memory_space=pl.ANY)


def _flip(v, a):
    return v + a - 2 * a * v


def _gather_plan(ins, outs, send_sems, recv_sems, local_sems):
    num = len(ins)
    x, y, c = lax.axis_index("x"), lax.axis_index("y"), lax.axis_index("c")
    me, sibling = (x, y, c), (x, y, 1 - c)
    chip = 2 * x + y
    others = [(_flip(x, a), _flip(y, b)) for a, b in _FLIPS]
    pairs = [(k, n, 2 * ox + oy) for k in range(num) for n, (ox, oy) in enumerate(others)]

    def half_of(ref, hc):
        half = ref.shape[0] // 2
        start = hc * half
        for align in (16, 8):
            if half % align == 0:
                start = pl.multiple_of(start, align)
                break
        return ref.at[pl.ds(start, half)]

    def copy(k, n, s, hc, to, src=None):
        dst = half_of(outs[k].at[s], hc)
        return pltpu.make_async_remote_copy(
            src_ref=dst if src is None else src, dst_ref=dst,
            send_sem=send_sems.at[6 * k + n], recv_sem=recv_sems.at[6 * k + n], device_id=to, device_id_type=MESH)

    mine = [pltpu.make_async_copy(ins[k], outs[k].at[chip], local_sems.at[k]) for k in range(num)]
    first = [copy(k, n, chip, c, (others[n][0], others[n][1], c), src=half_of(ins[k], c)) for k, n, _ in pairs]
    passed = [copy(k, 3 + n, s, c, sibling) for k, n, s in pairs]

    def start():
        for cp in mine + first:
            cp.start()

    def forward():
        for (k, n, s), fw in zip(pairs, passed):
            copy(k, n, s, c, me).wait_recv()
            fw.start()

    def finish():
        for k, n, s in pairs:
            copy(k, 3 + n, s, 1 - c, me).wait_recv()
        for cp in first + passed:
            cp.wait_send()
        for cp in mine:
            cp.wait()

    return start, forward, finish


def _gather_chips(shards):
    num = len(shards)

    def body(*refs):
        for phase in _gather_plan(refs[:num], refs[num:2 * num], *refs[2 * num:]):
            phase()

    return pl.pallas_call(
        body, name="gather_chips", out_shape=[jax.ShapeDtypeStruct((N_CHIPS,) + t.shape, t.dtype) for t in shards],
        in_specs=[_ANY] * num, out_specs=[_ANY] * num,
        scratch_shapes=[pltpu.SemaphoreType.DMA((6 * num,)), pltpu.SemaphoreType.DMA((6 * num,)),
                        pltpu.SemaphoreType.DMA((num,))],
    )(*shards)


def _scatter_plan(ins, outs, send_sems, recv_sems):
    x, y, c = lax.axis_index("x"), lax.axis_index("y"), lax.axis_index("c")
    cps = []
    for k in range(len(ins)):
        for n, (a, b) in enumerate(_FLIPS):
            ox, oy = _flip(x, a), _flip(y, b)
            cps.append(pltpu.make_async_remote_copy(
                src_ref=ins[k].at[2 * ox + oy], dst_ref=outs[k].at[n], send_sem=send_sems.at[3 * k + n],
                recv_sem=recv_sems.at[3 * k + n], device_id=(ox, oy, c), device_id_type=MESH))

    def start():
        for cp in cps:
            cp.start()

    def finish():
        for cp in cps:
            cp.wait()

    return start, finish


def _scatter_chips(parts):
    num = len(parts)

    def body(*refs):
        for phase in _scatter_plan(refs[:num], refs[num:2 * num], *refs[2 * num:]):
            phase()

    return pl.pallas_call(
        body, name="scatter_chips", out_shape=[jax.ShapeDtypeStruct((3,) + t.shape[1:], t.dtype) for t in parts],
        in_specs=[_ANY] * num, out_specs=[_ANY] * num,
        scratch_shapes=[pltpu.SemaphoreType.DMA((3 * num,)), pltpu.SemaphoreType.DMA((3 * num,))],
    )(*parts)


def _swap_sibling(arrs):
    num = len(arrs)

    def body(*refs):
        ins, outs = refs[:num], refs[num:2 * num]
        send_sems, recv_sems = refs[2 * num:]
        x, y, c = lax.axis_index("x"), lax.axis_index("y"), lax.axis_index("c")
        cps = [pltpu.make_async_remote_copy(src_ref=ins[k], dst_ref=outs[k], send_sem=send_sems.at[k],
                                            recv_sem=recv_sems.at[k], device_id=(x, y, 1 - c), device_id_type=MESH)
               for k in range(num)]
        for cp in cps:
            cp.start()
        for cp in cps:
            cp.wait()

    return pl.pallas_call(
        body, name="swap_sibling", out_shape=[jax.ShapeDtypeStruct(t.shape, t.dtype) for t in arrs],
        in_specs=[_ANY] * num, out_specs=[_ANY] * num,
        scratch_shapes=[pltpu.SemaphoreType.DMA((num,)), pltpu.SemaphoreType.DMA((num,))],
    )(*arrs)


def _gather_all(v):
    m_per, n = v.shape

    def body(x_ref, out_ref, send_sems, recv_sems, local_sem):
        x, y, c = lax.axis_index("x"), lax.axis_index("y"), lax.axis_index("c")
        me, sibling = (x, y, c), (x, y, 1 - c)
        chips = [(_flip(x, a), _flip(y, b)) for a, b in _FLIPS]

        def rows(px, py, pc):
            return out_ref.at[pl.ds(pl.multiple_of((4 * px + 2 * py + pc) * m_per, 8), m_per), :]

        def copy(k, block, to, src=None):
            return pltpu.make_async_remote_copy(
                src_ref=rows(*block) if src is None else src, dst_ref=rows(*block),
                send_sem=send_sems.at[k], recv_sem=recv_sems.at[k], device_id=to, device_id_type=MESH)

        mine = pltpu.make_async_copy(x_ref, rows(*me), local_sem)
        mine.start()
        first = [copy(0, me, sibling, src=x_ref)]
        first += [copy(1 + j, me, (*chip, c), src=x_ref) for j, chip in enumerate(chips)]
        for cp in first:
            cp.start()
        passed = [copy(4 + j, (*chip, c), sibling) for j, chip in enumerate(chips)]
        for j, chip in enumerate(chips):
            copy(1 + j, (*chip, c), me).wait_recv()
            passed[j].start()
        copy(0, sibling, me).wait_recv()
        for j, chip in enumerate(chips):
            copy(4 + j, (*chip, 1 - c), me).wait_recv()
        for cp in first + passed:
            cp.wait_send()
        mine.wait()

    return pl.pallas_call(
        body, name="gather_all", out_shape=jax.ShapeDtypeStruct((N_DEV * m_per, n), v.dtype),
        in_specs=[pl.BlockSpec(memory_space=pltpu.VMEM)], out_specs=pl.BlockSpec(memory_space=pltpu.VMEM),
        scratch_shapes=[pltpu.SemaphoreType.DMA((7,)), pltpu.SemaphoreType.DMA((7,)), pltpu.SemaphoreType.DMA],
    )(v)


def _sum_chips(parts, got, chip, name):
    cols = parts.shape[-1]
    rows = parts.size // (N_CHIPS * cols)
    tm = _pick(rows, (256, 128, 64, 32, 16))

    def body(chip_r, own_r, got_r, o_r):
        acc = own_r[0]
        for n in range(3):
            acc = acc + got_r[n].astype(f32)
        o_r[...] = acc

    return pl.pallas_call(
        body, name=name, out_shape=jax.ShapeDtypeStruct((rows, cols), f32),
        grid_spec=pltpu.PrefetchScalarGridSpec(
            num_scalar_prefetch=1, grid=(rows // tm,),
            in_specs=[pl.BlockSpec((1, tm, cols), lambda i, s: (s[0], i, 0)),
                      pl.BlockSpec((3, tm, cols), lambda i, s: (0, i, 0))],
            out_specs=pl.BlockSpec((tm, cols), lambda i, s: (i, 0))),
        compiler_params=_params(("parallel",)),
    )(chip, parts.reshape(N_CHIPS, rows, cols), got.reshape(3, rows, cols))


def _sum_devices(g, m_per):
    n = g.shape[1]

    def body(g_r, o_r):
        acc = g_r[0:m_per, :]
        for d in range(1, N_DEV):
            acc = acc + g_r[d * m_per:(d + 1) * m_per, :]
        o_r[...] = acc

    return pl.pallas_call(body, name="sum_devices", out_shape=jax.ShapeDtypeStruct((m_per, n), f32))(g)


def _adamw(w, gs, m, v, name):
    shape = w.shape
    cols = shape[-1]
    rows = w.size // cols
    tm = _pick(rows, (256, 128, 64, 32, 16, 8)) if rows * cols * 4 > (1 << 20) else rows

    def fn(i, wv, mv, vv, *gv):
        g = functools.reduce(jnp.add, gv)
        mn = ADAM_B1 * mv + (1.0 - ADAM_B1) * g
        vn = ADAM_B2 * vv + (1.0 - ADAM_B2) * jnp.square(g)
        m_hat = mn / (1.0 - ADAM_B1 ** ADAM_STEP)
        v_hat = vn / (1.0 - ADAM_B2 ** ADAM_STEP)
        delta = -ADAM_LR * (m_hat / (jnp.sqrt(v_hat) + ADAM_EPS) + ADAM_WD * wv)
        return g, delta, mn, vn

    outs = _rowwise(fn, name, rows, tm, [(t.reshape(rows, cols), (cols, 0)) for t in (w, m, v) + tuple(gs)], [(cols, f32)] * 4)
    return tuple(o.reshape(shape) for o in outs)


def _pack(pieces, rows, dtype):
    flat = jnp.concatenate([p.reshape(-1).astype(dtype) for p in pieces])
    return jnp.pad(flat, (0, rows * PACK_COLS - flat.size)).reshape(rows, PACK_COLS)


def _unpack(buf, shapes):
    lead = buf.shape[:-2]
    flat = buf.reshape(lead + (-1,))
    out, off = [], 0
    for s in shapes:
        n = 1
        for d in s:
            n *= d
        out.append(flat[..., off:off + n].reshape(lead + tuple(s)))
        off += n
    return out


def _join_cols(t):
    return jnp.moveaxis(t, 0, -2).reshape(t.shape[1:-1] + (N_CHIPS * t.shape[-1],))


def _join_rows(t):
    return t.reshape((N_CHIPS * t.shape[1],) + t.shape[2:])


def _split_cols(t, parts=N_CHIPS):
    r, cols = t.shape
    return jnp.moveaxis(t.reshape(r, parts, cols // parts), 1, 0)


def _split_rows(t):
    return t.reshape((N_CHIPS, t.shape[0] // N_CHIPS) + t.shape[1:])


def kernel(x, meta_tokens, gdn_norm_g, gdn_w_in, gdn_conv_w, gdn_a_log, gdn_dt_bias, gdn_onorm_g, gdn_w_out, kv_norm_g, w_kv, sb_norm_g, sb_w_q, sb_w_o, ffn_norm_g, ffn_w_gate_up, ffn_w_down, final_norm_g, loss_target, m_meta_tokens, m_gdn_norm_g, m_gdn_w_in, m_gdn_conv_w, m_gdn_a_log, m_gdn_dt_bias, m_gdn_onorm_g, m_gdn_w_out, m_kv_norm_g, m_w_kv, m_sb_norm_g, m_sb_w_q, m_sb_w_o, m_ffn_norm_g, m_ffn_w_gate_up, m_ffn_w_down, m_final_norm_g, v_meta_tokens, v_gdn_norm_g, v_gdn_w_in, v_gdn_conv_w, v_gdn_a_log, v_gdn_dt_bias, v_gdn_onorm_g, v_gdn_w_out, v_kv_norm_g, v_w_kv, v_sb_norm_g, v_sb_w_q, v_sb_w_o, v_ffn_norm_g, v_ffn_w_gate_up, v_ffn_w_down, v_final_norm_g):
    weights = dict(meta_tokens=meta_tokens, gdn_norm_g=gdn_norm_g, gdn_w_in=gdn_w_in, gdn_conv_w=gdn_conv_w,
                   gdn_a_log=gdn_a_log, gdn_dt_bias=gdn_dt_bias, gdn_onorm_g=gdn_onorm_g, gdn_w_out=gdn_w_out,
                   kv_norm_g=kv_norm_g, w_kv=w_kv, sb_norm_g=sb_norm_g, sb_w_q=sb_w_q, sb_w_o=sb_w_o,
                   ffn_norm_g=ffn_norm_g, ffn_w_gate_up=ffn_w_gate_up, ffn_w_down=ffn_w_down, final_norm_g=final_norm_g)
    m_in = dict(meta_tokens=m_meta_tokens, gdn_norm_g=m_gdn_norm_g, gdn_w_in=m_gdn_w_in, gdn_conv_w=m_gdn_conv_w,
                gdn_a_log=m_gdn_a_log, gdn_dt_bias=m_gdn_dt_bias, gdn_onorm_g=m_gdn_onorm_g, gdn_w_out=m_gdn_w_out,
                kv_norm_g=m_kv_norm_g, w_kv=m_w_kv, sb_norm_g=m_sb_norm_g, sb_w_q=m_sb_w_q, sb_w_o=m_sb_w_o,
                ffn_norm_g=m_ffn_norm_g, ffn_w_gate_up=m_ffn_w_gate_up, ffn_w_down=m_ffn_w_down, final_norm_g=m_final_norm_g)
    v_in = dict(meta_tokens=v_meta_tokens, gdn_norm_g=v_gdn_norm_g, gdn_w_in=v_gdn_w_in, gdn_conv_w=v_gdn_conv_w,
                gdn_a_log=v_gdn_a_log, gdn_dt_bias=v_gdn_dt_bias, gdn_onorm_g=v_gdn_onorm_g, gdn_w_out=v_gdn_w_out,
                kv_norm_g=v_kv_norm_g, w_kv=v_w_kv, sb_norm_g=v_sb_norm_g, sb_w_q=v_sb_w_q, sb_w_o=v_sb_w_o,
                ffn_norm_g=v_ffn_norm_g, ffn_w_gate_up=v_ffn_w_gate_up, ffn_w_down=v_ffn_w_down, final_norm_g=v_final_norm_g)
    names = list(weights)

    seq, d = x.shape[1], x.shape[2]
    lo_frames = FRONT + N_META
    used = lo_frames + seq
    rows = -(-used // SB_BLOCK) * SB_BLOCK
    tm = _pick(rows, (640, 512, 384, 256, 128))
    tp = _pick(rows, (320, 256, 128))
    n_ffn = ffn_w_gate_up.shape[0]
    sb_width = sb_w_q.shape[2]
    chip =2 * lax.axis_index("x") + lax.axis_index("y")

    big = [gdn_w_in[0], gdn_w_out[0], w_kv, sb_w_q[0], sb_w_o[0], ffn_w_gate_up, ffn_w_down]
    small = [meta_tokens, gdn_norm_g, gdn_conv_w[0]]
    n_early = 2
    big_bf16 = [t.astype(bf16) for t in big]
    w_in_s, w_out_s, small_g = _gather_chips(big_bf16[:n_early] + [_pack(small, 16, f32)])
    small_s = _unpack(small_g, [t.shape for t in small])
    w_in = _join_cols(w_in_s)
    pad_ab = jnp.zeros((d, LANES - GDN_HEADS), bf16)
    w_in_ext = jnp.concatenate([w_in[:, :4 * GDN_WIDTH], w_in[:, 4 * GDN_WIDTH:4 * GDN_WIDTH + GDN_HEADS], pad_ab,
                                w_in[:, 4 * GDN_WIDTH + GDN_HEADS:], pad_ab], axis=1)
    w_out = _join_rows(w_out_s)
    meta_full, gdn_g_full, conv_full = (_join_cols(t) for t in small_s)

    zeros = lambda n: jnp.zeros((n, d), f32)
    h0 = jnp.concatenate([zeros(FRONT), meta_full, x[0], zeros(rows - used)], axis=0)
    tgt = jnp.concatenate([zeros(lo_frames), loss_target[0], zeros(rows - used)], axis=0)
    pad8 = lambda t: jnp.pad(t, ((0, 0), (0, LANES - t.shape[1])))
    a_log8, dt_bias8 = pad8(gdn_a_log), pad8(gdn_dt_bias)
    r_i = jnp.arange(tp)
    ltri = ((r_i[:, None] >= r_i[None, :]) & (r_i[:, None] // CHUNK == r_i[None, :] // CHUNK)).astype(f32)
    ffn_g = [ffn_norm_g[l:l + 1] for l in range(n_ffn)]
    kv_g, fin_g = kv_norm_g.reshape(1, d), final_norm_g.reshape(1, d)

    n0 = _rms_fwd(h0, gdn_g_full, "gdn_norm")
    proj = _matmul(n0, w_in_ext, "nn", "gdn_proj")
    gq, gk, gv, gw, bw = _gdn_prep_fwd(proj, conv_full, a_log8, dt_bias8, ltri, FRONT, used, tp)
    g_o, g_states, (w_kv_s, w_q_s, w_o_s, w_gu_s, w_dn_s) = _gdn_fwd(gq, gk, gv, gw, bw, big_bf16[n_early:])
    w_kvf = _join_cols(w_kv_s)
    w_q = _join_rows(w_q_s)
    w_o = _join_rows(w_o_s)
    w_gu = [_join_cols(w_gu_s[:, l]) for l in range(n_ffn)]
    w_dn = [_join_rows(w_dn_s[:, l]) for l in range(n_ffn)]
    og = _gdn_gate_fwd(g_o, proj, gdn_onorm_g, tm)
    h1 = _matmul(og, w_out, "nn", "gdn_out", res=h0)

    def ffn_fwd(h, l):
        n = _rms_fwd(h, ffn_g[l], f"ffn{l}_norm")
        gate, up, act = _ffn_up(n, w_gu[l], f"ffn{l}_gate_up")
        return _matmul(act, w_dn[l], "nn", f"ffn{l}_down", res=h), (n, gate, up, act)

    h2, ffn0_saved = ffn_fwd(h1, 0)
    n_kv = _rms_fwd(h2, kv_g, "kv_norm")
    kv = _matmul(n_kv, w_kvf, "nn", "kv_proj", out_dtype=bf16)
    n_sb = _rms_fwd(h2, sb_norm_g, "sb_norm")
    sq = _matmul(n_sb, w_q, "nn", "q_proj", out_dtype=bf16)
    s_o = _sb_fwd(sq, kv, sb_width)
    h3 = _matmul(s_o, w_o, "nn", "sb_out", res=h2)
    h4, ffn1_saved = ffn_fwd(h3, 1)
    dh4, d_fin_g, loss_part = _loss_head(h4, fin_g, tgt, lo_frames, used, "loss_head")

    def ffn_bwd(dh, h, l, saved):
        n, gate, up, act = saved
        d_wdn = _matmul(act, dh, "tn", f"ffn{l}_d_w_down")
        d_gate, d_up = _ffn_dact(dh, w_dn[l], gate, up, f"ffn{l}_d_gate_up")
        d_wgu = jnp.concatenate([_split_cols(_matmul(n, d_gate, "tn", f"ffn{l}_d_w_gate"), N_CHIPS // 2),
                                 _split_cols(_matmul(n, d_up, "tn", f"ffn{l}_d_w_up"), N_CHIPS // 2)], axis=0)
        dh_in, dg = _norm_bwd([d_gate, d_up], w_gu[l], h, ffn_g[l], dh, f"ffn{l}_d_norm")
        return dh_in, d_wgu, d_wdn, dg

    dh3, d_wgu1, d_wdn1, d_ffn_g1 = ffn_bwd(dh4, h3, 1, ffn1_saved)
    d_wo = _matmul(s_o, dh3, "tn", "d_w_o")
    d_so = _matmul(dh3, w_o, "nt", "d_sb_o")
    d_sq, d_sk, d_sv = _sb_bwd(sq, kv, d_so, sb_width)
    d_wq = _matmul(n_sb, d_sq, "tn", "d_w_q")
    dh2, d_sb_g = _norm_bwd([d_sq], w_q, h2, sb_norm_g, dh3, "d_sb_norm")
    d_wkv = jnp.concatenate([_matmul(n_kv, d_sk, "tn", "d_w_k"), _matmul(n_kv, d_sv, "tn", "d_w_v")], axis=1)
    dh2, d_kv_g = _norm_bwd([d_sk, d_sv], w_kvf, h2, kv_g, dh2, "d_kv_norm")
    dh1, d_wgu0, d_wdn0, d_ffn_g0 = ffn_bwd(dh2, h1, 0, ffn0_saved)
    d_wout = _matmul(og, dh1, "tn", "d_w_out")
    d_og = _matmul(dh1, w_out, "nt", "d_gdn_gated")
    d_go, d_gate, d_onorm = _gdn_gate_bwd(g_o, proj, gdn_onorm_g, d_og, tm)
    by_chip = [None, _split_rows(d_wout), _split_cols(d_wkv), _split_rows(d_wq), _split_rows(d_wo),
               jnp.stack([d_wgu0, d_wgu1], axis=1),
               jnp.stack([_split_rows(d_wdn0), _split_rows(d_wdn1)], axis=1)]
    (d_gq, d_gk, d_gv, d_gw, d_bw), got_early = _gdn_bwd(gq, gk, gv, gw, bw, g_states, d_go,
                                                         [t.astype(bf16) for t in by_chip[1:]])
    dconv, d_a_in, d_b_in, d_a_log8, d_dt_bias8 = _gdn_prep_bwd_act(
        proj, conv_full, a_log8, dt_bias8, ltri, d_gq, d_gk, d_gv, d_gw, d_bw, FRONT, used, tp)
    dproj, d_conv = _gdn_prep_bwd_conv(proj, conv_full, dconv, d_gate, d_a_in, d_b_in, tp)
    d_win_ext = _matmul(n0, dproj, "tn", "d_w_in")
    dh0, d_gdn_g = _norm_bwd([dproj], w_in_ext, h0, gdn_g_full, dh1, "d_gdn_norm")
    grad_x = dh0[lo_frames:used][None]
    d_win = jnp.concatenate([d_win_ext[:, :4 * GDN_WIDTH], d_win_ext[:, 4 * GDN_WIDTH:4 * GDN_WIDTH + GDN_HEADS],
                             d_win_ext[:, 4 * GDN_WIDTH + LANES:4 * GDN_WIDTH + LANES + GDN_HEADS]], axis=1)

    by_chip[0] = _split_cols(d_win)
    got = list(_scatter_chips([by_chip[0].astype(bf16)])) + list(got_early)
    chip_arr = jnp.reshape(chip, (1,)).astype(i32)
    over_chips = [_sum_chips(t, g, chip_arr, f"sum_chips_{k}") for k, (t, g) in enumerate(zip(by_chip, got))]
    over_sibling = _swap_sibling(over_chips)
    big_names = ["gdn_w_in", "gdn_w_out", "w_kv", "sb_w_q", "sb_w_o", "ffn_w_gate_up", "ffn_w_down"]
    g_big = dict(zip(big_names, zip(over_chips, over_sibling)))

    small_parts = [dh0[FRONT:lo_frames], d_gdn_g, d_conv, d_a_log8, d_dt_bias8, d_onorm, d_kv_g, d_sb_g,
                   d_ffn_g0, d_ffn_g1, d_fin_g, loss_part]
    s_rows = -(-sum(t.size for t in small_parts) // (8 * PACK_COLS)) * 8
    s_sum = _sum_devices(_gather_all(_pack(small_parts, s_rows, f32)), s_rows)
    (g_meta, g_gdn_g, g_conv, g_a_log8, g_dt8, g_onorm, g_kv_g, g_sb_g, g_ffn_g0, g_ffn_g1, g_fin_g,
     loss_v) = _unpack(s_sum, [t.shape for t in small_parts])
    col_shard = lambda t, w: lax.dynamic_slice_in_dim(t, chip * w, w, axis=t.ndim - 1)

    g_small = dict(
        meta_tokens=col_shard(g_meta, meta_tokens.shape[1]), gdn_norm_g=col_shard(g_gdn_g, gdn_norm_g.shape[1]),
        gdn_conv_w=col_shard(g_conv, gdn_conv_w.shape[2])[None],
        gdn_a_log=g_a_log8[:, :GDN_HEADS], gdn_dt_bias=g_dt8[:, :GDN_HEADS], gdn_onorm_g=g_onorm,
        kv_norm_g=g_kv_g.reshape(-1), sb_norm_g=g_sb_g, ffn_norm_g=jnp.concatenate([g_ffn_g0, g_ffn_g1], axis=0),
        final_norm_g=g_fin_g.reshape(-1))

    grads, delta, new_m, new_v = {}, {}, {}, {}
    for n in names:
        gs = g_big[n] if n in g_big else (g_small[n],)
        grads[n], delta[n], new_m[n], new_v[n] = _adamw(weights[n], gs, m_in[n], v_in[n], f"adamw_{n}")
    loss = loss_v[0, 0]
    return (loss, grad_x, *[grads[n] for n in names], *[delta[n] for n in names],
            *[new_m[n] for n in names], *[new_v[n] for n in names])
```

```python
import functools

import jax
import jax.numpy as jnp
from jax import lax
from jax.experimental import pallas as pl
from jax.experimental.pallas import tpu as pltpu

f32 = jnp.float32
bf16 = jnp.bfloat16
i32 = jnp.int32

EPS = 1e-6
N_META = 16
CHUNK = 64
FRONT = (-N_META) % CHUNK
GDN_HEADS = 8
GDN_DIM = 128
GDN_WIDTH = GDN_HEADS * GDN_DIM
CONV_WIDTH = 4
SB_DIM = 64
SB_BLOCK = 128
SB_FWD_HEADS = 4
SB_FIRST = 3
SB_UNDERFLOW = 104.0
LANES = 128
PACK_COLS = 1024
N_CHIPS = 4
N_DEV = 8
ADAM_LR, ADAM_B1, ADAM_B2, ADAM_EPS, ADAM_WD, ADAM_STEP = 0.001, 0.9, 0.999, 1e-08, 0.01, 10
VMEM_LIMIT = 56 * 1024 * 1024
MESH = pl.DeviceIdType.MESH


def _pick(n, prefs):
    for p in prefs:
        if n % p == 0:
            return p
    return n


def _params(sem):
    return pltpu.CompilerParams(dimension_semantics=sem, vmem_limit_bytes=VMEM_LIMIT)


_DIMS = {"nn": ((1,), (0,)), "nt": ((1,), (1,)), "tn": ((0,), (0,))}


def _bdot(a, b, mode):
    return lax.dot_general(a.astype(bf16), b.astype(bf16), (_DIMS[mode], ((), ())), preferred_element_type=f32)


def _matmul(a, b, mode, name, res=None, out_dtype=f32):
    if mode == "nn":
        (m, k), n = a.shape, b.shape[1]
    elif mode == "nt":
        (m, k), n = a.shape, b.shape[0]
    else:
        (k, m), n = a.shape, b.shape[1]
    tm = _pick(m, (640, 1408, 1024, 512, 384, 256, 128))
    tn = _pick(n, (1408, 2176, 1024, 512, 384, 256, 128))
    tk = _pick(k, (1408, 2176, 1024, 640, 512, 384, 256, 128))
    nk = k // tk
    a_spec = pl.BlockSpec((tk, tm), lambda j, i, kk: (kk, i)) if mode == "tn" else pl.BlockSpec((tm, tk), lambda j, i, kk: (i, kk))
    b_spec = pl.BlockSpec((tn, tk), lambda j, i, kk: (j, kk)) if mode == "nt" else pl.BlockSpec((tk, tn), lambda j, i, kk: (kk, j))
    o_spec = pl.BlockSpec((tm, tn), lambda j, i, kk: (i, j))
    has_res = res is not None

    def body(*refs):
        if has_res:
            a_ref, b_ref, r_ref, o_ref, acc = refs
        else:
            a_ref, b_ref, o_ref, acc = refs
        kk = pl.program_id(2)

        @pl.when(kk == 0)
        def _():
            acc[...] = jnp.zeros_like(acc)

        acc[...] += _bdot(a_ref[...], b_ref[...], mode)

        @pl.when(kk == nk - 1)
        def _():
            y = acc[...]
            if has_res:
                y = y + r_ref[...]
            o_ref[...] = y.astype(o_ref.dtype)

    ins = [a, b] + ([res] if has_res else [])
    specs = [a_spec, b_spec] + ([o_spec] if has_res else [])
    return pl.pallas_call(
        body, name=name, out_shape=jax.ShapeDtypeStruct((m, n), out_dtype), grid=(n // tn, m // tm, nk),
        in_specs=specs, out_specs=o_spec, scratch_shapes=[pltpu.VMEM((tm, tn), f32)],
        compiler_params=_params(("parallel", "parallel", "arbitrary")),
    )(*ins)


def _rowwise(fn, name, rows, tm, ins, outs, reds=()):
    n_in, n_out, n_red = len(ins), len(outs), len(reds)
    in_specs = []
    for arr, spec in ins:
        if spec is None:
            in_specs.append(pl.BlockSpec(arr.shape, lambda i, nd=arr.ndim: (0,) * nd))
        else:
            w, cb = spec
            in_specs.append(pl.BlockSpec((tm, w), lambda i, cb=cb: (i, cb)))
    out_specs = [pl.BlockSpec((tm, w), lambda i: (i, 0)) for w, _ in outs]
    out_specs += [pl.BlockSpec(s, lambda i, nd=len(s): (0,) * nd) for s in reds]
    out_shape = [jax.ShapeDtypeStruct((rows, w), dt) for w, dt in outs]
    out_shape += [jax.ShapeDtypeStruct(s, f32) for s in reds]

    def body(*refs):
        i = pl.program_id(0)
        vals = fn(i, *[r[...] for r in refs[:n_in]])
        for r, v in zip(refs[n_in:n_in + n_out], vals[:n_out]):
            r[...] = v.astype(r.dtype)
        red_refs = refs[n_in + n_out:]

        @pl.when(i == 0)
        def _():
            for r in red_refs:
                r[...] = jnp.zeros_like(r)

        for r, v in zip(red_refs, vals[n_out:]):
            r[...] += v

    res = pl.pallas_call(
        body, name=name, out_shape=out_shape, grid=(rows // tm,), in_specs=in_specs, out_specs=out_specs,
        compiler_params=_params(("arbitrary",)),
    )(*[a for a, _ in ins])
    return res


def _rms(x, g):
    return x * lax.rsqrt(jnp.mean(x * x, axis=-1, keepdims=True) + EPS) * g


def _row_mask(i, tm, lo, hi, shape):
    r = i * tm + lax.broadcasted_iota(i32, shape, 0)
    return (r >= lo) & (r < hi)


def _rms_fwd(x, g, name):
    rows, d = x.shape
    tm = _pick(rows, (640, 512, 384, 256, 128))
    return _rowwise(lambda i, xv, gv: (_rms(xv, gv),), name, rows, tm, [(x, (d, 0)), (g, None)], [(d, bf16)])[0]


def _norm_bwd(parts, w, x, g, res, name):
    rows, k = parts[0].shape
    d = w.shape[0]
    num = len(parts)
    tm = _pick(rows, (640, 512, 384, 256, 128))
    tk = _pick(k, (1408, 2176, 1024, 512, 384, 256, 128))
    nk = k // tk

    def body(*refs):
        a_refs, w_refs = refs[:num], refs[num:2 * num]
        x_r, g_r, r_r, o_r, dg_r, acc = refs[2 * num:]
        i, kk = pl.program_id(0), pl.program_id(1)

        @pl.when(kk == 0)
        def _():
            acc[...] = jnp.zeros_like(acc)

        acc[...] += functools.reduce(jnp.add, [_bdot(a[...], b[...], "nt") for a, b in zip(a_refs, w_refs)])

        @pl.when((i == 0) & (kk == 0))
        def _():
            dg_r[...] = jnp.zeros_like(dg_r)

        @pl.when(kk == nk - 1)
        def _():
            _, vjp = jax.vjp(_rms, x_r[...], g_r[...])
            dx, dg = vjp(acc[...])
            o_r[...] = r_r[...] + dx
            dg_r[...] += dg

    row = pl.BlockSpec((tm, d), lambda i, kk: (i, 0))
    one = pl.BlockSpec((1, d), lambda i, kk: (0, 0))
    return pl.pallas_call(
        body, name=name, out_shape=[jax.ShapeDtypeStruct((rows, d), f32), jax.ShapeDtypeStruct((1, d), f32)],
        grid=(rows // tm, nk),
        in_specs=[pl.BlockSpec((tm, tk), lambda i, kk: (i, kk))] * num
        + [pl.BlockSpec((d, tk), lambda i, kk, p=p: (0, p * nk + kk)) for p in range(num)] + [row, one, row],
        out_specs=[row, one], scratch_shapes=[pltpu.VMEM((tm, d), f32)],
        compiler_params=_params(("arbitrary", "arbitrary")),
    )(*parts, *([w] * num), x, g, res)


def _swiglu(gate, up):
    return jax.nn.silu(gate) * up


def _ffn_up(n, w_gu, name):
    rows, d = n.shape
    f = w_gu.shape[1] // 2
    tm = _pick(rows, (640, 512, 384, 256, 128))
    tn = _pick(f, (1408, 1024, 512, 384, 256, 128))
    nj = f // tn

    def body(n_r, wg_r, wu_r, g_r, u_r, a_r):
        g = jnp.dot(n_r[...], wg_r[...], preferred_element_type=f32)
        u = jnp.dot(n_r[...], wu_r[...], preferred_element_type=f32)
        g_r[...] = g.astype(g_r.dtype)
        u_r[...] = u.astype(u_r.dtype)
        a_r[...] = _swiglu(g, u).astype(a_r.dtype)

    o_spec = pl.BlockSpec((tm, tn), lambda j, i: (i, j))
    return pl.pallas_call(
        body, name=name, grid=(nj, rows // tm), out_shape=[jax.ShapeDtypeStruct((rows, f), bf16)] * 3,
        in_specs=[pl.BlockSpec((tm, d), lambda j, i: (i, 0)), pl.BlockSpec((d, tn), lambda j, i: (0, j)),
                  pl.BlockSpec((d, tn), lambda j, i: (0, nj + j))],
        out_specs=[o_spec] * 3, compiler_params=_params(("parallel", "parallel")),
    )(n, w_gu, w_gu)


def _ffn_dact(dh, w_dn, gate, up, name):
    rows, d = dh.shape
    f = w_dn.shape[0]
    tm = _pick(rows, (640, 512, 384, 256, 128))
    tn = _pick(f, (1408, 1024, 512, 384, 256, 128))

    def body(dh_r, w_r, g_r, u_r, dg_r, du_r):
        dact = _bdot(dh_r[...], w_r[...], "nt")
        _, vjp = jax.vjp(_swiglu, g_r[...].astype(f32), u_r[...].astype(f32))
        dg, du = vjp(dact)
        dg_r[...] = dg.astype(dg_r.dtype)
        du_r[...] = du.astype(du_r.dtype)

    t_spec = pl.BlockSpec((tm, tn), lambda j, i: (i, j))
    return pl.pallas_call(
        body, name=name, grid=(f // tn, rows // tm), out_shape=[jax.ShapeDtypeStruct((rows, f), bf16)] * 2,
        in_specs=[pl.BlockSpec((tm, d), lambda j, i: (i, 0)), pl.BlockSpec((tn, d), lambda j, i: (j, 0)), t_spec, t_spec],
        out_specs=[t_spec] * 2, compiler_params=_params(("parallel", "parallel")),
    )(dh, w_dn, gate, up)


def _loss_head(h, g, tgt, lo, hi, name):
    rows, d = h.shape
    tm = _pick(rows, (640, 512, 384, 256, 128))

    def fn(i, hv, gv, tv):
        mask = _row_mask(i, tm, lo, hi, (tm, 1))

        def f(hh, gg):
            err = _rms(hh, gg) - tv
            per_row = jnp.where(mask, jnp.mean(err * err, axis=-1, keepdims=True), 0.0)
            return 0.5 * jnp.sum(per_row, axis=0, keepdims=True)

        loss, vjp = jax.vjp(f, hv, gv)
        dh, dg = vjp(jnp.ones_like(loss))
        return dh, dg, jnp.broadcast_to(loss, (1, LANES))

    return _rowwise(fn, name, rows, tm, [(h, (d, 0)), (g, None), (tgt, (d, 0))], [(d, f32)], [(1, d), (1, LANES)])


def _heads_l2(x):
    t = x.shape[0]
    x3 = x.reshape(t, GDN_HEADS, GDN_DIM)
    return (x3 * lax.rsqrt(jnp.sum(x3 * x3, axis=-1, keepdims=True) + EPS)).reshape(t, GDN_WIDTH)


def _gdn_act(conv, a_in, b_in, a_log, dt_bias, mask):
    s = jax.nn.silu(conv)
    q = _heads_l2(s[:, :GDN_WIDTH])
    k = _heads_l2(s[:, GDN_WIDTH:2 * GDN_WIDTH])
    v = s[:, 2 * GDN_WIDTH:]
    g = jnp.where(mask, -jnp.exp(a_log) * jax.nn.softplus(a_in + dt_bias), 0.0)
    beta = jnp.where(mask, jax.nn.sigmoid(b_in), 0.0)
    return q, k, v, g, beta


def _widen(x8):
    return [jnp.broadcast_to(x8[:, h:h + 1], (x8.shape[0], GDN_DIM)) for h in range(GDN_HEADS)]


def _narrow(per_head):
    t = per_head[0].shape[0]
    lane = lax.broadcasted_iota(i32, (t, LANES), 1)
    out = jnp.zeros((t, LANES), f32)
    for h, x in enumerate(per_head):
        out = out + jnp.where(lane == h, jnp.sum(x, axis=1, keepdims=True), 0.0)
    return out


def _conv_taps(cur, prev8, w):
    tm = cur.shape[0]
    cat = jnp.concatenate([prev8, cur], axis=0)
    y = cur * w[CONV_WIDTH - 1:CONV_WIDTH, :]
    for j in range(1, CONV_WIDTH):
        y = y + pltpu.roll(cat, j, axis=0)[8:8 + tm, :] * w[CONV_WIDTH - 1 - j:CONV_WIDTH - j, :]
    return y


def _gdn_prep_specs(proj, tm):
    c3 = 3 * GDN_WIDTH
    ab = 4 * GDN_WIDTH // LANES
    t8 = tm // 8
    return [
        pl.BlockSpec((tm, c3), lambda i: (i, 0)),
        pl.BlockSpec((8, c3), lambda i: (jnp.maximum(i * t8 - 1, 0), 0)),
        pl.BlockSpec((tm, LANES), lambda i: (i, ab)),
        pl.BlockSpec((tm, LANES), lambda i: (i, ab + 1)),
    ]


def _full(arr):
    return pl.BlockSpec(arr.shape, lambda i, nd=arr.ndim: (0,) * nd)


def _gdn_prep_fwd(proj, conv_w, a_log, dt_bias, ltri, lo, hi, tm):
    rows = proj.shape[0]

    def body(cur, prev8, a_in, b_in, w, al, dtb, lt, q_o, k_o, v_o, g_o, b_o):
        i = pl.program_id(0)
        mask = _row_mask(i, tm, lo, hi, (tm, LANES)) & (lax.broadcasted_iota(i32, (tm, LANES), 1) < GDN_HEADS)
        conv = _conv_taps(cur[...], prev8[...], w[...])
        q, k, v, g, beta = _gdn_act(conv, a_in[...], b_in[...], al[...], dtb[...], mask)
        q_o[...] = q
        k_o[...] = k
        v_o[...] = v
        gcum = jnp.dot(lt[...], g, preferred_element_type=f32, precision=lax.Precision.HIGHEST)
        g_o[...] = gcum
        b_o[...] = beta

    wide = jax.ShapeDtypeStruct((rows, GDN_WIDTH), f32)
    narrow = jax.ShapeDtypeStruct((rows, LANES), f32)
    o_spec = pl.BlockSpec((tm, GDN_WIDTH), lambda i: (i, 0))
    n_spec = pl.BlockSpec((tm, LANES), lambda i: (i, 0))
    return pl.pallas_call(
        body, name="gdn_prep_fwd", out_shape=[wide] * 3 + [narrow] * 2, grid=(rows // tm,),
        in_specs=_gdn_prep_specs(proj, tm) + [_full(conv_w), _full(a_log), _full(dt_bias), _full(ltri)],
        out_specs=[o_spec] * 3 + [n_spec] * 2, compiler_params=_params(("parallel",)),
    )(proj, proj, proj, proj, conv_w, a_log, dt_bias, ltri)


def _gdn_prep_bwd_act(proj, conv_w, a_log, dt_bias, ltri, dq, dk, dv, dgw, dbw, lo, hi, tm):
    rows = proj.shape[0]
    c3 = 3 * GDN_WIDTH

    def body(cur, prev8, a_in, b_in, w, al, dtb, lt, dq_r, dk_r, dv_r, dg_r, db_r, dconv_o, da_o, dbin_o, dal_o, ddt_o):
        i = pl.program_id(0)
        mask = _row_mask(i, tm, lo, hi, (tm, LANES)) & (lax.broadcasted_iota(i32, (tm, LANES), 1) < GDN_HEADS)
        conv = _conv_taps(cur[...], prev8[...], w[...])
        dg = lax.dot_general(lt[...], dg_r[...], (((0,), (0,)), ((), ())), preferred_element_type=f32,
                             precision=lax.Precision.HIGHEST)
        dbeta = db_r[...]
        _, vjp = jax.vjp(lambda c, a, b, x, y: _gdn_act(c, a, b, x, y, mask), conv, a_in[...], b_in[...], al[...], dtb[...])
        dconv, da, dbin, dal, ddt = vjp((dq_r[...], dk_r[...], dv_r[...], dg, dbeta))
        dconv_o[...] = dconv
        da_o[...] = da
        dbin_o[...] = dbin

        @pl.when(i == 0)
        def _():
            dal_o[...] = jnp.zeros_like(dal_o)
            ddt_o[...] = jnp.zeros_like(ddt_o)

        dal_o[...] += dal
        ddt_o[...] += ddt

    w_spec = pl.BlockSpec((tm, GDN_WIDTH), lambda i: (i, 0))
    n_spec = pl.BlockSpec((tm, LANES), lambda i: (i, 0))
    s_spec = pl.BlockSpec((1, LANES), lambda i: (0, 0))
    return pl.pallas_call(
        body, name="gdn_prep_bwd_act",
        out_shape=[jax.ShapeDtypeStruct((rows, c3), f32), jax.ShapeDtypeStruct((rows, LANES), f32),
                   jax.ShapeDtypeStruct((rows, LANES), f32), jax.ShapeDtypeStruct((1, LANES), f32),
                   jax.ShapeDtypeStruct((1, LANES), f32)],
        grid=(rows // tm,),
        in_specs=_gdn_prep_specs(proj, tm) + [_full(conv_w), _full(a_log), _full(dt_bias), _full(ltri)] + [w_spec] * 3 + [n_spec] * 2,
        out_specs=[pl.BlockSpec((tm, c3), lambda i: (i, 0)), n_spec, n_spec, s_spec, s_spec],
        compiler_params=_params(("arbitrary",)),
    )(proj, proj, proj, proj, conv_w, a_log, dt_bias, ltri, dq, dk, dv, dgw, dbw)


def _gdn_prep_bwd_conv(proj, conv_w, dconv, dgate, da, dbin, tm):
    rows, width = proj.shape
    c3 = 3 * GDN_WIDTH
    t8 = tm // 8
    nt = rows // tm

    def body(cur, prev8, w, dc, dnext8, dgt, da_r, db_r, dp_o, dw_o):
        i = pl.program_id(0)
        d = dc[...]
        nxt = jnp.where(i == nt - 1, 0.0, dnext8[...])
        cat = jnp.concatenate([d, nxt], axis=0)
        wv = w[...]
        dx = d * wv[CONV_WIDTH - 1:CONV_WIDTH, :]
        for j in range(1, CONV_WIDTH):
            dx = dx + pltpu.roll(cat, tm + 8 - j, axis=0)[:tm, :] * wv[CONV_WIDTH - 1 - j:CONV_WIDTH - j, :]
        dp_o[:, :c3] = dx.astype(bf16)
        dp_o[:, c3:4 * GDN_WIDTH] = dgt[...].astype(bf16)
        dp_o[:, 4 * GDN_WIDTH:4 * GDN_WIDTH + LANES] = da_r[...].astype(bf16)
        dp_o[:, 4 * GDN_WIDTH + LANES:] = db_r[...].astype(bf16)

        xcat = jnp.concatenate([prev8[...], cur[...]], axis=0)
        parts = [jnp.sum(d * cur[...], axis=0, keepdims=True)]
        for j in range(1, CONV_WIDTH):
            parts.append(jnp.sum(d * pltpu.roll(xcat, j, axis=0)[8:8 + tm, :], axis=0, keepdims=True))
        dwt = jnp.concatenate(parts[::-1], axis=0)

        @pl.when(i == 0)
        def _():
            dw_o[...] = jnp.zeros_like(dw_o)

        dw_o[...] += dwt

    n_spec = pl.BlockSpec((tm, LANES), lambda i: (i, 0))
    return pl.pallas_call(
        body, name="gdn_prep_bwd_conv",
        out_shape=[jax.ShapeDtypeStruct((rows, width), bf16), jax.ShapeDtypeStruct((CONV_WIDTH, c3), f32)],
        grid=(nt,),
        in_specs=[pl.BlockSpec((tm, c3), lambda i: (i, 0)),
                  pl.BlockSpec((8, c3), lambda i: (jnp.maximum(i * t8 - 1, 0), 0)),
                  _full(conv_w),
                  pl.BlockSpec((tm, c3), lambda i: (i, 0)),
                  pl.BlockSpec((8, c3), lambda i: (jnp.minimum((i + 1) * t8, rows // 8 - 1), 0)),
                  pl.BlockSpec((tm, GDN_WIDTH), lambda i: (i, 0)), n_spec, n_spec],
        out_specs=[pl.BlockSpec((tm, width), lambda i: (i, 0)), pl.BlockSpec((CONV_WIDTH, c3), lambda i: (0, 0))],
        compiler_params=_params(("arbitrary",)),
    )(proj, proj, conv_w, dconv, dconv, dgate, da, dbin)


def _split(a):
    hi = a.astype(bf16)
    return hi, (a - hi.astype(f32)).astype(bf16)


def _make_mm(dot):
    @jax.custom_vjp
    def nn(a, b):
        return dot(a, b, "nn")

    nn.defvjp(lambda a, b: (dot(a, b, "nn"), (a, b)),
              lambda r, ct: (dot(ct, r[1], "nt"), dot(r[0], ct, "tn")))

    @jax.custom_vjp
    def nt(a, b):
        return dot(a, b, "nt")

    nt.defvjp(lambda a, b: (dot(a, b, "nt"), (a, b)),
              lambda r, ct: (dot(ct, r[1], "nn"), dot(ct, r[0], "tn")))

    @jax.custom_vjp
    def tn(a, b):
        return dot(a, b, "tn")

    tn.defvjp(lambda a, b: (dot(a, b, "tn"), (a, b)),
              lambda r, ct: (dot(r[1], ct, "nt"), dot(r[0], ct, "nn")))
    return nn, nt, tn


_mm, _mm_nt, _mm_tn = _make_mm(_bdot)


def _each(f, *lists):
    return [f(*xs) for xs in zip(*lists)]


def _gdn_chunk(q, k, v, gcb, bcb, s_in):
    c = q[0].shape[0]
    ri = lax.broadcasted_iota(i32, (c, c), 0)
    ci = lax.broadcasted_iota(i32, (c, c), 1)
    incl, strict = ri >= ci, ri > ci
    rowi = lax.broadcasted_iota(i32, gcb[0].shape, 0)
    qs = _each(lambda t: t * (GDN_DIM ** -0.5), q)
    decay = _each(lambda g: jnp.where(incl, jnp.exp(jnp.where(incl, g[:, :c] - g[:, :c].T, 0.0)), 0.0), gcb)
    kk = _each(lambda t: _mm_nt(t, t), k)
    a1 = _each(lambda b, d, t: jnp.where(strict, b[:, :c] * d * t, 0.0), bcb, decay, kk)
    eg = _each(jnp.exp, gcb)
    x = _each(lambda b, vv, e, t: jnp.concatenate([b * vv, (b * e) * t], axis=1), bcb, v, eg, k)
    pows = [a1]
    for _ in range(5):
        pows.append(_each(lambda p: _mm(p, p), pows[-1]))
    for ps in pows[:0:-1]:
        x = _each(lambda p, t: t + _mm(p, t), ps, x)
    x = _each(lambda p, t: t - _mm(p, t), a1, x)
    attn = _each(lambda a, b, d: _mm_nt(a, b) * d, qs, k, decay)
    glast = _each(lambda g: jnp.sum(jnp.where(rowi == c - 1, g, 0.0), axis=0, keepdims=True), gcb)
    u = _each(lambda t, s: t[:, :GDN_DIM] - _mm(t[:, GDN_DIM:], s), x, s_in)
    o = _each(lambda a, e, s, w, uu: _mm(a * e, s) + _mm(w, uu), qs, eg, s_in, attn, u)
    s_out = _each(lambda s, gl, t, g, uu: s * jnp.exp(gl) + _mm_tn(t * jnp.exp(gl - g), uu), s_in, glast, k, gcb, u)
    return o, s_out


def _gdn_heads(ref):
    return [ref[:, h * GDN_DIM:(h + 1) * GDN_DIM] for h in range(GDN_HEADS)]


def _gdn_fwd(q, k, v, gw, bw, shards):
    rows = q.shape[0]
    nc = rows // CHUNK
    num = len(shards)
    blk = pl.BlockSpec((CHUNK, GDN_WIDTH), lambda c: (c, 0))

    def body(*refs):
        q_r, k_r, v_r, g_r, b_r = refs[:5]
        ins = refs[5:5 + num]
        o_r, st_r = refs[5 + num:7 + num]
        outs = refs[7 + num:7 + 2 * num]
        s_sc, send_sems, recv_sems, local_sems = refs[7 + 2 * num:]
        c = pl.program_id(0)
        start, forward, finish = _gather_plan(ins, outs, send_sems, recv_sems, local_sems)

        @pl.when(c == 0)
        def _():
            s_sc[...] = jnp.zeros_like(s_sc)
            start()

        s_in = [s_sc[h] for h in range(GDN_HEADS)]
        st_r[0] = s_sc[...]
        o, s_out = _gdn_chunk(_gdn_heads(q_r), _gdn_heads(k_r), _gdn_heads(v_r), _widen(g_r[...]), _widen(b_r[...]), s_in)
        o_r[...] = jnp.concatenate(o, axis=1)
        for h in range(GDN_HEADS):
            s_sc[h] = s_out[h]
        pl.when(c == nc // 2)(forward)
        pl.when(c == nc - 1)(finish)

    res = pl.pallas_call(
        body, name="gdn_fwd",
        out_shape=[jax.ShapeDtypeStruct((rows, GDN_WIDTH), f32), jax.ShapeDtypeStruct((nc, GDN_HEADS, GDN_DIM, GDN_DIM), f32)]
        + [jax.ShapeDtypeStruct((N_CHIPS,) + t.shape, t.dtype) for t in shards],
        grid=(nc,), in_specs=[blk] * 3 + [pl.BlockSpec((CHUNK, LANES), lambda c: (c, 0))] * 2 + [_ANY] * num,
        out_specs=[blk, pl.BlockSpec((1, GDN_HEADS, GDN_DIM, GDN_DIM), lambda c: (c, 0, 0, 0))] + [_ANY] * num,
        scratch_shapes=[pltpu.VMEM((GDN_HEADS, GDN_DIM, GDN_DIM), f32), pltpu.SemaphoreType.DMA((6 * num,)),
                        pltpu.SemaphoreType.DMA((6 * num,)), pltpu.SemaphoreType.DMA((num,))],
        compiler_params=_params(("arbitrary",)),
    )(q, k, v, gw, bw, *shards)
    return res[0], res[1], res[2:]


def _gdn_bwd(q, k, v, gw, bw, states, do, parts):
    rows = q.shape[0]
    nc = rows // CHUNK
    num = len(parts)
    blk = pl.BlockSpec((CHUNK, GDN_WIDTH), lambda c: (nc - 1 - c, 0))

    def body(*refs):
        q_r, k_r, v_r, g_r, b_r, st_r, do_r = refs[:7]
        ins = refs[7:7 + num]
        dq_r, dk_r, dv_r, dg_r, db_r = refs[7 + num:12 + num]
        outs = refs[12 + num:12 + 2 * num]
        ds_sc, send_sems, recv_sems = refs[12 + 2 * num:]
        c = pl.program_id(0)
        start, finish = _scatter_plan(ins, outs, send_sems, recv_sems)

        @pl.when(c == 0)
        def _():
            ds_sc[...] = jnp.zeros_like(ds_sc)
            start()

        s_in = [st_r[0, h] for h in range(GDN_HEADS)]
        _, vjp = jax.vjp(_gdn_chunk, _gdn_heads(q_r), _gdn_heads(k_r), _gdn_heads(v_r), _widen(g_r[...]), _widen(b_r[...]), s_in)
        dq, dk, dv, dg, db, ds_in = vjp((_gdn_heads(do_r), [ds_sc[h] for h in range(GDN_HEADS)]))
        dq_r[...] = jnp.concatenate(dq, axis=1)
        dk_r[...] = jnp.concatenate(dk, axis=1)
        dv_r[...] = jnp.concatenate(dv, axis=1)
        dg_r[...] = _narrow(dg)
        db_r[...] = _narrow(db)
        for h in range(GDN_HEADS):
            ds_sc[h] = ds_in[h]
        pl.when(c == nc - 1)(finish)

    wide = jax.ShapeDtypeStruct((rows, GDN_WIDTH), f32)
    narrow = jax.ShapeDtypeStruct((rows, LANES), f32)
    nblk = pl.BlockSpec((CHUNK, LANES), lambda c: (nc - 1 - c, 0))
    res = pl.pallas_call(
        body, name="gdn_bwd",
        out_shape=[wide] * 3 + [narrow] * 2 + [jax.ShapeDtypeStruct((3,) + t.shape[1:], t.dtype) for t in parts],
        grid=(nc,),
        in_specs=[blk] * 3 + [nblk] * 2
        + [pl.BlockSpec((1, GDN_HEADS, GDN_DIM, GDN_DIM), lambda c: (nc - 1 - c, 0, 0, 0)), blk] + [_ANY] * num,
        out_specs=[blk] * 3 + [nblk] * 2 + [_ANY] * num,
        scratch_shapes=[pltpu.VMEM((GDN_HEADS, GDN_DIM, GDN_DIM), f32), pltpu.SemaphoreType.DMA((3 * num,)),
                        pltpu.SemaphoreType.DMA((3 * num,))],
        compiler_params=_params(("arbitrary",)),
    )(q, k, v, gw, bw, states, do, *parts)
    return res[:5], res[5:]


def _gdn_gate(o, gate, og):
    t = o.shape[0]
    o3 = o.reshape(t, GDN_HEADS, GDN_DIM)
    n = o3 * lax.rsqrt(jnp.mean(o3 * o3, axis=-1, keepdims=True) + EPS) * og.reshape(1, 1, GDN_DIM)
    return n.reshape(t, GDN_WIDTH) * jax.nn.silu(gate)


def _gdn_gate_fwd(o, proj, og, tm):
    rows = o.shape[0]
    return _rowwise(lambda i, ov, gv, w: (_gdn_gate(ov, gv, w),), "gdn_gate_fwd", rows, tm,
                    [(o, (GDN_WIDTH, 0)), (proj, (GDN_WIDTH, 3)), (og, None)], [(GDN_WIDTH, bf16)])[0]


def _gdn_gate_bwd(o, proj, og, dy, tm):
    rows = o.shape[0]

    def fn(i, ov, gv, w, d):
        _, vjp = jax.vjp(_gdn_gate, ov, gv, w)
        return vjp(d)

    return _rowwise(fn, "gdn_gate_bwd", rows, tm,
                    [(o, (GDN_WIDTH, 0)), (proj, (GDN_WIDTH, 3)), (og, None), (dy, (GDN_WIDTH, 0))],
                    [(GDN_WIDTH, f32), (GDN_WIDTH, f32)], [(1, GDN_DIM)])


def _sb_visible(i, j, valid):
    qpos = i * SB_BLOCK + lax.broadcasted_iota(i32, (SB_BLOCK, SB_BLOCK), 0)
    kpos = j * SB_BLOCK + lax.broadcasted_iota(i32, (SB_BLOCK, SB_BLOCK), 1)
    return (kpos < qpos) & (kpos >= FRONT) & valid


def _sb_logs(z, vis):
    l1p = jnp.log(1.0 + jnp.exp(-jnp.abs(z)))
    return -(jnp.maximum(-z, 0.0) + l1p), jnp.where(vis, -(jnp.maximum(z, 0.0) + l1p), 0.0)


def _tri_sum(x, tri):
    hi, lo = _split(x)
    return jnp.dot(hi, tri, preferred_element_type=f32) + jnp.dot(lo, tri, preferred_element_type=f32)


def _sb_live(t, i, runs):
    return (t <= i) & (jnp.max(functools.reduce(jnp.maximum, runs)) > -SB_UNDERFLOW)


def _sb_blocks(i, t, nb):
    js = [i - t - b for b in range(nb)]
    kss = [pl.ds(pl.multiple_of(jnp.maximum(j, 0) * SB_BLOCK, SB_BLOCK), SB_BLOCK) for j in js]
    return kss, [_sb_visible(i, j, j >= 0) for j in js]


def _sb_weights(i, t, nb, qs, sls, k_r, runs, after, scale):
    nh = len(qs)
    kss, vis = _sb_blocks(i, t, nb)
    units = [(a, b) for b in range(nb) for a in range(nh)]
    z = [_bdot(qs[a], k_r[kss[b], sls[a]], "nt") * scale for a, b in units]
    logs = [_sb_logs(zz, vis[b]) for zz, (a, b) in zip(z, units)]
    later = [_tri_sum(l[1], after) for l in logs]
    sums = [jnp.sum(l[1], axis=1, keepdims=True) for l in logs]
    w = []
    runs = list(runs)
    for b in range(nb):
        for a in range(nh):
            u = b * nh + a
            w.append(jnp.where(vis[b], jnp.exp(logs[u][0] + later[u] + runs[a]), 0.0))
        runs = [runs[a] + sums[b * nh + a] for a in range(nh)]
    return kss, vis, units, logs, w, tuple(runs)


def _sb_fwd(q, kv, width):
    rows = q.shape[0]
    nq = rows // SB_BLOCK
    lanes = SB_FWD_HEADS * SB_DIM
    npair = width // lanes
    scale = SB_DIM ** -0.5

    def body(q_r, k_r, v_r, o_r):
        i = pl.program_id(1)
        rj = lax.broadcasted_iota(i32, (SB_BLOCK, SB_BLOCK), 0)
        cs = lax.broadcasted_iota(i32, (SB_BLOCK, SB_BLOCK), 1)
        after = (rj > cs).astype(bf16)
        sls = [slice(a * SB_DIM, (a + 1) * SB_DIM) for a in range(SB_FWD_HEADS)]
        qs = [q_r[:, sl] for sl in sls]

        def step(carry, nb):
            t, accs, runs = carry
            kss, _, units, _, w, runs = _sb_weights(i, t, nb, qs, sls, k_r, runs, after, scale)
            prods = [_bdot(ww, v_r[kss[b], sls[a]], "nn") for ww, (a, b) in zip(w, units)]
            accs = tuple(functools.reduce(jnp.add, [accs[a]] + prods[a::SB_FWD_HEADS]) for a in range(SB_FWD_HEADS))
            return t + nb, accs, runs

        init = (jnp.int32(0), tuple(jnp.zeros((SB_BLOCK, SB_DIM), f32) for _ in sls),
                tuple(jnp.zeros((SB_BLOCK, 1), f32) for _ in sls))
        _, accs, _ = lax.while_loop(lambda c: _sb_live(c[0], i, c[2]), lambda c: step(c, 2), step(init, SB_FIRST))
        o_r[...] = jnp.concatenate(accs, axis=1)

    return pl.pallas_call(
        body, name="sb_fwd", out_shape=jax.ShapeDtypeStruct((rows, width), f32), grid=(npair, nq),
        in_specs=[pl.BlockSpec((SB_BLOCK, lanes), lambda p, i: (i, p)),
                  pl.BlockSpec((rows, lanes), lambda p, i: (0, p)),
                  pl.BlockSpec((rows, lanes), lambda p, i: (0, npair + p))],
        out_specs=pl.BlockSpec((SB_BLOCK, lanes), lambda p, i: (i, p)),
        compiler_params=_params(("parallel", "arbitrary")),
    )(q, kv, kv)


def _sb_bwd(q, kv, do, width):
    rows = q.shape[0]
    nq = rows // SB_BLOCK
    npair = width // LANES
    nh = LANES // SB_DIM
    scale = SB_DIM ** -0.5

    def body(q_r, k_r, v_r, do_r, dq_r, dk_r, dv_r, e_sc, sig_sc, w_sc):
        i = pl.program_id(1)

        @pl.when(i == 0)
        def _():
            dk_r[...] = jnp.zeros_like(dk_r)
            dv_r[...] = jnp.zeros_like(dv_r)

        rj = lax.broadcasted_iota(i32, (SB_BLOCK, SB_BLOCK), 0)
        cs = lax.broadcasted_iota(i32, (SB_BLOCK, SB_BLOCK), 1)
        after = (rj > cs).astype(bf16)
        from_s = (rj >= cs).astype(bf16)
        zero1 = jnp.zeros((SB_BLOCK, 1), f32)
        sls = [slice(a * SB_DIM, (a + 1) * SB_DIM) for a in range(nh)]
        qs = [q_r[:, sl] for sl in sls]
        dos = [do_r[:, sl] for sl in sls]

        def weigh(carry, nb):
            t, runs, eruns = carry
            kss, _, units, logs, w, runs = _sb_weights(i, t, nb, qs, sls, k_r, runs, after, scale)
            dw = [_bdot(dos[a], v_r[kss[b], sls[a]], "nt") for a, b in units]
            e = [ww * d for ww, d in zip(w, dw)]
            for u, (a, b) in enumerate(units):
                e_sc[a, t + b] = e[u]
                sig_sc[a, t + b] = jnp.exp(logs[u][0])
                w_sc[a, t + b] = w[u].astype(w_sc.dtype)
            sums = [jnp.sum(ee, axis=1, keepdims=True) for ee in e]
            eruns = tuple(functools.reduce(jnp.add, [eruns[a]] + sums[a::nh]) for a in range(nh))
            return t + nb, runs, eruns

        n_blk, _, etots = lax.while_loop(lambda c: _sb_live(c[0], i, c[1]), lambda c: weigh(c, 2),
                                         weigh((jnp.int32(0), (zero1,) * nh, (zero1,) * nh), SB_FIRST))

        def push(t, carry, nb):
            dqs, eruns = carry
            kss, vis = _sb_blocks(i, t, nb)
            units = [(a, b) for b in range(nb) for a in range(nh)]
            e = [e_sc[a, t + b] for a, b in units]
            dvs = [_bdot(w_sc[a, t + b], dos[a], "tn") for a, b in units]
            upto = [_tri_sum(ee, from_s) for ee in e]
            sums = [jnp.sum(ee, axis=1, keepdims=True) for ee in e]
            dz = []
            eruns = list(eruns)
            for b in range(nb):
                for a in range(nh):
                    u = b * nh + a
                    sig = sig_sc[a, t + b]
                    before = etots[a] - eruns[a] - upto[u]
                    dz.append(jnp.where(vis[b], e[u] * (1.0 - sig) - before * sig, 0.0) * scale)
                eruns = [eruns[a] + sums[b * nh + a] for a in range(nh)]
            dks = [_bdot(d, qs[a], "tn") for d, (a, b) in zip(dz, units)]
            dqp = [_bdot(d, k_r[kss[b], sls[a]], "nn") for d, (a, b) in zip(dz, units)]
            for b in range(nb):
                dk_r[kss[b], :] += jnp.concatenate(dks[b * nh:(b + 1) * nh], axis=1)
                dv_r[kss[b], :] += jnp.concatenate(dvs[b * nh:(b + 1) * nh], axis=1)
            dqs = tuple(functools.reduce(jnp.add, [dqs[a]] + dqp[a::nh]) for a in range(nh))
            return dqs, tuple(eruns)

        first = push(jnp.int32(0), (tuple(jnp.zeros((SB_BLOCK, SB_DIM), f32) for _ in sls), (zero1,) * nh), SB_FIRST)
        dqs, _ = lax.fori_loop(0, (n_blk - SB_FIRST) // 2, lambda p, c: push(SB_FIRST + 2 * p, c, 2), first)
        dq_r[...] = jnp.concatenate(dqs, axis=1)

    blk = pl.BlockSpec((SB_BLOCK, LANES), lambda p, i: (i, p))
    col = pl.BlockSpec((rows, LANES), lambda p, i: (0, p))
    wide = jax.ShapeDtypeStruct((rows, width), f32)
    depth = nq + SB_FIRST
    return pl.pallas_call(
        body, name="sb_bwd", out_shape=[wide] * 3, grid=(npair, nq),
        in_specs=[blk, col, pl.BlockSpec((rows, LANES), lambda p, i: (0, npair + p)), blk],
        out_specs=[blk, col, col],
        scratch_shapes=[pltpu.VMEM((nh, depth, SB_BLOCK, SB_BLOCK), f32), pltpu.VMEM((nh, depth, SB_BLOCK, SB_BLOCK), f32),
                        pltpu.VMEM((nh, depth, SB_BLOCK, SB_BLOCK), bf16)],
        compiler_params=_params(("parallel", "arbitrary")),
    )(q, kv, kv, do)


_FLIPS = ((1, 0), (0, 1), (1, 1))
_ANY = pl.BlockSpec(memory_space=pl.ANY)


def _flip(v, a):
    return v + a - 2 * a * v


def _gather_plan(ins, outs, send_sems, recv_sems, local_sems):
    num = len(ins)
    x, y, c = lax.axis_index("x"), lax.axis_index("y"), lax.axis_index("c")
    me, sibling = (x, y, c), (x, y, 1 - c)
    chip = 2 * x + y
    others = [(_flip(x, a), _flip(y, b)) for a, b in _FLIPS]
    pairs = [(k, n, 2 * ox + oy) for k in range(num) for n, (ox, oy) in enumerate(others)]

    def half_of(ref, hc):
        half = ref.shape[0] // 2
        start = hc * half
        for align in (16, 8):
            if half % align == 0:
                start = pl.multiple_of(start, align)
                break
        return ref.at[pl.ds(start, half)]

    def copy(k, n, s, hc, to, src=None):
        dst = half_of(outs[k].at[s], hc)
        return pltpu.make_async_remote_copy(
            src_ref=dst if src is None else src, dst_ref=dst,
            send_sem=send_sems.at[6 * k + n], recv_sem=recv_sems.at[6 * k + n], device_id=to, device_id_type=MESH)

    mine = [pltpu.make_async_copy(ins[k], outs[k].at[chip], local_sems.at[k]) for k in range(num)]
    first = [copy(k, n, chip, c, (others[n][0], others[n][1], c), src=half_of(ins[k], c)) for k, n, _ in pairs]
    passed = [copy(k, 3 + n, s, c, sibling) for k, n, s in pairs]

    def start():
        for cp in mine + first:
            cp.start()

    def forward():
        for (k, n, s), fw in zip(pairs, passed):
            copy(k, n, s, c, me).wait_recv()
            fw.start()

    def finish():
        for k, n, s in pairs:
            copy(k, 3 + n, s, 1 - c, me).wait_recv()
        for cp in first + passed:
            cp.wait_send()
        for cp in mine:
            cp.wait()

    return start, forward, finish


def _gather_chips(shards):
    num = len(shards)

    def body(*refs):
        for phase in _gather_plan(refs[:num], refs[num:2 * num], *refs[2 * num:]):
            phase()

    return pl.pallas_call(
        body, name="gather_chips", out_shape=[jax.ShapeDtypeStruct((N_CHIPS,) + t.shape, t.dtype) for t in shards],
        in_specs=[_ANY] * num, out_specs=[_ANY] * num,
        scratch_shapes=[pltpu.SemaphoreType.DMA((6 * num,)), pltpu.SemaphoreType.DMA((6 * num,)),
                        pltpu.SemaphoreType.DMA((num,))],
    )(*shards)


def _scatter_plan(ins, outs, send_sems, recv_sems):
    x, y, c = lax.axis_index("x"), lax.axis_index("y"), lax.axis_index("c")
    cps = []
    for k in range(len(ins)):
        for n, (a, b) in enumerate(_FLIPS):
            ox, oy = _flip(x, a), _flip(y, b)
            cps.append(pltpu.make_async_remote_copy(
                src_ref=ins[k].at[2 * ox + oy], dst_ref=outs[k].at[n], send_sem=send_sems.at[3 * k + n],
                recv_sem=recv_sems.at[3 * k + n], device_id=(ox, oy, c), device_id_type=MESH))

    def start():
        for cp in cps:
            cp.start()

    def finish():
        for cp in cps:
            cp.wait()

    return start, finish


def _scatter_chips(parts):
    num = len(parts)

    def body(*refs):
        for phase in _scatter_plan(refs[:num], refs[num:2 * num], *refs[2 * num:]):
            phase()

    return pl.pallas_call(
        body, name="scatter_chips", out_shape=[jax.ShapeDtypeStruct((3,) + t.shape[1:], t.dtype) for t in parts],
        in_specs=[_ANY] * num, out_specs=[_ANY] * num,
        scratch_shapes=[pltpu.SemaphoreType.DMA((3 * num,)), pltpu.SemaphoreType.DMA((3 * num,))],
    )(*parts)


def _swap_sibling(arrs):
    num = len(arrs)

    def body(*refs):
        ins, outs = refs[:num], refs[num:2 * num]
        send_sems, recv_sems = refs[2 * num:]
        x, y, c = lax.axis_index("x"), lax.axis_index("y"), lax.axis_index("c")
        cps = [pltpu.make_async_remote_copy(src_ref=ins[k], dst_ref=outs[k], send_sem=send_sems.at[k],
                                            recv_sem=recv_sems.at[k], device_id=(x, y, 1 - c), device_id_type=MESH)
               for k in range(num)]
        for cp in cps:
            cp.start()
        for cp in cps:
            cp.wait()

    return pl.pallas_call(
        body, name="swap_sibling", out_shape=[jax.ShapeDtypeStruct(t.shape, t.dtype) for t in arrs],
        in_specs=[_ANY] * num, out_specs=[_ANY] * num,
        scratch_shapes=[pltpu.SemaphoreType.DMA((num,)), pltpu.SemaphoreType.DMA((num,))],
    )(*arrs)


def _gather_all(v):
    m_per, n = v.shape

    def body(x_ref, out_ref, send_sems, recv_sems, local_sem):
        x, y, c = lax.axis_index("x"), lax.axis_index("y"), lax.axis_index("c")
        me, sibling = (x, y, c), (x, y, 1 - c)
        chips = [(_flip(x, a), _flip(y, b)) for a, b in _FLIPS]

        def rows(px, py, pc):
            return out_ref.at[pl.ds(pl.multiple_of((4 * px + 2 * py + pc) * m_per, 8), m_per), :]

        def copy(k, block, to, src=None):
            return pltpu.make_async_remote_copy(
                src_ref=rows(*block) if src is None else src, dst_ref=rows(*block),
                send_sem=send_sems.at[k], recv_sem=recv_sems.at[k], device_id=to, device_id_type=MESH)

        mine = pltpu.make_async_copy(x_ref, rows(*me), local_sem)
        mine.start()
        first = [copy(0, me, sibling, src=x_ref)]
        first += [copy(1 + j, me, (*chip, c), src=x_ref) for j, chip in enumerate(chips)]
        for cp in first:
            cp.start()
        passed = [copy(4 + j, (*chip, c), sibling) for j, chip in enumerate(chips)]
        for j, chip in enumerate(chips):
            copy(1 + j, (*chip, c), me).wait_recv()
            passed[j].start()
        copy(0, sibling, me).wait_recv()
        for j, chip in enumerate(chips):
            copy(4 + j, (*chip, 1 - c), me).wait_recv()
        for cp in first + passed:
            cp.wait_send()
        mine.wait()

    return pl.pallas_call(
        body, name="gather_all", out_shape=jax.ShapeDtypeStruct((N_DEV * m_per, n), v.dtype),
        in_specs=[pl.BlockSpec(memory_space=pltpu.VMEM)], out_specs=pl.BlockSpec(memory_space=pltpu.VMEM),
        scratch_shapes=[pltpu.SemaphoreType.DMA((7,)), pltpu.SemaphoreType.DMA((7,)), pltpu.SemaphoreType.DMA],
    )(v)


def _sum_chips(parts, got, chip, name):
    cols = parts.shape[-1]
    rows = parts.size // (N_CHIPS * cols)
    tm = _pick(rows, (256, 128, 64, 32, 16))

    def body(chip_r, own_r, got_r, o_r):
        acc = own_r[0]
        for n in range(3):
            acc = acc + got_r[n].astype(f32)
        o_r[...] = acc

    return pl.pallas_call(
        body, name=name, out_shape=jax.ShapeDtypeStruct((rows, cols), f32),
        grid_spec=pltpu.PrefetchScalarGridSpec(
            num_scalar_prefetch=1, grid=(rows // tm,),
            in_specs=[pl.BlockSpec((1, tm, cols), lambda i, s: (s[0], i, 0)),
                      pl.BlockSpec((3, tm, cols), lambda i, s: (0, i, 0))],
            out_specs=pl.BlockSpec((tm, cols), lambda i, s: (i, 0))),
        compiler_params=_params(("parallel",)),
    )(chip, parts.reshape(N_CHIPS, rows, cols), got.reshape(3, rows, cols))


def _sum_devices(g, m_per):
    n = g.shape[1]

    def body(g_r, o_r):
        acc = g_r[0:m_per, :]
        for d in range(1, N_DEV):
            acc = acc + g_r[d * m_per:(d + 1) * m_per, :]
        o_r[...] = acc

    return pl.pallas_call(body, name="sum_devices", out_shape=jax.ShapeDtypeStruct((m_per, n), f32))(g)


def _adamw(w, gs, m, v, name):
    shape = w.shape
    cols = shape[-1]
    rows = w.size // cols
    tm = _pick(rows, (256, 128, 64, 32, 16, 8)) if rows * cols * 4 > (1 << 20) else rows

    def fn(i, wv, mv, vv, *gv):
        g = functools.reduce(jnp.add, gv)
        mn = ADAM_B1 * mv + (1.0 - ADAM_B1) * g
        vn = ADAM_B2 * vv + (1.0 - ADAM_B2) * jnp.square(g)
        m_hat = mn / (1.0 - ADAM_B1 ** ADAM_STEP)
        v_hat = vn / (1.0 - ADAM_B2 ** ADAM_STEP)
        delta = -ADAM_LR * (m_hat / (jnp.sqrt(v_hat) + ADAM_EPS) + ADAM_WD * wv)
        return g, delta, mn, vn

    outs = _rowwise(fn, name, rows, tm, [(t.reshape(rows, cols), (cols, 0)) for t in (w, m, v) + tuple(gs)], [(cols, f32)] * 4)
    return tuple(o.reshape(shape) for o in outs)


def _pack(pieces, rows, dtype):
    flat = jnp.concatenate([p.reshape(-1).astype(dtype) for p in pieces])
    return jnp.pad(flat, (0, rows * PACK_COLS - flat.size)).reshape(rows, PACK_COLS)


def _unpack(buf, shapes):
    lead = buf.shape[:-2]
    flat = buf.reshape(lead + (-1,))
    out, off = [], 0
    for s in shapes:
        n = 1
        for d in s:
            n *= d
        out.append(flat[..., off:off + n].reshape(lead + tuple(s)))
        off += n
    return out


def _join_cols(t):
    return jnp.moveaxis(t, 0, -2).reshape(t.shape[1:-1] + (N_CHIPS * t.shape[-1],))


def _join_rows(t):
    return t.reshape((N_CHIPS * t.shape[1],) + t.shape[2:])


def _split_cols(t, parts=N_CHIPS):
    r, cols = t.shape
    return jnp.moveaxis(t.reshape(r, parts, cols // parts), 1, 0)


def _split_rows(t):
    return t.reshape((N_CHIPS, t.shape[0] // N_CHIPS) + t.shape[1:])


def kernel(x, meta_tokens, gdn_norm_g, gdn_w_in, gdn_conv_w, gdn_a_log, gdn_dt_bias, gdn_onorm_g, gdn_w_out, kv_norm_g, w_kv, sb_norm_g, sb_w_q, sb_w_o, ffn_norm_g, ffn_w_gate_up, ffn_w_down, final_norm_g, loss_target, m_meta_tokens, m_gdn_norm_g, m_gdn_w_in, m_gdn_conv_w, m_gdn_a_log, m_gdn_dt_bias, m_gdn_onorm_g, m_gdn_w_out, m_kv_norm_g, m_w_kv, m_sb_norm_g, m_sb_w_q, m_sb_w_o, m_ffn_norm_g, m_ffn_w_gate_up, m_ffn_w_down, m_final_norm_g, v_meta_tokens, v_gdn_norm_g, v_gdn_w_in, v_gdn_conv_w, v_gdn_a_log, v_gdn_dt_bias, v_gdn_onorm_g, v_gdn_w_out, v_kv_norm_g, v_w_kv, v_sb_norm_g, v_sb_w_q, v_sb_w_o, v_ffn_norm_g, v_ffn_w_gate_up, v_ffn_w_down, v_final_norm_g):
    weights = dict(meta_tokens=meta_tokens, gdn_norm_g=gdn_norm_g, gdn_w_in=gdn_w_in, gdn_conv_w=gdn_conv_w,
                   gdn_a_log=gdn_a_log, gdn_dt_bias=gdn_dt_bias, gdn_onorm_g=gdn_onorm_g, gdn_w_out=gdn_w_out,
                   kv_norm_g=kv_norm_g, w_kv=w_kv, sb_norm_g=sb_norm_g, sb_w_q=sb_w_q, sb_w_o=sb_w_o,
                   ffn_norm_g=ffn_norm_g, ffn_w_gate_up=ffn_w_gate_up, ffn_w_down=ffn_w_down, final_norm_g=final_norm_g)
    m_in = dict(meta_tokens=m_meta_tokens, gdn_norm_g=m_gdn_norm_g, gdn_w_in=m_gdn_w_in, gdn_conv_w=m_gdn_conv_w,
                gdn_a_log=m_gdn_a_log, gdn_dt_bias=m_gdn_dt_bias, gdn_onorm_g=m_gdn_onorm_g, gdn_w_out=m_gdn_w_out,
                kv_norm_g=m_kv_norm_g, w_kv=m_w_kv, sb_norm_g=m_sb_norm_g, sb_w_q=m_sb_w_q, sb_w_o=m_sb_w_o,
                ffn_norm_g=m_ffn_norm_g, ffn_w_gate_up=m_ffn_w_gate_up, ffn_w_down=m_ffn_w_down, final_norm_g=m_final_norm_g)
    v_in = dict(meta_tokens=v_meta_tokens, gdn_norm_g=v_gdn_norm_g, gdn_w_in=v_gdn_w_in, gdn_conv_w=v_gdn_conv_w,
                gdn_a_log=v_gdn_a_log, gdn_dt_bias=v_gdn_dt_bias, gdn_onorm_g=v_gdn_onorm_g, gdn_w_out=v_gdn_w_out,
                kv_norm_g=v_kv_norm_g, w_kv=v_w_kv, sb_norm_g=v_sb_norm_g, sb_w_q=v_sb_w_q, sb_w_o=v_sb_w_o,
                ffn_norm_g=v_ffn_norm_g, ffn_w_gate_up=v_ffn_w_gate_up, ffn_w_down=v_ffn_w_down, final_norm_g=v_final_norm_g)
    names = list(weights)

    seq, d = x.shape[1], x.shape[2]
    lo_frames = FRONT + N_META
    used = lo_frames + seq
    rows = -(-used // SB_BLOCK) * SB_BLOCK
    tm = _pick(rows, (640, 512, 384, 256, 128))
    tp = _pick(rows, (320, 256, 128))
    n_ffn = ffn_w_gate_up.shape[0]
    sb_width = sb_w_q.shape[2]
    chip =2 * lax.axis_index("x") + lax.axis_index("y")

    big = [gdn_w_in[0], gdn_w_out[0], w_kv, sb_w_q[0], sb_w_o[0], ffn_w_gate_up, ffn_w_down]
    small = [meta_tokens, gdn_norm_g, gdn_conv_w[0]]
    n_early = 2
    big_bf16 = [t.astype(bf16) for t in big]
    w_in_s, w_out_s, small_g = _gather_chips(big_bf16[:n_early] + [_pack(small, 16, f32)])
    small_s = _unpack(small_g, [t.shape for t in small])
    w_in = _join_cols(w_in_s)
    pad_ab = jnp.zeros((d, LANES - GDN_HEADS), bf16)
    w_in_ext = jnp.concatenate([w_in[:, :4 * GDN_WIDTH], w_in[:, 4 * GDN_WIDTH:4 * GDN_WIDTH + GDN_HEADS], pad_ab,
                                w_in[:, 4 * GDN_WIDTH + GDN_HEADS:], pad_ab], axis=1)
    w_out = _join_rows(w_out_s)
    meta_full, gdn_g_full, conv_full = (_join_cols(t) for t in small_s)

    zeros = lambda n: jnp.zeros((n, d), f32)
    h0 = jnp.concatenate([zeros(FRONT), meta_full, x[0], zeros(rows - used)], axis=0)
    tgt = jnp.concatenate([zeros(lo_frames), loss_target[0], zeros(rows - used)], axis=0)
    pad8 = lambda t: jnp.pad(t, ((0, 0), (0, LANES - t.shape[1])))
    a_log8, dt_bias8 = pad8(gdn_a_log), pad8(gdn_dt_bias)
    r_i = jnp.arange(tp)
    ltri = ((r_i[:, None] >= r_i[None, :]) & (r_i[:, None] // CHUNK == r_i[None, :] // CHUNK)).astype(f32)
    ffn_g = [ffn_norm_g[l:l + 1] for l in range(n_ffn)]
    kv_g, fin_g = kv_norm_g.reshape(1, d), final_norm_g.reshape(1, d)

    n0 = _rms_fwd(h0, gdn_g_full, "gdn_norm")
    proj = _matmul(n0, w_in_ext, "nn", "gdn_proj")
    gq, gk, gv, gw, bw = _gdn_prep_fwd(proj, conv_full, a_log8, dt_bias8, ltri, FRONT, used, tp)
    g_o, g_states, (w_kv_s, w_q_s, w_o_s, w_gu_s, w_dn_s) = _gdn_fwd(gq, gk, gv, gw, bw, big_bf16[n_early:])
    w_kvf = _join_cols(w_kv_s)
    w_q = _join_rows(w_q_s)
    w_o = _join_rows(w_o_s)
    w_gu = [_join_cols(w_gu_s[:, l]) for l in range(n_ffn)]
    w_dn = [_join_rows(w_dn_s[:, l]) for l in range(n_ffn)]
    og = _gdn_gate_fwd(g_o, proj, gdn_onorm_g, tm)
    h1 = _matmul(og, w_out, "nn", "gdn_out", res=h0)

    def ffn_fwd(h, l):
        n = _rms_fwd(h, ffn_g[l], f"ffn{l}_norm")
        gate, up, act = _ffn_up(n, w_gu[l], f"ffn{l}_gate_up")
        return _matmul(act, w_dn[l], "nn", f"ffn{l}_down", res=h), (n, gate, up, act)

    h2, ffn0_saved = ffn_fwd(h1, 0)
    n_kv = _rms_fwd(h2, kv_g, "kv_norm")
    kv = _matmul(n_kv, w_kvf, "nn", "kv_proj", out_dtype=bf16)
    n_sb = _rms_fwd(h2, sb_norm_g, "sb_norm")
    sq = _matmul(n_sb, w_q, "nn", "q_proj", out_dtype=bf16)
    s_o = _sb_fwd(sq, kv, sb_width)
    h3 = _matmul(s_o, w_o, "nn", "sb_out", res=h2)
    h4, ffn1_saved = ffn_fwd(h3, 1)
    dh4, d_fin_g, loss_part = _loss_head(h4, fin_g, tgt, lo_frames, used, "loss_head")

    def ffn_bwd(dh, h, l, saved):
        n, gate, up, act = saved
        d_wdn = _matmul(act, dh, "tn", f"ffn{l}_d_w_down")
        d_gate, d_up = _ffn_dact(dh, w_dn[l], gate, up, f"ffn{l}_d_gate_up")
        d_wgu = jnp.concatenate([_split_cols(_matmul(n, d_gate, "tn", f"ffn{l}_d_w_gate"), N_CHIPS // 2),
                                 _split_cols(_matmul(n, d_up, "tn", f"ffn{l}_d_w_up"), N_CHIPS // 2)], axis=0)
        dh_in, dg = _norm_bwd([d_gate, d_up], w_gu[l], h, ffn_g[l], dh, f"ffn{l}_d_norm")
        return dh_in, d_wgu, d_wdn, dg

    dh3, d_wgu1, d_wdn1, d_ffn_g1 = ffn_bwd(dh4, h3, 1, ffn1_saved)
    d_wo = _matmul(s_o, dh3, "tn", "d_w_o")
    d_so = _matmul(dh3, w_o, "nt", "d_sb_o")
    d_sq, d_sk, d_sv = _sb_bwd(sq, kv, d_so, sb_width)
    d_wq = _matmul(n_sb, d_sq, "tn", "d_w_q")
    dh2, d_sb_g = _norm_bwd([d_sq], w_q, h2, sb_norm_g, dh3, "d_sb_norm")
    d_wkv = jnp.concatenate([_matmul(n_kv, d_sk, "tn", "d_w_k"), _matmul(n_kv, d_sv, "tn", "d_w_v")], axis=1)
    dh2, d_kv_g = _norm_bwd([d_sk, d_sv], w_kvf, h2, kv_g, dh2, "d_kv_norm")
    dh1, d_wgu0, d_wdn0, d_ffn_g0 = ffn_bwd(dh2, h1, 0, ffn0_saved)
    d_wout = _matmul(og, dh1, "tn", "d_w_out")
    d_og = _matmul(dh1, w_out, "nt", "d_gdn_gated")
    d_go, d_gate, d_onorm = _gdn_gate_bwd(g_o, proj, gdn_onorm_g, d_og, tm)
    by_chip = [None, _split_rows(d_wout), _split_cols(d_wkv), _split_rows(d_wq), _split_rows(d_wo),
               jnp.stack([d_wgu0, d_wgu1], axis=1),
               jnp.stack([_split_rows(d_wdn0), _split_rows(d_wdn1)], axis=1)]
    (d_gq, d_gk, d_gv, d_gw, d_bw), got_early = _gdn_bwd(gq, gk, gv, gw, bw, g_states, d_go,
                                                         [t.astype(bf16) for t in by_chip[1:]])
    dconv, d_a_in, d_b_in, d_a_log8, d_dt_bias8 = _gdn_prep_bwd_act(
        proj, conv_full, a_log8, dt_bias8, ltri, d_gq, d_gk, d_gv, d_gw, d_bw, FRONT, used, tp)
    dproj, d_conv = _gdn_prep_bwd_conv(proj, conv_full, dconv, d_gate, d_a_in, d_b_in, tp)
    d_win_ext = _matmul(n0, dproj, "tn", "d_w_in")
    dh0, d_gdn_g = _norm_bwd([dproj], w_in_ext, h0, gdn_g_full, dh1, "d_gdn_norm")
    grad_x = dh0[lo_frames:used][None]
    d_win = jnp.concatenate([d_win_ext[:, :4 * GDN_WIDTH], d_win_ext[:, 4 * GDN_WIDTH:4 * GDN_WIDTH + GDN_HEADS],
                             d_win_ext[:, 4 * GDN_WIDTH + LANES:4 * GDN_WIDTH + LANES + GDN_HEADS]], axis=1)

    by_chip[0] = _split_cols(d_win)
    got = list(_scatter_chips([by_chip[0].astype(bf16)])) + list(got_early)
    chip_arr = jnp.reshape(chip, (1,)).astype(i32)
    over_chips = [_sum_chips(t, g, chip_arr, f"sum_chips_{k}") for k, (t, g) in enumerate(zip(by_chip, got))]
    over_sibling = _swap_sibling(over_chips)
    big_names = ["gdn_w_in", "gdn_w_out", "w_kv", "sb_w_q", "sb_w_o", "ffn_w_gate_up", "ffn_w_down"]
    g_big = dict(zip(big_names, zip(over_chips, over_sibling)))

    small_parts = [dh0[FRONT:lo_frames], d_gdn_g, d_conv, d_a_log8, d_dt_bias8, d_onorm, d_kv_g, d_sb_g,
                   d_ffn_g0, d_ffn_g1, d_fin_g, loss_part]
    s_rows = -(-sum(t.size for t in small_parts) // (8 * PACK_COLS)) * 8
    s_sum = _sum_devices(_gather_all(_pack(small_parts, s_rows, f32)), s_rows)
    (g_meta, g_gdn_g, g_conv, g_a_log8, g_dt8, g_onorm, g_kv_g, g_sb_g, g_ffn_g0, g_ffn_g1, g_fin_g,
     loss_v) = _unpack(s_sum, [t.shape for t in small_parts])
    col_shard = lambda t, w: lax.dynamic_slice_in_dim(t, chip * w, w, axis=t.ndim - 1)

    g_small = dict(
        meta_tokens=col_shard(g_meta, meta_tokens.shape[1]), gdn_norm_g=col_shard(g_gdn_g, gdn_norm_g.shape[1]),
        gdn_conv_w=col_shard(g_conv, gdn_conv_w.shape[2])[None],
        gdn_a_log=g_a_log8[:, :GDN_HEADS], gdn_dt_bias=g_dt8[:, :GDN_HEADS], gdn_onorm_g=g_onorm,
        kv_norm_g=g_kv_g.reshape(-1), sb_norm_g=g_sb_g, ffn_norm_g=jnp.concatenate([g_ffn_g0, g_ffn_g1], axis=0),
        final_norm_g=g_fin_g.reshape(-1))

    grads, delta, new_m, new_v = {}, {}, {}, {}
    for n in names:
        gs = g_big[n] if n in g_big else (g_small[n],)
        grads[n], delta[n], new_m[n], new_v[n] = _adamw(weights[n], gs, m_in[n], v_in[n], f"adamw_{n}")
    loss = loss_v[0, 0]
    return (loss, grad_x, *[grads[n] for n in names], *[delta[n] for n in names],
            *[new_m[n] for n in names], *[new_v[n] for n in names])
```

```python
import functools

import jax
import jax.numpy as jnp
from jax import lax
from jax.experimental import pallas as pl
from jax.experimental.pallas import tpu as pltpu

f32 = jnp.float32
bf16 = jnp.bfloat16
i32 = jnp.int32

EPS = 1e-6
N_META = 16
CHUNK = 64
FRONT = (-N_META) % CHUNK
GDN_HEADS = 8
GDN_DIM = 128
GDN_WIDTH = GDN_HEADS * GDN_DIM
CONV_WIDTH = 4
SB_DIM = 64
SB_BLOCK = 128
SB_FWD_HEADS = 4
SB_FIRST = 3
SB_UNDERFLOW = 104.0
LANES = 128
PACK_COLS = 1024
N_CHIPS = 4
N_DEV = 8
ADAM_LR, ADAM_B1, ADAM_B2, ADAM_EPS, ADAM_WD, ADAM_STEP = 0.001, 0.9, 0.999, 1e-08, 0.01, 10
VMEM_LIMIT = 56 * 1024 * 1024
MESH = pl.DeviceIdType.MESH


def _pick(n, prefs):
    for p in prefs:
        if n % p == 0:
            return p
    return n


def _params(sem):
    return pltpu.CompilerParams(dimension_semantics=sem, vmem_limit_bytes=VMEM_LIMIT)


_DIMS = {"nn": ((1,), (0,)), "nt": ((1,), (1,)), "tn": ((0,), (0,))}


def _bdot(a, b, mode):
    return lax.dot_general(a.astype(bf16), b.astype(bf16), (_DIMS[mode], ((), ())), preferred_element_type=f32)


def _matmul(a, b, mode, name, res=None, out_dtype=f32):
    if mode == "nn":
        (m, k), n = a.shape, b.shape[1]
    elif mode == "nt":
        (m, k), n = a.shape, b.shape[0]
    else:
        (k, m), n = a.shape, b.shape[1]
    tm = _pick(m, (640, 1408, 1024, 512, 384, 256, 128))
    tn = _pick(n, (1408, 2176, 1024, 512, 384, 256, 128))
    tk = _pick(k, (1408, 2176, 1024, 640, 512, 384, 256, 128))
    nk = k // tk
    a_spec = pl.BlockSpec((tk, tm), lambda j, i, kk: (kk, i)) if mode == "tn" else pl.BlockSpec((tm, tk), lambda j, i, kk: (i, kk))
    b_spec = pl.BlockSpec((tn, tk), lambda j, i, kk: (j, kk)) if mode == "nt" else pl.BlockSpec((tk, tn), lambda j, i, kk: (kk, j))
    o_spec = pl.BlockSpec((tm, tn), lambda j, i, kk: (i, j))
    has_res = res is not None

    def body(*refs):
        if has_res:
            a_ref, b_ref, r_ref, o_ref, acc = refs
        else:
            a_ref, b_ref, o_ref, acc = refs
        kk = pl.program_id(2)

        @pl.when(kk == 0)
        def _():
            acc[...] = jnp.zeros_like(acc)

        acc[...] += _bdot(a_ref[...], b_ref[...], mode)

        @pl.when(kk == nk - 1)
        def _():
            y = acc[...]
            if has_res:
                y = y + r_ref[...]
            o_ref[...] = y.astype(o_ref.dtype)

    ins = [a, b] + ([res] if has_res else [])
    specs = [a_spec, b_spec] + ([o_spec] if has_res else [])
    return pl.pallas_call(
        body, name=name, out_shape=jax.ShapeDtypeStruct((m, n), out_dtype), grid=(n // tn, m // tm, nk),
        in_specs=specs, out_specs=o_spec, scratch_shapes=[pltpu.VMEM((tm, tn), f32)],
        compiler_params=_params(("parallel", "parallel", "arbitrary")),
    )(*ins)


def _rowwise(fn, name, rows, tm, ins, outs, reds=()):
    n_in, n_out, n_red = len(ins), len(outs), len(reds)
    in_specs = []
    for arr, spec in ins:
        if spec is None:
            in_specs.append(pl.BlockSpec(arr.shape, lambda i, nd=arr.ndim: (0,) * nd))
        else:
            w, cb = spec
            in_specs.append(pl.BlockSpec((tm, w), lambda i, cb=cb: (i, cb)))
    out_specs = [pl.BlockSpec((tm, w), lambda i: (i, 0)) for w, _ in outs]
    out_specs += [pl.BlockSpec(s, lambda i, nd=len(s): (0,) * nd) for s in reds]
    out_shape = [jax.ShapeDtypeStruct((rows, w), dt) for w, dt in outs]
    out_shape += [jax.ShapeDtypeStruct(s, f32) for s in reds]

    def body(*refs):
        i = pl.program_id(0)
        vals = fn(i, *[r[...] for r in refs[:n_in]])
        for r, v in zip(refs[n_in:n_in + n_out], vals[:n_out]):
            r[...] = v.astype(r.dtype)
        red_refs = refs[n_in + n_out:]

        @pl.when(i == 0)
        def _():
            for r in red_refs:
                r[...] = jnp.zeros_like(r)

        for r, v in zip(red_refs, vals[n_out:]):
            r[...] += v

    res = pl.pallas_call(
        body, name=name, out_shape=out_shape, grid=(rows // tm,), in_specs=in_specs, out_specs=out_specs,
        compiler_params=_params(("arbitrary",)),
    )(*[a for a, _ in ins])
    return res


def _rms(x, g):
    return x * lax.rsqrt(jnp.mean(x * x, axis=-1, keepdims=True) + EPS) * g


def _row_mask(i, tm, lo, hi, shape):
    r = i * tm + lax.broadcasted_iota(i32, shape, 0)
    return (r >= lo) & (r < hi)


def _rms_fwd(x, g, name):
    rows, d = x.shape
    tm = _pick(rows, (640, 512, 384, 256, 128))
    return _rowwise(lambda i, xv, gv: (_rms(xv, gv),), name, rows, tm, [(x, (d, 0)), (g, None)], [(d, bf16)])[0]


def _norm_bwd(parts, w, x, g, res, name):
    rows, k = parts[0].shape
    d = w.shape[0]
    num = len(parts)
    tm = _pick(rows, (640, 512, 384, 256, 128))
    tk = _pick(k, (1408, 2176, 1024, 512, 384, 256, 128))
    nk = k // tk

    def body(*refs):
        a_refs, w_refs = refs[:num], refs[num:2 * num]
        x_r, g_r, r_r, o_r, dg_r, acc = refs[2 * num:]
        i, kk = pl.program_id(0), pl.program_id(1)

        @pl.when(kk == 0)
        def _():
            acc[...] = jnp.zeros_like(acc)

        acc[...] += functools.reduce(jnp.add, [_bdot(a[...], b[...], "nt") for a, b in zip(a_refs, w_refs)])

        @pl.when((i == 0) & (kk == 0))
        def _():
            dg_r[...] = jnp.zeros_like(dg_r)

        @pl.when(kk == nk - 1)
        def _():
            _, vjp = jax.vjp(_rms, x_r[...], g_r[...])
            dx, dg = vjp(acc[...])
            o_r[...] = r_r[...] + dx
            dg_r[...] += dg

    row = pl.BlockSpec((tm, d), lambda i, kk: (i, 0))
    one = pl.BlockSpec((1, d), lambda i, kk: (0, 0))
    return pl.pallas_call(
        body, name=name, out_shape=[jax.ShapeDtypeStruct((rows, d), f32), jax.ShapeDtypeStruct((1, d), f32)],
        grid=(rows // tm, nk),
        in_specs=[pl.BlockSpec((tm, tk), lambda i, kk: (i, kk))] * num
        + [pl.BlockSpec((d, tk), lambda i, kk, p=p: (0, p * nk + kk)) for p in range(num)] + [row, one, row],
        out_specs=[row, one], scratch_shapes=[pltpu.VMEM((tm, d), f32)],
        compiler_params=_params(("arbitrary", "arbitrary")),
    )(*parts, *([w] * num), x, g, res)


def _swiglu(gate, up):
    return jax.nn.silu(gate) * up


def _ffn_up(n, w_gu, name):
    rows, d = n.shape
    f = w_gu.shape[1] // 2
    tm = _pick(rows, (640, 512, 384, 256, 128))
    tn = _pick(f, (1408, 1024, 512, 384, 256, 128))
    nj = f // tn

    def body(n_r, wg_r, wu_r, g_r, u_r, a_r):
        g = jnp.dot(n_r[...], wg_r[...], preferred_element_type=f32)
        u = jnp.dot(n_r[...], wu_r[...], preferred_element_type=f32)
        g_r[...] = g.astype(g_r.dtype)
        u_r[...] = u.astype(u_r.dtype)
        a_r[...] = _swiglu(g, u).astype(a_r.dtype)

    o_spec = pl.BlockSpec((tm, tn), lambda j, i: (i, j))
    return pl.pallas_call(
        body, name=name, grid=(nj, rows // tm), out_shape=[jax.ShapeDtypeStruct((rows, f), bf16)] * 3,
        in_specs=[pl.BlockSpec((tm, d), lambda j, i: (i, 0)), pl.BlockSpec((d, tn), lambda j, i: (0, j)),
                  pl.BlockSpec((d, tn), lambda j, i: (0, nj + j))],
        out_specs=[o_spec] * 3, compiler_params=_params(("parallel", "parallel")),
    )(n, w_gu, w_gu)


def _ffn_dact(dh, w_dn, gate, up, name):
    rows, d = dh.shape
    f = w_dn.shape[0]
    tm = _pick(rows, (640, 512, 384, 256, 128))
    tn = _pick(f, (1408, 1024, 512, 384, 256, 128))

    def body(dh_r, w_r, g_r, u_r, dg_r, du_r):
        dact = _bdot(dh_r[...], w_r[...], "nt")
        _, vjp = jax.vjp(_swiglu, g_r[...].astype(f32), u_r[...].astype(f32))
        dg, du = vjp(dact)
        dg_r[...] = dg.astype(dg_r.dtype)
        du_r[...] = du.astype(du_r.dtype)

    t_spec = pl.BlockSpec((tm, tn), lambda j, i: (i, j))
    return pl.pallas_call(
        body, name=name, grid=(f // tn, rows // tm), out_shape=[jax.ShapeDtypeStruct((rows, f), bf16)] * 2,
        in_specs=[pl.BlockSpec((tm, d), lambda j, i: (i, 0)), pl.BlockSpec((tn, d), lambda j, i: (j, 0)), t_spec, t_spec],
        out_specs=[t_spec] * 2, compiler_params=_params(("parallel", "parallel")),
    )(dh, w_dn, gate, up)


def _loss_head(h, g, tgt, lo, hi, name):
    rows, d = h.shape
    tm = _pick(rows, (640, 512, 384, 256, 128))

    def fn(i, hv, gv, tv):
        mask = _row_mask(i, tm, lo, hi, (tm, 1))

        def f(hh, gg):
            err = _rms(hh, gg) - tv
            per_row = jnp.where(mask, jnp.mean(err * err, axis=-1, keepdims=True), 0.0)
            return 0.5 * jnp.sum(per_row, axis=0, keepdims=True)

        loss, vjp = jax.vjp(f, hv, gv)
        dh, dg = vjp(jnp.ones_like(loss))
        return dh, dg, jnp.broadcast_to(loss, (1, LANES))

    return _rowwise(fn, name, rows, tm, [(h, (d, 0)), (g, None), (tgt, (d, 0))], [(d, f32)], [(1, d), (1, LANES)])


def _heads_l2(x):
    heads = [x[:, h * GDN_DIM:(h + 1) * GDN_DIM] for h in range(GDN_HEADS)]
    return jnp.concatenate([xh * lax.rsqrt(jnp.sum(xh * xh, axis=-1, keepdims=True) + EPS) for xh in heads], axis=1)


def _gdn_act(conv, a_in, b_in, a_log, dt_bias, mask):
    s = jax.nn.silu(conv)
    q = _heads_l2(s[:, :GDN_WIDTH])
    k = _heads_l2(s[:, GDN_WIDTH:2 * GDN_WIDTH])
    v = s[:, 2 * GDN_WIDTH:]
    g = jnp.where(mask, -jnp.exp(a_log) * jax.nn.softplus(a_in + dt_bias), 0.0)
    beta = jnp.where(mask, jax.nn.sigmoid(b_in), 0.0)
    return q, k, v, g, beta


def _widen(x8):
    return [jnp.broadcast_to(x8[:, h:h + 1], (x8.shape[0], GDN_DIM)) for h in range(GDN_HEADS)]


def _narrow(per_head):
    t = per_head[0].shape[0]
    lane = lax.broadcasted_iota(i32, (t, LANES), 1)
    out = jnp.zeros((t, LANES), f32)
    for h, x in enumerate(per_head):
        out = out + jnp.where(lane == h, jnp.sum(x, axis=1, keepdims=True), 0.0)
    return out


def _conv_taps(cur, prev8, w):
    tm = cur.shape[0]
    cat = jnp.concatenate([prev8, cur], axis=0)
    y = cur * w[CONV_WIDTH - 1:CONV_WIDTH, :]
    for j in range(1, CONV_WIDTH):
        y = y + pltpu.roll(cat, j, axis=0)[8:8 + tm, :] * w[CONV_WIDTH - 1 - j:CONV_WIDTH - j, :]
    return y


def _gdn_prep_specs(proj, tm):
    c3 = 3 * GDN_WIDTH
    ab = 4 * GDN_WIDTH // LANES
    t8 = tm // 8
    return [
        pl.BlockSpec((tm, c3), lambda i: (i, 0)),
        pl.BlockSpec((8, c3), lambda i: (jnp.maximum(i * t8 - 1, 0), 0)),
        pl.BlockSpec((tm, LANES), lambda i: (i, ab)),
        pl.BlockSpec((tm, LANES), lambda i: (i, ab + 1)),
    ]


def _full(arr):
    return pl.BlockSpec(arr.shape, lambda i, nd=arr.ndim: (0,) * nd)


def _gdn_prep_fwd(proj, conv_w, a_log, dt_bias, ltri, lo, hi, tm):
    rows = proj.shape[0]

    def body(cur, prev8, a_in, b_in, w, al, dtb, lt, q_o, k_o, v_o, g_o, b_o):
        i = pl.program_id(0)
        mask = _row_mask(i, tm, lo, hi, (tm, LANES)) & (lax.broadcasted_iota(i32, (tm, LANES), 1) < GDN_HEADS)
        conv = _conv_taps(cur[...], prev8[...], w[...])
        q, k, v, g, beta = _gdn_act(conv, a_in[...], b_in[...], al[...], dtb[...], mask)
        q_o[...] = q
        k_o[...] = k
        v_o[...] = v
        gcum = jnp.dot(lt[...], g, preferred_element_type=f32, precision=lax.Precision.HIGHEST)
        g_o[...] = gcum
        b_o[...] = beta

    wide = jax.ShapeDtypeStruct((rows, GDN_WIDTH), f32)
    narrow = jax.ShapeDtypeStruct((rows, LANES), f32)
    o_spec = pl.BlockSpec((tm, GDN_WIDTH), lambda i: (i, 0))
    n_spec = pl.BlockSpec((tm, LANES), lambda i: (i, 0))
    return pl.pallas_call(
        body, name="gdn_prep_fwd", out_shape=[wide] * 3 + [narrow] * 2, grid=(rows // tm,),
        in_specs=_gdn_prep_specs(proj, tm) + [_full(conv_w), _full(a_log), _full(dt_bias), _full(ltri)],
        out_specs=[o_spec] * 3 + [n_spec] * 2, compiler_params=_params(("parallel",)),
    )(proj, proj, proj, proj, conv_w, a_log, dt_bias, ltri)


def _gdn_prep_bwd_act(proj, conv_w, a_log, dt_bias, ltri, dq, dk, dv, dgw, dbw, lo, hi, tm):
    rows = proj.shape[0]
    c3 = 3 * GDN_WIDTH

    def body(cur, prev8, a_in, b_in, w, al, dtb, lt, dq_r, dk_r, dv_r, dg_r, db_r, dconv_o, da_o, dbin_o, dal_o, ddt_o):
        i = pl.program_id(0)
        mask = _row_mask(i, tm, lo, hi, (tm, LANES)) & (lax.broadcasted_iota(i32, (tm, LANES), 1) < GDN_HEADS)
        conv = _conv_taps(cur[...], prev8[...], w[...])
        dg = lax.dot_general(lt[...], dg_r[...], (((0,), (0,)), ((), ())), preferred_element_type=f32,
                             precision=lax.Precision.HIGHEST)
        dbeta = db_r[...]
        _, vjp = jax.vjp(lambda c, a, b, x, y: _gdn_act(c, a, b, x, y, mask), conv, a_in[...], b_in[...], al[...], dtb[...])
        dconv, da, dbin, dal, ddt = vjp((dq_r[...], dk_r[...], dv_r[...], dg, dbeta))
        dconv_o[...] = dconv
        da_o[...] = da
        dbin_o[...] = dbin

        @pl.when(i == 0)
        def _():
            dal_o[...] = jnp.zeros_like(dal_o)
            ddt_o[...] = jnp.zeros_like(ddt_o)

        dal_o[...] += dal
        ddt_o[...] += ddt

    w_spec = pl.BlockSpec((tm, GDN_WIDTH), lambda i: (i, 0))
    n_spec = pl.BlockSpec((tm, LANES), lambda i: (i, 0))
    s_spec = pl.BlockSpec((1, LANES), lambda i: (0, 0))
    return pl.pallas_call(
        body, name="gdn_prep_bwd_act",
        out_shape=[jax.ShapeDtypeStruct((rows, c3), f32), jax.ShapeDtypeStruct((rows, LANES), f32),
                   jax.ShapeDtypeStruct((rows, LANES), f32), jax.ShapeDtypeStruct((1, LANES), f32),
                   jax.ShapeDtypeStruct((1, LANES), f32)],
        grid=(rows // tm,),
        in_specs=_gdn_prep_specs(proj, tm) + [_full(conv_w), _full(a_log), _full(dt_bias), _full(ltri)] + [w_spec] * 3 + [n_spec] * 2,
        out_specs=[pl.BlockSpec((tm, c3), lambda i: (i, 0)), n_spec, n_spec, s_spec, s_spec],
        compiler_params=_params(("arbitrary",)),
    )(proj, proj, proj, proj, conv_w, a_log, dt_bias, ltri, dq, dk, dv, dgw, dbw)


def _gdn_prep_bwd_conv(proj, conv_w, dconv, dgate, da, dbin, tm):
    rows, width = proj.shape
    c3 = 3 * GDN_WIDTH
    t8 = tm // 8
    nt = rows // tm

    def body(cur, w, dc, dnext8, dgt, da_r, db_r, dp_o, dw_o):
        i = pl.program_id(0)
        d = dc[...]
        x = cur[...]
        nxt = jnp.where(i == nt - 1, 0.0, dnext8[...])
        cat = jnp.concatenate([d, nxt], axis=0)
        wv = w[...]
        dx = d * wv[CONV_WIDTH - 1:CONV_WIDTH, :]
        parts = [jnp.sum(d * x, axis=0, keepdims=True)]
        for j in range(1, CONV_WIDTH):
            ahead = pltpu.roll(cat, tm + 8 - j, axis=0)[:tm, :]
            dx = dx + ahead * wv[CONV_WIDTH - 1 - j:CONV_WIDTH - j, :]
            parts.append(jnp.sum(ahead * x, axis=0, keepdims=True))
        dp_o[:, :c3] = dx.astype(bf16)
        dp_o[:, c3:4 * GDN_WIDTH] = dgt[...].astype(bf16)
        dp_o[:, 4 * GDN_WIDTH:4 * GDN_WIDTH + LANES] = da_r[...].astype(bf16)
        dp_o[:, 4 * GDN_WIDTH + LANES:] = db_r[...].astype(bf16)
        dwt = jnp.concatenate(parts[::-1], axis=0)

        @pl.when(i == 0)
        def _():
            dw_o[...] = jnp.zeros_like(dw_o)

        dw_o[...] += dwt

    n_spec = pl.BlockSpec((tm, LANES), lambda i: (i, 0))
    return pl.pallas_call(
        body, name="gdn_prep_bwd_conv",
        out_shape=[jax.ShapeDtypeStruct((rows, width), bf16), jax.ShapeDtypeStruct((CONV_WIDTH, c3), f32)],
        grid=(nt,),
        in_specs=[pl.BlockSpec((tm, c3), lambda i: (i, 0)),
                  _full(conv_w),
                  pl.BlockSpec((tm, c3), lambda i: (i, 0)),
                  pl.BlockSpec((8, c3), lambda i: (jnp.minimum((i + 1) * t8, rows // 8 - 1), 0)),
                  pl.BlockSpec((tm, GDN_WIDTH), lambda i: (i, 0)), n_spec, n_spec],
        out_specs=[pl.BlockSpec((tm, width), lambda i: (i, 0)), pl.BlockSpec((CONV_WIDTH, c3), lambda i: (0, 0))],
        compiler_params=_params(("arbitrary",)),
    )(proj, conv_w, dconv, dconv, dgate, da, dbin)


def _split(a):
    hi = a.astype(bf16)
    return hi, (a - hi.astype(f32)).astype(bf16)


def _make_mm(dot):
    @jax.custom_vjp
    def nn(a, b):
        return dot(a, b, "nn")

    nn.defvjp(lambda a, b: (dot(a, b, "nn"), (a, b)),
              lambda r, ct: (dot(ct, r[1], "nt"), dot(r[0], ct, "tn")))

    @jax.custom_vjp
    def nt(a, b):
        return dot(a, b, "nt")

    nt.defvjp(lambda a, b: (dot(a, b, "nt"), (a, b)),
              lambda r, ct: (dot(ct, r[1], "nn"), dot(ct, r[0], "tn")))

    @jax.custom_vjp
    def tn(a, b):
        return dot(a, b, "tn")

    tn.defvjp(lambda a, b: (dot(a, b, "tn"), (a, b)),
              lambda r, ct: (dot(r[1], ct, "nt"), dot(r[0], ct, "nn")))
    return nn, nt, tn


_mm, _mm_nt, _mm_tn = _make_mm(_bdot)


def _each(f, *lists):
    return [f(*xs) for xs in zip(*lists)]


def _gdn_chunk(q, k, v, gcb, bcb, s_in):
    c = q[0].shape[0]
    ri = lax.broadcasted_iota(i32, (c, c), 0)
    ci = lax.broadcasted_iota(i32, (c, c), 1)
    incl, strict = ri >= ci, ri > ci
    rowi = lax.broadcasted_iota(i32, gcb[0].shape, 0)
    qs = _each(lambda t: t * (GDN_DIM ** -0.5), q)
    decay = _each(lambda g: jnp.where(incl, jnp.exp(jnp.where(incl, g[:, :c] - g[:, :c].T, 0.0)), 0.0), gcb)
    kk = _each(lambda t: _mm_nt(t, t), k)
    a1 = _each(lambda b, d, t: jnp.where(strict, b[:, :c] * d * t, 0.0), bcb, decay, kk)
    eg = _each(jnp.exp, gcb)
    x = _each(lambda b, vv, e, t: jnp.concatenate([b * vv, (b * e) * t], axis=1), bcb, v, eg, k)
    pows = [a1]
    for _ in range(5):
        pows.append(_each(lambda p: _mm(p, p), pows[-1]))
    for ps in pows[:0:-1]:
        x = _each(lambda p, t: t + _mm(p, t), ps, x)
    x = _each(lambda p, t: t - _mm(p, t), a1, x)
    attn = _each(lambda a, b, d: _mm_nt(a, b) * d, qs, k, decay)
    glast = _each(lambda g: jnp.sum(jnp.where(rowi == c - 1, g, 0.0), axis=0, keepdims=True), gcb)
    u = _each(lambda t, s: t[:, :GDN_DIM] - _mm(t[:, GDN_DIM:], s), x, s_in)
    o = _each(lambda a, e, s, w, uu: _mm(a * e, s) + _mm(w, uu), qs, eg, s_in, attn, u)
    s_out = _each(lambda s, gl, t, g, uu: s * jnp.exp(gl) + _mm_tn(t * jnp.exp(gl - g), uu), s_in, glast, k, gcb, u)
    return o, s_out


def _gdn_heads(ref):
    return [ref[:, h * GDN_DIM:(h + 1) * GDN_DIM] for h in range(GDN_HEADS)]


def _gdn_fwd(q, k, v, gw, bw, shards):
    rows = q.shape[0]
    nc = rows // CHUNK
    num = len(shards)
    blk = pl.BlockSpec((CHUNK, GDN_WIDTH), lambda c: (c, 0))

    def body(*refs):
        q_r, k_r, v_r, g_r, b_r = refs[:5]
        ins = refs[5:5 + num]
        o_r, st_r = refs[5 + num:7 + num]
        outs = refs[7 + num:7 + 2 * num]
        s_sc, send_sems, recv_sems, local_sems = refs[7 + 2 * num:]
        c = pl.program_id(0)
        start, forward, finish = _gather_plan(ins, outs, send_sems, recv_sems, local_sems)

        @pl.when(c == 0)
        def _():
            s_sc[...] = jnp.zeros_like(s_sc)
            start()

        s_in = [s_sc[h] for h in range(GDN_HEADS)]
        st_r[0] = s_sc[...]
        o, s_out = _gdn_chunk(_gdn_heads(q_r), _gdn_heads(k_r), _gdn_heads(v_r), _widen(g_r[...]), _widen(b_r[...]), s_in)
        o_r[...] = jnp.concatenate(o, axis=1)
        for h in range(GDN_HEADS):
            s_sc[h] = s_out[h]
        pl.when(c == nc // 2)(forward)
        pl.when(c == nc - 1)(finish)

    res = pl.pallas_call(
        body, name="gdn_fwd",
        out_shape=[jax.ShapeDtypeStruct((rows, GDN_WIDTH), f32), jax.ShapeDtypeStruct((nc, GDN_HEADS, GDN_DIM, GDN_DIM), f32)]
        + [jax.ShapeDtypeStruct((N_CHIPS,) + t.shape, t.dtype) for t in shards],
        grid=(nc,), in_specs=[blk] * 3 + [pl.BlockSpec((CHUNK, LANES), lambda c: (c, 0))] * 2 + [_ANY] * num,
        out_specs=[blk, pl.BlockSpec((1, GDN_HEADS, GDN_DIM, GDN_DIM), lambda c: (c, 0, 0, 0))] + [_ANY] * num,
        scratch_shapes=[pltpu.VMEM((GDN_HEADS, GDN_DIM, GDN_DIM), f32), pltpu.SemaphoreType.DMA((6 * num,)),
                        pltpu.SemaphoreType.DMA((6 * num,)), pltpu.SemaphoreType.DMA((num,))],
        compiler_params=_params(("arbitrary",)),
    )(q, k, v, gw, bw, *shards)
    return res[0], res[1], res[2:]


def _gdn_bwd(q, k, v, gw, bw, states, do, parts):
    rows = q.shape[0]
    nc = rows // CHUNK
    num = len(parts)
    blk = pl.BlockSpec((CHUNK, GDN_WIDTH), lambda c: (nc - 1 - c, 0))

    def body(*refs):
        q_r, k_r, v_r, g_r, b_r, st_r, do_r = refs[:7]
        ins = refs[7:7 + num]
        dq_r, dk_r, dv_r, dg_r, db_r = refs[7 + num:12 + num]
        outs = refs[12 + num:12 + 2 * num]
        ds_sc, send_sems, recv_sems = refs[12 + 2 * num:]
        c = pl.program_id(0)
        start, finish = _scatter_plan(ins, outs, send_sems, recv_sems)

        @pl.when(c == 0)
        def _():
            ds_sc[...] = jnp.zeros_like(ds_sc)
            start()

        s_in = [st_r[0, h] for h in range(GDN_HEADS)]
        _, vjp = jax.vjp(_gdn_chunk, _gdn_heads(q_r), _gdn_heads(k_r), _gdn_heads(v_r), _widen(g_r[...]), _widen(b_r[...]), s_in)
        dq, dk, dv, dg, db, ds_in = vjp((_gdn_heads(do_r), [ds_sc[h] for h in range(GDN_HEADS)]))
        dq_r[...] = jnp.concatenate(dq, axis=1)
        dk_r[...] = jnp.concatenate(dk, axis=1)
        dv_r[...] = jnp.concatenate(dv, axis=1)
        dg_r[...] = _narrow(dg)
        db_r[...] = _narrow(db)
        for h in range(GDN_HEADS):
            ds_sc[h] = ds_in[h]
        pl.when(c == nc - 1)(finish)

    wide = jax.ShapeDtypeStruct((rows, GDN_WIDTH), f32)
    narrow = jax.ShapeDtypeStruct((rows, LANES), f32)
    nblk = pl.BlockSpec((CHUNK, LANES), lambda c: (nc - 1 - c, 0))
    res = pl.pallas_call(
        body, name="gdn_bwd",
        out_shape=[wide] * 3 + [narrow] * 2 + [jax.ShapeDtypeStruct((3,) + t.shape[1:], t.dtype) for t in parts],
        grid=(nc,),
        in_specs=[blk] * 3 + [nblk] * 2
        + [pl.BlockSpec((1, GDN_HEADS, GDN_DIM, GDN_DIM), lambda c: (nc - 1 - c, 0, 0, 0)), blk] + [_ANY] * num,
        out_specs=[blk] * 3 + [nblk] * 2 + [_ANY] * num,
        scratch_shapes=[pltpu.VMEM((GDN_HEADS, GDN_DIM, GDN_DIM), f32), pltpu.SemaphoreType.DMA((3 * num,)),
                        pltpu.SemaphoreType.DMA((3 * num,))],
        compiler_params=_params(("arbitrary",)),
    )(q, k, v, gw, bw, states, do, *parts)
    return res[:5], res[5:]


def _gdn_gate(o, gate, og):
    heads = [o[:, h * GDN_DIM:(h + 1) * GDN_DIM] for h in range(GDN_HEADS)]
    n = jnp.concatenate([oh * lax.rsqrt(jnp.mean(oh * oh, axis=-1, keepdims=True) + EPS) * og for oh in heads], axis=1)
    return n * jax.nn.silu(gate)


def _gdn_gate_fwd(o, proj, og, tm):
    rows = o.shape[0]
    return _rowwise(lambda i, ov, gv, w: (_gdn_gate(ov, gv, w),), "gdn_gate_fwd", rows, tm,
                    [(o, (GDN_WIDTH, 0)), (proj, (GDN_WIDTH, 3)), (og, None)], [(GDN_WIDTH, bf16)])[0]


def _gdn_gate_bwd(o, proj, og, dy, tm):
    rows = o.shape[0]

    def fn(i, ov, gv, w, d):
        _, vjp = jax.vjp(_gdn_gate, ov, gv, w)
        return vjp(d)

    return _rowwise(fn, "gdn_gate_bwd", rows, tm,
                    [(o, (GDN_WIDTH, 0)), (proj, (GDN_WIDTH, 3)), (og, None), (dy, (GDN_WIDTH, 0))],
                    [(GDN_WIDTH, f32), (GDN_WIDTH, f32)], [(1, GDN_DIM)])


def _sb_visible(i, j, valid):
    qpos = i * SB_BLOCK + lax.broadcasted_iota(i32, (SB_BLOCK, SB_BLOCK), 0)
    kpos = j * SB_BLOCK + lax.broadcasted_iota(i32, (SB_BLOCK, SB_BLOCK), 1)
    return (kpos < qpos) & (kpos >= FRONT) & valid


def _sb_logs(z, vis):
    l1p = jnp.log(1.0 + jnp.exp(-jnp.abs(z)))
    return -(jnp.maximum(-z, 0.0) + l1p), jnp.where(vis, -(jnp.maximum(z, 0.0) + l1p), 0.0)


def _tri_sum(x, tri):
    hi, lo = _split(x)
    return jnp.dot(hi, tri, preferred_element_type=f32) + jnp.dot(lo, tri, preferred_element_type=f32)


def _sb_live(t, i, runs):
    return (t <= i) & (jnp.max(functools.reduce(jnp.maximum, runs)) > -SB_UNDERFLOW)


def _sb_blocks(i, t, nb):
    js = [i - t - b for b in range(nb)]
    kss = [pl.ds(pl.multiple_of(jnp.maximum(j, 0) * SB_BLOCK, SB_BLOCK), SB_BLOCK) for j in js]
    return kss, [_sb_visible(i, j, j >= 0) for j in js]


def _sb_weights(i, t, nb, qs, sls, k_r, runs, after, scale):
    nh = len(qs)
    kss, vis = _sb_blocks(i, t, nb)
    units = [(a, b) for b in range(nb) for a in range(nh)]
    z = [_bdot(qs[a], k_r[kss[b], sls[a]], "nt") * scale for a, b in units]
    logs = [_sb_logs(zz, vis[b]) for zz, (a, b) in zip(z, units)]
    later = [_tri_sum(l[1], after) for l in logs]
    sums = [jnp.sum(l[1], axis=1, keepdims=True) for l in logs]
    w = []
    runs = list(runs)
    for b in range(nb):
        for a in range(nh):
            u = b * nh + a
            w.append(jnp.where(vis[b], jnp.exp(logs[u][0] + later[u] + runs[a]), 0.0))
        runs = [runs[a] + sums[b * nh + a] for a in range(nh)]
    return kss, vis, units, logs, w, tuple(runs)


def _sb_fwd(q, kv, width):
    rows = q.shape[0]
    nq = rows // SB_BLOCK
    lanes = SB_FWD_HEADS * SB_DIM
    npair = width // lanes
    scale = SB_DIM ** -0.5

    def body(q_r, k_r, v_r, o_r):
        i = pl.program_id(1)
        rj = lax.broadcasted_iota(i32, (SB_BLOCK, SB_BLOCK), 0)
        cs = lax.broadcasted_iota(i32, (SB_BLOCK, SB_BLOCK), 1)
        after = (rj > cs).astype(bf16)
        sls = [slice(a * SB_DIM, (a + 1) * SB_DIM) for a in range(SB_FWD_HEADS)]
        qs = [q_r[:, sl] for sl in sls]

        def step(carry, nb):
            t, accs, runs = carry
            kss, _, units, _, w, runs = _sb_weights(i, t, nb, qs, sls, k_r, runs, after, scale)
            prods = [_bdot(ww, v_r[kss[b], sls[a]], "nn") for ww, (a, b) in zip(w, units)]
            accs = tuple(functools.reduce(jnp.add, [accs[a]] + prods[a::SB_FWD_HEADS]) for a in range(SB_FWD_HEADS))
            return t + nb, accs, runs

        init = (jnp.int32(0), tuple(jnp.zeros((SB_BLOCK, SB_DIM), f32) for _ in sls),
                tuple(jnp.zeros((SB_BLOCK, 1), f32) for _ in sls))
        _, accs, _ = lax.while_loop(lambda c: _sb_live(c[0], i, c[2]), lambda c: step(c, 2), step(init, SB_FIRST))
        o_r[...] = jnp.concatenate(accs, axis=1)

    return pl.pallas_call(
        body, name="sb_fwd", out_shape=jax.ShapeDtypeStruct((rows, width), f32), grid=(npair, nq),
        in_specs=[pl.BlockSpec((SB_BLOCK, lanes), lambda p, i: (i, p)),
                  pl.BlockSpec((rows, lanes), lambda p, i: (0, p)),
                  pl.BlockSpec((rows, lanes), lambda p, i: (0, npair + p))],
        out_specs=pl.BlockSpec((SB_BLOCK, lanes), lambda p, i: (i, p)),
        compiler_params=_params(("parallel", "arbitrary")),
    )(q, kv, kv)


def _sb_bwd(q, kv, do, width):
    rows = q.shape[0]
    nq = rows // SB_BLOCK
    npair = width // LANES
    nh = LANES // SB_DIM
    scale = SB_DIM ** -0.5

    def body(q_r, k_r, v_r, do_r, dq_r, dk_r, dv_r, e_sc, sig_sc, w_sc):
        i = pl.program_id(1)

        @pl.when(i == 0)
        def _():
            dk_r[...] = jnp.zeros_like(dk_r)
            dv_r[...] = jnp.zeros_like(dv_r)

        rj = lax.broadcasted_iota(i32, (SB_BLOCK, SB_BLOCK), 0)
        cs = lax.broadcasted_iota(i32, (SB_BLOCK, SB_BLOCK), 1)
        after = (rj > cs).astype(bf16)
        from_s = (rj >= cs).astype(bf16)
        zero1 = jnp.zeros((SB_BLOCK, 1), f32)
        sls = [slice(a * SB_DIM, (a + 1) * SB_DIM) for a in range(nh)]
        qs = [q_r[:, sl] for sl in sls]
        dos = [do_r[:, sl] for sl in sls]

        def weigh(carry, nb):
            t, runs, eruns = carry
            kss, _, units, logs, w, runs = _sb_weights(i, t, nb, qs, sls, k_r, runs, after, scale)
            dw = [_bdot(dos[a], v_r[kss[b], sls[a]], "nt") for a, b in units]
            e = [ww * d for ww, d in zip(w, dw)]
            for u, (a, b) in enumerate(units):
                e_sc[a, t + b] = e[u]
                sig_sc[a, t + b] = jnp.exp(logs[u][0])
                w_sc[a, t + b] = w[u].astype(w_sc.dtype)
            sums = [jnp.sum(ee, axis=1, keepdims=True) for ee in e]
            eruns = tuple(functools.reduce(jnp.add, [eruns[a]] + sums[a::nh]) for a in range(nh))
            return t + nb, runs, eruns

        n_blk, _, etots = lax.while_loop(lambda c: _sb_live(c[0], i, c[1]), lambda c: weigh(c, 2),
                                         weigh((jnp.int32(0), (zero1,) * nh, (zero1,) * nh), SB_FIRST))

        def push(t, carry, nb):
            dqs, eruns = carry
            kss, vis = _sb_blocks(i, t, nb)
            units = [(a, b) for b in range(nb) for a in range(nh)]
            e = [e_sc[a, t + b] for a, b in units]
            dvs = [_bdot(w_sc[a, t + b], dos[a], "tn") for a, b in units]
            upto = [_tri_sum(ee, from_s) for ee in e]
            sums = [jnp.sum(ee, axis=1, keepdims=True) for ee in e]
            dz = []
            eruns = list(eruns)
            for b in range(nb):
                for a in range(nh):
                    u = b * nh + a
                    sig = sig_sc[a, t + b]
                    before = etots[a] - eruns[a] - upto[u]
                    dz.append(jnp.where(vis[b], e[u] * (1.0 - sig) - before * sig, 0.0) * scale)
                eruns = [eruns[a] + sums[b * nh + a] for a in range(nh)]
            dks = [_bdot(d, qs[a], "tn") for d, (a, b) in zip(dz, units)]
            dqp = [_bdot(d, k_r[kss[b], sls[a]], "nn") for d, (a, b) in zip(dz, units)]
            for b in range(nb):
                dk_r[kss[b], :] += jnp.concatenate(dks[b * nh:(b + 1) * nh], axis=1)
                dv_r[kss[b], :] += jnp.concatenate(dvs[b * nh:(b + 1) * nh], axis=1)
            dqs = tuple(functools.reduce(jnp.add, [dqs[a]] + dqp[a::nh]) for a in range(nh))
            return dqs, tuple(eruns)

        first = push(jnp.int32(0), (tuple(jnp.zeros((SB_BLOCK, SB_DIM), f32) for _ in sls), (zero1,) * nh), SB_FIRST)
        dqs, _ = lax.fori_loop(0, (n_blk - SB_FIRST) // 2, lambda p, c: push(SB_FIRST + 2 * p, c, 2), first)
        dq_r[...] = jnp.concatenate(dqs, axis=1)

    blk = pl.BlockSpec((SB_BLOCK, LANES), lambda p, i: (i, p))
    col = pl.BlockSpec((rows, LANES), lambda p, i: (0, p))
    wide = jax.ShapeDtypeStruct((rows, width), f32)
    depth = nq + SB_FIRST
    return pl.pallas_call(
        body, name="sb_bwd", out_shape=[wide] * 3, grid=(npair, nq),
        in_specs=[blk, col, pl.BlockSpec((rows, LANES), lambda p, i: (0, npair + p)), blk],
        out_specs=[blk, col, col],
        scratch_shapes=[pltpu.VMEM((nh, depth, SB_BLOCK, SB_BLOCK), f32), pltpu.VMEM((nh, depth, SB_BLOCK, SB_BLOCK), f32),
                        pltpu.VMEM((nh, depth, SB_BLOCK, SB_BLOCK), bf16)],
        compiler_params=_params(("parallel", "arbitrary")),
    )(q, kv, kv, do)


_FLIPS = ((1, 0), (0, 1), (1, 1))
_ANY = pl.BlockSpec(memory_space=pl.ANY)


def _flip(v, a):
    return v + a - 2 * a * v


def _gather_plan(ins, outs, send_sems, recv_sems, local_sems):
    num = len(ins)
    x, y, c = lax.axis_index("x"), lax.axis_index("y"), lax.axis_index("c")
    me, sibling = (x, y, c), (x, y, 1 - c)
    chip = 2 * x + y
    others = [(_flip(x, a), _flip(y, b)) for a, b in _FLIPS]
    pairs = [(k, n, 2 * ox + oy) for k in range(num) for n, (ox, oy) in enumerate(others)]

    def half_of(ref, hc):
        half = ref.shape[0] // 2
        start = hc * half
        for align in (16, 8):
            if half % align == 0:
                start = pl.multiple_of(start, align)
                break
        return ref.at[pl.ds(start, half)]

    def copy(k, n, s, hc, to, src=None):
        dst = half_of(outs[k].at[s], hc)
        return pltpu.make_async_remote_copy(
            src_ref=dst if src is None else src, dst_ref=dst,
            send_sem=send_sems.at[6 * k + n], recv_sem=recv_sems.at[6 * k + n], device_id=to, device_id_type=MESH)

    mine = [pltpu.make_async_copy(ins[k], outs[k].at[chip], local_sems.at[k]) for k in range(num)]
    first = [copy(k, n, chip, c, (others[n][0], others[n][1], c), src=half_of(ins[k], c)) for k, n, _ in pairs]
    passed = [copy(k, 3 + n, s, c, sibling) for k, n, s in pairs]

    def start():
        for cp in mine + first:
            cp.start()

    def forward():
        for (k, n, s), fw in zip(pairs, passed):
            copy(k, n, s, c, me).wait_recv()
            fw.start()

    def finish():
        for k, n, s in pairs:
            copy(k, 3 + n, s, 1 - c, me).wait_recv()
        for cp in first + passed:
            cp.wait_send()
        for cp in mine:
            cp.wait()

    return start, forward, finish


def _gather_chips(shards):
    num = len(shards)

    def body(*refs):
        for phase in _gather_plan(refs[:num], refs[num:2 * num], *refs[2 * num:]):
            phase()

    return pl.pallas_call(
        body, name="gather_chips", out_shape=[jax.ShapeDtypeStruct((N_CHIPS,) + t.shape, t.dtype) for t in shards],
        in_specs=[_ANY] * num, out_specs=[_ANY] * num,
        scratch_shapes=[pltpu.SemaphoreType.DMA((6 * num,)), pltpu.SemaphoreType.DMA((6 * num,)),
                        pltpu.SemaphoreType.DMA((num,))],
    )(*shards)


def _scatter_plan(ins, outs, send_sems, recv_sems):
    x, y, c = lax.axis_index("x"), lax.axis_index("y"), lax.axis_index("c")
    cps = []
    for k in range(len(ins)):
        for n, (a, b) in enumerate(_FLIPS):
            ox, oy = _flip(x, a), _flip(y, b)
            cps.append(pltpu.make_async_remote_copy(
                src_ref=ins[k].at[2 * ox + oy], dst_ref=outs[k].at[n], send_sem=send_sems.at[3 * k + n],
                recv_sem=recv_sems.at[3 * k + n], device_id=(ox, oy, c), device_id_type=MESH))

    def start():
        for cp in cps:
            cp.start()

    def finish():
        for cp in cps:
            cp.wait()

    return start, finish


def _scatter_chips(parts):
    num = len(parts)

    def body(*refs):
        for phase in _scatter_plan(refs[:num], refs[num:2 * num], *refs[2 * num:]):
            phase()

    return pl.pallas_call(
        body, name="scatter_chips", out_shape=[jax.ShapeDtypeStruct((3,) + t.shape[1:], t.dtype) for t in parts],
        in_specs=[_ANY] * num, out_specs=[_ANY] * num,
        scratch_shapes=[pltpu.SemaphoreType.DMA((3 * num,)), pltpu.SemaphoreType.DMA((3 * num,))],
    )(*parts)


def _swap_sibling(arrs):
    num = len(arrs)

    def body(*refs):
        ins, outs = refs[:num], refs[num:2 * num]
        send_sems, recv_sems = refs[2 * num:]
        x, y, c = lax.axis_index("x"), lax.axis_index("y"), lax.axis_index("c")
        cps = [pltpu.make_async_remote_copy(src_ref=ins[k], dst_ref=outs[k], send_sem=send_sems.at[k],
                                            recv_sem=recv_sems.at[k], device_id=(x, y, 1 - c), device_id_type=MESH)
               for k in range(num)]
        for cp in cps:
            cp.start()
        for cp in cps:
            cp.wait()

    return pl.pallas_call(
        body, name="swap_sibling", out_shape=[jax.ShapeDtypeStruct(t.shape, t.dtype) for t in arrs],
        in_specs=[_ANY] * num, out_specs=[_ANY] * num,
        scratch_shapes=[pltpu.SemaphoreType.DMA((num,)), pltpu.SemaphoreType.DMA((num,))],
    )(*arrs)


def _gather_all(v):
    m_per, n = v.shape

    def body(x_ref, out_ref, send_sems, recv_sems, local_sem):
        x, y, c = lax.axis_index("x"), lax.axis_index("y"), lax.axis_index("c")
        me, sibling = (x, y, c), (x, y, 1 - c)
        chips = [(_flip(x, a), _flip(y, b)) for a, b in _FLIPS]

        def rows(px, py, pc):
            return out_ref.at[pl.ds(pl.multiple_of((4 * px + 2 * py + pc) * m_per, 8), m_per), :]

        def copy(k, block, to, src=None):
            return pltpu.make_async_remote_copy(
                src_ref=rows(*block) if src is None else src, dst_ref=rows(*block),
                send_sem=send_sems.at[k], recv_sem=recv_sems.at[k], device_id=to, device_id_type=MESH)

        mine = pltpu.make_async_copy(x_ref, rows(*me), local_sem)
        mine.start()
        first = [copy(0, me, sibling, src=x_ref)]
        first += [copy(1 + j, me, (*chip, c), src=x_ref) for j, chip in enumerate(chips)]
        for cp in first:
            cp.start()
        passed = [copy(4 + j, (*chip, c), sibling) for j, chip in enumerate(chips)]
        for j, chip in enumerate(chips):
            copy(1 + j, (*chip, c), me).wait_recv()
            passed[j].start()
        copy(0, sibling, me).wait_recv()
        for j, chip in enumerate(chips):
            copy(4 + j, (*chip, 1 - c), me).wait_recv()
        for cp in first + passed:
            cp.wait_send()
        mine.wait()

    return pl.pallas_call(
        body, name="gather_all", out_shape=jax.ShapeDtypeStruct((N_DEV * m_per, n), v.dtype),
        in_specs=[pl.BlockSpec(memory_space=pltpu.VMEM)], out_specs=pl.BlockSpec(memory_space=pltpu.VMEM),
        scratch_shapes=[pltpu.SemaphoreType.DMA((7,)), pltpu.SemaphoreType.DMA((7,)), pltpu.SemaphoreType.DMA],
    )(v)


def _sum_chips(parts, got, chip, name):
    cols = parts.shape[-1]
    rows = parts.size // (N_CHIPS * cols)
    tm = _pick(rows, (256, 128, 64, 32, 16))

    def body(chip_r, own_r, got_r, o_r):
        acc = own_r[0]
        for n in range(3):
            acc = acc + got_r[n].astype(f32)
        o_r[...] = acc

    return pl.pallas_call(
        body, name=name, out_shape=jax.ShapeDtypeStruct((rows, cols), f32),
        grid_spec=pltpu.PrefetchScalarGridSpec(
            num_scalar_prefetch=1, grid=(rows // tm,),
            in_specs=[pl.BlockSpec((1, tm, cols), lambda i, s: (s[0], i, 0)),
                      pl.BlockSpec((3, tm, cols), lambda i, s: (0, i, 0))],
            out_specs=pl.BlockSpec((tm, cols), lambda i, s: (i, 0))),
        compiler_params=_params(("parallel",)),
    )(chip, parts.reshape(N_CHIPS, rows, cols), got.reshape(3, rows, cols))


def _sum_devices(g, m_per):
    n = g.shape[1]

    def body(g_r, o_r):
        acc = g_r[0:m_per, :]
        for d in range(1, N_DEV):
            acc = acc + g_r[d * m_per:(d + 1) * m_per, :]
        o_r[...] = acc

    return pl.pallas_call(body, name="sum_devices", out_shape=jax.ShapeDtypeStruct((m_per, n), f32))(g)


def _adamw(w, gs, m, v, name):
    shape = w.shape
    cols = shape[-1]
    rows = w.size // cols
    tm = _pick(rows, (256, 128, 64, 32, 16, 8)) if rows * cols * 4 > (1 << 20) else rows

    def fn(i, wv, mv, vv, *gv):
        g = functools.reduce(jnp.add, gv)
        mn = ADAM_B1 * mv + (1.0 - ADAM_B1) * g
        vn = ADAM_B2 * vv + (1.0 - ADAM_B2) * jnp.square(g)
        m_hat = mn / (1.0 - ADAM_B1 ** ADAM_STEP)
        v_hat = vn / (1.0 - ADAM_B2 ** ADAM_STEP)
        delta = -ADAM_LR * (m_hat / (jnp.sqrt(v_hat) + ADAM_EPS) + ADAM_WD * wv)
        return g, delta, mn, vn

    outs = _rowwise(fn, name, rows, tm, [(t.reshape(rows, cols), (cols, 0)) for t in (w, m, v) + tuple(gs)], [(cols, f32)] * 4)
    return tuple(o.reshape(shape) for o in outs)


def _pack(pieces, rows, dtype):
    flat = jnp.concatenate([p.reshape(-1).astype(dtype) for p in pieces])
    return jnp.pad(flat, (0, rows * PACK_COLS - flat.size)).reshape(rows, PACK_COLS)


def _unpack(buf, shapes):
    lead = buf.shape[:-2]
    flat = buf.reshape(lead + (-1,))
    out, off = [], 0
    for s in shapes:
        n = 1
        for d in s:
            n *= d
        out.append(flat[..., off:off + n].reshape(lead + tuple(s)))
        off += n
    return out


def _join_cols(t):
    return jnp.moveaxis(t, 0, -2).reshape(t.shape[1:-1] + (N_CHIPS * t.shape[-1],))


def _join_rows(t):
    return t.reshape((N_CHIPS * t.shape[1],) + t.shape[2:])


def _split_cols(t, parts=N_CHIPS):
    r, cols = t.shape
    return jnp.moveaxis(t.reshape(r, parts, cols // parts), 1, 0)


def _split_rows(t):
    return t.reshape((N_CHIPS, t.shape[0] // N_CHIPS) + t.shape[1:])


def kernel(x, meta_tokens, gdn_norm_g, gdn_w_in, gdn_conv_w, gdn_a_log, gdn_dt_bias, gdn_onorm_g, gdn_w_out, kv_norm_g, w_kv, sb_norm_g, sb_w_q, sb_w_o, ffn_norm_g, ffn_w_gate_up, ffn_w_down, final_norm_g, loss_target, m_meta_tokens, m_gdn_norm_g, m_gdn_w_in, m_gdn_conv_w, m_gdn_a_log, m_gdn_dt_bias, m_gdn_onorm_g, m_gdn_w_out, m_kv_norm_g, m_w_kv, m_sb_norm_g, m_sb_w_q, m_sb_w_o, m_ffn_norm_g, m_ffn_w_gate_up, m_ffn_w_down, m_final_norm_g, v_meta_tokens, v_gdn_norm_g, v_gdn_w_in, v_gdn_conv_w, v_gdn_a_log, v_gdn_dt_bias, v_gdn_onorm_g, v_gdn_w_out, v_kv_norm_g, v_w_kv, v_sb_norm_g, v_sb_w_q, v_sb_w_o, v_ffn_norm_g, v_ffn_w_gate_up, v_ffn_w_down, v_final_norm_g):
    weights = dict(meta_tokens=meta_tokens, gdn_norm_g=gdn_norm_g, gdn_w_in=gdn_w_in, gdn_conv_w=gdn_conv_w,
                   gdn_a_log=gdn_a_log, gdn_dt_bias=gdn_dt_bias, gdn_onorm_g=gdn_onorm_g, gdn_w_out=gdn_w_out,
                   kv_norm_g=kv_norm_g, w_kv=w_kv, sb_norm_g=sb_norm_g, sb_w_q=sb_w_q, sb_w_o=sb_w_o,
                   ffn_norm_g=ffn_norm_g, ffn_w_gate_up=ffn_w_gate_up, ffn_w_down=ffn_w_down, final_norm_g=final_norm_g)
    m_in = dict(meta_tokens=m_meta_tokens, gdn_norm_g=m_gdn_norm_g, gdn_w_in=m_gdn_w_in, gdn_conv_w=m_gdn_conv_w,
                gdn_a_log=m_gdn_a_log, gdn_dt_bias=m_gdn_dt_bias, gdn_onorm_g=m_gdn_onorm_g, gdn_w_out=m_gdn_w_out,
                kv_norm_g=m_kv_norm_g, w_kv=m_w_kv, sb_norm_g=m_sb_norm_g, sb_w_q=m_sb_w_q, sb_w_o=m_sb_w_o,
                ffn_norm_g=m_ffn_norm_g, ffn_w_gate_up=m_ffn_w_gate_up, ffn_w_down=m_ffn_w_down, final_norm_g=m_final_norm_g)
    v_in = dict(meta_tokens=v_meta_tokens, gdn_norm_g=v_gdn_norm_g, gdn_w_in=v_gdn_w_in, gdn_conv_w=v_gdn_conv_w,
                gdn_a_log=v_gdn_a_log, gdn_dt_bias=v_gdn_dt_bias, gdn_onorm_g=v_gdn_onorm_g, gdn_w_out=v_gdn_w_out,
                kv_norm_g=v_kv_norm_g, w_kv=v_w_kv, sb_norm_g=v_sb_norm_g, sb_w_q=v_sb_w_q, sb_w_o=v_sb_w_o,
                ffn_norm_g=v_ffn_norm_g, ffn_w_gate_up=v_ffn_w_gate_up, ffn_w_down=v_ffn_w_down, final_norm_g=v_final_norm_g)
    names = list(weights)

    seq, d = x.shape[1], x.shape[2]
    lo_frames = FRONT + N_META
    used = lo_frames + seq
    rows = -(-used // SB_BLOCK) * SB_BLOCK
    tm = _pick(rows, (640, 512, 384, 256, 128))
    tp = _pick(rows, (320, 256, 128))
    n_ffn = ffn_w_gate_up.shape[0]
    sb_width = sb_w_q.shape[2]
    chip =2 * lax.axis_index("x") + lax.axis_index("y")

    big = [gdn_w_in[0], gdn_w_out[0], w_kv, sb_w_q[0], sb_w_o[0], ffn_w_gate_up, ffn_w_down]
    small = [meta_tokens, gdn_norm_g, gdn_conv_w[0]]
    n_early = 2
    big_bf16 = [t.astype(bf16) for t in big]
    w_in_s, w_out_s, small_g = _gather_chips(big_bf16[:n_early] + [_pack(small, 16, f32)])
    small_s = _unpack(small_g, [t.shape for t in small])
    w_in = _join_cols(w_in_s)
    pad_ab = jnp.zeros((d, LANES - GDN_HEADS), bf16)
    w_in_ext = jnp.concatenate([w_in[:, :4 * GDN_WIDTH], w_in[:, 4 * GDN_WIDTH:4 * GDN_WIDTH + GDN_HEADS], pad_ab,
                                w_in[:, 4 * GDN_WIDTH + GDN_HEADS:], pad_ab], axis=1)
    w_out = _join_rows(w_out_s)
    meta_full, gdn_g_full, conv_full = (_join_cols(t) for t in small_s)

    zeros = lambda n: jnp.zeros((n, d), f32)
    h0 = jnp.concatenate([zeros(FRONT), meta_full, x[0], zeros(rows - used)], axis=0)
    tgt = jnp.concatenate([zeros(lo_frames), loss_target[0], zeros(rows - used)], axis=0)
    pad8 = lambda t: jnp.pad(t, ((0, 0), (0, LANES - t.shape[1])))
    a_log8, dt_bias8 = pad8(gdn_a_log), pad8(gdn_dt_bias)
    r_i = jnp.arange(tp)
    ltri = ((r_i[:, None] >= r_i[None, :]) & (r_i[:, None] // CHUNK == r_i[None, :] // CHUNK)).astype(f32)
    ffn_g = [ffn_norm_g[l:l + 1] for l in range(n_ffn)]
    kv_g, fin_g = kv_norm_g.reshape(1, d), final_norm_g.reshape(1, d)

    n0 = _rms_fwd(h0, gdn_g_full, "gdn_norm")
    proj = _matmul(n0, w_in_ext, "nn", "gdn_proj")
    gq, gk, gv, gw, bw = _gdn_prep_fwd(proj, conv_full, a_log8, dt_bias8, ltri, FRONT, used, tp)
    g_o, g_states, (w_kv_s, w_q_s, w_o_s, w_gu_s, w_dn_s) = _gdn_fwd(gq, gk, gv, gw, bw, big_bf16[n_early:])
    w_kvf = _join_cols(w_kv_s)
    w_q = _join_rows(w_q_s)
    w_o = _join_rows(w_o_s)
    w_gu = [_join_cols(w_gu_s[:, l]) for l in range(n_ffn)]
    w_dn = [_join_rows(w_dn_s[:, l]) for l in range(n_ffn)]
    og = _gdn_gate_fwd(g_o, proj, gdn_onorm_g, tm)
    h1 = _matmul(og, w_out, "nn", "gdn_out", res=h0)

    def ffn_fwd(h, l):
        n = _rms_fwd(h, ffn_g[l], f"ffn{l}_norm")
        gate, up, act = _ffn_up(n, w_gu[l], f"ffn{l}_gate_up")
        return _matmul(act, w_dn[l], "nn", f"ffn{l}_down", res=h), (n, gate, up, act)

    h2, ffn0_saved = ffn_fwd(h1, 0)
    n_kv = _rms_fwd(h2, kv_g, "kv_norm")
    kv = _matmul(n_kv, w_kvf, "nn", "kv_proj", out_dtype=bf16)
    n_sb = _rms_fwd(h2, sb_norm_g, "sb_norm")
    sq = _matmul(n_sb, w_q, "nn", "q_proj", out_dtype=bf16)
    s_o = _sb_fwd(sq, kv, sb_width)
    h3 = _matmul(s_o, w_o, "nn", "sb_out", res=h2)
    h4, ffn1_saved = ffn_fwd(h3, 1)
    dh4, d_fin_g, loss_part = _loss_head(h4, fin_g, tgt, lo_frames, used, "loss_head")

    def ffn_bwd(dh, h, l, saved):
        n, gate, up, act = saved
        d_wdn = _matmul(act, dh, "tn", f"ffn{l}_d_w_down")
        d_gate, d_up = _ffn_dact(dh, w_dn[l], gate, up, f"ffn{l}_d_gate_up")
        d_wgu = jnp.concatenate([_split_cols(_matmul(n, d_gate, "tn", f"ffn{l}_d_w_gate"), N_CHIPS // 2),
                                 _split_cols(_matmul(n, d_up, "tn", f"ffn{l}_d_w_up"), N_CHIPS // 2)], axis=0)
        dh_in, dg = _norm_bwd([d_gate, d_up], w_gu[l], h, ffn_g[l], dh, f"ffn{l}_d_norm")
        return dh_in, d_wgu, d_wdn, dg

    dh3, d_wgu1, d_wdn1, d_ffn_g1 = ffn_bwd(dh4, h3, 1, ffn1_saved)
    d_wo = _matmul(s_o, dh3, "tn", "d_w_o")
    d_so = _matmul(dh3, w_o, "nt", "d_sb_o")
    d_sq, d_sk, d_sv = _sb_bwd(sq, kv, d_so, sb_width)
    d_wq = _matmul(n_sb, d_sq, "tn", "d_w_q")
    dh2, d_sb_g = _norm_bwd([d_sq], w_q, h2, sb_norm_g, dh3, "d_sb_norm")
    d_wkv = jnp.concatenate([_matmul(n_kv, d_sk, "tn", "d_w_k"), _matmul(n_kv, d_sv, "tn", "d_w_v")], axis=1)
    dh2, d_kv_g = _norm_bwd([d_sk, d_sv], w_kvf, h2, kv_g, dh2, "d_kv_norm")
    dh1, d_wgu0, d_wdn0, d_ffn_g0 = ffn_bwd(dh2, h1, 0, ffn0_saved)
    d_wout = _matmul(og, dh1, "tn", "d_w_out")
    d_og = _matmul(dh1, w_out, "nt", "d_gdn_gated")
    d_go, d_gate, d_onorm = _gdn_gate_bwd(g_o, proj, gdn_onorm_g, d_og, tm)
    by_chip = [None, _split_rows(d_wout), _split_cols(d_wkv), _split_rows(d_wq), _split_rows(d_wo),
               jnp.stack([d_wgu0, d_wgu1], axis=1),
               jnp.stack([_split_rows(d_wdn0), _split_rows(d_wdn1)], axis=1)]
    (d_gq, d_gk, d_gv, d_gw, d_bw), got_early = _gdn_bwd(gq, gk, gv, gw, bw, g_states, d_go,
                                                         [t.astype(bf16) for t in by_chip[1:]])
    dconv, d_a_in, d_b_in, d_a_log8, d_dt_bias8 = _gdn_prep_bwd_act(
        proj, conv_full, a_log8, dt_bias8, ltri, d_gq, d_gk, d_gv, d_gw, d_bw, FRONT, used, tp)
    dproj, d_conv = _gdn_prep_bwd_conv(proj, conv_full, dconv, d_gate, d_a_in, d_b_in, tp)
    d_win_ext = _matmul(n0, dproj, "tn", "d_w_in")
    dh0, d_gdn_g = _norm_bwd([dproj], w_in_ext, h0, gdn_g_full, dh1, "d_gdn_norm")
    grad_x = dh0[lo_frames:used][None]
    d_win = jnp.concatenate([d_win_ext[:, :4 * GDN_WIDTH], d_win_ext[:, 4 * GDN_WIDTH:4 * GDN_WIDTH + GDN_HEADS],
                             d_win_ext[:, 4 * GDN_WIDTH + LANES:4 * GDN_WIDTH + LANES + GDN_HEADS]], axis=1)

    by_chip[0] = _split_cols(d_win)
    got = list(_scatter_chips([by_chip[0].astype(bf16)])) + list(got_early)
    chip_arr = jnp.reshape(chip, (1,)).astype(i32)
    over_chips = [_sum_chips(t, g, chip_arr, f"sum_chips_{k}") for k, (t, g) in enumerate(zip(by_chip, got))]
    over_sibling = _swap_sibling(over_chips)
    big_names = ["gdn_w_in", "gdn_w_out", "w_kv", "sb_w_q", "sb_w_o", "ffn_w_gate_up", "ffn_w_down"]
    g_big = dict(zip(big_names, zip(over_chips, over_sibling)))

    small_parts = [dh0[FRONT:lo_frames], d_gdn_g, d_conv, d_a_log8, d_dt_bias8, d_onorm, d_kv_g, d_sb_g,
                   d_ffn_g0, d_ffn_g1, d_fin_g, loss_part]
    s_rows = -(-sum(t.size for t in small_parts) // (8 * PACK_COLS)) * 8
    s_sum = _sum_devices(_gather_all(_pack(small_parts, s_rows, f32)), s_rows)
    (g_meta, g_gdn_g, g_conv, g_a_log8, g_dt8, g_onorm, g_kv_g, g_sb_g, g_ffn_g0, g_ffn_g1, g_fin_g,
     loss_v) = _unpack(s_sum, [t.shape for t in small_parts])
    col_shard = lambda t, w: lax.dynamic_slice_in_dim(t, chip * w, w, axis=t.ndim - 1)

    g_small = dict(
        meta_tokens=col_shard(g_meta, meta_tokens.shape[1]), gdn_norm_g=col_shard(g_gdn_g, gdn_norm_g.shape[1]),
        gdn_conv_w=col_shard(g_conv, gdn_conv_w.shape[2])[None],
        gdn_a_log=g_a_log8[:, :GDN_HEADS], gdn_dt_bias=g_dt8[:, :GDN_HEADS], gdn_onorm_g=g_onorm,
        kv_norm_g=g_kv_g.reshape(-1), sb_norm_g=g_sb_g, ffn_norm_g=jnp.concatenate([g_ffn_g0, g_ffn_g1], axis=0),
        final_norm_g=g_fin_g.reshape(-1))

    grads, delta, new_m, new_v = {}, {}, {}, {}
    for n in names:
        gs = g_big[n] if n in g_big else (g_small[n],)
        grads[n], delta[n], new_m[n], new_v[n] = _adamw(weights[n], gs, m_in[n], v_in[n], f"adamw_{n}")
    loss = loss_v[0, 0]
    return (loss, grad_x, *[grads[n] for n in names], *[delta[n] for n in names],
            *[new_m[n] for n in names], *[new_v[n] for n in names])
```

```python
import functools

import jax
import jax.numpy as jnp
from jax import lax
from jax.experimental import pallas as pl
from jax.experimental.pallas import tpu as pltpu

f32 = jnp.float32
bf16 = jnp.bfloat16
i32 = jnp.int32

EPS = 1e-6
N_META = 16
CHUNK = 64
FRONT = (-N_META) % CHUNK
GDN_HEADS = 8
GDN_DIM = 128
GDN_WIDTH = GDN_HEADS * GDN_DIM
CONV_WIDTH = 4
SB_DIM = 64
SB_BLOCK = 128
SB_FWD_HEADS = 4
SB_FIRST = 3
SB_UNDERFLOW = 104.0
LANES = 128
PACK_COLS = 1024
N_CHIPS = 4
N_DEV = 8
ADAM_LR, ADAM_B1, ADAM_B2, ADAM_EPS, ADAM_WD, ADAM_STEP = 0.001, 0.9, 0.999, 1e-08, 0.01, 10
VMEM_LIMIT = 56 * 1024 * 1024
MESH = pl.DeviceIdType.MESH


def _pick(n, prefs):
    for p in prefs:
        if n % p == 0:
            return p
    return n


def _params(sem):
    return pltpu.CompilerParams(dimension_semantics=sem, vmem_limit_bytes=VMEM_LIMIT)


_DIMS = {"nn": ((1,), (0,)), "nt": ((1,), (1,)), "tn": ((0,), (0,))}


def _bdot(a, b, mode):
    return lax.dot_general(a.astype(bf16), b.astype(bf16), (_DIMS[mode], ((), ())), preferred_element_type=f32)


def _matmul(a, b, mode, name, res=None, out_dtype=f32):
    if mode == "nn":
        (m, k), n = a.shape, b.shape[1]
    elif mode == "nt":
        (m, k), n = a.shape, b.shape[0]
    else:
        (k, m), n = a.shape, b.shape[1]
    tm = _pick(m, (640, 1408, 1024, 512, 384, 256, 128))
    tn = _pick(n, (1408, 2176, 1024, 512, 384, 256, 128))
    tk = _pick(k, (1664, 1408, 2176, 1024, 640, 512, 384, 256, 128))
    nk = k // tk
    a_spec = pl.BlockSpec((tk, tm), lambda j, i, kk: (kk, i)) if mode == "tn" else pl.BlockSpec((tm, tk), lambda j, i, kk: (i, kk))
    b_spec = pl.BlockSpec((tn, tk), lambda j, i, kk: (j, kk)) if mode == "nt" else pl.BlockSpec((tk, tn), lambda j, i, kk: (kk, j))
    o_spec = pl.BlockSpec((tm, tn), lambda j, i, kk: (i, j))
    has_res = res is not None

    def body(*refs):
        a_ref, b_ref = refs[:2]
        r_ref = refs[2] if has_res else None
        o_ref = refs[3] if has_res else refs[2]

        def finish(y):
            if has_res:
                y = y + r_ref[...]
            o_ref[...] = y.astype(o_ref.dtype)

        if nk == 1:
            finish(_bdot(a_ref[...], b_ref[...], mode))
            return
        acc = refs[-1]
        kk = pl.program_id(2)
        part = _bdot(a_ref[...], b_ref[...], mode)

        @pl.when(kk == 0)
        def _():
            acc[...] = part

        @pl.when((kk > 0) & (kk < nk - 1))
        def _():
            acc[...] += part

        @pl.when(kk == nk - 1)
        def _():
            finish(acc[...] + part)

    ins = [a, b] + ([res] if has_res else [])
    specs = [a_spec, b_spec] + ([o_spec] if has_res else [])
    return pl.pallas_call(
        body, name=name, out_shape=jax.ShapeDtypeStruct((m, n), out_dtype), grid=(n // tn, m // tm, nk),
        in_specs=specs, out_specs=o_spec, scratch_shapes=[pltpu.VMEM((tm, tn), f32)] if nk > 1 else [],
        compiler_params=_params(("parallel", "parallel", "arbitrary")),
    )(*ins)


def _rowwise(fn, name, rows, tm, ins, outs, reds=()):
    n_in, n_out, n_red = len(ins), len(outs), len(reds)
    in_specs = []
    for arr, spec in ins:
        if spec is None:
            in_specs.append(pl.BlockSpec(arr.shape, lambda i, nd=arr.ndim: (0,) * nd))
        else:
            w, cb = spec
            in_specs.append(pl.BlockSpec((tm, w), lambda i, cb=cb: (i, cb)))
    out_specs = [pl.BlockSpec((tm, w), lambda i: (i, 0)) for w, _ in outs]
    out_specs += [pl.BlockSpec(s, lambda i, nd=len(s): (0,) * nd) for s in reds]
    out_shape = [jax.ShapeDtypeStruct((rows, w), dt) for w, dt in outs]
    out_shape += [jax.ShapeDtypeStruct(s, f32) for s in reds]

    def body(*refs):
        i = pl.program_id(0)
        vals = fn(i, *[r[...] for r in refs[:n_in]])
        for r, v in zip(refs[n_in:n_in + n_out], vals[:n_out]):
            r[...] = v.astype(r.dtype)
        red_refs = refs[n_in + n_out:]

        @pl.when(i == 0)
        def _():
            for r in red_refs:
                r[...] = jnp.zeros_like(r)

        for r, v in zip(red_refs, vals[n_out:]):
            r[...] += v

    res = pl.pallas_call(
        body, name=name, out_shape=out_shape, grid=(rows // tm,), in_specs=in_specs, out_specs=out_specs,
        compiler_params=_params(("arbitrary",)),
    )(*[a for a, _ in ins])
    return res


def _rms(x, g):
    return x * lax.rsqrt(jnp.mean(x * x, axis=-1, keepdims=True) + EPS) * g


def _row_mask(i, tm, lo, hi, shape):
    r = i * tm + lax.broadcasted_iota(i32, shape, 0)
    return (r >= lo) & (r < hi)


def _rms_fwd(x, g, name):
    rows, d = x.shape
    tm = _pick(rows, (640, 512, 384, 256, 128))
    return _rowwise(lambda i, xv, gv: (_rms(xv, gv),), name, rows, tm, [(x, (d, 0)), (g, None)], [(d, bf16)])[0]


def _norm_bwd(parts, w, x, g, res, name):
    rows, k = parts[0].shape
    d = w.shape[0]
    num = len(parts)
    tm = _pick(rows, (640, 512, 384, 256, 128))
    tk = _pick(k, (1408, 2176, 1024, 512, 384, 256, 128))
    nk = k // tk

    def body(*refs):
        a_refs, w_refs = refs[:num], refs[num:2 * num]
        x_r, g_r, r_r, o_r, dg_r, acc = refs[2 * num:]
        i, kk = pl.program_id(0), pl.program_id(1)
        part = functools.reduce(jnp.add, [_bdot(a[...], b[...], "nt") for a, b in zip(a_refs, w_refs)])

        @pl.when((i == 0) & (kk == 0))
        def _():
            dg_r[...] = jnp.zeros_like(dg_r)

        def finish(dn):
            _, vjp = jax.vjp(_rms, x_r[...], g_r[...])
            dx, dg = vjp(dn)
            o_r[...] = r_r[...] + dx
            dg_r[...] += dg

        if nk == 1:
            finish(part)
            return

        @pl.when(kk == 0)
        def _():
            acc[...] = part

        @pl.when((kk > 0) & (kk < nk - 1))
        def _():
            acc[...] += part

        @pl.when(kk == nk - 1)
        def _():
            finish(acc[...] + part)

    row = pl.BlockSpec((tm, d), lambda i, kk: (i, 0))
    one = pl.BlockSpec((1, d), lambda i, kk: (0, 0))
    return pl.pallas_call(
        body, name=name, out_shape=[jax.ShapeDtypeStruct((rows, d), f32), jax.ShapeDtypeStruct((1, d), f32)],
        grid=(rows // tm, nk),
        in_specs=[pl.BlockSpec((tm, tk), lambda i, kk: (i, kk))] * num
        + [pl.BlockSpec((d, tk), lambda i, kk, p=p: (0, p * nk + kk)) for p in range(num)] + [row, one, row],
        out_specs=[row, one], scratch_shapes=[pltpu.VMEM((tm, d), f32)],
        compiler_params=_params(("arbitrary", "arbitrary")),
    )(*parts, *([w] * num), x, g, res)


def _swiglu(gate, up):
    return jax.nn.silu(gate) * up


def _ffn_up(n, w_gu, name):
    rows, d = n.shape
    f = w_gu.shape[1] // 2
    tm = _pick(rows, (640, 512, 384, 256, 128))
    tn = _pick(f, (1408, 1024, 512, 384, 256, 128))
    nj = f // tn

    def body(n_r, wg_r, wu_r, g_r, u_r, a_r):
        g = jnp.dot(n_r[...], wg_r[...], preferred_element_type=f32)
        u = jnp.dot(n_r[...], wu_r[...], preferred_element_type=f32)
        g_r[...] = g.astype(g_r.dtype)
        u_r[...] = u.astype(u_r.dtype)
        a_r[...] = _swiglu(g, u).astype(a_r.dtype)

    o_spec = pl.BlockSpec((tm, tn), lambda j, i: (i, j))
    return pl.pallas_call(
        body, name=name, grid=(nj, rows // tm), out_shape=[jax.ShapeDtypeStruct((rows, f), bf16)] * 3,
        in_specs=[pl.BlockSpec((tm, d), lambda j, i: (i, 0)), pl.BlockSpec((d, tn), lambda j, i: (0, j)),
                  pl.BlockSpec((d, tn), lambda j, i: (0, nj + j))],
        out_specs=[o_spec] * 3, compiler_params=_params(("parallel", "parallel")),
    )(n, w_gu, w_gu)


def _ffn_dact(dh, w_dn, gate, up, name):
    rows, d = dh.shape
    f = w_dn.shape[0]
    tm = _pick(rows, (640, 512, 384, 256, 128))
    tn = _pick(f, (1408, 1024, 512, 384, 256, 128))

    def body(dh_r, w_r, g_r, u_r, dg_r, du_r):
        dact = _bdot(dh_r[...], w_r[...], "nt")
        _, vjp = jax.vjp(_swiglu, g_r[...].astype(f32), u_r[...].astype(f32))
        dg, du = vjp(dact)
        dg_r[...] = dg.astype(dg_r.dtype)
        du_r[...] = du.astype(du_r.dtype)

    t_spec = pl.BlockSpec((tm, tn), lambda j, i: (i, j))
    return pl.pallas_call(
        body, name=name, grid=(f // tn, rows // tm), out_shape=[jax.ShapeDtypeStruct((rows, f), bf16)] * 2,
        in_specs=[pl.BlockSpec((tm, d), lambda j, i: (i, 0)), pl.BlockSpec((tn, d), lambda j, i: (j, 0)), t_spec, t_spec],
        out_specs=[t_spec] * 2, compiler_params=_params(("parallel", "parallel")),
    )(dh, w_dn, gate, up)


def _loss_head(h, g, tgt, lo, hi, name):
    rows, d = h.shape
    tm = _pick(rows, (640, 512, 384, 256, 128))

    def fn(i, hv, gv, tv):
        mask = _row_mask(i, tm, lo, hi, (tm, 1))

        def f(hh, gg):
            err = _rms(hh, gg) - tv
            per_row = jnp.where(mask, jnp.mean(err * err, axis=-1, keepdims=True), 0.0)
            return 0.5 * jnp.sum(per_row, axis=0, keepdims=True)

        loss, vjp = jax.vjp(f, hv, gv)
        dh, dg = vjp(jnp.ones_like(loss))
        return dh, dg, jnp.broadcast_to(loss, (1, LANES))

    return _rowwise(fn, name, rows, tm, [(h, (d, 0)), (g, None), (tgt, (d, 0))], [(d, f32)], [(1, d), (1, LANES)])


def _heads_l2(x):
    heads = [x[:, h * GDN_DIM:(h + 1) * GDN_DIM] for h in range(GDN_HEADS)]
    return jnp.concatenate([xh * lax.rsqrt(jnp.sum(xh * xh, axis=-1, keepdims=True) + EPS) for xh in heads], axis=1)


def _gdn_act(conv, a_in, b_in, a_log, dt_bias, mask):
    s = jax.nn.silu(conv)
    q = _heads_l2(s[:, :GDN_WIDTH])
    k = _heads_l2(s[:, GDN_WIDTH:2 * GDN_WIDTH])
    v = s[:, 2 * GDN_WIDTH:]
    g = jnp.where(mask, -jnp.exp(a_log) * jax.nn.softplus(a_in + dt_bias), 0.0)
    beta = jnp.where(mask, jax.nn.sigmoid(b_in), 0.0)
    return q, k, v, g, beta


def _widen(x8):
    return [jnp.broadcast_to(x8[:, h:h + 1], (x8.shape[0], GDN_DIM)) for h in range(GDN_HEADS)]


def _narrow(per_head):
    t = per_head[0].shape[0]
    lane = lax.broadcasted_iota(i32, (t, LANES), 1)
    out = jnp.zeros((t, LANES), f32)
    for h, x in enumerate(per_head):
        out = out + jnp.where(lane == h, jnp.sum(x, axis=1, keepdims=True), 0.0)
    return out


def _conv_taps(cur, prev8, w):
    tm = cur.shape[0]
    cat = jnp.concatenate([prev8, cur], axis=0)
    y = cur * w[CONV_WIDTH - 1:CONV_WIDTH, :]
    for j in range(1, CONV_WIDTH):
        y = y + pltpu.roll(cat, j, axis=0)[8:8 + tm, :] * w[CONV_WIDTH - 1 - j:CONV_WIDTH - j, :]
    return y


def _gdn_prep_specs(proj, tm):
    c3 = 3 * GDN_WIDTH
    ab = 4 * GDN_WIDTH // LANES
    t8 = tm // 8
    return [
        pl.BlockSpec((tm, c3), lambda i: (i, 0)),
        pl.BlockSpec((8, c3), lambda i: (jnp.maximum(i * t8 - 1, 0), 0)),
        pl.BlockSpec((tm, LANES), lambda i: (i, ab)),
        pl.BlockSpec((tm, LANES), lambda i: (i, ab + 1)),
    ]


def _full(arr):
    return pl.BlockSpec(arr.shape, lambda i, nd=arr.ndim: (0,) * nd)


def _gdn_prep_fwd(proj, conv_w, a_log, dt_bias, ltri, lo, hi, tm):
    rows = proj.shape[0]

    def body(cur, prev8, a_in, b_in, w, al, dtb, lt, q_o, k_o, v_o, g_o, b_o):
        i = pl.program_id(0)
        mask = _row_mask(i, tm, lo, hi, (tm, LANES)) & (lax.broadcasted_iota(i32, (tm, LANES), 1) < GDN_HEADS)
        conv = _conv_taps(cur[...], prev8[...], w[...])
        q, k, v, g, beta = _gdn_act(conv, a_in[...], b_in[...], al[...], dtb[...], mask)
        q_o[...] = q
        k_o[...] = k
        v_o[...] = v
        gcum = jnp.dot(lt[...], g, preferred_element_type=f32, precision=lax.Precision.HIGHEST)
        g_o[...] = gcum
        b_o[...] = beta

    wide = jax.ShapeDtypeStruct((rows, GDN_WIDTH), f32)
    narrow = jax.ShapeDtypeStruct((rows, LANES), f32)
    o_spec = pl.BlockSpec((tm, GDN_WIDTH), lambda i: (i, 0))
    n_spec = pl.BlockSpec((tm, LANES), lambda i: (i, 0))
    return pl.pallas_call(
        body, name="gdn_prep_fwd", out_shape=[wide] * 3 + [narrow] * 2, grid=(rows // tm,),
        in_specs=_gdn_prep_specs(proj, tm) + [_full(conv_w), _full(a_log), _full(dt_bias), _full(ltri)],
        out_specs=[o_spec] * 3 + [n_spec] * 2, compiler_params=_params(("parallel",)),
    )(proj, proj, proj, proj, conv_w, a_log, dt_bias, ltri)


def _gdn_prep_bwd_act(proj, conv_w, a_log, dt_bias, ltri, dq, dk, dv, dgw, dbw, lo, hi, tm):
    rows = proj.shape[0]
    c3 = 3 * GDN_WIDTH

    def body(cur, prev8, a_in, b_in, w, al, dtb, lt, dq_r, dk_r, dv_r, dg_r, db_r, dconv_o, da_o, dbin_o, dal_o, ddt_o):
        i = pl.program_id(0)
        mask = _row_mask(i, tm, lo, hi, (tm, LANES)) & (lax.broadcasted_iota(i32, (tm, LANES), 1) < GDN_HEADS)
        conv = _conv_taps(cur[...], prev8[...], w[...])
        dg = lax.dot_general(lt[...], dg_r[...], (((0,), (0,)), ((), ())), preferred_element_type=f32,
                             precision=lax.Precision.HIGHEST)
        dbeta = db_r[...]
        _, vjp = jax.vjp(lambda c, a, b, x, y: _gdn_act(c, a, b, x, y, mask), conv, a_in[...], b_in[...], al[...], dtb[...])
        dconv, da, dbin, dal, ddt = vjp((dq_r[...], dk_r[...], dv_r[...], dg, dbeta))
        dconv_o[...] = dconv
        da_o[...] = da
        dbin_o[...] = dbin

        @pl.when(i == 0)
        def _():
            dal_o[...] = jnp.zeros_like(dal_o)
            ddt_o[...] = jnp.zeros_like(ddt_o)

        dal_o[...] += dal
        ddt_o[...] += ddt

    w_spec = pl.BlockSpec((tm, GDN_WIDTH), lambda i: (i, 0))
    n_spec = pl.BlockSpec((tm, LANES), lambda i: (i, 0))
    s_spec = pl.BlockSpec((1, LANES), lambda i: (0, 0))
    return pl.pallas_call(
        body, name="gdn_prep_bwd_act",
        out_shape=[jax.ShapeDtypeStruct((rows, c3), f32), jax.ShapeDtypeStruct((rows, LANES), f32),
                   jax.ShapeDtypeStruct((rows, LANES), f32), jax.ShapeDtypeStruct((1, LANES), f32),
                   jax.ShapeDtypeStruct((1, LANES), f32)],
        grid=(rows // tm,),
        in_specs=_gdn_prep_specs(proj, tm) + [_full(conv_w), _full(a_log), _full(dt_bias), _full(ltri)] + [w_spec] * 3 + [n_spec] * 2,
        out_specs=[pl.BlockSpec((tm, c3), lambda i: (i, 0)), n_spec, n_spec, s_spec, s_spec],
        compiler_params=_params(("arbitrary",)),
    )(proj, proj, proj, proj, conv_w, a_log, dt_bias, ltri, dq, dk, dv, dgw, dbw)


def _gdn_prep_bwd_conv(proj, conv_w, dconv, dgate, da, dbin, tm):
    rows, width = proj.shape
    c3 = 3 * GDN_WIDTH
    t8 = tm // 8
    nt = rows // tm

    def body(cur, w, dc, dnext8, dgt, da_r, db_r, dp_o, dw_o):
        i = pl.program_id(0)
        d = dc[...]
        x = cur[...]
        nxt = jnp.where(i == nt - 1, 0.0, dnext8[...])
        cat = jnp.concatenate([d, nxt], axis=0)
        wv = w[...]
        dx = d * wv[CONV_WIDTH - 1:CONV_WIDTH, :]
        parts = [jnp.sum(d * x, axis=0, keepdims=True)]
        for j in range(1, CONV_WIDTH):
            ahead = pltpu.roll(cat, tm + 8 - j, axis=0)[:tm, :]
            dx = dx + ahead * wv[CONV_WIDTH - 1 - j:CONV_WIDTH - j, :]
            parts.append(jnp.sum(ahead * x, axis=0, keepdims=True))
        dp_o[:, :c3] = dx.astype(bf16)
        dp_o[:, c3:4 * GDN_WIDTH] = dgt[...].astype(bf16)
        dp_o[:, 4 * GDN_WIDTH:4 * GDN_WIDTH + LANES] = da_r[...].astype(bf16)
        dp_o[:, 4 * GDN_WIDTH + LANES:] = db_r[...].astype(bf16)
        dwt = jnp.concatenate(parts[::-1], axis=0)

        @pl.when(i == 0)
        def _():
            dw_o[...] = jnp.zeros_like(dw_o)

        dw_o[...] += dwt

    n_spec = pl.BlockSpec((tm, LANES), lambda i: (i, 0))
    return pl.pallas_call(
        body, name="gdn_prep_bwd_conv",
        out_shape=[jax.ShapeDtypeStruct((rows, width), bf16), jax.ShapeDtypeStruct((CONV_WIDTH, c3), f32)],
        grid=(nt,),
        in_specs=[pl.BlockSpec((tm, c3), lambda i: (i, 0)),
                  _full(conv_w),
                  pl.BlockSpec((tm, c3), lambda i: (i, 0)),
                  pl.BlockSpec((8, c3), lambda i: (jnp.minimum((i + 1) * t8, rows // 8 - 1), 0)),
                  pl.BlockSpec((tm, GDN_WIDTH), lambda i: (i, 0)), n_spec, n_spec],
        out_specs=[pl.BlockSpec((tm, width), lambda i: (i, 0)), pl.BlockSpec((CONV_WIDTH, c3), lambda i: (0, 0))],
        compiler_params=_params(("arbitrary",)),
    )(proj, conv_w, dconv, dconv, dgate, da, dbin)


def _split(a):
    hi = a.astype(bf16)
    return hi, (a - hi.astype(f32)).astype(bf16)


def _make_mm(dot):
    @jax.custom_vjp
    def nn(a, b):
        return dot(a, b, "nn")

    nn.defvjp(lambda a, b: (dot(a, b, "nn"), (a, b)),
              lambda r, ct: (dot(ct, r[1], "nt"), dot(r[0], ct, "tn")))

    @jax.custom_vjp
    def nt(a, b):
        return dot(a, b, "nt")

    nt.defvjp(lambda a, b: (dot(a, b, "nt"), (a, b)),
              lambda r, ct: (dot(ct, r[1], "nn"), dot(ct, r[0], "tn")))

    @jax.custom_vjp
    def tn(a, b):
        return dot(a, b, "tn")

    tn.defvjp(lambda a, b: (dot(a, b, "tn"), (a, b)),
              lambda r, ct: (dot(r[1], ct, "nt"), dot(r[0], ct, "nn")))
    return nn, nt, tn


_mm, _mm_nt, _mm_tn = _make_mm(_bdot)


def _each(f, *lists):
    return [f(*xs) for xs in zip(*lists)]


def _gdn_chunk(q, k, v, gcb, bcb, s_in):
    c = q[0].shape[0]
    ri = lax.broadcasted_iota(i32, (c, c), 0)
    ci = lax.broadcasted_iota(i32, (c, c), 1)
    incl, strict = ri >= ci, ri > ci
    rowi = lax.broadcasted_iota(i32, gcb[0].shape, 0)
    qs = _each(lambda t: t * (GDN_DIM ** -0.5), q)
    decay = _each(lambda g: jnp.where(incl, jnp.exp(jnp.where(incl, g[:, :c] - g[:, :c].T, 0.0)), 0.0), gcb)
    kk = _each(lambda t: _mm_nt(t, t), k)
    a1 = _each(lambda b, d, t: jnp.where(strict, b[:, :c] * d * t, 0.0), bcb, decay, kk)
    eg = _each(jnp.exp, gcb)
    x = _each(lambda b, vv, e, t: jnp.concatenate([b * vv, (b * e) * t], axis=1), bcb, v, eg, k)
    pows = [a1]
    for _ in range(5):
        pows.append(_each(lambda p: _mm(p, p), pows[-1]))
    for ps in pows[:0:-1]:
        x = _each(lambda p, t: t + _mm(p, t), ps, x)
    x = _each(lambda p, t: t - _mm(p, t), a1, x)
    attn = _each(lambda a, b, d: _mm_nt(a, b) * d, qs, k, decay)
    glast = _each(lambda g: jnp.sum(jnp.where(rowi == c - 1, g, 0.0), axis=0, keepdims=True), gcb)
    u = _each(lambda t, s: t[:, :GDN_DIM] - _mm(t[:, GDN_DIM:], s), x, s_in)
    o = _each(lambda a, e, s, w, uu: _mm(a * e, s) + _mm(w, uu), qs, eg, s_in, attn, u)
    s_out = _each(lambda s, gl, t, g, uu: s * jnp.exp(gl) + _mm_tn(t * jnp.exp(gl - g), uu), s_in, glast, k, gcb, u)
    return o, s_out


def _gdn_heads(ref):
    return [ref[:, h * GDN_DIM:(h + 1) * GDN_DIM] for h in range(GDN_HEADS)]


def _gdn_fwd(q, k, v, gw, bw, shards):
    rows = q.shape[0]
    nc = rows // CHUNK
    num = len(shards)
    blk = pl.BlockSpec((CHUNK, GDN_WIDTH), lambda c: (c, 0))

    def body(*refs):
        q_r, k_r, v_r, g_r, b_r = refs[:5]
        ins = refs[5:5 + num]
        o_r, st_r = refs[5 + num:7 + num]
        outs = refs[7 + num:7 + 2 * num]
        s_sc, send_sems, recv_sems, local_sems = refs[7 + 2 * num:]
        c = pl.program_id(0)
        start, forward, finish = _gather_plan(ins, outs, send_sems, recv_sems, local_sems)

        @pl.when(c == 0)
        def _():
            s_sc[...] = jnp.zeros_like(s_sc)
            start()

        s_in = [s_sc[h] for h in range(GDN_HEADS)]
        st_r[0] = s_sc[...]
        o, s_out = _gdn_chunk(_gdn_heads(q_r), _gdn_heads(k_r), _gdn_heads(v_r), _widen(g_r[...]), _widen(b_r[...]), s_in)
        o_r[...] = jnp.concatenate(o, axis=1)
        for h in range(GDN_HEADS):
            s_sc[h] = s_out[h]
        pl.when(c == nc // 2)(forward)
        pl.when(c == nc - 1)(finish)

    res = pl.pallas_call(
        body, name="gdn_fwd",
        out_shape=[jax.ShapeDtypeStruct((rows, GDN_WIDTH), f32), jax.ShapeDtypeStruct((nc, GDN_HEADS, GDN_DIM, GDN_DIM), f32)]
        + [jax.ShapeDtypeStruct((N_CHIPS,) + t.shape, t.dtype) for t in shards],
        grid=(nc,), in_specs=[blk] * 3 + [pl.BlockSpec((CHUNK, LANES), lambda c: (c, 0))] * 2 + [_ANY] * num,
        out_specs=[blk, pl.BlockSpec((1, GDN_HEADS, GDN_DIM, GDN_DIM), lambda c: (c, 0, 0, 0))] + [_ANY] * num,
        scratch_shapes=[pltpu.VMEM((GDN_HEADS, GDN_DIM, GDN_DIM), f32), pltpu.SemaphoreType.DMA((6 * num,)),
                        pltpu.SemaphoreType.DMA((6 * num,)), pltpu.SemaphoreType.DMA((num,))],
        compiler_params=_params(("arbitrary",)),
    )(q, k, v, gw, bw, *shards)
    return res[0], res[1], res[2:]


def _gdn_bwd(q, k, v, gw, bw, states, do, parts):
    rows = q.shape[0]
    nc = rows // CHUNK
    num = len(parts)
    blk = pl.BlockSpec((CHUNK, GDN_WIDTH), lambda c: (nc - 1 - c, 0))

    def body(*refs):
        q_r, k_r, v_r, g_r, b_r, st_r, do_r = refs[:7]
        ins = refs[7:7 + num]
        dq_r, dk_r, dv_r, dg_r, db_r = refs[7 + num:12 + num]
        outs = refs[12 + num:12 + 2 * num]
        ds_sc, send_sems, recv_sems = refs[12 + 2 * num:]
        c = pl.program_id(0)
        start, finish = _scatter_plan(ins, outs, send_sems, recv_sems)

        @pl.when(c == 0)
        def _():
            ds_sc[...] = jnp.zeros_like(ds_sc)
            start()

        s_in = [st_r[0, h] for h in range(GDN_HEADS)]
        _, vjp = jax.vjp(_gdn_chunk, _gdn_heads(q_r), _gdn_heads(k_r), _gdn_heads(v_r), _widen(g_r[...]), _widen(b_r[...]), s_in)
        dq, dk, dv, dg, db, ds_in = vjp((_gdn_heads(do_r), [ds_sc[h] for h in range(GDN_HEADS)]))
        dq_r[...] = jnp.concatenate(dq, axis=1)
        dk_r[...] = jnp.concatenate(dk, axis=1)
        dv_r[...] = jnp.concatenate(dv, axis=1)
        dg_r[...] = _narrow(dg)
        db_r[...] = _narrow(db)
        for h in range(GDN_HEADS):
            ds_sc[h] = ds_in[h]
        pl.when(c == nc - 1)(finish)

    wide = jax.ShapeDtypeStruct((rows, GDN_WIDTH), f32)
    narrow = jax.ShapeDtypeStruct((rows, LANES), f32)
    nblk = pl.BlockSpec((CHUNK, LANES), lambda c: (nc - 1 - c, 0))
    res = pl.pallas_call(
        body, name="gdn_bwd",
        out_shape=[wide] * 3 + [narrow] * 2 + [jax.ShapeDtypeStruct((3,) + t.shape[1:], t.dtype) for t in parts],
        grid=(nc,),
        in_specs=[blk] * 3 + [nblk] * 2
        + [pl.BlockSpec((1, GDN_HEADS, GDN_DIM, GDN_DIM), lambda c: (nc - 1 - c, 0, 0, 0)), blk] + [_ANY] * num,
        out_specs=[blk] * 3 + [nblk] * 2 + [_ANY] * num,
        scratch_shapes=[pltpu.VMEM((GDN_HEADS, GDN_DIM, GDN_DIM), f32), pltpu.SemaphoreType.DMA((3 * num,)),
                        pltpu.SemaphoreType.DMA((3 * num,))],
        compiler_params=_params(("arbitrary",)),
    )(q, k, v, gw, bw, states, do, *parts)
    return res[:5], res[5:]


def _gdn_gate(o, gate, og):
    heads = [o[:, h * GDN_DIM:(h + 1) * GDN_DIM] for h in range(GDN_HEADS)]
    n = jnp.concatenate([oh * lax.rsqrt(jnp.mean(oh * oh, axis=-1, keepdims=True) + EPS) * og for oh in heads], axis=1)
    return n * jax.nn.silu(gate)


def _gdn_gate_fwd(o, proj, og, tm):
    rows = o.shape[0]
    return _rowwise(lambda i, ov, gv, w: (_gdn_gate(ov, gv, w),), "gdn_gate_fwd", rows, tm,
                    [(o, (GDN_WIDTH, 0)), (proj, (GDN_WIDTH, 3)), (og, None)], [(GDN_WIDTH, bf16)])[0]


def _gdn_gate_bwd(o, proj, og, dy, tm):
    rows = o.shape[0]

    def fn(i, ov, gv, w, d):
        _, vjp = jax.vjp(_gdn_gate, ov, gv, w)
        return vjp(d)

    return _rowwise(fn, "gdn_gate_bwd", rows, tm,
                    [(o, (GDN_WIDTH, 0)), (proj, (GDN_WIDTH, 3)), (og, None), (dy, (GDN_WIDTH, 0))],
                    [(GDN_WIDTH, f32), (GDN_WIDTH, f32)], [(1, GDN_DIM)])


def _sb_visible(i, j, valid):
    qpos = i * SB_BLOCK + lax.broadcasted_iota(i32, (SB_BLOCK, SB_BLOCK), 0)
    kpos = j * SB_BLOCK + lax.broadcasted_iota(i32, (SB_BLOCK, SB_BLOCK), 1)
    return (kpos < qpos) & (kpos >= FRONT) & valid


def _sb_logs(z, vis):
    l1p = jnp.log(1.0 + jnp.exp(-jnp.abs(z)))
    return -(jnp.maximum(-z, 0.0) + l1p), jnp.where(vis, -(jnp.maximum(z, 0.0) + l1p), 0.0)


def _tri_sum(x, tri):
    hi, lo = _split(x)
    return jnp.dot(hi, tri, preferred_element_type=f32) + jnp.dot(lo, tri, preferred_element_type=f32)


def _sb_live(t, i, runs):
    return (t <= i) & (jnp.max(functools.reduce(jnp.maximum, runs)) > -SB_UNDERFLOW)


def _sb_blocks(i, t, nb):
    js = [i - t - b for b in range(nb)]
    kss = [pl.ds(pl.multiple_of(jnp.maximum(j, 0) * SB_BLOCK, SB_BLOCK), SB_BLOCK) for j in js]
    return kss, [_sb_visible(i, j, j >= 0) for j in js]


def _sb_weights(i, t, nb, qs, sls, k_r, runs, after, scale):
    nh = len(qs)
    kss, vis = _sb_blocks(i, t, nb)
    units = [(a, b) for b in range(nb) for a in range(nh)]
    z = [_bdot(qs[a], k_r[kss[b], sls[a]], "nt") * scale for a, b in units]
    logs = [_sb_logs(zz, vis[b]) for zz, (a, b) in zip(z, units)]
    later = [_tri_sum(l[1], after) for l in logs]
    sums = [jnp.sum(l[1], axis=1, keepdims=True) for l in logs]
    w = []
    runs = list(runs)
    for b in range(nb):
        for a in range(nh):
            u = b * nh + a
            w.append(jnp.where(vis[b], jnp.exp(logs[u][0] + later[u] + runs[a]), 0.0))
        runs = [runs[a] + sums[b * nh + a] for a in range(nh)]
    return kss, vis, units, logs, w, tuple(runs)


def _sb_fwd(q, kv, width):
    rows = q.shape[0]
    nq = rows // SB_BLOCK
    lanes = SB_FWD_HEADS * SB_DIM
    npair = width // lanes
    scale = SB_DIM ** -0.5

    def body(q_r, k_r, v_r, o_r):
        i = pl.program_id(1)
        rj = lax.broadcasted_iota(i32, (SB_BLOCK, SB_BLOCK), 0)
        cs = lax.broadcasted_iota(i32, (SB_BLOCK, SB_BLOCK), 1)
        after = (rj > cs).astype(bf16)
        sls = [slice(a * SB_DIM, (a + 1) * SB_DIM) for a in range(SB_FWD_HEADS)]
        qs = [q_r[:, sl] for sl in sls]

        def step(carry, nb):
            t, accs, runs = carry
            kss, _, units, _, w, runs = _sb_weights(i, t, nb, qs, sls, k_r, runs, after, scale)
            prods = [_bdot(ww, v_r[kss[b], sls[a]], "nn") for ww, (a, b) in zip(w, units)]
            accs = tuple(functools.reduce(jnp.add, [accs[a]] + prods[a::SB_FWD_HEADS]) for a in range(SB_FWD_HEADS))
            return t + nb, accs, runs

        init = (jnp.int32(0), tuple(jnp.zeros((SB_BLOCK, SB_DIM), f32) for _ in sls),
                tuple(jnp.zeros((SB_BLOCK, 1), f32) for _ in sls))
        _, accs, _ = lax.while_loop(lambda c: _sb_live(c[0], i, c[2]), lambda c: step(c, 2), step(init, SB_FIRST))
        o_r[...] = jnp.concatenate(accs, axis=1)

    return pl.pallas_call(
        body, name="sb_fwd", out_shape=jax.ShapeDtypeStruct((rows, width), f32), grid=(npair, nq),
        in_specs=[pl.BlockSpec((SB_BLOCK, lanes), lambda p, i: (i, p)),
                  pl.BlockSpec((rows, lanes), lambda p, i: (0, p)),
                  pl.BlockSpec((rows, lanes), lambda p, i: (0, npair + p))],
        out_specs=pl.BlockSpec((SB_BLOCK, lanes), lambda p, i: (i, p)),
        compiler_params=_params(("parallel", "arbitrary")),
    )(q, kv, kv)


def _sb_bwd(q, kv, do, width):
    rows = q.shape[0]
    nq = rows // SB_BLOCK
    npair = width // LANES
    nh = LANES // SB_DIM
    scale = SB_DIM ** -0.5

    def body(q_r, k_r, v_r, do_r, dq_r, dk_r, dv_r, e_sc, sig_sc, w_sc):
        i = pl.program_id(1)

        @pl.when(i == 0)
        def _():
            dk_r[...] = jnp.zeros_like(dk_r)
            dv_r[...] = jnp.zeros_like(dv_r)

        rj = lax.broadcasted_iota(i32, (SB_BLOCK, SB_BLOCK), 0)
        cs = lax.broadcasted_iota(i32, (SB_BLOCK, SB_BLOCK), 1)
        after = (rj > cs).astype(bf16)
        from_s = (rj >= cs).astype(bf16)
        zero1 = jnp.zeros((SB_BLOCK, 1), f32)
        sls = [slice(a * SB_DIM, (a + 1) * SB_DIM) for a in range(nh)]
        qs = [q_r[:, sl] for sl in sls]
        dos = [do_r[:, sl] for sl in sls]

        def weigh(carry, nb):
            t, runs, eruns = carry
            kss, _, units, logs, w, runs = _sb_weights(i, t, nb, qs, sls, k_r, runs, after, scale)
            dw = [_bdot(dos[a], v_r[kss[b], sls[a]], "nt") for a, b in units]
            e = [ww * d for ww, d in zip(w, dw)]
            for u, (a, b) in enumerate(units):
                e_sc[a, t + b] = e[u]
                sig_sc[a, t + b] = jnp.exp(logs[u][0])
                w_sc[a, t + b] = w[u].astype(w_sc.dtype)
            sums = [jnp.sum(ee, axis=1, keepdims=True) for ee in e]
            eruns = tuple(functools.reduce(jnp.add, [eruns[a]] + sums[a::nh]) for a in range(nh))
            return t + nb, runs, eruns

        n_blk, _, etots = lax.while_loop(lambda c: _sb_live(c[0], i, c[1]), lambda c: weigh(c, 2),
                                         weigh((jnp.int32(0), (zero1,) * nh, (zero1,) * nh), SB_FIRST))

        def push(t, carry, nb):
            dqs, eruns = carry
            kss, vis = _sb_blocks(i, t, nb)
            units = [(a, b) for b in range(nb) for a in range(nh)]
            e = [e_sc[a, t + b] for a, b in units]
            dvs = [_bdot(w_sc[a, t + b], dos[a], "tn") for a, b in units]
            upto = [_tri_sum(ee, from_s) for ee in e]
            sums = [jnp.sum(ee, axis=1, keepdims=True) for ee in e]
            dz = []
            eruns = list(eruns)
            for b in range(nb):
                for a in range(nh):
                    u = b * nh + a
                    sig = sig_sc[a, t + b]
                    before = etots[a] - eruns[a] - upto[u]
                    dz.append(jnp.where(vis[b], e[u] * (1.0 - sig) - before * sig, 0.0) * scale)
                eruns = [eruns[a] + sums[b * nh + a] for a in range(nh)]
            dks = [_bdot(d, qs[a], "tn") for d, (a, b) in zip(dz, units)]
            dqp = [_bdot(d, k_r[kss[b], sls[a]], "nn") for d, (a, b) in zip(dz, units)]
            for b in range(nb):
                dk_r[kss[b], :] += jnp.concatenate(dks[b * nh:(b + 1) * nh], axis=1)
                dv_r[kss[b], :] += jnp.concatenate(dvs[b * nh:(b + 1) * nh], axis=1)
            dqs = tuple(functools.reduce(jnp.add, [dqs[a]] + dqp[a::nh]) for a in range(nh))
            return dqs, tuple(eruns)

        first = push(jnp.int32(0), (tuple(jnp.zeros((SB_BLOCK, SB_DIM), f32) for _ in sls), (zero1,) * nh), SB_FIRST)
        dqs, _ = lax.fori_loop(0, (n_blk - SB_FIRST) // 2, lambda p, c: push(SB_FIRST + 2 * p, c, 2), first)
        dq_r[...] = jnp.concatenate(dqs, axis=1)

    blk = pl.BlockSpec((SB_BLOCK, LANES), lambda p, i: (i, p))
    col = pl.BlockSpec((rows, LANES), lambda p, i: (0, p))
    wide = jax.ShapeDtypeStruct((rows, width), f32)
    depth = nq + SB_FIRST
    return pl.pallas_call(
        body, name="sb_bwd", out_shape=[wide] * 3, grid=(npair, nq),
        in_specs=[blk, col, pl.BlockSpec((rows, LANES), lambda p, i: (0, npair + p)), blk],
        out_specs=[blk, col, col],
        scratch_shapes=[pltpu.VMEM((nh, depth, SB_BLOCK, SB_BLOCK), f32), pltpu.VMEM((nh, depth, SB_BLOCK, SB_BLOCK), f32),
                        pltpu.VMEM((nh, depth, SB_BLOCK, SB_BLOCK), bf16)],
        compiler_params=_params(("parallel", "arbitrary")),
    )(q, kv, kv, do)


_FLIPS = ((1, 0), (0, 1), (1, 1))
_ANY = pl.BlockSpec(memory_space=pl.ANY)


def _flip(v, a):
    return v + a - 2 * a * v


def _gather_plan(ins, outs, send_sems, recv_sems, local_sems):
    num = len(ins)
    x, y, c = lax.axis_index("x"), lax.axis_index("y"), lax.axis_index("c")
    me, sibling = (x, y, c), (x, y, 1 - c)
    chip = 2 * x + y
    others = [(_flip(x, a), _flip(y, b)) for a, b in _FLIPS]
    pairs = [(k, n, 2 * ox + oy) for k in range(num) for n, (ox, oy) in enumerate(others)]

    def half_of(ref, hc):
        half = ref.shape[0] // 2
        start = hc * half
        for align in (16, 8):
            if half % align == 0:
                start = pl.multiple_of(start, align)
                break
        return ref.at[pl.ds(start, half)]

    def copy(k, n, s, hc, to, src=None):
        dst = half_of(outs[k].at[s], hc)
        return pltpu.make_async_remote_copy(
            src_ref=dst if src is None else src, dst_ref=dst,
            send_sem=send_sems.at[6 * k + n], recv_sem=recv_sems.at[6 * k + n], device_id=to, device_id_type=MESH)

    mine = [pltpu.make_async_copy(ins[k], outs[k].at[chip], local_sems.at[k]) for k in range(num)]
    first = [copy(k, n, chip, c, (others[n][0], others[n][1], c), src=half_of(ins[k], c)) for k, n, _ in pairs]
    passed = [copy(k, 3 + n, s, c, sibling) for k, n, s in pairs]

    def start():
        for cp in mine + first:
            cp.start()

    def forward():
        for (k, n, s), fw in zip(pairs, passed):
            copy(k, n, s, c, me).wait_recv()
            fw.start()

    def finish():
        for k, n, s in pairs:
            copy(k, 3 + n, s, 1 - c, me).wait_recv()
        for cp in first + passed:
            cp.wait_send()
        for cp in mine:
            cp.wait()

    return start, forward, finish


def _gather_chips(shards):
    num = len(shards)

    def body(*refs):
        for phase in _gather_plan(refs[:num], refs[num:2 * num], *refs[2 * num:]):
            phase()

    return pl.pallas_call(
        body, name="gather_chips", out_shape=[jax.ShapeDtypeStruct((N_CHIPS,) + t.shape, t.dtype) for t in shards],
        in_specs=[_ANY] * num, out_specs=[_ANY] * num,
        scratch_shapes=[pltpu.SemaphoreType.DMA((6 * num,)), pltpu.SemaphoreType.DMA((6 * num,)),
                        pltpu.SemaphoreType.DMA((num,))],
    )(*shards)


def _scatter_plan(ins, outs, send_sems, recv_sems):
    x, y, c = lax.axis_index("x"), lax.axis_index("y"), lax.axis_index("c")
    cps = []
    for k in range(len(ins)):
        for n, (a, b) in enumerate(_FLIPS):
            ox, oy = _flip(x, a), _flip(y, b)
            cps.append(pltpu.make_async_remote_copy(
                src_ref=ins[k].at[2 * ox + oy], dst_ref=outs[k].at[n], send_sem=send_sems.at[3 * k + n],
                recv_sem=recv_sems.at[3 * k + n], device_id=(ox, oy, c), device_id_type=MESH))

    def start():
        for cp in cps:
            cp.start()

    def finish():
        for cp in cps:
            cp.wait()

    return start, finish


def _scatter_chips(parts):
    num = len(parts)

    def body(*refs):
        for phase in _scatter_plan(refs[:num], refs[num:2 * num], *refs[2 * num:]):
            phase()

    return pl.pallas_call(
        body, name="scatter_chips", out_shape=[jax.ShapeDtypeStruct((3,) + t.shape[1:], t.dtype) for t in parts],
        in_specs=[_ANY] * num, out_specs=[_ANY] * num,
        scratch_shapes=[pltpu.SemaphoreType.DMA((3 * num,)), pltpu.SemaphoreType.DMA((3 * num,))],
    )(*parts)


def _swap_sibling(arrs):
    num = len(arrs)

    def body(*refs):
        ins, outs = refs[:num], refs[num:2 * num]
        send_sems, recv_sems = refs[2 * num:]
        x, y, c = lax.axis_index("x"), lax.axis_index("y"), lax.axis_index("c")
        cps = [pltpu.make_async_remote_copy(src_ref=ins[k], dst_ref=outs[k], send_sem=send_sems.at[k],
                                            recv_sem=recv_sems.at[k], device_id=(x, y, 1 - c), device_id_type=MESH)
               for k in range(num)]
        for cp in cps:
            cp.start()
        for cp in cps:
            cp.wait()

    return pl.pallas_call(
        body, name="swap_sibling", out_shape=[jax.ShapeDtypeStruct(t.shape, t.dtype) for t in arrs],
        in_specs=[_ANY] * num, out_specs=[_ANY] * num,
        scratch_shapes=[pltpu.SemaphoreType.DMA((num,)), pltpu.SemaphoreType.DMA((num,))],
    )(*arrs)


def _gather_all(v):
    m_per, n = v.shape

    def body(x_ref, out_ref, send_sems, recv_sems, local_sem):
        x, y, c = lax.axis_index("x"), lax.axis_index("y"), lax.axis_index("c")
        me, sibling = (x, y, c), (x, y, 1 - c)
        chips = [(_flip(x, a), _flip(y, b)) for a, b in _FLIPS]

        def rows(px, py, pc):
            return out_ref.at[pl.ds(pl.multiple_of((4 * px + 2 * py + pc) * m_per, 8), m_per), :]

        def copy(k, block, to, src=None):
            return pltpu.make_async_remote_copy(
                src_ref=rows(*block) if src is None else src, dst_ref=rows(*block),
                send_sem=send_sems.at[k], recv_sem=recv_sems.at[k], device_id=to, device_id_type=MESH)

        mine = pltpu.make_async_copy(x_ref, rows(*me), local_sem)
        mine.start()
        first = [copy(0, me, sibling, src=x_ref)]
        first += [copy(1 + j, me, (*chip, c), src=x_ref) for j, chip in enumerate(chips)]
        for cp in first:
            cp.start()
        passed = [copy(4 + j, (*chip, c), sibling) for j, chip in enumerate(chips)]
        for j, chip in enumerate(chips):
            copy(1 + j, (*chip, c), me).wait_recv()
            passed[j].start()
        copy(0, sibling, me).wait_recv()
        for j, chip in enumerate(chips):
            copy(4 + j, (*chip, 1 - c), me).wait_recv()
        for cp in first + passed:
            cp.wait_send()
        mine.wait()

    return pl.pallas_call(
        body, name="gather_all", out_shape=jax.ShapeDtypeStruct((N_DEV * m_per, n), v.dtype),
        in_specs=[pl.BlockSpec(memory_space=pltpu.VMEM)], out_specs=pl.BlockSpec(memory_space=pltpu.VMEM),
        scratch_shapes=[pltpu.SemaphoreType.DMA((7,)), pltpu.SemaphoreType.DMA((7,)), pltpu.SemaphoreType.DMA],
    )(v)


def _sum_chips(parts, got, chip, name):
    cols = parts.shape[-1]
    rows = parts.size // (N_CHIPS * cols)
    tm = _pick(rows, (256, 128, 64, 32, 16))

    def body(chip_r, own_r, got_r, o_r):
        acc = own_r[0]
        for n in range(3):
            acc = acc + got_r[n].astype(f32)
        o_r[...] = acc

    return pl.pallas_call(
        body, name=name, out_shape=jax.ShapeDtypeStruct((rows, cols), f32),
        grid_spec=pltpu.PrefetchScalarGridSpec(
            num_scalar_prefetch=1, grid=(rows // tm,),
            in_specs=[pl.BlockSpec((1, tm, cols), lambda i, s: (s[0], i, 0)),
                      pl.BlockSpec((3, tm, cols), lambda i, s: (0, i, 0))],
            out_specs=pl.BlockSpec((tm, cols), lambda i, s: (i, 0))),
        compiler_params=_params(("parallel",)),
    )(chip, parts.reshape(N_CHIPS, rows, cols), got.reshape(3, rows, cols))


def _sum_devices(g, m_per):
    n = g.shape[1]

    def body(g_r, o_r):
        acc = g_r[0:m_per, :]
        for d in range(1, N_DEV):
            acc = acc + g_r[d * m_per:(d + 1) * m_per, :]
        o_r[...] = acc

    return pl.pallas_call(body, name="sum_devices", out_shape=jax.ShapeDtypeStruct((m_per, n), f32))(g)


def _adamw(w, gs, m, v, name):
    shape = w.shape
    cols = shape[-1]
    rows = w.size // cols
    tm = _pick(rows, (256, 128, 64, 32, 16, 8)) if rows * cols * 4 > (1 << 20) else rows

    def fn(i, wv, mv, vv, *gv):
        g = functools.reduce(jnp.add, gv)
        mn = ADAM_B1 * mv + (1.0 - ADAM_B1) * g
        vn = ADAM_B2 * vv + (1.0 - ADAM_B2) * jnp.square(g)
        m_hat = mn / (1.0 - ADAM_B1 ** ADAM_STEP)
        v_hat = vn / (1.0 - ADAM_B2 ** ADAM_STEP)
        delta = -ADAM_LR * (m_hat / (jnp.sqrt(v_hat) + ADAM_EPS) + ADAM_WD * wv)
        return g, delta, mn, vn

    outs = _rowwise(fn, name, rows, tm, [(t.reshape(rows, cols), (cols, 0)) for t in (w, m, v) + tuple(gs)], [(cols, f32)] * 4)
    return tuple(o.reshape(shape) for o in outs)


def _pack(pieces, rows, dtype):
    flat = jnp.concatenate([p.reshape(-1).astype(dtype) for p in pieces])
    return jnp.pad(flat, (0, rows * PACK_COLS - flat.size)).reshape(rows, PACK_COLS)


def _unpack(buf, shapes):
    lead = buf.shape[:-2]
    flat = buf.reshape(lead + (-1,))
    out, off = [], 0
    for s in shapes:
        n = 1
        for d in s:
            n *= d
        out.append(flat[..., off:off + n].reshape(lead + tuple(s)))
        off += n
    return out


def _join_cols(t):
    return jnp.moveaxis(t, 0, -2).reshape(t.shape[1:-1] + (N_CHIPS * t.shape[-1],))


def _join_rows(t):
    return t.reshape((N_CHIPS * t.shape[1],) + t.shape[2:])


def _split_cols(t, parts=N_CHIPS):
    r, cols = t.shape
    return jnp.moveaxis(t.reshape(r, parts, cols // parts), 1, 0)


def _split_rows(t):
    return t.reshape((N_CHIPS, t.shape[0] // N_CHIPS) + t.shape[1:])


def kernel(x, meta_tokens, gdn_norm_g, gdn_w_in, gdn_conv_w, gdn_a_log, gdn_dt_bias, gdn_onorm_g, gdn_w_out, kv_norm_g, w_kv, sb_norm_g, sb_w_q, sb_w_o, ffn_norm_g, ffn_w_gate_up, ffn_w_down, final_norm_g, loss_target, m_meta_tokens, m_gdn_norm_g, m_gdn_w_in, m_gdn_conv_w, m_gdn_a_log, m_gdn_dt_bias, m_gdn_onorm_g, m_gdn_w_out, m_kv_norm_g, m_w_kv, m_sb_norm_g, m_sb_w_q, m_sb_w_o, m_ffn_norm_g, m_ffn_w_gate_up, m_ffn_w_down, m_final_norm_g, v_meta_tokens, v_gdn_norm_g, v_gdn_w_in, v_gdn_conv_w, v_gdn_a_log, v_gdn_dt_bias, v_gdn_onorm_g, v_gdn_w_out, v_kv_norm_g, v_w_kv, v_sb_norm_g, v_sb_w_q, v_sb_w_o, v_ffn_norm_g, v_ffn_w_gate_up, v_ffn_w_down, v_final_norm_g):
    weights = dict(meta_tokens=meta_tokens, gdn_norm_g=gdn_norm_g, gdn_w_in=gdn_w_in, gdn_conv_w=gdn_conv_w,
                   gdn_a_log=gdn_a_log, gdn_dt_bias=gdn_dt_bias, gdn_onorm_g=gdn_onorm_g, gdn_w_out=gdn_w_out,
                   kv_norm_g=kv_norm_g, w_kv=w_kv, sb_norm_g=sb_norm_g, sb_w_q=sb_w_q, sb_w_o=sb_w_o,
                   ffn_norm_g=ffn_norm_g, ffn_w_gate_up=ffn_w_gate_up, ffn_w_down=ffn_w_down, final_norm_g=final_norm_g)
    m_in = dict(meta_tokens=m_meta_tokens, gdn_norm_g=m_gdn_norm_g, gdn_w_in=m_gdn_w_in, gdn_conv_w=m_gdn_conv_w,
                gdn_a_log=m_gdn_a_log, gdn_dt_bias=m_gdn_dt_bias, gdn_onorm_g=m_gdn_onorm_g, gdn_w_out=m_gdn_w_out,
                kv_norm_g=m_kv_norm_g, w_kv=m_w_kv, sb_norm_g=m_sb_norm_g, sb_w_q=m_sb_w_q, sb_w_o=m_sb_w_o,
                ffn_norm_g=m_ffn_norm_g, ffn_w_gate_up=m_ffn_w_gate_up, ffn_w_down=m_ffn_w_down, final_norm_g=m_final_norm_g)
    v_in = dict(meta_tokens=v_meta_tokens, gdn_norm_g=v_gdn_norm_g, gdn_w_in=v_gdn_w_in, gdn_conv_w=v_gdn_conv_w,
                gdn_a_log=v_gdn_a_log, gdn_dt_bias=v_gdn_dt_bias, gdn_onorm_g=v_gdn_onorm_g, gdn_w_out=v_gdn_w_out,
                kv_norm_g=v_kv_norm_g, w_kv=v_w_kv, sb_norm_g=v_sb_norm_g, sb_w_q=v_sb_w_q, sb_w_o=v_sb_w_o,
                ffn_norm_g=v_ffn_norm_g, ffn_w_gate_up=v_ffn_w_gate_up, ffn_w_down=v_ffn_w_down, final_norm_g=v_final_norm_g)
    names = list(weights)

    seq, d = x.shape[1], x.shape[2]
    lo_frames = FRONT + N_META
    used = lo_frames + seq
    rows = -(-used // SB_BLOCK) * SB_BLOCK
    tm = _pick(rows, (640, 512, 384, 256, 128))
    tp = _pick(rows, (320, 256, 128))
    n_ffn = ffn_w_gate_up.shape[0]
    sb_width = sb_w_q.shape[2]
    chip =2 * lax.axis_index("x") + lax.axis_index("y")

    big = [gdn_w_in[0], gdn_w_out[0], w_kv, sb_w_q[0], sb_w_o[0], ffn_w_gate_up, ffn_w_down]
    small = [meta_tokens, gdn_norm_g, gdn_conv_w[0]]
    n_early = 2
    big_bf16 = [t.astype(bf16) for t in big]
    w_in_s, w_out_s, small_g = _gather_chips(big_bf16[:n_early] + [_pack(small, 16, f32)])
    small_s = _unpack(small_g, [t.shape for t in small])
    w_in = _join_cols(w_in_s)
    pad_ab = jnp.zeros((d, LANES - GDN_HEADS), bf16)
    w_in_ext = jnp.concatenate([w_in[:, :4 * GDN_WIDTH], w_in[:, 4 * GDN_WIDTH:4 * GDN_WIDTH + GDN_HEADS], pad_ab,
                                w_in[:, 4 * GDN_WIDTH + GDN_HEADS:], pad_ab], axis=1)
    w_out = _join_rows(w_out_s)
    meta_full, gdn_g_full, conv_full = (_join_cols(t) for t in small_s)

    zeros = lambda n: jnp.zeros((n, d), f32)
    h0 = jnp.concatenate([zeros(FRONT), meta_full, x[0], zeros(rows - used)], axis=0)
    tgt = jnp.concatenate([zeros(lo_frames), loss_target[0], zeros(rows - used)], axis=0)
    pad8 = lambda t: jnp.pad(t, ((0, 0), (0, LANES - t.shape[1])))
    a_log8, dt_bias8 = pad8(gdn_a_log), pad8(gdn_dt_bias)
    r_i = jnp.arange(tp)
    ltri = ((r_i[:, None] >= r_i[None, :]) & (r_i[:, None] // CHUNK == r_i[None, :] // CHUNK)).astype(f32)
    ffn_g = [ffn_norm_g[l:l + 1] for l in range(n_ffn)]
    kv_g, fin_g = kv_norm_g.reshape(1, d), final_norm_g.reshape(1, d)

    n0 = _rms_fwd(h0, gdn_g_full, "gdn_norm")
    proj = _matmul(n0, w_in_ext, "nn", "gdn_proj")
    gq, gk, gv, gw, bw = _gdn_prep_fwd(proj, conv_full, a_log8, dt_bias8, ltri, FRONT, used, tp)
    g_o, g_states, (w_kv_s, w_q_s, w_o_s, w_gu_s, w_dn_s) = _gdn_fwd(gq, gk, gv, gw, bw, big_bf16[n_early:])
    w_kvf = _join_cols(w_kv_s)
    w_q = _join_rows(w_q_s)
    w_o = _join_rows(w_o_s)
    w_gu = [_join_cols(w_gu_s[:, l]) for l in range(n_ffn)]
    w_dn = [_join_rows(w_dn_s[:, l]) for l in range(n_ffn)]
    og = _gdn_gate_fwd(g_o, proj, gdn_onorm_g, tm)
    h1 = _matmul(og, w_out, "nn", "gdn_out", res=h0)

    def ffn_fwd(h, l):
        n = _rms_fwd(h, ffn_g[l], f"ffn{l}_norm")
        gate, up, act = _ffn_up(n, w_gu[l], f"ffn{l}_gate_up")
        return _matmul(act, w_dn[l], "nn", f"ffn{l}_down", res=h), (n, gate, up, act)

    h2, ffn0_saved = ffn_fwd(h1, 0)
    n_kv = _rms_fwd(h2, kv_g, "kv_norm")
    kv = _matmul(n_kv, w_kvf, "nn", "kv_proj", out_dtype=bf16)
    n_sb = _rms_fwd(h2, sb_norm_g, "sb_norm")
    sq = _matmul(n_sb, w_q, "nn", "q_proj", out_dtype=bf16)
    s_o = _sb_fwd(sq, kv, sb_width)
    h3 = _matmul(s_o, w_o, "nn", "sb_out", res=h2)
    h4, ffn1_saved = ffn_fwd(h3, 1)
    dh4, d_fin_g, loss_part = _loss_head(h4, fin_g, tgt, lo_frames, used, "loss_head")

    def ffn_bwd(dh, h, l, saved):
        n, gate, up, act = saved
        d_wdn = _matmul(act, dh, "tn", f"ffn{l}_d_w_down")
        d_gate, d_up = _ffn_dact(dh, w_dn[l], gate, up, f"ffn{l}_d_gate_up")
        d_wgu = jnp.concatenate([_split_cols(_matmul(n, d_gate, "tn", f"ffn{l}_d_w_gate"), N_CHIPS // 2),
                                 _split_cols(_matmul(n, d_up, "tn", f"ffn{l}_d_w_up"), N_CHIPS // 2)], axis=0)
        dh_in, dg = _norm_bwd([d_gate, d_up], w_gu[l], h, ffn_g[l], dh, f"ffn{l}_d_norm")
        return dh_in, d_wgu, d_wdn, dg

    dh3, d_wgu1, d_wdn1, d_ffn_g1 = ffn_bwd(dh4, h3, 1, ffn1_saved)
    d_wo = _matmul(s_o, dh3, "tn", "d_w_o")
    d_so = _matmul(dh3, w_o, "nt", "d_sb_o")
    d_sq, d_sk, d_sv = _sb_bwd(sq, kv, d_so, sb_width)
    d_wq = _matmul(n_sb, d_sq, "tn", "d_w_q")
    dh2, d_sb_g = _norm_bwd([d_sq], w_q, h2, sb_norm_g, dh3, "d_sb_norm")
    d_wkv = jnp.concatenate([_matmul(n_kv, d_sk, "tn", "d_w_k"), _matmul(n_kv, d_sv, "tn", "d_w_v")], axis=1)
    dh2, d_kv_g = _norm_bwd([d_sk, d_sv], w_kvf, h2, kv_g, dh2, "d_kv_norm")
    dh1, d_wgu0, d_wdn0, d_ffn_g0 = ffn_bwd(dh2, h1, 0, ffn0_saved)
    d_wout = _matmul(og, dh1, "tn", "d_w_out")
    d_og = _matmul(dh1, w_out, "nt", "d_gdn_gated")
    d_go, d_gate, d_onorm = _gdn_gate_bwd(g_o, proj, gdn_onorm_g, d_og, tm)
    by_chip = [None, _split_rows(d_wout), _split_cols(d_wkv), _split_rows(d_wq), _split_rows(d_wo),
               jnp.stack([d_wgu0, d_wgu1], axis=1),
               jnp.stack([_split_rows(d_wdn0), _split_rows(d_wdn1)], axis=1)]
    (d_gq, d_gk, d_gv, d_gw, d_bw), got_early = _gdn_bwd(gq, gk, gv, gw, bw, g_states, d_go,
                                                         [t.astype(bf16) for t in by_chip[1:]])
    dconv, d_a_in, d_b_in, d_a_log8, d_dt_bias8 = _gdn_prep_bwd_act(
        proj, conv_full, a_log8, dt_bias8, ltri, d_gq, d_gk, d_gv, d_gw, d_bw, FRONT, used, tp)
    dproj, d_conv = _gdn_prep_bwd_conv(proj, conv_full, dconv, d_gate, d_a_in, d_b_in, tp)
    d_win_ext = _matmul(n0, dproj, "tn", "d_w_in")
    dh0, d_gdn_g = _norm_bwd([dproj], w_in_ext, h0, gdn_g_full, dh1, "d_gdn_norm")
    grad_x = dh0[lo_frames:used][None]
    d_win = jnp.concatenate([d_win_ext[:, :4 * GDN_WIDTH], d_win_ext[:, 4 * GDN_WIDTH:4 * GDN_WIDTH + GDN_HEADS],
                             d_win_ext[:, 4 * GDN_WIDTH + LANES:4 * GDN_WIDTH + LANES + GDN_HEADS]], axis=1)

    by_chip[0] = _split_cols(d_win)
    got = list(_scatter_chips([by_chip[0].astype(bf16)])) + list(got_early)
    chip_arr = jnp.reshape(chip, (1,)).astype(i32)
    over_chips = [_sum_chips(t, g, chip_arr, f"sum_chips_{k}") for k, (t, g) in enumerate(zip(by_chip, got))]
    over_sibling = _swap_sibling(over_chips)
    big_names = ["gdn_w_in", "gdn_w_out", "w_kv", "sb_w_q", "sb_w_o", "ffn_w_gate_up", "ffn_w_down"]
    g_big = dict(zip(big_names, zip(over_chips, over_sibling)))

    small_parts = [dh0[FRONT:lo_frames], d_gdn_g, d_conv, d_a_log8, d_dt_bias8, d_onorm, d_kv_g, d_sb_g,
                   d_ffn_g0, d_ffn_g1, d_fin_g, loss_part]
    s_rows = -(-sum(t.size for t in small_parts) // (8 * PACK_COLS)) * 8
    s_sum = _sum_devices(_gather_all(_pack(small_parts, s_rows, f32)), s_rows)
    (g_meta, g_gdn_g, g_conv, g_a_log8, g_dt8, g_onorm, g_kv_g, g_sb_g, g_ffn_g0, g_ffn_g1, g_fin_g,
     loss_v) = _unpack(s_sum, [t.shape for t in small_parts])
    col_shard = lambda t, w: lax.dynamic_slice_in_dim(t, chip * w, w, axis=t.ndim - 1)

    g_small = dict(
        meta_tokens=col_shard(g_meta, meta_tokens.shape[1]), gdn_norm_g=col_shard(g_gdn_g, gdn_norm_g.shape[1]),
        gdn_conv_w=col_shard(g_conv, gdn_conv_w.shape[2])[None],
        gdn_a_log=g_a_log8[:, :GDN_HEADS], gdn_dt_bias=g_dt8[:, :GDN_HEADS], gdn_onorm_g=g_onorm,
        kv_norm_g=g_kv_g.reshape(-1), sb_norm_g=g_sb_g, ffn_norm_g=jnp.concatenate([g_ffn_g0, g_ffn_g1], axis=0),
        final_norm_g=g_fin_g.reshape(-1))

    grads, delta, new_m, new_v = {}, {}, {}, {}
    for n in names:
        gs = g_big[n] if n in g_big else (g_small[n],)
        grads[n], delta[n], new_m[n], new_v[n] = _adamw(weights[n], gs, m_in[n], v_in[n], f"adamw_{n}")
    loss = loss_v[0, 0]
    return (loss, grad_x, *[grads[n] for n in names], *[delta[n] for n in names],
            *[new_m[n] for n in names], *[new_v[n] for n in names])
```

```python
import functools

import jax
import jax.numpy as jnp
from jax import lax
from jax.experimental import pallas as pl
from jax.experimental.pallas import tpu as pltpu

f32 = jnp.float32
bf16 = jnp.bfloat16
i32 = jnp.int32

EPS = 1e-6
N_META = 16
CHUNK = 64
FRONT = (-N_META) % CHUNK
GDN_HEADS = 8
GDN_DIM = 128
GDN_WIDTH = GDN_HEADS * GDN_DIM
CONV_WIDTH = 4
SB_DIM = 64
SB_BLOCK = 128
SB_FWD_HEADS = 4
SB_FIRST = 3
SB_UNDERFLOW = 104.0
LANES = 128
PACK_COLS = 1024
N_CHIPS = 4
N_DEV = 8
ADAM_LR, ADAM_B1, ADAM_B2, ADAM_EPS, ADAM_WD, ADAM_STEP = 0.001, 0.9, 0.999, 1e-08, 0.01, 10
VMEM_LIMIT = 56 * 1024 * 1024
MESH = pl.DeviceIdType.MESH


def _pick(n, prefs):
    for p in prefs:
        if n % p == 0:
            return p
    return n


def _params(sem):
    return pltpu.CompilerParams(dimension_semantics=sem, vmem_limit_bytes=VMEM_LIMIT)


_DIMS = {"nn": ((1,), (0,)), "nt": ((1,), (1,)), "tn": ((0,), (0,))}


def _bdot(a, b, mode):
    return lax.dot_general(a.astype(bf16), b.astype(bf16), (_DIMS[mode], ((), ())), preferred_element_type=f32)


def _matmul(a, b, mode, name, res=None, out_dtype=f32):
    if mode == "nn":
        (m, k), n = a.shape, b.shape[1]
    elif mode == "nt":
        (m, k), n = a.shape, b.shape[0]
    else:
        (k, m), n = a.shape, b.shape[1]
    tm = _pick(m, (640, 1408, 1024, 512, 384, 256, 128))
    tn = _pick(n, (1408, 2176, 1024, 512, 384, 256, 128))
    tk = _pick(k, (1664, 1408, 2176, 1024, 640, 512, 384, 256, 128))
    nk = k // tk
    a_spec = pl.BlockSpec((tk, tm), lambda j, i, kk: (kk, i)) if mode == "tn" else pl.BlockSpec((tm, tk), lambda j, i, kk: (i, kk))
    b_spec = pl.BlockSpec((tn, tk), lambda j, i, kk: (j, kk)) if mode == "nt" else pl.BlockSpec((tk, tn), lambda j, i, kk: (kk, j))
    o_spec = pl.BlockSpec((tm, tn), lambda j, i, kk: (i, j))
    has_res = res is not None

    def body(*refs):
        a_ref, b_ref = refs[:2]
        r_ref = refs[2] if has_res else None
        o_ref = refs[3] if has_res else refs[2]

        def finish(y):
            if has_res:
                y = y + r_ref[...]
            o_ref[...] = y.astype(o_ref.dtype)

        if nk == 1:
            finish(_bdot(a_ref[...], b_ref[...], mode))
            return
        acc = refs[-1]
        kk = pl.program_id(2)
        part = _bdot(a_ref[...], b_ref[...], mode)

        @pl.when(kk == 0)
        def _():
            acc[...] = part

        @pl.when((kk > 0) & (kk < nk - 1))
        def _():
            acc[...] += part

        @pl.when(kk == nk - 1)
        def _():
            finish(acc[...] + part)

    ins = [a, b] + ([res] if has_res else [])
    specs = [a_spec, b_spec] + ([o_spec] if has_res else [])
    return pl.pallas_call(
        body, name=name, out_shape=jax.ShapeDtypeStruct((m, n), out_dtype), grid=(n // tn, m // tm, nk),
        in_specs=specs, out_specs=o_spec, scratch_shapes=[pltpu.VMEM((tm, tn), f32)] if nk > 1 else [],
        compiler_params=_params(("parallel", "parallel", "arbitrary")),
    )(*ins)


def _rowwise(fn, name, rows, tm, ins, outs, reds=()):
    n_in, n_out, n_red = len(ins), len(outs), len(reds)
    in_specs = []
    for arr, spec in ins:
        if spec is None:
            in_specs.append(pl.BlockSpec(arr.shape, lambda i, nd=arr.ndim: (0,) * nd))
        else:
            w, cb = spec
            in_specs.append(pl.BlockSpec((tm, w), lambda i, cb=cb: (i, cb)))
    out_specs = [pl.BlockSpec((tm, w), lambda i: (i, 0)) for w, _ in outs]
    out_specs += [pl.BlockSpec(s, lambda i, nd=len(s): (0,) * nd) for s in reds]
    out_shape = [jax.ShapeDtypeStruct((rows, w), dt) for w, dt in outs]
    out_shape += [jax.ShapeDtypeStruct(s, f32) for s in reds]

    def body(*refs):
        i = pl.program_id(0)
        vals = fn(i, *[r[...] for r in refs[:n_in]])
        for r, v in zip(refs[n_in:n_in + n_out], vals[:n_out]):
            r[...] = v.astype(r.dtype)
        red_refs = refs[n_in + n_out:]

        @pl.when(i == 0)
        def _():
            for r in red_refs:
                r[...] = jnp.zeros_like(r)

        for r, v in zip(red_refs, vals[n_out:]):
            r[...] += v

    res = pl.pallas_call(
        body, name=name, out_shape=out_shape, grid=(rows // tm,), in_specs=in_specs, out_specs=out_specs,
        compiler_params=_params(("arbitrary",)),
    )(*[a for a, _ in ins])
    return res


def _rms(x, g):
    return x * lax.rsqrt(jnp.mean(x * x, axis=-1, keepdims=True) + EPS) * g


def _row_mask(i, tm, lo, hi, shape):
    r = i * tm + lax.broadcasted_iota(i32, shape, 0)
    return (r >= lo) & (r < hi)


def _rms_fwd(x, g, name):
    rows, d = x.shape
    tm = _pick(rows, (640, 512, 384, 256, 128))
    return _rowwise(lambda i, xv, gv: (_rms(xv, gv),), name, rows, tm, [(x, (d, 0)), (g, None)], [(d, bf16)])[0]


def _norm_bwd(parts, w, x, g, res, name, send=()):
    rows, k = parts[0].shape
    d = w.shape[0]
    num, ns = len(parts), len(send)
    tm = _pick(rows, (640, 512, 384, 256, 128))
    tk = _pick(k, (1408, 2176, 1024, 512, 384, 256, 128))
    ni, nk = rows // tm, k // tk

    def body(*refs):
        a_refs, w_refs = refs[:num], refs[num:2 * num]
        x_r, g_r, r_r = refs[2 * num:2 * num + 3]
        s_ins = refs[2 * num + 3:2 * num + 3 + ns]
        o_r, dg_r = refs[2 * num + 3 + ns:2 * num + 5 + ns]
        s_outs = refs[2 * num + 5 + ns:2 * num + 5 + 2 * ns]
        acc = refs[2 * num + 5 + 2 * ns]
        i, kk = pl.program_id(0), pl.program_id(1)
        start, done = _scatter_plan(s_ins, s_outs, *refs[2 * num + 6 + 2 * ns:]) if ns else (None, None)
        part = functools.reduce(jnp.add, [_bdot(a[...], b[...], "nt") for a, b in zip(a_refs, w_refs)])

        @pl.when((i == 0) & (kk == 0))
        def _():
            dg_r[...] = jnp.zeros_like(dg_r)
            if ns:
                start()

        def finish(dn):
            _, vjp = jax.vjp(_rms, x_r[...], g_r[...])
            dx, dg = vjp(dn)
            o_r[...] = r_r[...] + dx
            dg_r[...] += dg

        if nk == 1:
            finish(part)
        else:
            @pl.when(kk == 0)
            def _():
                acc[...] = part

            @pl.when((kk > 0) & (kk < nk - 1))
            def _():
                acc[...] += part

            @pl.when(kk == nk - 1)
            def _():
                finish(acc[...] + part)

        if ns:
            pl.when((i == ni - 1) & (kk == nk - 1))(done)

    row = pl.BlockSpec((tm, d), lambda i, kk: (i, 0))
    one = pl.BlockSpec((1, d), lambda i, kk: (0, 0))
    res_all = pl.pallas_call(
        body, name=name,
        out_shape=[jax.ShapeDtypeStruct((rows, d), f32), jax.ShapeDtypeStruct((1, d), f32)]
        + [jax.ShapeDtypeStruct((3,) + t.shape[1:], t.dtype) for t in send],
        grid=(ni, nk),
        in_specs=[pl.BlockSpec((tm, tk), lambda i, kk: (i, kk))] * num
        + [pl.BlockSpec((d, tk), lambda i, kk, p=p: (0, p * nk + kk)) for p in range(num)] + [row, one, row] + [_ANY] * ns,
        out_specs=[row, one] + [_ANY] * ns,
        scratch_shapes=[pltpu.VMEM((tm, d), f32)] + ([pltpu.SemaphoreType.DMA((3 * ns,))] * 2 if ns else []),
        compiler_params=_params(("arbitrary", "arbitrary")),
    )(*parts, *([w] * num), x, g, res, *send)
    return (res_all[0], res_all[1], list(res_all[2:])) if ns else (res_all[0], res_all[1])


def _swiglu(gate, up):
    return jax.nn.silu(gate) * up


def _ffn_up(n, w_gu, name):
    rows, d = n.shape
    f = w_gu.shape[1] // 2
    tm = _pick(rows, (640, 512, 384, 256, 128))
    tn = _pick(f, (1408, 1024, 512, 384, 256, 128))
    nj = f // tn

    def body(n_r, wg_r, wu_r, g_r, u_r, a_r):
        g = jnp.dot(n_r[...], wg_r[...], preferred_element_type=f32)
        u = jnp.dot(n_r[...], wu_r[...], preferred_element_type=f32)
        g_r[...] = g.astype(g_r.dtype)
        u_r[...] = u.astype(u_r.dtype)
        a_r[...] = _swiglu(g, u).astype(a_r.dtype)

    o_spec = pl.BlockSpec((tm, tn), lambda j, i: (i, j))
    return pl.pallas_call(
        body, name=name, grid=(nj, rows // tm), out_shape=[jax.ShapeDtypeStruct((rows, f), bf16)] * 3,
        in_specs=[pl.BlockSpec((tm, d), lambda j, i: (i, 0)), pl.BlockSpec((d, tn), lambda j, i: (0, j)),
                  pl.BlockSpec((d, tn), lambda j, i: (0, nj + j))],
        out_specs=[o_spec] * 3, compiler_params=_params(("parallel", "parallel")),
    )(n, w_gu, w_gu)


def _ffn_dact(dh, w_dn, gate, up, name):
    rows, d = dh.shape
    f = w_dn.shape[0]
    tm = _pick(rows, (640, 512, 384, 256, 128))
    tn = _pick(f, (1408, 1024, 512, 384, 256, 128))

    def body(dh_r, w_r, g_r, u_r, dg_r, du_r):
        dact = _bdot(dh_r[...], w_r[...], "nt")
        _, vjp = jax.vjp(_swiglu, g_r[...].astype(f32), u_r[...].astype(f32))
        dg, du = vjp(dact)
        dg_r[...] = dg.astype(dg_r.dtype)
        du_r[...] = du.astype(du_r.dtype)

    t_spec = pl.BlockSpec((tm, tn), lambda j, i: (i, j))
    return pl.pallas_call(
        body, name=name, grid=(f // tn, rows // tm), out_shape=[jax.ShapeDtypeStruct((rows, f), bf16)] * 2,
        in_specs=[pl.BlockSpec((tm, d), lambda j, i: (i, 0)), pl.BlockSpec((tn, d), lambda j, i: (j, 0)), t_spec, t_spec],
        out_specs=[t_spec] * 2, compiler_params=_params(("parallel", "parallel")),
    )(dh, w_dn, gate, up)


def _loss_head(h, g, tgt, lo, hi, name):
    rows, d = h.shape
    tm = _pick(rows, (640, 512, 384, 256, 128))

    def fn(i, hv, gv, tv):
        mask = _row_mask(i, tm, lo, hi, (tm, 1))

        def f(hh, gg):
            err = _rms(hh, gg) - tv
            per_row = jnp.where(mask, jnp.mean(err * err, axis=-1, keepdims=True), 0.0)
            return 0.5 * jnp.sum(per_row, axis=0, keepdims=True)

        loss, vjp = jax.vjp(f, hv, gv)
        dh, dg = vjp(jnp.ones_like(loss))
        return dh, dg, jnp.broadcast_to(loss, (1, LANES))

    return _rowwise(fn, name, rows, tm, [(h, (d, 0)), (g, None), (tgt, (d, 0))], [(d, f32)], [(1, d), (1, LANES)])


def _heads_l2(x):
    heads = [x[:, h * GDN_DIM:(h + 1) * GDN_DIM] for h in range(GDN_HEADS)]
    return jnp.concatenate([xh * lax.rsqrt(jnp.sum(xh * xh, axis=-1, keepdims=True) + EPS) for xh in heads], axis=1)


def _gdn_act(conv, a_in, b_in, a_log, dt_bias, mask):
    s = jax.nn.silu(conv)
    q = _heads_l2(s[:, :GDN_WIDTH])
    k = _heads_l2(s[:, GDN_WIDTH:2 * GDN_WIDTH])
    v = s[:, 2 * GDN_WIDTH:]
    g = jnp.where(mask, -jnp.exp(a_log) * jax.nn.softplus(a_in + dt_bias), 0.0)
    beta = jnp.where(mask, jax.nn.sigmoid(b_in), 0.0)
    return q, k, v, g, beta


def _widen(x8):
    return [jnp.broadcast_to(x8[:, h:h + 1], (x8.shape[0], GDN_DIM)) for h in range(GDN_HEADS)]


def _narrow(per_head):
    t = per_head[0].shape[0]
    lane = lax.broadcasted_iota(i32, (t, LANES), 1)
    out = jnp.zeros((t, LANES), f32)
    for h, x in enumerate(per_head):
        out = out + jnp.where(lane == h, jnp.sum(x, axis=1, keepdims=True), 0.0)
    return out


def _conv_taps(cur, prev8, w):
    tm = cur.shape[0]
    cat = jnp.concatenate([prev8, cur], axis=0)
    y = cur * w[CONV_WIDTH - 1:CONV_WIDTH, :]
    for j in range(1, CONV_WIDTH):
        y = y + pltpu.roll(cat, j, axis=0)[8:8 + tm, :] * w[CONV_WIDTH - 1 - j:CONV_WIDTH - j, :]
    return y


def _gdn_prep_specs(proj, tm):
    c3 = 3 * GDN_WIDTH
    ab = 4 * GDN_WIDTH // LANES
    t8 = tm // 8
    return [
        pl.BlockSpec((tm, c3), lambda i: (i, 0)),
        pl.BlockSpec((8, c3), lambda i: (jnp.maximum(i * t8 - 1, 0), 0)),
        pl.BlockSpec((tm, LANES), lambda i: (i, ab)),
        pl.BlockSpec((tm, LANES), lambda i: (i, ab + 1)),
    ]


def _full(arr):
    return pl.BlockSpec(arr.shape, lambda i, nd=arr.ndim: (0,) * nd)


def _gdn_prep_fwd(proj, conv_w, a_log, dt_bias, ltri, lo, hi, tm):
    rows = proj.shape[0]

    def body(cur, prev8, a_in, b_in, w, al, dtb, lt, q_o, k_o, v_o, g_o, b_o):
        i = pl.program_id(0)
        mask = _row_mask(i, tm, lo, hi, (tm, LANES)) & (lax.broadcasted_iota(i32, (tm, LANES), 1) < GDN_HEADS)
        conv = _conv_taps(cur[...], prev8[...], w[...])
        q, k, v, g, beta = _gdn_act(conv, a_in[...], b_in[...], al[...], dtb[...], mask)
        q_o[...] = q
        k_o[...] = k
        v_o[...] = v
        gcum = jnp.dot(lt[...], g, preferred_element_type=f32, precision=lax.Precision.HIGHEST)
        g_o[...] = gcum
        b_o[...] = beta

    wide = jax.ShapeDtypeStruct((rows, GDN_WIDTH), f32)
    narrow = jax.ShapeDtypeStruct((rows, LANES), f32)
    o_spec = pl.BlockSpec((tm, GDN_WIDTH), lambda i: (i, 0))
    n_spec = pl.BlockSpec((tm, LANES), lambda i: (i, 0))
    return pl.pallas_call(
        body, name="gdn_prep_fwd", out_shape=[wide] * 3 + [narrow] * 2, grid=(rows // tm,),
        in_specs=_gdn_prep_specs(proj, tm) + [_full(conv_w), _full(a_log), _full(dt_bias), _full(ltri)],
        out_specs=[o_spec] * 3 + [n_spec] * 2, compiler_params=_params(("parallel",)),
    )(proj, proj, proj, proj, conv_w, a_log, dt_bias, ltri)


def _gdn_prep_bwd_act(proj, conv_w, a_log, dt_bias, ltri, dq, dk, dv, dgw, dbw, lo, hi, tm):
    rows = proj.shape[0]
    c3 = 3 * GDN_WIDTH

    def body(cur, prev8, a_in, b_in, w, al, dtb, lt, dq_r, dk_r, dv_r, dg_r, db_r, dconv_o, da_o, dbin_o, dal_o, ddt_o):
        i = pl.program_id(0)
        mask = _row_mask(i, tm, lo, hi, (tm, LANES)) & (lax.broadcasted_iota(i32, (tm, LANES), 1) < GDN_HEADS)
        conv = _conv_taps(cur[...], prev8[...], w[...])
        dg = lax.dot_general(lt[...], dg_r[...], (((0,), (0,)), ((), ())), preferred_element_type=f32,
                             precision=lax.Precision.HIGHEST)
        dbeta = db_r[...]
        _, vjp = jax.vjp(lambda c, a, b, x, y: _gdn_act(c, a, b, x, y, mask), conv, a_in[...], b_in[...], al[...], dtb[...])
        dconv, da, dbin, dal, ddt = vjp((dq_r[...], dk_r[...], dv_r[...], dg, dbeta))
        dconv_o[...] = dconv
        da_o[...] = da
        dbin_o[...] = dbin

        @pl.when(i == 0)
        def _():
            dal_o[...] = jnp.zeros_like(dal_o)
            ddt_o[...] = jnp.zeros_like(ddt_o)

        dal_o[...] += dal
        ddt_o[...] += ddt

    w_spec = pl.BlockSpec((tm, GDN_WIDTH), lambda i: (i, 0))
    n_spec = pl.BlockSpec((tm, LANES), lambda i: (i, 0))
    s_spec = pl.BlockSpec((1, LANES), lambda i: (0, 0))
    return pl.pallas_call(
        body, name="gdn_prep_bwd_act",
        out_shape=[jax.ShapeDtypeStruct((rows, c3), f32), jax.ShapeDtypeStruct((rows, LANES), f32),
                   jax.ShapeDtypeStruct((rows, LANES), f32), jax.ShapeDtypeStruct((1, LANES), f32),
                   jax.ShapeDtypeStruct((1, LANES), f32)],
        grid=(rows // tm,),
        in_specs=_gdn_prep_specs(proj, tm) + [_full(conv_w), _full(a_log), _full(dt_bias), _full(ltri)] + [w_spec] * 3 + [n_spec] * 2,
        out_specs=[pl.BlockSpec((tm, c3), lambda i: (i, 0)), n_spec, n_spec, s_spec, s_spec],
        compiler_params=_params(("arbitrary",)),
    )(proj, proj, proj, proj, conv_w, a_log, dt_bias, ltri, dq, dk, dv, dgw, dbw)


def _gdn_prep_bwd_conv(proj, conv_w, dconv, dgate, da, dbin, tm):
    rows, width = proj.shape
    c3 = 3 * GDN_WIDTH
    t8 = tm // 8
    nt = rows // tm

    def body(cur, w, dc, dnext8, dgt, da_r, db_r, dp_o, dw_o):
        i = pl.program_id(0)
        d = dc[...]
        x = cur[...]
        nxt = jnp.where(i == nt - 1, 0.0, dnext8[...])
        cat = jnp.concatenate([d, nxt], axis=0)
        wv = w[...]
        dx = d * wv[CONV_WIDTH - 1:CONV_WIDTH, :]
        parts = [jnp.sum(d * x, axis=0, keepdims=True)]
        for j in range(1, CONV_WIDTH):
            ahead = pltpu.roll(cat, tm + 8 - j, axis=0)[:tm, :]
            dx = dx + ahead * wv[CONV_WIDTH - 1 - j:CONV_WIDTH - j, :]
            parts.append(jnp.sum(ahead * x, axis=0, keepdims=True))
        dp_o[:, :c3] = dx.astype(bf16)
        dp_o[:, c3:4 * GDN_WIDTH] = dgt[...].astype(bf16)
        dp_o[:, 4 * GDN_WIDTH:4 * GDN_WIDTH + LANES] = da_r[...].astype(bf16)
        dp_o[:, 4 * GDN_WIDTH + LANES:] = db_r[...].astype(bf16)
        dwt = jnp.concatenate(parts[::-1], axis=0)

        @pl.when(i == 0)
        def _():
            dw_o[...] = jnp.zeros_like(dw_o)

        dw_o[...] += dwt

    n_spec = pl.BlockSpec((tm, LANES), lambda i: (i, 0))
    return pl.pallas_call(
        body, name="gdn_prep_bwd_conv",
        out_shape=[jax.ShapeDtypeStruct((rows, width), bf16), jax.ShapeDtypeStruct((CONV_WIDTH, c3), f32)],
        grid=(nt,),
        in_specs=[pl.BlockSpec((tm, c3), lambda i: (i, 0)),
                  _full(conv_w),
                  pl.BlockSpec((tm, c3), lambda i: (i, 0)),
                  pl.BlockSpec((8, c3), lambda i: (jnp.minimum((i + 1) * t8, rows // 8 - 1), 0)),
                  pl.BlockSpec((tm, GDN_WIDTH), lambda i: (i, 0)), n_spec, n_spec],
        out_specs=[pl.BlockSpec((tm, width), lambda i: (i, 0)), pl.BlockSpec((CONV_WIDTH, c3), lambda i: (0, 0))],
        compiler_params=_params(("arbitrary",)),
    )(proj, conv_w, dconv, dconv, dgate, da, dbin)


def _split(a):
    hi = a.astype(bf16)
    return hi, (a - hi.astype(f32)).astype(bf16)


def _make_mm(dot):
    @jax.custom_vjp
    def nn(a, b):
        return dot(a, b, "nn")

    nn.defvjp(lambda a, b: (dot(a, b, "nn"), (a, b)),
              lambda r, ct: (dot(ct, r[1], "nt"), dot(r[0], ct, "tn")))

    @jax.custom_vjp
    def nt(a, b):
        return dot(a, b, "nt")

    nt.defvjp(lambda a, b: (dot(a, b, "nt"), (a, b)),
              lambda r, ct: (dot(ct, r[1], "nn"), dot(ct, r[0], "tn")))

    @jax.custom_vjp
    def tn(a, b):
        return dot(a, b, "tn")

    tn.defvjp(lambda a, b: (dot(a, b, "tn"), (a, b)),
              lambda r, ct: (dot(r[1], ct, "nt"), dot(r[0], ct, "nn")))
    return nn, nt, tn


_mm, _mm_nt, _mm_tn = _make_mm(_bdot)


def _each(f, *lists):
    return [f(*xs) for xs in zip(*lists)]


def _gdn_chunk(q, k, v, gcb, bcb, s_in):
    c = q[0].shape[0]
    ri = lax.broadcasted_iota(i32, (c, c), 0)
    ci = lax.broadcasted_iota(i32, (c, c), 1)
    incl, strict = ri >= ci, ri > ci
    rowi = lax.broadcasted_iota(i32, gcb[0].shape, 0)
    qs = _each(lambda t: t * (GDN_DIM ** -0.5), q)
    decay = _each(lambda g: jnp.where(incl, jnp.exp(jnp.where(incl, g[:, :c] - g[:, :c].T, 0.0)), 0.0), gcb)
    kk = _each(lambda t: _mm_nt(t, t), k)
    a1 = _each(lambda b, d, t: jnp.where(strict, b[:, :c] * d * t, 0.0), bcb, decay, kk)
    eg = _each(jnp.exp, gcb)
    x = _each(lambda b, vv, e, t: jnp.concatenate([b * vv, (b * e) * t], axis=1), bcb, v, eg, k)
    pows = [a1]
    for _ in range(5):
        pows.append(_each(lambda p: _mm(p, p), pows[-1]))
    for ps in pows[:0:-1]:
        x = _each(lambda p, t: t + _mm(p, t), ps, x)
    x = _each(lambda p, t: t - _mm(p, t), a1, x)
    attn = _each(lambda a, b, d: _mm_nt(a, b) * d, qs, k, decay)
    glast = _each(lambda g: jnp.sum(jnp.where(rowi == c - 1, g, 0.0), axis=0, keepdims=True), gcb)
    u = _each(lambda t, s: t[:, :GDN_DIM] - _mm(t[:, GDN_DIM:], s), x, s_in)
    o = _each(lambda a, e, s, w, uu: _mm(a * e, s) + _mm(w, uu), qs, eg, s_in, attn, u)
    s_out = _each(lambda s, gl, t, g, uu: s * jnp.exp(gl) + _mm_tn(t * jnp.exp(gl - g), uu), s_in, glast, k, gcb, u)
    return o, s_out


def _gdn_heads(ref):
    return [ref[:, h * GDN_DIM:(h + 1) * GDN_DIM] for h in range(GDN_HEADS)]


def _gdn_fwd(q, k, v, gw, bw, shards):
    rows = q.shape[0]
    nc = rows // CHUNK
    num = len(shards)
    blk = pl.BlockSpec((CHUNK, GDN_WIDTH), lambda c: (c, 0))

    def body(*refs):
        q_r, k_r, v_r, g_r, b_r = refs[:5]
        ins = refs[5:5 + num]
        o_r, st_r = refs[5 + num:7 + num]
        outs = refs[7 + num:7 + 2 * num]
        s_sc, send_sems, recv_sems, local_sems = refs[7 + 2 * num:]
        c = pl.program_id(0)
        start, forward, finish = _gather_plan(ins, outs, send_sems, recv_sems, local_sems)

        @pl.when(c == 0)
        def _():
            s_sc[...] = jnp.zeros_like(s_sc)
            start()

        s_in = [s_sc[h] for h in range(GDN_HEADS)]
        st_r[0] = s_sc[...]
        o, s_out = _gdn_chunk(_gdn_heads(q_r), _gdn_heads(k_r), _gdn_heads(v_r), _widen(g_r[...]), _widen(b_r[...]), s_in)
        o_r[...] = jnp.concatenate(o, axis=1)
        for h in range(GDN_HEADS):
            s_sc[h] = s_out[h]
        pl.when(c == nc // 2)(forward)
        pl.when(c == nc - 1)(finish)

    res = pl.pallas_call(
        body, name="gdn_fwd",
        out_shape=[jax.ShapeDtypeStruct((rows, GDN_WIDTH), f32), jax.ShapeDtypeStruct((nc, GDN_HEADS, GDN_DIM, GDN_DIM), f32)]
        + [jax.ShapeDtypeStruct((N_CHIPS,) + t.shape, t.dtype) for t in shards],
        grid=(nc,), in_specs=[blk] * 3 + [pl.BlockSpec((CHUNK, LANES), lambda c: (c, 0))] * 2 + [_ANY] * num,
        out_specs=[blk, pl.BlockSpec((1, GDN_HEADS, GDN_DIM, GDN_DIM), lambda c: (c, 0, 0, 0))] + [_ANY] * num,
        scratch_shapes=[pltpu.VMEM((GDN_HEADS, GDN_DIM, GDN_DIM), f32), pltpu.SemaphoreType.DMA((6 * num,)),
                        pltpu.SemaphoreType.DMA((6 * num,)), pltpu.SemaphoreType.DMA((num,))],
        compiler_params=_params(("arbitrary",)),
    )(q, k, v, gw, bw, *shards)
    return res[0], res[1], res[2:]


def _gdn_bwd(q, k, v, gw, bw, states, do, parts):
    rows = q.shape[0]
    nc = rows // CHUNK
    num = len(parts)
    blk = pl.BlockSpec((CHUNK, GDN_WIDTH), lambda c: (nc - 1 - c, 0))

    def body(*refs):
        q_r, k_r, v_r, g_r, b_r, st_r, do_r = refs[:7]
        ins = refs[7:7 + num]
        dq_r, dk_r, dv_r, dg_r, db_r = refs[7 + num:12 + num]
        outs = refs[12 + num:12 + 2 * num]
        ds_sc, send_sems, recv_sems = refs[12 + 2 * num:]
        c = pl.program_id(0)
        start, finish = _scatter_plan(ins, outs, send_sems, recv_sems)

        @pl.when(c == 0)
        def _():
            ds_sc[...] = jnp.zeros_like(ds_sc)
            start()

        s_in = [st_r[0, h] for h in range(GDN_HEADS)]
        _, vjp = jax.vjp(_gdn_chunk, _gdn_heads(q_r), _gdn_heads(k_r), _gdn_heads(v_r), _widen(g_r[...]), _widen(b_r[...]), s_in)
        dq, dk, dv, dg, db, ds_in = vjp((_gdn_heads(do_r), [ds_sc[h] for h in range(GDN_HEADS)]))
        dq_r[...] = jnp.concatenate(dq, axis=1)
        dk_r[...] = jnp.concatenate(dk, axis=1)
        dv_r[...] = jnp.concatenate(dv, axis=1)
        dg_r[...] = _narrow(dg)
        db_r[...] = _narrow(db)
        for h in range(GDN_HEADS):
            ds_sc[h] = ds_in[h]
        pl.when(c == nc - 1)(finish)

    wide = jax.ShapeDtypeStruct((rows, GDN_WIDTH), f32)
    narrow = jax.ShapeDtypeStruct((rows, LANES), f32)
    nblk = pl.BlockSpec((CHUNK, LANES), lambda c: (nc - 1 - c, 0))
    res = pl.pallas_call(
        body, name="gdn_bwd",
        out_shape=[wide] * 3 + [narrow] * 2 + [jax.ShapeDtypeStruct((3,) + t.shape[1:], t.dtype) for t in parts],
        grid=(nc,),
        in_specs=[blk] * 3 + [nblk] * 2
        + [pl.BlockSpec((1, GDN_HEADS, GDN_DIM, GDN_DIM), lambda c: (nc - 1 - c, 0, 0, 0)), blk] + [_ANY] * num,
        out_specs=[blk] * 3 + [nblk] * 2 + [_ANY] * num,
        scratch_shapes=[pltpu.VMEM((GDN_HEADS, GDN_DIM, GDN_DIM), f32), pltpu.SemaphoreType.DMA((3 * num,)),
                        pltpu.SemaphoreType.DMA((3 * num,))],
        compiler_params=_params(("arbitrary",)),
    )(q, k, v, gw, bw, states, do, *parts)
    return res[:5], res[5:]


def _gdn_gate(o, gate, og):
    heads = [o[:, h * GDN_DIM:(h + 1) * GDN_DIM] for h in range(GDN_HEADS)]
    n = jnp.concatenate([oh * lax.rsqrt(jnp.mean(oh * oh, axis=-1, keepdims=True) + EPS) * og for oh in heads], axis=1)
    return n * jax.nn.silu(gate)


def _gdn_gate_fwd(o, proj, og, tm):
    rows = o.shape[0]
    return _rowwise(lambda i, ov, gv, w: (_gdn_gate(ov, gv, w),), "gdn_gate_fwd", rows, tm,
                    [(o, (GDN_WIDTH, 0)), (proj, (GDN_WIDTH, 3)), (og, None)], [(GDN_WIDTH, bf16)])[0]


def _gdn_gate_bwd(o, proj, og, dy, tm):
    rows = o.shape[0]

    def fn(i, ov, gv, w, d):
        _, vjp = jax.vjp(_gdn_gate, ov, gv, w)
        return vjp(d)

    return _rowwise(fn, "gdn_gate_bwd", rows, tm,
                    [(o, (GDN_WIDTH, 0)), (proj, (GDN_WIDTH, 3)), (og, None), (dy, (GDN_WIDTH, 0))],
                    [(GDN_WIDTH, f32), (GDN_WIDTH, f32)], [(1, GDN_DIM)])


def _sb_visible(i, j, valid):
    qpos = i * SB_BLOCK + lax.broadcasted_iota(i32, (SB_BLOCK, SB_BLOCK), 0)
    kpos = j * SB_BLOCK + lax.broadcasted_iota(i32, (SB_BLOCK, SB_BLOCK), 1)
    return (kpos < qpos) & (kpos >= FRONT) & valid


def _sb_logs(z, vis):
    l1p = jnp.log(1.0 + jnp.exp(-jnp.abs(z)))
    return -(jnp.maximum(-z, 0.0) + l1p), jnp.where(vis, -(jnp.maximum(z, 0.0) + l1p), 0.0)


def _tri_sum(x, tri):
    hi, lo = _split(x)
    return jnp.dot(hi, tri, preferred_element_type=f32) + jnp.dot(lo, tri, preferred_element_type=f32)


def _sb_live(t, i, runs):
    return (t <= i) & (jnp.max(functools.reduce(jnp.maximum, runs)) > -SB_UNDERFLOW)


def _sb_blocks(i, t, nb):
    js = [i - t - b for b in range(nb)]
    kss = [pl.ds(pl.multiple_of(jnp.maximum(j, 0) * SB_BLOCK, SB_BLOCK), SB_BLOCK) for j in js]
    return kss, [_sb_visible(i, j, j >= 0) for j in js]


def _sb_weights(i, t, nb, qs, sls, k_r, runs, after, scale):
    nh = len(qs)
    kss, vis = _sb_blocks(i, t, nb)
    units = [(a, b) for b in range(nb) for a in range(nh)]
    z = [_bdot(qs[a], k_r[kss[b], sls[a]], "nt") * scale for a, b in units]
    logs = [_sb_logs(zz, vis[b]) for zz, (a, b) in zip(z, units)]
    later = [_tri_sum(l[1], after) for l in logs]
    sums = [jnp.sum(l[1], axis=1, keepdims=True) for l in logs]
    w = []
    runs = list(runs)
    for b in range(nb):
        for a in range(nh):
            u = b * nh + a
            w.append(jnp.where(vis[b], jnp.exp(logs[u][0] + later[u] + runs[a]), 0.0))
        runs = [runs[a] + sums[b * nh + a] for a in range(nh)]
    return kss, vis, units, logs, w, tuple(runs)


def _sb_fwd(q, kv, width):
    rows = q.shape[0]
    nq = rows // SB_BLOCK
    lanes = SB_FWD_HEADS * SB_DIM
    npair = width // lanes
    scale = SB_DIM ** -0.5

    def body(q_r, k_r, v_r, o_r):
        i = pl.program_id(1)
        rj = lax.broadcasted_iota(i32, (SB_BLOCK, SB_BLOCK), 0)
        cs = lax.broadcasted_iota(i32, (SB_BLOCK, SB_BLOCK), 1)
        after = (rj > cs).astype(bf16)
        sls = [slice(a * SB_DIM, (a + 1) * SB_DIM) for a in range(SB_FWD_HEADS)]
        qs = [q_r[:, sl] for sl in sls]

        def step(carry, nb):
            t, accs, runs = carry
            kss, _, units, _, w, runs = _sb_weights(i, t, nb, qs, sls, k_r, runs, after, scale)
            prods = [_bdot(ww, v_r[kss[b], sls[a]], "nn") for ww, (a, b) in zip(w, units)]
            accs = tuple(functools.reduce(jnp.add, [accs[a]] + prods[a::SB_FWD_HEADS]) for a in range(SB_FWD_HEADS))
            return t + nb, accs, runs

        init = (jnp.int32(0), tuple(jnp.zeros((SB_BLOCK, SB_DIM), f32) for _ in sls),
                tuple(jnp.zeros((SB_BLOCK, 1), f32) for _ in sls))
        _, accs, _ = lax.while_loop(lambda c: _sb_live(c[0], i, c[2]), lambda c: step(c, 2), step(init, SB_FIRST))
        o_r[...] = jnp.concatenate(accs, axis=1)

    return pl.pallas_call(
        body, name="sb_fwd", out_shape=jax.ShapeDtypeStruct((rows, width), f32), grid=(npair, nq),
        in_specs=[pl.BlockSpec((SB_BLOCK, lanes), lambda p, i: (i, p)),
                  pl.BlockSpec((rows, lanes), lambda p, i: (0, p)),
                  pl.BlockSpec((rows, lanes), lambda p, i: (0, npair + p))],
        out_specs=pl.BlockSpec((SB_BLOCK, lanes), lambda p, i: (i, p)),
        compiler_params=_params(("parallel", "arbitrary")),
    )(q, kv, kv)


def _sb_bwd(q, kv, do, width):
    rows = q.shape[0]
    nq = rows // SB_BLOCK
    npair = width // LANES
    nh = LANES // SB_DIM
    scale = SB_DIM ** -0.5

    def body(q_r, k_r, v_r, do_r, dq_r, dk_r, dv_r, e_sc, sig_sc, w_sc):
        i = pl.program_id(1)

        @pl.when(i == 0)
        def _():
            dk_r[...] = jnp.zeros_like(dk_r)
            dv_r[...] = jnp.zeros_like(dv_r)

        rj = lax.broadcasted_iota(i32, (SB_BLOCK, SB_BLOCK), 0)
        cs = lax.broadcasted_iota(i32, (SB_BLOCK, SB_BLOCK), 1)
        after = (rj > cs).astype(bf16)
        from_s = (rj >= cs).astype(bf16)
        zero1 = jnp.zeros((SB_BLOCK, 1), f32)
        sls = [slice(a * SB_DIM, (a + 1) * SB_DIM) for a in range(nh)]
        qs = [q_r[:, sl] for sl in sls]
        dos = [do_r[:, sl] for sl in sls]

        def weigh(carry, nb):
            t, runs, eruns = carry
            kss, _, units, logs, w, runs = _sb_weights(i, t, nb, qs, sls, k_r, runs, after, scale)
            dw = [_bdot(dos[a], v_r[kss[b], sls[a]], "nt") for a, b in units]
            e = [ww * d for ww, d in zip(w, dw)]
            for u, (a, b) in enumerate(units):
                e_sc[a, t + b] = e[u]
                sig_sc[a, t + b] = jnp.exp(logs[u][0])
                w_sc[a, t + b] = w[u].astype(w_sc.dtype)
            sums = [jnp.sum(ee, axis=1, keepdims=True) for ee in e]
            eruns = tuple(functools.reduce(jnp.add, [eruns[a]] + sums[a::nh]) for a in range(nh))
            return t + nb, runs, eruns

        n_blk, _, etots = lax.while_loop(lambda c: _sb_live(c[0], i, c[1]), lambda c: weigh(c, 2),
                                         weigh((jnp.int32(0), (zero1,) * nh, (zero1,) * nh), SB_FIRST))

        def push(t, carry, nb):
            dqs, eruns = carry
            kss, vis = _sb_blocks(i, t, nb)
            units = [(a, b) for b in range(nb) for a in range(nh)]
            e = [e_sc[a, t + b] for a, b in units]
            dvs = [_bdot(w_sc[a, t + b], dos[a], "tn") for a, b in units]
            upto = [_tri_sum(ee, from_s) for ee in e]
            sums = [jnp.sum(ee, axis=1, keepdims=True) for ee in e]
            dz = []
            eruns = list(eruns)
            for b in range(nb):
                for a in range(nh):
                    u = b * nh + a
                    sig = sig_sc[a, t + b]
                    before = etots[a] - eruns[a] - upto[u]
                    dz.append(jnp.where(vis[b], e[u] * (1.0 - sig) - before * sig, 0.0) * scale)
                eruns = [eruns[a] + sums[b * nh + a] for a in range(nh)]
            dks = [_bdot(d, qs[a], "tn") for d, (a, b) in zip(dz, units)]
            dqp = [_bdot(d, k_r[kss[b], sls[a]], "nn") for d, (a, b) in zip(dz, units)]
            for b in range(nb):
                dk_r[kss[b], :] += jnp.concatenate(dks[b * nh:(b + 1) * nh], axis=1)
                dv_r[kss[b], :] += jnp.concatenate(dvs[b * nh:(b + 1) * nh], axis=1)
            dqs = tuple(functools.reduce(jnp.add, [dqs[a]] + dqp[a::nh]) for a in range(nh))
            return dqs, tuple(eruns)

        first = push(jnp.int32(0), (tuple(jnp.zeros((SB_BLOCK, SB_DIM), f32) for _ in sls), (zero1,) * nh), SB_FIRST)
        dqs, _ = lax.fori_loop(0, (n_blk - SB_FIRST) // 2, lambda p, c: push(SB_FIRST + 2 * p, c, 2), first)
        dq_r[...] = jnp.concatenate(dqs, axis=1)

    blk = pl.BlockSpec((SB_BLOCK, LANES), lambda p, i: (i, p))
    col = pl.BlockSpec((rows, LANES), lambda p, i: (0, p))
    wide = jax.ShapeDtypeStruct((rows, width), f32)
    depth = nq + SB_FIRST
    return pl.pallas_call(
        body, name="sb_bwd", out_shape=[wide] * 3, grid=(npair, nq),
        in_specs=[blk, col, pl.BlockSpec((rows, LANES), lambda p, i: (0, npair + p)), blk],
        out_specs=[blk, col, col],
        scratch_shapes=[pltpu.VMEM((nh, depth, SB_BLOCK, SB_BLOCK), f32), pltpu.VMEM((nh, depth, SB_BLOCK, SB_BLOCK), f32),
                        pltpu.VMEM((nh, depth, SB_BLOCK, SB_BLOCK), bf16)],
        compiler_params=_params(("parallel", "arbitrary")),
    )(q, kv, kv, do)


_FLIPS = ((1, 0), (0, 1), (1, 1))
_ANY = pl.BlockSpec(memory_space=pl.ANY)


def _flip(v, a):
    return v + a - 2 * a * v


def _gather_plan(ins, outs, send_sems, recv_sems, local_sems):
    num = len(ins)
    x, y, c = lax.axis_index("x"), lax.axis_index("y"), lax.axis_index("c")
    me, sibling = (x, y, c), (x, y, 1 - c)
    chip = 2 * x + y
    others = [(_flip(x, a), _flip(y, b)) for a, b in _FLIPS]
    pairs = [(k, n, 2 * ox + oy) for k in range(num) for n, (ox, oy) in enumerate(others)]

    def half_of(ref, hc):
        half = ref.shape[0] // 2
        start = hc * half
        for align in (16, 8):
            if half % align == 0:
                start = pl.multiple_of(start, align)
                break
        return ref.at[pl.ds(start, half)]

    def copy(k, n, s, hc, to, src=None):
        dst = half_of(outs[k].at[s], hc)
        return pltpu.make_async_remote_copy(
            src_ref=dst if src is None else src, dst_ref=dst,
            send_sem=send_sems.at[6 * k + n], recv_sem=recv_sems.at[6 * k + n], device_id=to, device_id_type=MESH)

    mine = [pltpu.make_async_copy(ins[k], outs[k].at[chip], local_sems.at[k]) for k in range(num)]
    first = [copy(k, n, chip, c, (others[n][0], others[n][1], c), src=half_of(ins[k], c)) for k, n, _ in pairs]
    passed = [copy(k, 3 + n, s, c, sibling) for k, n, s in pairs]

    def start():
        for cp in mine + first:
            cp.start()

    def forward():
        for (k, n, s), fw in zip(pairs, passed):
            copy(k, n, s, c, me).wait_recv()
            fw.start()

    def finish():
        for k, n, s in pairs:
            copy(k, 3 + n, s, 1 - c, me).wait_recv()
        for cp in first + passed:
            cp.wait_send()
        for cp in mine:
            cp.wait()

    return start, forward, finish


def _gather_chips(shards):
    num = len(shards)

    def body(*refs):
        for phase in _gather_plan(refs[:num], refs[num:2 * num], *refs[2 * num:]):
            phase()

    return pl.pallas_call(
        body, name="gather_chips", out_shape=[jax.ShapeDtypeStruct((N_CHIPS,) + t.shape, t.dtype) for t in shards],
        in_specs=[_ANY] * num, out_specs=[_ANY] * num,
        scratch_shapes=[pltpu.SemaphoreType.DMA((6 * num,)), pltpu.SemaphoreType.DMA((6 * num,)),
                        pltpu.SemaphoreType.DMA((num,))],
    )(*shards)


def _scatter_plan(ins, outs, send_sems, recv_sems):
    x, y, c = lax.axis_index("x"), lax.axis_index("y"), lax.axis_index("c")
    cps = []
    for k in range(len(ins)):
        for n, (a, b) in enumerate(_FLIPS):
            ox, oy = _flip(x, a), _flip(y, b)
            cps.append(pltpu.make_async_remote_copy(
                src_ref=ins[k].at[2 * ox + oy], dst_ref=outs[k].at[n], send_sem=send_sems.at[3 * k + n],
                recv_sem=recv_sems.at[3 * k + n], device_id=(ox, oy, c), device_id_type=MESH))

    def start():
        for cp in cps:
            cp.start()

    def finish():
        for cp in cps:
            cp.wait()

    return start, finish


def _swap_sibling(arrs):
    num = len(arrs)

    def body(*refs):
        ins, outs = refs[:num], refs[num:2 * num]
        send_sems, recv_sems = refs[2 * num:]
        x, y, c = lax.axis_index("x"), lax.axis_index("y"), lax.axis_index("c")
        cps = [pltpu.make_async_remote_copy(src_ref=ins[k], dst_ref=outs[k], send_sem=send_sems.at[k],
                                            recv_sem=recv_sems.at[k], device_id=(x, y, 1 - c), device_id_type=MESH)
               for k in range(num)]
        for cp in cps:
            cp.start()
        for cp in cps:
            cp.wait()

    return pl.pallas_call(
        body, name="swap_sibling", out_shape=[jax.ShapeDtypeStruct(t.shape, t.dtype) for t in arrs],
        in_specs=[_ANY] * num, out_specs=[_ANY] * num,
        scratch_shapes=[pltpu.SemaphoreType.DMA((num,)), pltpu.SemaphoreType.DMA((num,))],
    )(*arrs)


def _gather_all(v):
    m_per, n = v.shape

    def body(x_ref, out_ref, send_sems, recv_sems, local_sem):
        x, y, c = lax.axis_index("x"), lax.axis_index("y"), lax.axis_index("c")
        me, sibling = (x, y, c), (x, y, 1 - c)
        chips = [(_flip(x, a), _flip(y, b)) for a, b in _FLIPS]

        def rows(px, py, pc):
            return out_ref.at[pl.ds(pl.multiple_of((4 * px + 2 * py + pc) * m_per, 8), m_per), :]

        def copy(k, block, to, src=None):
            return pltpu.make_async_remote_copy(
                src_ref=rows(*block) if src is None else src, dst_ref=rows(*block),
                send_sem=send_sems.at[k], recv_sem=recv_sems.at[k], device_id=to, device_id_type=MESH)

        mine = pltpu.make_async_copy(x_ref, rows(*me), local_sem)
        mine.start()
        first = [copy(0, me, sibling, src=x_ref)]
        first += [copy(1 + j, me, (*chip, c), src=x_ref) for j, chip in enumerate(chips)]
        for cp in first:
            cp.start()
        passed = [copy(4 + j, (*chip, c), sibling) for j, chip in enumerate(chips)]
        for j, chip in enumerate(chips):
            copy(1 + j, (*chip, c), me).wait_recv()
            passed[j].start()
        copy(0, sibling, me).wait_recv()
        for j, chip in enumerate(chips):
            copy(4 + j, (*chip, 1 - c), me).wait_recv()
        for cp in first + passed:
            cp.wait_send()
        mine.wait()

    return pl.pallas_call(
        body, name="gather_all", out_shape=jax.ShapeDtypeStruct((N_DEV * m_per, n), v.dtype),
        in_specs=[pl.BlockSpec(memory_space=pltpu.VMEM)], out_specs=pl.BlockSpec(memory_space=pltpu.VMEM),
        scratch_shapes=[pltpu.SemaphoreType.DMA((7,)), pltpu.SemaphoreType.DMA((7,)), pltpu.SemaphoreType.DMA],
    )(v)


def _sum_chips(parts, got, chip, name):
    cols = parts.shape[-1]
    rows = parts.size // (N_CHIPS * cols)
    tm = _pick(rows, (256, 128, 64, 32, 16))

    def body(chip_r, own_r, got_r, o_r):
        acc = own_r[0]
        for n in range(3):
            acc = acc + got_r[n].astype(f32)
        o_r[...] = acc

    return pl.pallas_call(
        body, name=name, out_shape=jax.ShapeDtypeStruct((rows, cols), f32),
        grid_spec=pltpu.PrefetchScalarGridSpec(
            num_scalar_prefetch=1, grid=(rows // tm,),
            in_specs=[pl.BlockSpec((1, tm, cols), lambda i, s: (s[0], i, 0)),
                      pl.BlockSpec((3, tm, cols), lambda i, s: (0, i, 0))],
            out_specs=pl.BlockSpec((tm, cols), lambda i, s: (i, 0))),
        compiler_params=_params(("parallel",)),
    )(chip, parts.reshape(N_CHIPS, rows, cols), got.reshape(3, rows, cols))


def _sum_devices(g, m_per):
    n = g.shape[1]

    def body(g_r, o_r):
        acc = g_r[0:m_per, :]
        for d in range(1, N_DEV):
            acc = acc + g_r[d * m_per:(d + 1) * m_per, :]
        o_r[...] = acc

    return pl.pallas_call(body, name="sum_devices", out_shape=jax.ShapeDtypeStruct((m_per, n), f32))(g)


def _adamw(w, gs, m, v, name):
    shape = w.shape
    cols = shape[-1]
    rows = w.size // cols
    tm = _pick(rows, (256, 128, 64, 32, 16, 8)) if rows * cols * 4 > (1 << 20) else rows

    def fn(i, wv, mv, vv, *gv):
        g = functools.reduce(jnp.add, gv)
        mn = ADAM_B1 * mv + (1.0 - ADAM_B1) * g
        vn = ADAM_B2 * vv + (1.0 - ADAM_B2) * jnp.square(g)
        m_hat = mn / (1.0 - ADAM_B1 ** ADAM_STEP)
        v_hat = vn / (1.0 - ADAM_B2 ** ADAM_STEP)
        delta = -ADAM_LR * (m_hat / (jnp.sqrt(v_hat) + ADAM_EPS) + ADAM_WD * wv)
        return g, delta, mn, vn

    outs = _rowwise(fn, name, rows, tm, [(t.reshape(rows, cols), (cols, 0)) for t in (w, m, v) + tuple(gs)], [(cols, f32)] * 4)
    return tuple(o.reshape(shape) for o in outs)


def _pack(pieces, rows, dtype):
    flat = jnp.concatenate([p.reshape(-1).astype(dtype) for p in pieces])
    return jnp.pad(flat, (0, rows * PACK_COLS - flat.size)).reshape(rows, PACK_COLS)


def _unpack(buf, shapes):
    lead = buf.shape[:-2]
    flat = buf.reshape(lead + (-1,))
    out, off = [], 0
    for s in shapes:
        n = 1
        for d in s:
            n *= d
        out.append(flat[..., off:off + n].reshape(lead + tuple(s)))
        off += n
    return out


def _join_cols(t):
    return jnp.moveaxis(t, 0, -2).reshape(t.shape[1:-1] + (N_CHIPS * t.shape[-1],))


def _join_rows(t):
    return t.reshape((N_CHIPS * t.shape[1],) + t.shape[2:])


def _split_cols(t, parts=N_CHIPS):
    r, cols = t.shape
    return jnp.moveaxis(t.reshape(r, parts, cols // parts), 1, 0)


def _split_rows(t):
    return t.reshape((N_CHIPS, t.shape[0] // N_CHIPS) + t.shape[1:])


def kernel(x, meta_tokens, gdn_norm_g, gdn_w_in, gdn_conv_w, gdn_a_log, gdn_dt_bias, gdn_onorm_g, gdn_w_out, kv_norm_g, w_kv, sb_norm_g, sb_w_q, sb_w_o, ffn_norm_g, ffn_w_gate_up, ffn_w_down, final_norm_g, loss_target, m_meta_tokens, m_gdn_norm_g, m_gdn_w_in, m_gdn_conv_w, m_gdn_a_log, m_gdn_dt_bias, m_gdn_onorm_g, m_gdn_w_out, m_kv_norm_g, m_w_kv, m_sb_norm_g, m_sb_w_q, m_sb_w_o, m_ffn_norm_g, m_ffn_w_gate_up, m_ffn_w_down, m_final_norm_g, v_meta_tokens, v_gdn_norm_g, v_gdn_w_in, v_gdn_conv_w, v_gdn_a_log, v_gdn_dt_bias, v_gdn_onorm_g, v_gdn_w_out, v_kv_norm_g, v_w_kv, v_sb_norm_g, v_sb_w_q, v_sb_w_o, v_ffn_norm_g, v_ffn_w_gate_up, v_ffn_w_down, v_final_norm_g):
    weights = dict(meta_tokens=meta_tokens, gdn_norm_g=gdn_norm_g, gdn_w_in=gdn_w_in, gdn_conv_w=gdn_conv_w,
                   gdn_a_log=gdn_a_log, gdn_dt_bias=gdn_dt_bias, gdn_onorm_g=gdn_onorm_g, gdn_w_out=gdn_w_out,
                   kv_norm_g=kv_norm_g, w_kv=w_kv, sb_norm_g=sb_norm_g, sb_w_q=sb_w_q, sb_w_o=sb_w_o,
                   ffn_norm_g=ffn_norm_g, ffn_w_gate_up=ffn_w_gate_up, ffn_w_down=ffn_w_down, final_norm_g=final_norm_g)
    m_in = dict(meta_tokens=m_meta_tokens, gdn_norm_g=m_gdn_norm_g, gdn_w_in=m_gdn_w_in, gdn_conv_w=m_gdn_conv_w,
                gdn_a_log=m_gdn_a_log, gdn_dt_bias=m_gdn_dt_bias, gdn_onorm_g=m_gdn_onorm_g, gdn_w_out=m_gdn_w_out,
                kv_norm_g=m_kv_norm_g, w_kv=m_w_kv, sb_norm_g=m_sb_norm_g, sb_w_q=m_sb_w_q, sb_w_o=m_sb_w_o,
                ffn_norm_g=m_ffn_norm_g, ffn_w_gate_up=m_ffn_w_gate_up, ffn_w_down=m_ffn_w_down, final_norm_g=m_final_norm_g)
    v_in = dict(meta_tokens=v_meta_tokens, gdn_norm_g=v_gdn_norm_g, gdn_w_in=v_gdn_w_in, gdn_conv_w=v_gdn_conv_w,
                gdn_a_log=v_gdn_a_log, gdn_dt_bias=v_gdn_dt_bias, gdn_onorm_g=v_gdn_onorm_g, gdn_w_out=v_gdn_w_out,
                kv_norm_g=v_kv_norm_g, w_kv=v_w_kv, sb_norm_g=v_sb_norm_g, sb_w_q=v_sb_w_q, sb_w_o=v_sb_w_o,
                ffn_norm_g=v_ffn_norm_g, ffn_w_gate_up=v_ffn_w_gate_up, ffn_w_down=v_ffn_w_down, final_norm_g=v_final_norm_g)
    names = list(weights)

    seq, d = x.shape[1], x.shape[2]
    lo_frames = FRONT + N_META
    used = lo_frames + seq
    rows = -(-used // SB_BLOCK) * SB_BLOCK
    tm = _pick(rows, (640, 512, 384, 256, 128))
    tp = _pick(rows, (320, 256, 128))
    n_ffn = ffn_w_gate_up.shape[0]
    sb_width = sb_w_q.shape[2]
    chip =2 * lax.axis_index("x") + lax.axis_index("y")

    big = [gdn_w_in[0], gdn_w_out[0], w_kv, sb_w_q[0], sb_w_o[0], ffn_w_gate_up, ffn_w_down]
    small = [meta_tokens, gdn_norm_g, gdn_conv_w[0]]
    n_early = 2
    big_bf16 = [t.astype(bf16) for t in big]
    w_in_s, w_out_s, small_g = _gather_chips(big_bf16[:n_early] + [_pack(small, 16, f32)])
    small_s = _unpack(small_g, [t.shape for t in small])
    w_in = _join_cols(w_in_s)
    pad_ab = jnp.zeros((d, LANES - GDN_HEADS), bf16)
    w_in_ext = jnp.concatenate([w_in[:, :4 * GDN_WIDTH], w_in[:, 4 * GDN_WIDTH:4 * GDN_WIDTH + GDN_HEADS], pad_ab,
                                w_in[:, 4 * GDN_WIDTH + GDN_HEADS:], pad_ab], axis=1)
    w_out = _join_rows(w_out_s)
    meta_full, gdn_g_full, conv_full = (_join_cols(t) for t in small_s)

    zeros = lambda n: jnp.zeros((n, d), f32)
    h0 = jnp.concatenate([zeros(FRONT), meta_full, x[0], zeros(rows - used)], axis=0)
    tgt = jnp.concatenate([zeros(lo_frames), loss_target[0], zeros(rows - used)], axis=0)
    pad8 = lambda t: jnp.pad(t, ((0, 0), (0, LANES - t.shape[1])))
    a_log8, dt_bias8 = pad8(gdn_a_log), pad8(gdn_dt_bias)
    r_i = jnp.arange(tp)
    ltri = ((r_i[:, None] >= r_i[None, :]) & (r_i[:, None] // CHUNK == r_i[None, :] // CHUNK)).astype(f32)
    ffn_g = [ffn_norm_g[l:l + 1] for l in range(n_ffn)]
    kv_g, fin_g = kv_norm_g.reshape(1, d), final_norm_g.reshape(1, d)

    n0 = _rms_fwd(h0, gdn_g_full, "gdn_norm")
    proj = _matmul(n0, w_in_ext, "nn", "gdn_proj")
    gq, gk, gv, gw, bw = _gdn_prep_fwd(proj, conv_full, a_log8, dt_bias8, ltri, FRONT, used, tp)
    g_o, g_states, (w_kv_s, w_q_s, w_o_s, w_gu_s, w_dn_s) = _gdn_fwd(gq, gk, gv, gw, bw, big_bf16[n_early:])
    w_kvf = _join_cols(w_kv_s)
    w_q = _join_rows(w_q_s)
    w_o = _join_rows(w_o_s)
    w_gu = [_join_cols(w_gu_s[:, l]) for l in range(n_ffn)]
    w_dn = [_join_rows(w_dn_s[:, l]) for l in range(n_ffn)]
    og = _gdn_gate_fwd(g_o, proj, gdn_onorm_g, tm)
    h1 = _matmul(og, w_out, "nn", "gdn_out", res=h0)

    def ffn_fwd(h, l):
        n = _rms_fwd(h, ffn_g[l], f"ffn{l}_norm")
        gate, up, act = _ffn_up(n, w_gu[l], f"ffn{l}_gate_up")
        return _matmul(act, w_dn[l], "nn", f"ffn{l}_down", res=h), (n, gate, up, act)

    h2, ffn0_saved = ffn_fwd(h1, 0)
    n_kv = _rms_fwd(h2, kv_g, "kv_norm")
    kv = _matmul(n_kv, w_kvf, "nn", "kv_proj", out_dtype=bf16)
    n_sb = _rms_fwd(h2, sb_norm_g, "sb_norm")
    sq = _matmul(n_sb, w_q, "nn", "q_proj", out_dtype=bf16)
    s_o = _sb_fwd(sq, kv, sb_width)
    h3 = _matmul(s_o, w_o, "nn", "sb_out", res=h2)
    h4, ffn1_saved = ffn_fwd(h3, 1)
    dh4, d_fin_g, loss_part = _loss_head(h4, fin_g, tgt, lo_frames, used, "loss_head")

    def ffn_bwd(dh, h, l, saved):
        n, gate, up, act = saved
        d_wdn = _matmul(act, dh, "tn", f"ffn{l}_d_w_down")
        d_gate, d_up = _ffn_dact(dh, w_dn[l], gate, up, f"ffn{l}_d_gate_up")
        d_wgu = jnp.concatenate([_split_cols(_matmul(n, d_gate, "tn", f"ffn{l}_d_w_gate"), N_CHIPS // 2),
                                 _split_cols(_matmul(n, d_up, "tn", f"ffn{l}_d_w_up"), N_CHIPS // 2)], axis=0)
        dh_in, dg = _norm_bwd([d_gate, d_up], w_gu[l], h, ffn_g[l], dh, f"ffn{l}_d_norm")
        return dh_in, d_wgu, d_wdn, dg

    dh3, d_wgu1, d_wdn1, d_ffn_g1 = ffn_bwd(dh4, h3, 1, ffn1_saved)
    d_wo = _matmul(s_o, dh3, "tn", "d_w_o")
    d_so = _matmul(dh3, w_o, "nt", "d_sb_o")
    d_sq, d_sk, d_sv = _sb_bwd(sq, kv, d_so, sb_width)
    d_wq = _matmul(n_sb, d_sq, "tn", "d_w_q")
    dh2, d_sb_g = _norm_bwd([d_sq], w_q, h2, sb_norm_g, dh3, "d_sb_norm")
    d_wkv = jnp.concatenate([_matmul(n_kv, d_sk, "tn", "d_w_k"), _matmul(n_kv, d_sv, "tn", "d_w_v")], axis=1)
    dh2, d_kv_g = _norm_bwd([d_sk, d_sv], w_kvf, h2, kv_g, dh2, "d_kv_norm")
    dh1, d_wgu0, d_wdn0, d_ffn_g0 = ffn_bwd(dh2, h1, 0, ffn0_saved)
    d_wout = _matmul(og, dh1, "tn", "d_w_out")
    d_og = _matmul(dh1, w_out, "nt", "d_gdn_gated")
    d_go, d_gate, d_onorm = _gdn_gate_bwd(g_o, proj, gdn_onorm_g, d_og, tm)
    by_chip = [None, _split_rows(d_wout), _split_cols(d_wkv), _split_rows(d_wq), _split_rows(d_wo),
               jnp.stack([d_wgu0, d_wgu1], axis=1),
               jnp.stack([_split_rows(d_wdn0), _split_rows(d_wdn1)], axis=1)]
    (d_gq, d_gk, d_gv, d_gw, d_bw), got_early = _gdn_bwd(gq, gk, gv, gw, bw, g_states, d_go,
                                                         [t.astype(bf16) for t in by_chip[1:]])
    dconv, d_a_in, d_b_in, d_a_log8, d_dt_bias8 = _gdn_prep_bwd_act(
        proj, conv_full, a_log8, dt_bias8, ltri, d_gq, d_gk, d_gv, d_gw, d_bw, FRONT, used, tp)
    dproj, d_conv = _gdn_prep_bwd_conv(proj, conv_full, dconv, d_gate, d_a_in, d_b_in, tp)
    d_win_ext = _matmul(n0, dproj, "tn", "d_w_in")
    d_win = jnp.concatenate([d_win_ext[:, :4 * GDN_WIDTH], d_win_ext[:, 4 * GDN_WIDTH:4 * GDN_WIDTH + GDN_HEADS],
                             d_win_ext[:, 4 * GDN_WIDTH + LANES:4 * GDN_WIDTH + LANES + GDN_HEADS]], axis=1)
    by_chip[0] = _split_cols(d_win)
    dh0, d_gdn_g, got_late = _norm_bwd([dproj], w_in_ext, h0, gdn_g_full, dh1, "d_gdn_norm",
                                       send=[by_chip[0].astype(bf16)])
    grad_x = dh0[lo_frames:used][None]

    got = got_late + list(got_early)
    chip_arr = jnp.reshape(chip, (1,)).astype(i32)
    over_chips = [_sum_chips(t, g, chip_arr, f"sum_chips_{k}") for k, (t, g) in enumerate(zip(by_chip, got))]
    over_sibling = _swap_sibling(over_chips)
    big_names = ["gdn_w_in", "gdn_w_out", "w_kv", "sb_w_q", "sb_w_o", "ffn_w_gate_up", "ffn_w_down"]
    g_big = dict(zip(big_names, zip(over_chips, over_sibling)))

    small_parts = [dh0[FRONT:lo_frames], d_gdn_g, d_conv, d_a_log8, d_dt_bias8, d_onorm, d_kv_g, d_sb_g,
                   d_ffn_g0, d_ffn_g1, d_fin_g, loss_part]
    s_rows = -(-sum(t.size for t in small_parts) // (8 * PACK_COLS)) * 8
    s_sum = _sum_devices(_gather_all(_pack(small_parts, s_rows, f32)), s_rows)
    (g_meta, g_gdn_g, g_conv, g_a_log8, g_dt8, g_onorm, g_kv_g, g_sb_g, g_ffn_g0, g_ffn_g1, g_fin_g,
     loss_v) = _unpack(s_sum, [t.shape for t in small_parts])
    col_shard = lambda t, w: lax.dynamic_slice_in_dim(t, chip * w, w, axis=t.ndim - 1)

    g_small = dict(
        meta_tokens=col_shard(g_meta, meta_tokens.shape[1]), gdn_norm_g=col_shard(g_gdn_g, gdn_norm_g.shape[1]),
        gdn_conv_w=col_shard(g_conv, gdn_conv_w.shape[2])[None],
        gdn_a_log=g_a_log8[:, :GDN_HEADS], gdn_dt_bias=g_dt8[:, :GDN_HEADS], gdn_onorm_g=g_onorm,
        kv_norm_g=g_kv_g.reshape(-1), sb_norm_g=g_sb_g, ffn_norm_g=jnp.concatenate([g_ffn_g0, g_ffn_g1], axis=0),
        final_norm_g=g_fin_g.reshape(-1))

    grads, delta, new_m, new_v = {}, {}, {}, {}
    for n in names:
        gs = g_big[n] if n in g_big else (g_small[n],)
        grads[n], delta[n], new_m[n], new_v[n] = _adamw(weights[n], gs, m_in[n], v_in[n], f"adamw_{n}")
    loss = loss_v[0, 0]
    return (loss, grad_x, *[grads[n] for n in names], *[delta[n] for n in names],
            *[new_m[n] for n in names], *[new_v[n] for n in names])
```

```python
import functools

import jax
import jax.numpy as jnp
from jax import lax
from jax.experimental import pallas as pl
from jax.experimental.pallas import tpu as pltpu

f32 = jnp.float32
bf16 = jnp.bfloat16
i32 = jnp.int32

EPS = 1e-6
N_META = 16
CHUNK = 64
FRONT = (-N_META) % CHUNK
GDN_HEADS = 8
GDN_DIM = 128
GDN_WIDTH = GDN_HEADS * GDN_DIM
CONV_WIDTH = 4
SB_DIM = 64
SB_BLOCK = 128
SB_FWD_HEADS = 4
SB_FIRST = 3
SB_UNDERFLOW = 104.0
LANES = 128
PACK_COLS = 1024
N_CHIPS = 4
N_DEV = 8
ADAM_LR, ADAM_B1, ADAM_B2, ADAM_EPS, ADAM_WD, ADAM_STEP = 0.001, 0.9, 0.999, 1e-08, 0.01, 10
VMEM_LIMIT = 56 * 1024 * 1024
MESH = pl.DeviceIdType.MESH


def _pick(n, prefs):
    for p in prefs:
        if n % p == 0:
            return p
    return n


def _params(sem):
    return pltpu.CompilerParams(dimension_semantics=sem, vmem_limit_bytes=VMEM_LIMIT)


_DIMS = {"nn": ((1,), (0,)), "nt": ((1,), (1,)), "tn": ((0,), (0,))}


def _bdot(a, b, mode):
    return lax.dot_general(a.astype(bf16), b.astype(bf16), (_DIMS[mode], ((), ())), preferred_element_type=f32)


def _matmul(a, b, mode, name, res=None, out_dtype=f32, norms=()):
    if mode == "nn":
        (m, k), n = a.shape, b.shape[1]
    elif mode == "nt":
        (m, k), n = a.shape, b.shape[0]
    else:
        (k, m), n = a.shape, b.shape[1]
    tm = _pick(m, (640, 1408, 1024, 512, 384, 256, 128))
    tn = _pick(n, (1408, 2176, 1024, 512, 384, 256, 128))
    tk = _pick(k, (1664, 1408, 2176, 1024, 640, 512, 384, 256, 128))
    nk = k // tk
    a_spec = pl.BlockSpec((tk, tm), lambda j, i, kk: (kk, i)) if mode == "tn" else pl.BlockSpec((tm, tk), lambda j, i, kk: (i, kk))
    b_spec = pl.BlockSpec((tn, tk), lambda j, i, kk: (j, kk)) if mode == "nt" else pl.BlockSpec((tk, tn), lambda j, i, kk: (kk, j))
    o_spec = pl.BlockSpec((tm, tn), lambda j, i, kk: (i, j))
    has_res = res is not None
    nn_ = len(norms)
    assert not nn_ or tn == n

    def body(*refs):
        a_ref, b_ref = refs[:2]
        r_ref = refs[2] if has_res else None
        first = 3 if has_res else 2
        g_refs = refs[first:first + nn_]
        o_ref = refs[first + nn_]
        n_refs = refs[first + nn_ + 1:first + 2 * nn_ + 1]

        def finish(y):
            if has_res:
                y = y + r_ref[...]
            o_ref[...] = y.astype(o_ref.dtype)
            for g_ref, n_ref in zip(g_refs, n_refs):
                n_ref[...] = _rms(y, g_ref[...]).astype(n_ref.dtype)

        if nk == 1:
            finish(_bdot(a_ref[...], b_ref[...], mode))
            return
        acc = refs[-1]
        kk = pl.program_id(2)
        part = _bdot(a_ref[...], b_ref[...], mode)

        @pl.when(kk == 0)
        def _():
            acc[...] = part

        @pl.when((kk > 0) & (kk < nk - 1))
        def _():
            acc[...] += part

        @pl.when(kk == nk - 1)
        def _():
            finish(acc[...] + part)

    ins = [a, b] + ([res] if has_res else []) + list(norms)
    specs = [a_spec, b_spec] + ([o_spec] if has_res else []) + [pl.BlockSpec((1, n), lambda j, i, kk: (0, 0))] * nn_
    out = pl.pallas_call(
        body, name=name,
        out_shape=[jax.ShapeDtypeStruct((m, n), out_dtype)] + [jax.ShapeDtypeStruct((m, n), bf16)] * nn_,
        grid=(n // tn, m // tm, nk), in_specs=specs, out_specs=[o_spec] * (1 + nn_),
        scratch_shapes=[pltpu.VMEM((tm, tn), f32)] if nk > 1 else [],
        compiler_params=_params(("parallel", "parallel", "arbitrary")),
    )(*ins)
    return out if nn_ else out[0]


def _rowwise(fn, name, rows, tm, ins, outs, reds=()):
    n_in, n_out, n_red = len(ins), len(outs), len(reds)
    in_specs = []
    for arr, spec in ins:
        if spec is None:
            in_specs.append(pl.BlockSpec(arr.shape, lambda i, nd=arr.ndim: (0,) * nd))
        else:
            w, cb = spec
            in_specs.append(pl.BlockSpec((tm, w), lambda i, cb=cb: (i, cb)))
    out_specs = [pl.BlockSpec((tm, w), lambda i: (i, 0)) for w, _ in outs]
    out_specs += [pl.BlockSpec(s, lambda i, nd=len(s): (0,) * nd) for s in reds]
    out_shape = [jax.ShapeDtypeStruct((rows, w), dt) for w, dt in outs]
    out_shape += [jax.ShapeDtypeStruct(s, f32) for s in reds]

    def body(*refs):
        i = pl.program_id(0)
        vals = fn(i, *[r[...] for r in refs[:n_in]])
        for r, v in zip(refs[n_in:n_in + n_out], vals[:n_out]):
            r[...] = v.astype(r.dtype)
        red_refs = refs[n_in + n_out:]

        @pl.when(i == 0)
        def _():
            for r in red_refs:
                r[...] = jnp.zeros_like(r)

        for r, v in zip(red_refs, vals[n_out:]):
            r[...] += v

    res = pl.pallas_call(
        body, name=name, out_shape=out_shape, grid=(rows // tm,), in_specs=in_specs, out_specs=out_specs,
        compiler_params=_params(("arbitrary",)),
    )(*[a for a, _ in ins])
    return res


def _rms(x, g):
    return x * lax.rsqrt(jnp.mean(x * x, axis=-1, keepdims=True) + EPS) * g


def _row_mask(i, tm, lo, hi, shape):
    r = i * tm + lax.broadcasted_iota(i32, shape, 0)
    return (r >= lo) & (r < hi)


def _rms_fwd(x, g, name):
    rows, d = x.shape
    tm = _pick(rows, (640, 512, 384, 256, 128))
    return _rowwise(lambda i, xv, gv: (_rms(xv, gv),), name, rows, tm, [(x, (d, 0)), (g, None)], [(d, bf16)])[0]


def _norm_bwd(parts, w, x, g, res, name, send=()):
    rows, k = parts[0].shape
    d = w.shape[0]
    num, ns = len(parts), len(send)
    tm = _pick(rows, (640, 512, 384, 256, 128))
    tk = _pick(k, (1408, 2176, 1024, 512, 384, 256, 128))
    ni, nk = rows // tm, k // tk

    def body(*refs):
        a_refs, w_refs = refs[:num], refs[num:2 * num]
        x_r, g_r, r_r = refs[2 * num:2 * num + 3]
        s_ins = refs[2 * num + 3:2 * num + 3 + ns]
        o_r, dg_r = refs[2 * num + 3 + ns:2 * num + 5 + ns]
        s_outs = refs[2 * num + 5 + ns:2 * num + 5 + 2 * ns]
        acc = refs[2 * num + 5 + 2 * ns]
        i, kk = pl.program_id(0), pl.program_id(1)
        start, done = _scatter_plan(s_ins, s_outs, *refs[2 * num + 6 + 2 * ns:]) if ns else (None, None)
        part = functools.reduce(jnp.add, [_bdot(a[...], b[...], "nt") for a, b in zip(a_refs, w_refs)])

        @pl.when((i == 0) & (kk == 0))
        def _():
            dg_r[...] = jnp.zeros_like(dg_r)
            if ns:
                start()

        def finish(dn):
            _, vjp = jax.vjp(_rms, x_r[...], g_r[...])
            dx, dg = vjp(dn)
            o_r[...] = r_r[...] + dx
            dg_r[...] += dg

        if nk == 1:
            finish(part)
        else:
            @pl.when(kk == 0)
            def _():
                acc[...] = part

            @pl.when((kk > 0) & (kk < nk - 1))
            def _():
                acc[...] += part

            @pl.when(kk == nk - 1)
            def _():
                finish(acc[...] + part)

        if ns:
            pl.when((i == ni - 1) & (kk == nk - 1))(done)

    row = pl.BlockSpec((tm, d), lambda i, kk: (i, 0))
    one = pl.BlockSpec((1, d), lambda i, kk: (0, 0))
    res_all = pl.pallas_call(
        body, name=name,
        out_shape=[jax.ShapeDtypeStruct((rows, d), f32), jax.ShapeDtypeStruct((1, d), f32)]
        + [jax.ShapeDtypeStruct((3,) + t.shape[1:], t.dtype) for t in send],
        grid=(ni, nk),
        in_specs=[pl.BlockSpec((tm, tk), lambda i, kk: (i, kk))] * num
        + [pl.BlockSpec((d, tk), lambda i, kk, p=p: (0, p * nk + kk)) for p in range(num)] + [row, one, row] + [_ANY] * ns,
        out_specs=[row, one] + [_ANY] * ns,
        scratch_shapes=[pltpu.VMEM((tm, d), f32)] + ([pltpu.SemaphoreType.DMA((3 * ns,))] * 2 if ns else []),
        compiler_params=_params(("arbitrary", "arbitrary")),
    )(*parts, *([w] * num), x, g, res, *send)
    return (res_all[0], res_all[1], list(res_all[2:])) if ns else (res_all[0], res_all[1])


def _swiglu(gate, up):
    return jax.nn.silu(gate) * up


def _ffn_up(n, w_gu, name):
    rows, d = n.shape
    f = w_gu.shape[1] // 2
    tm = _pick(rows, (640, 512, 384, 256, 128))
    tn = _pick(f, (1408, 1024, 512, 384, 256, 128))
    nj = f // tn

    def body(n_r, wg_r, wu_r, g_r, u_r, a_r):
        g = jnp.dot(n_r[...], wg_r[...], preferred_element_type=f32)
        u = jnp.dot(n_r[...], wu_r[...], preferred_element_type=f32)
        g_r[...] = g.astype(g_r.dtype)
        u_r[...] = u.astype(u_r.dtype)
        a_r[...] = _swiglu(g, u).astype(a_r.dtype)

    o_spec = pl.BlockSpec((tm, tn), lambda j, i: (i, j))
    return pl.pallas_call(
        body, name=name, grid=(nj, rows // tm), out_shape=[jax.ShapeDtypeStruct((rows, f), bf16)] * 3,
        in_specs=[pl.BlockSpec((tm, d), lambda j, i: (i, 0)), pl.BlockSpec((d, tn), lambda j, i: (0, j)),
                  pl.BlockSpec((d, tn), lambda j, i: (0, nj + j))],
        out_specs=[o_spec] * 3, compiler_params=_params(("parallel", "parallel")),
    )(n, w_gu, w_gu)


def _ffn_dact(dh, w_dn, gate, up, name):
    rows, d = dh.shape
    f = w_dn.shape[0]
    tm = _pick(rows, (640, 512, 384, 256, 128))
    tn = _pick(f, (1408, 1024, 512, 384, 256, 128))

    def body(dh_r, w_r, g_r, u_r, dg_r, du_r):
        dact = _bdot(dh_r[...], w_r[...], "nt")
        _, vjp = jax.vjp(_swiglu, g_r[...].astype(f32), u_r[...].astype(f32))
        dg, du = vjp(dact)
        dg_r[...] = dg.astype(dg_r.dtype)
        du_r[...] = du.astype(du_r.dtype)

    t_spec = pl.BlockSpec((tm, tn), lambda j, i: (i, j))
    return pl.pallas_call(
        body, name=name, grid=(f // tn, rows // tm), out_shape=[jax.ShapeDtypeStruct((rows, f), bf16)] * 2,
        in_specs=[pl.BlockSpec((tm, d), lambda j, i: (i, 0)), pl.BlockSpec((tn, d), lambda j, i: (j, 0)), t_spec, t_spec],
        out_specs=[t_spec] * 2, compiler_params=_params(("parallel", "parallel")),
    )(dh, w_dn, gate, up)


def _loss_head(h, g, tgt, lo, hi, name):
    rows, d = h.shape
    tm = _pick(rows, (640, 512, 384, 256, 128))

    def fn(i, hv, gv, tv):
        mask = _row_mask(i, tm, lo, hi, (tm, 1))

        def f(hh, gg):
            err = _rms(hh, gg) - tv
            per_row = jnp.where(mask, jnp.mean(err * err, axis=-1, keepdims=True), 0.0)
            return 0.5 * jnp.sum(per_row, axis=0, keepdims=True)

        loss, vjp = jax.vjp(f, hv, gv)
        dh, dg = vjp(jnp.ones_like(loss))
        return dh, dg, jnp.broadcast_to(loss, (1, LANES))

    return _rowwise(fn, name, rows, tm, [(h, (d, 0)), (g, None), (tgt, (d, 0))], [(d, f32)], [(1, d), (1, LANES)])


def _heads_l2(x):
    heads = [x[:, h * GDN_DIM:(h + 1) * GDN_DIM] for h in range(GDN_HEADS)]
    return jnp.concatenate([xh * lax.rsqrt(jnp.sum(xh * xh, axis=-1, keepdims=True) + EPS) for xh in heads], axis=1)


def _gdn_act(conv, a_in, b_in, a_log, dt_bias, mask):
    s = jax.nn.silu(conv)
    q = _heads_l2(s[:, :GDN_WIDTH])
    k = _heads_l2(s[:, GDN_WIDTH:2 * GDN_WIDTH])
    v = s[:, 2 * GDN_WIDTH:]
    g = jnp.where(mask, -jnp.exp(a_log) * jax.nn.softplus(a_in + dt_bias), 0.0)
    beta = jnp.where(mask, jax.nn.sigmoid(b_in), 0.0)
    return q, k, v, g, beta


def _widen(x8):
    return [jnp.broadcast_to(x8[:, h:h + 1], (x8.shape[0], GDN_DIM)) for h in range(GDN_HEADS)]


def _narrow(per_head):
    t = per_head[0].shape[0]
    lane = lax.broadcasted_iota(i32, (t, LANES), 1)
    out = jnp.zeros((t, LANES), f32)
    for h, x in enumerate(per_head):
        out = out + jnp.where(lane == h, jnp.sum(x, axis=1, keepdims=True), 0.0)
    return out


def _conv_taps(cur, prev8, w):
    tm = cur.shape[0]
    cat = jnp.concatenate([prev8, cur], axis=0)
    y = cur * w[CONV_WIDTH - 1:CONV_WIDTH, :]
    for j in range(1, CONV_WIDTH):
        y = y + pltpu.roll(cat, j, axis=0)[8:8 + tm, :] * w[CONV_WIDTH - 1 - j:CONV_WIDTH - j, :]
    return y


def _gdn_prep_specs(proj, tm):
    c3 = 3 * GDN_WIDTH
    ab = 4 * GDN_WIDTH // LANES
    t8 = tm // 8
    return [
        pl.BlockSpec((tm, c3), lambda i: (i, 0)),
        pl.BlockSpec((8, c3), lambda i: (jnp.maximum(i * t8 - 1, 0), 0)),
        pl.BlockSpec((tm, LANES), lambda i: (i, ab)),
        pl.BlockSpec((tm, LANES), lambda i: (i, ab + 1)),
    ]


def _full(arr):
    return pl.BlockSpec(arr.shape, lambda i, nd=arr.ndim: (0,) * nd)


def _gdn_prep_fwd(proj, conv_w, a_log, dt_bias, ltri, lo, hi, tm):
    rows = proj.shape[0]

    def body(cur, prev8, a_in, b_in, w, al, dtb, lt, q_o, k_o, v_o, g_o, b_o):
        i = pl.program_id(0)
        mask = _row_mask(i, tm, lo, hi, (tm, LANES)) & (lax.broadcasted_iota(i32, (tm, LANES), 1) < GDN_HEADS)
        conv = _conv_taps(cur[...], prev8[...], w[...])
        q, k, v, g, beta = _gdn_act(conv, a_in[...], b_in[...], al[...], dtb[...], mask)
        q_o[...] = q
        k_o[...] = k
        v_o[...] = v
        gcum = jnp.dot(lt[...], g, preferred_element_type=f32, precision=lax.Precision.HIGHEST)
        g_o[...] = gcum
        b_o[...] = beta

    wide = jax.ShapeDtypeStruct((rows, GDN_WIDTH), f32)
    narrow = jax.ShapeDtypeStruct((rows, LANES), f32)
    o_spec = pl.BlockSpec((tm, GDN_WIDTH), lambda i: (i, 0))
    n_spec = pl.BlockSpec((tm, LANES), lambda i: (i, 0))
    return pl.pallas_call(
        body, name="gdn_prep_fwd", out_shape=[wide] * 3 + [narrow] * 2, grid=(rows // tm,),
        in_specs=_gdn_prep_specs(proj, tm) + [_full(conv_w), _full(a_log), _full(dt_bias), _full(ltri)],
        out_specs=[o_spec] * 3 + [n_spec] * 2, compiler_params=_params(("parallel",)),
    )(proj, proj, proj, proj, conv_w, a_log, dt_bias, ltri)


def _gdn_prep_bwd_act(proj, conv_w, a_log, dt_bias, ltri, dq, dk, dv, dgw, dbw, lo, hi, tm):
    rows = proj.shape[0]
    c3 = 3 * GDN_WIDTH

    def body(cur, prev8, a_in, b_in, w, al, dtb, lt, dq_r, dk_r, dv_r, dg_r, db_r, dconv_o, da_o, dbin_o, dal_o, ddt_o):
        i = pl.program_id(0)
        mask = _row_mask(i, tm, lo, hi, (tm, LANES)) & (lax.broadcasted_iota(i32, (tm, LANES), 1) < GDN_HEADS)
        conv = _conv_taps(cur[...], prev8[...], w[...])
        dg = lax.dot_general(lt[...], dg_r[...], (((0,), (0,)), ((), ())), preferred_element_type=f32,
                             precision=lax.Precision.HIGHEST)
        dbeta = db_r[...]
        _, vjp = jax.vjp(lambda c, a, b, x, y: _gdn_act(c, a, b, x, y, mask), conv, a_in[...], b_in[...], al[...], dtb[...])
        dconv, da, dbin, dal, ddt = vjp((dq_r[...], dk_r[...], dv_r[...], dg, dbeta))
        dconv_o[...] = dconv
        da_o[...] = da
        dbin_o[...] = dbin

        @pl.when(i == 0)
        def _():
            dal_o[...] = jnp.zeros_like(dal_o)
            ddt_o[...] = jnp.zeros_like(ddt_o)

        dal_o[...] += dal
        ddt_o[...] += ddt

    w_spec = pl.BlockSpec((tm, GDN_WIDTH), lambda i: (i, 0))
    n_spec = pl.BlockSpec((tm, LANES), lambda i: (i, 0))
    s_spec = pl.BlockSpec((1, LANES), lambda i: (0, 0))
    return pl.pallas_call(
        body, name="gdn_prep_bwd_act",
        out_shape=[jax.ShapeDtypeStruct((rows, c3), f32), jax.ShapeDtypeStruct((rows, LANES), f32),
                   jax.ShapeDtypeStruct((rows, LANES), f32), jax.ShapeDtypeStruct((1, LANES), f32),
                   jax.ShapeDtypeStruct((1, LANES), f32)],
        grid=(rows // tm,),
        in_specs=_gdn_prep_specs(proj, tm) + [_full(conv_w), _full(a_log), _full(dt_bias), _full(ltri)] + [w_spec] * 3 + [n_spec] * 2,
        out_specs=[pl.BlockSpec((tm, c3), lambda i: (i, 0)), n_spec, n_spec, s_spec, s_spec],
        compiler_params=_params(("arbitrary",)),
    )(proj, proj, proj, proj, conv_w, a_log, dt_bias, ltri, dq, dk, dv, dgw, dbw)


def _gdn_prep_bwd_conv(proj, conv_w, dconv, dgate, da, dbin, tm):
    rows, width = proj.shape
    c3 = 3 * GDN_WIDTH
    t8 = tm // 8
    nt = rows // tm

    def body(cur, w, dc, dnext8, dgt, da_r, db_r, dp_o, dw_o):
        i = pl.program_id(0)
        d = dc[...]
        x = cur[...]
        nxt = jnp.where(i == nt - 1, 0.0, dnext8[...])
        cat = jnp.concatenate([d, nxt], axis=0)
        wv = w[...]
        dx = d * wv[CONV_WIDTH - 1:CONV_WIDTH, :]
        parts = [jnp.sum(d * x, axis=0, keepdims=True)]
        for j in range(1, CONV_WIDTH):
            ahead = pltpu.roll(cat, tm + 8 - j, axis=0)[:tm, :]
            dx = dx + ahead * wv[CONV_WIDTH - 1 - j:CONV_WIDTH - j, :]
            parts.append(jnp.sum(ahead * x, axis=0, keepdims=True))
        dp_o[:, :c3] = dx.astype(bf16)
        dp_o[:, c3:4 * GDN_WIDTH] = dgt[...].astype(bf16)
        dp_o[:, 4 * GDN_WIDTH:4 * GDN_WIDTH + LANES] = da_r[...].astype(bf16)
        dp_o[:, 4 * GDN_WIDTH + LANES:] = db_r[...].astype(bf16)
        dwt = jnp.concatenate(parts[::-1], axis=0)

        @pl.when(i == 0)
        def _():
            dw_o[...] = jnp.zeros_like(dw_o)

        dw_o[...] += dwt

    n_spec = pl.BlockSpec((tm, LANES), lambda i: (i, 0))
    return pl.pallas_call(
        body, name="gdn_prep_bwd_conv",
        out_shape=[jax.ShapeDtypeStruct((rows, width), bf16), jax.ShapeDtypeStruct((CONV_WIDTH, c3), f32)],
        grid=(nt,),
        in_specs=[pl.BlockSpec((tm, c3), lambda i: (i, 0)),
                  _full(conv_w),
                  pl.BlockSpec((tm, c3), lambda i: (i, 0)),
                  pl.BlockSpec((8, c3), lambda i: (jnp.minimum((i + 1) * t8, rows // 8 - 1), 0)),
                  pl.BlockSpec((tm, GDN_WIDTH), lambda i: (i, 0)), n_spec, n_spec],
        out_specs=[pl.BlockSpec((tm, width), lambda i: (i, 0)), pl.BlockSpec((CONV_WIDTH, c3), lambda i: (0, 0))],
        compiler_params=_params(("arbitrary",)),
    )(proj, conv_w, dconv, dconv, dgate, da, dbin)


def _split(a):
    hi = a.astype(bf16)
    return hi, (a - hi.astype(f32)).astype(bf16)


def _make_mm(dot):
    @jax.custom_vjp
    def nn(a, b):
        return dot(a, b, "nn")

    nn.defvjp(lambda a, b: (dot(a, b, "nn"), (a, b)),
              lambda r, ct: (dot(ct, r[1], "nt"), dot(r[0], ct, "tn")))

    @jax.custom_vjp
    def nt(a, b):
        return dot(a, b, "nt")

    nt.defvjp(lambda a, b: (dot(a, b, "nt"), (a, b)),
              lambda r, ct: (dot(ct, r[1], "nn"), dot(ct, r[0], "tn")))

    @jax.custom_vjp
    def tn(a, b):
        return dot(a, b, "tn")

    tn.defvjp(lambda a, b: (dot(a, b, "tn"), (a, b)),
              lambda r, ct: (dot(r[1], ct, "nt"), dot(r[0], ct, "nn")))
    return nn, nt, tn


_mm, _mm_nt, _mm_tn = _make_mm(_bdot)


def _each(f, *lists):
    return [f(*xs) for xs in zip(*lists)]


def _gdn_chunk(q, k, v, gcb, bcb, s_in):
    c = q[0].shape[0]
    ri = lax.broadcasted_iota(i32, (c, c), 0)
    ci = lax.broadcasted_iota(i32, (c, c), 1)
    incl, strict = ri >= ci, ri > ci
    rowi = lax.broadcasted_iota(i32, gcb[0].shape, 0)
    qs = _each(lambda t: t * (GDN_DIM ** -0.5), q)
    decay = _each(lambda g: jnp.where(incl, jnp.exp(jnp.where(incl, g[:, :c] - g[:, :c].T, 0.0)), 0.0), gcb)
    kk = _each(lambda t: _mm_nt(t, t), k)
    a1 = _each(lambda b, d, t: jnp.where(strict, b[:, :c] * d * t, 0.0), bcb, decay, kk)
    eg = _each(jnp.exp, gcb)
    x = _each(lambda b, vv, e, t: jnp.concatenate([b * vv, (b * e) * t], axis=1), bcb, v, eg, k)
    pows = [a1]
    for _ in range(5):
        pows.append(_each(lambda p: _mm(p, p), pows[-1]))
    for ps in pows[:0:-1]:
        x = _each(lambda p, t: t + _mm(p, t), ps, x)
    x = _each(lambda p, t: t - _mm(p, t), a1, x)
    attn = _each(lambda a, b, d: _mm_nt(a, b) * d, qs, k, decay)
    glast = _each(lambda g: jnp.sum(jnp.where(rowi == c - 1, g, 0.0), axis=0, keepdims=True), gcb)
    u = _each(lambda t, s: t[:, :GDN_DIM] - _mm(t[:, GDN_DIM:], s), x, s_in)
    o = _each(lambda a, e, s, w, uu: _mm(a * e, s) + _mm(w, uu), qs, eg, s_in, attn, u)
    s_out = _each(lambda s, gl, t, g, uu: s * jnp.exp(gl) + _mm_tn(t * jnp.exp(gl - g), uu), s_in, glast, k, gcb, u)
    return o, s_out


def _gdn_heads(ref):
    return [ref[:, h * GDN_DIM:(h + 1) * GDN_DIM] for h in range(GDN_HEADS)]


def _gdn_fwd(q, k, v, gw, bw, shards):
    rows = q.shape[0]
    nc = rows // CHUNK
    num = len(shards)
    blk = pl.BlockSpec((CHUNK, GDN_WIDTH), lambda c: (c, 0))

    def body(*refs):
        q_r, k_r, v_r, g_r, b_r = refs[:5]
        ins = refs[5:5 + num]
        o_r, st_r = refs[5 + num:7 + num]
        outs = refs[7 + num:7 + 2 * num]
        s_sc, send_sems, recv_sems, local_sems = refs[7 + 2 * num:]
        c = pl.program_id(0)
        start, forward, finish = _gather_plan(ins, outs, send_sems, recv_sems, local_sems)

        @pl.when(c == 0)
        def _():
            s_sc[...] = jnp.zeros_like(s_sc)
            start()

        s_in = [s_sc[h] for h in range(GDN_HEADS)]
        st_r[0] = s_sc[...]
        o, s_out = _gdn_chunk(_gdn_heads(q_r), _gdn_heads(k_r), _gdn_heads(v_r), _widen(g_r[...]), _widen(b_r[...]), s_in)
        o_r[...] = jnp.concatenate(o, axis=1)
        for h in range(GDN_HEADS):
            s_sc[h] = s_out[h]
        pl.when(c == nc // 2)(forward)
        pl.when(c == nc - 1)(finish)

    res = pl.pallas_call(
        body, name="gdn_fwd",
        out_shape=[jax.ShapeDtypeStruct((rows, GDN_WIDTH), f32), jax.ShapeDtypeStruct((nc, GDN_HEADS, GDN_DIM, GDN_DIM), f32)]
        + [jax.ShapeDtypeStruct((N_CHIPS,) + t.shape, t.dtype) for t in shards],
        grid=(nc,), in_specs=[blk] * 3 + [pl.BlockSpec((CHUNK, LANES), lambda c: (c, 0))] * 2 + [_ANY] * num,
        out_specs=[blk, pl.BlockSpec((1, GDN_HEADS, GDN_DIM, GDN_DIM), lambda c: (c, 0, 0, 0))] + [_ANY] * num,
        scratch_shapes=[pltpu.VMEM((GDN_HEADS, GDN_DIM, GDN_DIM), f32), pltpu.SemaphoreType.DMA((6 * num,)),
                        pltpu.SemaphoreType.DMA((6 * num,)), pltpu.SemaphoreType.DMA((num,))],
        compiler_params=_params(("arbitrary",)),
    )(q, k, v, gw, bw, *shards)
    return res[0], res[1], res[2:]


def _gdn_bwd(q, k, v, gw, bw, states, do, parts):
    rows = q.shape[0]
    nc = rows // CHUNK
    num = len(parts)
    blk = pl.BlockSpec((CHUNK, GDN_WIDTH), lambda c: (nc - 1 - c, 0))

    def body(*refs):
        q_r, k_r, v_r, g_r, b_r, st_r, do_r = refs[:7]
        ins = refs[7:7 + num]
        dq_r, dk_r, dv_r, dg_r, db_r = refs[7 + num:12 + num]
        outs = refs[12 + num:12 + 2 * num]
        ds_sc, send_sems, recv_sems = refs[12 + 2 * num:]
        c = pl.program_id(0)
        start, finish = _scatter_plan(ins, outs, send_sems, recv_sems)

        @pl.when(c == 0)
        def _():
            ds_sc[...] = jnp.zeros_like(ds_sc)
            start()

        s_in = [st_r[0, h] for h in range(GDN_HEADS)]
        _, vjp = jax.vjp(_gdn_chunk, _gdn_heads(q_r), _gdn_heads(k_r), _gdn_heads(v_r), _widen(g_r[...]), _widen(b_r[...]), s_in)
        dq, dk, dv, dg, db, ds_in = vjp((_gdn_heads(do_r), [ds_sc[h] for h in range(GDN_HEADS)]))
        dq_r[...] = jnp.concatenate(dq, axis=1)
        dk_r[...] = jnp.concatenate(dk, axis=1)
        dv_r[...] = jnp.concatenate(dv, axis=1)
        dg_r[...] = _narrow(dg)
        db_r[...] = _narrow(db)
        for h in range(GDN_HEADS):
            ds_sc[h] = ds_in[h]
        pl.when(c == nc - 1)(finish)

    wide = jax.ShapeDtypeStruct((rows, GDN_WIDTH), f32)
    narrow = jax.ShapeDtypeStruct((rows, LANES), f32)
    nblk = pl.BlockSpec((CHUNK, LANES), lambda c: (nc - 1 - c, 0))
    res = pl.pallas_call(
        body, name="gdn_bwd",
        out_shape=[wide] * 3 + [narrow] * 2 + [jax.ShapeDtypeStruct((3,) + t.shape[1:], t.dtype) for t in parts],
        grid=(nc,),
        in_specs=[blk] * 3 + [nblk] * 2
        + [pl.BlockSpec((1, GDN_HEADS, GDN_DIM, GDN_DIM), lambda c: (nc - 1 - c, 0, 0, 0)), blk] + [_ANY] * num,
        out_specs=[blk] * 3 + [nblk] * 2 + [_ANY] * num,
        scratch_shapes=[pltpu.VMEM((GDN_HEADS, GDN_DIM, GDN_DIM), f32), pltpu.SemaphoreType.DMA((3 * num,)),
                        pltpu.SemaphoreType.DMA((3 * num,))],
        compiler_params=_params(("arbitrary",)),
    )(q, k, v, gw, bw, states, do, *parts)
    return res[:5], res[5:]


def _gdn_gate(o, gate, og):
    heads = [o[:, h * GDN_DIM:(h + 1) * GDN_DIM] for h in range(GDN_HEADS)]
    n = jnp.concatenate([oh * lax.rsqrt(jnp.mean(oh * oh, axis=-1, keepdims=True) + EPS) * og for oh in heads], axis=1)
    return n * jax.nn.silu(gate)


def _gdn_gate_fwd(o, proj, og, tm):
    rows = o.shape[0]
    return _rowwise(lambda i, ov, gv, w: (_gdn_gate(ov, gv, w),), "gdn_gate_fwd", rows, tm,
                    [(o, (GDN_WIDTH, 0)), (proj, (GDN_WIDTH, 3)), (og, None)], [(GDN_WIDTH, bf16)])[0]


def _gdn_gate_bwd(o, proj, og, dy, tm):
    rows = o.shape[0]

    def fn(i, ov, gv, w, d):
        _, vjp = jax.vjp(_gdn_gate, ov, gv, w)
        return vjp(d)

    return _rowwise(fn, "gdn_gate_bwd", rows, tm,
                    [(o, (GDN_WIDTH, 0)), (proj, (GDN_WIDTH, 3)), (og, None), (dy, (GDN_WIDTH, 0))],
                    [(GDN_WIDTH, f32), (GDN_WIDTH, f32)], [(1, GDN_DIM)])


def _sb_visible(i, j, valid):
    qpos = i * SB_BLOCK + lax.broadcasted_iota(i32, (SB_BLOCK, SB_BLOCK), 0)
    kpos = j * SB_BLOCK + lax.broadcasted_iota(i32, (SB_BLOCK, SB_BLOCK), 1)
    return (kpos < qpos) & (kpos >= FRONT) & valid


def _sb_logs(z, vis):
    l1p = jnp.log(1.0 + jnp.exp(-jnp.abs(z)))
    return -(jnp.maximum(-z, 0.0) + l1p), jnp.where(vis, -(jnp.maximum(z, 0.0) + l1p), 0.0)


def _tri_sum(x, tri):
    hi, lo = _split(x)
    return jnp.dot(hi, tri, preferred_element_type=f32) + jnp.dot(lo, tri, preferred_element_type=f32)


def _sb_live(t, i, runs):
    return (t <= i) & (jnp.max(functools.reduce(jnp.maximum, runs)) > -SB_UNDERFLOW)


def _sb_blocks(i, t, nb):
    js = [i - t - b for b in range(nb)]
    kss = [pl.ds(pl.multiple_of(jnp.maximum(j, 0) * SB_BLOCK, SB_BLOCK), SB_BLOCK) for j in js]
    return kss, [_sb_visible(i, j, j >= 0) for j in js]


def _sb_weights(i, t, nb, qs, sls, k_r, runs, after, scale):
    nh = len(qs)
    kss, vis = _sb_blocks(i, t, nb)
    units = [(a, b) for b in range(nb) for a in range(nh)]
    z = [_bdot(qs[a], k_r[kss[b], sls[a]], "nt") * scale for a, b in units]
    logs = [_sb_logs(zz, vis[b]) for zz, (a, b) in zip(z, units)]
    later = [_tri_sum(l[1], after) for l in logs]
    sums = [jnp.sum(l[1], axis=1, keepdims=True) for l in logs]
    w = []
    runs = list(runs)
    for b in range(nb):
        for a in range(nh):
            u = b * nh + a
            w.append(jnp.where(vis[b], jnp.exp(logs[u][0] + later[u] + runs[a]), 0.0))
        runs = [runs[a] + sums[b * nh + a] for a in range(nh)]
    return kss, vis, units, logs, w, tuple(runs)


def _sb_fwd(q, kv, width):
    rows = q.shape[0]
    nq = rows // SB_BLOCK
    lanes = SB_FWD_HEADS * SB_DIM
    npair = width // lanes
    scale = SB_DIM ** -0.5

    def body(q_r, k_r, v_r, o_r):
        i = pl.program_id(1)
        rj = lax.broadcasted_iota(i32, (SB_BLOCK, SB_BLOCK), 0)
        cs = lax.broadcasted_iota(i32, (SB_BLOCK, SB_BLOCK), 1)
        after = (rj > cs).astype(bf16)
        sls = [slice(a * SB_DIM, (a + 1) * SB_DIM) for a in range(SB_FWD_HEADS)]
        qs = [q_r[:, sl] for sl in sls]

        def step(carry, nb):
            t, accs, runs = carry
            kss, _, units, _, w, runs = _sb_weights(i, t, nb, qs, sls, k_r, runs, after, scale)
            prods = [_bdot(ww, v_r[kss[b], sls[a]], "nn") for ww, (a, b) in zip(w, units)]
            accs = tuple(functools.reduce(jnp.add, [accs[a]] + prods[a::SB_FWD_HEADS]) for a in range(SB_FWD_HEADS))
            return t + nb, accs, runs

        init = (jnp.int32(0), tuple(jnp.zeros((SB_BLOCK, SB_DIM), f32) for _ in sls),
                tuple(jnp.zeros((SB_BLOCK, 1), f32) for _ in sls))
        _, accs, _ = lax.while_loop(lambda c: _sb_live(c[0], i, c[2]), lambda c: step(c, 2), step(init, SB_FIRST))
        o_r[...] = jnp.concatenate(accs, axis=1)

    return pl.pallas_call(
        body, name="sb_fwd", out_shape=jax.ShapeDtypeStruct((rows, width), f32), grid=(npair, nq),
        in_specs=[pl.BlockSpec((SB_BLOCK, lanes), lambda p, i: (i, p)),
                  pl.BlockSpec((rows, lanes), lambda p, i: (0, p)),
                  pl.BlockSpec((rows, lanes), lambda p, i: (0, npair + p))],
        out_specs=pl.BlockSpec((SB_BLOCK, lanes), lambda p, i: (i, p)),
        compiler_params=_params(("parallel", "arbitrary")),
    )(q, kv, kv)


def _sb_bwd(q, kv, do, width):
    rows = q.shape[0]
    nq = rows // SB_BLOCK
    npair = width // LANES
    nh = LANES // SB_DIM
    scale = SB_DIM ** -0.5

    def body(q_r, k_r, v_r, do_r, dq_r, dk_r, dv_r, e_sc, sig_sc, w_sc):
        i = pl.program_id(1)

        @pl.when(i == 0)
        def _():
            dk_r[...] = jnp.zeros_like(dk_r)
            dv_r[...] = jnp.zeros_like(dv_r)

        rj = lax.broadcasted_iota(i32, (SB_BLOCK, SB_BLOCK), 0)
        cs = lax.broadcasted_iota(i32, (SB_BLOCK, SB_BLOCK), 1)
        after = (rj > cs).astype(bf16)
        from_s = (rj >= cs).astype(bf16)
        zero1 = jnp.zeros((SB_BLOCK, 1), f32)
        sls = [slice(a * SB_DIM, (a + 1) * SB_DIM) for a in range(nh)]
        qs = [q_r[:, sl] for sl in sls]
        dos = [do_r[:, sl] for sl in sls]

        def weigh(carry, nb):
            t, runs, eruns = carry
            kss, _, units, logs, w, runs = _sb_weights(i, t, nb, qs, sls, k_r, runs, after, scale)
            dw = [_bdot(dos[a], v_r[kss[b], sls[a]], "nt") for a, b in units]
            e = [ww * d for ww, d in zip(w, dw)]
            for u, (a, b) in enumerate(units):
                e_sc[a, t + b] = e[u]
                sig_sc[a, t + b] = jnp.exp(logs[u][0])
                w_sc[a, t + b] = w[u].astype(w_sc.dtype)
            sums = [jnp.sum(ee, axis=1, keepdims=True) for ee in e]
            eruns = tuple(functools.reduce(jnp.add, [eruns[a]] + sums[a::nh]) for a in range(nh))
            return t + nb, runs, eruns

        n_blk, _, etots = lax.while_loop(lambda c: _sb_live(c[0], i, c[1]), lambda c: weigh(c, 2),
                                         weigh((jnp.int32(0), (zero1,) * nh, (zero1,) * nh), SB_FIRST))

        def push(t, carry, nb):
            dqs, eruns = carry
            kss, vis = _sb_blocks(i, t, nb)
            units = [(a, b) for b in range(nb) for a in range(nh)]
            e = [e_sc[a, t + b] for a, b in units]
            dvs = [_bdot(w_sc[a, t + b], dos[a], "tn") for a, b in units]
            upto = [_tri_sum(ee, from_s) for ee in e]
            sums = [jnp.sum(ee, axis=1, keepdims=True) for ee in e]
            dz = []
            eruns = list(eruns)
            for b in range(nb):
                for a in range(nh):
                    u = b * nh + a
                    sig = sig_sc[a, t + b]
                    before = etots[a] - eruns[a] - upto[u]
                    dz.append(jnp.where(vis[b], e[u] * (1.0 - sig) - before * sig, 0.0) * scale)
                eruns = [eruns[a] + sums[b * nh + a] for a in range(nh)]
            dks = [_bdot(d, qs[a], "tn") for d, (a, b) in zip(dz, units)]
            dqp = [_bdot(d, k_r[kss[b], sls[a]], "nn") for d, (a, b) in zip(dz, units)]
            for b in range(nb):
                dk_r[kss[b], :] += jnp.concatenate(dks[b * nh:(b + 1) * nh], axis=1)
                dv_r[kss[b], :] += jnp.concatenate(dvs[b * nh:(b + 1) * nh], axis=1)
            dqs = tuple(functools.reduce(jnp.add, [dqs[a]] + dqp[a::nh]) for a in range(nh))
            return dqs, tuple(eruns)

        first = push(jnp.int32(0), (tuple(jnp.zeros((SB_BLOCK, SB_DIM), f32) for _ in sls), (zero1,) * nh), SB_FIRST)
        dqs, _ = lax.fori_loop(0, (n_blk - SB_FIRST) // 2, lambda p, c: push(SB_FIRST + 2 * p, c, 2), first)
        dq_r[...] = jnp.concatenate(dqs, axis=1)

    blk = pl.BlockSpec((SB_BLOCK, LANES), lambda p, i: (i, p))
    col = pl.BlockSpec((rows, LANES), lambda p, i: (0, p))
    wide = jax.ShapeDtypeStruct((rows, width), f32)
    depth = nq + SB_FIRST
    return pl.pallas_call(
        body, name="sb_bwd", out_shape=[wide] * 3, grid=(npair, nq),
        in_specs=[blk, col, pl.BlockSpec((rows, LANES), lambda p, i: (0, npair + p)), blk],
        out_specs=[blk, col, col],
        scratch_shapes=[pltpu.VMEM((nh, depth, SB_BLOCK, SB_BLOCK), f32), pltpu.VMEM((nh, depth, SB_BLOCK, SB_BLOCK), f32),
                        pltpu.VMEM((nh, depth, SB_BLOCK, SB_BLOCK), bf16)],
        compiler_params=_params(("parallel", "arbitrary")),
    )(q, kv, kv, do)


_FLIPS = ((1, 0), (0, 1), (1, 1))
_ANY = pl.BlockSpec(memory_space=pl.ANY)


def _flip(v, a):
    return v + a - 2 * a * v


def _gather_plan(ins, outs, send_sems, recv_sems, local_sems):
    num = len(ins)
    x, y, c = lax.axis_index("x"), lax.axis_index("y"), lax.axis_index("c")
    me, sibling = (x, y, c), (x, y, 1 - c)
    chip = 2 * x + y
    others = [(_flip(x, a), _flip(y, b)) for a, b in _FLIPS]
    pairs = [(k, n, 2 * ox + oy) for k in range(num) for n, (ox, oy) in enumerate(others)]

    def half_of(ref, hc):
        half = ref.shape[0] // 2
        start = hc * half
        for align in (16, 8):
            if half % align == 0:
                start = pl.multiple_of(start, align)
                break
        return ref.at[pl.ds(start, half)]

    def copy(k, n, s, hc, to, src=None):
        dst = half_of(outs[k].at[s], hc)
        return pltpu.make_async_remote_copy(
            src_ref=dst if src is None else src, dst_ref=dst,
            send_sem=send_sems.at[6 * k + n], recv_sem=recv_sems.at[6 * k + n], device_id=to, device_id_type=MESH)

    mine = [pltpu.make_async_copy(ins[k], outs[k].at[chip], local_sems.at[k]) for k in range(num)]
    first = [copy(k, n, chip, c, (others[n][0], others[n][1], c), src=half_of(ins[k], c)) for k, n, _ in pairs]
    passed = [copy(k, 3 + n, s, c, sibling) for k, n, s in pairs]

    def start():
        for cp in mine + first:
            cp.start()

    def forward():
        for (k, n, s), fw in zip(pairs, passed):
            copy(k, n, s, c, me).wait_recv()
            fw.start()

    def finish():
        for k, n, s in pairs:
            copy(k, 3 + n, s, 1 - c, me).wait_recv()
        for cp in first + passed:
            cp.wait_send()
        for cp in mine:
            cp.wait()

    return start, forward, finish


def _gather_chips(shards):
    num = len(shards)

    def body(*refs):
        for phase in _gather_plan(refs[:num], refs[num:2 * num], *refs[2 * num:]):
            phase()

    return pl.pallas_call(
        body, name="gather_chips", out_shape=[jax.ShapeDtypeStruct((N_CHIPS,) + t.shape, t.dtype) for t in shards],
        in_specs=[_ANY] * num, out_specs=[_ANY] * num,
        scratch_shapes=[pltpu.SemaphoreType.DMA((6 * num,)), pltpu.SemaphoreType.DMA((6 * num,)),
                        pltpu.SemaphoreType.DMA((num,))],
    )(*shards)


def _scatter_plan(ins, outs, send_sems, recv_sems):
    x, y, c = lax.axis_index("x"), lax.axis_index("y"), lax.axis_index("c")
    cps = []
    for k in range(len(ins)):
        for n, (a, b) in enumerate(_FLIPS):
            ox, oy = _flip(x, a), _flip(y, b)
            cps.append(pltpu.make_async_remote_copy(
                src_ref=ins[k].at[2 * ox + oy], dst_ref=outs[k].at[n], send_sem=send_sems.at[3 * k + n],
                recv_sem=recv_sems.at[3 * k + n], device_id=(ox, oy, c), device_id_type=MESH))

    def start():
        for cp in cps:
            cp.start()

    def finish():
        for cp in cps:
            cp.wait()

    return start, finish


def _swap_sibling(arrs):
    num = len(arrs)

    def body(*refs):
        ins, outs = refs[:num], refs[num:2 * num]
        send_sems, recv_sems = refs[2 * num:]
        x, y, c = lax.axis_index("x"), lax.axis_index("y"), lax.axis_index("c")
        cps = [pltpu.make_async_remote_copy(src_ref=ins[k], dst_ref=outs[k], send_sem=send_sems.at[k],
                                            recv_sem=recv_sems.at[k], device_id=(x, y, 1 - c), device_id_type=MESH)
               for k in range(num)]
        for cp in cps:
            cp.start()
        for cp in cps:
            cp.wait()

    return pl.pallas_call(
        body, name="swap_sibling", out_shape=[jax.ShapeDtypeStruct(t.shape, t.dtype) for t in arrs],
        in_specs=[_ANY] * num, out_specs=[_ANY] * num,
        scratch_shapes=[pltpu.SemaphoreType.DMA((num,)), pltpu.SemaphoreType.DMA((num,))],
    )(*arrs)


def _gather_all(v):
    m_per, n = v.shape

    def body(x_ref, out_ref, send_sems, recv_sems, local_sem):
        x, y, c = lax.axis_index("x"), lax.axis_index("y"), lax.axis_index("c")
        me, sibling = (x, y, c), (x, y, 1 - c)
        chips = [(_flip(x, a), _flip(y, b)) for a, b in _FLIPS]

        def rows(px, py, pc):
            return out_ref.at[pl.ds(pl.multiple_of((4 * px + 2 * py + pc) * m_per, 8), m_per), :]

        def copy(k, block, to, src=None):
            return pltpu.make_async_remote_copy(
                src_ref=rows(*block) if src is None else src, dst_ref=rows(*block),
                send_sem=send_sems.at[k], recv_sem=recv_sems.at[k], device_id=to, device_id_type=MESH)

        mine = pltpu.make_async_copy(x_ref, rows(*me), local_sem)
        mine.start()
        first = [copy(0, me, sibling, src=x_ref)]
        first += [copy(1 + j, me, (*chip, c), src=x_ref) for j, chip in enumerate(chips)]
        for cp in first:
            cp.start()
        passed = [copy(4 + j, (*chip, c), sibling) for j, chip in enumerate(chips)]
        for j, chip in enumerate(chips):
            copy(1 + j, (*chip, c), me).wait_recv()
            passed[j].start()
        copy(0, sibling, me).wait_recv()
        for j, chip in enumerate(chips):
            copy(4 + j, (*chip, 1 - c), me).wait_recv()
        for cp in first + passed:
            cp.wait_send()
        mine.wait()

    return pl.pallas_call(
        body, name="gather_all", out_shape=jax.ShapeDtypeStruct((N_DEV * m_per, n), v.dtype),
        in_specs=[pl.BlockSpec(memory_space=pltpu.VMEM)], out_specs=pl.BlockSpec(memory_space=pltpu.VMEM),
        scratch_shapes=[pltpu.SemaphoreType.DMA((7,)), pltpu.SemaphoreType.DMA((7,)), pltpu.SemaphoreType.DMA],
    )(v)


def _sum_chips(parts, got, chip, name):
    cols = parts.shape[-1]
    rows = parts.size // (N_CHIPS * cols)
    tm = _pick(rows, (256, 128, 64, 32, 16))

    def body(chip_r, own_r, got_r, o_r):
        acc = own_r[0]
        for n in range(3):
            acc = acc + got_r[n].astype(f32)
        o_r[...] = acc

    return pl.pallas_call(
        body, name=name, out_shape=jax.ShapeDtypeStruct((rows, cols), f32),
        grid_spec=pltpu.PrefetchScalarGridSpec(
            num_scalar_prefetch=1, grid=(rows // tm,),
            in_specs=[pl.BlockSpec((1, tm, cols), lambda i, s: (s[0], i, 0)),
                      pl.BlockSpec((3, tm, cols), lambda i, s: (0, i, 0))],
            out_specs=pl.BlockSpec((tm, cols), lambda i, s: (i, 0))),
        compiler_params=_params(("parallel",)),
    )(chip, parts.reshape(N_CHIPS, rows, cols), got.reshape(3, rows, cols))


def _sum_devices(g, m_per):
    n = g.shape[1]

    def body(g_r, o_r):
        acc = g_r[0:m_per, :]
        for d in range(1, N_DEV):
            acc = acc + g_r[d * m_per:(d + 1) * m_per, :]
        o_r[...] = acc

    return pl.pallas_call(body, name="sum_devices", out_shape=jax.ShapeDtypeStruct((m_per, n), f32))(g)


def _adamw(w, gs, m, v, name):
    shape = w.shape
    cols = shape[-1]
    rows = w.size // cols
    tm = _pick(rows, (256, 128, 64, 32, 16, 8)) if rows * cols * 4 > (1 << 20) else rows

    def fn(i, wv, mv, vv, *gv):
        g = functools.reduce(jnp.add, gv)
        mn = ADAM_B1 * mv + (1.0 - ADAM_B1) * g
        vn = ADAM_B2 * vv + (1.0 - ADAM_B2) * jnp.square(g)
        m_hat = mn / (1.0 - ADAM_B1 ** ADAM_STEP)
        v_hat = vn / (1.0 - ADAM_B2 ** ADAM_STEP)
        delta = -ADAM_LR * (m_hat / (jnp.sqrt(v_hat) + ADAM_EPS) + ADAM_WD * wv)
        return g, delta, mn, vn

    outs = _rowwise(fn, name, rows, tm, [(t.reshape(rows, cols), (cols, 0)) for t in (w, m, v) + tuple(gs)], [(cols, f32)] * 4)
    return tuple(o.reshape(shape) for o in outs)


def _pack(pieces, rows, dtype):
    flat = jnp.concatenate([p.reshape(-1).astype(dtype) for p in pieces])
    return jnp.pad(flat, (0, rows * PACK_COLS - flat.size)).reshape(rows, PACK_COLS)


def _unpack(buf, shapes):
    lead = buf.shape[:-2]
    flat = buf.reshape(lead + (-1,))
    out, off = [], 0
    for s in shapes:
        n = 1
        for d in s:
            n *= d
        out.append(flat[..., off:off + n].reshape(lead + tuple(s)))
        off += n
    return out


def _join_cols(t):
    return jnp.moveaxis(t, 0, -2).reshape(t.shape[1:-1] + (N_CHIPS * t.shape[-1],))


def _join_rows(t):
    return t.reshape((N_CHIPS * t.shape[1],) + t.shape[2:])


def _split_cols(t, parts=N_CHIPS):
    r, cols = t.shape
    return jnp.moveaxis(t.reshape(r, parts, cols // parts), 1, 0)


def _split_rows(t):
    return t.reshape((N_CHIPS, t.shape[0] // N_CHIPS) + t.shape[1:])


def kernel(x, meta_tokens, gdn_norm_g, gdn_w_in, gdn_conv_w, gdn_a_log, gdn_dt_bias, gdn_onorm_g, gdn_w_out, kv_norm_g, w_kv, sb_norm_g, sb_w_q, sb_w_o, ffn_norm_g, ffn_w_gate_up, ffn_w_down, final_norm_g, loss_target, m_meta_tokens, m_gdn_norm_g, m_gdn_w_in, m_gdn_conv_w, m_gdn_a_log, m_gdn_dt_bias, m_gdn_onorm_g, m_gdn_w_out, m_kv_norm_g, m_w_kv, m_sb_norm_g, m_sb_w_q, m_sb_w_o, m_ffn_norm_g, m_ffn_w_gate_up, m_ffn_w_down, m_final_norm_g, v_meta_tokens, v_gdn_norm_g, v_gdn_w_in, v_gdn_conv_w, v_gdn_a_log, v_gdn_dt_bias, v_gdn_onorm_g, v_gdn_w_out, v_kv_norm_g, v_w_kv, v_sb_norm_g, v_sb_w_q, v_sb_w_o, v_ffn_norm_g, v_ffn_w_gate_up, v_ffn_w_down, v_final_norm_g):
    weights = dict(meta_tokens=meta_tokens, gdn_norm_g=gdn_norm_g, gdn_w_in=gdn_w_in, gdn_conv_w=gdn_conv_w,
                   gdn_a_log=gdn_a_log, gdn_dt_bias=gdn_dt_bias, gdn_onorm_g=gdn_onorm_g, gdn_w_out=gdn_w_out,
                   kv_norm_g=kv_norm_g, w_kv=w_kv, sb_norm_g=sb_norm_g, sb_w_q=sb_w_q, sb_w_o=sb_w_o,
                   ffn_norm_g=ffn_norm_g, ffn_w_gate_up=ffn_w_gate_up, ffn_w_down=ffn_w_down, final_norm_g=final_norm_g)
    m_in = dict(meta_tokens=m_meta_tokens, gdn_norm_g=m_gdn_norm_g, gdn_w_in=m_gdn_w_in, gdn_conv_w=m_gdn_conv_w,
                gdn_a_log=m_gdn_a_log, gdn_dt_bias=m_gdn_dt_bias, gdn_onorm_g=m_gdn_onorm_g, gdn_w_out=m_gdn_w_out,
                kv_norm_g=m_kv_norm_g, w_kv=m_w_kv, sb_norm_g=m_sb_norm_g, sb_w_q=m_sb_w_q, sb_w_o=m_sb_w_o,
                ffn_norm_g=m_ffn_norm_g, ffn_w_gate_up=m_ffn_w_gate_up, ffn_w_down=m_ffn_w_down, final_norm_g=m_final_norm_g)
    v_in = dict(meta_tokens=v_meta_tokens, gdn_norm_g=v_gdn_norm_g, gdn_w_in=v_gdn_w_in, gdn_conv_w=v_gdn_conv_w,
                gdn_a_log=v_gdn_a_log, gdn_dt_bias=v_gdn_dt_bias, gdn_onorm_g=v_gdn_onorm_g, gdn_w_out=v_gdn_w_out,
                kv_norm_g=v_kv_norm_g, w_kv=v_w_kv, sb_norm_g=v_sb_norm_g, sb_w_q=v_sb_w_q, sb_w_o=v_sb_w_o,
                ffn_norm_g=v_ffn_norm_g, ffn_w_gate_up=v_ffn_w_gate_up, ffn_w_down=v_ffn_w_down, final_norm_g=v_final_norm_g)
    names = list(weights)

    seq, d = x.shape[1], x.shape[2]
    lo_frames = FRONT + N_META
    used = lo_frames + seq
    rows = -(-used // SB_BLOCK) * SB_BLOCK
    tm = _pick(rows, (640, 512, 384, 256, 128))
    tp = _pick(rows, (320, 256, 128))
    n_ffn = ffn_w_gate_up.shape[0]
    sb_width = sb_w_q.shape[2]
    chip =2 * lax.axis_index("x") + lax.axis_index("y")

    big = [gdn_w_in[0], gdn_w_out[0], w_kv, sb_w_q[0], sb_w_o[0], ffn_w_gate_up, ffn_w_down]
    small = [meta_tokens, gdn_norm_g, gdn_conv_w[0]]
    n_early = 2
    big_bf16 = [t.astype(bf16) for t in big]
    w_in_s, w_out_s, small_g = _gather_chips(big_bf16[:n_early] + [_pack(small, 16, f32)])
    small_s = _unpack(small_g, [t.shape for t in small])
    w_in = _join_cols(w_in_s)
    pad_ab = jnp.zeros((d, LANES - GDN_HEADS), bf16)
    w_in_ext = jnp.concatenate([w_in[:, :4 * GDN_WIDTH], w_in[:, 4 * GDN_WIDTH:4 * GDN_WIDTH + GDN_HEADS], pad_ab,
                                w_in[:, 4 * GDN_WIDTH + GDN_HEADS:], pad_ab], axis=1)
    w_out = _join_rows(w_out_s)
    meta_full, gdn_g_full, conv_full = (_join_cols(t) for t in small_s)

    zeros = lambda n: jnp.zeros((n, d), f32)
    h0 = jnp.concatenate([zeros(FRONT), meta_full, x[0], zeros(rows - used)], axis=0)
    tgt = jnp.concatenate([zeros(lo_frames), loss_target[0], zeros(rows - used)], axis=0)
    pad8 = lambda t: jnp.pad(t, ((0, 0), (0, LANES - t.shape[1])))
    a_log8, dt_bias8 = pad8(gdn_a_log), pad8(gdn_dt_bias)
    r_i = jnp.arange(tp)
    ltri = ((r_i[:, None] >= r_i[None, :]) & (r_i[:, None] // CHUNK == r_i[None, :] // CHUNK)).astype(f32)
    ffn_g = [ffn_norm_g[l:l + 1] for l in range(n_ffn)]
    kv_g, fin_g = kv_norm_g.reshape(1, d), final_norm_g.reshape(1, d)

    n0 = _rms_fwd(h0, gdn_g_full, "gdn_norm")
    proj = _matmul(n0, w_in_ext, "nn", "gdn_proj")
    gq, gk, gv, gw, bw = _gdn_prep_fwd(proj, conv_full, a_log8, dt_bias8, ltri, FRONT, used, tp)
    g_o, g_states, (w_kv_s, w_q_s, w_o_s, w_gu_s, w_dn_s) = _gdn_fwd(gq, gk, gv, gw, bw, big_bf16[n_early:])
    w_kvf = _join_cols(w_kv_s)
    w_q = _join_rows(w_q_s)
    w_o = _join_rows(w_o_s)
    w_gu = [_join_cols(w_gu_s[:, l]) for l in range(n_ffn)]
    w_dn = [_join_rows(w_dn_s[:, l]) for l in range(n_ffn)]
    og = _gdn_gate_fwd(g_o, proj, gdn_onorm_g, tm)
    h1, n_f0 = _matmul(og, w_out, "nn", "gdn_out", res=h0, norms=[ffn_g[0]])

    def ffn_fwd(h, n, l, norms):
        gate, up, act = _ffn_up(n, w_gu[l], f"ffn{l}_gate_up")
        return _matmul(act, w_dn[l], "nn", f"ffn{l}_down", res=h, norms=norms), (n, gate, up, act)

    (h2, n_kv, n_sb), ffn0_saved = ffn_fwd(h1, n_f0, 0, [kv_g, sb_norm_g])
    kv = _matmul(n_kv, w_kvf, "nn", "kv_proj", out_dtype=bf16)
    sq = _matmul(n_sb, w_q, "nn", "q_proj", out_dtype=bf16)
    s_o = _sb_fwd(sq, kv, sb_width)
    h3, n_f1 = _matmul(s_o, w_o, "nn", "sb_out", res=h2, norms=[ffn_g[1]])
    h4, ffn1_saved = ffn_fwd(h3, n_f1, 1, [])
    dh4, d_fin_g, loss_part = _loss_head(h4, fin_g, tgt, lo_frames, used, "loss_head")

    def ffn_bwd(dh, h, l, saved):
        n, gate, up, act = saved
        d_wdn = _matmul(act, dh, "tn", f"ffn{l}_d_w_down")
        d_gate, d_up = _ffn_dact(dh, w_dn[l], gate, up, f"ffn{l}_d_gate_up")
        d_wgu = jnp.concatenate([_split_cols(_matmul(n, d_gate, "tn", f"ffn{l}_d_w_gate"), N_CHIPS // 2),
                                 _split_cols(_matmul(n, d_up, "tn", f"ffn{l}_d_w_up"), N_CHIPS // 2)], axis=0)
        dh_in, dg = _norm_bwd([d_gate, d_up], w_gu[l], h, ffn_g[l], dh, f"ffn{l}_d_norm")
        return dh_in, d_wgu, d_wdn, dg

    dh3, d_wgu1, d_wdn1, d_ffn_g1 = ffn_bwd(dh4, h3, 1, ffn1_saved)
    d_wo = _matmul(s_o, dh3, "tn", "d_w_o")
    d_so = _matmul(dh3, w_o, "nt", "d_sb_o")
    d_sq, d_sk, d_sv = _sb_bwd(sq, kv, d_so, sb_width)
    d_wq = _matmul(n_sb, d_sq, "tn", "d_w_q")
    dh2, d_sb_g = _norm_bwd([d_sq], w_q, h2, sb_norm_g, dh3, "d_sb_norm")
    d_wkv = jnp.concatenate([_matmul(n_kv, d_sk, "tn", "d_w_k"), _matmul(n_kv, d_sv, "tn", "d_w_v")], axis=1)
    dh2, d_kv_g = _norm_bwd([d_sk, d_sv], w_kvf, h2, kv_g, dh2, "d_kv_norm")
    dh1, d_wgu0, d_wdn0, d_ffn_g0 = ffn_bwd(dh2, h1, 0, ffn0_saved)
    d_wout = _matmul(og, dh1, "tn", "d_w_out")
    d_og = _matmul(dh1, w_out, "nt", "d_gdn_gated")
    d_go, d_gate, d_onorm = _gdn_gate_bwd(g_o, proj, gdn_onorm_g, d_og, tm)
    by_chip = [None, _split_rows(d_wout), _split_cols(d_wkv), _split_rows(d_wq), _split_rows(d_wo),
               jnp.stack([d_wgu0, d_wgu1], axis=1),
               jnp.stack([_split_rows(d_wdn0), _split_rows(d_wdn1)], axis=1)]
    (d_gq, d_gk, d_gv, d_gw, d_bw), got_early = _gdn_bwd(gq, gk, gv, gw, bw, g_states, d_go,
                                                         [t.astype(bf16) for t in by_chip[1:]])
    dconv, d_a_in, d_b_in, d_a_log8, d_dt_bias8 = _gdn_prep_bwd_act(
        proj, conv_full, a_log8, dt_bias8, ltri, d_gq, d_gk, d_gv, d_gw, d_bw, FRONT, used, tp)
    dproj, d_conv = _gdn_prep_bwd_conv(proj, conv_full, dconv, d_gate, d_a_in, d_b_in, tp)
    d_win_ext = _matmul(n0, dproj, "tn", "d_w_in")
    d_win = jnp.concatenate([d_win_ext[:, :4 * GDN_WIDTH], d_win_ext[:, 4 * GDN_WIDTH:4 * GDN_WIDTH + GDN_HEADS],
                             d_win_ext[:, 4 * GDN_WIDTH + LANES:4 * GDN_WIDTH + LANES + GDN_HEADS]], axis=1)
    by_chip[0] = _split_cols(d_win)
    dh0, d_gdn_g, got_late = _norm_bwd([dproj], w_in_ext, h0, gdn_g_full, dh1, "d_gdn_norm",
                                       send=[by_chip[0].astype(bf16)])
    grad_x = dh0[lo_frames:used][None]

    got = got_late + list(got_early)
    chip_arr = jnp.reshape(chip, (1,)).astype(i32)
    over_chips = [_sum_chips(t, g, chip_arr, f"sum_chips_{k}") for k, (t, g) in enumerate(zip(by_chip, got))]
    over_sibling = _swap_sibling(over_chips)
    big_names = ["gdn_w_in", "gdn_w_out", "w_kv", "sb_w_q", "sb_w_o", "ffn_w_gate_up", "ffn_w_down"]
    g_big = dict(zip(big_names, zip(over_chips, over_sibling)))

    small_parts = [dh0[FRONT:lo_frames], d_gdn_g, d_conv, d_a_log8, d_dt_bias8, d_onorm, d_kv_g, d_sb_g,
                   d_ffn_g0, d_ffn_g1, d_fin_g, loss_part]
    s_rows = -(-sum(t.size for t in small_parts) // (8 * PACK_COLS)) * 8
    s_sum = _sum_devices(_gather_all(_pack(small_parts, s_rows, f32)), s_rows)
    (g_meta, g_gdn_g, g_conv, g_a_log8, g_dt8, g_onorm, g_kv_g, g_sb_g, g_ffn_g0, g_ffn_g1, g_fin_g,
     loss_v) = _unpack(s_sum, [t.shape for t in small_parts])
    col_shard = lambda t, w: lax.dynamic_slice_in_dim(t, chip * w, w, axis=t.ndim - 1)

    g_small = dict(
        meta_tokens=col_shard(g_meta, meta_tokens.shape[1]), gdn_norm_g=col_shard(g_gdn_g, gdn_norm_g.shape[1]),
        gdn_conv_w=col_shard(g_conv, gdn_conv_w.shape[2])[None],
        gdn_a_log=g_a_log8[:, :GDN_HEADS], gdn_dt_bias=g_dt8[:, :GDN_HEADS], gdn_onorm_g=g_onorm,
        kv_norm_g=g_kv_g.reshape(-1), sb_norm_g=g_sb_g, ffn_norm_g=jnp.concatenate([g_ffn_g0, g_ffn_g1], axis=0),
        final_norm_g=g_fin_g.reshape(-1))

    grads, delta, new_m, new_v = {}, {}, {}, {}
    for n in names:
        gs = g_big[n] if n in g_big else (g_small[n],)
        grads[n], delta[n], new_m[n], new_v[n] = _adamw(weights[n], gs, m_in[n], v_in[n], f"adamw_{n}")
    loss = loss_v[0, 0]
    return (loss, grad_x, *[grads[n] for n in names], *[delta[n] for n in names],
            *[new_m[n] for n in names], *[new_v[n] for n in names])
```

```python
import functools

import jax
import jax.numpy as jnp
from jax import lax
from jax.experimental import pallas as pl
from jax.experimental.pallas import tpu as pltpu

f32 = jnp.float32
bf16 = jnp.bfloat16
i32 = jnp.int32

EPS = 1e-6
N_META = 16
CHUNK = 64
FRONT = (-N_META) % CHUNK
GDN_HEADS = 8
GDN_DIM = 128
GDN_WIDTH = GDN_HEADS * GDN_DIM
CONV_WIDTH = 4
SB_DIM = 64
SB_BLOCK = 128
SB_FWD_HEADS = 4
SB_FIRST = 3
SB_UNDERFLOW = 104.0
LANES = 128
PACK_COLS = 1024
N_CHIPS = 4
N_DEV = 8
ADAM_LR, ADAM_B1, ADAM_B2, ADAM_EPS, ADAM_WD, ADAM_STEP = 0.001, 0.9, 0.999, 1e-08, 0.01, 10
VMEM_LIMIT = 56 * 1024 * 1024
MESH = pl.DeviceIdType.MESH


def _pick(n, prefs):
    for p in prefs:
        if n % p == 0:
            return p
    return n


def _params(sem):
    return pltpu.CompilerParams(dimension_semantics=sem, vmem_limit_bytes=VMEM_LIMIT)


_DIMS = {"nn": ((1,), (0,)), "nt": ((1,), (1,)), "tn": ((0,), (0,))}


def _bdot(a, b, mode):
    return lax.dot_general(a.astype(bf16), b.astype(bf16), (_DIMS[mode], ((), ())), preferred_element_type=f32)


def _matmul(a, b, mode, name, res=None, out_dtype=f32, norms=()):
    if mode == "nn":
        (m, k), n = a.shape, b.shape[1]
    elif mode == "nt":
        (m, k), n = a.shape, b.shape[0]
    else:
        (k, m), n = a.shape, b.shape[1]
    tm = _pick(m, (640, 1408, 1024, 512, 384, 256, 128))
    tn = _pick(n, (1408, 2176, 1024, 512, 384, 256, 128))
    tk = _pick(k, (1664, 1408, 2176, 1024, 640, 512, 384, 256, 128))
    nk = k // tk
    a_spec = pl.BlockSpec((tk, tm), lambda j, i, kk: (kk, i)) if mode == "tn" else pl.BlockSpec((tm, tk), lambda j, i, kk: (i, kk))
    b_spec = pl.BlockSpec((tn, tk), lambda j, i, kk: (j, kk)) if mode == "nt" else pl.BlockSpec((tk, tn), lambda j, i, kk: (kk, j))
    o_spec = pl.BlockSpec((tm, tn), lambda j, i, kk: (i, j))
    has_res = res is not None
    nn_ = len(norms)
    assert not nn_ or tn == n

    def body(*refs):
        a_ref, b_ref = refs[:2]
        r_ref = refs[2] if has_res else None
        first = 3 if has_res else 2
        g_refs = refs[first:first + nn_]
        o_ref = refs[first + nn_]
        n_refs = refs[first + nn_ + 1:first + 2 * nn_ + 1]

        def finish(y):
            if has_res:
                y = y + r_ref[...]
            o_ref[...] = y.astype(o_ref.dtype)
            for g_ref, n_ref in zip(g_refs, n_refs):
                n_ref[...] = _rms(y, g_ref[...]).astype(n_ref.dtype)

        if nk == 1:
            finish(_bdot(a_ref[...], b_ref[...], mode))
            return
        acc = refs[-1]
        kk = pl.program_id(2)
        part = _bdot(a_ref[...], b_ref[...], mode)

        @pl.when(kk == 0)
        def _():
            acc[...] = part

        @pl.when((kk > 0) & (kk < nk - 1))
        def _():
            acc[...] += part

        @pl.when(kk == nk - 1)
        def _():
            finish(acc[...] + part)

    ins = [a, b] + ([res] if has_res else []) + list(norms)
    specs = [a_spec, b_spec] + ([o_spec] if has_res else []) + [pl.BlockSpec((1, n), lambda j, i, kk: (0, 0))] * nn_
    out = pl.pallas_call(
        body, name=name,
        out_shape=[jax.ShapeDtypeStruct((m, n), out_dtype)] + [jax.ShapeDtypeStruct((m, n), bf16)] * nn_,
        grid=(n // tn, m // tm, nk), in_specs=specs, out_specs=[o_spec] * (1 + nn_),
        scratch_shapes=[pltpu.VMEM((tm, tn), f32)] if nk > 1 else [],
        compiler_params=_params(("parallel", "parallel", "arbitrary")),
    )(*ins)
    return out if nn_ else out[0]


def _rowwise(fn, name, rows, tm, ins, outs, reds=()):
    n_in, n_out, n_red = len(ins), len(outs), len(reds)
    in_specs = []
    for arr, spec in ins:
        if spec is None:
            in_specs.append(pl.BlockSpec(arr.shape, lambda i, nd=arr.ndim: (0,) * nd))
        else:
            w, cb = spec
            in_specs.append(pl.BlockSpec((tm, w), lambda i, cb=cb: (i, cb)))
    out_specs = [pl.BlockSpec((tm, w), lambda i: (i, 0)) for w, _ in outs]
    out_specs += [pl.BlockSpec(s, lambda i, nd=len(s): (0,) * nd) for s in reds]
    out_shape = [jax.ShapeDtypeStruct((rows, w), dt) for w, dt in outs]
    out_shape += [jax.ShapeDtypeStruct(s, f32) for s in reds]

    def body(*refs):
        i = pl.program_id(0)
        vals = fn(i, *[r[...] for r in refs[:n_in]])
        for r, v in zip(refs[n_in:n_in + n_out], vals[:n_out]):
            r[...] = v.astype(r.dtype)
        red_refs = refs[n_in + n_out:]

        @pl.when(i == 0)
        def _():
            for r in red_refs:
                r[...] = jnp.zeros_like(r)

        for r, v in zip(red_refs, vals[n_out:]):
            r[...] += v

    res = pl.pallas_call(
        body, name=name, out_shape=out_shape, grid=(rows // tm,), in_specs=in_specs, out_specs=out_specs,
        compiler_params=_params(("arbitrary",)),
    )(*[a for a, _ in ins])
    return res


def _rms(x, g):
    return x * lax.rsqrt(jnp.mean(x * x, axis=-1, keepdims=True) + EPS) * g


def _row_mask(i, tm, lo, hi, shape):
    r = i * tm + lax.broadcasted_iota(i32, shape, 0)
    return (r >= lo) & (r < hi)


def _rms_fwd(x, g, name):
    rows, d = x.shape
    tm = _pick(rows, (640, 512, 384, 256, 128))
    return _rowwise(lambda i, xv, gv: (_rms(xv, gv),), name, rows, tm, [(x, (d, 0)), (g, None)], [(d, bf16)])[0]


def _norm_bwd(parts, w, x, g, res, name, send=()):
    rows, k = parts[0].shape
    d = w.shape[0]
    num, ns = len(parts), len(send)
    tm = _pick(rows, (640, 512, 384, 256, 128))
    tk = _pick(k, (1408, 2176, 1024, 512, 384, 256, 128))
    ni, nk = rows // tm, k // tk

    def body(*refs):
        a_refs, w_refs = refs[:num], refs[num:2 * num]
        x_r, g_r, r_r = refs[2 * num:2 * num + 3]
        s_ins = refs[2 * num + 3:2 * num + 3 + ns]
        o_r, dg_r = refs[2 * num + 3 + ns:2 * num + 5 + ns]
        s_outs = refs[2 * num + 5 + ns:2 * num + 5 + 2 * ns]
        acc = refs[2 * num + 5 + 2 * ns]
        i, kk = pl.program_id(0), pl.program_id(1)
        start, done = _scatter_plan(s_ins, s_outs, *refs[2 * num + 6 + 2 * ns:]) if ns else (None, None)
        part = functools.reduce(jnp.add, [_bdot(a[...], b[...], "nt") for a, b in zip(a_refs, w_refs)])

        @pl.when((i == 0) & (kk == 0))
        def _():
            dg_r[...] = jnp.zeros_like(dg_r)
            if ns:
                start()

        def finish(dn):
            _, vjp = jax.vjp(_rms, x_r[...], g_r[...])
            dx, dg = vjp(dn)
            o_r[...] = r_r[...] + dx
            dg_r[...] += dg

        if nk == 1:
            finish(part)
        else:
            @pl.when(kk == 0)
            def _():
                acc[...] = part

            @pl.when((kk > 0) & (kk < nk - 1))
            def _():
                acc[...] += part

            @pl.when(kk == nk - 1)
            def _():
                finish(acc[...] + part)

        if ns:
            pl.when((i == ni - 1) & (kk == nk - 1))(done)

    row = pl.BlockSpec((tm, d), lambda i, kk: (i, 0))
    one = pl.BlockSpec((1, d), lambda i, kk: (0, 0))
    res_all = pl.pallas_call(
        body, name=name,
        out_shape=[jax.ShapeDtypeStruct((rows, d), f32), jax.ShapeDtypeStruct((1, d), f32)]
        + [jax.ShapeDtypeStruct((3,) + t.shape[1:], t.dtype) for t in send],
        grid=(ni, nk),
        in_specs=[pl.BlockSpec((tm, tk), lambda i, kk: (i, kk))] * num
        + [pl.BlockSpec((d, tk), lambda i, kk, p=p: (0, p * nk + kk)) for p in range(num)] + [row, one, row] + [_ANY] * ns,
        out_specs=[row, one] + [_ANY] * ns,
        scratch_shapes=[pltpu.VMEM((tm, d), f32)] + ([pltpu.SemaphoreType.DMA((3 * ns,))] * 2 if ns else []),
        compiler_params=_params(("arbitrary", "arbitrary")),
    )(*parts, *([w] * num), x, g, res, *send)
    return (res_all[0], res_all[1], list(res_all[2:])) if ns else (res_all[0], res_all[1])


def _swiglu(gate, up):
    return jax.nn.silu(gate) * up


def _ffn_up(n, w_gu, name):
    rows, d = n.shape
    f = w_gu.shape[1] // 2
    tm = _pick(rows, (640, 512, 384, 256, 128))
    tn = _pick(f, (1408, 1024, 512, 384, 256, 128))
    nj = f // tn

    def body(n_r, wg_r, wu_r, g_r, u_r, a_r):
        g = jnp.dot(n_r[...], wg_r[...], preferred_element_type=f32)
        u = jnp.dot(n_r[...], wu_r[...], preferred_element_type=f32)
        g_r[...] = g.astype(g_r.dtype)
        u_r[...] = u.astype(u_r.dtype)
        a_r[...] = _swiglu(g, u).astype(a_r.dtype)

    o_spec = pl.BlockSpec((tm, tn), lambda j, i: (i, j))
    return pl.pallas_call(
        body, name=name, grid=(nj, rows // tm), out_shape=[jax.ShapeDtypeStruct((rows, f), bf16)] * 3,
        in_specs=[pl.BlockSpec((tm, d), lambda j, i: (i, 0)), pl.BlockSpec((d, tn), lambda j, i: (0, j)),
                  pl.BlockSpec((d, tn), lambda j, i: (0, nj + j))],
        out_specs=[o_spec] * 3, compiler_params=_params(("parallel", "parallel")),
    )(n, w_gu, w_gu)


def _ffn_dact(dh, w_dn, gate, up, name):
    rows, d = dh.shape
    f = w_dn.shape[0]
    tm = _pick(rows, (640, 512, 384, 256, 128))
    tn = _pick(f, (1408, 1024, 512, 384, 256, 128))

    def body(dh_r, w_r, g_r, u_r, dg_r, du_r):
        dact = _bdot(dh_r[...], w_r[...], "nt")
        _, vjp = jax.vjp(_swiglu, g_r[...].astype(f32), u_r[...].astype(f32))
        dg, du = vjp(dact)
        dg_r[...] = dg.astype(dg_r.dtype)
        du_r[...] = du.astype(du_r.dtype)

    t_spec = pl.BlockSpec((tm, tn), lambda j, i: (i, j))
    return pl.pallas_call(
        body, name=name, grid=(f // tn, rows // tm), out_shape=[jax.ShapeDtypeStruct((rows, f), bf16)] * 2,
        in_specs=[pl.BlockSpec((tm, d), lambda j, i: (i, 0)), pl.BlockSpec((tn, d), lambda j, i: (j, 0)), t_spec, t_spec],
        out_specs=[t_spec] * 2, compiler_params=_params(("parallel", "parallel")),
    )(dh, w_dn, gate, up)


def _loss_head(h, g, tgt, lo, hi, name):
    rows, d = h.shape
    tm = _pick(rows, (640, 512, 384, 256, 128))

    def fn(i, hv, gv, tv):
        mask = _row_mask(i, tm, lo, hi, (tm, 1))

        def f(hh, gg):
            err = _rms(hh, gg) - tv
            per_row = jnp.where(mask, jnp.mean(err * err, axis=-1, keepdims=True), 0.0)
            return 0.5 * jnp.sum(per_row, axis=0, keepdims=True)

        loss, vjp = jax.vjp(f, hv, gv)
        dh, dg = vjp(jnp.ones_like(loss))
        return dh, dg, jnp.broadcast_to(loss, (1, LANES))

    return _rowwise(fn, name, rows, tm, [(h, (d, 0)), (g, None), (tgt, (d, 0))], [(d, f32)], [(1, d), (1, LANES)])


def _heads_l2(x):
    heads = [x[:, h * GDN_DIM:(h + 1) * GDN_DIM] for h in range(GDN_HEADS)]
    return jnp.concatenate([xh * lax.rsqrt(jnp.sum(xh * xh, axis=-1, keepdims=True) + EPS) for xh in heads], axis=1)


def _gdn_act(conv, a_in, b_in, a_log, dt_bias, mask):
    s = jax.nn.silu(conv)
    q = _heads_l2(s[:, :GDN_WIDTH])
    k = _heads_l2(s[:, GDN_WIDTH:2 * GDN_WIDTH])
    v = s[:, 2 * GDN_WIDTH:]
    g = jnp.where(mask, -jnp.exp(a_log) * jax.nn.softplus(a_in + dt_bias), 0.0)
    beta = jnp.where(mask, jax.nn.sigmoid(b_in), 0.0)
    return q, k, v, g, beta


def _widen(x8):
    return [jnp.broadcast_to(x8[:, h:h + 1], (x8.shape[0], GDN_DIM)) for h in range(GDN_HEADS)]


def _narrow(per_head):
    t = per_head[0].shape[0]
    lane = lax.broadcasted_iota(i32, (t, LANES), 1)
    out = jnp.zeros((t, LANES), f32)
    for h, x in enumerate(per_head):
        out = out + jnp.where(lane == h, jnp.sum(x, axis=1, keepdims=True), 0.0)
    return out


def _conv_taps(cur, prev8, w):
    tm = cur.shape[0]
    cat = jnp.concatenate([prev8, cur], axis=0)
    y = cur * w[CONV_WIDTH - 1:CONV_WIDTH, :]
    for j in range(1, CONV_WIDTH):
        y = y + pltpu.roll(cat, j, axis=0)[8:8 + tm, :] * w[CONV_WIDTH - 1 - j:CONV_WIDTH - j, :]
    return y


def _gdn_prep_specs(proj, tm):
    c3 = 3 * GDN_WIDTH
    ab = 4 * GDN_WIDTH // LANES
    t8 = tm // 8
    return [
        pl.BlockSpec((tm, c3), lambda i: (i, 0)),
        pl.BlockSpec((8, c3), lambda i: (jnp.maximum(i * t8 - 1, 0), 0)),
        pl.BlockSpec((tm, LANES), lambda i: (i, ab)),
        pl.BlockSpec((tm, LANES), lambda i: (i, ab + 1)),
    ]


def _full(arr):
    return pl.BlockSpec(arr.shape, lambda i, nd=arr.ndim: (0,) * nd)


def _gdn_prep_fwd(proj, conv_w, a_log, dt_bias, ltri, lo, hi, tm):
    rows = proj.shape[0]

    def body(cur, prev8, a_in, b_in, w, al, dtb, lt, q_o, k_o, v_o, g_o, b_o):
        i = pl.program_id(0)
        mask = _row_mask(i, tm, lo, hi, (tm, LANES)) & (lax.broadcasted_iota(i32, (tm, LANES), 1) < GDN_HEADS)
        conv = _conv_taps(cur[...], prev8[...], w[...])
        q, k, v, g, beta = _gdn_act(conv, a_in[...], b_in[...], al[...], dtb[...], mask)
        q_o[...] = q
        k_o[...] = k
        v_o[...] = v
        gcum = jnp.dot(lt[...], g, preferred_element_type=f32, precision=lax.Precision.HIGHEST)
        g_o[...] = gcum
        b_o[...] = beta

    wide = jax.ShapeDtypeStruct((rows, GDN_WIDTH), f32)
    narrow = jax.ShapeDtypeStruct((rows, LANES), f32)
    o_spec = pl.BlockSpec((tm, GDN_WIDTH), lambda i: (i, 0))
    n_spec = pl.BlockSpec((tm, LANES), lambda i: (i, 0))
    return pl.pallas_call(
        body, name="gdn_prep_fwd", out_shape=[wide] * 3 + [narrow] * 2, grid=(rows // tm,),
        in_specs=_gdn_prep_specs(proj, tm) + [_full(conv_w), _full(a_log), _full(dt_bias), _full(ltri)],
        out_specs=[o_spec] * 3 + [n_spec] * 2, compiler_params=_params(("parallel",)),
    )(proj, proj, proj, proj, conv_w, a_log, dt_bias, ltri)


def _gdn_prep_bwd_act(proj, conv_w, a_log, dt_bias, ltri, dq, dk, dv, dgw, dbw, lo, hi, tm, swap=()):
    rows = proj.shape[0]
    c3 = 3 * GDN_WIDTH
    ns = len(swap)
    nt = rows // tm

    def body(*refs):
        cur, prev8, a_in, b_in, w, al, dtb, lt, dq_r, dk_r, dv_r, dg_r, db_r = refs[:13]
        dconv_o, da_o, dbin_o, dal_o, ddt_o = refs[13 + ns:18 + ns]
        i = pl.program_id(0)
        if ns:
            start, done = _swap_plan(refs[13:13 + ns], refs[18 + ns:18 + 2 * ns], *refs[18 + 2 * ns:])
            pl.when(i == 0)(start)
        mask = _row_mask(i, tm, lo, hi, (tm, LANES)) & (lax.broadcasted_iota(i32, (tm, LANES), 1) < GDN_HEADS)
        conv = _conv_taps(cur[...], prev8[...], w[...])
        dg = lax.dot_general(lt[...], dg_r[...], (((0,), (0,)), ((), ())), preferred_element_type=f32,
                             precision=lax.Precision.HIGHEST)
        dbeta = db_r[...]
        _, vjp = jax.vjp(lambda c, a, b, x, y: _gdn_act(c, a, b, x, y, mask), conv, a_in[...], b_in[...], al[...], dtb[...])
        dconv, da, dbin, dal, ddt = vjp((dq_r[...], dk_r[...], dv_r[...], dg, dbeta))
        dconv_o[...] = dconv
        da_o[...] = da
        dbin_o[...] = dbin

        @pl.when(i == 0)
        def _():
            dal_o[...] = jnp.zeros_like(dal_o)
            ddt_o[...] = jnp.zeros_like(ddt_o)

        dal_o[...] += dal
        ddt_o[...] += ddt
        if ns:
            pl.when(i == nt - 1)(done)

    w_spec = pl.BlockSpec((tm, GDN_WIDTH), lambda i: (i, 0))
    n_spec = pl.BlockSpec((tm, LANES), lambda i: (i, 0))
    s_spec = pl.BlockSpec((1, LANES), lambda i: (0, 0))
    res = pl.pallas_call(
        body, name="gdn_prep_bwd_act",
        out_shape=[jax.ShapeDtypeStruct((rows, c3), f32), jax.ShapeDtypeStruct((rows, LANES), f32),
                   jax.ShapeDtypeStruct((rows, LANES), f32), jax.ShapeDtypeStruct((1, LANES), f32),
                   jax.ShapeDtypeStruct((1, LANES), f32)] + [jax.ShapeDtypeStruct(t.shape, t.dtype) for t in swap],
        grid=(nt,),
        in_specs=_gdn_prep_specs(proj, tm) + [_full(conv_w), _full(a_log), _full(dt_bias), _full(ltri)] + [w_spec] * 3
        + [n_spec] * 2 + [_ANY] * ns,
        out_specs=[pl.BlockSpec((tm, c3), lambda i: (i, 0)), n_spec, n_spec, s_spec, s_spec] + [_ANY] * ns,
        scratch_shapes=[pltpu.SemaphoreType.DMA((ns,))] * 2 if ns else [],
        compiler_params=_params(("arbitrary",)),
    )(proj, proj, proj, proj, conv_w, a_log, dt_bias, ltri, dq, dk, dv, dgw, dbw, *swap)
    return res[:5], list(res[5:])


def _gdn_prep_bwd_conv(proj, conv_w, dconv, dgate, da, dbin, tm):
    rows, width = proj.shape
    c3 = 3 * GDN_WIDTH
    t8 = tm // 8
    nt = rows // tm

    def body(cur, w, dc, dnext8, dgt, da_r, db_r, dp_o, dw_o):
        i = pl.program_id(0)
        d = dc[...]
        x = cur[...]
        nxt = jnp.where(i == nt - 1, 0.0, dnext8[...])
        cat = jnp.concatenate([d, nxt], axis=0)
        wv = w[...]
        dx = d * wv[CONV_WIDTH - 1:CONV_WIDTH, :]
        parts = [jnp.sum(d * x, axis=0, keepdims=True)]
        for j in range(1, CONV_WIDTH):
            ahead = pltpu.roll(cat, tm + 8 - j, axis=0)[:tm, :]
            dx = dx + ahead * wv[CONV_WIDTH - 1 - j:CONV_WIDTH - j, :]
            parts.append(jnp.sum(ahead * x, axis=0, keepdims=True))
        dp_o[:, :c3] = dx.astype(bf16)
        dp_o[:, c3:4 * GDN_WIDTH] = dgt[...].astype(bf16)
        dp_o[:, 4 * GDN_WIDTH:4 * GDN_WIDTH + LANES] = da_r[...].astype(bf16)
        dp_o[:, 4 * GDN_WIDTH + LANES:] = db_r[...].astype(bf16)
        dwt = jnp.concatenate(parts[::-1], axis=0)

        @pl.when(i == 0)
        def _():
            dw_o[...] = jnp.zeros_like(dw_o)

        dw_o[...] += dwt

    n_spec = pl.BlockSpec((tm, LANES), lambda i: (i, 0))
    return pl.pallas_call(
        body, name="gdn_prep_bwd_conv",
        out_shape=[jax.ShapeDtypeStruct((rows, width), bf16), jax.ShapeDtypeStruct((CONV_WIDTH, c3), f32)],
        grid=(nt,),
        in_specs=[pl.BlockSpec((tm, c3), lambda i: (i, 0)),
                  _full(conv_w),
                  pl.BlockSpec((tm, c3), lambda i: (i, 0)),
                  pl.BlockSpec((8, c3), lambda i: (jnp.minimum((i + 1) * t8, rows // 8 - 1), 0)),
                  pl.BlockSpec((tm, GDN_WIDTH), lambda i: (i, 0)), n_spec, n_spec],
        out_specs=[pl.BlockSpec((tm, width), lambda i: (i, 0)), pl.BlockSpec((CONV_WIDTH, c3), lambda i: (0, 0))],
        compiler_params=_params(("arbitrary",)),
    )(proj, conv_w, dconv, dconv, dgate, da, dbin)


def _split(a):
    hi = a.astype(bf16)
    return hi, (a - hi.astype(f32)).astype(bf16)


def _make_mm(dot):
    @jax.custom_vjp
    def nn(a, b):
        return dot(a, b, "nn")

    nn.defvjp(lambda a, b: (dot(a, b, "nn"), (a, b)),
              lambda r, ct: (dot(ct, r[1], "nt"), dot(r[0], ct, "tn")))

    @jax.custom_vjp
    def nt(a, b):
        return dot(a, b, "nt")

    nt.defvjp(lambda a, b: (dot(a, b, "nt"), (a, b)),
              lambda r, ct: (dot(ct, r[1], "nn"), dot(ct, r[0], "tn")))

    @jax.custom_vjp
    def tn(a, b):
        return dot(a, b, "tn")

    tn.defvjp(lambda a, b: (dot(a, b, "tn"), (a, b)),
              lambda r, ct: (dot(r[1], ct, "nt"), dot(r[0], ct, "nn")))
    return nn, nt, tn


_mm, _mm_nt, _mm_tn = _make_mm(_bdot)


def _each(f, *lists):
    return [f(*xs) for xs in zip(*lists)]


def _gdn_chunk(q, k, v, gcb, bcb, s_in):
    c = q[0].shape[0]
    ri = lax.broadcasted_iota(i32, (c, c), 0)
    ci = lax.broadcasted_iota(i32, (c, c), 1)
    incl, strict = ri >= ci, ri > ci
    rowi = lax.broadcasted_iota(i32, gcb[0].shape, 0)
    qs = _each(lambda t: t * (GDN_DIM ** -0.5), q)
    decay = _each(lambda g: jnp.where(incl, jnp.exp(jnp.where(incl, g[:, :c] - g[:, :c].T, 0.0)), 0.0), gcb)
    kk = _each(lambda t: _mm_nt(t, t), k)
    a1 = _each(lambda b, d, t: jnp.where(strict, b[:, :c] * d * t, 0.0), bcb, decay, kk)
    eg = _each(jnp.exp, gcb)
    x = _each(lambda b, vv, e, t: jnp.concatenate([b * vv, (b * e) * t], axis=1), bcb, v, eg, k)
    pows = [a1]
    for _ in range(5):
        pows.append(_each(lambda p: _mm(p, p), pows[-1]))
    for ps in pows[:0:-1]:
        x = _each(lambda p, t: t + _mm(p, t), ps, x)
    x = _each(lambda p, t: t - _mm(p, t), a1, x)
    attn = _each(lambda a, b, d: _mm_nt(a, b) * d, qs, k, decay)
    glast = _each(lambda g: jnp.sum(jnp.where(rowi == c - 1, g, 0.0), axis=0, keepdims=True), gcb)
    u = _each(lambda t, s: t[:, :GDN_DIM] - _mm(t[:, GDN_DIM:], s), x, s_in)
    o = _each(lambda a, e, s, w, uu: _mm(a * e, s) + _mm(w, uu), qs, eg, s_in, attn, u)
    s_out = _each(lambda s, gl, t, g, uu: s * jnp.exp(gl) + _mm_tn(t * jnp.exp(gl - g), uu), s_in, glast, k, gcb, u)
    return o, s_out


def _gdn_heads(ref):
    return [ref[:, h * GDN_DIM:(h + 1) * GDN_DIM] for h in range(GDN_HEADS)]


def _gdn_fwd(q, k, v, gw, bw, shards):
    rows = q.shape[0]
    nc = rows // CHUNK
    num = len(shards)
    blk = pl.BlockSpec((CHUNK, GDN_WIDTH), lambda c: (c, 0))

    def body(*refs):
        q_r, k_r, v_r, g_r, b_r = refs[:5]
        ins = refs[5:5 + num]
        o_r, st_r = refs[5 + num:7 + num]
        outs = refs[7 + num:7 + 2 * num]
        s_sc, send_sems, recv_sems, local_sems = refs[7 + 2 * num:]
        c = pl.program_id(0)
        start, forward, finish = _gather_plan(ins, outs, send_sems, recv_sems, local_sems)

        @pl.when(c == 0)
        def _():
            s_sc[...] = jnp.zeros_like(s_sc)
            start()

        s_in = [s_sc[h] for h in range(GDN_HEADS)]
        st_r[0] = s_sc[...]
        o, s_out = _gdn_chunk(_gdn_heads(q_r), _gdn_heads(k_r), _gdn_heads(v_r), _widen(g_r[...]), _widen(b_r[...]), s_in)
        o_r[...] = jnp.concatenate(o, axis=1)
        for h in range(GDN_HEADS):
            s_sc[h] = s_out[h]
        pl.when(c == nc // 2)(forward)
        pl.when(c == nc - 1)(finish)

    res = pl.pallas_call(
        body, name="gdn_fwd",
        out_shape=[jax.ShapeDtypeStruct((rows, GDN_WIDTH), f32), jax.ShapeDtypeStruct((nc, GDN_HEADS, GDN_DIM, GDN_DIM), f32)]
        + [jax.ShapeDtypeStruct((N_CHIPS,) + t.shape, t.dtype) for t in shards],
        grid=(nc,), in_specs=[blk] * 3 + [pl.BlockSpec((CHUNK, LANES), lambda c: (c, 0))] * 2 + [_ANY] * num,
        out_specs=[blk, pl.BlockSpec((1, GDN_HEADS, GDN_DIM, GDN_DIM), lambda c: (c, 0, 0, 0))] + [_ANY] * num,
        scratch_shapes=[pltpu.VMEM((GDN_HEADS, GDN_DIM, GDN_DIM), f32), pltpu.SemaphoreType.DMA((6 * num,)),
                        pltpu.SemaphoreType.DMA((6 * num,)), pltpu.SemaphoreType.DMA((num,))],
        compiler_params=_params(("arbitrary",)),
    )(q, k, v, gw, bw, *shards)
    return res[0], res[1], res[2:]


def _gdn_bwd(q, k, v, gw, bw, states, do, parts):
    rows = q.shape[0]
    nc = rows // CHUNK
    num = len(parts)
    blk = pl.BlockSpec((CHUNK, GDN_WIDTH), lambda c: (nc - 1 - c, 0))

    def body(*refs):
        q_r, k_r, v_r, g_r, b_r, st_r, do_r = refs[:7]
        ins = refs[7:7 + num]
        dq_r, dk_r, dv_r, dg_r, db_r = refs[7 + num:12 + num]
        outs = refs[12 + num:12 + 2 * num]
        ds_sc, send_sems, recv_sems = refs[12 + 2 * num:]
        c = pl.program_id(0)
        start, finish = _scatter_plan(ins, outs, send_sems, recv_sems)

        @pl.when(c == 0)
        def _():
            ds_sc[...] = jnp.zeros_like(ds_sc)
            start()

        s_in = [st_r[0, h] for h in range(GDN_HEADS)]
        _, vjp = jax.vjp(_gdn_chunk, _gdn_heads(q_r), _gdn_heads(k_r), _gdn_heads(v_r), _widen(g_r[...]), _widen(b_r[...]), s_in)
        dq, dk, dv, dg, db, ds_in = vjp((_gdn_heads(do_r), [ds_sc[h] for h in range(GDN_HEADS)]))
        dq_r[...] = jnp.concatenate(dq, axis=1)
        dk_r[...] = jnp.concatenate(dk, axis=1)
        dv_r[...] = jnp.concatenate(dv, axis=1)
        dg_r[...] = _narrow(dg)
        db_r[...] = _narrow(db)
        for h in range(GDN_HEADS):
            ds_sc[h] = ds_in[h]
        pl.when(c == nc - 1)(finish)

    wide = jax.ShapeDtypeStruct((rows, GDN_WIDTH), f32)
    narrow = jax.ShapeDtypeStruct((rows, LANES), f32)
    nblk = pl.BlockSpec((CHUNK, LANES), lambda c: (nc - 1 - c, 0))
    res = pl.pallas_call(
        body, name="gdn_bwd",
        out_shape=[wide] * 3 + [narrow] * 2 + [jax.ShapeDtypeStruct((3,) + t.shape[1:], t.dtype) for t in parts],
        grid=(nc,),
        in_specs=[blk] * 3 + [nblk] * 2
        + [pl.BlockSpec((1, GDN_HEADS, GDN_DIM, GDN_DIM), lambda c: (nc - 1 - c, 0, 0, 0)), blk] + [_ANY] * num,
        out_specs=[blk] * 3 + [nblk] * 2 + [_ANY] * num,
        scratch_shapes=[pltpu.VMEM((GDN_HEADS, GDN_DIM, GDN_DIM), f32), pltpu.SemaphoreType.DMA((3 * num,)),
                        pltpu.SemaphoreType.DMA((3 * num,))],
        compiler_params=_params(("arbitrary",)),
    )(q, k, v, gw, bw, states, do, *parts)
    return res[:5], res[5:]


def _gdn_gate(o, gate, og):
    heads = [o[:, h * GDN_DIM:(h + 1) * GDN_DIM] for h in range(GDN_HEADS)]
    n = jnp.concatenate([oh * lax.rsqrt(jnp.mean(oh * oh, axis=-1, keepdims=True) + EPS) * og for oh in heads], axis=1)
    return n * jax.nn.silu(gate)


def _gdn_gate_fwd(o, proj, og, tm):
    rows = o.shape[0]
    return _rowwise(lambda i, ov, gv, w: (_gdn_gate(ov, gv, w),), "gdn_gate_fwd", rows, tm,
                    [(o, (GDN_WIDTH, 0)), (proj, (GDN_WIDTH, 3)), (og, None)], [(GDN_WIDTH, bf16)])[0]


def _gdn_gate_bwd(o, proj, og, dy, tm):
    rows = o.shape[0]

    def fn(i, ov, gv, w, d):
        _, vjp = jax.vjp(_gdn_gate, ov, gv, w)
        return vjp(d)

    return _rowwise(fn, "gdn_gate_bwd", rows, tm,
                    [(o, (GDN_WIDTH, 0)), (proj, (GDN_WIDTH, 3)), (og, None), (dy, (GDN_WIDTH, 0))],
                    [(GDN_WIDTH, f32), (GDN_WIDTH, f32)], [(1, GDN_DIM)])


def _sb_visible(i, j, valid):
    qpos = i * SB_BLOCK + lax.broadcasted_iota(i32, (SB_BLOCK, SB_BLOCK), 0)
    kpos = j * SB_BLOCK + lax.broadcasted_iota(i32, (SB_BLOCK, SB_BLOCK), 1)
    return (kpos < qpos) & (kpos >= FRONT) & valid


def _sb_logs(z, vis):
    l1p = jnp.log(1.0 + jnp.exp(-jnp.abs(z)))
    return -(jnp.maximum(-z, 0.0) + l1p), jnp.where(vis, -(jnp.maximum(z, 0.0) + l1p), 0.0)


def _tri_sum(x, tri):
    hi, lo = _split(x)
    return jnp.dot(hi, tri, preferred_element_type=f32) + jnp.dot(lo, tri, preferred_element_type=f32)


def _sb_live(t, i, runs):
    return (t <= i) & (jnp.max(functools.reduce(jnp.maximum, runs)) > -SB_UNDERFLOW)


def _sb_blocks(i, t, nb):
    js = [i - t - b for b in range(nb)]
    kss = [pl.ds(pl.multiple_of(jnp.maximum(j, 0) * SB_BLOCK, SB_BLOCK), SB_BLOCK) for j in js]
    return kss, [_sb_visible(i, j, j >= 0) for j in js]


def _sb_weights(i, t, nb, qs, sls, k_r, runs, after, scale):
    nh = len(qs)
    kss, vis = _sb_blocks(i, t, nb)
    units = [(a, b) for b in range(nb) for a in range(nh)]
    z = [_bdot(qs[a], k_r[kss[b], sls[a]], "nt") * scale for a, b in units]
    logs = [_sb_logs(zz, vis[b]) for zz, (a, b) in zip(z, units)]
    later = [_tri_sum(l[1], after) for l in logs]
    sums = [jnp.sum(l[1], axis=1, keepdims=True) for l in logs]
    w = []
    runs = list(runs)
    for b in range(nb):
        for a in range(nh):
            u = b * nh + a
            w.append(jnp.where(vis[b], jnp.exp(logs[u][0] + later[u] + runs[a]), 0.0))
        runs = [runs[a] + sums[b * nh + a] for a in range(nh)]
    return kss, vis, units, logs, w, tuple(runs)


def _sb_fwd(q, kv, width):
    rows = q.shape[0]
    nq = rows // SB_BLOCK
    lanes = SB_FWD_HEADS * SB_DIM
    npair = width // lanes
    scale = SB_DIM ** -0.5

    def body(q_r, k_r, v_r, o_r):
        i = pl.program_id(1)
        rj = lax.broadcasted_iota(i32, (SB_BLOCK, SB_BLOCK), 0)
        cs = lax.broadcasted_iota(i32, (SB_BLOCK, SB_BLOCK), 1)
        after = (rj > cs).astype(bf16)
        sls = [slice(a * SB_DIM, (a + 1) * SB_DIM) for a in range(SB_FWD_HEADS)]
        qs = [q_r[:, sl] for sl in sls]

        def step(carry, nb):
            t, accs, runs = carry
            kss, _, units, _, w, runs = _sb_weights(i, t, nb, qs, sls, k_r, runs, after, scale)
            prods = [_bdot(ww, v_r[kss[b], sls[a]], "nn") for ww, (a, b) in zip(w, units)]
            accs = tuple(functools.reduce(jnp.add, [accs[a]] + prods[a::SB_FWD_HEADS]) for a in range(SB_FWD_HEADS))
            return t + nb, accs, runs

        init = (jnp.int32(0), tuple(jnp.zeros((SB_BLOCK, SB_DIM), f32) for _ in sls),
                tuple(jnp.zeros((SB_BLOCK, 1), f32) for _ in sls))
        _, accs, _ = lax.while_loop(lambda c: _sb_live(c[0], i, c[2]), lambda c: step(c, 2), step(init, SB_FIRST))
        o_r[...] = jnp.concatenate(accs, axis=1)

    return pl.pallas_call(
        body, name="sb_fwd", out_shape=jax.ShapeDtypeStruct((rows, width), f32), grid=(npair, nq),
        in_specs=[pl.BlockSpec((SB_BLOCK, lanes), lambda p, i: (i, p)),
                  pl.BlockSpec((rows, lanes), lambda p, i: (0, p)),
                  pl.BlockSpec((rows, lanes), lambda p, i: (0, npair + p))],
        out_specs=pl.BlockSpec((SB_BLOCK, lanes), lambda p, i: (i, p)),
        compiler_params=_params(("parallel", "arbitrary")),
    )(q, kv, kv)


def _sb_bwd(q, kv, do, width):
    rows = q.shape[0]
    nq = rows // SB_BLOCK
    npair = width // LANES
    nh = LANES // SB_DIM
    scale = SB_DIM ** -0.5

    def body(q_r, k_r, v_r, do_r, dq_r, dk_r, dv_r, e_sc, sig_sc, w_sc):
        i = pl.program_id(1)

        @pl.when(i == 0)
        def _():
            dk_r[...] = jnp.zeros_like(dk_r)
            dv_r[...] = jnp.zeros_like(dv_r)

        rj = lax.broadcasted_iota(i32, (SB_BLOCK, SB_BLOCK), 0)
        cs = lax.broadcasted_iota(i32, (SB_BLOCK, SB_BLOCK), 1)
        after = (rj > cs).astype(bf16)
        from_s = (rj >= cs).astype(bf16)
        zero1 = jnp.zeros((SB_BLOCK, 1), f32)
        sls = [slice(a * SB_DIM, (a + 1) * SB_DIM) for a in range(nh)]
        qs = [q_r[:, sl] for sl in sls]
        dos = [do_r[:, sl] for sl in sls]

        def weigh(carry, nb):
            t, runs, eruns = carry
            kss, _, units, logs, w, runs = _sb_weights(i, t, nb, qs, sls, k_r, runs, after, scale)
            dw = [_bdot(dos[a], v_r[kss[b], sls[a]], "nt") for a, b in units]
            e = [ww * d for ww, d in zip(w, dw)]
            for u, (a, b) in enumerate(units):
                e_sc[a, t + b] = e[u]
                sig_sc[a, t + b] = jnp.exp(logs[u][0])
                w_sc[a, t + b] = w[u].astype(w_sc.dtype)
            sums = [jnp.sum(ee, axis=1, keepdims=True) for ee in e]
            eruns = tuple(functools.reduce(jnp.add, [eruns[a]] + sums[a::nh]) for a in range(nh))
            return t + nb, runs, eruns

        n_blk, _, etots = lax.while_loop(lambda c: _sb_live(c[0], i, c[1]), lambda c: weigh(c, 2),
                                         weigh((jnp.int32(0), (zero1,) * nh, (zero1,) * nh), SB_FIRST))

        def push(t, carry, nb):
            dqs, eruns = carry
            kss, vis = _sb_blocks(i, t, nb)
            units = [(a, b) for b in range(nb) for a in range(nh)]
            e = [e_sc[a, t + b] for a, b in units]
            dvs = [_bdot(w_sc[a, t + b], dos[a], "tn") for a, b in units]
            upto = [_tri_sum(ee, from_s) for ee in e]
            sums = [jnp.sum(ee, axis=1, keepdims=True) for ee in e]
            dz = []
            eruns = list(eruns)
            for b in range(nb):
                for a in range(nh):
                    u = b * nh + a
                    sig = sig_sc[a, t + b]
                    before = etots[a] - eruns[a] - upto[u]
                    dz.append(jnp.where(vis[b], e[u] * (1.0 - sig) - before * sig, 0.0) * scale)
                eruns = [eruns[a] + sums[b * nh + a] for a in range(nh)]
            dks = [_bdot(d, qs[a], "tn") for d, (a, b) in zip(dz, units)]
            dqp = [_bdot(d, k_r[kss[b], sls[a]], "nn") for d, (a, b) in zip(dz, units)]
            for b in range(nb):
                dk_r[kss[b], :] += jnp.concatenate(dks[b * nh:(b + 1) * nh], axis=1)
                dv_r[kss[b], :] += jnp.concatenate(dvs[b * nh:(b + 1) * nh], axis=1)
            dqs = tuple(functools.reduce(jnp.add, [dqs[a]] + dqp[a::nh]) for a in range(nh))
            return dqs, tuple(eruns)

        first = push(jnp.int32(0), (tuple(jnp.zeros((SB_BLOCK, SB_DIM), f32) for _ in sls), (zero1,) * nh), SB_FIRST)
        dqs, _ = lax.fori_loop(0, (n_blk - SB_FIRST) // 2, lambda p, c: push(SB_FIRST + 2 * p, c, 2), first)
        dq_r[...] = jnp.concatenate(dqs, axis=1)

    blk = pl.BlockSpec((SB_BLOCK, LANES), lambda p, i: (i, p))
    col = pl.BlockSpec((rows, LANES), lambda p, i: (0, p))
    wide = jax.ShapeDtypeStruct((rows, width), f32)
    depth = nq + SB_FIRST
    return pl.pallas_call(
        body, name="sb_bwd", out_shape=[wide] * 3, grid=(npair, nq),
        in_specs=[blk, col, pl.BlockSpec((rows, LANES), lambda p, i: (0, npair + p)), blk],
        out_specs=[blk, col, col],
        scratch_shapes=[pltpu.VMEM((nh, depth, SB_BLOCK, SB_BLOCK), f32), pltpu.VMEM((nh, depth, SB_BLOCK, SB_BLOCK), f32),
                        pltpu.VMEM((nh, depth, SB_BLOCK, SB_BLOCK), bf16)],
        compiler_params=_params(("parallel", "arbitrary")),
    )(q, kv, kv, do)


_FLIPS = ((1, 0), (0, 1), (1, 1))
_ANY = pl.BlockSpec(memory_space=pl.ANY)


def _flip(v, a):
    return v + a - 2 * a * v


def _gather_plan(ins, outs, send_sems, recv_sems, local_sems):
    num = len(ins)
    x, y, c = lax.axis_index("x"), lax.axis_index("y"), lax.axis_index("c")
    me, sibling = (x, y, c), (x, y, 1 - c)
    chip = 2 * x + y
    others = [(_flip(x, a), _flip(y, b)) for a, b in _FLIPS]
    pairs = [(k, n, 2 * ox + oy) for k in range(num) for n, (ox, oy) in enumerate(others)]

    def half_of(ref, hc):
        half = ref.shape[0] // 2
        start = hc * half
        for align in (16, 8):
            if half % align == 0:
                start = pl.multiple_of(start, align)
                break
        return ref.at[pl.ds(start, half)]

    def copy(k, n, s, hc, to, src=None):
        dst = half_of(outs[k].at[s], hc)
        return pltpu.make_async_remote_copy(
            src_ref=dst if src is None else src, dst_ref=dst,
            send_sem=send_sems.at[6 * k + n], recv_sem=recv_sems.at[6 * k + n], device_id=to, device_id_type=MESH)

    mine = [pltpu.make_async_copy(ins[k], outs[k].at[chip], local_sems.at[k]) for k in range(num)]
    first = [copy(k, n, chip, c, (others[n][0], others[n][1], c), src=half_of(ins[k], c)) for k, n, _ in pairs]
    passed = [copy(k, 3 + n, s, c, sibling) for k, n, s in pairs]

    def start():
        for cp in mine + first:
            cp.start()

    def forward():
        for (k, n, s), fw in zip(pairs, passed):
            copy(k, n, s, c, me).wait_recv()
            fw.start()

    def finish():
        for k, n, s in pairs:
            copy(k, 3 + n, s, 1 - c, me).wait_recv()
        for cp in first + passed:
            cp.wait_send()
        for cp in mine:
            cp.wait()

    return start, forward, finish


def _gather_chips(shards):
    num = len(shards)

    def body(*refs):
        for phase in _gather_plan(refs[:num], refs[num:2 * num], *refs[2 * num:]):
            phase()

    return pl.pallas_call(
        body, name="gather_chips", out_shape=[jax.ShapeDtypeStruct((N_CHIPS,) + t.shape, t.dtype) for t in shards],
        in_specs=[_ANY] * num, out_specs=[_ANY] * num,
        scratch_shapes=[pltpu.SemaphoreType.DMA((6 * num,)), pltpu.SemaphoreType.DMA((6 * num,)),
                        pltpu.SemaphoreType.DMA((num,))],
    )(*shards)


def _scatter_plan(ins, outs, send_sems, recv_sems):
    x, y, c = lax.axis_index("x"), lax.axis_index("y"), lax.axis_index("c")
    cps = []
    for k in range(len(ins)):
        for n, (a, b) in enumerate(_FLIPS):
            ox, oy = _flip(x, a), _flip(y, b)
            cps.append(pltpu.make_async_remote_copy(
                src_ref=ins[k].at[2 * ox + oy], dst_ref=outs[k].at[n], send_sem=send_sems.at[3 * k + n],
                recv_sem=recv_sems.at[3 * k + n], device_id=(ox, oy, c), device_id_type=MESH))

    def start():
        for cp in cps:
            cp.start()

    def finish():
        for cp in cps:
            cp.wait()

    return start, finish


def _swap_plan(ins, outs, send_sems, recv_sems):
    x, y, c = lax.axis_index("x"), lax.axis_index("y"), lax.axis_index("c")
    cps = [pltpu.make_async_remote_copy(src_ref=ins[k], dst_ref=outs[k], send_sem=send_sems.at[k],
                                        recv_sem=recv_sems.at[k], device_id=(x, y, 1 - c), device_id_type=MESH)
           for k in range(len(ins))]

    def start():
        for cp in cps:
            cp.start()

    def finish():
        for cp in cps:
            cp.wait()

    return start, finish


def _swap_sibling(arrs):
    num = len(arrs)

    def body(*refs):
        for phase in _swap_plan(refs[:num], refs[num:2 * num], *refs[2 * num:]):
            phase()

    return pl.pallas_call(
        body, name="swap_sibling", out_shape=[jax.ShapeDtypeStruct(t.shape, t.dtype) for t in arrs],
        in_specs=[_ANY] * num, out_specs=[_ANY] * num,
        scratch_shapes=[pltpu.SemaphoreType.DMA((num,)), pltpu.SemaphoreType.DMA((num,))],
    )(*arrs)


def _gather_all(v):
    m_per, n = v.shape

    def body(x_ref, out_ref, send_sems, recv_sems, local_sem):
        x, y, c = lax.axis_index("x"), lax.axis_index("y"), lax.axis_index("c")
        me, sibling = (x, y, c), (x, y, 1 - c)
        chips = [(_flip(x, a), _flip(y, b)) for a, b in _FLIPS]

        def rows(px, py, pc):
            return out_ref.at[pl.ds(pl.multiple_of((4 * px + 2 * py + pc) * m_per, 8), m_per), :]

        def copy(k, block, to, src=None):
            return pltpu.make_async_remote_copy(
                src_ref=rows(*block) if src is None else src, dst_ref=rows(*block),
                send_sem=send_sems.at[k], recv_sem=recv_sems.at[k], device_id=to, device_id_type=MESH)

        mine = pltpu.make_async_copy(x_ref, rows(*me), local_sem)
        mine.start()
        first = [copy(0, me, sibling, src=x_ref)]
        first += [copy(1 + j, me, (*chip, c), src=x_ref) for j, chip in enumerate(chips)]
        for cp in first:
            cp.start()
        passed = [copy(4 + j, (*chip, c), sibling) for j, chip in enumerate(chips)]
        for j, chip in enumerate(chips):
            copy(1 + j, (*chip, c), me).wait_recv()
            passed[j].start()
        copy(0, sibling, me).wait_recv()
        for j, chip in enumerate(chips):
            copy(4 + j, (*chip, 1 - c), me).wait_recv()
        for cp in first + passed:
            cp.wait_send()
        mine.wait()

    return pl.pallas_call(
        body, name="gather_all", out_shape=jax.ShapeDtypeStruct((N_DEV * m_per, n), v.dtype),
        in_specs=[pl.BlockSpec(memory_space=pltpu.VMEM)], out_specs=pl.BlockSpec(memory_space=pltpu.VMEM),
        scratch_shapes=[pltpu.SemaphoreType.DMA((7,)), pltpu.SemaphoreType.DMA((7,)), pltpu.SemaphoreType.DMA],
    )(v)


def _sum_chips(parts, got, chip, name):
    cols = parts.shape[-1]
    rows = parts.size // (N_CHIPS * cols)
    tm = _pick(rows, (256, 128, 64, 32, 16))

    def body(chip_r, own_r, got_r, o_r):
        acc = own_r[0]
        for n in range(3):
            acc = acc + got_r[n].astype(f32)
        o_r[...] = acc

    return pl.pallas_call(
        body, name=name, out_shape=jax.ShapeDtypeStruct((rows, cols), f32),
        grid_spec=pltpu.PrefetchScalarGridSpec(
            num_scalar_prefetch=1, grid=(rows // tm,),
            in_specs=[pl.BlockSpec((1, tm, cols), lambda i, s: (s[0], i, 0)),
                      pl.BlockSpec((3, tm, cols), lambda i, s: (0, i, 0))],
            out_specs=pl.BlockSpec((tm, cols), lambda i, s: (i, 0))),
        compiler_params=_params(("parallel",)),
    )(chip, parts.reshape(N_CHIPS, rows, cols), got.reshape(3, rows, cols))


def _sum_devices(g, m_per):
    n = g.shape[1]

    def body(g_r, o_r):
        acc = g_r[0:m_per, :]
        for d in range(1, N_DEV):
            acc = acc + g_r[d * m_per:(d + 1) * m_per, :]
        o_r[...] = acc

    return pl.pallas_call(body, name="sum_devices", out_shape=jax.ShapeDtypeStruct((m_per, n), f32))(g)


def _adamw(w, gs, m, v, name):
    shape = w.shape
    cols = shape[-1]
    rows = w.size // cols
    tm = _pick(rows, (256, 128, 64, 32, 16, 8)) if rows * cols * 4 > (1 << 20) else rows

    def fn(i, wv, mv, vv, *gv):
        g = functools.reduce(jnp.add, gv)
        mn = ADAM_B1 * mv + (1.0 - ADAM_B1) * g
        vn = ADAM_B2 * vv + (1.0 - ADAM_B2) * jnp.square(g)
        m_hat = mn / (1.0 - ADAM_B1 ** ADAM_STEP)
        v_hat = vn / (1.0 - ADAM_B2 ** ADAM_STEP)
        delta = -ADAM_LR * (m_hat / (jnp.sqrt(v_hat) + ADAM_EPS) + ADAM_WD * wv)
        return g, delta, mn, vn

    outs = _rowwise(fn, name, rows, tm, [(t.reshape(rows, cols), (cols, 0)) for t in (w, m, v) + tuple(gs)], [(cols, f32)] * 4)
    return tuple(o.reshape(shape) for o in outs)


def _pack(pieces, rows, dtype):
    flat = jnp.concatenate([p.reshape(-1).astype(dtype) for p in pieces])
    return jnp.pad(flat, (0, rows * PACK_COLS - flat.size)).reshape(rows, PACK_COLS)


def _unpack(buf, shapes):
    lead = buf.shape[:-2]
    flat = buf.reshape(lead + (-1,))
    out, off = [], 0
    for s in shapes:
        n = 1
        for d in s:
            n *= d
        out.append(flat[..., off:off + n].reshape(lead + tuple(s)))
        off += n
    return out


def _join_cols(t):
    return jnp.moveaxis(t, 0, -2).reshape(t.shape[1:-1] + (N_CHIPS * t.shape[-1],))


def _join_rows(t):
    return t.reshape((N_CHIPS * t.shape[1],) + t.shape[2:])


def _split_cols(t, parts=N_CHIPS):
    r, cols = t.shape
    return jnp.moveaxis(t.reshape(r, parts, cols // parts), 1, 0)


def _split_rows(t):
    return t.reshape((N_CHIPS, t.shape[0] // N_CHIPS) + t.shape[1:])


def kernel(x, meta_tokens, gdn_norm_g, gdn_w_in, gdn_conv_w, gdn_a_log, gdn_dt_bias, gdn_onorm_g, gdn_w_out, kv_norm_g, w_kv, sb_norm_g, sb_w_q, sb_w_o, ffn_norm_g, ffn_w_gate_up, ffn_w_down, final_norm_g, loss_target, m_meta_tokens, m_gdn_norm_g, m_gdn_w_in, m_gdn_conv_w, m_gdn_a_log, m_gdn_dt_bias, m_gdn_onorm_g, m_gdn_w_out, m_kv_norm_g, m_w_kv, m_sb_norm_g, m_sb_w_q, m_sb_w_o, m_ffn_norm_g, m_ffn_w_gate_up, m_ffn_w_down, m_final_norm_g, v_meta_tokens, v_gdn_norm_g, v_gdn_w_in, v_gdn_conv_w, v_gdn_a_log, v_gdn_dt_bias, v_gdn_onorm_g, v_gdn_w_out, v_kv_norm_g, v_w_kv, v_sb_norm_g, v_sb_w_q, v_sb_w_o, v_ffn_norm_g, v_ffn_w_gate_up, v_ffn_w_down, v_final_norm_g):
    weights = dict(meta_tokens=meta_tokens, gdn_norm_g=gdn_norm_g, gdn_w_in=gdn_w_in, gdn_conv_w=gdn_conv_w,
                   gdn_a_log=gdn_a_log, gdn_dt_bias=gdn_dt_bias, gdn_onorm_g=gdn_onorm_g, gdn_w_out=gdn_w_out,
                   kv_norm_g=kv_norm_g, w_kv=w_kv, sb_norm_g=sb_norm_g, sb_w_q=sb_w_q, sb_w_o=sb_w_o,
                   ffn_norm_g=ffn_norm_g, ffn_w_gate_up=ffn_w_gate_up, ffn_w_down=ffn_w_down, final_norm_g=final_norm_g)
    m_in = dict(meta_tokens=m_meta_tokens, gdn_norm_g=m_gdn_norm_g, gdn_w_in=m_gdn_w_in, gdn_conv_w=m_gdn_conv_w,
                gdn_a_log=m_gdn_a_log, gdn_dt_bias=m_gdn_dt_bias, gdn_onorm_g=m_gdn_onorm_g, gdn_w_out=m_gdn_w_out,
                kv_norm_g=m_kv_norm_g, w_kv=m_w_kv, sb_norm_g=m_sb_norm_g, sb_w_q=m_sb_w_q, sb_w_o=m_sb_w_o,
                ffn_norm_g=m_ffn_norm_g, ffn_w_gate_up=m_ffn_w_gate_up, ffn_w_down=m_ffn_w_down, final_norm_g=m_final_norm_g)
    v_in = dict(meta_tokens=v_meta_tokens, gdn_norm_g=v_gdn_norm_g, gdn_w_in=v_gdn_w_in, gdn_conv_w=v_gdn_conv_w,
                gdn_a_log=v_gdn_a_log, gdn_dt_bias=v_gdn_dt_bias, gdn_onorm_g=v_gdn_onorm_g, gdn_w_out=v_gdn_w_out,
                kv_norm_g=v_kv_norm_g, w_kv=v_w_kv, sb_norm_g=v_sb_norm_g, sb_w_q=v_sb_w_q, sb_w_o=v_sb_w_o,
                ffn_norm_g=v_ffn_norm_g, ffn_w_gate_up=v_ffn_w_gate_up, ffn_w_down=v_ffn_w_down, final_norm_g=v_final_norm_g)
    names = list(weights)

    seq, d = x.shape[1], x.shape[2]
    lo_frames = FRONT + N_META
    used = lo_frames + seq
    rows = -(-used // SB_BLOCK) * SB_BLOCK
    tm = _pick(rows, (640, 512, 384, 256, 128))
    tp = _pick(rows, (320, 256, 128))
    n_ffn = ffn_w_gate_up.shape[0]
    sb_width = sb_w_q.shape[2]
    chip =2 * lax.axis_index("x") + lax.axis_index("y")

    big = [gdn_w_in[0], gdn_w_out[0], w_kv, sb_w_q[0], sb_w_o[0], ffn_w_gate_up, ffn_w_down]
    small = [meta_tokens, gdn_norm_g, gdn_conv_w[0]]
    n_early = 1
    big_bf16 = [t.astype(bf16) for t in big]
    w_in_s, small_g = _gather_chips(big_bf16[:n_early] + [_pack(small, 16, f32)])
    small_s = _unpack(small_g, [t.shape for t in small])
    w_in = _join_cols(w_in_s)
    pad_ab = jnp.zeros((d, LANES - GDN_HEADS), bf16)
    w_in_ext = jnp.concatenate([w_in[:, :4 * GDN_WIDTH], w_in[:, 4 * GDN_WIDTH:4 * GDN_WIDTH + GDN_HEADS], pad_ab,
                                w_in[:, 4 * GDN_WIDTH + GDN_HEADS:], pad_ab], axis=1)
    meta_full, gdn_g_full, conv_full = (_join_cols(t) for t in small_s)

    zeros = lambda n: jnp.zeros((n, d), f32)
    h0 = jnp.concatenate([zeros(FRONT), meta_full, x[0], zeros(rows - used)], axis=0)
    tgt = jnp.concatenate([zeros(lo_frames), loss_target[0], zeros(rows - used)], axis=0)
    pad8 = lambda t: jnp.pad(t, ((0, 0), (0, LANES - t.shape[1])))
    a_log8, dt_bias8 = pad8(gdn_a_log), pad8(gdn_dt_bias)
    r_i = jnp.arange(tp)
    ltri = ((r_i[:, None] >= r_i[None, :]) & (r_i[:, None] // CHUNK == r_i[None, :] // CHUNK)).astype(f32)
    ffn_g = [ffn_norm_g[l:l + 1] for l in range(n_ffn)]
    kv_g, fin_g = kv_norm_g.reshape(1, d), final_norm_g.reshape(1, d)

    n0 = _rms_fwd(h0, gdn_g_full, "gdn_norm")
    proj = _matmul(n0, w_in_ext, "nn", "gdn_proj")
    gq, gk, gv, gw, bw = _gdn_prep_fwd(proj, conv_full, a_log8, dt_bias8, ltri, FRONT, used, tp)
    g_o, g_states, (w_out_s, w_kv_s, w_q_s, w_o_s, w_gu_s, w_dn_s) = _gdn_fwd(gq, gk, gv, gw, bw, big_bf16[n_early:])
    w_out = _join_rows(w_out_s)
    w_kvf = _join_cols(w_kv_s)
    w_q = _join_rows(w_q_s)
    w_o = _join_rows(w_o_s)
    w_gu = [_join_cols(w_gu_s[:, l]) for l in range(n_ffn)]
    w_dn = [_join_rows(w_dn_s[:, l]) for l in range(n_ffn)]
    og = _gdn_gate_fwd(g_o, proj, gdn_onorm_g, tm)
    h1, n_f0 = _matmul(og, w_out, "nn", "gdn_out", res=h0, norms=[ffn_g[0]])

    def ffn_fwd(h, n, l, norms):
        gate, up, act = _ffn_up(n, w_gu[l], f"ffn{l}_gate_up")
        return _matmul(act, w_dn[l], "nn", f"ffn{l}_down", res=h, norms=norms), (n, gate, up, act)

    (h2, n_kv, n_sb), ffn0_saved = ffn_fwd(h1, n_f0, 0, [kv_g, sb_norm_g])
    kv = _matmul(n_kv, w_kvf, "nn", "kv_proj", out_dtype=bf16)
    sq = _matmul(n_sb, w_q, "nn", "q_proj", out_dtype=bf16)
    s_o = _sb_fwd(sq, kv, sb_width)
    h3, n_f1 = _matmul(s_o, w_o, "nn", "sb_out", res=h2, norms=[ffn_g[1]])
    h4, ffn1_saved = ffn_fwd(h3, n_f1, 1, [])
    dh4, d_fin_g, loss_part = _loss_head(h4, fin_g, tgt, lo_frames, used, "loss_head")

    def ffn_bwd(dh, h, l, saved):
        n, gate, up, act = saved
        d_wdn = _matmul(act, dh, "tn", f"ffn{l}_d_w_down")
        d_gate, d_up = _ffn_dact(dh, w_dn[l], gate, up, f"ffn{l}_d_gate_up")
        d_wgu = jnp.concatenate([_split_cols(_matmul(n, d_gate, "tn", f"ffn{l}_d_w_gate"), N_CHIPS // 2),
                                 _split_cols(_matmul(n, d_up, "tn", f"ffn{l}_d_w_up"), N_CHIPS // 2)], axis=0)
        dh_in, dg = _norm_bwd([d_gate, d_up], w_gu[l], h, ffn_g[l], dh, f"ffn{l}_d_norm")
        return dh_in, d_wgu, d_wdn, dg

    dh3, d_wgu1, d_wdn1, d_ffn_g1 = ffn_bwd(dh4, h3, 1, ffn1_saved)
    d_wo = _matmul(s_o, dh3, "tn", "d_w_o")
    d_so = _matmul(dh3, w_o, "nt", "d_sb_o")
    d_sq, d_sk, d_sv = _sb_bwd(sq, kv, d_so, sb_width)
    d_wq = _matmul(n_sb, d_sq, "tn", "d_w_q")
    dh2, d_sb_g = _norm_bwd([d_sq], w_q, h2, sb_norm_g, dh3, "d_sb_norm")
    d_wkv = jnp.concatenate([_matmul(n_kv, d_sk, "tn", "d_w_k"), _matmul(n_kv, d_sv, "tn", "d_w_v")], axis=1)
    dh2, d_kv_g = _norm_bwd([d_sk, d_sv], w_kvf, h2, kv_g, dh2, "d_kv_norm")
    dh1, d_wgu0, d_wdn0, d_ffn_g0 = ffn_bwd(dh2, h1, 0, ffn0_saved)
    d_wout = _matmul(og, dh1, "tn", "d_w_out")
    d_og = _matmul(dh1, w_out, "nt", "d_gdn_gated")
    d_go, d_gate, d_onorm = _gdn_gate_bwd(g_o, proj, gdn_onorm_g, d_og, tm)
    by_chip = [None, _split_rows(d_wout), _split_cols(d_wkv), _split_rows(d_wq), _split_rows(d_wo),
               jnp.stack([d_wgu0, d_wgu1], axis=1),
               jnp.stack([_split_rows(d_wdn0), _split_rows(d_wdn1)], axis=1)]
    (d_gq, d_gk, d_gv, d_gw, d_bw), got_early = _gdn_bwd(gq, gk, gv, gw, bw, g_states, d_go,
                                                         [t.astype(bf16) for t in by_chip[1:]])
    chip_arr = jnp.reshape(chip, (1,)).astype(i32)
    over_early = [_sum_chips(t, g, chip_arr, f"sum_chips_{k + 1}")
                  for k, (t, g) in enumerate(zip(by_chip[1:], got_early))]
    (dconv, d_a_in, d_b_in, d_a_log8, d_dt_bias8), sibling_early = _gdn_prep_bwd_act(
        proj, conv_full, a_log8, dt_bias8, ltri, d_gq, d_gk, d_gv, d_gw, d_bw, FRONT, used, tp, swap=over_early)
    dproj, d_conv = _gdn_prep_bwd_conv(proj, conv_full, dconv, d_gate, d_a_in, d_b_in, tp)
    d_win_ext = _matmul(n0, dproj, "tn", "d_w_in")
    d_win = jnp.concatenate([d_win_ext[:, :4 * GDN_WIDTH], d_win_ext[:, 4 * GDN_WIDTH:4 * GDN_WIDTH + GDN_HEADS],
                             d_win_ext[:, 4 * GDN_WIDTH + LANES:4 * GDN_WIDTH + LANES + GDN_HEADS]], axis=1)
    by_chip[0] = _split_cols(d_win)
    dh0, d_gdn_g, got_late = _norm_bwd([dproj], w_in_ext, h0, gdn_g_full, dh1, "d_gdn_norm",
                                       send=[by_chip[0].astype(bf16)])
    grad_x = dh0[lo_frames:used][None]

    over_late = [_sum_chips(by_chip[0], got_late[0], chip_arr, "sum_chips_0")]
    over_chips = over_late + over_early
    over_sibling = list(_swap_sibling(over_late)) + sibling_early
    big_names = ["gdn_w_in", "gdn_w_out", "w_kv", "sb_w_q", "sb_w_o", "ffn_w_gate_up", "ffn_w_down"]
    g_big = dict(zip(big_names, zip(over_chips, over_sibling)))

    small_parts = [dh0[FRONT:lo_frames], d_gdn_g, d_conv, d_a_log8, d_dt_bias8, d_onorm, d_kv_g, d_sb_g,
                   d_ffn_g0, d_ffn_g1, d_fin_g, loss_part]
    s_rows = -(-sum(t.size for t in small_parts) // (8 * PACK_COLS)) * 8
    s_sum = _sum_devices(_gather_all(_pack(small_parts, s_rows, f32)), s_rows)
    (g_meta, g_gdn_g, g_conv, g_a_log8, g_dt8, g_onorm, g_kv_g, g_sb_g, g_ffn_g0, g_ffn_g1, g_fin_g,
     loss_v) = _unpack(s_sum, [t.shape for t in small_parts])
    col_shard = lambda t, w: lax.dynamic_slice_in_dim(t, chip * w, w, axis=t.ndim - 1)

    g_small = dict(
        meta_tokens=col_shard(g_meta, meta_tokens.shape[1]), gdn_norm_g=col_shard(g_gdn_g, gdn_norm_g.shape[1]),
        gdn_conv_w=col_shard(g_conv, gdn_conv_w.shape[2])[None],
        gdn_a_log=g_a_log8[:, :GDN_HEADS], gdn_dt_bias=g_dt8[:, :GDN_HEADS], gdn_onorm_g=g_onorm,
        kv_norm_g=g_kv_g.reshape(-1), sb_norm_g=g_sb_g, ffn_norm_g=jnp.concatenate([g_ffn_g0, g_ffn_g1], axis=0),
        final_norm_g=g_fin_g.reshape(-1))

    grads, delta, new_m, new_v = {}, {}, {}, {}
    for n in names:
        gs = g_big[n] if n in g_big else (g_small[n],)
        grads[n], delta[n], new_m[n], new_v[n] = _adamw(weights[n], gs, m_in[n], v_in[n], f"adamw_{n}")
    loss = loss_v[0, 0]
    return (loss, grad_x, *[grads[n] for n in names], *[delta[n] for n in names],
            *[new_m[n] for n in names], *[new_v[n] for n in names])
```
